```python
import jax, jax.numpy as jnp
from jax import lax
import numpy as np

D_MODEL = 1024
BATCH = 8
SEQ = 2048
DEPTH = 1

ATTN_HEADS = 8
HEAD_DIM = 64
D_ATTN = ATTN_HEADS * HEAD_DIM
MOBA_BLOCK = 256
MOBA_TOPK = 3
Q_CHUNK = 32
ROPE_THETA = 10000.0
POOL_WINDOWS = (2, 4, 8, 16)
N_POOL_GROUPS = len(POOL_WINDOWS)
POOL_GROUP_DIM = 128
D_POOL = N_POOL_GROUPS * POOL_GROUP_DIM
N_BRANCHES = 2
D_IN_PROJ = 3 * D_ATTN + D_POOL + N_BRANCHES * D_MODEL
D_FF = 2816
CONV_WIDTH = 3
LN_EPS = 1e-5
DEEPNORM_ALPHA = (2.0 * DEPTH) ** 0.25
DEEPNORM_BETA = (8.0 * DEPTH) ** -0.25
NEG = -1e30

kernel_name = "hybrid_moba_pool_convffn_deepnorm"


def layer_norm(x, g, b):
    xf = x.astype(jnp.float32)
    mu = jnp.mean(xf, axis=-1, keepdims=True)
    var = jnp.mean(jnp.square(xf - mu), axis=-1, keepdims=True)
    y = (xf - mu) * lax.rsqrt(var + LN_EPS) * g.astype(jnp.float32) + b.astype(jnp.float32)
    return y.astype(x.dtype)


def rope_tables(s):
    half = HEAD_DIM // 2
    inv_freq = 1.0 / (ROPE_THETA ** (jnp.arange(half, dtype=jnp.float32) / half))
    ang = jnp.arange(s, dtype=jnp.float32)[:, None] * inv_freq[None, :]
    return jnp.cos(ang), jnp.sin(ang)


def apply_rope(t, cos, sin):
    tf = t.astype(jnp.float32)
    half = HEAD_DIM // 2
    t1, t2 = tf[..., :half], tf[..., half:]
    return jnp.concatenate([t1 * cos - t2 * sin, t2 * cos + t1 * sin], axis=-1).astype(t.dtype)


def moba_attention(q, k, v):
    b, h, s, d = q.shape
    nb = -(-s // MOBA_BLOCK)
    s_pad = nb * MOBA_BLOCK
    pad = ((0, 0), (0, 0), (0, s_pad - s), (0, 0))
    q, k, v = jnp.pad(q, pad), jnp.pad(k, pad), jnp.pad(v, pad)
    k_blocks = k.reshape(b, h, nb, MOBA_BLOCK, d)
    v_blocks = v.reshape(b, h, nb, MOBA_BLOCK, d)
    k_mean = jnp.mean(k_blocks.astype(jnp.float32), axis=3)
    q_block = jnp.arange(s_pad) // MOBA_BLOCK
    gate = jnp.einsum('bhsd,bhnd->bhsn', q.astype(jnp.float32), k_mean)
    past = jnp.arange(nb)[None, :] < q_block[:, None]
    gate = jnp.where(past, gate, NEG)
    n_sel = min(MOBA_TOPK, nb)
    _, sel = lax.top_k(gate, n_sel)
    sel_valid = sel < q_block[:, None]

    nc = s_pad // Q_CHUNK

    def to_chunks(t):
        return jnp.moveaxis(t.reshape(b, h, nc, Q_CHUNK, *t.shape[3:]), 2, 0)

    gather = jax.vmap(jax.vmap(lambda blocks, idx: blocks[idx]))
    scale = HEAD_DIM ** -0.5
    key_off = jnp.arange(MOBA_BLOCK)

    def one_chunk(args):
        c, qc, sc, vc = args
        qpos = c * Q_CHUNK + jnp.arange(Q_CHUNK)
        own = (c * Q_CHUNK) // MOBA_BLOCK
        k_own = lax.dynamic_index_in_dim(k_blocks, own, axis=2, keepdims=False).astype(jnp.float32)
        v_own = lax.dynamic_index_in_dim(v_blocks, own, axis=2, keepdims=False).astype(jnp.float32)
        k_sel = gather(k_blocks, sc).astype(jnp.float32)
        v_sel = gather(v_blocks, sc).astype(jnp.float32)
        qf = qc.astype(jnp.float32) * scale
        s_sel = jnp.einsum('bhqd,bhqnkd->bhqnk', qf, k_sel)
        s_sel = jnp.where(vc[..., None], s_sel, NEG).reshape(b, h, Q_CHUNK, n_sel * MOBA_BLOCK)
        s_own = jnp.einsum('bhqd,bhkd->bhqk', qf, k_own)
        causal = (own * MOBA_BLOCK + key_off)[None, :] <= qpos[:, None]
        s_own = jnp.where(causal, s_own, NEG)
        p = jax.nn.softmax(jnp.concatenate([s_sel, s_own], axis=-1), axis=-1)
        p_sel = p[..., :n_sel * MOBA_BLOCK].reshape(b, h, Q_CHUNK, n_sel, MOBA_BLOCK)
        p_own = p[..., n_sel * MOBA_BLOCK:]
        o = (jnp.einsum('bhqnk,bhqnkd->bhqd', p_sel, v_sel)
             + jnp.einsum('bhqk,bhkd->bhqd', p_own, v_own))
        return o.astype(q.dtype)

    out = lax.map(one_chunk, (jnp.arange(nc), to_chunks(q), to_chunks(sel), to_chunks(sel_valid)))
    out = jnp.moveaxis(out, 0, 2).reshape(b, h, s_pad, d)
    return out[:, :, :s]


def multiscale_pool(u, w_pool, pool_scale):
    b, s, _ = u.shape
    uf = u.astype(jnp.float32)
    cs = jnp.pad(lax.cumsum(uf, axis=1), ((0, 0), (1, 0), (0, 0)))
    t = jnp.arange(s)
    outs = []
    for g, w in enumerate(POOL_WINDOWS):
        sl = slice(g * POOL_GROUP_DIM, (g + 1) * POOL_GROUP_DIM)
        start = jnp.maximum(t + 1 - w, 0)
        count = (t + 1 - start).astype(jnp.float32)
        win_sum = cs[:, 1:, sl] - cs[:, start, sl]
        outs.append(win_sum / count[None, :, None] - uf[:, :, sl])
    pooled = jnp.stack(outs, axis=2)
    mixed = jnp.einsum('bsgc,gcd->bsgd', pooled, w_pool.astype(jnp.float32)).reshape(b, s, D_POOL)
    return (mixed * pool_scale.astype(jnp.float32)).astype(u.dtype)


def conv_ffn(x, w_ffn_gate, w_ffn_up, conv_w, conv_b, w_ffn_down):
    s = x.shape[1]
    a = x @ w_ffn_gate
    u = x @ w_ffn_up
    ap = jnp.pad(a, ((0, 0), (CONV_WIDTH - 1, 0), (0, 0)))
    a = sum(ap[:, i:i + s] * conv_w[i] for i in range(CONV_WIDTH)) + conv_b
    return (jax.nn.gelu(a, approximate=False) * u) @ w_ffn_down


def _fwd_setup_inputs(seed: int = 0) -> dict:
    key = jax.random.key(seed)
    ks = jax.random.split(key, 20)
    L, D = DEPTH, D_MODEL
    f32 = jnp.float32

    def nrm(k, shape, fan_in, gain=1.0):
        return (jax.random.normal(k, shape, f32) * (gain * fan_in ** -0.5)).astype(f32)

    return {
        "x": jax.random.normal(ks[0], (BATCH, SEQ, D), f32),
        "w_in": nrm(ks[1], (L, D, D_IN_PROJ), D),
        "b_gate": 0.02 * jax.random.normal(ks[2], (L, N_BRANCHES * D), f32),
        "w_branch_attn": nrm(ks[3], (L, D_ATTN, D), D_ATTN),
        "w_pool": nrm(ks[4], (L, N_POOL_GROUPS, POOL_GROUP_DIM, POOL_GROUP_DIM), POOL_GROUP_DIM),
        "pool_scale": 1.0 + 0.05 * jax.random.normal(ks[5], (L, D_POOL), f32),
        "w_branch_pool": nrm(ks[6], (L, D_POOL, D), D_POOL),
        "w_out": nrm(ks[7], (L, D, D), D, DEEPNORM_BETA),
        "ln1_g": 1.0 + 0.05 * jax.random.normal(ks[8], (L, D), f32),
        "ln1_b": 0.02 * jax.random.normal(ks[9], (L, D), f32),
        "w_ffn_gate": nrm(ks[10], (L, D, D_FF), D),
        "w_ffn_up": nrm(ks[11], (L, D, D_FF), D),
        "conv_w": nrm(ks[12], (L, CONV_WIDTH, D_FF), CONV_WIDTH),
        "conv_b": 0.02 * jax.random.normal(ks[13], (L, D_FF), f32),
        "w_ffn_down": nrm(ks[14], (L, D_FF, D), D_FF, DEEPNORM_BETA),
        "ln2_g": 1.0 + 0.05 * jax.random.normal(ks[15], (L, D), f32),
        "ln2_b": 0.02 * jax.random.normal(ks[16], (L, D), f32),
    }


def _fwd_reference(x, w_in, b_gate, w_branch_attn, w_pool, pool_scale, w_branch_pool, w_out,
              ln1_g, ln1_b, w_ffn_gate, w_ffn_up, conv_w, conv_b, w_ffn_down, ln2_g, ln2_b):
    b, s, _ = x.shape
    cos, sin = rope_tables(s)
    for l in range(DEPTH):
        z = x @ w_in[l]
        o0, o1, o2, o3 = D_ATTN, 2 * D_ATTN, 3 * D_ATTN, 3 * D_ATTN + D_POOL
        heads = lambda t: t.reshape(b, s, ATTN_HEADS, HEAD_DIM).transpose(0, 2, 1, 3)
        q = apply_rope(heads(z[..., :o0]), cos, sin)
        k = apply_rope(heads(z[..., o0:o1]), cos, sin)
        v = heads(z[..., o1:o2])
        u_pool = z[..., o2:o3]
        gates = jax.nn.sigmoid(z[..., o3:] + b_gate[l])
        g_attn, g_pool = gates[..., :D_MODEL], gates[..., D_MODEL:]
        y_attn = moba_attention(q, k, v).transpose(0, 2, 1, 3).reshape(b, s, D_ATTN)
        y_attn = y_attn @ w_branch_attn[l]
        y_pool = multiscale_pool(u_pool, w_pool[l], pool_scale[l]) @ w_branch_pool[l]
        mix = (g_attn * y_attn + g_pool * y_pool) @ w_out[l]
        x = layer_norm(DEEPNORM_ALPHA * x + mix, ln1_g[l], ln1_b[l])
        ffn = conv_ffn(x, w_ffn_gate[l], w_ffn_up[l], conv_w[l], conv_b[l], w_ffn_down[l])
        x = layer_norm(DEEPNORM_ALPHA * x + ffn, ln2_g[l], ln2_b[l])
    return x


import jax as _jax
import jax.numpy as _jnp

TWIN_FORMAT = 'train_step'
FWD_PARAMS = ['x', 'w_in', 'b_gate', 'w_branch_attn', 'w_pool', 'pool_scale', 'w_branch_pool', 'w_out', 'ln1_g', 'ln1_b', 'w_ffn_gate', 'w_ffn_up', 'conv_w', 'conv_b', 'w_ffn_down', 'ln2_g', 'ln2_b']
TWIN_WEIGHTS = ['w_in', 'b_gate', 'w_branch_attn', 'w_pool', 'pool_scale', 'w_branch_pool', 'w_out', 'ln1_g', 'ln1_b', 'w_ffn_gate', 'w_ffn_up', 'conv_w', 'conv_b', 'w_ffn_down', 'ln2_g', 'ln2_b']
TWIN_DIFF_INPUT = 'x'
TWIN_INPUTS = ['x', 'w_in', 'b_gate', 'w_branch_attn', 'w_pool', 'pool_scale', 'w_branch_pool', 'w_out', 'ln1_g', 'ln1_b', 'w_ffn_gate', 'w_ffn_up', 'conv_w', 'conv_b', 'w_ffn_down', 'ln2_g', 'ln2_b', 'loss_target', 'm_w_in', 'm_b_gate', 'm_w_branch_attn', 'm_w_pool', 'm_pool_scale', 'm_w_branch_pool', 'm_w_out', 'm_ln1_g', 'm_ln1_b', 'm_w_ffn_gate', 'm_w_ffn_up', 'm_conv_w', 'm_conv_b', 'm_w_ffn_down', 'm_ln2_g', 'm_ln2_b', 'v_w_in', 'v_b_gate', 'v_w_branch_attn', 'v_w_pool', 'v_pool_scale', 'v_w_branch_pool', 'v_w_out', 'v_ln1_g', 'v_ln1_b', 'v_w_ffn_gate', 'v_w_ffn_up', 'v_conv_w', 'v_conv_b', 'v_w_ffn_down', 'v_ln2_g', 'v_ln2_b']
TWIN_OUTPUTS = ['loss', 'grad_x', 'grad_w_in', 'grad_b_gate', 'grad_w_branch_attn', 'grad_w_pool', 'grad_pool_scale', 'grad_w_branch_pool', 'grad_w_out', 'grad_ln1_g', 'grad_ln1_b', 'grad_w_ffn_gate', 'grad_w_ffn_up', 'grad_conv_w', 'grad_conv_b', 'grad_w_ffn_down', 'grad_ln2_g', 'grad_ln2_b', 'delta_w_in', 'delta_b_gate', 'delta_w_branch_attn', 'delta_w_pool', 'delta_pool_scale', 'delta_w_branch_pool', 'delta_w_out', 'delta_ln1_g', 'delta_ln1_b', 'delta_w_ffn_gate', 'delta_w_ffn_up', 'delta_conv_w', 'delta_conv_b', 'delta_w_ffn_down', 'delta_ln2_g', 'delta_ln2_b', 'new_m_w_in', 'new_m_b_gate', 'new_m_w_branch_attn', 'new_m_w_pool', 'new_m_pool_scale', 'new_m_w_branch_pool', 'new_m_w_out', 'new_m_ln1_g', 'new_m_ln1_b', 'new_m_w_ffn_gate', 'new_m_w_ffn_up', 'new_m_conv_w', 'new_m_conv_b', 'new_m_w_ffn_down', 'new_m_ln2_g', 'new_m_ln2_b', 'new_v_w_in', 'new_v_b_gate', 'new_v_w_branch_attn', 'new_v_w_pool', 'new_v_pool_scale', 'new_v_w_branch_pool', 'new_v_w_out', 'new_v_ln1_g', 'new_v_ln1_b', 'new_v_w_ffn_gate', 'new_v_w_ffn_up', 'new_v_conv_w', 'new_v_conv_b', 'new_v_w_ffn_down', 'new_v_ln2_g', 'new_v_ln2_b']
TWIN_LEAF_KINDS = {'loss': 'loss', 'grad_x': 'grad_x', 'grad_w_in': 'grad_w', 'grad_b_gate': 'grad_w', 'grad_w_branch_attn': 'grad_w', 'grad_w_pool': 'grad_w', 'grad_pool_scale': 'grad_w', 'grad_w_branch_pool': 'grad_w', 'grad_w_out': 'grad_w', 'grad_ln1_g': 'grad_w', 'grad_ln1_b': 'grad_w', 'grad_w_ffn_gate': 'grad_w', 'grad_w_ffn_up': 'grad_w', 'grad_conv_w': 'grad_w', 'grad_conv_b': 'grad_w', 'grad_w_ffn_down': 'grad_w', 'grad_ln2_g': 'grad_w', 'grad_ln2_b': 'grad_w', 'delta_w_in': 'delta_w', 'delta_b_gate': 'delta_w', 'delta_w_branch_attn': 'delta_w', 'delta_w_pool': 'delta_w', 'delta_pool_scale': 'delta_w', 'delta_w_branch_pool': 'delta_w', 'delta_w_out': 'delta_w', 'delta_ln1_g': 'delta_w', 'delta_ln1_b': 'delta_w', 'delta_w_ffn_gate': 'delta_w', 'delta_w_ffn_up': 'delta_w', 'delta_conv_w': 'delta_w', 'delta_conv_b': 'delta_w', 'delta_w_ffn_down': 'delta_w', 'delta_ln2_g': 'delta_w', 'delta_ln2_b': 'delta_w', 'new_m_w_in': 'new_m', 'new_m_b_gate': 'new_m', 'new_m_w_branch_attn': 'new_m', 'new_m_w_pool': 'new_m', 'new_m_pool_scale': 'new_m', 'new_m_w_branch_pool': 'new_m', 'new_m_w_out': 'new_m', 'new_m_ln1_g': 'new_m', 'new_m_ln1_b': 'new_m', 'new_m_w_ffn_gate': 'new_m', 'new_m_w_ffn_up': 'new_m', 'new_m_conv_w': 'new_m', 'new_m_conv_b': 'new_m', 'new_m_w_ffn_down': 'new_m', 'new_m_ln2_g': 'new_m', 'new_m_ln2_b': 'new_m', 'new_v_w_in': 'new_v', 'new_v_b_gate': 'new_v', 'new_v_w_branch_attn': 'new_v', 'new_v_w_pool': 'new_v', 'new_v_pool_scale': 'new_v', 'new_v_w_branch_pool': 'new_v', 'new_v_w_out': 'new_v', 'new_v_ln1_g': 'new_v', 'new_v_ln1_b': 'new_v', 'new_v_w_ffn_gate': 'new_v', 'new_v_w_ffn_up': 'new_v', 'new_v_conv_w': 'new_v', 'new_v_conv_b': 'new_v', 'new_v_w_ffn_down': 'new_v', 'new_v_ln2_g': 'new_v', 'new_v_ln2_b': 'new_v'}


def _forward(args):
    return _fwd_reference(*[args[k] for k in FWD_PARAMS])


def _output_shape():
    out = _jax.eval_shape(lambda: _forward(_fwd_setup_inputs(0)))
    return out.shape, out.dtype

N_MICROBATCH = 1
ADAM_LR = 0.001
ADAM_B1 = 0.9
ADAM_B2 = 0.999
ADAM_EPS = 1e-08
ADAM_WD = 0.01
ADAM_STEP = 10
PER_EXAMPLE_BATCH_AXIS = {'x': 0, 'loss_target': 0}
SHARED_INPUTS = []
_WEIGHT_DTYPES = {'w_in': _jnp.float32, 'b_gate': _jnp.float32, 'w_branch_attn': _jnp.float32, 'w_pool': _jnp.float32, 'pool_scale': _jnp.float32, 'w_branch_pool': _jnp.float32, 'w_out': _jnp.float32, 'ln1_g': _jnp.float32, 'ln1_b': _jnp.float32, 'w_ffn_gate': _jnp.float32, 'w_ffn_up': _jnp.float32, 'conv_w': _jnp.float32, 'conv_b': _jnp.float32, 'w_ffn_down': _jnp.float32, 'ln2_g': _jnp.float32, 'ln2_b': _jnp.float32}
MOMENT_SCALE = {'w_in': 1.747582e-02, 'b_gate': 8.977378e-03, 'w_branch_attn': 7.490868e-03, 'w_pool': 4.459076e-02, 'pool_scale': 4.152080e-02, 'w_branch_pool': 3.137951e-02, 'w_out': 5.280482e-02, 'ln1_g': 1.407142e+00, 'ln1_b': 2.576509e-01, 'w_ffn_gate': 2.454117e-02, 'w_ffn_up': 2.394940e-02, 'conv_w': 2.446439e-02, 'conv_b': 2.408986e-02, 'w_ffn_down': 6.692937e-02, 'ln2_g': 1.605116e+01, 'ln2_b': 3.858768e-01}


def _to_microbatches(a, axis):
    t = _jnp.moveaxis(a, axis, 0)
    t = t.reshape((N_MICROBATCH, t.shape[0] // N_MICROBATCH) + t.shape[1:])
    return _jnp.moveaxis(t, 1, axis + 1)


def setup_inputs(seed: int = 0) -> dict:
    inp = _fwd_setup_inputs(seed)
    key = _jax.random.fold_in(_jax.random.key(seed), 7919)
    shape, _ = _output_shape()
    out = dict(inp)
    out["loss_target"] = _jax.random.normal(_jax.random.fold_in(key, 0), shape, _jnp.float32)
    for i, name in enumerate(TWIN_WEIGHTS):
        w = inp[name].astype(_jnp.float32)
        if MOMENT_SCALE is None:
            s = _jnp.sqrt(_jnp.mean(_jnp.square(w)) + 1e-30)
        else:
            s = MOMENT_SCALE[name]
        km, kv = _jax.random.split(_jax.random.fold_in(key, i + 1))
        out[name] = w
        out["m_" + name] = s * _jax.random.normal(km, w.shape, _jnp.float32)
        out["v_" + name] = (s * s) * _jax.random.uniform(kv, w.shape, _jnp.float32, 0.5, 1.5)
    if N_MICROBATCH > 1:
        for name, axis in PER_EXAMPLE_BATCH_AXIS.items():
            out[name] = _to_microbatches(out[name], axis)
    return {'x': out['x'], 'w_in': out['w_in'], 'b_gate': out['b_gate'], 'w_branch_attn': out['w_branch_attn'], 'w_pool': out['w_pool'], 'pool_scale': out['pool_scale'], 'w_branch_pool': out['w_branch_pool'], 'w_out': out['w_out'], 'ln1_g': out['ln1_g'], 'ln1_b': out['ln1_b'], 'w_ffn_gate': out['w_ffn_gate'], 'w_ffn_up': out['w_ffn_up'], 'conv_w': out['conv_w'], 'conv_b': out['conv_b'], 'w_ffn_down': out['w_ffn_down'], 'ln2_g': out['ln2_g'], 'ln2_b': out['ln2_b'], 'loss_target': out['loss_target'], 'm_w_in': out['m_w_in'], 'm_b_gate': out['m_b_gate'], 'm_w_branch_attn': out['m_w_branch_attn'], 'm_w_pool': out['m_w_pool'], 'm_pool_scale': out['m_pool_scale'], 'm_w_branch_pool': out['m_w_branch_pool'], 'm_w_out': out['m_w_out'], 'm_ln1_g': out['m_ln1_g'], 'm_ln1_b': out['m_ln1_b'], 'm_w_ffn_gate': out['m_w_ffn_gate'], 'm_w_ffn_up': out['m_w_ffn_up'], 'm_conv_w': out['m_conv_w'], 'm_conv_b': out['m_conv_b'], 'm_w_ffn_down': out['m_w_ffn_down'], 'm_ln2_g': out['m_ln2_g'], 'm_ln2_b': out['m_ln2_b'], 'v_w_in': out['v_w_in'], 'v_b_gate': out['v_b_gate'], 'v_w_branch_attn': out['v_w_branch_attn'], 'v_w_pool': out['v_w_pool'], 'v_pool_scale': out['v_pool_scale'], 'v_w_branch_pool': out['v_w_branch_pool'], 'v_w_out': out['v_w_out'], 'v_ln1_g': out['v_ln1_g'], 'v_ln1_b': out['v_ln1_b'], 'v_w_ffn_gate': out['v_w_ffn_gate'], 'v_w_ffn_up': out['v_w_ffn_up'], 'v_conv_w': out['v_conv_w'], 'v_conv_b': out['v_conv_b'], 'v_w_ffn_down': out['v_w_ffn_down'], 'v_ln2_g': out['v_ln2_g'], 'v_ln2_b': out['v_ln2_b']}


def _loss(weights, diff, rest, loss_target):
    with _jax.named_scope("forward"):
        args = {**rest, TWIN_DIFF_INPUT: diff, **{k: w.astype(_WEIGHT_DTYPES[k]) for k, w in weights.items()}}
        y = _forward(args)
    with _jax.named_scope("loss_head"):
        err = _jnp.square(y.astype(_jnp.float32) - loss_target)
        return 0.5 * _jnp.sum(_jnp.mean(err, axis=-1)) if err.ndim else 0.5 * err


def _adamw(w, g, m, v):
    m = ADAM_B1 * m + (1.0 - ADAM_B1) * g
    v = ADAM_B2 * v + (1.0 - ADAM_B2) * _jnp.square(g)
    m_hat = m / (1.0 - ADAM_B1 ** ADAM_STEP)
    v_hat = v / (1.0 - ADAM_B2 ** ADAM_STEP)
    delta = -ADAM_LR * (m_hat / (_jnp.sqrt(v_hat) + ADAM_EPS) + ADAM_WD * w)
    return delta, m, v


def reference(x, w_in, b_gate, w_branch_attn, w_pool, pool_scale, w_branch_pool, w_out, ln1_g, ln1_b, w_ffn_gate, w_ffn_up, conv_w, conv_b, w_ffn_down, ln2_g, ln2_b, loss_target, m_w_in, m_b_gate, m_w_branch_attn, m_w_pool, m_pool_scale, m_w_branch_pool, m_w_out, m_ln1_g, m_ln1_b, m_w_ffn_gate, m_w_ffn_up, m_conv_w, m_conv_b, m_w_ffn_down, m_ln2_g, m_ln2_b, v_w_in, v_b_gate, v_w_branch_attn, v_w_pool, v_pool_scale, v_w_branch_pool, v_w_out, v_ln1_g, v_ln1_b, v_w_ffn_gate, v_w_ffn_up, v_conv_w, v_conv_b, v_w_ffn_down, v_ln2_g, v_ln2_b):
    given = dict(x=x, w_in=w_in, b_gate=b_gate, w_branch_attn=w_branch_attn, w_pool=w_pool, pool_scale=pool_scale, w_branch_pool=w_branch_pool, w_out=w_out, ln1_g=ln1_g, ln1_b=ln1_b, w_ffn_gate=w_ffn_gate, w_ffn_up=w_ffn_up, conv_w=conv_w, conv_b=conv_b, w_ffn_down=w_ffn_down, ln2_g=ln2_g, ln2_b=ln2_b, loss_target=loss_target, m_w_in=m_w_in, m_b_gate=m_b_gate, m_w_branch_attn=m_w_branch_attn, m_w_pool=m_w_pool, m_pool_scale=m_pool_scale, m_w_branch_pool=m_w_branch_pool, m_w_out=m_w_out, m_ln1_g=m_ln1_g, m_ln1_b=m_ln1_b, m_w_ffn_gate=m_w_ffn_gate, m_w_ffn_up=m_w_ffn_up, m_conv_w=m_conv_w, m_conv_b=m_conv_b, m_w_ffn_down=m_w_ffn_down, m_ln2_g=m_ln2_g, m_ln2_b=m_ln2_b, v_w_in=v_w_in, v_b_gate=v_b_gate, v_w_branch_attn=v_w_branch_attn, v_w_pool=v_w_pool, v_pool_scale=v_pool_scale, v_w_branch_pool=v_w_branch_pool, v_w_out=v_w_out, v_ln1_g=v_ln1_g, v_ln1_b=v_ln1_b, v_w_ffn_gate=v_w_ffn_gate, v_w_ffn_up=v_w_ffn_up, v_conv_w=v_conv_w, v_conv_b=v_conv_b, v_w_ffn_down=v_w_ffn_down, v_ln2_g=v_ln2_g, v_ln2_b=v_ln2_b)
    weights = {n: given[n] for n in TWIN_WEIGHTS}
    shared = {n: given[n] for n in SHARED_INPUTS}
    per_example = {n: given[n] for n in ['x']}
    grad_fn = _jax.value_and_grad(_loss, argnums=(0, 1))

    def one_microbatch(ex, loss_target):
        ex = dict(ex)
        diff = ex.pop(TWIN_DIFF_INPUT)
        return grad_fn(weights, diff, {**shared, **ex}, loss_target)

    if N_MICROBATCH == 1:
        loss, (grad_w, grad_x) = one_microbatch(per_example, given["loss_target"])
    else:
        def body(carry, xs):
            loss_sum, grad_sum = carry
            l_k, (gw_k, gx_k) = one_microbatch(xs[0], xs[1])
            with _jax.named_scope("update"):
                return (loss_sum + l_k, _jax.tree.map(_jnp.add, grad_sum, gw_k)), gx_k

        init = (_jnp.zeros((), _jnp.float32), _jax.tree.map(_jnp.zeros_like, weights))
        (loss, grad_w), grad_x = _jax.lax.scan(body, init, (per_example, given["loss_target"]))
    with _jax.named_scope("update"):
        delta_w, new_m, new_v = {}, {}, {}
        for n in TWIN_WEIGHTS:
            delta_w[n], new_m[n], new_v[n] = _adamw(weights[n], grad_w[n], given["m_" + n], given["v_" + n])
    return (loss, grad_x, *[grad_w[n] for n in TWIN_WEIGHTS], *[delta_w[n] for n in TWIN_WEIGHTS],
            *[new_m[n] for n in TWIN_WEIGHTS], *[new_v[n] for n in TWIN_WEIGHTS])
```

```python
import functools
import math

import jax
import jax.numpy as jnp
from jax import lax
from jax.experimental import pallas as pl
from jax.experimental.pallas import tpu as pltpu

F32 = jnp.float32
BF16 = jnp.bfloat16

D_MODEL = 1024
N_HEADS = 8
HEAD_DIM = 64
D_ATTN = N_HEADS * HEAD_DIM
MOBA_BLOCK = 256
MOBA_TOPK = 3
ROPE_THETA = 10000.0
POOL_WINDOWS = (2, 4, 8, 16)
POOL_GROUP = 128
D_POOL = len(POOL_WINDOWS) * POOL_GROUP
POOL_HALO = 16
D_FF = 2816
D_IN_PROJ = 3 * D_ATTN + D_POOL + 2 * D_MODEL
LN_EPS = 1e-5
ALPHA = 2.0 ** 0.25
NEG = -1e30
N_DEV = 8
FF_SHARD = D_FF // N_DEV

ADAM_LR = 0.001
ADAM_B1 = 0.9
ADAM_B2 = 0.999
ADAM_EPS = 1e-08
ADAM_WD = 0.01
ADAM_STEP = 10

TOK = 256
FF_CHUNK = 256
LANE = 128
VMEM_LIMIT = 56 * 1024 * 1024

MESH = pl.DeviceIdType.MESH
NT_DIMS = (((1,), (1,)), ((), ()))
TN_DIMS = (((0,), (0,)), ((), ()))


def _params(*sem):
    return pltpu.CompilerParams(dimension_semantics=sem or None, vmem_limit_bytes=VMEM_LIMIT)


def _full(shape):
    zeros = (0,) * len(shape)
    return pl.BlockSpec(shape, lambda *_: zeros, pipeline_mode=pl.Buffered(1))


def _rows(width, tile=TOK):
    return pl.BlockSpec((tile, width), lambda i: (i, 0))


def _sds(shape, dtype):
    return jax.ShapeDtypeStruct(shape, dtype)


def _dot(a, b):
    return jnp.dot(a, b, preferred_element_type=F32)


def _dot_nt(a, b):
    return lax.dot_general(a, b, NT_DIMS, preferred_element_type=F32)


def _dot_tn(a, b):
    return lax.dot_general(a, b, TN_DIMS, preferred_element_type=F32)


def _rope_tables(seq):
    half = HEAD_DIM // 2
    inv_freq = 1.0 / (ROPE_THETA ** (jnp.arange(half, dtype=F32) / half))
    ang = jnp.arange(seq, dtype=F32)[:, None] * inv_freq[None, :]
    cos, sin = jnp.cos(ang), jnp.sin(ang)
    return jnp.tile(cos, (1, 4)), jnp.tile(jnp.concatenate([-sin, sin], axis=1), (1, 2))


def _swap_halves(t):
    lane = lax.broadcasted_iota(jnp.int32, t.shape, 1)
    return jnp.where((lane % HEAD_DIM) < HEAD_DIM // 2, pltpu.roll(t, LANE - 32, 1), pltpu.roll(t, 32, 1))


def _rope(t, cos, sin):
    return t * cos + _swap_halves(t) * sin


def _rope_transposed(g, cos, sin):
    return g * cos + _swap_halves(g * sin)


def _ln_fwd(r, g, b):
    mu = jnp.mean(r, axis=-1, keepdims=True)
    xc = r - mu
    var = jnp.mean(xc * xc, axis=-1, keepdims=True)
    rstd = lax.rsqrt(var + LN_EPS)
    xhat = xc * rstd
    return xhat * g + b, xhat, rstd


def _ln_bwd(dy, xhat, rstd, g):
    dxh = dy * g
    m1 = jnp.mean(dxh, axis=-1, keepdims=True)
    m2 = jnp.mean(dxh * xhat, axis=-1, keepdims=True)
    return rstd * (dxh - m1 - xhat * m2)


def _gelu_parts(a):
    cdf = 0.5 * (1.0 + lax.erf(a * (1.0 / math.sqrt(2.0))))
    pdf = jnp.exp(-0.5 * a * a) * (1.0 / math.sqrt(2.0 * math.pi))
    return a * cdf, cdf + a * pdf


def _shift_down(a, k):
    row = lax.broadcasted_iota(jnp.int32, a.shape, 0)
    return jnp.where(row >= k, pltpu.roll(a, k, 0), 0.0)


def _shift_up(a, k):
    n = a.shape[0]
    row = lax.broadcasted_iota(jnp.int32, a.shape, 0)
    return jnp.where(row < n - k, pltpu.roll(a, n - k, 0), 0.0)


def _conv(a, cw, cb):
    return cw[2:3, :] * a + cw[1:2, :] * _shift_down(a, 1) + cw[0:1, :] * _shift_down(a, 2) + cb


def _pool_count(first_row, rows, window):
    t = first_row + lax.broadcasted_iota(jnp.int32, (rows, 1), 0)
    return jnp.minimum(t + 1, window).astype(F32)


def _proj_in(x, win, b_gate, cos, sin):
    seq = x.shape[0]
    nt = seq // TOK

    def body(x_ref, win_ref, bg_ref, cos_ref, sin_ref, xb_ref, q_ref, k_ref, v_ref, u_ref, g_ref, km_ref):
        xb = x_ref[...].astype(BF16)
        xb_ref[...] = xb
        cos_t, sin_t = cos_ref[...], sin_ref[...]
        for sec, out_ref in ((0, q_ref), (1, k_ref)):
            z = _dot(xb, win_ref[sec])
            for c in range(D_ATTN // LANE):
                cols = slice(LANE * c, LANE * (c + 1))
                out_ref[:, cols] = _rope(z[:, cols], cos_t, sin_t)
        km_ref[0] = jnp.mean(k_ref[...], axis=0, keepdims=True)
        v_ref[...] = _dot(xb, win_ref[2]).astype(BF16)
        u_ref[...] = _dot(xb, win_ref[3])
        for n in range(4):
            cols = slice(D_ATTN * n, D_ATTN * (n + 1))
            g_ref[:, cols] = jax.nn.sigmoid(_dot(xb, win_ref[4 + n]) + bg_ref[:, cols])

    return pl.pallas_call(
        body, name="proj_in", grid=(nt,),
        in_specs=[_rows(D_MODEL), _full(win.shape), _full((1, 2 * D_MODEL)), _rows(LANE), _rows(LANE)],
        out_specs=[_rows(D_MODEL), _rows(D_ATTN), _rows(D_ATTN), _rows(D_ATTN), _rows(D_POOL), _rows(2 * D_MODEL),
                   pl.BlockSpec((1, 1, D_ATTN), lambda i: (i, 0, 0))],
        out_shape=[_sds((seq, D_MODEL), BF16), _sds((seq, D_ATTN), F32), _sds((seq, D_ATTN), F32),
                   _sds((seq, D_ATTN), BF16), _sds((seq, D_POOL), F32), _sds((seq, 2 * D_MODEL), F32),
                   _sds((nt, 1, D_ATTN), F32)],
        compiler_params=_params("parallel"),
    )(x, win, b_gate, cos, sin)


def _masked_scores(qf, kb, km, i):
    scale = HEAD_DIM ** -0.5
    width = MOBA_BLOCK * (i + 1)
    s = _dot_nt((qf * scale).astype(BF16), kb[:width])
    pieces = []
    if i > MOBA_TOPK:
        gate = lax.dot_general(qf, km, NT_DIMS, precision=lax.Precision.HIGHEST, preferred_element_type=F32)
        cols = [gate[:, j:j + 1] for j in range(i)]
        for j in range(i):
            rank = jnp.zeros_like(cols[j])
            for jp in range(i):
                if jp != j:
                    above = (cols[jp] >= cols[j]) if jp < j else (cols[jp] > cols[j])
                    rank = rank + above.astype(F32)
            blk = slice(MOBA_BLOCK * j, MOBA_BLOCK * (j + 1))
            pieces.append(jnp.where(rank < MOBA_TOPK, s[:, blk], NEG))
    elif i > 0:
        pieces.append(s[:, :MOBA_BLOCK * i])
    row = lax.broadcasted_iota(jnp.int32, (MOBA_BLOCK, MOBA_BLOCK), 0)
    col = lax.broadcasted_iota(jnp.int32, (MOBA_BLOCK, MOBA_BLOCK), 1)
    pieces.append(jnp.where(col <= row, s[:, MOBA_BLOCK * i:], NEG))
    return pieces[0] if len(pieces) == 1 else jnp.concatenate(pieces, axis=1)


def _attn_fwd(q, k, v, kmean):
    seq = q.shape[0]
    nb = seq // MOBA_BLOCK
    pair = pl.BlockSpec((seq, LANE), lambda p: (0, p))

    def body(q_ref, k_ref, v_ref, km_ref, o_ref, lse_ref):
        for hh in range(LANE // HEAD_DIM):
            ls = slice(HEAD_DIM * hh, HEAD_DIM * (hh + 1))
            kb = k_ref[:, ls].astype(BF16)
            vb = v_ref[:, ls]
            km = km_ref[:, ls]
            for i in range(nb):
                rs = slice(MOBA_BLOCK * i, MOBA_BLOCK * (i + 1))
                sm = _masked_scores(q_ref[rs, ls], kb, km, i)
                m = jnp.max(sm, axis=1, keepdims=True)
                p = jnp.exp(sm - m)
                l = jnp.sum(p, axis=1, keepdims=True)
                o = _dot(p.astype(BF16), vb[:MOBA_BLOCK * (i + 1)])
                o_ref[rs, ls] = o / l
                lse_ref[rs, ls] = jnp.broadcast_to(m + jnp.log(l), (MOBA_BLOCK, HEAD_DIM))

    return pl.pallas_call(
        body, name="attn_fwd", grid=(D_ATTN // LANE,),
        in_specs=[pair, pair, pair, pl.BlockSpec((nb, LANE), lambda p: (0, p))],
        out_specs=[pair, pair],
        out_shape=[_sds((seq, D_ATTN), F32), _sds((seq, D_ATTN), F32)],
        compiler_params=_params("parallel"),
    )(q, k, v, kmean)


def _mix(o, u, g, x, wba, wbp, wout, w_pool, pool_scale, ln_g, ln_b):
    seq = x.shape[0]

    def body(o_ref, u_ref, uprev_ref, g_ref, x_ref, wba_ref, wbp_ref, wout_ref, wp_ref, ps_ref, lg_ref, lb_ref,
             ya_ref, yp_ref, pooled_ref, mixed_ref, ypre_ref, merged_ref, xhat_ref, rstd_ref, h_ref, hb_ref, ext):
        i = pl.program_id(0)
        ya = _dot(o_ref[...].astype(BF16), wba_ref[...])
        ucur = u_ref[...]
        ext[0:POOL_HALO, :] = jnp.where(i > 0, uprev_ref[...], 0.0)
        ext[POOL_HALO:, :] = ucur
        for grp, window in enumerate(POOL_WINDOWS):
            cols = slice(POOL_GROUP * grp, POOL_GROUP * (grp + 1))
            acc = ucur[:, cols]
            for kk in range(1, window):
                acc = acc + ext[pl.ds(POOL_HALO - kk, TOK), cols]
            pooled = acc / _pool_count(i * TOK, TOK, window) - ucur[:, cols]
            pooled_ref[:, cols] = pooled.astype(BF16)
            mixed_ref[:, cols] = _dot(pooled.astype(BF16), wp_ref[grp].astype(BF16))
        mixed = mixed_ref[...]
        ypre = (mixed * ps_ref[...]).astype(BF16)
        ypre_ref[...] = ypre
        yp = _dot(ypre, wbp_ref[...])
        ya_ref[...] = ya
        yp_ref[...] = yp
        merged = (g_ref[:, :D_MODEL] * ya + g_ref[:, D_MODEL:] * yp).astype(BF16)
        merged_ref[...] = merged
        r1 = ALPHA * x_ref[...] + _dot(merged, wout_ref[...])
        h, xhat, rstd = _ln_fwd(r1, lg_ref[...], lb_ref[...])
        xhat_ref[...] = xhat
        rstd_ref[...] = jnp.broadcast_to(rstd, (TOK, LANE))
        h_ref[...] = h
        hb_ref[...] = h.astype(BF16)

    halo = pl.BlockSpec((POOL_HALO, D_POOL), lambda i: (jnp.maximum(i * (TOK // POOL_HALO) - 1, 0), 0))
    return pl.pallas_call(
        body, name="mix", grid=(seq // TOK,),
        in_specs=[_rows(D_ATTN), _rows(D_POOL), halo, _rows(2 * D_MODEL), _rows(D_MODEL),
                  _full(wba.shape), _full(wbp.shape), _full(wout.shape), _full(w_pool.shape),
                  _full((1, D_POOL)), _full((1, D_MODEL)), _full((1, D_MODEL))],
        out_specs=[_rows(D_MODEL), _rows(D_MODEL), _rows(D_POOL), _rows(D_POOL), _rows(D_POOL), _rows(D_MODEL),
                   _rows(D_MODEL), _rows(LANE), _rows(D_MODEL), _rows(D_MODEL)],
        out_shape=[_sds((seq, D_MODEL), F32), _sds((seq, D_MODEL), F32), _sds((seq, D_POOL), BF16),
                   _sds((seq, D_POOL), F32), _sds((seq, D_POOL), BF16), _sds((seq, D_MODEL), BF16),
                   _sds((seq, D_MODEL), F32), _sds((seq, LANE), F32), _sds((seq, D_MODEL), F32),
                   _sds((seq, D_MODEL), BF16)],
        scratch_shapes=[pltpu.VMEM((TOK + POOL_HALO, D_POOL), F32)],
        compiler_params=_params("parallel"),
    )(o, u, u, g, x, wba, wbp, wout, w_pool, pool_scale, ln_g, ln_b)


def _ffn_up(hb, wgt, wut, conv_w, conv_b):
    seq = hb.shape[0]
    wblk = pl.BlockSpec((FF_CHUNK, D_MODEL), lambda c: (c, 0))
    cblk = lambda rows: pl.BlockSpec((rows, FF_CHUNK), lambda c: (0, c))
    oblk = pl.BlockSpec((seq, FF_CHUNK), lambda c: (0, c))

    def body(h_ref, wg_ref, wu_ref, cw_ref, cb_ref, a_ref, u_ref, act_ref):
        h = h_ref[...]
        a = _dot_nt(h, wg_ref[...])
        u = _dot_nt(h, wu_ref[...])
        a_ref[...] = a
        u_ref[...] = u
        gelu, _ = _gelu_parts(_conv(a, cw_ref[...], cb_ref[...]))
        act_ref[...] = (gelu * u).astype(BF16)

    return pl.pallas_call(
        body, name="ffn_up", grid=(D_FF // FF_CHUNK,),
        in_specs=[_full(hb.shape), wblk, wblk, cblk(3), cblk(1)],
        out_specs=[oblk, oblk, oblk],
        out_shape=[_sds((seq, D_FF), F32), _sds((seq, D_FF), F32), _sds((seq, D_FF), BF16)],
        compiler_params=_params("parallel"),
    )(hb, wgt, wut, conv_w, conv_b)


def _ffn_down(act, wd, h, target, ln_g, ln_b):
    seq = h.shape[0]

    def body(act_ref, wd_ref, h_ref, t_ref, lg_ref, lb_ref, dr_ref, drb_ref, loss_ref, dg_ref, db_ref):
        i = pl.program_id(0)

        @pl.when(i == 0)
        def _():
            loss_ref[...] = jnp.zeros_like(loss_ref)
            dg_ref[...] = jnp.zeros_like(dg_ref)
            db_ref[...] = jnp.zeros_like(db_ref)

        r2 = ALPHA * h_ref[...] + _dot(act_ref[...], wd_ref[...])
        y, xhat, rstd = _ln_fwd(r2, lg_ref[...], lb_ref[...])
        diff = y - t_ref[...]
        loss_ref[...] += jnp.sum(diff * diff) * (0.5 / D_MODEL)
        dy = diff * (1.0 / D_MODEL)
        dg_ref[...] += jnp.sum(dy * xhat, axis=0, keepdims=True)
        db_ref[...] += jnp.sum(dy, axis=0, keepdims=True)
        dr = _ln_bwd(dy, xhat, rstd, lg_ref[...])
        dr_ref[...] = dr
        drb_ref[...] = dr.astype(BF16)

    vec = pl.BlockSpec((1, D_MODEL), lambda i: (0, 0))
    return pl.pallas_call(
        body, name="ffn_down", grid=(seq // TOK,),
        in_specs=[_rows(D_FF), _full(wd.shape), _rows(D_MODEL), _rows(D_MODEL), _full((1, D_MODEL)), _full((1, D_MODEL))],
        out_specs=[_rows(D_MODEL), _rows(D_MODEL), pl.BlockSpec((8, LANE), lambda i: (0, 0)), vec, vec],
        out_shape=[_sds((seq, D_MODEL), F32), _sds((seq, D_MODEL), BF16), _sds((8, LANE), F32),
                   _sds((1, D_MODEL), F32), _sds((1, D_MODEL), F32)],
        compiler_params=_params("arbitrary"),
    )(act, wd, h, target, ln_g, ln_b)


def _ffn_bwd(drb, hb, a, u, wd, conv_w, conv_b):
    seq = hb.shape[0]
    wblk = pl.BlockSpec((FF_CHUNK, D_MODEL), lambda c: (c, 0))
    cblk = lambda rows: pl.BlockSpec((rows, FF_CHUNK), lambda c: (0, c))
    sblk = pl.BlockSpec((seq, FF_CHUNK), lambda c: (0, c))

    def body(dr_ref, h_ref, a_ref, u_ref, wd_ref, cw_ref, cb_ref, da_ref, du_ref, dwd_ref, dwg_ref, dwu_ref, dc_ref):
        dr = dr_ref[...]
        h = h_ref[...]
        a = a_ref[...]
        u = u_ref[...]
        cw = cw_ref[...]
        dact = _dot_nt(dr, wd_ref[...])
        gelu, dgelu = _gelu_parts(_conv(a, cw, cb_ref[...]))
        dwd_ref[...] = _dot_tn((gelu * u).astype(BF16), dr).astype(BF16)
        du = (dact * gelu).astype(BF16)
        dac = dact * u * dgelu
        da = (cw[2:3, :] * dac + cw[1:2, :] * _shift_up(dac, 1) + cw[0:1, :] * _shift_up(dac, 2)).astype(BF16)
        da_ref[...] = da
        du_ref[...] = du
        dwg_ref[...] = _dot_tn(da, h).astype(BF16)
        dwu_ref[...] = _dot_tn(du, h).astype(BF16)
        dc_ref[0:1, :] = jnp.sum(dac * _shift_down(a, 2), axis=0, keepdims=True)
        dc_ref[1:2, :] = jnp.sum(dac * _shift_down(a, 1), axis=0, keepdims=True)
        dc_ref[2:3, :] = jnp.sum(dac * a, axis=0, keepdims=True)
        dc_ref[3:4, :] = jnp.sum(dac, axis=0, keepdims=True)
        dc_ref[4:8, :] = jnp.zeros((4, FF_CHUNK), F32)

    return pl.pallas_call(
        body, name="ffn_bwd", grid=(D_FF // FF_CHUNK,),
        in_specs=[_full(drb.shape), _full(hb.shape), sblk, sblk, wblk, cblk(3), cblk(1)],
        out_specs=[sblk, sblk, wblk, wblk, wblk, cblk(8)],
        out_shape=[_sds((seq, D_FF), BF16), _sds((seq, D_FF), BF16), _sds((D_FF, D_MODEL), BF16),
                   _sds((D_FF, D_MODEL), BF16), _sds((D_FF, D_MODEL), BF16), _sds((8, D_FF), F32)],
        compiler_params=_params("parallel"),
    )(drb, hb, a, u, wd, conv_w, conv_b)


def _ln1_bwd(dr2, da, du, wgt, wut, xhat, rstd, ln_g):
    seq = dr2.shape[0]

    def body(dr2_ref, da_ref, du_ref, wg_ref, wu_ref, xhat_ref, rstd_ref, lg_ref, dr_ref, drb_ref, dg_ref, db_ref):
        @pl.when(pl.program_id(0) == 0)
        def _():
            dg_ref[...] = jnp.zeros_like(dg_ref)
            db_ref[...] = jnp.zeros_like(db_ref)

        dh = ALPHA * dr2_ref[...] + _dot(da_ref[...], wg_ref[...]) + _dot(du_ref[...], wu_ref[...])
        xhat = xhat_ref[...]
        dg_ref[...] += jnp.sum(dh * xhat, axis=0, keepdims=True)
        db_ref[...] += jnp.sum(dh, axis=0, keepdims=True)
        dr = _ln_bwd(dh, xhat, rstd_ref[:, 0:1], lg_ref[...])
        dr_ref[...] = dr
        drb_ref[...] = dr.astype(BF16)

    vec = pl.BlockSpec((1, D_MODEL), lambda i: (0, 0))
    return pl.pallas_call(
        body, name="ln1_bwd", grid=(seq // TOK,),
        in_specs=[_rows(D_MODEL), _rows(D_FF), _rows(D_FF), _full(wgt.shape), _full(wut.shape), _rows(D_MODEL),
                  _rows(LANE), _full((1, D_MODEL))],
        out_specs=[_rows(D_MODEL), _rows(D_MODEL), vec, vec],
        out_shape=[_sds((seq, D_MODEL), F32), _sds((seq, D_MODEL), BF16), _sds((1, D_MODEL), F32),
                   _sds((1, D_MODEL), F32)],
        compiler_params=_params("arbitrary"),
    )(dr2, da, du, wgt, wut, xhat, rstd, ln_g)


def _mix_bwd(drb, ya, yp, g, mixed, wout, wba, wbp, w_pool, pool_scale):
    seq = drb.shape[0]

    def body(dr_ref, ya_ref, yp_ref, g_ref, mixed_ref, wout_ref, wba_ref, wbp_ref, wp_ref, ps_ref,
             dzg_ref, dya_ref, dyp_ref, do_ref, dmixed_ref, dpooled_ref, dbg_ref, dps_ref):
        @pl.when(pl.program_id(0) == 0)
        def _():
            dbg_ref[...] = jnp.zeros_like(dbg_ref)
            dps_ref[...] = jnp.zeros_like(dps_ref)

        dmerged = _dot_nt(dr_ref[...], wout_ref[...])
        ga, gp = g_ref[:, :D_MODEL], g_ref[:, D_MODEL:]
        dzga = dmerged * ya_ref[...] * ga * (1.0 - ga)
        dzgp = dmerged * yp_ref[...] * gp * (1.0 - gp)
        dzg_ref[:, :D_MODEL] = dzga.astype(BF16)
        dzg_ref[:, D_MODEL:] = dzgp.astype(BF16)
        dbg_ref[:, :D_MODEL] += jnp.sum(dzga, axis=0, keepdims=True)
        dbg_ref[:, D_MODEL:] += jnp.sum(dzgp, axis=0, keepdims=True)
        dya = (dmerged * ga).astype(BF16)
        dyp = (dmerged * gp).astype(BF16)
        dya_ref[...] = dya
        dyp_ref[...] = dyp
        do_ref[...] = _dot_nt(dya, wba_ref[...])
        dypre = _dot_nt(dyp, wbp_ref[...])
        dps_ref[...] += jnp.sum(dypre * mixed_ref[...], axis=0, keepdims=True)
        dmixed = (dypre * ps_ref[...]).astype(BF16)
        dmixed_ref[...] = dmixed
        for grp in range(len(POOL_WINDOWS)):
            cols = slice(POOL_GROUP * grp, POOL_GROUP * (grp + 1))
            dpooled_ref[:, cols] = _dot_nt(dmixed[:, cols], wp_ref[grp].astype(BF16))

    return pl.pallas_call(
        body, name="mix_bwd", grid=(seq // TOK,),
        in_specs=[_rows(D_MODEL), _rows(D_MODEL), _rows(D_MODEL), _rows(2 * D_MODEL), _rows(D_POOL),
                  _full(wout.shape), _full(wba.shape), _full(wbp.shape), _full(w_pool.shape), _full((1, D_POOL))],
        out_specs=[_rows(2 * D_MODEL), _rows(D_MODEL), _rows(D_MODEL), _rows(D_ATTN), _rows(D_POOL), _rows(D_POOL),
                   pl.BlockSpec((1, 2 * D_MODEL), lambda i: (0, 0)), pl.BlockSpec((1, D_POOL), lambda i: (0, 0))],
        out_shape=[_sds((seq, 2 * D_MODEL), BF16), _sds((seq, D_MODEL), BF16), _sds((seq, D_MODEL), BF16),
                   _sds((seq, D_ATTN), F32), _sds((seq, D_POOL), BF16), _sds((seq, D_POOL), F32),
                   _sds((1, 2 * D_MODEL), F32), _sds((1, D_POOL), F32)],
        compiler_params=_params("arbitrary"),
    )(drb, ya, yp, g, mixed, wout, wba, wbp, w_pool, pool_scale)


def _attn_bwd(q, k, v, kmean, o, lse, do, cos, sin):
    seq = q.shape[0]
    nb = seq // MOBA_BLOCK
    pair = pl.BlockSpec((seq, LANE), lambda p: (0, p))
    table = pl.BlockSpec((seq, LANE), lambda p: (0, 0))

    def body(q_ref, k_ref, v_ref, km_ref, o_ref, lse_ref, do_ref, cos_ref, sin_ref, dq_ref, dk_ref, dv_ref,
             dq_acc, dk_acc, dv_acc):
        dk_acc[...] = jnp.zeros_like(dk_acc)
        dv_acc[...] = jnp.zeros_like(dv_acc)
        for hh in range(LANE // HEAD_DIM):
            ls = slice(HEAD_DIM * hh, HEAD_DIM * (hh + 1))
            kb = k_ref[:, ls].astype(BF16)
            vb = v_ref[:, ls]
            km = km_ref[:, ls]
            for i in range(nb):
                rs = slice(MOBA_BLOCK * i, MOBA_BLOCK * (i + 1))
                ks = slice(0, MOBA_BLOCK * (i + 1))
                qf = q_ref[rs, ls]
                p = jnp.exp(_masked_scores(qf, kb, km, i) - lse_ref[rs, ls][:, 0:1])
                dob = do_ref[rs, ls]
                delta = jnp.sum(dob * o_ref[rs, ls], axis=1, keepdims=True)
                dob16 = dob.astype(BF16)
                dp = _dot_nt(dob16, vb[ks])
                ds = (p * (dp - delta) * (HEAD_DIM ** -0.5)).astype(BF16)
                dq_acc[rs, ls] = _dot(ds, kb[ks])
                dk_acc[ks, ls] += _dot_tn(ds, qf.astype(BF16))
                dv_acc[ks, ls] += _dot_tn(p.astype(BF16), dob16)
        cos_t, sin_t = cos_ref[...], sin_ref[...]
        dq_ref[...] = _rope_transposed(dq_acc[...], cos_t, sin_t).astype(BF16)
        dk_ref[...] = _rope_transposed(dk_acc[...], cos_t, sin_t).astype(BF16)
        dv_ref[...] = dv_acc[...].astype(BF16)

    return pl.pallas_call(
        body, name="attn_bwd", grid=(D_ATTN // LANE,),
        in_specs=[pair, pair, pair, pl.BlockSpec((nb, LANE), lambda p: (0, p)), pair, pair, pair, table, table],
        out_specs=[pair, pair, pair],
        out_shape=[_sds((seq, D_ATTN), BF16)] * 3,
        scratch_shapes=[pltpu.VMEM((seq, LANE), F32)] * 3,
        compiler_params=_params("parallel"),
    )(q, k, v, kmean, o, lse, do, cos, sin)


def _in_bwd(dq, dk, dv, dpooled, dzg, dr1, win):
    seq = dr1.shape[0]
    nt = seq // TOK

    def body(dq_ref, dk_ref, dv_ref, dp_ref, dpnext_ref, dzg_ref, dr_ref, win_ref, dx_ref, dz_ref, ext):
        i = pl.program_id(0)
        dp = dp_ref[...]
        dpn = jnp.where(i < nt - 1, dpnext_ref[...], 0.0)
        for grp, window in enumerate(POOL_WINDOWS):
            cols = slice(POOL_GROUP * grp, POOL_GROUP * (grp + 1))
            ext[0:TOK, cols] = dp[:, cols] / _pool_count(i * TOK, TOK, window)
            ext[TOK:, cols] = dpn[:, cols] / _pool_count((i + 1) * TOK, POOL_HALO, window)
        for grp, window in enumerate(POOL_WINDOWS):
            cols = slice(POOL_GROUP * grp, POOL_GROUP * (grp + 1))
            acc = ext[0:TOK, cols] - dp[:, cols]
            for kk in range(1, window):
                acc = acc + ext[pl.ds(kk, TOK), cols]
            dz_ref[:, 3 * D_ATTN + POOL_GROUP * grp:3 * D_ATTN + POOL_GROUP * (grp + 1)] = acc.astype(BF16)
        dz_ref[:, 0:D_ATTN] = dq_ref[...]
        dz_ref[:, D_ATTN:2 * D_ATTN] = dk_ref[...]
        dz_ref[:, 2 * D_ATTN:3 * D_ATTN] = dv_ref[...]
        dz_ref[:, 3 * D_ATTN + D_POOL:] = dzg_ref[...]
        dx = ALPHA * dr_ref[...]
        for n in range(N_DEV):
            dx = dx + _dot_nt(dz_ref[:, D_ATTN * n:D_ATTN * (n + 1)], win_ref[n])
        dx_ref[...] = dx

    halo = pl.BlockSpec((POOL_HALO, D_POOL),
                        lambda i: (jnp.minimum((i + 1) * (TOK // POOL_HALO), seq // POOL_HALO - 1), 0))
    return pl.pallas_call(
        body, name="in_bwd", grid=(nt,),
        in_specs=[_rows(D_ATTN), _rows(D_ATTN), _rows(D_ATTN), _rows(D_POOL), halo, _rows(2 * D_MODEL),
                  _rows(D_MODEL), _full(win.shape)],
        out_specs=[_rows(D_MODEL), _rows(D_IN_PROJ)],
        out_shape=[_sds((seq, D_MODEL), F32), _sds((seq, D_IN_PROJ), BF16)],
        scratch_shapes=[pltpu.VMEM((TOK + POOL_HALO, D_POOL), F32)],
        compiler_params=_params("parallel"),
    )(dq, dk, dv, dpooled, dpooled, dzg, dr1, win)


def _tn_matmul(name, a, b, out_shape, out_dtype, grid, a_spec, b_spec, o_spec):
    def body(a_ref, b_ref, o_ref):
        r = _dot_tn(a_ref[...].astype(BF16), b_ref[...].astype(BF16))
        o_ref[...] = r.reshape(o_ref.shape).astype(o_ref.dtype)

    return pl.pallas_call(
        body, name=name, grid=grid, in_specs=[a_spec, b_spec], out_specs=o_spec,
        out_shape=_sds(out_shape, out_dtype),
        compiler_params=_params(*(("parallel",) * len(grid))),
    )(a, b)


def _place():
    return lax.axis_index("x"), lax.axis_index("y"), lax.axis_index("c")


def _all_gather(name, shards):
    n = len(shards)

    def body(*refs):
        ins, outs = refs[:n], refs[n:2 * n]
        send_sems, recv_sems, local_sems = refs[2 * n:]
        x, y, c = _place()
        me, sibling = (x, y, c), (x, y, 1 - c)
        chips = [(1 - x, y), (x, 1 - y), (1 - x, 1 - y)]

        def slot(px, py, pc):
            return 4 * px + 2 * py + pc

        def copy(a, k, block, to, src=None):
            dst = outs[a].at[slot(*block)]
            return pltpu.make_async_remote_copy(
                src_ref=dst if src is None else src, dst_ref=dst,
                send_sem=send_sems.at[7 * a + k], recv_sem=recv_sems.at[7 * a + k],
                device_id=to, device_id_type=MESH)

        mine = [pltpu.make_async_copy(ins[a], outs[a].at[slot(*me)], local_sems.at[a]) for a in range(n)]
        for cp in mine:
            cp.start()
        first = []
        for a in range(n):
            first.append(copy(a, 0, me, sibling, src=ins[a]))
            first += [copy(a, 1 + j, me, (*chip, c), src=ins[a]) for j, chip in enumerate(chips)]
        for cp in first:
            cp.start()
        passed = []
        for j, chip in enumerate(chips):
            for a in range(n):
                copy(a, 1 + j, (*chip, c), me).wait_recv()
                fwd = copy(a, 4 + j, (*chip, c), sibling)
                fwd.start()
                passed.append(fwd)
        for a in range(n):
            copy(a, 0, sibling, me).wait_recv()
            for j, chip in enumerate(chips):
                copy(a, 4 + j, (*chip, 1 - c), me).wait_recv()
        for cp in first + passed:
            cp.wait_send()
        for cp in mine:
            cp.wait()

    hbm = pl.BlockSpec(memory_space=pl.ANY)
    return pl.pallas_call(
        body, name=name,
        in_specs=[hbm] * n, out_specs=[hbm] * n,
        out_shape=[_sds((N_DEV, *s.shape), s.dtype) for s in shards],
        scratch_shapes=[pltpu.SemaphoreType.DMA((7 * n,)), pltpu.SemaphoreType.DMA((7 * n,)),
                        pltpu.SemaphoreType.DMA((n,))],
    )(*shards)


def _scatter_blocks(name, partials):
    n = len(partials)

    def body(*refs):
        ins, outs = refs[:n], refs[n:2 * n]
        send_sems, recv_sems, local_sems = refs[2 * n:]
        x, y, c = _place()
        me = 4 * x + 2 * y + c
        peers = [(x, y, 1 - c), (1 - x, y, c), (x, 1 - y, c), (1 - x, 1 - y, c),
                 (1 - x, y, 1 - c), (x, 1 - y, 1 - c), (1 - x, 1 - y, 1 - c)]

        def copy(a, k):
            px, py, pc = peers[k]
            return pltpu.make_async_remote_copy(
                src_ref=ins[a].at[4 * px + 2 * py + pc], dst_ref=outs[a].at[me],
                send_sem=send_sems.at[7 * a + k], recv_sem=recv_sems.at[7 * a + k],
                device_id=peers[k], device_id_type=MESH)

        mine = [pltpu.make_async_copy(ins[a].at[me], outs[a].at[me], local_sems.at[a]) for a in range(n)]
        for cp in mine:
            cp.start()
        sends = [copy(a, k) for k in range(7) for a in range(n)]
        for cp in sends:
            cp.start()
        for a in range(n):
            for k in range(7):
                px, py, pc = peers[k]
                pltpu.make_async_remote_copy(
                    src_ref=ins[a].at[me], dst_ref=outs[a].at[4 * px + 2 * py + pc],
                    send_sem=send_sems.at[7 * a + k], recv_sem=recv_sems.at[7 * a + k],
                    device_id=peers[k], device_id_type=MESH).wait_recv()
        for cp in sends:
            cp.wait_send()
        for cp in mine:
            cp.wait()

    hbm = pl.BlockSpec(memory_space=pl.ANY)
    return pl.pallas_call(
        body, name=name,
        in_specs=[hbm] * n, out_specs=[hbm] * n,
        out_shape=[_sds(p.shape, p.dtype) for p in partials],
        scratch_shapes=[pltpu.SemaphoreType.DMA((7 * n,)), pltpu.SemaphoreType.DMA((7 * n,)),
                        pltpu.SemaphoreType.DMA((n,))],
    )(*partials)


def _row_tile(rows, cols):
    if rows * cols <= 256 * 1024:
        return rows
    for t in (256, 128, 64, 32, 16, 8):
        if rows % t == 0:
            return t
    return rows


def _sum_devices(name, stacked):
    _, rows, cols = stacked.shape
    tile = _row_tile(rows, cols)

    def body(s_ref, o_ref):
        acc = s_ref[0].astype(F32)
        for d in range(1, N_DEV):
            acc = acc + s_ref[d].astype(F32)
        o_ref[...] = acc

    return pl.pallas_call(
        body, name=name, grid=(rows // tile,),
        in_specs=[pl.BlockSpec((N_DEV, tile, cols), lambda i: (0, i, 0))],
        out_specs=pl.BlockSpec((tile, cols), lambda i: (i, 0)),
        out_shape=_sds((rows, cols), F32),
        compiler_params=_params("parallel"),
    )(stacked)


def _adamw(name, w, g, m, v):
    rows, cols = w.shape
    tile = _row_tile(rows, cols)

    def body(w_ref, g_ref, m_ref, v_ref, d_ref, nm_ref, nv_ref):
        g = g_ref[...]
        nm = ADAM_B1 * m_ref[...] + (1.0 - ADAM_B1) * g
        nv = ADAM_B2 * v_ref[...] + (1.0 - ADAM_B2) * (g * g)
        m_hat = nm / (1.0 - ADAM_B1 ** ADAM_STEP)
        v_hat = nv / (1.0 - ADAM_B2 ** ADAM_STEP)
        d_ref[...] = -ADAM_LR * (m_hat / (jnp.sqrt(v_hat) + ADAM_EPS) + ADAM_WD * w_ref[...])
        nm_ref[...] = nm
        nv_ref[...] = nv

    blk = pl.BlockSpec((tile, cols), lambda i: (i, 0))
    return pl.pallas_call(
        body, name=name, grid=(rows // tile,),
        in_specs=[blk] * 4, out_specs=[blk] * 3,
        out_shape=[_sds((rows, cols), F32)] * 3,
        compiler_params=_params("parallel"),
    )(w, g, m, v)


SMALL = ("b_gate", "w_pool", "pool_scale", "ln1_g", "ln1_b", "conv_b", "ln2_g", "ln2_b")
TILE = 8 * LANE


def _pack(parts):
    tiles = []
    for p in parts:
        flat = p.reshape(-1)
        tiles.append(jnp.pad(flat, (0, -flat.size % TILE)).reshape(-1, LANE))
    return jnp.concatenate(tiles, axis=0)


def _unpack(packed, shapes):
    out, at = [], 0
    for shape in shapes:
        size = math.prod(shape)
        rows = -(-size // TILE) * 8
        out.append(packed[at:at + rows].reshape(-1)[:size].reshape(shape))
        at += rows
    return out


def _local_step(x, target, full, small):
    seq = x.shape[0]
    cos, sin = _rope_tables(seq)
    xb, q, k, v, u, g, kmean = _proj_in(x, full["w_in"], small["b_gate"], cos, sin)
    kmean = kmean.reshape(seq // MOBA_BLOCK, D_ATTN)
    o, lse = _attn_fwd(q, k, v, kmean)
    ya, yp, pooled, mixed, ypre, merged, xhat1, rstd1, h1, h1b = _mix(
        o, u, g, x, full["w_branch_attn"], full["w_branch_pool"], full["w_out"], small["w_pool"],
        small["pool_scale"], small["ln1_g"], small["ln1_b"])
    a, uf, act = _ffn_up(h1b, full["w_ffn_gate_t"], full["w_ffn_up_t"], full["conv_w"], small["conv_b"])
    dr2, dr2b, loss, dg2, db2 = _ffn_down(act, full["w_ffn_down"], h1, target, small["ln2_g"], small["ln2_b"])

    da, du, dwd, dwg, dwu, dconv = _ffn_bwd(dr2b, h1b, a, uf, full["w_ffn_down"], full["conv_w"], small["conv_b"])
    dr1, dr1b, dg1, db1 = _ln1_bwd(dr2, da, du, full["w_ffn_gate_t"], full["w_ffn_up_t"], xhat1, rstd1,
                                   small["ln1_g"])
    dzg, dya, dyp, do, dmixed, dpooled, dbg, dps = _mix_bwd(
        dr1b, ya, yp, g, mixed, full["w_out"], full["w_branch_attn"], full["w_branch_pool"], small["w_pool"],
        small["pool_scale"])
    dq, dk, dv = _attn_bwd(q, k, v, kmean, o, lse, do, cos, sin)
    grad_x, dz = _in_bwd(dq, dk, dv, dpooled, dzg, dr1, full["w_in"])

    whole = lambda width: pl.BlockSpec((seq, width), lambda *_: (0, 0))
    dw_in = _tn_matmul(
        "dw_in", xb, dz, (N_DEV, D_MODEL, D_ATTN), BF16, (N_DEV, 2),
        pl.BlockSpec((seq, 512), lambda n, m: (0, m)), pl.BlockSpec((seq, D_ATTN), lambda n, m: (0, n)),
        pl.BlockSpec((1, 512, D_ATTN), lambda n, m: (n, m, 0)))
    dw_out = _tn_matmul(
        "dw_out", merged, dr1b, (D_MODEL, D_MODEL), BF16, (4,),
        pl.BlockSpec((seq, 256), lambda m: (0, m)), whole(D_MODEL), pl.BlockSpec((256, D_MODEL), lambda m: (m, 0)))
    dw_ba = _tn_matmul(
        "dw_branch_attn", o, dya, (N_DEV, D_ATTN, LANE), BF16, (N_DEV,),
        whole(D_ATTN), pl.BlockSpec((seq, LANE), lambda n: (0, n)), pl.BlockSpec((1, D_ATTN, LANE), lambda n: (n, 0, 0)))
    dw_bp = _tn_matmul(
        "dw_branch_pool", ypre, dyp, (N_DEV, D_POOL, LANE), BF16, (N_DEV,),
        whole(D_POOL), pl.BlockSpec((seq, LANE), lambda n: (0, n)), pl.BlockSpec((1, D_POOL, LANE), lambda n: (n, 0, 0)))
    dw_pool = _tn_matmul(
        "dw_pool", pooled, dmixed, (len(POOL_WINDOWS), POOL_GROUP, POOL_GROUP), F32, (len(POOL_WINDOWS),),
        pl.BlockSpec((seq, POOL_GROUP), lambda n: (0, n)), pl.BlockSpec((seq, POOL_GROUP), lambda n: (0, n)),
        pl.BlockSpec((1, POOL_GROUP, POOL_GROUP), lambda n: (n, 0, 0)))

    stacked = lambda t: t.reshape(N_DEV, t.shape[0] // N_DEV, t.shape[1])
    big = {"w_in": dw_in, "w_branch_attn": dw_ba, "w_branch_pool": dw_bp, "w_out": stacked(dw_out),
           "w_ffn_gate_t": stacked(dwg), "w_ffn_up_t": stacked(dwu), "w_ffn_down": stacked(dwd)}
    little = {"b_gate": dbg, "w_pool": dw_pool, "pool_scale": dps, "ln1_g": dg1, "ln1_b": db1, "conv_b": dconv[3:4],
              "ln2_g": dg2, "ln2_b": db2, "conv_w": dconv[0:3]}
    return loss[0, 0], grad_x, big, little


BIG = ("w_in", "w_branch_attn", "w_branch_pool", "w_out", "w_ffn_gate_t", "w_ffn_up_t", "w_ffn_down")


def kernel(x, w_in, b_gate, w_branch_attn, w_pool, pool_scale, w_branch_pool, w_out, ln1_g, ln1_b, w_ffn_gate, w_ffn_up, conv_w, conv_b, w_ffn_down, ln2_g, ln2_b, loss_target, m_w_in, m_b_gate, m_w_branch_attn, m_w_pool, m_pool_scale, m_w_branch_pool, m_w_out, m_ln1_g, m_ln1_b, m_w_ffn_gate, m_w_ffn_up, m_conv_w, m_conv_b, m_w_ffn_down, m_ln2_g, m_ln2_b, v_w_in, v_b_gate, v_w_branch_attn, v_w_pool, v_pool_scale, v_w_branch_pool, v_w_out, v_ln1_g, v_ln1_b, v_w_ffn_gate, v_w_ffn_up, v_conv_w, v_conv_b, v_w_ffn_down, v_ln2_g, v_ln2_b):
    me = 4 * lax.axis_index("x") + 2 * lax.axis_index("y") + lax.axis_index("c")
    weights = dict(w_in=w_in, b_gate=b_gate, w_branch_attn=w_branch_attn, w_pool=w_pool, pool_scale=pool_scale,
                   w_branch_pool=w_branch_pool, w_out=w_out, ln1_g=ln1_g, ln1_b=ln1_b, w_ffn_gate=w_ffn_gate,
                   w_ffn_up=w_ffn_up, conv_w=conv_w, conv_b=conv_b, w_ffn_down=w_ffn_down, ln2_g=ln2_g, ln2_b=ln2_b)
    m_in = dict(w_in=m_w_in, b_gate=m_b_gate, w_branch_attn=m_w_branch_attn, w_pool=m_w_pool,
                pool_scale=m_pool_scale, w_branch_pool=m_w_branch_pool, w_out=m_w_out, ln1_g=m_ln1_g, ln1_b=m_ln1_b,
                w_ffn_gate=m_w_ffn_gate, w_ffn_up=m_w_ffn_up, conv_w=m_conv_w, conv_b=m_conv_b,
                w_ffn_down=m_w_ffn_down, ln2_g=m_ln2_g, ln2_b=m_ln2_b)
    v_in = dict(w_in=v_w_in, b_gate=v_b_gate, w_branch_attn=v_w_branch_attn, w_pool=v_w_pool,
                pool_scale=v_pool_scale, w_branch_pool=v_w_branch_pool, w_out=v_w_out, ln1_g=v_ln1_g, ln1_b=v_ln1_b,
                w_ffn_gate=v_w_ffn_gate, w_ffn_up=v_w_ffn_up, conv_w=v_conv_w, conv_b=v_conv_b,
                w_ffn_down=v_w_ffn_down, ln2_g=v_ln2_g, ln2_b=v_ln2_b)
    weights = {n: a[0] for n, a in weights.items()}
    m_in = {n: a[0] for n, a in m_in.items()}
    v_in = {n: a[0] for n, a in v_in.items()}

    shards = {"w_in": weights["w_in"].astype(BF16), "w_branch_attn": weights["w_branch_attn"].astype(BF16),
              "w_branch_pool": weights["w_branch_pool"].astype(BF16), "w_out": weights["w_out"].astype(BF16),
              "w_ffn_gate_t": weights["w_ffn_gate"].T.astype(BF16), "w_ffn_up_t": weights["w_ffn_up"].T.astype(BF16),
              "w_ffn_down": weights["w_ffn_down"].astype(BF16)}
    gathered = _all_gather("gather_weights", [shards[n] for n in BIG] + [weights["conv_w"]])
    gathered = dict(zip(BIG + ("conv_w",), gathered))
    columns = lambda t: jnp.transpose(t, (1, 0, 2)).reshape(t.shape[1], N_DEV * t.shape[2])
    rows = lambda t: t.reshape(N_DEV * t.shape[1], t.shape[2])
    full = {"w_in": gathered["w_in"], "w_branch_attn": columns(gathered["w_branch_attn"]),
            "w_branch_pool": columns(gathered["w_branch_pool"]), "w_out": rows(gathered["w_out"]),
            "w_ffn_gate_t": rows(gathered["w_ffn_gate_t"]), "w_ffn_up_t": rows(gathered["w_ffn_up_t"]),
            "w_ffn_down": rows(gathered["w_ffn_down"]), "conv_w": columns(gathered["conv_w"])}
    small = {"b_gate": weights["b_gate"][None], "w_pool": weights["w_pool"], "pool_scale": weights["pool_scale"][None],
             "ln1_g": weights["ln1_g"][None], "ln1_b": weights["ln1_b"][None], "conv_b": weights["conv_b"][None],
             "ln2_g": weights["ln2_g"][None], "ln2_b": weights["ln2_b"][None]}

    loss_part, grad_x, big, little = _local_step(x[0], loss_target[0], full, small)
    loss = lax.psum(loss_part, ("x", "y", "c"))

    landed = _scatter_blocks("scatter_grads", [big[n] for n in BIG])
    names = SMALL + ("conv_w",)
    all_small = _all_gather("gather_small_grads", [_pack([little[n] for n in names])])[0]

    grads = {n: _sum_devices("sum_" + n, t) for n, t in zip(BIG, landed)}
    grads["w_ffn_gate"] = grads.pop("w_ffn_gate_t").T
    grads["w_ffn_up"] = grads.pop("w_ffn_up_t").T
    small_sum = _sum_devices("sum_small", all_small)
    *small_grads, conv_w_grad = _unpack(small_sum, [weights[n].shape for n in SMALL] + [(3, D_FF)])
    grads.update(zip(SMALL, small_grads))
    grads["conv_w"] = lax.dynamic_slice(conv_w_grad, (0, me * FF_SHARD), (3, FF_SHARD))

    delta, new_m, new_v = {}, {}, {}
    for n in ("w_in", "w_branch_attn", "w_branch_pool", "w_out", "w_ffn_gate", "w_ffn_up", "w_ffn_down"):
        delta[n], new_m[n], new_v[n] = _adamw("adamw_" + n, weights[n], grads[n], m_in[n], v_in[n])
    flat = lambda d: _pack([d[n] for n in names])
    shapes = [weights[n].shape for n in names]
    for out, packed in zip((delta, new_m, new_v),
                           _adamw("adamw_small", flat(weights), flat(grads), flat(m_in), flat(v_in))):
        out.update(zip(names, _unpack(packed, shapes)))

    order = ("w_in", "b_gate", "w_branch_attn", "w_pool", "pool_scale", "w_branch_pool", "w_out", "ln1_g", "ln1_b",
             "w_ffn_gate", "w_ffn_up", "conv_w", "conv_b", "w_ffn_down", "ln2_g", "ln2_b")
    lead = lambda t: t[None]
    return (loss, lead(grad_x), *[lead(grads[n]) for n in order], *[lead(delta[n]) for n in order],
            *[lead(new_m[n]) for n in order], *[lead(new_v[n]) for n in order])
```

```python
import functools
import math

import jax
import jax.numpy as jnp
from jax import lax
from jax.experimental import pallas as pl
from jax.experimental.pallas import tpu as pltpu

F32 = jnp.float32
BF16 = jnp.bfloat16

D_MODEL = 1024
N_HEADS = 8
HEAD_DIM = 64
D_ATTN = N_HEADS * HEAD_DIM
MOBA_BLOCK = 256
MOBA_TOPK = 3
ROPE_THETA = 10000.0
POOL_WINDOWS = (2, 4, 8, 16)
POOL_GROUP = 128
D_POOL = len(POOL_WINDOWS) * POOL_GROUP
POOL_HALO = 16
D_FF = 2816
D_IN_PROJ = 3 * D_ATTN + D_POOL + 2 * D_MODEL
LN_EPS = 1e-5
ALPHA = 2.0 ** 0.25
NEG = -1e30
N_DEV = 8
FF_SHARD = D_FF // N_DEV

ADAM_LR = 0.001
ADAM_B1 = 0.9
ADAM_B2 = 0.999
ADAM_EPS = 1e-08
ADAM_WD = 0.01
ADAM_STEP = 10

TOK = 256
FF_CHUNK = 256
LANE = 128
VMEM_LIMIT = 56 * 1024 * 1024

MESH = pl.DeviceIdType.MESH
NT_DIMS = (((1,), (1,)), ((), ()))
TN_DIMS = (((0,), (0,)), ((), ()))


def _params(*sem):
    return pltpu.CompilerParams(dimension_semantics=sem or None, vmem_limit_bytes=VMEM_LIMIT)


def _full(shape):
    zeros = (0,) * len(shape)
    return pl.BlockSpec(shape, lambda *_: zeros, pipeline_mode=pl.Buffered(1))


def _rows(width, tile=TOK):
    return pl.BlockSpec((tile, width), lambda i: (i, 0))


def _sds(shape, dtype):
    return jax.ShapeDtypeStruct(shape, dtype)


def _dot(a, b):
    return jnp.dot(a, b, preferred_element_type=F32)


def _dot_nt(a, b):
    return lax.dot_general(a, b, NT_DIMS, preferred_element_type=F32)


def _dot_tn(a, b):
    return lax.dot_general(a, b, TN_DIMS, preferred_element_type=F32)


def _rope_tables(seq):
    half = HEAD_DIM // 2
    inv_freq = 1.0 / (ROPE_THETA ** (jnp.arange(half, dtype=F32) / half))
    ang = jnp.arange(seq, dtype=F32)[:, None] * inv_freq[None, :]
    cos, sin = jnp.cos(ang), jnp.sin(ang)
    return jnp.tile(cos, (1, 4)), jnp.tile(jnp.concatenate([-sin, sin], axis=1), (1, 2))


def _swap_halves(t):
    lane = lax.broadcasted_iota(jnp.int32, t.shape, 1)
    return jnp.where((lane % HEAD_DIM) < HEAD_DIM // 2, pltpu.roll(t, LANE - 32, 1), pltpu.roll(t, 32, 1))


def _rope(t, cos, sin):
    return t * cos + _swap_halves(t) * sin


def _rope_transposed(g, cos, sin):
    return g * cos + _swap_halves(g * sin)


def _ln_fwd(r, g, b):
    mu = jnp.mean(r, axis=-1, keepdims=True)
    xc = r - mu
    var = jnp.mean(xc * xc, axis=-1, keepdims=True)
    rstd = lax.rsqrt(var + LN_EPS)
    xhat = xc * rstd
    return xhat * g + b, xhat, rstd


def _ln_bwd(dy, xhat, rstd, g):
    dxh = dy * g
    m1 = jnp.mean(dxh, axis=-1, keepdims=True)
    m2 = jnp.mean(dxh * xhat, axis=-1, keepdims=True)
    return rstd * (dxh - m1 - xhat * m2)


def _gelu_parts(a):
    cdf = 0.5 * (1.0 + lax.erf(a * (1.0 / math.sqrt(2.0))))
    pdf = jnp.exp(-0.5 * a * a) * (1.0 / math.sqrt(2.0 * math.pi))
    return a * cdf, cdf + a * pdf


def _shift_down(a, k):
    row = lax.broadcasted_iota(jnp.int32, a.shape, 0)
    return jnp.where(row >= k, pltpu.roll(a, k, 0), 0.0)


def _shift_up(a, k):
    n = a.shape[0]
    row = lax.broadcasted_iota(jnp.int32, a.shape, 0)
    return jnp.where(row < n - k, pltpu.roll(a, n - k, 0), 0.0)


def _conv(a, cw, cb):
    return cw[2:3, :] * a + cw[1:2, :] * _shift_down(a, 1) + cw[0:1, :] * _shift_down(a, 2) + cb


def _pool_count(first_row, rows, window):
    t = first_row + lax.broadcasted_iota(jnp.int32, (rows, 1), 0)
    return jnp.minimum(t + 1, window).astype(F32)


def _proj_in(x, win, b_gate, cos, sin):
    seq = x.shape[0]
    nt = seq // TOK

    def body(x_ref, win_ref, bg_ref, cos_ref, sin_ref, xb_ref, q_ref, k_ref, v_ref, u_ref, g_ref, km_ref):
        xb = x_ref[...].astype(BF16)
        xb_ref[...] = xb
        cos_t, sin_t = cos_ref[...], sin_ref[...]
        for sec, out_ref in ((0, q_ref), (1, k_ref)):
            z = _dot(xb, win_ref[sec])
            for c in range(D_ATTN // LANE):
                cols = slice(LANE * c, LANE * (c + 1))
                out_ref[:, cols] = _rope(z[:, cols], cos_t, sin_t)
        km_ref[0] = jnp.mean(k_ref[...], axis=0, keepdims=True)
        v_ref[...] = _dot(xb, win_ref[2]).astype(BF16)
        u_ref[...] = _dot(xb, win_ref[3])
        for n in range(4):
            cols = slice(D_ATTN * n, D_ATTN * (n + 1))
            g_ref[:, cols] = jax.nn.sigmoid(_dot(xb, win_ref[4 + n]) + bg_ref[:, cols])

    return pl.pallas_call(
        body, name="proj_in", grid=(nt,),
        in_specs=[_rows(D_MODEL), _full(win.shape), _full((1, 2 * D_MODEL)), _rows(LANE), _rows(LANE)],
        out_specs=[_rows(D_MODEL), _rows(D_ATTN), _rows(D_ATTN), _rows(D_ATTN), _rows(D_POOL), _rows(2 * D_MODEL),
                   pl.BlockSpec((1, 1, D_ATTN), lambda i: (i, 0, 0))],
        out_shape=[_sds((seq, D_MODEL), BF16), _sds((seq, D_ATTN), F32), _sds((seq, D_ATTN), F32),
                   _sds((seq, D_ATTN), BF16), _sds((seq, D_POOL), F32), _sds((seq, 2 * D_MODEL), F32),
                   _sds((nt, 1, D_ATTN), F32)],
        compiler_params=_params("parallel"),
    )(x, win, b_gate, cos, sin)


def _masked_scores(qf, kb, km, i):
    scale = HEAD_DIM ** -0.5
    width = MOBA_BLOCK * (i + 1)
    s = _dot_nt((qf * scale).astype(BF16), kb[:width])
    pieces = []
    if i > MOBA_TOPK:
        gate = lax.dot_general(qf, km, NT_DIMS, precision=lax.Precision.HIGHEST, preferred_element_type=F32)
        cols = [gate[:, j:j + 1] for j in range(i)]
        for j in range(i):
            rank = jnp.zeros_like(cols[j])
            for jp in range(i):
                if jp != j:
                    above = (cols[jp] >= cols[j]) if jp < j else (cols[jp] > cols[j])
                    rank = rank + above.astype(F32)
            blk = slice(MOBA_BLOCK * j, MOBA_BLOCK * (j + 1))
            pieces.append(jnp.where(rank < MOBA_TOPK, s[:, blk], NEG))
    elif i > 0:
        pieces.append(s[:, :MOBA_BLOCK * i])
    row = lax.broadcasted_iota(jnp.int32, (MOBA_BLOCK, MOBA_BLOCK), 0)
    col = lax.broadcasted_iota(jnp.int32, (MOBA_BLOCK, MOBA_BLOCK), 1)
    pieces.append(jnp.where(col <= row, s[:, MOBA_BLOCK * i:], NEG))
    return pieces[0] if len(pieces) == 1 else jnp.concatenate(pieces, axis=1)


def _attn_fwd(q, k, v, kmean, tasks=()):
    seq = q.shape[0]
    nb = seq // MOBA_BLOCK
    pair = pl.BlockSpec((seq, LANE), lambda p: (0, p))

    def body(q_ref, k_ref, v_ref, km_ref, o_ref, lse_ref):
        for hh in range(LANE // HEAD_DIM):
            ls = slice(HEAD_DIM * hh, HEAD_DIM * (hh + 1))
            kb = k_ref[:, ls].astype(BF16)
            vb = v_ref[:, ls]
            km = km_ref[:, ls]
            for i in range(nb):
                rs = slice(MOBA_BLOCK * i, MOBA_BLOCK * (i + 1))
                sm = _masked_scores(q_ref[rs, ls], kb, km, i)
                m = jnp.max(sm, axis=1, keepdims=True)
                p = jnp.exp(sm - m)
                l = jnp.sum(p, axis=1, keepdims=True)
                o = _dot(p.astype(BF16), vb[:MOBA_BLOCK * (i + 1)])
                o_ref[rs, ls] = o / l
                lse_ref[rs, ls] = jnp.broadcast_to(m + jnp.log(l), (MOBA_BLOCK, HEAD_DIM))

    steps = D_ATTN // LANE
    t_operands, t_in_specs, t_out_shape, t_out_specs, t_sems = _task_args(tasks)
    outs = pl.pallas_call(
        _carry(body, tasks, 4, 2, 0, steps), name="attn_fwd", grid=(steps,),
        in_specs=[pair, pair, pair, pl.BlockSpec((nb, LANE), lambda p: (0, p))] + t_in_specs,
        out_specs=[pair, pair] + t_out_specs,
        out_shape=[_sds((seq, D_ATTN), F32), _sds((seq, D_ATTN), F32)] + t_out_shape,
        scratch_shapes=t_sems,
        compiler_params=_params("arbitrary"),
    )(q, k, v, kmean, *t_operands)
    return outs[:2], _task_results(tasks, outs[2:])


def _mix(o, u, g, x, wba, wbp, wout, w_pool, pool_scale, ln_g, ln_b):
    seq = x.shape[0]

    def body(o_ref, u_ref, uprev_ref, g_ref, x_ref, wba_ref, wbp_ref, wout_ref, wp_ref, ps_ref, lg_ref, lb_ref,
             ya_ref, yp_ref, pooled_ref, mixed_ref, ypre_ref, merged_ref, xhat_ref, rstd_ref, h_ref, hb_ref, ext):
        i = pl.program_id(0)
        ya = _dot(o_ref[...].astype(BF16), wba_ref[...])
        ucur = u_ref[...]
        ext[0:POOL_HALO, :] = jnp.where(i > 0, uprev_ref[...], 0.0)
        ext[POOL_HALO:, :] = ucur
        for grp, window in enumerate(POOL_WINDOWS):
            cols = slice(POOL_GROUP * grp, POOL_GROUP * (grp + 1))
            acc = ucur[:, cols]
            for kk in range(1, window):
                acc = acc + ext[pl.ds(POOL_HALO - kk, TOK), cols]
            pooled = acc / _pool_count(i * TOK, TOK, window) - ucur[:, cols]
            pooled_ref[:, cols] = pooled.astype(BF16)
            mixed_ref[:, cols] = _dot(pooled.astype(BF16), wp_ref[grp].astype(BF16))
        mixed = mixed_ref[...]
        ypre = (mixed * ps_ref[...]).astype(BF16)
        ypre_ref[...] = ypre
        yp = _dot(ypre, wbp_ref[...])
        ya_ref[...] = ya
        yp_ref[...] = yp
        merged = (g_ref[:, :D_MODEL] * ya + g_ref[:, D_MODEL:] * yp).astype(BF16)
        merged_ref[...] = merged
        r1 = ALPHA * x_ref[...] + _dot(merged, wout_ref[...])
        h, xhat, rstd = _ln_fwd(r1, lg_ref[...], lb_ref[...])
        xhat_ref[...] = xhat
        rstd_ref[...] = jnp.broadcast_to(rstd, (TOK, LANE))
        h_ref[...] = h
        hb_ref[...] = h.astype(BF16)

    halo = pl.BlockSpec((POOL_HALO, D_POOL), lambda i: (jnp.maximum(i * (TOK // POOL_HALO) - 1, 0), 0))
    return pl.pallas_call(
        body, name="mix", grid=(seq // TOK,),
        in_specs=[_rows(D_ATTN), _rows(D_POOL), halo, _rows(2 * D_MODEL), _rows(D_MODEL),
                  _full(wba.shape), _full(wbp.shape), _full(wout.shape), _full(w_pool.shape),
                  _full((1, D_POOL)), _full((1, D_MODEL)), _full((1, D_MODEL))],
        out_specs=[_rows(D_MODEL), _rows(D_MODEL), _rows(D_POOL), _rows(D_POOL), _rows(D_POOL), _rows(D_MODEL),
                   _rows(D_MODEL), _rows(LANE), _rows(D_MODEL), _rows(D_MODEL)],
        out_shape=[_sds((seq, D_MODEL), F32), _sds((seq, D_MODEL), F32), _sds((seq, D_POOL), BF16),
                   _sds((seq, D_POOL), F32), _sds((seq, D_POOL), BF16), _sds((seq, D_MODEL), BF16),
                   _sds((seq, D_MODEL), F32), _sds((seq, LANE), F32), _sds((seq, D_MODEL), F32),
                   _sds((seq, D_MODEL), BF16)],
        scratch_shapes=[pltpu.VMEM((TOK + POOL_HALO, D_POOL), F32)],
        compiler_params=_params("parallel"),
    )(o, u, u, g, x, wba, wbp, wout, w_pool, pool_scale, ln_g, ln_b)


def _ffn_up(hb, wgt, wut, conv_w, conv_b):
    seq = hb.shape[0]
    wblk = pl.BlockSpec((FF_CHUNK, D_MODEL), lambda c: (c, 0))
    cblk = lambda rows: pl.BlockSpec((rows, FF_CHUNK), lambda c: (0, c))
    oblk = pl.BlockSpec((seq, FF_CHUNK), lambda c: (0, c))

    def body(h_ref, wg_ref, wu_ref, cw_ref, cb_ref, a_ref, u_ref, act_ref):
        h = h_ref[...]
        a = _dot_nt(h, wg_ref[...])
        u = _dot_nt(h, wu_ref[...])
        a_ref[...] = a
        u_ref[...] = u
        gelu, _ = _gelu_parts(_conv(a, cw_ref[...], cb_ref[...]))
        act_ref[...] = (gelu * u).astype(BF16)

    return pl.pallas_call(
        body, name="ffn_up", grid=(D_FF // FF_CHUNK,),
        in_specs=[_full(hb.shape), wblk, wblk, cblk(3), cblk(1)],
        out_specs=[oblk, oblk, oblk],
        out_shape=[_sds((seq, D_FF), F32), _sds((seq, D_FF), F32), _sds((seq, D_FF), BF16)],
        compiler_params=_params("parallel"),
    )(hb, wgt, wut, conv_w, conv_b)


def _ffn_down(act, wd, h, target, ln_g, ln_b):
    seq = h.shape[0]

    def body(act_ref, wd_ref, h_ref, t_ref, lg_ref, lb_ref, dr_ref, drb_ref, loss_ref, dg_ref, db_ref):
        i = pl.program_id(0)

        @pl.when(i == 0)
        def _():
            loss_ref[...] = jnp.zeros_like(loss_ref)
            dg_ref[...] = jnp.zeros_like(dg_ref)
            db_ref[...] = jnp.zeros_like(db_ref)

        r2 = ALPHA * h_ref[...] + _dot(act_ref[...], wd_ref[...])
        y, xhat, rstd = _ln_fwd(r2, lg_ref[...], lb_ref[...])
        diff = y - t_ref[...]
        loss_ref[...] += jnp.sum(diff * diff) * (0.5 / D_MODEL)
        dy = diff * (1.0 / D_MODEL)
        dg_ref[...] += jnp.sum(dy * xhat, axis=0, keepdims=True)
        db_ref[...] += jnp.sum(dy, axis=0, keepdims=True)
        dr = _ln_bwd(dy, xhat, rstd, lg_ref[...])
        dr_ref[...] = dr
        drb_ref[...] = dr.astype(BF16)

    vec = pl.BlockSpec((1, D_MODEL), lambda i: (0, 0))
    return pl.pallas_call(
        body, name="ffn_down", grid=(seq // TOK,),
        in_specs=[_rows(D_FF), _full(wd.shape), _rows(D_MODEL), _rows(D_MODEL), _full((1, D_MODEL)), _full((1, D_MODEL))],
        out_specs=[_rows(D_MODEL), _rows(D_MODEL), pl.BlockSpec((8, LANE), lambda i: (0, 0)), vec, vec],
        out_shape=[_sds((seq, D_MODEL), F32), _sds((seq, D_MODEL), BF16), _sds((8, LANE), F32),
                   _sds((1, D_MODEL), F32), _sds((1, D_MODEL), F32)],
        compiler_params=_params("arbitrary"),
    )(act, wd, h, target, ln_g, ln_b)


def _ffn_bwd(drb, hb, a, u, wd, conv_w, conv_b):
    seq = hb.shape[0]
    wblk = pl.BlockSpec((FF_CHUNK, D_MODEL), lambda c: (c, 0))
    cblk = lambda rows: pl.BlockSpec((rows, FF_CHUNK), lambda c: (0, c))
    sblk = pl.BlockSpec((seq, FF_CHUNK), lambda c: (0, c))

    def body(dr_ref, h_ref, a_ref, u_ref, wd_ref, cw_ref, cb_ref, da_ref, du_ref, dwd_ref, dwg_ref, dwu_ref, dc_ref):
        dr = dr_ref[...]
        h = h_ref[...]
        a = a_ref[...]
        u = u_ref[...]
        cw = cw_ref[...]
        dact = _dot_nt(dr, wd_ref[...])
        gelu, dgelu = _gelu_parts(_conv(a, cw, cb_ref[...]))
        dwd_ref[...] = _dot_tn((gelu * u).astype(BF16), dr).astype(BF16)
        du = (dact * gelu).astype(BF16)
        dac = dact * u * dgelu
        da = (cw[2:3, :] * dac + cw[1:2, :] * _shift_up(dac, 1) + cw[0:1, :] * _shift_up(dac, 2)).astype(BF16)
        da_ref[...] = da
        du_ref[...] = du
        dwg_ref[...] = _dot_tn(da, h).astype(BF16)
        dwu_ref[...] = _dot_tn(du, h).astype(BF16)
        dc_ref[0:1, :] = jnp.sum(dac * _shift_down(a, 2), axis=0, keepdims=True)
        dc_ref[1:2, :] = jnp.sum(dac * _shift_down(a, 1), axis=0, keepdims=True)
        dc_ref[2:3, :] = jnp.sum(dac * a, axis=0, keepdims=True)
        dc_ref[3:4, :] = jnp.sum(dac, axis=0, keepdims=True)
        dc_ref[4:8, :] = jnp.zeros((4, FF_CHUNK), F32)

    return pl.pallas_call(
        body, name="ffn_bwd", grid=(D_FF // FF_CHUNK,),
        in_specs=[_full(drb.shape), _full(hb.shape), sblk, sblk, wblk, cblk(3), cblk(1)],
        out_specs=[sblk, sblk, wblk, wblk, wblk, cblk(8)],
        out_shape=[_sds((seq, D_FF), BF16), _sds((seq, D_FF), BF16), _sds((D_FF, D_MODEL), BF16),
                   _sds((D_FF, D_MODEL), BF16), _sds((D_FF, D_MODEL), BF16), _sds((8, D_FF), F32)],
        compiler_params=_params("parallel"),
    )(drb, hb, a, u, wd, conv_w, conv_b)


def _ln1_bwd(dr2, da, du, wgt, wut, xhat, rstd, ln_g, tasks=()):
    seq = dr2.shape[0]

    def body(dr2_ref, da_ref, du_ref, wg_ref, wu_ref, xhat_ref, rstd_ref, lg_ref, dr_ref, drb_ref, dg_ref, db_ref):
        @pl.when(pl.program_id(0) == 0)
        def _():
            dg_ref[...] = jnp.zeros_like(dg_ref)
            db_ref[...] = jnp.zeros_like(db_ref)

        dh = ALPHA * dr2_ref[...] + _dot(da_ref[...], wg_ref[...]) + _dot(du_ref[...], wu_ref[...])
        xhat = xhat_ref[...]
        dg_ref[...] += jnp.sum(dh * xhat, axis=0, keepdims=True)
        db_ref[...] += jnp.sum(dh, axis=0, keepdims=True)
        dr = _ln_bwd(dh, xhat, rstd_ref[:, 0:1], lg_ref[...])
        dr_ref[...] = dr
        drb_ref[...] = dr.astype(BF16)

    vec = pl.BlockSpec((1, D_MODEL), lambda i: (0, 0))
    steps = seq // TOK
    t_operands, t_in_specs, t_out_shape, t_out_specs, t_sems = _task_args(tasks)
    outs = pl.pallas_call(
        _carry(body, tasks, 8, 4, 0, steps), name="ln1_bwd", grid=(steps,),
        in_specs=[_rows(D_MODEL), _rows(D_FF), _rows(D_FF), _full(wgt.shape), _full(wut.shape), _rows(D_MODEL),
                  _rows(LANE), _full((1, D_MODEL))] + t_in_specs,
        out_specs=[_rows(D_MODEL), _rows(D_MODEL), vec, vec] + t_out_specs,
        out_shape=[_sds((seq, D_MODEL), F32), _sds((seq, D_MODEL), BF16), _sds((1, D_MODEL), F32),
                   _sds((1, D_MODEL), F32)] + t_out_shape,
        scratch_shapes=t_sems,
        compiler_params=_params("arbitrary"),
    )(dr2, da, du, wgt, wut, xhat, rstd, ln_g, *t_operands)
    return outs[:4], _task_results(tasks, outs[4:])


def _mix_bwd(drb, ya, yp, g, mixed, wout, wba, wbp, w_pool, pool_scale):
    seq = drb.shape[0]

    def body(dr_ref, ya_ref, yp_ref, g_ref, mixed_ref, wout_ref, wba_ref, wbp_ref, wp_ref, ps_ref,
             dzg_ref, dya_ref, dyp_ref, do_ref, dmixed_ref, dpooled_ref, dbg_ref, dps_ref):
        @pl.when(pl.program_id(0) == 0)
        def _():
            dbg_ref[...] = jnp.zeros_like(dbg_ref)
            dps_ref[...] = jnp.zeros_like(dps_ref)

        dmerged = _dot_nt(dr_ref[...], wout_ref[...])
        ga, gp = g_ref[:, :D_MODEL], g_ref[:, D_MODEL:]
        dzga = dmerged * ya_ref[...] * ga * (1.0 - ga)
        dzgp = dmerged * yp_ref[...] * gp * (1.0 - gp)
        dzg_ref[:, :D_MODEL] = dzga.astype(BF16)
        dzg_ref[:, D_MODEL:] = dzgp.astype(BF16)
        dbg_ref[:, :D_MODEL] += jnp.sum(dzga, axis=0, keepdims=True)
        dbg_ref[:, D_MODEL:] += jnp.sum(dzgp, axis=0, keepdims=True)
        dya = (dmerged * ga).astype(BF16)
        dyp = (dmerged * gp).astype(BF16)
        dya_ref[...] = dya
        dyp_ref[...] = dyp
        do_ref[...] = _dot_nt(dya, wba_ref[...])
        dypre = _dot_nt(dyp, wbp_ref[...])
        dps_ref[...] += jnp.sum(dypre * mixed_ref[...], axis=0, keepdims=True)
        dmixed = (dypre * ps_ref[...]).astype(BF16)
        dmixed_ref[...] = dmixed
        for grp in range(len(POOL_WINDOWS)):
            cols = slice(POOL_GROUP * grp, POOL_GROUP * (grp + 1))
            dpooled_ref[:, cols] = _dot_nt(dmixed[:, cols], wp_ref[grp].astype(BF16))

    return pl.pallas_call(
        body, name="mix_bwd", grid=(seq // TOK,),
        in_specs=[_rows(D_MODEL), _rows(D_MODEL), _rows(D_MODEL), _rows(2 * D_MODEL), _rows(D_POOL),
                  _full(wout.shape), _full(wba.shape), _full(wbp.shape), _full(w_pool.shape), _full((1, D_POOL))],
        out_specs=[_rows(2 * D_MODEL), _rows(D_MODEL), _rows(D_MODEL), _rows(D_ATTN), _rows(D_POOL), _rows(D_POOL),
                   pl.BlockSpec((1, 2 * D_MODEL), lambda i: (0, 0)), pl.BlockSpec((1, D_POOL), lambda i: (0, 0))],
        out_shape=[_sds((seq, 2 * D_MODEL), BF16), _sds((seq, D_MODEL), BF16), _sds((seq, D_MODEL), BF16),
                   _sds((seq, D_ATTN), F32), _sds((seq, D_POOL), BF16), _sds((seq, D_POOL), F32),
                   _sds((1, 2 * D_MODEL), F32), _sds((1, D_POOL), F32)],
        compiler_params=_params("arbitrary"),
    )(drb, ya, yp, g, mixed, wout, wba, wbp, w_pool, pool_scale)


def _attn_bwd(q, k, v, kmean, o, lse, do, cos, sin, tasks=()):
    seq = q.shape[0]
    nb = seq // MOBA_BLOCK
    pair = pl.BlockSpec((seq, LANE), lambda p: (0, p))
    table = pl.BlockSpec((seq, LANE), lambda p: (0, 0))

    def body(q_ref, k_ref, v_ref, km_ref, o_ref, lse_ref, do_ref, cos_ref, sin_ref, dq_ref, dk_ref, dv_ref,
             dq_acc, dk_acc, dv_acc):
        dk_acc[...] = jnp.zeros_like(dk_acc)
        dv_acc[...] = jnp.zeros_like(dv_acc)
        for hh in range(LANE // HEAD_DIM):
            ls = slice(HEAD_DIM * hh, HEAD_DIM * (hh + 1))
            kb = k_ref[:, ls].astype(BF16)
            vb = v_ref[:, ls]
            km = km_ref[:, ls]
            for i in range(nb):
                rs = slice(MOBA_BLOCK * i, MOBA_BLOCK * (i + 1))
                ks = slice(0, MOBA_BLOCK * (i + 1))
                qf = q_ref[rs, ls]
                p = jnp.exp(_masked_scores(qf, kb, km, i) - lse_ref[rs, ls][:, 0:1])
                dob = do_ref[rs, ls]
                delta = jnp.sum(dob * o_ref[rs, ls], axis=1, keepdims=True)
                dob16 = dob.astype(BF16)
                dp = _dot_nt(dob16, vb[ks])
                ds = (p * (dp - delta) * (HEAD_DIM ** -0.5)).astype(BF16)
                dq_acc[rs, ls] = _dot(ds, kb[ks])
                dk_acc[ks, ls] += _dot_tn(ds, qf.astype(BF16))
                dv_acc[ks, ls] += _dot_tn(p.astype(BF16), dob16)
        cos_t, sin_t = cos_ref[...], sin_ref[...]
        dq_ref[...] = _rope_transposed(dq_acc[...], cos_t, sin_t).astype(BF16)
        dk_ref[...] = _rope_transposed(dk_acc[...], cos_t, sin_t).astype(BF16)
        dv_ref[...] = dv_acc[...].astype(BF16)

    steps = D_ATTN // LANE
    t_operands, t_in_specs, t_out_shape, t_out_specs, t_sems = _task_args(tasks)
    outs = pl.pallas_call(
        _carry(body, tasks, 9, 3, 3, steps), name="attn_bwd", grid=(steps,),
        in_specs=[pair, pair, pair, pl.BlockSpec((nb, LANE), lambda p: (0, p)), pair, pair, pair, table, table]
        + t_in_specs,
        out_specs=[pair, pair, pair] + t_out_specs,
        out_shape=[_sds((seq, D_ATTN), BF16)] * 3 + t_out_shape,
        scratch_shapes=[pltpu.VMEM((seq, LANE), F32)] * 3 + t_sems,
        compiler_params=_params("arbitrary"),
    )(q, k, v, kmean, o, lse, do, cos, sin, *t_operands)
    return outs[:3], _task_results(tasks, outs[3:])


def _in_bwd(dq, dk, dv, dpooled, dzg, dr1, win):
    seq = dr1.shape[0]
    nt = seq // TOK

    def body(dq_ref, dk_ref, dv_ref, dp_ref, dpnext_ref, dzg_ref, dr_ref, win_ref, dx_ref, dz_ref, ext):
        i = pl.program_id(0)
        dp = dp_ref[...]
        dpn = jnp.where(i < nt - 1, dpnext_ref[...], 0.0)
        for grp, window in enumerate(POOL_WINDOWS):
            cols = slice(POOL_GROUP * grp, POOL_GROUP * (grp + 1))
            ext[0:TOK, cols] = dp[:, cols] / _pool_count(i * TOK, TOK, window)
            ext[TOK:, cols] = dpn[:, cols] / _pool_count((i + 1) * TOK, POOL_HALO, window)
        for grp, window in enumerate(POOL_WINDOWS):
            cols = slice(POOL_GROUP * grp, POOL_GROUP * (grp + 1))
            acc = ext[0:TOK, cols] - dp[:, cols]
            for kk in range(1, window):
                acc = acc + ext[pl.ds(kk, TOK), cols]
            dz_ref[:, 3 * D_ATTN + POOL_GROUP * grp:3 * D_ATTN + POOL_GROUP * (grp + 1)] = acc.astype(BF16)
        dz_ref[:, 0:D_ATTN] = dq_ref[...]
        dz_ref[:, D_ATTN:2 * D_ATTN] = dk_ref[...]
        dz_ref[:, 2 * D_ATTN:3 * D_ATTN] = dv_ref[...]
        dz_ref[:, 3 * D_ATTN + D_POOL:] = dzg_ref[...]
        dx = ALPHA * dr_ref[...]
        for n in range(N_DEV):
            dx = dx + _dot_nt(dz_ref[:, D_ATTN * n:D_ATTN * (n + 1)], win_ref[n])
        dx_ref[...] = dx

    halo = pl.BlockSpec((POOL_HALO, D_POOL),
                        lambda i: (jnp.minimum((i + 1) * (TOK // POOL_HALO), seq // POOL_HALO - 1), 0))
    return pl.pallas_call(
        body, name="in_bwd", grid=(nt,),
        in_specs=[_rows(D_ATTN), _rows(D_ATTN), _rows(D_ATTN), _rows(D_POOL), halo, _rows(2 * D_MODEL),
                  _rows(D_MODEL), _full(win.shape)],
        out_specs=[_rows(D_MODEL), _rows(D_IN_PROJ)],
        out_shape=[_sds((seq, D_MODEL), F32), _sds((seq, D_IN_PROJ), BF16)],
        scratch_shapes=[pltpu.VMEM((TOK + POOL_HALO, D_POOL), F32)],
        compiler_params=_params("parallel"),
    )(dq, dk, dv, dpooled, dpooled, dzg, dr1, win)


def _tn_matmul(name, a, b, out_shape, out_dtype, grid, a_spec, b_spec, o_spec):
    def body(a_ref, b_ref, o_ref):
        r = _dot_tn(a_ref[...].astype(BF16), b_ref[...].astype(BF16))
        o_ref[...] = r.reshape(o_ref.shape).astype(o_ref.dtype)

    return pl.pallas_call(
        body, name=name, grid=grid, in_specs=[a_spec, b_spec], out_specs=o_spec,
        out_shape=_sds(out_shape, out_dtype),
        compiler_params=_params(*(("parallel",) * len(grid))),
    )(a, b)


def _place():
    return lax.axis_index("x"), lax.axis_index("y"), lax.axis_index("c")


def _other_chips(x, y):
    return [(1 - x, y), (x, 1 - y), (1 - x, 1 - y)]


DMA_SEMS = pltpu.SemaphoreType.DMA


class _AllGather:
    def __init__(self, shards):
        self.operands = list(shards)
        self.n = len(shards)
        self.out_shape = [_sds((N_DEV, *s.shape), s.dtype) for s in shards]
        self.sems = [DMA_SEMS((7 * self.n,)), DMA_SEMS((7 * self.n,)), DMA_SEMS((self.n,))]

    def _copy(self, refs, a, k, block, to, from_input=False):
        ins, outs, (send_sems, recv_sems, _) = refs
        px, py, pc = block
        dst = outs[a].at[4 * px + 2 * py + pc]
        return pltpu.make_async_remote_copy(
            src_ref=ins[a] if from_input else dst, dst_ref=dst,
            send_sem=send_sems.at[7 * a + k], recv_sem=recv_sems.at[7 * a + k],
            device_id=to, device_id_type=MESH)

    def _local(self, refs, a):
        ins, outs, (_, _, local_sems) = refs
        x, y, c = _place()
        return pltpu.make_async_copy(ins[a], outs[a].at[4 * x + 2 * y + c], local_sems.at[a])

    def start(self, refs):
        x, y, c = _place()
        for a in range(self.n):
            self._local(refs, a).start()
        for a in range(self.n):
            self._copy(refs, a, 0, (x, y, c), (x, y, 1 - c), True).start()
            for j, chip in enumerate(_other_chips(x, y)):
                self._copy(refs, a, 1 + j, (x, y, c), (*chip, c), True).start()

    def middle(self, refs):
        x, y, c = _place()
        for j, chip in enumerate(_other_chips(x, y)):
            for a in range(self.n):
                self._copy(refs, a, 1 + j, (*chip, c), (x, y, c)).wait_recv()
                self._copy(refs, a, 4 + j, (*chip, c), (x, y, 1 - c)).start()

    def finish(self, refs):
        x, y, c = _place()
        me, sibling = (x, y, c), (x, y, 1 - c)
        chips = _other_chips(x, y)
        for a in range(self.n):
            self._copy(refs, a, 0, sibling, me).wait_recv()
            for j, chip in enumerate(chips):
                self._copy(refs, a, 4 + j, (*chip, 1 - c), me).wait_recv()
        for a in range(self.n):
            self._copy(refs, a, 0, me, sibling, True).wait_send()
            for j, chip in enumerate(chips):
                self._copy(refs, a, 1 + j, me, (*chip, c), True).wait_send()
                self._copy(refs, a, 4 + j, (*chip, c), sibling).wait_send()
            self._local(refs, a).wait()


class _SiblingSend:
    def __init__(self, partials):
        self.operands = list(partials)
        self.n = len(partials)
        self.out_shape = [_sds((4, *p.shape[1:]), p.dtype) for p in partials]
        self.sems = [DMA_SEMS((4 * self.n,)), DMA_SEMS((4 * self.n,))]

    def _copy(self, refs, a, q):
        ins, outs, (send_sems, recv_sems) = refs
        x, y, c = _place()
        return pltpu.make_async_remote_copy(
            src_ref=ins[a].at[2 * q + 1 - c], dst_ref=outs[a].at[q],
            send_sem=send_sems.at[4 * a + q], recv_sem=recv_sems.at[4 * a + q],
            device_id=(x, y, 1 - c), device_id_type=MESH)

    def start(self, refs):
        for a in range(self.n):
            for q in range(4):
                self._copy(refs, a, q).start()

    def middle(self, refs):
        pass

    def finish(self, refs):
        for a in range(self.n):
            for q in range(4):
                self._copy(refs, a, q).wait()


class _ChipScatter:
    def __init__(self, chip_partials):
        self.operands = list(chip_partials)
        self.n = len(chip_partials)
        self.out_shape = [_sds(p.shape, p.dtype) for p in chip_partials]
        self.sems = [DMA_SEMS((3 * self.n,)), DMA_SEMS((3 * self.n,)), DMA_SEMS((self.n,))]

    def _copy(self, refs, a, k, arrival=False):
        ins, outs, (send_sems, recv_sems, _) = refs
        x, y, c = _place()
        px, py = _other_chips(x, y)[k]
        mine, theirs = 2 * x + y, 2 * px + py
        return pltpu.make_async_remote_copy(
            src_ref=ins[a].at[mine if arrival else theirs], dst_ref=outs[a].at[theirs if arrival else mine],
            send_sem=send_sems.at[3 * a + k], recv_sem=recv_sems.at[3 * a + k],
            device_id=(px, py, c), device_id_type=MESH)

    def _local(self, refs, a):
        ins, outs, (_, _, local_sems) = refs
        x, y, _ = _place()
        return pltpu.make_async_copy(ins[a].at[2 * x + y], outs[a].at[2 * x + y], local_sems.at[a])

    def start(self, refs):
        for a in range(self.n):
            self._local(refs, a).start()
            for k in range(3):
                self._copy(refs, a, k).start()

    def middle(self, refs):
        pass

    def finish(self, refs):
        for a in range(self.n):
            for k in range(3):
                self._copy(refs, a, k, arrival=True).wait_recv()
        for a in range(self.n):
            for k in range(3):
                self._copy(refs, a, k).wait_send()
            self._local(refs, a).wait()


def _task_args(tasks):
    hbm = pl.BlockSpec(memory_space=pl.ANY)
    operands = [o for t in tasks for o in t.operands]
    out_shape = [s for t in tasks for s in t.out_shape]
    sems = [s for t in tasks for s in t.sems]
    return operands, [hbm] * len(operands), out_shape, [hbm] * len(out_shape), sems


def _task_refs(tasks, ins, outs, sems):
    per_task = []
    for t in tasks:
        ni, no, ns = len(t.operands), len(t.out_shape), len(t.sems)
        per_task.append((ins[:ni], outs[:no], sems[:ns]))
        ins, outs, sems = ins[ni:], outs[no:], sems[ns:]
    return per_task


def _task_results(tasks, outs):
    res = []
    for t in tasks:
        res.append(list(outs[:len(t.out_shape)]))
        outs = outs[len(t.out_shape):]
    return res


def _carry(body, tasks, n_in, n_out, n_scratch, steps):
    if not tasks:
        return body
    t_in = sum(len(t.operands) for t in tasks)
    t_out = sum(len(t.out_shape) for t in tasks)

    def wrapped(*refs):
        ins, refs = refs[:n_in], refs[n_in:]
        t_ins, refs = refs[:t_in], refs[t_in:]
        outs, refs = refs[:n_out], refs[n_out:]
        t_outs, refs = refs[:t_out], refs[t_out:]
        scratch, t_sems = refs[:n_scratch], refs[n_scratch:]
        per_task = _task_refs(tasks, t_ins, t_outs, t_sems)
        step = pl.program_id(0)

        @pl.when(step == 0)
        def _():
            for t, r in zip(tasks, per_task):
                t.start(r)

        @pl.when(step == steps - 1)
        def _():
            for t, r in zip(tasks, per_task):
                t.middle(r)

        body(*ins, *outs, *scratch)

        @pl.when(step == steps - 1)
        def _():
            for t, r in zip(tasks, per_task):
                t.finish(r)

    return wrapped


def _exchange(name, tasks):
    operands, in_specs, out_shape, out_specs, sems = _task_args(tasks)

    def body(*refs):
        ni, no = len(operands), len(out_shape)
        per_task = _task_refs(tasks, refs[:ni], refs[ni:ni + no], refs[ni + no:])
        for phase in ("start", "middle", "finish"):
            for t, r in zip(tasks, per_task):
                getattr(t, phase)(r)

    outs = pl.pallas_call(body, name=name, in_specs=in_specs, out_specs=out_specs, out_shape=out_shape,
                          scratch_shapes=sems)(*operands)
    return _task_results(tasks, outs)


def _row_tile(rows, cols):
    if rows * cols <= 256 * 1024:
        return rows
    for t in (256, 128, 64, 32, 16, 8):
        if rows % t == 0:
            return t
    return rows


def _pair_sum(name, partials, from_sibling):
    _, rows, cols = partials.shape
    tile = _row_tile(rows, cols)

    def body(p_ref, s_ref, o_ref):
        mine = jnp.where(lax.axis_index("c") == 0, p_ref[0, 0].astype(F32), p_ref[0, 1].astype(F32))
        o_ref[0] = (mine + s_ref[0].astype(F32)).astype(o_ref.dtype)

    blk = pl.BlockSpec((1, tile, cols), lambda q, i: (q, i, 0))
    return pl.pallas_call(
        body, name=name, grid=(4, rows // tile),
        in_specs=[pl.BlockSpec((1, 2, tile, cols), lambda q, i: (q, 0, i, 0)), blk],
        out_specs=blk, out_shape=_sds(from_sibling.shape, from_sibling.dtype),
        compiler_params=_params("parallel", "parallel"),
    )(partials.reshape(4, 2, rows, cols), from_sibling)


def _sum_leading(name, stacked):
    parts, rows, cols = stacked.shape
    tile = _row_tile(rows, cols)

    def body(s_ref, o_ref):
        acc = s_ref[0].astype(F32)
        for d in range(1, parts):
            acc = acc + s_ref[d].astype(F32)
        o_ref[...] = acc

    return pl.pallas_call(
        body, name=name, grid=(rows // tile,),
        in_specs=[pl.BlockSpec((parts, tile, cols), lambda i: (0, i, 0))],
        out_specs=pl.BlockSpec((tile, cols), lambda i: (i, 0)),
        out_shape=_sds((rows, cols), F32),
        compiler_params=_params("parallel"),
    )(stacked)


def _adamw(name, w, g, m, v):
    rows, cols = w.shape
    tile = _row_tile(rows, cols)

    def body(w_ref, g_ref, m_ref, v_ref, d_ref, nm_ref, nv_ref):
        g = g_ref[...]
        nm = ADAM_B1 * m_ref[...] + (1.0 - ADAM_B1) * g
        nv = ADAM_B2 * v_ref[...] + (1.0 - ADAM_B2) * (g * g)
        m_hat = nm / (1.0 - ADAM_B1 ** ADAM_STEP)
        v_hat = nv / (1.0 - ADAM_B2 ** ADAM_STEP)
        d_ref[...] = -ADAM_LR * (m_hat / (jnp.sqrt(v_hat) + ADAM_EPS) + ADAM_WD * w_ref[...])
        nm_ref[...] = nm
        nv_ref[...] = nv

    blk = pl.BlockSpec((tile, cols), lambda i: (i, 0))
    return pl.pallas_call(
        body, name=name, grid=(rows // tile,),
        in_specs=[blk] * 4, out_specs=[blk] * 3,
        out_shape=[_sds((rows, cols), F32)] * 3,
        compiler_params=_params("parallel"),
    )(w, g, m, v)


SMALL = ("b_gate", "w_pool", "pool_scale", "ln1_g", "ln1_b", "conv_b", "ln2_g", "ln2_b")
TILE = 8 * LANE


def _pack(parts):
    tiles = []
    for p in parts:
        flat = p.reshape(-1)
        tiles.append(jnp.pad(flat, (0, -flat.size % TILE)).reshape(-1, LANE))
    return jnp.concatenate(tiles, axis=0)


def _unpack(packed, shapes):
    out, at = [], 0
    for shape in shapes:
        size = math.prod(shape)
        rows = -(-size // TILE) * 8
        out.append(packed[at:at + rows].reshape(-1)[:size].reshape(shape))
        at += rows
    return out


FIRST = ("w_in", "w_branch_attn", "w_branch_pool", "w_out")
FFN = ("w_ffn_gate_t", "w_ffn_up_t", "w_ffn_down")
BIG = FIRST + FFN


def _columns(t):
    return jnp.transpose(t, (1, 0, 2)).reshape(t.shape[1], N_DEV * t.shape[2])


def _row_blocks(t):
    return t.reshape(N_DEV * t.shape[1], t.shape[2])


def _by_owner(t):
    return t.reshape(N_DEV, t.shape[0] // N_DEV, t.shape[1])


def _reduce_halves(names, partials, from_sibling):
    return [_pair_sum("pair_sum_" + n, p, s) for n, p, s in zip(names, partials, from_sibling)]


def _local_step(x, target, shards, small):
    seq = x.shape[0]
    cos, sin = _rope_tables(seq)
    (first,) = _exchange("gather_first", [_AllGather([shards[n] for n in FIRST] + [shards["conv_w"]])])
    first = dict(zip(FIRST + ("conv_w",), first))
    full = {"w_in": first["w_in"], "w_branch_attn": _columns(first["w_branch_attn"]),
            "w_branch_pool": _columns(first["w_branch_pool"]), "w_out": _row_blocks(first["w_out"]),
            "conv_w": _columns(first["conv_w"])}
    xb, q, k, v, u, g, kmean = _proj_in(x, full["w_in"], small["b_gate"], cos, sin)
    kmean = kmean.reshape(seq // MOBA_BLOCK, D_ATTN)
    (o, lse), (ffn,) = _attn_fwd(q, k, v, kmean, tasks=[_AllGather([shards[n] for n in FFN])])
    full.update({n: _row_blocks(t) for n, t in zip(FFN, ffn)})
    ya, yp, pooled, mixed, ypre, merged, xhat1, rstd1, h1, h1b = _mix(
        o, u, g, x, full["w_branch_attn"], full["w_branch_pool"], full["w_out"], small["w_pool"],
        small["pool_scale"], small["ln1_g"], small["ln1_b"])
    a, uf, act = _ffn_up(h1b, full["w_ffn_gate_t"], full["w_ffn_up_t"], full["conv_w"], small["conv_b"])
    dr2, dr2b, loss, dg2, db2 = _ffn_down(act, full["w_ffn_down"], h1, target, small["ln2_g"], small["ln2_b"])

    da, du, dwd, dwg, dwu, dconv = _ffn_bwd(dr2b, h1b, a, uf, full["w_ffn_down"], full["conv_w"], small["conv_b"])
    ffn_partials = [_by_owner(dwg), _by_owner(dwu), _by_owner(dwd)]
    (dr1, dr1b, dg1, db1), (ffn_sibling,) = _ln1_bwd(
        dr2, da, du, full["w_ffn_gate_t"], full["w_ffn_up_t"], xhat1, rstd1, small["ln1_g"],
        tasks=[_SiblingSend(ffn_partials)])
    ffn_chip = _reduce_halves(FFN, ffn_partials, ffn_sibling)
    dzg, dya, dyp, do, dmixed, dpooled, dbg, dps = _mix_bwd(
        dr1b, ya, yp, g, mixed, full["w_out"], full["w_branch_attn"], full["w_branch_pool"], small["w_pool"],
        small["pool_scale"])
    (dq, dk, dv), (ffn_landed,) = _attn_bwd(q, k, v, kmean, o, lse, do, cos, sin, tasks=[_ChipScatter(ffn_chip)])
    grad_x, dz = _in_bwd(dq, dk, dv, dpooled, dzg, dr1, full["w_in"])

    whole = lambda width: pl.BlockSpec((seq, width), lambda *_: (0, 0))
    dw_in = _tn_matmul(
        "dw_in", xb, dz, (N_DEV, D_MODEL, D_ATTN), BF16, (N_DEV, 2),
        pl.BlockSpec((seq, 512), lambda n, m: (0, m)), pl.BlockSpec((seq, D_ATTN), lambda n, m: (0, n)),
        pl.BlockSpec((1, 512, D_ATTN), lambda n, m: (n, m, 0)))
    dw_out = _tn_matmul(
        "dw_out", merged, dr1b, (D_MODEL, D_MODEL), BF16, (4,),
        pl.BlockSpec((seq, 256), lambda m: (0, m)), whole(D_MODEL), pl.BlockSpec((256, D_MODEL), lambda m: (m, 0)))
    dw_ba = _tn_matmul(
        "dw_branch_attn", o, dya, (N_DEV, D_ATTN, LANE), BF16, (N_DEV,),
        whole(D_ATTN), pl.BlockSpec((seq, LANE), lambda n: (0, n)), pl.BlockSpec((1, D_ATTN, LANE), lambda n: (n, 0, 0)))
    dw_bp = _tn_matmul(
        "dw_branch_pool", ypre, dyp, (N_DEV, D_POOL, LANE), BF16, (N_DEV,),
        whole(D_POOL), pl.BlockSpec((seq, LANE), lambda n: (0, n)), pl.BlockSpec((1, D_POOL, LANE), lambda n: (n, 0, 0)))
    dw_pool = _tn_matmul(
        "dw_pool", pooled, dmixed, (len(POOL_WINDOWS), POOL_GROUP, POOL_GROUP), F32, (len(POOL_WINDOWS),),
        pl.BlockSpec((seq, POOL_GROUP), lambda n: (0, n)), pl.BlockSpec((seq, POOL_GROUP), lambda n: (0, n)),
        pl.BlockSpec((1, POOL_GROUP, POOL_GROUP), lambda n: (n, 0, 0)))

    rest = [dw_in, dw_ba, dw_bp, _by_owner(dw_out)]
    little = {"b_gate": dbg, "w_pool": dw_pool, "pool_scale": dps, "ln1_g": dg1, "ln1_b": db1, "conv_b": dconv[3:4],
              "ln2_g": dg2, "ln2_b": db2, "conv_w": dconv[0:3]}
    return loss[0, 0], grad_x, ffn_landed, rest, little


def kernel(x, w_in, b_gate, w_branch_attn, w_pool, pool_scale, w_branch_pool, w_out, ln1_g, ln1_b, w_ffn_gate, w_ffn_up, conv_w, conv_b, w_ffn_down, ln2_g, ln2_b, loss_target, m_w_in, m_b_gate, m_w_branch_attn, m_w_pool, m_pool_scale, m_w_branch_pool, m_w_out, m_ln1_g, m_ln1_b, m_w_ffn_gate, m_w_ffn_up, m_conv_w, m_conv_b, m_w_ffn_down, m_ln2_g, m_ln2_b, v_w_in, v_b_gate, v_w_branch_attn, v_w_pool, v_pool_scale, v_w_branch_pool, v_w_out, v_ln1_g, v_ln1_b, v_w_ffn_gate, v_w_ffn_up, v_conv_w, v_conv_b, v_w_ffn_down, v_ln2_g, v_ln2_b):
    me = 4 * lax.axis_index("x") + 2 * lax.axis_index("y") + lax.axis_index("c")
    weights = dict(w_in=w_in, b_gate=b_gate, w_branch_attn=w_branch_attn, w_pool=w_pool, pool_scale=pool_scale,
                   w_branch_pool=w_branch_pool, w_out=w_out, ln1_g=ln1_g, ln1_b=ln1_b, w_ffn_gate=w_ffn_gate,
                   w_ffn_up=w_ffn_up, conv_w=conv_w, conv_b=conv_b, w_ffn_down=w_ffn_down, ln2_g=ln2_g, ln2_b=ln2_b)
    m_in = dict(w_in=m_w_in, b_gate=m_b_gate, w_branch_attn=m_w_branch_attn, w_pool=m_w_pool,
                pool_scale=m_pool_scale, w_branch_pool=m_w_branch_pool, w_out=m_w_out, ln1_g=m_ln1_g, ln1_b=m_ln1_b,
                w_ffn_gate=m_w_ffn_gate, w_ffn_up=m_w_ffn_up, conv_w=m_conv_w, conv_b=m_conv_b,
                w_ffn_down=m_w_ffn_down, ln2_g=m_ln2_g, ln2_b=m_ln2_b)
    v_in = dict(w_in=v_w_in, b_gate=v_b_gate, w_branch_attn=v_w_branch_attn, w_pool=v_w_pool,
                pool_scale=v_pool_scale, w_branch_pool=v_w_branch_pool, w_out=v_w_out, ln1_g=v_ln1_g, ln1_b=v_ln1_b,
                w_ffn_gate=v_w_ffn_gate, w_ffn_up=v_w_ffn_up, conv_w=v_conv_w, conv_b=v_conv_b,
                w_ffn_down=v_w_ffn_down, ln2_g=v_ln2_g, ln2_b=v_ln2_b)
    weights = {n: a[0] for n, a in weights.items()}
    m_in = {n: a[0] for n, a in m_in.items()}
    v_in = {n: a[0] for n, a in v_in.items()}

    shards = {"w_in": weights["w_in"].astype(BF16), "w_branch_attn": weights["w_branch_attn"].astype(BF16),
              "w_branch_pool": weights["w_branch_pool"].astype(BF16), "w_out": weights["w_out"].astype(BF16),
              "w_ffn_gate_t": weights["w_ffn_gate"].T.astype(BF16), "w_ffn_up_t": weights["w_ffn_up"].T.astype(BF16),
              "w_ffn_down": weights["w_ffn_down"].astype(BF16), "conv_w": weights["conv_w"]}
    small = {"b_gate": weights["b_gate"][None], "w_pool": weights["w_pool"], "pool_scale": weights["pool_scale"][None],
             "ln1_g": weights["ln1_g"][None], "ln1_b": weights["ln1_b"][None], "conv_b": weights["conv_b"][None],
             "ln2_g": weights["ln2_g"][None], "ln2_b": weights["ln2_b"][None]}

    loss_part, grad_x, ffn_landed, rest, little = _local_step(x[0], loss_target[0], shards, small)
    loss = lax.psum(loss_part, ("x", "y", "c"))

    (rest_sibling,) = _exchange("sibling_grads", [_SiblingSend(rest)])
    rest_chip = _reduce_halves(FIRST, rest, rest_sibling)
    names = SMALL + ("conv_w",)
    rest_landed, (all_small,) = _exchange(
        "scatter_grads", [_ChipScatter(rest_chip), _AllGather([_pack([little[n] for n in names])])])

    grads = {n: _sum_leading("sum_" + n, t) for n, t in zip(BIG, rest_landed + ffn_landed)}
    grads["w_ffn_gate"] = grads.pop("w_ffn_gate_t").T
    grads["w_ffn_up"] = grads.pop("w_ffn_up_t").T
    small_sum = _sum_leading("sum_small", all_small)
    *small_grads, conv_w_grad = _unpack(small_sum, [weights[n].shape for n in SMALL] + [(3, D_FF)])
    grads.update(zip(SMALL, small_grads))
    grads["conv_w"] = lax.dynamic_slice(conv_w_grad, (0, me * FF_SHARD), (3, FF_SHARD))

    delta, new_m, new_v = {}, {}, {}
    for n in ("w_in", "w_branch_attn", "w_branch_pool", "w_out", "w_ffn_gate", "w_ffn_up", "w_ffn_down"):
        delta[n], new_m[n], new_v[n] = _adamw("adamw_" + n, weights[n], grads[n], m_in[n], v_in[n])
    flat = lambda d: _pack([d[n] for n in names])
    shapes = [weights[n].shape for n in names]
    for out, packed in zip((delta, new_m, new_v),
                           _adamw("adamw_small", flat(weights), flat(grads), flat(m_in), flat(v_in))):
        out.update(zip(names, _unpack(packed, shapes)))

    order = ("w_in", "b_gate", "w_branch_attn", "w_pool", "pool_scale", "w_branch_pool", "w_out", "ln1_g", "ln1_b",
             "w_ffn_gate", "w_ffn_up", "conv_w", "conv_b", "w_ffn_down", "ln2_g", "ln2_b")
    lead = lambda t: t[None]
    return (loss, lead(grad_x), *[lead(grads[n]) for n in order], *[lead(delta[n]) for n in order],
            *[lead(new_m[n]) for n in order], *[lead(new_v[n]) for n in order])
```

```python
import functools
import math

import jax
import jax.numpy as jnp
from jax import lax
from jax.experimental import pallas as pl
from jax.experimental.pallas import tpu as pltpu

F32 = jnp.float32
BF16 = jnp.bfloat16

D_MODEL = 1024
N_HEADS = 8
HEAD_DIM = 64
D_ATTN = N_HEADS * HEAD_DIM
MOBA_BLOCK = 256
MOBA_TOPK = 3
ROPE_THETA = 10000.0
POOL_WINDOWS = (2, 4, 8, 16)
POOL_GROUP = 128
D_POOL = len(POOL_WINDOWS) * POOL_GROUP
POOL_HALO = 16
D_FF = 2816
D_IN_PROJ = 3 * D_ATTN + D_POOL + 2 * D_MODEL
LN_EPS = 1e-5
ALPHA = 2.0 ** 0.25
NEG = -1e30
N_DEV = 8
FF_SHARD = D_FF // N_DEV

ADAM_LR = 0.001
ADAM_B1 = 0.9
ADAM_B2 = 0.999
ADAM_EPS = 1e-08
ADAM_WD = 0.01
ADAM_STEP = 10

TOK = 256
FF_CHUNK = 256
LANE = 128
VMEM_LIMIT = 56 * 1024 * 1024

MESH = pl.DeviceIdType.MESH
NT_DIMS = (((1,), (1,)), ((), ()))
TN_DIMS = (((0,), (0,)), ((), ()))


def _params(*sem):
    return pltpu.CompilerParams(dimension_semantics=sem or None, vmem_limit_bytes=VMEM_LIMIT)


def _full(shape):
    zeros = (0,) * len(shape)
    return pl.BlockSpec(shape, lambda *_: zeros, pipeline_mode=pl.Buffered(1))


def _rows(width, tile=TOK):
    return pl.BlockSpec((tile, width), lambda i: (i, 0))


def _sds(shape, dtype):
    return jax.ShapeDtypeStruct(shape, dtype)


def _dot(a, b):
    return jnp.dot(a, b, preferred_element_type=F32)


def _dot_nt(a, b):
    return lax.dot_general(a, b, NT_DIMS, preferred_element_type=F32)


def _dot_tn(a, b):
    return lax.dot_general(a, b, TN_DIMS, preferred_element_type=F32)


def _rope_tables(seq):
    half = HEAD_DIM // 2
    inv_freq = 1.0 / (ROPE_THETA ** (jnp.arange(half, dtype=F32) / half))
    ang = jnp.arange(seq, dtype=F32)[:, None] * inv_freq[None, :]
    cos, sin = jnp.cos(ang), jnp.sin(ang)
    return jnp.tile(cos, (1, 4)), jnp.tile(jnp.concatenate([-sin, sin], axis=1), (1, 2))


def _swap_halves(t):
    lane = lax.broadcasted_iota(jnp.int32, t.shape, 1)
    return jnp.where((lane % HEAD_DIM) < HEAD_DIM // 2, pltpu.roll(t, LANE - 32, 1), pltpu.roll(t, 32, 1))


def _rope(t, cos, sin):
    return t * cos + _swap_halves(t) * sin


def _rope_transposed(g, cos, sin):
    return g * cos + _swap_halves(g * sin)


def _ln_fwd(r, g, b):
    mu = jnp.mean(r, axis=-1, keepdims=True)
    xc = r - mu
    var = jnp.mean(xc * xc, axis=-1, keepdims=True)
    rstd = lax.rsqrt(var + LN_EPS)
    xhat = xc * rstd
    return xhat * g + b, xhat, rstd


def _ln_bwd(dy, xhat, rstd, g):
    dxh = dy * g
    m1 = jnp.mean(dxh, axis=-1, keepdims=True)
    m2 = jnp.mean(dxh * xhat, axis=-1, keepdims=True)
    return rstd * (dxh - m1 - xhat * m2)


def _gelu_parts(a):
    cdf = 0.5 * (1.0 + lax.erf(a * (1.0 / math.sqrt(2.0))))
    pdf = jnp.exp(-0.5 * a * a) * (1.0 / math.sqrt(2.0 * math.pi))
    return a * cdf, cdf + a * pdf


def _shift_down(a, k):
    row = lax.broadcasted_iota(jnp.int32, a.shape, 0)
    return jnp.where(row >= k, pltpu.roll(a, k, 0), 0.0)


def _shift_up(a, k):
    n = a.shape[0]
    row = lax.broadcasted_iota(jnp.int32, a.shape, 0)
    return jnp.where(row < n - k, pltpu.roll(a, n - k, 0), 0.0)


def _conv(a, cw, cb):
    return cw[2:3, :] * a + cw[1:2, :] * _shift_down(a, 1) + cw[0:1, :] * _shift_down(a, 2) + cb


def _pool_count(first_row, rows, window):
    t = first_row + lax.broadcasted_iota(jnp.int32, (rows, 1), 0)
    return jnp.minimum(t + 1, window).astype(F32)


def _grid_call(body, name, steps, in_specs, out_specs, out_shape, operands, scratch=(), tasks=()):
    t_operands, t_in_specs, t_out_shape, t_out_specs, t_sems = _task_args(tasks)
    outs = pl.pallas_call(
        _carry(body, tasks, len(in_specs), len(out_specs), len(scratch), steps), name=name, grid=(steps,),
        in_specs=list(in_specs) + t_in_specs, out_specs=list(out_specs) + t_out_specs,
        out_shape=list(out_shape) + t_out_shape, scratch_shapes=list(scratch) + t_sems,
        compiler_params=_params("arbitrary"),
    )(*operands, *t_operands)
    return outs[:len(out_specs)], _task_results(tasks, outs[len(out_specs):])


def _proj_in(x, win, b_gate, cos, sin, tasks=()):
    seq = x.shape[0]
    nt = seq // TOK

    def body(x_ref, win_ref, bg_ref, cos_ref, sin_ref, xb_ref, q_ref, k_ref, v_ref, u_ref, g_ref, km_ref):
        xb = x_ref[...].astype(BF16)
        xb_ref[...] = xb
        cos_t, sin_t = cos_ref[...], sin_ref[...]
        for sec, out_ref in ((0, q_ref), (1, k_ref)):
            z = _dot(xb, win_ref[sec])
            for c in range(D_ATTN // LANE):
                cols = slice(LANE * c, LANE * (c + 1))
                out_ref[:, cols] = _rope(z[:, cols], cos_t, sin_t)
        km_ref[0] = jnp.mean(k_ref[...], axis=0, keepdims=True)
        v_ref[...] = _dot(xb, win_ref[2]).astype(BF16)
        u_ref[...] = _dot(xb, win_ref[3])
        for n in range(4):
            cols = slice(D_ATTN * n, D_ATTN * (n + 1))
            g_ref[:, cols] = jax.nn.sigmoid(_dot(xb, win_ref[4 + n]) + bg_ref[:, cols])

    return _grid_call(
        body, "proj_in", nt,
        in_specs=[_rows(D_MODEL), _full(win.shape), _full((1, 2 * D_MODEL)), _rows(LANE), _rows(LANE)],
        out_specs=[_rows(D_MODEL), _rows(D_ATTN), _rows(D_ATTN), _rows(D_ATTN), _rows(D_POOL), _rows(2 * D_MODEL),
                   pl.BlockSpec((1, 1, D_ATTN), lambda i: (i, 0, 0))],
        out_shape=[_sds((seq, D_MODEL), BF16), _sds((seq, D_ATTN), F32), _sds((seq, D_ATTN), F32),
                   _sds((seq, D_ATTN), BF16), _sds((seq, D_POOL), F32), _sds((seq, 2 * D_MODEL), F32),
                   _sds((nt, 1, D_ATTN), F32)],
        operands=(x, win, b_gate, cos, sin), tasks=tasks)


def _masked_scores(qf, kb, km, i):
    scale = HEAD_DIM ** -0.5
    width = MOBA_BLOCK * (i + 1)
    s = _dot_nt((qf * scale).astype(BF16), kb[:width])
    pieces = []
    if i > MOBA_TOPK:
        gate = lax.dot_general(qf, km, NT_DIMS, precision=lax.Precision.HIGHEST, preferred_element_type=F32)
        cols = [gate[:, j:j + 1] for j in range(i)]
        for j in range(i):
            rank = jnp.zeros_like(cols[j])
            for jp in range(i):
                if jp != j:
                    above = (cols[jp] >= cols[j]) if jp < j else (cols[jp] > cols[j])
                    rank = rank + above.astype(F32)
            blk = slice(MOBA_BLOCK * j, MOBA_BLOCK * (j + 1))
            pieces.append(jnp.where(rank < MOBA_TOPK, s[:, blk], NEG))
    elif i > 0:
        pieces.append(s[:, :MOBA_BLOCK * i])
    row = lax.broadcasted_iota(jnp.int32, (MOBA_BLOCK, MOBA_BLOCK), 0)
    col = lax.broadcasted_iota(jnp.int32, (MOBA_BLOCK, MOBA_BLOCK), 1)
    pieces.append(jnp.where(col <= row, s[:, MOBA_BLOCK * i:], NEG))
    return pieces[0] if len(pieces) == 1 else jnp.concatenate(pieces, axis=1)


def _attn_fwd(q, k, v, kmean, tasks=()):
    seq = q.shape[0]
    nb = seq // MOBA_BLOCK
    pair = pl.BlockSpec((seq, LANE), lambda p: (0, p))

    def body(q_ref, k_ref, v_ref, km_ref, o_ref, lse_ref):
        for hh in range(LANE // HEAD_DIM):
            ls = slice(HEAD_DIM * hh, HEAD_DIM * (hh + 1))
            kb = k_ref[:, ls].astype(BF16)
            vb = v_ref[:, ls]
            km = km_ref[:, ls]
            for i in range(nb):
                rs = slice(MOBA_BLOCK * i, MOBA_BLOCK * (i + 1))
                sm = _masked_scores(q_ref[rs, ls], kb, km, i)
                m = jnp.max(sm, axis=1, keepdims=True)
                p = jnp.exp(sm - m)
                l = jnp.sum(p, axis=1, keepdims=True)
                o = _dot(p.astype(BF16), vb[:MOBA_BLOCK * (i + 1)])
                o_ref[rs, ls] = o / l
                lse_ref[rs, ls] = jnp.broadcast_to(m + jnp.log(l), (MOBA_BLOCK, HEAD_DIM))

    return _grid_call(
        body, "attn_fwd", D_ATTN // LANE,
        in_specs=[pair, pair, pair, pl.BlockSpec((nb, LANE), lambda p: (0, p))], out_specs=[pair, pair],
        out_shape=[_sds((seq, D_ATTN), F32), _sds((seq, D_ATTN), F32)],
        operands=(q, k, v, kmean), tasks=tasks)


def _mix(o, u, g, x, wba, wbp, wout, w_pool, pool_scale, ln_g, ln_b, tasks=()):
    seq = x.shape[0]

    def body(o_ref, u_ref, uprev_ref, g_ref, x_ref, wba_ref, wbp_ref, wout_ref, wp_ref, ps_ref, lg_ref, lb_ref,
             ya_ref, yp_ref, pooled_ref, mixed_ref, ypre_ref, merged_ref, xhat_ref, rstd_ref, h_ref, hb_ref, ext):
        i = pl.program_id(0)
        ya = _dot(o_ref[...].astype(BF16), wba_ref[...])
        ucur = u_ref[...]
        ext[0:POOL_HALO, :] = jnp.where(i > 0, uprev_ref[...], 0.0)
        ext[POOL_HALO:, :] = ucur
        for grp, window in enumerate(POOL_WINDOWS):
            cols = slice(POOL_GROUP * grp, POOL_GROUP * (grp + 1))
            acc = ucur[:, cols]
            for kk in range(1, window):
                acc = acc + ext[pl.ds(POOL_HALO - kk, TOK), cols]
            pooled = acc / _pool_count(i * TOK, TOK, window) - ucur[:, cols]
            pooled_ref[:, cols] = pooled.astype(BF16)
            mixed_ref[:, cols] = _dot(pooled.astype(BF16), wp_ref[grp].astype(BF16))
        mixed = mixed_ref[...]
        ypre = (mixed * ps_ref[...]).astype(BF16)
        ypre_ref[...] = ypre
        yp = _dot(ypre, wbp_ref[...])
        ya_ref[...] = ya
        yp_ref[...] = yp
        merged = (g_ref[:, :D_MODEL] * ya + g_ref[:, D_MODEL:] * yp).astype(BF16)
        merged_ref[...] = merged
        r1 = ALPHA * x_ref[...] + _dot(merged, wout_ref[...])
        h, xhat, rstd = _ln_fwd(r1, lg_ref[...], lb_ref[...])
        xhat_ref[...] = xhat
        rstd_ref[...] = jnp.broadcast_to(rstd, (TOK, LANE))
        h_ref[...] = h
        hb_ref[...] = h.astype(BF16)

    halo = pl.BlockSpec((POOL_HALO, D_POOL), lambda i: (jnp.maximum(i * (TOK // POOL_HALO) - 1, 0), 0))
    return _grid_call(
        body, "mix", seq // TOK,
        in_specs=[_rows(D_ATTN), _rows(D_POOL), halo, _rows(2 * D_MODEL), _rows(D_MODEL),
                  _full(wba.shape), _full(wbp.shape), _full(wout.shape), _full(w_pool.shape),
                  _full((1, D_POOL)), _full((1, D_MODEL)), _full((1, D_MODEL))],
        out_specs=[_rows(D_MODEL), _rows(D_MODEL), _rows(D_POOL), _rows(D_POOL), _rows(D_POOL), _rows(D_MODEL),
                   _rows(D_MODEL), _rows(LANE), _rows(D_MODEL), _rows(D_MODEL)],
        out_shape=[_sds((seq, D_MODEL), F32), _sds((seq, D_MODEL), F32), _sds((seq, D_POOL), BF16),
                   _sds((seq, D_POOL), F32), _sds((seq, D_POOL), BF16), _sds((seq, D_MODEL), BF16),
                   _sds((seq, D_MODEL), F32), _sds((seq, LANE), F32), _sds((seq, D_MODEL), F32),
                   _sds((seq, D_MODEL), BF16)],
        operands=(o, u, u, g, x, wba, wbp, wout, w_pool, pool_scale, ln_g, ln_b),
        scratch=[pltpu.VMEM((TOK + POOL_HALO, D_POOL), F32)], tasks=tasks)


def _ffn_up(hb, wgt, wut, conv_w, conv_b):
    seq = hb.shape[0]
    wblk = pl.BlockSpec((FF_CHUNK, D_MODEL), lambda c: (c, 0))
    cblk = lambda rows: pl.BlockSpec((rows, FF_CHUNK), lambda c: (0, c))
    oblk = pl.BlockSpec((seq, FF_CHUNK), lambda c: (0, c))

    def body(h_ref, wg_ref, wu_ref, cw_ref, cb_ref, a_ref, u_ref, act_ref):
        h = h_ref[...]
        a = _dot_nt(h, wg_ref[...])
        u = _dot_nt(h, wu_ref[...])
        a_ref[...] = a
        u_ref[...] = u
        gelu, _ = _gelu_parts(_conv(a, cw_ref[...], cb_ref[...]))
        act_ref[...] = (gelu * u).astype(BF16)

    return pl.pallas_call(
        body, name="ffn_up", grid=(D_FF // FF_CHUNK,),
        in_specs=[_full(hb.shape), wblk, wblk, cblk(3), cblk(1)],
        out_specs=[oblk, oblk, oblk],
        out_shape=[_sds((seq, D_FF), F32), _sds((seq, D_FF), F32), _sds((seq, D_FF), BF16)],
        compiler_params=_params("parallel"),
    )(hb, wgt, wut, conv_w, conv_b)


def _ffn_down(act, wd, h, target, ln_g, ln_b):
    seq = h.shape[0]

    def body(act_ref, wd_ref, h_ref, t_ref, lg_ref, lb_ref, dr_ref, drb_ref, loss_ref, dg_ref, db_ref):
        i = pl.program_id(0)

        @pl.when(i == 0)
        def _():
            loss_ref[...] = jnp.zeros_like(loss_ref)
            dg_ref[...] = jnp.zeros_like(dg_ref)
            db_ref[...] = jnp.zeros_like(db_ref)

        r2 = ALPHA * h_ref[...] + _dot(act_ref[...], wd_ref[...])
        y, xhat, rstd = _ln_fwd(r2, lg_ref[...], lb_ref[...])
        diff = y - t_ref[...]
        loss_ref[...] += jnp.sum(diff * diff) * (0.5 / D_MODEL)
        dy = diff * (1.0 / D_MODEL)
        dg_ref[...] += jnp.sum(dy * xhat, axis=0, keepdims=True)
        db_ref[...] += jnp.sum(dy, axis=0, keepdims=True)
        dr = _ln_bwd(dy, xhat, rstd, lg_ref[...])
        dr_ref[...] = dr
        drb_ref[...] = dr.astype(BF16)

    vec = pl.BlockSpec((1, D_MODEL), lambda i: (0, 0))
    return pl.pallas_call(
        body, name="ffn_down", grid=(seq // TOK,),
        in_specs=[_rows(D_FF), _full(wd.shape), _rows(D_MODEL), _rows(D_MODEL), _full((1, D_MODEL)), _full((1, D_MODEL))],
        out_specs=[_rows(D_MODEL), _rows(D_MODEL), pl.BlockSpec((8, LANE), lambda i: (0, 0)), vec, vec],
        out_shape=[_sds((seq, D_MODEL), F32), _sds((seq, D_MODEL), BF16), _sds((8, LANE), F32),
                   _sds((1, D_MODEL), F32), _sds((1, D_MODEL), F32)],
        compiler_params=_params("arbitrary"),
    )(act, wd, h, target, ln_g, ln_b)


def _ffn_bwd(drb, hb, a, u, wd, conv_w, conv_b):
    seq = hb.shape[0]
    wblk = pl.BlockSpec((FF_CHUNK, D_MODEL), lambda c: (c, 0))
    cblk = lambda rows: pl.BlockSpec((rows, FF_CHUNK), lambda c: (0, c))
    sblk = pl.BlockSpec((seq, FF_CHUNK), lambda c: (0, c))

    def body(dr_ref, h_ref, a_ref, u_ref, wd_ref, cw_ref, cb_ref, da_ref, du_ref, dwd_ref, dwg_ref, dwu_ref, dc_ref):
        dr = dr_ref[...]
        h = h_ref[...]
        a = a_ref[...]
        u = u_ref[...]
        cw = cw_ref[...]
        dact = _dot_nt(dr, wd_ref[...])
        gelu, dgelu = _gelu_parts(_conv(a, cw, cb_ref[...]))
        dwd_ref[...] = _dot_tn((gelu * u).astype(BF16), dr).astype(BF16)
        du = (dact * gelu).astype(BF16)
        dac = dact * u * dgelu
        da = (cw[2:3, :] * dac + cw[1:2, :] * _shift_up(dac, 1) + cw[0:1, :] * _shift_up(dac, 2)).astype(BF16)
        da_ref[...] = da
        du_ref[...] = du
        dwg_ref[...] = _dot_tn(da, h).astype(BF16)
        dwu_ref[...] = _dot_tn(du, h).astype(BF16)
        dc_ref[0:1, :] = jnp.sum(dac * _shift_down(a, 2), axis=0, keepdims=True)
        dc_ref[1:2, :] = jnp.sum(dac * _shift_down(a, 1), axis=0, keepdims=True)
        dc_ref[2:3, :] = jnp.sum(dac * a, axis=0, keepdims=True)
        dc_ref[3:4, :] = jnp.sum(dac, axis=0, keepdims=True)
        dc_ref[4:8, :] = jnp.zeros((4, FF_CHUNK), F32)

    return pl.pallas_call(
        body, name="ffn_bwd", grid=(D_FF // FF_CHUNK,),
        in_specs=[_full(drb.shape), _full(hb.shape), sblk, sblk, wblk, cblk(3), cblk(1)],
        out_specs=[sblk, sblk, wblk, wblk, wblk, cblk(8)],
        out_shape=[_sds((seq, D_FF), BF16), _sds((seq, D_FF), BF16), _sds((D_FF, D_MODEL), BF16),
                   _sds((D_FF, D_MODEL), BF16), _sds((D_FF, D_MODEL), BF16), _sds((8, D_FF), F32)],
        compiler_params=_params("parallel"),
    )(drb, hb, a, u, wd, conv_w, conv_b)


def _ln1_bwd(dr2, da, du, wgt, wut, xhat, rstd, ln_g, tasks=()):
    seq = dr2.shape[0]

    def body(dr2_ref, da_ref, du_ref, wg_ref, wu_ref, xhat_ref, rstd_ref, lg_ref, dr_ref, drb_ref, dg_ref, db_ref):
        @pl.when(pl.program_id(0) == 0)
        def _():
            dg_ref[...] = jnp.zeros_like(dg_ref)
            db_ref[...] = jnp.zeros_like(db_ref)

        dh = ALPHA * dr2_ref[...] + _dot(da_ref[...], wg_ref[...]) + _dot(du_ref[...], wu_ref[...])
        xhat = xhat_ref[...]
        dg_ref[...] += jnp.sum(dh * xhat, axis=0, keepdims=True)
        db_ref[...] += jnp.sum(dh, axis=0, keepdims=True)
        dr = _ln_bwd(dh, xhat, rstd_ref[:, 0:1], lg_ref[...])
        dr_ref[...] = dr
        drb_ref[...] = dr.astype(BF16)

    vec = pl.BlockSpec((1, D_MODEL), lambda i: (0, 0))
    return _grid_call(
        body, "ln1_bwd", seq // TOK,
        in_specs=[_rows(D_MODEL), _rows(D_FF), _rows(D_FF), _full(wgt.shape), _full(wut.shape), _rows(D_MODEL),
                  _rows(LANE), _full((1, D_MODEL))],
        out_specs=[_rows(D_MODEL), _rows(D_MODEL), vec, vec],
        out_shape=[_sds((seq, D_MODEL), F32), _sds((seq, D_MODEL), BF16), _sds((1, D_MODEL), F32),
                   _sds((1, D_MODEL), F32)],
        operands=(dr2, da, du, wgt, wut, xhat, rstd, ln_g), tasks=tasks)


def _mix_bwd(drb, ya, yp, g, mixed, wout, wba, wbp, w_pool, pool_scale):
    seq = drb.shape[0]

    def body(dr_ref, ya_ref, yp_ref, g_ref, mixed_ref, wout_ref, wba_ref, wbp_ref, wp_ref, ps_ref,
             dzg_ref, dya_ref, dyp_ref, do_ref, dmixed_ref, dpooled_ref, dbg_ref, dps_ref):
        @pl.when(pl.program_id(0) == 0)
        def _():
            dbg_ref[...] = jnp.zeros_like(dbg_ref)
            dps_ref[...] = jnp.zeros_like(dps_ref)

        dmerged = _dot_nt(dr_ref[...], wout_ref[...])
        ga, gp = g_ref[:, :D_MODEL], g_ref[:, D_MODEL:]
        dzga = dmerged * ya_ref[...] * ga * (1.0 - ga)
        dzgp = dmerged * yp_ref[...] * gp * (1.0 - gp)
        dzg_ref[:, :D_MODEL] = dzga.astype(BF16)
        dzg_ref[:, D_MODEL:] = dzgp.astype(BF16)
        dbg_ref[:, :D_MODEL] += jnp.sum(dzga, axis=0, keepdims=True)
        dbg_ref[:, D_MODEL:] += jnp.sum(dzgp, axis=0, keepdims=True)
        dya = (dmerged * ga).astype(BF16)
        dyp = (dmerged * gp).astype(BF16)
        dya_ref[...] = dya
        dyp_ref[...] = dyp
        do_ref[...] = _dot_nt(dya, wba_ref[...])
        dypre = _dot_nt(dyp, wbp_ref[...])
        dps_ref[...] += jnp.sum(dypre * mixed_ref[...], axis=0, keepdims=True)
        dmixed = (dypre * ps_ref[...]).astype(BF16)
        dmixed_ref[...] = dmixed
        for grp in range(len(POOL_WINDOWS)):
            cols = slice(POOL_GROUP * grp, POOL_GROUP * (grp + 1))
            dpooled_ref[:, cols] = _dot_nt(dmixed[:, cols], wp_ref[grp].astype(BF16))

    return pl.pallas_call(
        body, name="mix_bwd", grid=(seq // TOK,),
        in_specs=[_rows(D_MODEL), _rows(D_MODEL), _rows(D_MODEL), _rows(2 * D_MODEL), _rows(D_POOL),
                  _full(wout.shape), _full(wba.shape), _full(wbp.shape), _full(w_pool.shape), _full((1, D_POOL))],
        out_specs=[_rows(2 * D_MODEL), _rows(D_MODEL), _rows(D_MODEL), _rows(D_ATTN), _rows(D_POOL), _rows(D_POOL),
                   pl.BlockSpec((1, 2 * D_MODEL), lambda i: (0, 0)), pl.BlockSpec((1, D_POOL), lambda i: (0, 0))],
        out_shape=[_sds((seq, 2 * D_MODEL), BF16), _sds((seq, D_MODEL), BF16), _sds((seq, D_MODEL), BF16),
                   _sds((seq, D_ATTN), F32), _sds((seq, D_POOL), BF16), _sds((seq, D_POOL), F32),
                   _sds((1, 2 * D_MODEL), F32), _sds((1, D_POOL), F32)],
        compiler_params=_params("arbitrary"),
    )(drb, ya, yp, g, mixed, wout, wba, wbp, w_pool, pool_scale)


def _attn_bwd(q, k, v, kmean, o, lse, do, cos, sin, tasks=()):
    seq = q.shape[0]
    nb = seq // MOBA_BLOCK
    pair = pl.BlockSpec((seq, LANE), lambda p: (0, p))
    table = pl.BlockSpec((seq, LANE), lambda p: (0, 0))

    def body(q_ref, k_ref, v_ref, km_ref, o_ref, lse_ref, do_ref, cos_ref, sin_ref, dq_ref, dk_ref, dv_ref,
             dq_acc, dk_acc, dv_acc):
        dk_acc[...] = jnp.zeros_like(dk_acc)
        dv_acc[...] = jnp.zeros_like(dv_acc)
        for hh in range(LANE // HEAD_DIM):
            ls = slice(HEAD_DIM * hh, HEAD_DIM * (hh + 1))
            kb = k_ref[:, ls].astype(BF16)
            vb = v_ref[:, ls]
            km = km_ref[:, ls]
            for i in range(nb):
                rs = slice(MOBA_BLOCK * i, MOBA_BLOCK * (i + 1))
                ks = slice(0, MOBA_BLOCK * (i + 1))
                qf = q_ref[rs, ls]
                p = jnp.exp(_masked_scores(qf, kb, km, i) - lse_ref[rs, ls][:, 0:1])
                dob = do_ref[rs, ls]
                delta = jnp.sum(dob * o_ref[rs, ls], axis=1, keepdims=True)
                dob16 = dob.astype(BF16)
                dp = _dot_nt(dob16, vb[ks])
                ds = (p * (dp - delta) * (HEAD_DIM ** -0.5)).astype(BF16)
                dq_acc[rs, ls] = _dot(ds, kb[ks])
                dk_acc[ks, ls] += _dot_tn(ds, qf.astype(BF16))
                dv_acc[ks, ls] += _dot_tn(p.astype(BF16), dob16)
        cos_t, sin_t = cos_ref[...], sin_ref[...]
        dq_ref[...] = _rope_transposed(dq_acc[...], cos_t, sin_t).astype(BF16)
        dk_ref[...] = _rope_transposed(dk_acc[...], cos_t, sin_t).astype(BF16)
        dv_ref[...] = dv_acc[...].astype(BF16)

    return _grid_call(
        body, "attn_bwd", D_ATTN // LANE,
        in_specs=[pair, pair, pair, pl.BlockSpec((nb, LANE), lambda p: (0, p)), pair, pair, pair, table, table],
        out_specs=[pair, pair, pair], out_shape=[_sds((seq, D_ATTN), BF16)] * 3,
        operands=(q, k, v, kmean, o, lse, do, cos, sin),
        scratch=[pltpu.VMEM((seq, LANE), F32)] * 3, tasks=tasks)


def _in_bwd(dq, dk, dv, dpooled, dzg, dr1, win):
    seq = dr1.shape[0]
    nt = seq // TOK

    def body(dq_ref, dk_ref, dv_ref, dp_ref, dpnext_ref, dzg_ref, dr_ref, win_ref, dx_ref, dz_ref, ext):
        i = pl.program_id(0)
        dp = dp_ref[...]
        dpn = jnp.where(i < nt - 1, dpnext_ref[...], 0.0)
        for grp, window in enumerate(POOL_WINDOWS):
            cols = slice(POOL_GROUP * grp, POOL_GROUP * (grp + 1))
            ext[0:TOK, cols] = dp[:, cols] / _pool_count(i * TOK, TOK, window)
            ext[TOK:, cols] = dpn[:, cols] / _pool_count((i + 1) * TOK, POOL_HALO, window)
        for grp, window in enumerate(POOL_WINDOWS):
            cols = slice(POOL_GROUP * grp, POOL_GROUP * (grp + 1))
            acc = ext[0:TOK, cols] - dp[:, cols]
            for kk in range(1, window):
                acc = acc + ext[pl.ds(kk, TOK), cols]
            dz_ref[:, 3 * D_ATTN + POOL_GROUP * grp:3 * D_ATTN + POOL_GROUP * (grp + 1)] = acc.astype(BF16)
        dz_ref[:, 0:D_ATTN] = dq_ref[...]
        dz_ref[:, D_ATTN:2 * D_ATTN] = dk_ref[...]
        dz_ref[:, 2 * D_ATTN:3 * D_ATTN] = dv_ref[...]
        dz_ref[:, 3 * D_ATTN + D_POOL:] = dzg_ref[...]
        dx = ALPHA * dr_ref[...]
        for n in range(N_DEV):
            dx = dx + _dot_nt(dz_ref[:, D_ATTN * n:D_ATTN * (n + 1)], win_ref[n])
        dx_ref[...] = dx

    halo = pl.BlockSpec((POOL_HALO, D_POOL),
                        lambda i: (jnp.minimum((i + 1) * (TOK // POOL_HALO), seq // POOL_HALO - 1), 0))
    return pl.pallas_call(
        body, name="in_bwd", grid=(nt,),
        in_specs=[_rows(D_ATTN), _rows(D_ATTN), _rows(D_ATTN), _rows(D_POOL), halo, _rows(2 * D_MODEL),
                  _rows(D_MODEL), _full(win.shape)],
        out_specs=[_rows(D_MODEL), _rows(D_IN_PROJ)],
        out_shape=[_sds((seq, D_MODEL), F32), _sds((seq, D_IN_PROJ), BF16)],
        scratch_shapes=[pltpu.VMEM((TOK + POOL_HALO, D_POOL), F32)],
        compiler_params=_params("parallel"),
    )(dq, dk, dv, dpooled, dpooled, dzg, dr1, win)


def _tn_matmul(name, a, b, out_shape, out_dtype, grid, a_spec, b_spec, o_spec):
    def body(a_ref, b_ref, o_ref):
        r = _dot_tn(a_ref[...].astype(BF16), b_ref[...].astype(BF16))
        o_ref[...] = r.reshape(o_ref.shape).astype(o_ref.dtype)

    return pl.pallas_call(
        body, name=name, grid=grid, in_specs=[a_spec, b_spec], out_specs=o_spec,
        out_shape=_sds(out_shape, out_dtype),
        compiler_params=_params(*(("parallel",) * len(grid))),
    )(a, b)


def _place():
    return lax.axis_index("x"), lax.axis_index("y"), lax.axis_index("c")


def _other_chips(x, y):
    return [(1 - x, y), (x, 1 - y), (1 - x, 1 - y)]


DMA_SEMS = pltpu.SemaphoreType.DMA


class _AllGather:
    def __init__(self, shards):
        self.operands = list(shards)
        self.n = len(shards)
        self.out_shape = [_sds((N_DEV, *s.shape), s.dtype) for s in shards]
        self.sems = [DMA_SEMS((7 * self.n,)), DMA_SEMS((7 * self.n,)), DMA_SEMS((self.n,))]

    def _copy(self, refs, a, k, block, to, from_input=False):
        ins, outs, (send_sems, recv_sems, _) = refs
        px, py, pc = block
        dst = outs[a].at[4 * px + 2 * py + pc]
        return pltpu.make_async_remote_copy(
            src_ref=ins[a] if from_input else dst, dst_ref=dst,
            send_sem=send_sems.at[7 * a + k], recv_sem=recv_sems.at[7 * a + k],
            device_id=to, device_id_type=MESH)

    def _local(self, refs, a):
        ins, outs, (_, _, local_sems) = refs
        x, y, c = _place()
        return pltpu.make_async_copy(ins[a], outs[a].at[4 * x + 2 * y + c], local_sems.at[a])

    def start(self, refs):
        x, y, c = _place()
        for a in range(self.n):
            self._local(refs, a).start()
        for a in range(self.n):
            self._copy(refs, a, 0, (x, y, c), (x, y, 1 - c), True).start()
            for j, chip in enumerate(_other_chips(x, y)):
                self._copy(refs, a, 1 + j, (x, y, c), (*chip, c), True).start()

    def middle(self, refs):
        x, y, c = _place()
        for j, chip in enumerate(_other_chips(x, y)):
            for a in range(self.n):
                self._copy(refs, a, 1 + j, (*chip, c), (x, y, c)).wait_recv()
                self._copy(refs, a, 4 + j, (*chip, c), (x, y, 1 - c)).start()

    def finish(self, refs):
        x, y, c = _place()
        me, sibling = (x, y, c), (x, y, 1 - c)
        chips = _other_chips(x, y)
        for a in range(self.n):
            self._copy(refs, a, 0, sibling, me).wait_recv()
            for j, chip in enumerate(chips):
                self._copy(refs, a, 4 + j, (*chip, 1 - c), me).wait_recv()
        for a in range(self.n):
            self._copy(refs, a, 0, me, sibling, True).wait_send()
            for j, chip in enumerate(chips):
                self._copy(refs, a, 1 + j, me, (*chip, c), True).wait_send()
                self._copy(refs, a, 4 + j, (*chip, c), sibling).wait_send()
            self._local(refs, a).wait()


class _SiblingSend:
    def __init__(self, partials):
        self.operands = list(partials)
        self.n = len(partials)
        self.out_shape = [_sds((4, *p.shape[1:]), p.dtype) for p in partials]
        self.sems = [DMA_SEMS((4 * self.n,)), DMA_SEMS((4 * self.n,))]

    def _copy(self, refs, a, q):
        ins, outs, (send_sems, recv_sems) = refs
        x, y, c = _place()
        return pltpu.make_async_remote_copy(
            src_ref=ins[a].at[2 * q + 1 - c], dst_ref=outs[a].at[q],
            send_sem=send_sems.at[4 * a + q], recv_sem=recv_sems.at[4 * a + q],
            device_id=(x, y, 1 - c), device_id_type=MESH)

    def start(self, refs):
        for a in range(self.n):
            for q in range(4):
                self._copy(refs, a, q).start()

    def middle(self, refs):
        pass

    def finish(self, refs):
        for a in range(self.n):
            for q in range(4):
                self._copy(refs, a, q).wait()


class _ChipScatter:
    def __init__(self, chip_partials):
        self.operands = list(chip_partials)
        self.n = len(chip_partials)
        self.out_shape = [_sds(p.shape, p.dtype) for p in chip_partials]
        self.sems = [DMA_SEMS((3 * self.n,)), DMA_SEMS((3 * self.n,)), DMA_SEMS((self.n,))]

    def _copy(self, refs, a, k, arrival=False):
        ins, outs, (send_sems, recv_sems, _) = refs
        x, y, c = _place()
        px, py = _other_chips(x, y)[k]
        mine, theirs = 2 * x + y, 2 * px + py
        return pltpu.make_async_remote_copy(
            src_ref=ins[a].at[mine if arrival else theirs], dst_ref=outs[a].at[theirs if arrival else mine],
            send_sem=send_sems.at[3 * a + k], recv_sem=recv_sems.at[3 * a + k],
            device_id=(px, py, c), device_id_type=MESH)

    def _local(self, refs, a):
        ins, outs, (_, _, local_sems) = refs
        x, y, _ = _place()
        return pltpu.make_async_copy(ins[a].at[2 * x + y], outs[a].at[2 * x + y], local_sems.at[a])

    def start(self, refs):
        for a in range(self.n):
            self._local(refs, a).start()
            for k in range(3):
                self._copy(refs, a, k).start()

    def middle(self, refs):
        pass

    def finish(self, refs):
        for a in range(self.n):
            for k in range(3):
                self._copy(refs, a, k, arrival=True).wait_recv()
        for a in range(self.n):
            for k in range(3):
                self._copy(refs, a, k).wait_send()
            self._local(refs, a).wait()


class _DirectScatter:
    def __init__(self, partials):
        self.operands = list(partials)
        self.n = len(partials)
        self.out_shape = [_sds(p.shape, p.dtype) for p in partials]
        self.sems = [DMA_SEMS((7 * self.n,)), DMA_SEMS((7 * self.n,)), DMA_SEMS((self.n,))]

    def _copy(self, refs, a, k, arrival=False):
        ins, outs, (send_sems, recv_sems, _) = refs
        x, y, c = _place()
        peer = [(x, y, 1 - c), (1 - x, y, c), (x, 1 - y, c), (1 - x, 1 - y, c),
                (1 - x, y, 1 - c), (x, 1 - y, 1 - c), (1 - x, 1 - y, 1 - c)][k]
        mine, theirs = 4 * x + 2 * y + c, 4 * peer[0] + 2 * peer[1] + peer[2]
        return pltpu.make_async_remote_copy(
            src_ref=ins[a].at[mine if arrival else theirs], dst_ref=outs[a].at[theirs if arrival else mine],
            send_sem=send_sems.at[7 * a + k], recv_sem=recv_sems.at[7 * a + k],
            device_id=peer, device_id_type=MESH)

    def _local(self, refs, a):
        ins, outs, (_, _, local_sems) = refs
        x, y, c = _place()
        return pltpu.make_async_copy(ins[a].at[4 * x + 2 * y + c], outs[a].at[4 * x + 2 * y + c], local_sems.at[a])

    def start(self, refs):
        for a in range(self.n):
            self._local(refs, a).start()
            for k in range(7):
                self._copy(refs, a, k).start()

    def middle(self, refs):
        pass

    def finish(self, refs):
        for a in range(self.n):
            for k in range(7):
                self._copy(refs, a, k, arrival=True).wait_recv()
        for a in range(self.n):
            for k in range(7):
                self._copy(refs, a, k).wait_send()
            self._local(refs, a).wait()


def _task_args(tasks):
    hbm = pl.BlockSpec(memory_space=pl.ANY)
    operands = [o for t in tasks for o in t.operands]
    out_shape = [s for t in tasks for s in t.out_shape]
    sems = [s for t in tasks for s in t.sems]
    return operands, [hbm] * len(operands), out_shape, [hbm] * len(out_shape), sems


def _task_refs(tasks, ins, outs, sems):
    per_task = []
    for t in tasks:
        ni, no, ns = len(t.operands), len(t.out_shape), len(t.sems)
        per_task.append((ins[:ni], outs[:no], sems[:ns]))
        ins, outs, sems = ins[ni:], outs[no:], sems[ns:]
    return per_task


def _task_results(tasks, outs):
    res = []
    for t in tasks:
        res.append(list(outs[:len(t.out_shape)]))
        outs = outs[len(t.out_shape):]
    return res


def _carry(body, tasks, n_in, n_out, n_scratch, steps):
    if not tasks:
        return body
    t_in = sum(len(t.operands) for t in tasks)
    t_out = sum(len(t.out_shape) for t in tasks)

    def wrapped(*refs):
        ins, refs = refs[:n_in], refs[n_in:]
        t_ins, refs = refs[:t_in], refs[t_in:]
        outs, refs = refs[:n_out], refs[n_out:]
        t_outs, refs = refs[:t_out], refs[t_out:]
        scratch, t_sems = refs[:n_scratch], refs[n_scratch:]
        per_task = _task_refs(tasks, t_ins, t_outs, t_sems)
        step = pl.program_id(0)

        @pl.when(step == 0)
        def _():
            for t, r in zip(tasks, per_task):
                t.start(r)

        @pl.when(step == steps - 1)
        def _():
            for t, r in zip(tasks, per_task):
                t.middle(r)

        body(*ins, *outs, *scratch)

        @pl.when(step == steps - 1)
        def _():
            for t, r in zip(tasks, per_task):
                t.finish(r)

    return wrapped


def _exchange(name, tasks):
    operands, in_specs, out_shape, out_specs, sems = _task_args(tasks)

    def body(*refs):
        ni, no = len(operands), len(out_shape)
        per_task = _task_refs(tasks, refs[:ni], refs[ni:ni + no], refs[ni + no:])
        for phase in ("start", "middle", "finish"):
            for t, r in zip(tasks, per_task):
                getattr(t, phase)(r)

    outs = pl.pallas_call(body, name=name, in_specs=in_specs, out_specs=out_specs, out_shape=out_shape,
                          scratch_shapes=sems)(*operands)
    return _task_results(tasks, outs)


def _row_tile(rows, cols, whole_up_to=256 * 1024):
    if rows * cols <= whole_up_to:
        return rows
    for t in (256, 128, 64, 32, 16, 8):
        if rows % t == 0:
            return t
    return rows


def _pair_sum(name, partials, from_sibling):
    _, rows, cols = partials.shape
    tile = _row_tile(rows, cols, 512 * 1024)

    def body(p_ref, s_ref, o_ref):
        mine = jnp.where(lax.axis_index("c") == 0, p_ref[0, 0].astype(F32), p_ref[0, 1].astype(F32))
        o_ref[0] = (mine + s_ref[0].astype(F32)).astype(o_ref.dtype)

    blk = pl.BlockSpec((1, tile, cols), lambda q, i: (q, i, 0))
    return pl.pallas_call(
        body, name=name, grid=(4, rows // tile),
        in_specs=[pl.BlockSpec((1, 2, tile, cols), lambda q, i: (q, 0, i, 0)), blk],
        out_specs=blk, out_shape=_sds(from_sibling.shape, from_sibling.dtype),
        compiler_params=_params("parallel", "parallel"),
    )(partials.reshape(4, 2, rows, cols), from_sibling)


def _sum_leading(name, stacked):
    parts, rows, cols = stacked.shape
    tile = _row_tile(rows, cols, (512 if parts <= 4 else 256) * 1024)

    def body(s_ref, o_ref):
        acc = s_ref[0].astype(F32)
        for d in range(1, parts):
            acc = acc + s_ref[d].astype(F32)
        o_ref[...] = acc

    return pl.pallas_call(
        body, name=name, grid=(rows // tile,),
        in_specs=[pl.BlockSpec((parts, tile, cols), lambda i: (0, i, 0))],
        out_specs=pl.BlockSpec((tile, cols), lambda i: (i, 0)),
        out_shape=_sds((rows, cols), F32),
        compiler_params=_params("parallel"),
    )(stacked)


def _adamw(name, w, g, m, v):
    rows, cols = w.shape
    tile = _row_tile(rows, cols)

    def body(w_ref, g_ref, m_ref, v_ref, d_ref, nm_ref, nv_ref):
        g = g_ref[...]
        nm = ADAM_B1 * m_ref[...] + (1.0 - ADAM_B1) * g
        nv = ADAM_B2 * v_ref[...] + (1.0 - ADAM_B2) * (g * g)
        m_hat = nm / (1.0 - ADAM_B1 ** ADAM_STEP)
        v_hat = nv / (1.0 - ADAM_B2 ** ADAM_STEP)
        d_ref[...] = -ADAM_LR * (m_hat / (jnp.sqrt(v_hat) + ADAM_EPS) + ADAM_WD * w_ref[...])
        nm_ref[...] = nm
        nv_ref[...] = nv

    blk = pl.BlockSpec((tile, cols), lambda i: (i, 0))
    return pl.pallas_call(
        body, name=name, grid=(rows // tile,),
        in_specs=[blk] * 4, out_specs=[blk] * 3,
        out_shape=[_sds((rows, cols), F32)] * 3,
        compiler_params=_params("parallel"),
    )(w, g, m, v)


SMALL = ("b_gate", "w_pool", "pool_scale", "ln1_g", "ln1_b", "conv_b", "ln2_g", "ln2_b")
TILE = 8 * LANE


def _pack(parts):
    tiles = []
    for p in parts:
        flat = p.reshape(-1)
        tiles.append(jnp.pad(flat, (0, -flat.size % TILE)).reshape(-1, LANE))
    return jnp.concatenate(tiles, axis=0)


def _unpack(packed, shapes):
    out, at = [], 0
    for shape in shapes:
        size = math.prod(shape)
        rows = -(-size // TILE) * 8
        out.append(packed[at:at + rows].reshape(-1)[:size].reshape(shape))
        at += rows
    return out


MIXER = ("w_branch_attn", "w_branch_pool", "w_out", "conv_w")
FFN = ("w_ffn_gate_t", "w_ffn_up_t", "w_ffn_down")


def _columns(t):
    return jnp.transpose(t, (1, 0, 2)).reshape(t.shape[1], N_DEV * t.shape[2])


def _row_blocks(t):
    return t.reshape(N_DEV * t.shape[1], t.shape[2])


def _by_owner(t):
    return t.reshape(N_DEV, t.shape[0] // N_DEV, t.shape[1])


def _reduce_halves(names, partials, from_sibling):
    return [_pair_sum("pair_sum_" + n, p, s) for n, p, s in zip(names, partials, from_sibling)]


def _local_step(x, target, shards, small):
    seq = x.shape[0]
    cos, sin = _rope_tables(seq)
    whole = lambda width: pl.BlockSpec((seq, width), lambda *_: (0, 0))
    ((w_in_all,),) = _exchange("gather_w_in", [_AllGather([shards["w_in"]])])
    (xb, q, k, v, u, g, kmean), (mixer,) = _proj_in(
        x, w_in_all, small["b_gate"], cos, sin, tasks=[_AllGather([shards[n] for n in MIXER])])
    wba, wbp, wout, conv_w = _columns(mixer[0]), _columns(mixer[1]), _row_blocks(mixer[2]), _columns(mixer[3])
    kmean = kmean.reshape(seq // MOBA_BLOCK, D_ATTN)
    (o, lse), (gate_up,) = _attn_fwd(
        q, k, v, kmean, tasks=[_AllGather([shards["w_ffn_gate_t"], shards["w_ffn_up_t"]])])
    wgt, wut = _row_blocks(gate_up[0]), _row_blocks(gate_up[1])
    (ya, yp, pooled, mixed, ypre, merged, xhat1, rstd1, h1, h1b), ((wd,),) = _mix(
        o, u, g, x, wba, wbp, wout, small["w_pool"], small["pool_scale"], small["ln1_g"], small["ln1_b"],
        tasks=[_AllGather([shards["w_ffn_down"]])])
    wd = _row_blocks(wd)
    a, uf, act = _ffn_up(h1b, wgt, wut, conv_w, small["conv_b"])
    dr2, dr2b, loss, dg2, db2 = _ffn_down(act, wd, h1, target, small["ln2_g"], small["ln2_b"])

    da, du, dwd, dwg, dwu, dconv = _ffn_bwd(dr2b, h1b, a, uf, wd, conv_w, small["conv_b"])
    ffn_partials = [_by_owner(dwg), _by_owner(dwu), _by_owner(dwd)]
    (dr1, dr1b, dg1, db1), (ffn_sibling,) = _ln1_bwd(
        dr2, da, du, wgt, wut, xhat1, rstd1, small["ln1_g"], tasks=[_SiblingSend(ffn_partials)])
    ffn_chip = _reduce_halves(FFN, ffn_partials, ffn_sibling)
    dzg, dya, dyp, do, dmixed, dpooled, dbg, dps = _mix_bwd(
        dr1b, ya, yp, g, mixed, wout, wba, wbp, small["w_pool"], small["pool_scale"])
    dw_out = _tn_matmul(
        "dw_out", merged, dr1b, (D_MODEL, D_MODEL), BF16, (4,),
        pl.BlockSpec((seq, 256), lambda m: (0, m)), whole(D_MODEL), pl.BlockSpec((256, D_MODEL), lambda m: (m, 0)))
    dw_ba = _tn_matmul(
        "dw_branch_attn", o, dya, (N_DEV, D_ATTN, LANE), BF16, (N_DEV,),
        whole(D_ATTN), pl.BlockSpec((seq, LANE), lambda n: (0, n)), pl.BlockSpec((1, D_ATTN, LANE), lambda n: (n, 0, 0)))
    dw_bp = _tn_matmul(
        "dw_branch_pool", ypre, dyp, (N_DEV, D_POOL, LANE), BF16, (N_DEV,),
        whole(D_POOL), pl.BlockSpec((seq, LANE), lambda n: (0, n)), pl.BlockSpec((1, D_POOL, LANE), lambda n: (n, 0, 0)))
    dw_pool = _tn_matmul(
        "dw_pool", pooled, dmixed, (len(POOL_WINDOWS), POOL_GROUP, POOL_GROUP), F32, (len(POOL_WINDOWS),),
        pl.BlockSpec((seq, POOL_GROUP), lambda n: (0, n)), pl.BlockSpec((seq, POOL_GROUP), lambda n: (0, n)),
        pl.BlockSpec((1, POOL_GROUP, POOL_GROUP), lambda n: (n, 0, 0)))
    (dq, dk, dv), (ffn_landed, mixer_landed) = _attn_bwd(
        q, k, v, kmean, o, lse, do, cos, sin,
        tasks=[_ChipScatter(ffn_chip), _DirectScatter([dw_ba, dw_bp, _by_owner(dw_out)])])
    grad_x, dz = _in_bwd(dq, dk, dv, dpooled, dzg, dr1, w_in_all)
    dw_in = _tn_matmul(
        "dw_in", xb, dz, (N_DEV, D_MODEL, D_ATTN), BF16, (N_DEV, 2),
        pl.BlockSpec((seq, 512), lambda n, m: (0, m)), pl.BlockSpec((seq, D_ATTN), lambda n, m: (0, n)),
        pl.BlockSpec((1, 512, D_ATTN), lambda n, m: (n, m, 0)))

    landed = dict(zip(FFN + MIXER[:3], ffn_landed + mixer_landed))
    little = {"b_gate": dbg, "w_pool": dw_pool, "pool_scale": dps, "ln1_g": dg1, "ln1_b": db1, "conv_b": dconv[3:4],
              "ln2_g": dg2, "ln2_b": db2, "conv_w": dconv[0:3]}
    return loss[0, 0], grad_x, landed, dw_in, little


def kernel(x, w_in, b_gate, w_branch_attn, w_pool, pool_scale, w_branch_pool, w_out, ln1_g, ln1_b, w_ffn_gate, w_ffn_up, conv_w, conv_b, w_ffn_down, ln2_g, ln2_b, loss_target, m_w_in, m_b_gate, m_w_branch_attn, m_w_pool, m_pool_scale, m_w_branch_pool, m_w_out, m_ln1_g, m_ln1_b, m_w_ffn_gate, m_w_ffn_up, m_conv_w, m_conv_b, m_w_ffn_down, m_ln2_g, m_ln2_b, v_w_in, v_b_gate, v_w_branch_attn, v_w_pool, v_pool_scale, v_w_branch_pool, v_w_out, v_ln1_g, v_ln1_b, v_w_ffn_gate, v_w_ffn_up, v_conv_w, v_conv_b, v_w_ffn_down, v_ln2_g, v_ln2_b):
    me = 4 * lax.axis_index("x") + 2 * lax.axis_index("y") + lax.axis_index("c")
    weights = dict(w_in=w_in, b_gate=b_gate, w_branch_attn=w_branch_attn, w_pool=w_pool, pool_scale=pool_scale,
                   w_branch_pool=w_branch_pool, w_out=w_out, ln1_g=ln1_g, ln1_b=ln1_b, w_ffn_gate=w_ffn_gate,
                   w_ffn_up=w_ffn_up, conv_w=conv_w, conv_b=conv_b, w_ffn_down=w_ffn_down, ln2_g=ln2_g, ln2_b=ln2_b)
    m_in = dict(w_in=m_w_in, b_gate=m_b_gate, w_branch_attn=m_w_branch_attn, w_pool=m_w_pool,
                pool_scale=m_pool_scale, w_branch_pool=m_w_branch_pool, w_out=m_w_out, ln1_g=m_ln1_g, ln1_b=m_ln1_b,
                w_ffn_gate=m_w_ffn_gate, w_ffn_up=m_w_ffn_up, conv_w=m_conv_w, conv_b=m_conv_b,
                w_ffn_down=m_w_ffn_down, ln2_g=m_ln2_g, ln2_b=m_ln2_b)
    v_in = dict(w_in=v_w_in, b_gate=v_b_gate, w_branch_attn=v_w_branch_attn, w_pool=v_w_pool,
                pool_scale=v_pool_scale, w_branch_pool=v_w_branch_pool, w_out=v_w_out, ln1_g=v_ln1_g, ln1_b=v_ln1_b,
                w_ffn_gate=v_w_ffn_gate, w_ffn_up=v_w_ffn_up, conv_w=v_conv_w, conv_b=v_conv_b,
                w_ffn_down=v_w_ffn_down, ln2_g=v_ln2_g, ln2_b=v_ln2_b)
    weights = {n: a[0] for n, a in weights.items()}
    m_in = {n: a[0] for n, a in m_in.items()}
    v_in = {n: a[0] for n, a in v_in.items()}

    shards = {"w_in": weights["w_in"].astype(BF16), "w_branch_attn": weights["w_branch_attn"].astype(BF16),
              "w_branch_pool": weights["w_branch_pool"].astype(BF16), "w_out": weights["w_out"].astype(BF16),
              "w_ffn_gate_t": weights["w_ffn_gate"].T.astype(BF16), "w_ffn_up_t": weights["w_ffn_up"].T.astype(BF16),
              "w_ffn_down": weights["w_ffn_down"].astype(BF16), "conv_w": weights["conv_w"]}
    small = {"b_gate": weights["b_gate"][None], "w_pool": weights["w_pool"], "pool_scale": weights["pool_scale"][None],
             "ln1_g": weights["ln1_g"][None], "ln1_b": weights["ln1_b"][None], "conv_b": weights["conv_b"][None],
             "ln2_g": weights["ln2_g"][None], "ln2_b": weights["ln2_b"][None]}

    loss_part, grad_x, landed, dw_in, little = _local_step(x[0], loss_target[0], shards, small)
    loss = lax.psum(loss_part, ("x", "y", "c"))

    ((w_in_sibling,),) = _exchange("sibling_grads", [_SiblingSend([dw_in])])
    w_in_chip = _reduce_halves(["w_in"], [dw_in], [w_in_sibling])
    names = SMALL + ("conv_w",)
    (landed["w_in"],), (all_small,) = _exchange(
        "scatter_grads", [_ChipScatter(w_in_chip), _AllGather([_pack([little[n] for n in names])])])

    grads = {n: _sum_leading("sum_" + n, t) for n, t in landed.items()}
    grads["w_ffn_gate"] = grads.pop("w_ffn_gate_t").T
    grads["w_ffn_up"] = grads.pop("w_ffn_up_t").T
    small_sum = _sum_leading("sum_small", all_small)
    *small_grads, conv_w_grad = _unpack(small_sum, [weights[n].shape for n in SMALL] + [(3, D_FF)])
    grads.update(zip(SMALL, small_grads))
    grads["conv_w"] = lax.dynamic_slice(conv_w_grad, (0, me * FF_SHARD), (3, FF_SHARD))

    delta, new_m, new_v = {}, {}, {}
    for n in ("w_in", "w_branch_attn", "w_branch_pool", "w_out", "w_ffn_gate", "w_ffn_up", "w_ffn_down"):
        delta[n], new_m[n], new_v[n] = _adamw("adamw_" + n, weights[n], grads[n], m_in[n], v_in[n])
    flat = lambda d: _pack([d[n] for n in names])
    shapes = [weights[n].shape for n in names]
    for out, packed in zip((delta, new_m, new_v),
                           _adamw("adamw_small", flat(weights), flat(grads), flat(m_in), flat(v_in))):
        out.update(zip(names, _unpack(packed, shapes)))

    order = ("w_in", "b_gate", "w_branch_attn", "w_pool", "pool_scale", "w_branch_pool", "w_out", "ln1_g", "ln1_b",
             "w_ffn_gate", "w_ffn_up", "conv_w", "conv_b", "w_ffn_down", "ln2_g", "ln2_b")
    lead = lambda t: t[None]
    return (loss, lead(grad_x), *[lead(grads[n]) for n in order], *[lead(delta[n]) for n in order],
            *[lead(new_m[n]) for n in order], *[lead(new_v[n]) for n in order])
```

```python
import functools
import math

import jax
import jax.numpy as jnp
from jax import lax
from jax.experimental import pallas as pl
from jax.experimental.pallas import tpu as pltpu

F32 = jnp.float32
BF16 = jnp.bfloat16

D_MODEL = 1024
N_HEADS = 8
HEAD_DIM = 64
D_ATTN = N_HEADS * HEAD_DIM
MOBA_BLOCK = 256
MOBA_TOPK = 3
ROPE_THETA = 10000.0
POOL_WINDOWS = (2, 4, 8, 16)
POOL_GROUP = 128
D_POOL = len(POOL_WINDOWS) * POOL_GROUP
POOL_HALO = 16
D_FF = 2816
D_IN_PROJ = 3 * D_ATTN + D_POOL + 2 * D_MODEL
LN_EPS = 1e-5
ALPHA = 2.0 ** 0.25
NEG = -1e30
N_DEV = 8
FF_SHARD = D_FF // N_DEV

ADAM_LR = 0.001
ADAM_B1 = 0.9
ADAM_B2 = 0.999
ADAM_EPS = 1e-08
ADAM_WD = 0.01
ADAM_STEP = 10

TOK = 256
FF_CHUNK = 256
LANE = 128
VMEM_LIMIT = 56 * 1024 * 1024

MESH = pl.DeviceIdType.MESH
NT_DIMS = (((1,), (1,)), ((), ()))
TN_DIMS = (((0,), (0,)), ((), ()))


def _params(*sem):
    return pltpu.CompilerParams(dimension_semantics=sem or None, vmem_limit_bytes=VMEM_LIMIT)


def _full(shape):
    zeros = (0,) * len(shape)
    return pl.BlockSpec(shape, lambda *_: zeros, pipeline_mode=pl.Buffered(1))


def _rows(width, tile=TOK):
    return pl.BlockSpec((tile, width), lambda i: (i, 0))


def _sds(shape, dtype):
    return jax.ShapeDtypeStruct(shape, dtype)


def _dot(a, b):
    return jnp.dot(a, b, preferred_element_type=F32)


def _dot_nt(a, b):
    return lax.dot_general(a, b, NT_DIMS, preferred_element_type=F32)


def _dot_tn(a, b):
    return lax.dot_general(a, b, TN_DIMS, preferred_element_type=F32)


def _rope_tables(seq):
    half = HEAD_DIM // 2
    inv_freq = 1.0 / (ROPE_THETA ** (jnp.arange(half, dtype=F32) / half))
    ang = jnp.arange(seq, dtype=F32)[:, None] * inv_freq[None, :]
    cos, sin = jnp.cos(ang), jnp.sin(ang)
    return jnp.tile(cos, (1, 4)), jnp.tile(jnp.concatenate([-sin, sin], axis=1), (1, 2))


def _swap_halves(t):
    lane = lax.broadcasted_iota(jnp.int32, t.shape, 1)
    return jnp.where((lane % HEAD_DIM) < HEAD_DIM // 2, pltpu.roll(t, LANE - 32, 1), pltpu.roll(t, 32, 1))


def _rope(t, cos, sin):
    return t * cos + _swap_halves(t) * sin


def _rope_transposed(g, cos, sin):
    return g * cos + _swap_halves(g * sin)


def _ln_fwd(r, g, b):
    mu = jnp.mean(r, axis=-1, keepdims=True)
    xc = r - mu
    var = jnp.mean(xc * xc, axis=-1, keepdims=True)
    rstd = lax.rsqrt(var + LN_EPS)
    xhat = xc * rstd
    return xhat * g + b, xhat, rstd


def _ln_bwd(dy, xhat, rstd, g):
    dxh = dy * g
    m1 = jnp.mean(dxh, axis=-1, keepdims=True)
    m2 = jnp.mean(dxh * xhat, axis=-1, keepdims=True)
    return rstd * (dxh - m1 - xhat * m2)


def _gelu_parts(a):
    cdf = 0.5 * (1.0 + lax.erf(a * (1.0 / math.sqrt(2.0))))
    pdf = jnp.exp(-0.5 * a * a) * (1.0 / math.sqrt(2.0 * math.pi))
    return a * cdf, cdf + a * pdf


def _shift_down(a, k):
    row = lax.broadcasted_iota(jnp.int32, a.shape, 0)
    return jnp.where(row >= k, pltpu.roll(a, k, 0), 0.0)


def _shift_up(a, k):
    n = a.shape[0]
    row = lax.broadcasted_iota(jnp.int32, a.shape, 0)
    return jnp.where(row < n - k, pltpu.roll(a, n - k, 0), 0.0)


def _conv(a, cw, cb):
    return cw[2:3, :] * a + cw[1:2, :] * _shift_down(a, 1) + cw[0:1, :] * _shift_down(a, 2) + cb


def _pool_count(first_row, rows, window):
    t = first_row + lax.broadcasted_iota(jnp.int32, (rows, 1), 0)
    return jnp.minimum(t + 1, window).astype(F32)


def _grid_call(body, name, steps, in_specs, out_specs, out_shape, operands, scratch=(), tasks=()):
    t_operands, t_in_specs, t_out_shape, t_out_specs, t_sems = _task_args(tasks)
    outs = pl.pallas_call(
        _carry(body, tasks, len(in_specs), len(out_specs), len(scratch), steps), name=name, grid=(steps,),
        in_specs=list(in_specs) + t_in_specs, out_specs=list(out_specs) + t_out_specs,
        out_shape=list(out_shape) + t_out_shape, scratch_shapes=list(scratch) + t_sems,
        compiler_params=_params("arbitrary"),
    )(*operands, *t_operands)
    return outs[:len(out_specs)], _task_results(tasks, outs[len(out_specs):])


def _proj_in(x, win, b_gate, cos, sin, tasks=()):
    seq = x.shape[0]
    nt = seq // TOK

    def body(x_ref, win_ref, bg_ref, cos_ref, sin_ref, xb_ref, q_ref, k_ref, v_ref, u_ref, g_ref, km_ref):
        xb = x_ref[...].astype(BF16)
        xb_ref[...] = xb
        cos_t, sin_t = cos_ref[...], sin_ref[...]
        for sec, out_ref in ((0, q_ref), (1, k_ref)):
            z = _dot(xb, win_ref[sec])
            for c in range(D_ATTN // LANE):
                cols = slice(LANE * c, LANE * (c + 1))
                out_ref[:, cols] = _rope(z[:, cols], cos_t, sin_t)
        km_ref[0] = jnp.mean(k_ref[...], axis=0, keepdims=True)
        v_ref[...] = _dot(xb, win_ref[2]).astype(BF16)
        u_ref[...] = _dot(xb, win_ref[3])
        for n in range(4):
            cols = slice(D_ATTN * n, D_ATTN * (n + 1))
            g_ref[:, cols] = jax.nn.sigmoid(_dot(xb, win_ref[4 + n]) + bg_ref[:, cols])

    return _grid_call(
        body, "proj_in", nt,
        in_specs=[_rows(D_MODEL), _full(win.shape), _full((1, 2 * D_MODEL)), _rows(LANE), _rows(LANE)],
        out_specs=[_rows(D_MODEL), _rows(D_ATTN), _rows(D_ATTN), _rows(D_ATTN), _rows(D_POOL), _rows(2 * D_MODEL),
                   pl.BlockSpec((1, 1, D_ATTN), lambda i: (i, 0, 0))],
        out_shape=[_sds((seq, D_MODEL), BF16), _sds((seq, D_ATTN), F32), _sds((seq, D_ATTN), F32),
                   _sds((seq, D_ATTN), BF16), _sds((seq, D_POOL), F32), _sds((seq, 2 * D_MODEL), F32),
                   _sds((nt, 1, D_ATTN), F32)],
        operands=(x, win, b_gate, cos, sin), tasks=tasks)


SCORE_CHUNK = 128


def _store_keys(ka_sc, k_ref, ls):
    seq = ka_sc.shape[0]
    ka_sc[:, 0:HEAD_DIM] = k_ref[:, ls].astype(BF16)
    row = lax.broadcasted_iota(jnp.int32, (seq, HEAD_DIM), 0)
    lane = lax.broadcasted_iota(jnp.int32, (seq, HEAD_DIM), 1)
    in_block = (lane * MOBA_BLOCK <= row) & (row < (lane + 1) * MOBA_BLOCK)
    ka_sc[:, HEAD_DIM:] = jnp.where(in_block, 1.0, 0.0).astype(BF16)


def _block_bias(qf, km, i):
    if i <= MOBA_TOPK:
        return jnp.zeros((MOBA_BLOCK, HEAD_DIM), BF16)
    nb = km.shape[0]
    gate = lax.dot_general(km, qf, NT_DIMS, precision=lax.Precision.HIGHEST, preferred_element_type=F32)
    blk = lax.broadcasted_iota(jnp.int32, gate.shape, 0)
    rank = jnp.zeros(gate.shape, F32)
    for r in range(1, i):
        lower = pltpu.roll(gate, r, 0)
        rank = rank + jnp.where((blk >= r) & (lower >= gate), 1.0, 0.0)
        higher = pltpu.roll(gate, nb - r, 0)
        rank = rank + jnp.where((blk + r < i) & (higher > gate), 1.0, 0.0)
    bias = jnp.where((blk < i) & (rank >= MOBA_TOPK), NEG, 0.0)
    padded = jnp.concatenate([bias, jnp.zeros((LANE - nb, MOBA_BLOCK), F32)], axis=0)
    return jnp.transpose(padded)[:, 0:HEAD_DIM].astype(BF16)


def _causal(shape, transposed=False):
    row = lax.broadcasted_iota(jnp.int32, shape, 0)
    col = lax.broadcasted_iota(jnp.int32, shape, 1)
    return (row <= col) if transposed else (col <= row)


def _row_vector(col):
    return jnp.transpose(jnp.broadcast_to(col, (MOBA_BLOCK, LANE)))[0:1, :]


def _attn_fwd(q, k, v, kmean, tasks=()):
    seq = q.shape[0]
    nb = seq // MOBA_BLOCK
    assert nb == 8, "the block ranking keeps one sublane per key block"
    pair = pl.BlockSpec((seq, LANE), lambda p: (0, p))
    heads = LANE // HEAD_DIM

    def body(q_ref, k_ref, v_ref, km_ref, o_ref, lse_ref, bias_ref, ka_sc, qa_sc, s_sc, p_sc):
        lse_ref[0, heads:, :] = jnp.zeros((8 - heads, seq), F32)
        for hh in range(heads):
            ls = slice(HEAD_DIM * hh, HEAD_DIM * (hh + 1))
            _store_keys(ka_sc, k_ref, ls)
            vb = v_ref[:, ls]
            km = km_ref[:, ls]
            for i in range(nb):
                rs = slice(MOBA_BLOCK * i, MOBA_BLOCK * (i + 1))
                width = MOBA_BLOCK * (i + 1)
                qf = q_ref[rs, ls]
                bias = _block_bias(qf, km, i)
                bias_ref[rs, ls] = bias
                qa_sc[:, 0:HEAD_DIM] = (qf * HEAD_DIM ** -0.5).astype(BF16)
                qa_sc[:, HEAD_DIM:] = bias
                s_sc[:, 0:width] = _dot_nt(qa_sc[...], ka_sc[0:width, :])
                s_sc[:, rs] = jnp.where(_causal((MOBA_BLOCK, MOBA_BLOCK)), s_sc[:, rs], NEG)
                chunks = [slice(SCORE_CHUNK * c, SCORE_CHUNK * (c + 1)) for c in range(width // SCORE_CHUNK)]
                top = s_sc[:, chunks[0]]
                for c in chunks[1:]:
                    top = jnp.maximum(top, s_sc[:, c])
                m = jnp.max(top, axis=1, keepdims=True)
                total = jnp.zeros((MOBA_BLOCK, SCORE_CHUNK), F32)
                for c in chunks:
                    p = jnp.exp(s_sc[:, c] - m)
                    total = total + p
                    p_sc[:, c] = p.astype(BF16)
                l = jnp.sum(total, axis=1, keepdims=True)
                o_ref[rs, ls] = _dot(p_sc[:, 0:width], vb[0:width]) / l
                lse_ref[0, hh:hh + 1, rs] = _row_vector(m + jnp.log(l))

    return _grid_call(
        body, "attn_fwd", D_ATTN // LANE,
        in_specs=[pair, pair, pair, pl.BlockSpec((nb, LANE), lambda p: (0, p))],
        out_specs=[pair, pl.BlockSpec((1, 8, seq), lambda p: (p, 0, 0)), pair],
        out_shape=[_sds((seq, D_ATTN), F32), _sds((D_ATTN // LANE, 8, seq), F32), _sds((seq, D_ATTN), BF16)],
        operands=(q, k, v, kmean),
        scratch=[pltpu.VMEM((seq, LANE), BF16), pltpu.VMEM((MOBA_BLOCK, LANE), BF16),
                 pltpu.VMEM((MOBA_BLOCK, seq), F32), pltpu.VMEM((MOBA_BLOCK, seq), BF16)],
        tasks=tasks)


def _mix(o, u, g, x, wba, wbp, wout, w_pool, pool_scale, ln_g, ln_b, tasks=()):
    seq = x.shape[0]

    def body(o_ref, u_ref, uprev_ref, g_ref, x_ref, wba_ref, wbp_ref, wout_ref, wp_ref, ps_ref, lg_ref, lb_ref,
             ya_ref, yp_ref, pooled_ref, mixed_ref, ypre_ref, merged_ref, xhat_ref, rstd_ref, h_ref, hb_ref, ext):
        i = pl.program_id(0)
        ya = _dot(o_ref[...].astype(BF16), wba_ref[...])
        ucur = u_ref[...]
        ext[0:POOL_HALO, :] = jnp.where(i > 0, uprev_ref[...], 0.0)
        ext[POOL_HALO:, :] = ucur
        for grp, window in enumerate(POOL_WINDOWS):
            cols = slice(POOL_GROUP * grp, POOL_GROUP * (grp + 1))
            acc = ucur[:, cols]
            for kk in range(1, window):
                acc = acc + ext[pl.ds(POOL_HALO - kk, TOK), cols]
            pooled = acc / _pool_count(i * TOK, TOK, window) - ucur[:, cols]
            pooled_ref[:, cols] = pooled.astype(BF16)
            mixed_ref[:, cols] = _dot(pooled.astype(BF16), wp_ref[grp].astype(BF16))
        mixed = mixed_ref[...]
        ypre = (mixed * ps_ref[...]).astype(BF16)
        ypre_ref[...] = ypre
        yp = _dot(ypre, wbp_ref[...])
        ya_ref[...] = ya
        yp_ref[...] = yp
        merged = (g_ref[:, :D_MODEL] * ya + g_ref[:, D_MODEL:] * yp).astype(BF16)
        merged_ref[...] = merged
        r1 = ALPHA * x_ref[...] + _dot(merged, wout_ref[...])
        h, xhat, rstd = _ln_fwd(r1, lg_ref[...], lb_ref[...])
        xhat_ref[...] = xhat
        rstd_ref[...] = jnp.broadcast_to(rstd, (TOK, LANE))
        h_ref[...] = h
        hb_ref[...] = h.astype(BF16)

    halo = pl.BlockSpec((POOL_HALO, D_POOL), lambda i: (jnp.maximum(i * (TOK // POOL_HALO) - 1, 0), 0))
    return _grid_call(
        body, "mix", seq // TOK,
        in_specs=[_rows(D_ATTN), _rows(D_POOL), halo, _rows(2 * D_MODEL), _rows(D_MODEL),
                  _full(wba.shape), _full(wbp.shape), _full(wout.shape), _full(w_pool.shape),
                  _full((1, D_POOL)), _full((1, D_MODEL)), _full((1, D_MODEL))],
        out_specs=[_rows(D_MODEL), _rows(D_MODEL), _rows(D_POOL), _rows(D_POOL), _rows(D_POOL), _rows(D_MODEL),
                   _rows(D_MODEL), _rows(LANE), _rows(D_MODEL), _rows(D_MODEL)],
        out_shape=[_sds((seq, D_MODEL), F32), _sds((seq, D_MODEL), F32), _sds((seq, D_POOL), BF16),
                   _sds((seq, D_POOL), F32), _sds((seq, D_POOL), BF16), _sds((seq, D_MODEL), BF16),
                   _sds((seq, D_MODEL), F32), _sds((seq, LANE), F32), _sds((seq, D_MODEL), F32),
                   _sds((seq, D_MODEL), BF16)],
        operands=(o, u, u, g, x, wba, wbp, wout, w_pool, pool_scale, ln_g, ln_b),
        scratch=[pltpu.VMEM((TOK + POOL_HALO, D_POOL), F32)], tasks=tasks)


def _ffn_up(hb, wgt, wut, conv_w, conv_b):
    seq = hb.shape[0]
    wblk = pl.BlockSpec((FF_CHUNK, D_MODEL), lambda c: (c, 0))
    cblk = lambda rows: pl.BlockSpec((rows, FF_CHUNK), lambda c: (0, c))
    oblk = pl.BlockSpec((seq, FF_CHUNK), lambda c: (0, c))

    def body(h_ref, wg_ref, wu_ref, cw_ref, cb_ref, a_ref, u_ref, act_ref):
        h = h_ref[...]
        a = _dot_nt(h, wg_ref[...])
        u = _dot_nt(h, wu_ref[...])
        a_ref[...] = a
        u_ref[...] = u
        gelu, _ = _gelu_parts(_conv(a, cw_ref[...], cb_ref[...]))
        act_ref[...] = (gelu * u).astype(BF16)

    return pl.pallas_call(
        body, name="ffn_up", grid=(D_FF // FF_CHUNK,),
        in_specs=[_full(hb.shape), wblk, wblk, cblk(3), cblk(1)],
        out_specs=[oblk, oblk, oblk],
        out_shape=[_sds((seq, D_FF), F32), _sds((seq, D_FF), F32), _sds((seq, D_FF), BF16)],
        compiler_params=_params("parallel"),
    )(hb, wgt, wut, conv_w, conv_b)


def _ffn_down(act, wd, h, target, ln_g, ln_b):
    seq = h.shape[0]

    def body(act_ref, wd_ref, h_ref, t_ref, lg_ref, lb_ref, dr_ref, drb_ref, loss_ref, dg_ref, db_ref):
        i = pl.program_id(0)

        @pl.when(i == 0)
        def _():
            loss_ref[...] = jnp.zeros_like(loss_ref)
            dg_ref[...] = jnp.zeros_like(dg_ref)
            db_ref[...] = jnp.zeros_like(db_ref)

        r2 = ALPHA * h_ref[...] + _dot(act_ref[...], wd_ref[...])
        y, xhat, rstd = _ln_fwd(r2, lg_ref[...], lb_ref[...])
        diff = y - t_ref[...]
        loss_ref[...] += jnp.sum(diff * diff) * (0.5 / D_MODEL)
        dy = diff * (1.0 / D_MODEL)
        dg_ref[...] += jnp.sum(dy * xhat, axis=0, keepdims=True)
        db_ref[...] += jnp.sum(dy, axis=0, keepdims=True)
        dr = _ln_bwd(dy, xhat, rstd, lg_ref[...])
        dr_ref[...] = dr
        drb_ref[...] = dr.astype(BF16)

    vec = pl.BlockSpec((1, D_MODEL), lambda i: (0, 0))
    return pl.pallas_call(
        body, name="ffn_down", grid=(seq // TOK,),
        in_specs=[_rows(D_FF), _full(wd.shape), _rows(D_MODEL), _rows(D_MODEL), _full((1, D_MODEL)), _full((1, D_MODEL))],
        out_specs=[_rows(D_MODEL), _rows(D_MODEL), pl.BlockSpec((8, LANE), lambda i: (0, 0)), vec, vec],
        out_shape=[_sds((seq, D_MODEL), F32), _sds((seq, D_MODEL), BF16), _sds((8, LANE), F32),
                   _sds((1, D_MODEL), F32), _sds((1, D_MODEL), F32)],
        compiler_params=_params("arbitrary"),
    )(act, wd, h, target, ln_g, ln_b)


def _ffn_bwd(drb, hb, a, u, wd, conv_w, conv_b):
    seq = hb.shape[0]
    wblk = pl.BlockSpec((FF_CHUNK, D_MODEL), lambda c: (c, 0))
    cblk = lambda rows: pl.BlockSpec((rows, FF_CHUNK), lambda c: (0, c))
    sblk = pl.BlockSpec((seq, FF_CHUNK), lambda c: (0, c))

    def body(dr_ref, h_ref, a_ref, u_ref, wd_ref, cw_ref, cb_ref, da_ref, du_ref, dwd_ref, dwg_ref, dwu_ref, dc_ref):
        dr = dr_ref[...]
        h = h_ref[...]
        a = a_ref[...]
        u = u_ref[...]
        cw = cw_ref[...]
        dact = _dot_nt(dr, wd_ref[...])
        gelu, dgelu = _gelu_parts(_conv(a, cw, cb_ref[...]))
        dwd_ref[...] = _dot_tn((gelu * u).astype(BF16), dr).astype(BF16)
        du = (dact * gelu).astype(BF16)
        dac = dact * u * dgelu
        da = (cw[2:3, :] * dac + cw[1:2, :] * _shift_up(dac, 1) + cw[0:1, :] * _shift_up(dac, 2)).astype(BF16)
        da_ref[...] = da
        du_ref[...] = du
        dwg_ref[...] = _dot_tn(da, h).astype(BF16)
        dwu_ref[...] = _dot_tn(du, h).astype(BF16)
        dc_ref[0:1, :] = jnp.sum(dac * _shift_down(a, 2), axis=0, keepdims=True)
        dc_ref[1:2, :] = jnp.sum(dac * _shift_down(a, 1), axis=0, keepdims=True)
        dc_ref[2:3, :] = jnp.sum(dac * a, axis=0, keepdims=True)
        dc_ref[3:4, :] = jnp.sum(dac, axis=0, keepdims=True)
        dc_ref[4:8, :] = jnp.zeros((4, FF_CHUNK), F32)

    return pl.pallas_call(
        body, name="ffn_bwd", grid=(D_FF // FF_CHUNK,),
        in_specs=[_full(drb.shape), _full(hb.shape), sblk, sblk, wblk, cblk(3), cblk(1)],
        out_specs=[sblk, sblk, wblk, wblk, wblk, cblk(8)],
        out_shape=[_sds((seq, D_FF), BF16), _sds((seq, D_FF), BF16), _sds((D_FF, D_MODEL), BF16),
                   _sds((D_FF, D_MODEL), BF16), _sds((D_FF, D_MODEL), BF16), _sds((8, D_FF), F32)],
        compiler_params=_params("parallel"),
    )(drb, hb, a, u, wd, conv_w, conv_b)


def _ln1_bwd(dr2, da, du, wgt, wut, xhat, rstd, ln_g, tasks=()):
    seq = dr2.shape[0]

    def body(dr2_ref, da_ref, du_ref, wg_ref, wu_ref, xhat_ref, rstd_ref, lg_ref, dr_ref, drb_ref, dg_ref, db_ref):
        @pl.when(pl.program_id(0) == 0)
        def _():
            dg_ref[...] = jnp.zeros_like(dg_ref)
            db_ref[...] = jnp.zeros_like(db_ref)

        dh = ALPHA * dr2_ref[...] + _dot(da_ref[...], wg_ref[...]) + _dot(du_ref[...], wu_ref[...])
        xhat = xhat_ref[...]
        dg_ref[...] += jnp.sum(dh * xhat, axis=0, keepdims=True)
        db_ref[...] += jnp.sum(dh, axis=0, keepdims=True)
        dr = _ln_bwd(dh, xhat, rstd_ref[:, 0:1], lg_ref[...])
        dr_ref[...] = dr
        drb_ref[...] = dr.astype(BF16)

    vec = pl.BlockSpec((1, D_MODEL), lambda i: (0, 0))
    return _grid_call(
        body, "ln1_bwd", seq // TOK,
        in_specs=[_rows(D_MODEL), _rows(D_FF), _rows(D_FF), _full(wgt.shape), _full(wut.shape), _rows(D_MODEL),
                  _rows(LANE), _full((1, D_MODEL))],
        out_specs=[_rows(D_MODEL), _rows(D_MODEL), vec, vec],
        out_shape=[_sds((seq, D_MODEL), F32), _sds((seq, D_MODEL), BF16), _sds((1, D_MODEL), F32),
                   _sds((1, D_MODEL), F32)],
        operands=(dr2, da, du, wgt, wut, xhat, rstd, ln_g), tasks=tasks)


def _mix_bwd(drb, ya, yp, g, mixed, wout, wba, wbp, w_pool, pool_scale, tasks=()):
    seq = drb.shape[0]

    def body(dr_ref, ya_ref, yp_ref, g_ref, mixed_ref, wout_ref, wba_ref, wbp_ref, wp_ref, ps_ref,
             dzg_ref, dya_ref, dyp_ref, do_ref, dmixed_ref, dpooled_ref, dbg_ref, dps_ref):
        @pl.when(pl.program_id(0) == 0)
        def _():
            dbg_ref[...] = jnp.zeros_like(dbg_ref)
            dps_ref[...] = jnp.zeros_like(dps_ref)

        dmerged = _dot_nt(dr_ref[...], wout_ref[...])
        ga, gp = g_ref[:, :D_MODEL], g_ref[:, D_MODEL:]
        dzga = dmerged * ya_ref[...] * ga * (1.0 - ga)
        dzgp = dmerged * yp_ref[...] * gp * (1.0 - gp)
        dzg_ref[:, :D_MODEL] = dzga.astype(BF16)
        dzg_ref[:, D_MODEL:] = dzgp.astype(BF16)
        dbg_ref[:, :D_MODEL] += jnp.sum(dzga, axis=0, keepdims=True)
        dbg_ref[:, D_MODEL:] += jnp.sum(dzgp, axis=0, keepdims=True)
        dya = (dmerged * ga).astype(BF16)
        dyp = (dmerged * gp).astype(BF16)
        dya_ref[...] = dya
        dyp_ref[...] = dyp
        do_ref[...] = _dot_nt(dya, wba_ref[...])
        dypre = _dot_nt(dyp, wbp_ref[...])
        dps_ref[...] += jnp.sum(dypre * mixed_ref[...], axis=0, keepdims=True)
        dmixed = (dypre * ps_ref[...]).astype(BF16)
        dmixed_ref[...] = dmixed
        for grp in range(len(POOL_WINDOWS)):
            cols = slice(POOL_GROUP * grp, POOL_GROUP * (grp + 1))
            dpooled_ref[:, cols] = _dot_nt(dmixed[:, cols], wp_ref[grp].astype(BF16))

    return _grid_call(
        body, "mix_bwd", seq // TOK,
        in_specs=[_rows(D_MODEL), _rows(D_MODEL), _rows(D_MODEL), _rows(2 * D_MODEL), _rows(D_POOL),
                  _full(wout.shape), _full(wba.shape), _full(wbp.shape), _full(w_pool.shape), _full((1, D_POOL))],
        out_specs=[_rows(2 * D_MODEL), _rows(D_MODEL), _rows(D_MODEL), _rows(D_ATTN), _rows(D_POOL), _rows(D_POOL),
                   pl.BlockSpec((1, 2 * D_MODEL), lambda i: (0, 0)), pl.BlockSpec((1, D_POOL), lambda i: (0, 0))],
        out_shape=[_sds((seq, 2 * D_MODEL), BF16), _sds((seq, D_MODEL), BF16), _sds((seq, D_MODEL), BF16),
                   _sds((seq, D_ATTN), F32), _sds((seq, D_POOL), BF16), _sds((seq, D_POOL), F32),
                   _sds((1, 2 * D_MODEL), F32), _sds((1, D_POOL), F32)],
        operands=(drb, ya, yp, g, mixed, wout, wba, wbp, w_pool, pool_scale), tasks=tasks)


def _attn_bwd(q, k, v, bias, o, lse, do, cos, sin, tasks=()):
    seq = q.shape[0]
    nb = seq // MOBA_BLOCK
    pair = pl.BlockSpec((seq, LANE), lambda p: (0, p))
    table = pl.BlockSpec((seq, LANE), lambda p: (0, 0))
    scale = HEAD_DIM ** -0.5

    def body(q_ref, k_ref, v_ref, bias_ref, o_ref, lse_ref, do_ref, cos_ref, sin_ref, dq_ref, dk_ref, dv_ref,
             dq_acc, dk_acc, dv_acc, dk_head, dv_head, ka_sc, qa_sc, s_sc, dp_sc, p_sc, ds_sc):
        for hh in range(LANE // HEAD_DIM):
            ls = slice(HEAD_DIM * hh, HEAD_DIM * (hh + 1))
            _store_keys(ka_sc, k_ref, ls)
            vb = v_ref[:, ls]
            dk_head[...] = jnp.zeros_like(dk_head)
            dv_head[...] = jnp.zeros_like(dv_head)
            for i in range(nb):
                rs = slice(MOBA_BLOCK * i, MOBA_BLOCK * (i + 1))
                width = MOBA_BLOCK * (i + 1)
                qa_sc[:, 0:HEAD_DIM] = (q_ref[rs, ls] * scale).astype(BF16)
                qa_sc[:, HEAD_DIM:] = bias_ref[rs, ls]
                s_sc[0:width, :] = _dot_nt(ka_sc[0:width, :], qa_sc[...])
                s_sc[rs, :] = jnp.where(_causal((MOBA_BLOCK, MOBA_BLOCK), transposed=True), s_sc[rs, :], NEG)
                dob = do_ref[rs, ls]
                delta = _row_vector(jnp.sum(dob * o_ref[rs, ls], axis=1, keepdims=True))
                lse_row = lse_ref[0, hh:hh + 1, rs]
                dob16 = dob.astype(BF16)
                dp_sc[0:width, :] = _dot_nt(vb[0:width], dob16)
                for c in range(width // SCORE_CHUNK):
                    rows = slice(SCORE_CHUNK * c, SCORE_CHUNK * (c + 1))
                    p = jnp.exp(s_sc[rows, :] - lse_row)
                    p_sc[rows, :] = p.astype(BF16)
                    ds_sc[rows, :] = (p * (dp_sc[rows, :] - delta)).astype(BF16)
                dv_head[0:width, :] += _dot(p_sc[0:width, :], dob16)
                dk_head[0:width, :] += _dot(ds_sc[0:width, :], qa_sc[:, 0:HEAD_DIM])
                dq_acc[rs, ls] = _dot_tn(ds_sc[0:width, :], ka_sc[0:width, 0:HEAD_DIM]) * scale
            dk_acc[:, ls] = dk_head[...]
            dv_acc[:, ls] = dv_head[...]
        cos_t, sin_t = cos_ref[...], sin_ref[...]
        dq_ref[...] = _rope_transposed(dq_acc[...], cos_t, sin_t).astype(BF16)
        dk_ref[...] = _rope_transposed(dk_acc[...], cos_t, sin_t).astype(BF16)
        dv_ref[...] = dv_acc[...].astype(BF16)

    return _grid_call(
        body, "attn_bwd", D_ATTN // LANE,
        in_specs=[pair, pair, pair, pair, pair, pl.BlockSpec((1, 8, seq), lambda p: (p, 0, 0)), pair, table, table],
        out_specs=[pair, pair, pair], out_shape=[_sds((seq, D_ATTN), BF16)] * 3,
        operands=(q, k, v, bias, o, lse, do, cos, sin),
        scratch=[pltpu.VMEM((seq, LANE), F32)] * 3 + [pltpu.VMEM((seq, HEAD_DIM), F32)] * 2
        + [pltpu.VMEM((seq, LANE), BF16), pltpu.VMEM((MOBA_BLOCK, LANE), BF16)]
        + [pltpu.VMEM((seq, MOBA_BLOCK), F32)] * 2 + [pltpu.VMEM((seq, MOBA_BLOCK), BF16)] * 2,
        tasks=tasks)


def _in_bwd(dq, dk, dv, dpooled, dzg, dr1, win, tasks=()):
    seq = dr1.shape[0]
    nt = seq // TOK

    def body(dq_ref, dk_ref, dv_ref, dp_ref, dpnext_ref, dzg_ref, dr_ref, win_ref, dx_ref, dz_ref, ext):
        i = pl.program_id(0)
        dp = dp_ref[...]
        dpn = jnp.where(i < nt - 1, dpnext_ref[...], 0.0)
        for grp, window in enumerate(POOL_WINDOWS):
            cols = slice(POOL_GROUP * grp, POOL_GROUP * (grp + 1))
            ext[0:TOK, cols] = dp[:, cols] / _pool_count(i * TOK, TOK, window)
            ext[TOK:, cols] = dpn[:, cols] / _pool_count((i + 1) * TOK, POOL_HALO, window)
        for grp, window in enumerate(POOL_WINDOWS):
            cols = slice(POOL_GROUP * grp, POOL_GROUP * (grp + 1))
            acc = ext[0:TOK, cols] - dp[:, cols]
            for kk in range(1, window):
                acc = acc + ext[pl.ds(kk, TOK), cols]
            dz_ref[:, 3 * D_ATTN + POOL_GROUP * grp:3 * D_ATTN + POOL_GROUP * (grp + 1)] = acc.astype(BF16)
        dz_ref[:, 0:D_ATTN] = dq_ref[...]
        dz_ref[:, D_ATTN:2 * D_ATTN] = dk_ref[...]
        dz_ref[:, 2 * D_ATTN:3 * D_ATTN] = dv_ref[...]
        dz_ref[:, 3 * D_ATTN + D_POOL:] = dzg_ref[...]
        dx = ALPHA * dr_ref[...]
        for n in range(N_DEV):
            dx = dx + _dot_nt(dz_ref[:, D_ATTN * n:D_ATTN * (n + 1)], win_ref[n])
        dx_ref[...] = dx

    halo = pl.BlockSpec((POOL_HALO, D_POOL),
                        lambda i: (jnp.minimum((i + 1) * (TOK // POOL_HALO), seq // POOL_HALO - 1), 0))
    return _grid_call(
        body, "in_bwd", nt,
        in_specs=[_rows(D_ATTN), _rows(D_ATTN), _rows(D_ATTN), _rows(D_POOL), halo, _rows(2 * D_MODEL),
                  _rows(D_MODEL), _full(win.shape)],
        out_specs=[_rows(D_MODEL), _rows(D_IN_PROJ)],
        out_shape=[_sds((seq, D_MODEL), F32), _sds((seq, D_IN_PROJ), BF16)],
        operands=(dq, dk, dv, dpooled, dpooled, dzg, dr1, win),
        scratch=[pltpu.VMEM((TOK + POOL_HALO, D_POOL), F32)], tasks=tasks)


def _tn_matmul(name, a, b, out_shape, out_dtype, steps, a_spec, b_spec, o_spec, tasks=()):
    def body(a_ref, b_ref, o_ref):
        r = _dot_tn(a_ref[...].astype(BF16), b_ref[...].astype(BF16))
        o_ref[...] = r.reshape(o_ref.shape).astype(o_ref.dtype)

    (out,), results = _grid_call(body, name, steps, in_specs=[a_spec, b_spec], out_specs=[o_spec],
                                 out_shape=[_sds(out_shape, out_dtype)], operands=(a, b), tasks=tasks)
    return out, results


def _place():
    return lax.axis_index("x"), lax.axis_index("y"), lax.axis_index("c")


def _other_chips(x, y):
    return [(1 - x, y), (x, 1 - y), (1 - x, 1 - y)]


DMA_SEMS = pltpu.SemaphoreType.DMA


class _AllGather:
    def __init__(self, shards):
        self.operands = list(shards)
        self.n = len(shards)
        self.out_shape = [_sds((N_DEV, *s.shape), s.dtype) for s in shards]
        self.sems = [DMA_SEMS((7 * self.n,)), DMA_SEMS((7 * self.n,)), DMA_SEMS((self.n,))]

    def _copy(self, refs, a, k, block, to, from_input=False):
        ins, outs, (send_sems, recv_sems, _) = refs
        px, py, pc = block
        dst = outs[a].at[4 * px + 2 * py + pc]
        return pltpu.make_async_remote_copy(
            src_ref=ins[a] if from_input else dst, dst_ref=dst,
            send_sem=send_sems.at[7 * a + k], recv_sem=recv_sems.at[7 * a + k],
            device_id=to, device_id_type=MESH)

    def _local(self, refs, a):
        ins, outs, (_, _, local_sems) = refs
        x, y, c = _place()
        return pltpu.make_async_copy(ins[a], outs[a].at[4 * x + 2 * y + c], local_sems.at[a])

    def start(self, refs):
        x, y, c = _place()
        for a in range(self.n):
            self._local(refs, a).start()
        for a in range(self.n):
            self._copy(refs, a, 0, (x, y, c), (x, y, 1 - c), True).start()
            for j, chip in enumerate(_other_chips(x, y)):
                self._copy(refs, a, 1 + j, (x, y, c), (*chip, c), True).start()

    def middle(self, refs):
        x, y, c = _place()
        for j, chip in enumerate(_other_chips(x, y)):
            for a in range(self.n):
                self._copy(refs, a, 1 + j, (*chip, c), (x, y, c)).wait_recv()
                self._copy(refs, a, 4 + j, (*chip, c), (x, y, 1 - c)).start()

    def finish(self, refs):
        x, y, c = _place()
        me, sibling = (x, y, c), (x, y, 1 - c)
        chips = _other_chips(x, y)
        for a in range(self.n):
            self._copy(refs, a, 0, sibling, me).wait_recv()
            for j, chip in enumerate(chips):
                self._copy(refs, a, 4 + j, (*chip, 1 - c), me).wait_recv()
        for a in range(self.n):
            self._copy(refs, a, 0, me, sibling, True).wait_send()
            for j, chip in enumerate(chips):
                self._copy(refs, a, 1 + j, me, (*chip, c), True).wait_send()
                self._copy(refs, a, 4 + j, (*chip, c), sibling).wait_send()
            self._local(refs, a).wait()


class _SiblingSend:
    def __init__(self, partials):
        self.operands = list(partials)
        self.n = len(partials)
        self.out_shape = [_sds((4, *p.shape[1:]), p.dtype) for p in partials]
        self.sems = [DMA_SEMS((4 * self.n,)), DMA_SEMS((4 * self.n,))]

    def _copy(self, refs, a, q):
        ins, outs, (send_sems, recv_sems) = refs
        x, y, c = _place()
        return pltpu.make_async_remote_copy(
            src_ref=ins[a].at[2 * q + 1 - c], dst_ref=outs[a].at[q],
            send_sem=send_sems.at[4 * a + q], recv_sem=recv_sems.at[4 * a + q],
            device_id=(x, y, 1 - c), device_id_type=MESH)

    def start(self, refs):
        for a in range(self.n):
            for q in range(4):
                self._copy(refs, a, q).start()

    def middle(self, refs):
        pass

    def finish(self, refs):
        for a in range(self.n):
            for q in range(4):
                self._copy(refs, a, q).wait()


class _ChipScatter:
    def __init__(self, chip_partials):
        self.operands = list(chip_partials)
        self.n = len(chip_partials)
        self.out_shape = [_sds(p.shape, p.dtype) for p in chip_partials]
        self.sems = [DMA_SEMS((3 * self.n,)), DMA_SEMS((3 * self.n,)), DMA_SEMS((self.n,))]

    def _copy(self, refs, a, k, arrival=False):
        ins, outs, (send_sems, recv_sems, _) = refs
        x, y, c = _place()
        px, py = _other_chips(x, y)[k]
        mine, theirs = 2 * x + y, 2 * px + py
        return pltpu.make_async_remote_copy(
            src_ref=ins[a].at[mine if arrival else theirs], dst_ref=outs[a].at[theirs if arrival else mine],
            send_sem=send_sems.at[3 * a + k], recv_sem=recv_sems.at[3 * a + k],
            device_id=(px, py, c), device_id_type=MESH)

    def _local(self, refs, a):
        ins, outs, (_, _, local_sems) = refs
        x, y, _ = _place()
        return pltpu.make_async_copy(ins[a].at[2 * x + y], outs[a].at[2 * x + y], local_sems.at[a])

    def start(self, refs):
        for a in range(self.n):
            self._local(refs, a).start()
            for k in range(3):
                self._copy(refs, a, k).start()

    def middle(self, refs):
        pass

    def finish(self, refs):
        for a in range(self.n):
            for k in range(3):
                self._copy(refs, a, k, arrival=True).wait_recv()
        for a in range(self.n):
            for k in range(3):
                self._copy(refs, a, k).wait_send()
            self._local(refs, a).wait()


class _DirectScatter:
    def __init__(self, partials):
        self.operands = list(partials)
        self.n = len(partials)
        self.out_shape = [_sds(p.shape, p.dtype) for p in partials]
        self.sems = [DMA_SEMS((7 * self.n,)), DMA_SEMS((7 * self.n,)), DMA_SEMS((self.n,))]

    def _copy(self, refs, a, k, arrival=False):
        ins, outs, (send_sems, recv_sems, _) = refs
        x, y, c = _place()
        peer = [(x, y, 1 - c), (1 - x, y, c), (x, 1 - y, c), (1 - x, 1 - y, c),
                (1 - x, y, 1 - c), (x, 1 - y, 1 - c), (1 - x, 1 - y, 1 - c)][k]
        mine, theirs = 4 * x + 2 * y + c, 4 * peer[0] + 2 * peer[1] + peer[2]
        return pltpu.make_async_remote_copy(
            src_ref=ins[a].at[mine if arrival else theirs], dst_ref=outs[a].at[theirs if arrival else mine],
            send_sem=send_sems.at[7 * a + k], recv_sem=recv_sems.at[7 * a + k],
            device_id=peer, device_id_type=MESH)

    def _local(self, refs, a):
        ins, outs, (_, _, local_sems) = refs
        x, y, c = _place()
        return pltpu.make_async_copy(ins[a].at[4 * x + 2 * y + c], outs[a].at[4 * x + 2 * y + c], local_sems.at[a])

    def start(self, refs):
        for a in range(self.n):
            self._local(refs, a).start()
            for k in range(7):
                self._copy(refs, a, k).start()

    def middle(self, refs):
        pass

    def finish(self, refs):
        for a in range(self.n):
            for k in range(7):
                self._copy(refs, a, k, arrival=True).wait_recv()
        for a in range(self.n):
            for k in range(7):
                self._copy(refs, a, k).wait_send()
            self._local(refs, a).wait()


def _task_args(tasks):
    hbm = pl.BlockSpec(memory_space=pl.ANY)
    operands = [o for t in tasks for o in t.operands]
    out_shape = [s for t in tasks for s in t.out_shape]
    sems = [s for t in tasks for s in t.sems]
    return operands, [hbm] * len(operands), out_shape, [hbm] * len(out_shape), sems


def _task_refs(tasks, ins, outs, sems):
    per_task = []
    for t in tasks:
        ni, no, ns = len(t.operands), len(t.out_shape), len(t.sems)
        per_task.append((ins[:ni], outs[:no], sems[:ns]))
        ins, outs, sems = ins[ni:], outs[no:], sems[ns:]
    return per_task


def _task_results(tasks, outs):
    res = []
    for t in tasks:
        res.append(list(outs[:len(t.out_shape)]))
        outs = outs[len(t.out_shape):]
    return res


def _carry(body, tasks, n_in, n_out, n_scratch, steps):
    if not tasks:
        return body
    t_in = sum(len(t.operands) for t in tasks)
    t_out = sum(len(t.out_shape) for t in tasks)

    def wrapped(*refs):
        ins, refs = refs[:n_in], refs[n_in:]
        t_ins, refs = refs[:t_in], refs[t_in:]
        outs, refs = refs[:n_out], refs[n_out:]
        t_outs, refs = refs[:t_out], refs[t_out:]
        scratch, t_sems = refs[:n_scratch], refs[n_scratch:]
        per_task = _task_refs(tasks, t_ins, t_outs, t_sems)
        step = pl.program_id(0)

        @pl.when(step == 0)
        def _():
            for t, r in zip(tasks, per_task):
                t.start(r)

        @pl.when(step == steps - 1)
        def _():
            for t, r in zip(tasks, per_task):
                t.middle(r)

        body(*ins, *outs, *scratch)

        @pl.when(step == steps - 1)
        def _():
            for t, r in zip(tasks, per_task):
                t.finish(r)

    return wrapped


def _exchange(name, tasks):
    operands, in_specs, out_shape, out_specs, sems = _task_args(tasks)

    def body(*refs):
        ni, no = len(operands), len(out_shape)
        per_task = _task_refs(tasks, refs[:ni], refs[ni:ni + no], refs[ni + no:])
        for phase in ("start", "middle", "finish"):
            for t, r in zip(tasks, per_task):
                getattr(t, phase)(r)

    outs = pl.pallas_call(body, name=name, in_specs=in_specs, out_specs=out_specs, out_shape=out_shape,
                          scratch_shapes=sems)(*operands)
    return _task_results(tasks, outs)


def _row_tile(rows, cols, whole_up_to=256 * 1024):
    if rows * cols <= whole_up_to:
        return rows
    for t in (256, 128, 64, 32, 16, 8):
        if rows % t == 0:
            return t
    return rows


def _pair_sum(name, partials, from_sibling):
    _, rows, cols = partials.shape
    tile = _row_tile(rows, cols, 512 * 1024)

    def body(p_ref, s_ref, o_ref):
        mine = jnp.where(lax.axis_index("c") == 0, p_ref[0, 0].astype(F32), p_ref[0, 1].astype(F32))
        o_ref[0] = (mine + s_ref[0].astype(F32)).astype(o_ref.dtype)

    blk = pl.BlockSpec((1, tile, cols), lambda q, i: (q, i, 0))
    return pl.pallas_call(
        body, name=name, grid=(4, rows // tile),
        in_specs=[pl.BlockSpec((1, 2, tile, cols), lambda q, i: (q, 0, i, 0)), blk],
        out_specs=blk, out_shape=_sds(from_sibling.shape, from_sibling.dtype),
        compiler_params=_params("parallel", "parallel"),
    )(partials.reshape(4, 2, rows, cols), from_sibling)


def _sum_leading(name, stacked):
    parts, rows, cols = stacked.shape
    tile = _row_tile(rows, cols, (512 if parts <= 4 else 256) * 1024)

    def body(s_ref, o_ref):
        acc = s_ref[0].astype(F32)
        for d in range(1, parts):
            acc = acc + s_ref[d].astype(F32)
        o_ref[...] = acc

    return pl.pallas_call(
        body, name=name, grid=(rows // tile,),
        in_specs=[pl.BlockSpec((parts, tile, cols), lambda i: (0, i, 0))],
        out_specs=pl.BlockSpec((tile, cols), lambda i: (i, 0)),
        out_shape=_sds((rows, cols), F32),
        compiler_params=_params("parallel"),
    )(stacked)


def _adamw(name, w, g, m, v):
    rows, cols = w.shape
    tile = _row_tile(rows, cols)

    def body(w_ref, g_ref, m_ref, v_ref, d_ref, nm_ref, nv_ref):
        g = g_ref[...]
        nm = ADAM_B1 * m_ref[...] + (1.0 - ADAM_B1) * g
        nv = ADAM_B2 * v_ref[...] + (1.0 - ADAM_B2) * (g * g)
        m_hat = nm / (1.0 - ADAM_B1 ** ADAM_STEP)
        v_hat = nv / (1.0 - ADAM_B2 ** ADAM_STEP)
        d_ref[...] = -ADAM_LR * (m_hat / (jnp.sqrt(v_hat) + ADAM_EPS) + ADAM_WD * w_ref[...])
        nm_ref[...] = nm
        nv_ref[...] = nv

    blk = pl.BlockSpec((tile, cols), lambda i: (i, 0))
    return pl.pallas_call(
        body, name=name, grid=(rows // tile,),
        in_specs=[blk] * 4, out_specs=[blk] * 3,
        out_shape=[_sds((rows, cols), F32)] * 3,
        compiler_params=_params("parallel"),
    )(w, g, m, v)


SMALL = ("b_gate", "w_pool", "pool_scale", "ln1_g", "ln1_b", "conv_b", "ln2_g", "ln2_b")
TILE = 8 * LANE


def _pack(parts):
    tiles = []
    for p in parts:
        flat = p.reshape(-1)
        tiles.append(jnp.pad(flat, (0, -flat.size % TILE)).reshape(-1, LANE))
    return jnp.concatenate(tiles, axis=0)


def _unpack(packed, shapes):
    out, at = [], 0
    for shape in shapes:
        size = math.prod(shape)
        rows = -(-size // TILE) * 8
        out.append(packed[at:at + rows].reshape(-1)[:size].reshape(shape))
        at += rows
    return out


MIXER = ("w_branch_attn", "w_branch_pool", "w_out", "conv_w")
FFN = ("w_ffn_gate_t", "w_ffn_up_t", "w_ffn_down")


def _columns(t):
    return jnp.transpose(t, (1, 0, 2)).reshape(t.shape[1], N_DEV * t.shape[2])


def _row_blocks(t):
    return t.reshape(N_DEV * t.shape[1], t.shape[2])


def _by_owner(t):
    return t.reshape(N_DEV, t.shape[0] // N_DEV, t.shape[1])


def _reduce_halves(names, partials, from_sibling):
    return [_pair_sum("pair_sum_" + n, p, s) for n, p, s in zip(names, partials, from_sibling)]


def _local_step(x, target, shards, small):
    seq = x.shape[0]
    cos, sin = _rope_tables(seq)
    whole = lambda width: pl.BlockSpec((seq, width), lambda *_: (0, 0))
    ((w_in_all,),) = _exchange("gather_w_in", [_AllGather([shards["w_in"]])])
    (xb, q, k, v, u, g, kmean), (mixer,) = _proj_in(
        x, w_in_all, small["b_gate"], cos, sin, tasks=[_AllGather([shards[n] for n in MIXER])])
    wba, wbp, wout, conv_w = _columns(mixer[0]), _columns(mixer[1]), _row_blocks(mixer[2]), _columns(mixer[3])
    (o, lse, bias), (gate_up,) = _attn_fwd(
        q, k, v, kmean.reshape(seq // MOBA_BLOCK, D_ATTN),
        tasks=[_AllGather([shards["w_ffn_gate_t"], shards["w_ffn_up_t"]])])
    wgt, wut = _row_blocks(gate_up[0]), _row_blocks(gate_up[1])
    (ya, yp, pooled, mixed, ypre, merged, xhat1, rstd1, h1, h1b), ((wd,),) = _mix(
        o, u, g, x, wba, wbp, wout, small["w_pool"], small["pool_scale"], small["ln1_g"], small["ln1_b"],
        tasks=[_AllGather([shards["w_ffn_down"]])])
    wd = _row_blocks(wd)
    a, uf, act = _ffn_up(h1b, wgt, wut, conv_w, small["conv_b"])
    dr2, dr2b, loss, dg2, db2 = _ffn_down(act, wd, h1, target, small["ln2_g"], small["ln2_b"])

    da, du, dwd, dwg, dwu, dconv = _ffn_bwd(dr2b, h1b, a, uf, wd, conv_w, small["conv_b"])
    ffn_partials = [_by_owner(dwg), _by_owner(dwu), _by_owner(dwd)]
    (dr1, dr1b, dg1, db1), (ffn_sibling,) = _ln1_bwd(
        dr2, da, du, wgt, wut, xhat1, rstd1, small["ln1_g"], tasks=[_SiblingSend(ffn_partials)])
    ffn_chip = _reduce_halves(FFN, ffn_partials, ffn_sibling)
    (dzg, dya, dyp, do, dmixed, dpooled, dbg, dps), (gate_landed,) = _mix_bwd(
        dr1b, ya, yp, g, mixed, wout, wba, wbp, small["w_pool"], small["pool_scale"],
        tasks=[_ChipScatter(ffn_chip[0:1])])
    dw_out, _ = _tn_matmul(
        "dw_out", merged, dr1b, (D_MODEL, D_MODEL), BF16, 4,
        pl.BlockSpec((seq, 256), lambda m: (0, m)), whole(D_MODEL), pl.BlockSpec((256, D_MODEL), lambda m: (m, 0)))
    dw_ba, _ = _tn_matmul(
        "dw_branch_attn", o, dya, (N_DEV, D_ATTN, LANE), BF16, N_DEV,
        whole(D_ATTN), pl.BlockSpec((seq, LANE), lambda n: (0, n)), pl.BlockSpec((1, D_ATTN, LANE), lambda n: (n, 0, 0)))
    dw_bp, _ = _tn_matmul(
        "dw_branch_pool", ypre, dyp, (N_DEV, D_POOL, LANE), BF16, N_DEV,
        whole(D_POOL), pl.BlockSpec((seq, LANE), lambda n: (0, n)), pl.BlockSpec((1, D_POOL, LANE), lambda n: (n, 0, 0)))
    dw_pool, _ = _tn_matmul(
        "dw_pool", pooled, dmixed, (len(POOL_WINDOWS), POOL_GROUP, POOL_GROUP), F32, len(POOL_WINDOWS),
        pl.BlockSpec((seq, POOL_GROUP), lambda n: (0, n)), pl.BlockSpec((seq, POOL_GROUP), lambda n: (0, n)),
        pl.BlockSpec((1, POOL_GROUP, POOL_GROUP), lambda n: (n, 0, 0)))
    (dq, dk, dv), (up_down_landed,) = _attn_bwd(
        q, k, v, bias, o, lse, do, cos, sin, tasks=[_ChipScatter(ffn_chip[1:3])])
    (grad_x, dz), (out_landed,) = _in_bwd(
        dq, dk, dv, dpooled, dzg, dr1, w_in_all, tasks=[_DirectScatter([_by_owner(dw_out)])])
    dw_in, (branch_landed,) = _tn_matmul(
        "dw_in", xb, dz, (N_DEV, D_MODEL, D_ATTN), BF16, 2 * N_DEV,
        pl.BlockSpec((seq, 512), lambda s: (0, s % 2)), pl.BlockSpec((seq, D_ATTN), lambda s: (0, s // 2)),
        pl.BlockSpec((1, 512, D_ATTN), lambda s: (s // 2, s % 2, 0)), tasks=[_DirectScatter([dw_ba, dw_bp])])

    landed = dict(zip(FFN + MIXER[:3], gate_landed + up_down_landed + branch_landed + out_landed))
    little = {"b_gate": dbg, "w_pool": dw_pool, "pool_scale": dps, "ln1_g": dg1, "ln1_b": db1, "conv_b": dconv[3:4],
              "ln2_g": dg2, "ln2_b": db2, "conv_w": dconv[0:3]}
    return loss[0, 0], grad_x, landed, dw_in, little


def kernel(x, w_in, b_gate, w_branch_attn, w_pool, pool_scale, w_branch_pool, w_out, ln1_g, ln1_b, w_ffn_gate, w_ffn_up, conv_w, conv_b, w_ffn_down, ln2_g, ln2_b, loss_target, m_w_in, m_b_gate, m_w_branch_attn, m_w_pool, m_pool_scale, m_w_branch_pool, m_w_out, m_ln1_g, m_ln1_b, m_w_ffn_gate, m_w_ffn_up, m_conv_w, m_conv_b, m_w_ffn_down, m_ln2_g, m_ln2_b, v_w_in, v_b_gate, v_w_branch_attn, v_w_pool, v_pool_scale, v_w_branch_pool, v_w_out, v_ln1_g, v_ln1_b, v_w_ffn_gate, v_w_ffn_up, v_conv_w, v_conv_b, v_w_ffn_down, v_ln2_g, v_ln2_b):
    me = 4 * lax.axis_index("x") + 2 * lax.axis_index("y") + lax.axis_index("c")
    weights = dict(w_in=w_in, b_gate=b_gate, w_branch_attn=w_branch_attn, w_pool=w_pool, pool_scale=pool_scale,
                   w_branch_pool=w_branch_pool, w_out=w_out, ln1_g=ln1_g, ln1_b=ln1_b, w_ffn_gate=w_ffn_gate,
                   w_ffn_up=w_ffn_up, conv_w=conv_w, conv_b=conv_b, w_ffn_down=w_ffn_down, ln2_g=ln2_g, ln2_b=ln2_b)
    m_in = dict(w_in=m_w_in, b_gate=m_b_gate, w_branch_attn=m_w_branch_attn, w_pool=m_w_pool,
                pool_scale=m_pool_scale, w_branch_pool=m_w_branch_pool, w_out=m_w_out, ln1_g=m_ln1_g, ln1_b=m_ln1_b,
                w_ffn_gate=m_w_ffn_gate, w_ffn_up=m_w_ffn_up, conv_w=m_conv_w, conv_b=m_conv_b,
                w_ffn_down=m_w_ffn_down, ln2_g=m_ln2_g, ln2_b=m_ln2_b)
    v_in = dict(w_in=v_w_in, b_gate=v_b_gate, w_branch_attn=v_w_branch_attn, w_pool=v_w_pool,
                pool_scale=v_pool_scale, w_branch_pool=v_w_branch_pool, w_out=v_w_out, ln1_g=v_ln1_g, ln1_b=v_ln1_b,
                w_ffn_gate=v_w_ffn_gate, w_ffn_up=v_w_ffn_up, conv_w=v_conv_w, conv_b=v_conv_b,
                w_ffn_down=v_w_ffn_down, ln2_g=v_ln2_g, ln2_b=v_ln2_b)
    weights = {n: a[0] for n, a in weights.items()}
    m_in = {n: a[0] for n, a in m_in.items()}
    v_in = {n: a[0] for n, a in v_in.items()}

    shards = {"w_in": weights["w_in"].astype(BF16), "w_branch_attn": weights["w_branch_attn"].astype(BF16),
              "w_branch_pool": weights["w_branch_pool"].astype(BF16), "w_out": weights["w_out"].astype(BF16),
              "w_ffn_gate_t": weights["w_ffn_gate"].T.astype(BF16), "w_ffn_up_t": weights["w_ffn_up"].T.astype(BF16),
              "w_ffn_down": weights["w_ffn_down"].astype(BF16), "conv_w": weights["conv_w"]}
    small = {"b_gate": weights["b_gate"][None], "w_pool": weights["w_pool"], "pool_scale": weights["pool_scale"][None],
             "ln1_g": weights["ln1_g"][None], "ln1_b": weights["ln1_b"][None], "conv_b": weights["conv_b"][None],
             "ln2_g": weights["ln2_g"][None], "ln2_b": weights["ln2_b"][None]}

    loss_part, grad_x, landed, dw_in, little = _local_step(x[0], loss_target[0], shards, small)
    loss = lax.psum(loss_part, ("x", "y", "c"))

    ((w_in_sibling,),) = _exchange("sibling_grads", [_SiblingSend([dw_in])])
    w_in_chip = _reduce_halves(["w_in"], [dw_in], [w_in_sibling])
    names = SMALL + ("conv_w",)
    (landed["w_in"],), (all_small,) = _exchange(
        "scatter_grads", [_ChipScatter(w_in_chip), _AllGather([_pack([little[n] for n in names])])])

    grads = {n: _sum_leading("sum_" + n, t) for n, t in landed.items()}
    grads["w_ffn_gate"] = grads.pop("w_ffn_gate_t").T
    grads["w_ffn_up"] = grads.pop("w_ffn_up_t").T
    small_sum = _sum_leading("sum_small", all_small)
    *small_grads, conv_w_grad = _unpack(small_sum, [weights[n].shape for n in SMALL] + [(3, D_FF)])
    grads.update(zip(SMALL, small_grads))
    grads["conv_w"] = lax.dynamic_slice(conv_w_grad, (0, me * FF_SHARD), (3, FF_SHARD))

    delta, new_m, new_v = {}, {}, {}
    for n in ("w_in", "w_branch_attn", "w_branch_pool", "w_out", "w_ffn_gate", "w_ffn_up", "w_ffn_down"):
        delta[n], new_m[n], new_v[n] = _adamw("adamw_" + n, weights[n], grads[n], m_in[n], v_in[n])
    flat = lambda d: _pack([d[n] for n in names])
    shapes = [weights[n].shape for n in names]
    for out, packed in zip((delta, new_m, new_v),
                           _adamw("adamw_small", flat(weights), flat(grads), flat(m_in), flat(v_in))):
        out.update(zip(names, _unpack(packed, shapes)))

    order = ("w_in", "b_gate", "w_branch_attn", "w_pool", "pool_scale", "w_branch_pool", "w_out", "ln1_g", "ln1_b",
             "w_ffn_gate", "w_ffn_up", "conv_w", "conv_b", "w_ffn_down", "ln2_g", "ln2_b")
    lead = lambda t: t[None]
    return (loss, lead(grad_x), *[lead(grads[n]) for n in order], *[lead(delta[n]) for n in order],
            *[lead(new_m[n]) for n in order], *[lead(new_v[n]) for n in order])
```

```python
import functools
import math

import jax
import jax.numpy as jnp
from jax import lax
from jax.experimental import pallas as pl
from jax.experimental.pallas import tpu as pltpu

F32 = jnp.float32
BF16 = jnp.bfloat16

D_MODEL = 1024
N_HEADS = 8
HEAD_DIM = 64
D_ATTN = N_HEADS * HEAD_DIM
MOBA_BLOCK = 256
MOBA_TOPK = 3
ROPE_THETA = 10000.0
POOL_WINDOWS = (2, 4, 8, 16)
POOL_GROUP = 128
D_POOL = len(POOL_WINDOWS) * POOL_GROUP
POOL_HALO = 16
D_FF = 2816
D_IN_PROJ = 3 * D_ATTN + D_POOL + 2 * D_MODEL
LN_EPS = 1e-5
ALPHA = 2.0 ** 0.25
NEG = -1e30
N_DEV = 8
FF_SHARD = D_FF // N_DEV

ADAM_LR = 0.001
ADAM_B1 = 0.9
ADAM_B2 = 0.999
ADAM_EPS = 1e-08
ADAM_WD = 0.01
ADAM_STEP = 10

TOK = 256
FF_CHUNK = 256
LANE = 128
VMEM_LIMIT = 56 * 1024 * 1024

MESH = pl.DeviceIdType.MESH
NT_DIMS = (((1,), (1,)), ((), ()))
TN_DIMS = (((0,), (0,)), ((), ()))


def _params(*sem):
    return pltpu.CompilerParams(dimension_semantics=sem or None, vmem_limit_bytes=VMEM_LIMIT)


def _full(shape):
    zeros = (0,) * len(shape)
    return pl.BlockSpec(shape, lambda *_: zeros, pipeline_mode=pl.Buffered(1))


def _rows(width, tile=TOK):
    return pl.BlockSpec((tile, width), lambda i: (i, 0))


def _sds(shape, dtype):
    return jax.ShapeDtypeStruct(shape, dtype)


def _dot(a, b):
    return jnp.dot(a, b, preferred_element_type=F32)


def _dot_nt(a, b):
    return lax.dot_general(a, b, NT_DIMS, preferred_element_type=F32)


def _dot_tn(a, b):
    return lax.dot_general(a, b, TN_DIMS, preferred_element_type=F32)


def _rope_tables(seq):
    half = HEAD_DIM // 2
    inv_freq = 1.0 / (ROPE_THETA ** (jnp.arange(half, dtype=F32) / half))
    ang = jnp.arange(seq, dtype=F32)[:, None] * inv_freq[None, :]
    cos, sin = jnp.cos(ang), jnp.sin(ang)
    return jnp.tile(cos, (1, 4)), jnp.tile(jnp.concatenate([-sin, sin], axis=1), (1, 2))


def _swap_halves(t):
    lane = lax.broadcasted_iota(jnp.int32, t.shape, 1)
    return jnp.where((lane % HEAD_DIM) < HEAD_DIM // 2, pltpu.roll(t, LANE - 32, 1), pltpu.roll(t, 32, 1))


def _rope(t, cos, sin):
    return t * cos + _swap_halves(t) * sin


def _rope_transposed(g, cos, sin):
    return g * cos + _swap_halves(g * sin)


def _ln_fwd(r, g, b):
    mu = jnp.mean(r, axis=-1, keepdims=True)
    xc = r - mu
    var = jnp.mean(xc * xc, axis=-1, keepdims=True)
    rstd = lax.rsqrt(var + LN_EPS)
    xhat = xc * rstd
    return xhat * g + b, xhat, rstd


def _ln_bwd(dy, xhat, rstd, g):
    dxh = dy * g
    m1 = jnp.mean(dxh, axis=-1, keepdims=True)
    m2 = jnp.mean(dxh * xhat, axis=-1, keepdims=True)
    return rstd * (dxh - m1 - xhat * m2)


def _gelu_parts(a):
    cdf = 0.5 * (1.0 + lax.erf(a * (1.0 / math.sqrt(2.0))))
    pdf = jnp.exp(-0.5 * a * a) * (1.0 / math.sqrt(2.0 * math.pi))
    return a * cdf, cdf + a * pdf


def _shift_down(a, k):
    row = lax.broadcasted_iota(jnp.int32, a.shape, 0)
    return jnp.where(row >= k, pltpu.roll(a, k, 0), 0.0)


def _shift_up(a, k):
    n = a.shape[0]
    row = lax.broadcasted_iota(jnp.int32, a.shape, 0)
    return jnp.where(row < n - k, pltpu.roll(a, n - k, 0), 0.0)


def _conv(a, cw, cb):
    return cw[2:3, :] * a + cw[1:2, :] * _shift_down(a, 1) + cw[0:1, :] * _shift_down(a, 2) + cb


def _pool_count(first_row, rows, window):
    t = first_row + lax.broadcasted_iota(jnp.int32, (rows, 1), 0)
    return jnp.minimum(t + 1, window).astype(F32)


def _grid_call(body, name, steps, in_specs, out_specs, out_shape, operands, scratch=(), tasks=()):
    t_operands, t_in_specs, t_out_shape, t_out_specs, t_sems = _task_args(tasks)
    outs = pl.pallas_call(
        _carry(body, tasks, len(in_specs), len(out_specs), len(scratch), steps), name=name, grid=(steps,),
        in_specs=list(in_specs) + t_in_specs, out_specs=list(out_specs) + t_out_specs,
        out_shape=list(out_shape) + t_out_shape, scratch_shapes=list(scratch) + t_sems,
        compiler_params=_params("arbitrary"),
    )(*operands, *t_operands)
    return outs[:len(out_specs)], _task_results(tasks, outs[len(out_specs):])


def _proj_in(x, win, b_gate, cos, sin, tasks=()):
    seq = x.shape[0]
    nt = seq // TOK

    def body(x_ref, win_ref, bg_ref, cos_ref, sin_ref, xb_ref, q_ref, k_ref, v_ref, u_ref, g_ref, km_ref):
        xb = x_ref[...].astype(BF16)
        xb_ref[...] = xb
        cos_t, sin_t = cos_ref[...], sin_ref[...]
        for sec, out_ref in ((0, q_ref), (1, k_ref)):
            z = _dot(xb, win_ref[sec])
            for c in range(D_ATTN // LANE):
                cols = slice(LANE * c, LANE * (c + 1))
                out_ref[:, cols] = _rope(z[:, cols], cos_t, sin_t)
        km_ref[0] = jnp.mean(k_ref[...], axis=0, keepdims=True)
        v_ref[...] = _dot(xb, win_ref[2]).astype(BF16)
        u_ref[...] = _dot(xb, win_ref[3])
        for n in range(4):
            cols = slice(D_ATTN * n, D_ATTN * (n + 1))
            g_ref[:, cols] = jax.nn.sigmoid(_dot(xb, win_ref[4 + n]) + bg_ref[:, cols])

    return _grid_call(
        body, "proj_in", nt,
        in_specs=[_rows(D_MODEL), _full(win.shape), _full((1, 2 * D_MODEL)), _rows(LANE), _rows(LANE)],
        out_specs=[_rows(D_MODEL), _rows(D_ATTN), _rows(D_ATTN), _rows(D_ATTN), _rows(D_POOL), _rows(2 * D_MODEL),
                   pl.BlockSpec((1, 1, D_ATTN), lambda i: (i, 0, 0))],
        out_shape=[_sds((seq, D_MODEL), BF16), _sds((seq, D_ATTN), F32), _sds((seq, D_ATTN), F32),
                   _sds((seq, D_ATTN), BF16), _sds((seq, D_POOL), F32), _sds((seq, 2 * D_MODEL), F32),
                   _sds((nt, 1, D_ATTN), F32)],
        operands=(x, win, b_gate, cos, sin), tasks=tasks)


SCORE_CHUNK = 128


def _store_keys(ka_sc, k_ref, ls):
    seq = ka_sc.shape[0]
    ka_sc[:, 0:HEAD_DIM] = k_ref[:, ls].astype(BF16)
    row = lax.broadcasted_iota(jnp.int32, (seq, HEAD_DIM), 0)
    lane = lax.broadcasted_iota(jnp.int32, (seq, HEAD_DIM), 1)
    in_block = (lane * MOBA_BLOCK <= row) & (row < (lane + 1) * MOBA_BLOCK)
    ka_sc[:, HEAD_DIM:] = jnp.where(in_block, 1.0, 0.0).astype(BF16)


def _block_bias(qf, km, i):
    if i <= MOBA_TOPK:
        return jnp.zeros((MOBA_BLOCK, HEAD_DIM), BF16)
    nb = km.shape[0]
    gate = lax.dot_general(km, qf, NT_DIMS, precision=lax.Precision.HIGHEST, preferred_element_type=F32)
    blk = lax.broadcasted_iota(jnp.int32, gate.shape, 0)
    rank = jnp.zeros(gate.shape, F32)
    for r in range(1, i):
        lower = pltpu.roll(gate, r, 0)
        rank = rank + jnp.where((blk >= r) & (lower >= gate), 1.0, 0.0)
        higher = pltpu.roll(gate, nb - r, 0)
        rank = rank + jnp.where((blk + r < i) & (higher > gate), 1.0, 0.0)
    bias = jnp.where((blk < i) & (rank >= MOBA_TOPK), NEG, 0.0)
    padded = jnp.concatenate([bias, jnp.zeros((LANE - nb, MOBA_BLOCK), F32)], axis=0)
    return jnp.transpose(padded)[:, 0:HEAD_DIM].astype(BF16)


def _causal(shape, transposed=False):
    row = lax.broadcasted_iota(jnp.int32, shape, 0)
    col = lax.broadcasted_iota(jnp.int32, shape, 1)
    return (row <= col) if transposed else (col <= row)


def _row_vector(col):
    return jnp.transpose(jnp.broadcast_to(col, (MOBA_BLOCK, LANE)))[0:1, :]


def _attn_fwd(q, k, v, kmean, tasks=()):
    seq = q.shape[0]
    nb = seq // MOBA_BLOCK
    assert nb == 8, "the block ranking keeps one sublane per key block"
    pair = pl.BlockSpec((seq, LANE), lambda p: (0, p))
    heads = LANE // HEAD_DIM

    def body(q_ref, k_ref, v_ref, km_ref, o_ref, lse_ref, bias_ref, ka_sc, qa_sc, s_sc, p_sc):
        lse_ref[0, heads:, :] = jnp.zeros((8 - heads, seq), F32)
        for hh in range(heads):
            ls = slice(HEAD_DIM * hh, HEAD_DIM * (hh + 1))
            _store_keys(ka_sc, k_ref, ls)
            vb = v_ref[:, ls]
            km = km_ref[:, ls]
            for i in range(nb):
                rs = slice(MOBA_BLOCK * i, MOBA_BLOCK * (i + 1))
                width = MOBA_BLOCK * (i + 1)
                qf = q_ref[rs, ls]
                bias = _block_bias(qf, km, i)
                bias_ref[rs, ls] = bias
                qa_sc[:, 0:HEAD_DIM] = (qf * HEAD_DIM ** -0.5).astype(BF16)
                qa_sc[:, HEAD_DIM:] = bias
                s_sc[:, 0:width] = _dot_nt(qa_sc[...], ka_sc[0:width, :])
                s_sc[:, rs] = jnp.where(_causal((MOBA_BLOCK, MOBA_BLOCK)), s_sc[:, rs], NEG)
                chunks = [slice(SCORE_CHUNK * c, SCORE_CHUNK * (c + 1)) for c in range(width // SCORE_CHUNK)]
                top = s_sc[:, chunks[0]]
                for c in chunks[1:]:
                    top = jnp.maximum(top, s_sc[:, c])
                m = jnp.max(top, axis=1, keepdims=True)
                total = jnp.zeros((MOBA_BLOCK, SCORE_CHUNK), F32)
                for c in chunks:
                    p = jnp.exp(s_sc[:, c] - m)
                    total = total + p
                    p_sc[:, c] = p.astype(BF16)
                l = jnp.sum(total, axis=1, keepdims=True)
                o_ref[rs, ls] = _dot(p_sc[:, 0:width], vb[0:width]) / l
                lse_ref[0, hh:hh + 1, rs] = _row_vector(m + jnp.log(l))

    return _grid_call(
        body, "attn_fwd", D_ATTN // LANE,
        in_specs=[pair, pair, pair, pl.BlockSpec((nb, LANE), lambda p: (0, p))],
        out_specs=[pair, pl.BlockSpec((1, 8, seq), lambda p: (p, 0, 0)), pair],
        out_shape=[_sds((seq, D_ATTN), F32), _sds((D_ATTN // LANE, 8, seq), F32), _sds((seq, D_ATTN), BF16)],
        operands=(q, k, v, kmean),
        scratch=[pltpu.VMEM((seq, LANE), BF16), pltpu.VMEM((MOBA_BLOCK, LANE), BF16),
                 pltpu.VMEM((MOBA_BLOCK, seq), F32), pltpu.VMEM((MOBA_BLOCK, seq), BF16)],
        tasks=tasks)


def _mix(o, u, g, x, wba, wbp, wout, w_pool, pool_scale, ln_g, ln_b, tasks=()):
    seq = x.shape[0]

    def body(o_ref, u_ref, uprev_ref, g_ref, x_ref, wba_ref, wbp_ref, wout_ref, wp_ref, ps_ref, lg_ref, lb_ref,
             ya_ref, yp_ref, pooled_ref, mixed_ref, ypre_ref, merged_ref, xhat_ref, rstd_ref, h_ref, hb_ref, ext):
        i = pl.program_id(0)
        ya = _dot(o_ref[...].astype(BF16), wba_ref[...])
        ucur = u_ref[...]
        ext[0:POOL_HALO, :] = jnp.where(i > 0, uprev_ref[...], 0.0)
        ext[POOL_HALO:, :] = ucur
        for grp, window in enumerate(POOL_WINDOWS):
            cols = slice(POOL_GROUP * grp, POOL_GROUP * (grp + 1))
            acc = ucur[:, cols]
            for kk in range(1, window):
                acc = acc + ext[pl.ds(POOL_HALO - kk, TOK), cols]
            pooled = acc / _pool_count(i * TOK, TOK, window) - ucur[:, cols]
            pooled_ref[:, cols] = pooled.astype(BF16)
            mixed_ref[:, cols] = _dot(pooled.astype(BF16), wp_ref[grp].astype(BF16))
        mixed = mixed_ref[...]
        ypre = (mixed * ps_ref[...]).astype(BF16)
        ypre_ref[...] = ypre
        yp = _dot(ypre, wbp_ref[...])
        ya_ref[...] = ya
        yp_ref[...] = yp
        merged = (g_ref[:, :D_MODEL] * ya + g_ref[:, D_MODEL:] * yp).astype(BF16)
        merged_ref[...] = merged
        r1 = ALPHA * x_ref[...] + _dot(merged, wout_ref[...])
        h, xhat, rstd = _ln_fwd(r1, lg_ref[...], lb_ref[...])
        xhat_ref[...] = xhat
        rstd_ref[...] = jnp.broadcast_to(rstd, (TOK, LANE))
        h_ref[...] = h
        hb_ref[...] = h.astype(BF16)

    halo = pl.BlockSpec((POOL_HALO, D_POOL), lambda i: (jnp.maximum(i * (TOK // POOL_HALO) - 1, 0), 0))
    return _grid_call(
        body, "mix", seq // TOK,
        in_specs=[_rows(D_ATTN), _rows(D_POOL), halo, _rows(2 * D_MODEL), _rows(D_MODEL),
                  _full(wba.shape), _full(wbp.shape), _full(wout.shape), _full(w_pool.shape),
                  _full((1, D_POOL)), _full((1, D_MODEL)), _full((1, D_MODEL))],
        out_specs=[_rows(D_MODEL), _rows(D_MODEL), _rows(D_POOL), _rows(D_POOL), _rows(D_POOL), _rows(D_MODEL),
                   _rows(D_MODEL), _rows(LANE), _rows(D_MODEL), _rows(D_MODEL)],
        out_shape=[_sds((seq, D_MODEL), F32), _sds((seq, D_MODEL), F32), _sds((seq, D_POOL), BF16),
                   _sds((seq, D_POOL), F32), _sds((seq, D_POOL), BF16), _sds((seq, D_MODEL), BF16),
                   _sds((seq, D_MODEL), F32), _sds((seq, LANE), F32), _sds((seq, D_MODEL), F32),
                   _sds((seq, D_MODEL), BF16)],
        operands=(o, u, u, g, x, wba, wbp, wout, w_pool, pool_scale, ln_g, ln_b),
        scratch=[pltpu.VMEM((TOK + POOL_HALO, D_POOL), F32)], tasks=tasks)


def _ffn_up(hb, wgt, wut, conv_w, conv_b, tasks=()):
    seq = hb.shape[0]
    wblk = pl.BlockSpec((FF_CHUNK, D_MODEL), lambda c: (c, 0))
    cblk = lambda rows: pl.BlockSpec((rows, FF_CHUNK), lambda c: (0, c))
    oblk = pl.BlockSpec((seq, FF_CHUNK), lambda c: (0, c))

    def body(h_ref, wg_ref, wu_ref, cw_ref, cb_ref, a_ref, u_ref, act_ref):
        h = h_ref[...]
        a = _dot_nt(h, wg_ref[...])
        u = _dot_nt(h, wu_ref[...])
        a_ref[...] = a
        u_ref[...] = u
        gelu, _ = _gelu_parts(_conv(a, cw_ref[...], cb_ref[...]))
        act_ref[...] = (gelu * u).astype(BF16)

    return _grid_call(
        body, "ffn_up", D_FF // FF_CHUNK,
        in_specs=[_full(hb.shape), wblk, wblk, cblk(3), cblk(1)],
        out_specs=[oblk, oblk, oblk],
        out_shape=[_sds((seq, D_FF), F32), _sds((seq, D_FF), F32), _sds((seq, D_FF), BF16)],
        operands=(hb, wgt, wut, conv_w, conv_b), tasks=tasks)


def _ffn_down(act, wd, h, target, ln_g, ln_b):
    seq = h.shape[0]

    def body(act_ref, wd_ref, h_ref, t_ref, lg_ref, lb_ref, dr_ref, drb_ref, loss_ref, dg_ref, db_ref):
        i = pl.program_id(0)

        @pl.when(i == 0)
        def _():
            loss_ref[...] = jnp.zeros_like(loss_ref)
            dg_ref[...] = jnp.zeros_like(dg_ref)
            db_ref[...] = jnp.zeros_like(db_ref)

        r2 = ALPHA * h_ref[...] + _dot(act_ref[...], wd_ref[...])
        y, xhat, rstd = _ln_fwd(r2, lg_ref[...], lb_ref[...])
        diff = y - t_ref[...]
        loss_ref[...] += jnp.sum(diff * diff) * (0.5 / D_MODEL)
        dy = diff * (1.0 / D_MODEL)
        dg_ref[...] += jnp.sum(dy * xhat, axis=0, keepdims=True)
        db_ref[...] += jnp.sum(dy, axis=0, keepdims=True)
        dr = _ln_bwd(dy, xhat, rstd, lg_ref[...])
        dr_ref[...] = dr
        drb_ref[...] = dr.astype(BF16)

    vec = pl.BlockSpec((1, D_MODEL), lambda i: (0, 0))
    return pl.pallas_call(
        body, name="ffn_down", grid=(seq // TOK,),
        in_specs=[_rows(D_FF), _full(wd.shape), _rows(D_MODEL), _rows(D_MODEL), _full((1, D_MODEL)), _full((1, D_MODEL))],
        out_specs=[_rows(D_MODEL), _rows(D_MODEL), pl.BlockSpec((8, LANE), lambda i: (0, 0)), vec, vec],
        out_shape=[_sds((seq, D_MODEL), F32), _sds((seq, D_MODEL), BF16), _sds((8, LANE), F32),
                   _sds((1, D_MODEL), F32), _sds((1, D_MODEL), F32)],
        compiler_params=_params("arbitrary"),
    )(act, wd, h, target, ln_g, ln_b)


def _ffn_bwd(drb, hb, a, u, wd, conv_w, conv_b):
    seq = hb.shape[0]
    wblk = pl.BlockSpec((FF_CHUNK, D_MODEL), lambda c: (c, 0))
    cblk = lambda rows: pl.BlockSpec((rows, FF_CHUNK), lambda c: (0, c))
    sblk = pl.BlockSpec((seq, FF_CHUNK), lambda c: (0, c))

    def body(dr_ref, h_ref, a_ref, u_ref, wd_ref, cw_ref, cb_ref, da_ref, du_ref, dwd_ref, dwg_ref, dwu_ref, dc_ref):
        dr = dr_ref[...]
        h = h_ref[...]
        a = a_ref[...]
        u = u_ref[...]
        cw = cw_ref[...]
        dact = _dot_nt(dr, wd_ref[...])
        gelu, dgelu = _gelu_parts(_conv(a, cw, cb_ref[...]))
        dwd_ref[...] = _dot_tn((gelu * u).astype(BF16), dr).astype(BF16)
        du = (dact * gelu).astype(BF16)
        dac = dact * u * dgelu
        da = (cw[2:3, :] * dac + cw[1:2, :] * _shift_up(dac, 1) + cw[0:1, :] * _shift_up(dac, 2)).astype(BF16)
        da_ref[...] = da
        du_ref[...] = du
        dwg_ref[...] = _dot_tn(da, h).astype(BF16)
        dwu_ref[...] = _dot_tn(du, h).astype(BF16)
        dc_ref[0:1, :] = jnp.sum(dac * _shift_down(a, 2), axis=0, keepdims=True)
        dc_ref[1:2, :] = jnp.sum(dac * _shift_down(a, 1), axis=0, keepdims=True)
        dc_ref[2:3, :] = jnp.sum(dac * a, axis=0, keepdims=True)
        dc_ref[3:4, :] = jnp.sum(dac, axis=0, keepdims=True)
        dc_ref[4:8, :] = jnp.zeros((4, FF_CHUNK), F32)

    return pl.pallas_call(
        body, name="ffn_bwd", grid=(D_FF // FF_CHUNK,),
        in_specs=[_full(drb.shape), _full(hb.shape), sblk, sblk, wblk, cblk(3), cblk(1)],
        out_specs=[sblk, sblk, wblk, wblk, wblk, cblk(8)],
        out_shape=[_sds((seq, D_FF), BF16), _sds((seq, D_FF), BF16), _sds((D_FF, D_MODEL), BF16),
                   _sds((D_FF, D_MODEL), BF16), _sds((D_FF, D_MODEL), BF16), _sds((8, D_FF), F32)],
        compiler_params=_params("parallel"),
    )(drb, hb, a, u, wd, conv_w, conv_b)


def _ln1_bwd(dr2, da, du, wgt, wut, xhat, rstd, ln_g, tasks=()):
    seq = dr2.shape[0]

    def body(dr2_ref, da_ref, du_ref, wg_ref, wu_ref, xhat_ref, rstd_ref, lg_ref, dr_ref, drb_ref, dg_ref, db_ref):
        @pl.when(pl.program_id(0) == 0)
        def _():
            dg_ref[...] = jnp.zeros_like(dg_ref)
            db_ref[...] = jnp.zeros_like(db_ref)

        dh = ALPHA * dr2_ref[...] + _dot(da_ref[...], wg_ref[...]) + _dot(du_ref[...], wu_ref[...])
        xhat = xhat_ref[...]
        dg_ref[...] += jnp.sum(dh * xhat, axis=0, keepdims=True)
        db_ref[...] += jnp.sum(dh, axis=0, keepdims=True)
        dr = _ln_bwd(dh, xhat, rstd_ref[:, 0:1], lg_ref[...])
        dr_ref[...] = dr
        drb_ref[...] = dr.astype(BF16)

    vec = pl.BlockSpec((1, D_MODEL), lambda i: (0, 0))
    return _grid_call(
        body, "ln1_bwd", seq // TOK,
        in_specs=[_rows(D_MODEL), _rows(D_FF), _rows(D_FF), _full(wgt.shape), _full(wut.shape), _rows(D_MODEL),
                  _rows(LANE), _full((1, D_MODEL))],
        out_specs=[_rows(D_MODEL), _rows(D_MODEL), vec, vec],
        out_shape=[_sds((seq, D_MODEL), F32), _sds((seq, D_MODEL), BF16), _sds((1, D_MODEL), F32),
                   _sds((1, D_MODEL), F32)],
        operands=(dr2, da, du, wgt, wut, xhat, rstd, ln_g), tasks=tasks)


def _mix_bwd(drb, ya, yp, g, mixed, wout, wba, wbp, w_pool, pool_scale, tasks=()):
    seq = drb.shape[0]

    def body(dr_ref, ya_ref, yp_ref, g_ref, mixed_ref, wout_ref, wba_ref, wbp_ref, wp_ref, ps_ref,
             dzg_ref, dya_ref, dyp_ref, do_ref, dmixed_ref, dpooled_ref, dbg_ref, dps_ref):
        @pl.when(pl.program_id(0) == 0)
        def _():
            dbg_ref[...] = jnp.zeros_like(dbg_ref)
            dps_ref[...] = jnp.zeros_like(dps_ref)

        dmerged = _dot_nt(dr_ref[...], wout_ref[...])
        ga, gp = g_ref[:, :D_MODEL], g_ref[:, D_MODEL:]
        dzga = dmerged * ya_ref[...] * ga * (1.0 - ga)
        dzgp = dmerged * yp_ref[...] * gp * (1.0 - gp)
        dzg_ref[:, :D_MODEL] = dzga.astype(BF16)
        dzg_ref[:, D_MODEL:] = dzgp.astype(BF16)
        dbg_ref[:, :D_MODEL] += jnp.sum(dzga, axis=0, keepdims=True)
        dbg_ref[:, D_MODEL:] += jnp.sum(dzgp, axis=0, keepdims=True)
        dya = (dmerged * ga).astype(BF16)
        dyp = (dmerged * gp).astype(BF16)
        dya_ref[...] = dya
        dyp_ref[...] = dyp
        do_ref[...] = _dot_nt(dya, wba_ref[...])
        dypre = _dot_nt(dyp, wbp_ref[...])
        dps_ref[...] += jnp.sum(dypre * mixed_ref[...], axis=0, keepdims=True)
        dmixed = (dypre * ps_ref[...]).astype(BF16)
        dmixed_ref[...] = dmixed
        for grp in range(len(POOL_WINDOWS)):
            cols = slice(POOL_GROUP * grp, POOL_GROUP * (grp + 1))
            dpooled_ref[:, cols] = _dot_nt(dmixed[:, cols], wp_ref[grp].astype(BF16))

    return _grid_call(
        body, "mix_bwd", seq // TOK,
        in_specs=[_rows(D_MODEL), _rows(D_MODEL), _rows(D_MODEL), _rows(2 * D_MODEL), _rows(D_POOL),
                  _full(wout.shape), _full(wba.shape), _full(wbp.shape), _full(w_pool.shape), _full((1, D_POOL))],
        out_specs=[_rows(2 * D_MODEL), _rows(D_MODEL), _rows(D_MODEL), _rows(D_ATTN), _rows(D_POOL), _rows(D_POOL),
                   pl.BlockSpec((1, 2 * D_MODEL), lambda i: (0, 0)), pl.BlockSpec((1, D_POOL), lambda i: (0, 0))],
        out_shape=[_sds((seq, 2 * D_MODEL), BF16), _sds((seq, D_MODEL), BF16), _sds((seq, D_MODEL), BF16),
                   _sds((seq, D_ATTN), F32), _sds((seq, D_POOL), BF16), _sds((seq, D_POOL), F32),
                   _sds((1, 2 * D_MODEL), F32), _sds((1, D_POOL), F32)],
        operands=(drb, ya, yp, g, mixed, wout, wba, wbp, w_pool, pool_scale), tasks=tasks)


def _attn_bwd(q, k, v, bias, o, lse, do, cos, sin, tasks=()):
    seq = q.shape[0]
    nb = seq // MOBA_BLOCK
    pair = pl.BlockSpec((seq, LANE), lambda p: (0, p))
    table = pl.BlockSpec((seq, LANE), lambda p: (0, 0))
    scale = HEAD_DIM ** -0.5

    def body(q_ref, k_ref, v_ref, bias_ref, o_ref, lse_ref, do_ref, cos_ref, sin_ref, dq_ref, dk_ref, dv_ref,
             dq_acc, dk_acc, dv_acc, dk_head, dv_head, ka_sc, qa_sc, s_sc, dp_sc, p_sc, ds_sc):
        for hh in range(LANE // HEAD_DIM):
            ls = slice(HEAD_DIM * hh, HEAD_DIM * (hh + 1))
            _store_keys(ka_sc, k_ref, ls)
            vb = v_ref[:, ls]
            dk_head[...] = jnp.zeros_like(dk_head)
            dv_head[...] = jnp.zeros_like(dv_head)
            for i in range(nb):
                rs = slice(MOBA_BLOCK * i, MOBA_BLOCK * (i + 1))
                width = MOBA_BLOCK * (i + 1)
                qa_sc[:, 0:HEAD_DIM] = (q_ref[rs, ls] * scale).astype(BF16)
                qa_sc[:, HEAD_DIM:] = bias_ref[rs, ls]
                s_sc[0:width, :] = _dot_nt(ka_sc[0:width, :], qa_sc[...])
                s_sc[rs, :] = jnp.where(_causal((MOBA_BLOCK, MOBA_BLOCK), transposed=True), s_sc[rs, :], NEG)
                dob = do_ref[rs, ls]
                delta = _row_vector(jnp.sum(dob * o_ref[rs, ls], axis=1, keepdims=True))
                lse_row = lse_ref[0, hh:hh + 1, rs]
                dob16 = dob.astype(BF16)
                dp_sc[0:width, :] = _dot_nt(vb[0:width], dob16)
                for c in range(width // SCORE_CHUNK):
                    rows = slice(SCORE_CHUNK * c, SCORE_CHUNK * (c + 1))
                    p = jnp.exp(s_sc[rows, :] - lse_row)
                    p_sc[rows, :] = p.astype(BF16)
                    ds_sc[rows, :] = (p * (dp_sc[rows, :] - delta)).astype(BF16)
                dv_head[0:width, :] += _dot(p_sc[0:width, :], dob16)
                dk_head[0:width, :] += _dot(ds_sc[0:width, :], qa_sc[:, 0:HEAD_DIM])
                dq_acc[rs, ls] = _dot_tn(ds_sc[0:width, :], ka_sc[0:width, 0:HEAD_DIM]) * scale
            dk_acc[:, ls] = dk_head[...]
            dv_acc[:, ls] = dv_head[...]
        cos_t, sin_t = cos_ref[...], sin_ref[...]
        dq_ref[...] = _rope_transposed(dq_acc[...], cos_t, sin_t).astype(BF16)
        dk_ref[...] = _rope_transposed(dk_acc[...], cos_t, sin_t).astype(BF16)
        dv_ref[...] = dv_acc[...].astype(BF16)

    return _grid_call(
        body, "attn_bwd", D_ATTN // LANE,
        in_specs=[pair, pair, pair, pair, pair, pl.BlockSpec((1, 8, seq), lambda p: (p, 0, 0)), pair, table, table],
        out_specs=[pair, pair, pair], out_shape=[_sds((seq, D_ATTN), BF16)] * 3,
        operands=(q, k, v, bias, o, lse, do, cos, sin),
        scratch=[pltpu.VMEM((seq, LANE), F32)] * 3 + [pltpu.VMEM((seq, HEAD_DIM), F32)] * 2
        + [pltpu.VMEM((seq, LANE), BF16), pltpu.VMEM((MOBA_BLOCK, LANE), BF16)]
        + [pltpu.VMEM((seq, MOBA_BLOCK), F32)] * 2 + [pltpu.VMEM((seq, MOBA_BLOCK), BF16)] * 2,
        tasks=tasks)


def _in_bwd(dq, dk, dv, dpooled, dzg, dr1, win, tasks=()):
    seq = dr1.shape[0]
    nt = seq // TOK

    def body(dq_ref, dk_ref, dv_ref, dp_ref, dpnext_ref, dzg_ref, dr_ref, win_ref, dx_ref, dz_ref, ext):
        i = pl.program_id(0)
        dp = dp_ref[...]
        dpn = jnp.where(i < nt - 1, dpnext_ref[...], 0.0)
        for grp, window in enumerate(POOL_WINDOWS):
            cols = slice(POOL_GROUP * grp, POOL_GROUP * (grp + 1))
            ext[0:TOK, cols] = dp[:, cols] / _pool_count(i * TOK, TOK, window)
            ext[TOK:, cols] = dpn[:, cols] / _pool_count((i + 1) * TOK, POOL_HALO, window)
        for grp, window in enumerate(POOL_WINDOWS):
            cols = slice(POOL_GROUP * grp, POOL_GROUP * (grp + 1))
            acc = ext[0:TOK, cols] - dp[:, cols]
            for kk in range(1, window):
                acc = acc + ext[pl.ds(kk, TOK), cols]
            dz_ref[:, 3 * D_ATTN + POOL_GROUP * grp:3 * D_ATTN + POOL_GROUP * (grp + 1)] = acc.astype(BF16)
        dz_ref[:, 0:D_ATTN] = dq_ref[...]
        dz_ref[:, D_ATTN:2 * D_ATTN] = dk_ref[...]
        dz_ref[:, 2 * D_ATTN:3 * D_ATTN] = dv_ref[...]
        dz_ref[:, 3 * D_ATTN + D_POOL:] = dzg_ref[...]
        dx = ALPHA * dr_ref[...]
        for n in range(N_DEV):
            dx = dx + _dot_nt(dz_ref[:, D_ATTN * n:D_ATTN * (n + 1)], win_ref[n])
        dx_ref[...] = dx

    halo = pl.BlockSpec((POOL_HALO, D_POOL),
                        lambda i: (jnp.minimum((i + 1) * (TOK // POOL_HALO), seq // POOL_HALO - 1), 0))
    return _grid_call(
        body, "in_bwd", nt,
        in_specs=[_rows(D_ATTN), _rows(D_ATTN), _rows(D_ATTN), _rows(D_POOL), halo, _rows(2 * D_MODEL),
                  _rows(D_MODEL), _full(win.shape)],
        out_specs=[_rows(D_MODEL), _rows(D_IN_PROJ)],
        out_shape=[_sds((seq, D_MODEL), F32), _sds((seq, D_IN_PROJ), BF16)],
        operands=(dq, dk, dv, dpooled, dpooled, dzg, dr1, win),
        scratch=[pltpu.VMEM((TOK + POOL_HALO, D_POOL), F32)], tasks=tasks)


def _tn_matmul(name, a, b, out_shape, out_dtype, steps, a_spec, b_spec, o_spec, tasks=()):
    def body(a_ref, b_ref, o_ref):
        r = _dot_tn(a_ref[...].astype(BF16), b_ref[...].astype(BF16))
        o_ref[...] = r.reshape(o_ref.shape).astype(o_ref.dtype)

    (out,), results = _grid_call(body, name, steps, in_specs=[a_spec, b_spec], out_specs=[o_spec],
                                 out_shape=[_sds(out_shape, out_dtype)], operands=(a, b), tasks=tasks)
    return out, results


def _place():
    return lax.axis_index("x"), lax.axis_index("y"), lax.axis_index("c")


def _other_chips(x, y):
    return [(1 - x, y), (x, 1 - y), (1 - x, 1 - y)]


DMA_SEMS = pltpu.SemaphoreType.DMA


class _AllGather:
    def __init__(self, shards):
        self.operands = list(shards)
        self.n = len(shards)
        self.out_shape = [_sds((N_DEV, *s.shape), s.dtype) for s in shards]
        self.sems = [DMA_SEMS((7 * self.n,)), DMA_SEMS((7 * self.n,)), DMA_SEMS((self.n,))]

    def _copy(self, refs, a, k, block, to, from_input=False):
        ins, outs, (send_sems, recv_sems, _) = refs
        px, py, pc = block
        dst = outs[a].at[4 * px + 2 * py + pc]
        return pltpu.make_async_remote_copy(
            src_ref=ins[a] if from_input else dst, dst_ref=dst,
            send_sem=send_sems.at[7 * a + k], recv_sem=recv_sems.at[7 * a + k],
            device_id=to, device_id_type=MESH)

    def _local(self, refs, a):
        ins, outs, (_, _, local_sems) = refs
        x, y, c = _place()
        return pltpu.make_async_copy(ins[a], outs[a].at[4 * x + 2 * y + c], local_sems.at[a])

    def start(self, refs):
        x, y, c = _place()
        for a in range(self.n):
            self._local(refs, a).start()
        for a in range(self.n):
            self._copy(refs, a, 0, (x, y, c), (x, y, 1 - c), True).start()
            for j, chip in enumerate(_other_chips(x, y)):
                self._copy(refs, a, 1 + j, (x, y, c), (*chip, c), True).start()

    def middle(self, refs):
        x, y, c = _place()
        for j, chip in enumerate(_other_chips(x, y)):
            for a in range(self.n):
                self._copy(refs, a, 1 + j, (*chip, c), (x, y, c)).wait_recv()
                self._copy(refs, a, 4 + j, (*chip, c), (x, y, 1 - c)).start()

    def finish(self, refs):
        x, y, c = _place()
        me, sibling = (x, y, c), (x, y, 1 - c)
        chips = _other_chips(x, y)
        for a in range(self.n):
            self._copy(refs, a, 0, sibling, me).wait_recv()
            for j, chip in enumerate(chips):
                self._copy(refs, a, 4 + j, (*chip, 1 - c), me).wait_recv()
        for a in range(self.n):
            self._copy(refs, a, 0, me, sibling, True).wait_send()
            for j, chip in enumerate(chips):
                self._copy(refs, a, 1 + j, me, (*chip, c), True).wait_send()
                self._copy(refs, a, 4 + j, (*chip, c), sibling).wait_send()
            self._local(refs, a).wait()


class _SiblingSend:
    def __init__(self, partials):
        self.operands = list(partials)
        self.n = len(partials)
        self.out_shape = [_sds((4, *p.shape[1:]), p.dtype) for p in partials]
        self.sems = [DMA_SEMS((4 * self.n,)), DMA_SEMS((4 * self.n,))]

    def _copy(self, refs, a, q):
        ins, outs, (send_sems, recv_sems) = refs
        x, y, c = _place()
        return pltpu.make_async_remote_copy(
            src_ref=ins[a].at[2 * q + 1 - c], dst_ref=outs[a].at[q],
            send_sem=send_sems.at[4 * a + q], recv_sem=recv_sems.at[4 * a + q],
            device_id=(x, y, 1 - c), device_id_type=MESH)

    def start(self, refs):
        for a in range(self.n):
            for q in range(4):
                self._copy(refs, a, q).start()

    def middle(self, refs):
        pass

    def finish(self, refs):
        for a in range(self.n):
            for q in range(4):
                self._copy(refs, a, q).wait()


class _ChipScatter:
    def __init__(self, chip_partials):
        self.operands = list(chip_partials)
        self.n = len(chip_partials)
        self.out_shape = [_sds(p.shape, p.dtype) for p in chip_partials]
        self.sems = [DMA_SEMS((3 * self.n,)), DMA_SEMS((3 * self.n,)), DMA_SEMS((self.n,))]

    def _copy(self, refs, a, k, arrival=False):
        ins, outs, (send_sems, recv_sems, _) = refs
        x, y, c = _place()
        px, py = _other_chips(x, y)[k]
        mine, theirs = 2 * x + y, 2 * px + py
        return pltpu.make_async_remote_copy(
            src_ref=ins[a].at[mine if arrival else theirs], dst_ref=outs[a].at[theirs if arrival else mine],
            send_sem=send_sems.at[3 * a + k], recv_sem=recv_sems.at[3 * a + k],
            device_id=(px, py, c), device_id_type=MESH)

    def _local(self, refs, a):
        ins, outs, (_, _, local_sems) = refs
        x, y, _ = _place()
        return pltpu.make_async_copy(ins[a].at[2 * x + y], outs[a].at[2 * x + y], local_sems.at[a])

    def start(self, refs):
        for a in range(self.n):
            self._local(refs, a).start()
            for k in range(3):
                self._copy(refs, a, k).start()

    def middle(self, refs):
        pass

    def finish(self, refs):
        for a in range(self.n):
            for k in range(3):
                self._copy(refs, a, k, arrival=True).wait_recv()
        for a in range(self.n):
            for k in range(3):
                self._copy(refs, a, k).wait_send()
            self._local(refs, a).wait()


class _DirectScatter:
    def __init__(self, partials):
        self.operands = list(partials)
        self.n = len(partials)
        self.out_shape = [_sds(p.shape, p.dtype) for p in partials]
        self.sems = [DMA_SEMS((7 * self.n,)), DMA_SEMS((7 * self.n,)), DMA_SEMS((self.n,))]

    def _copy(self, refs, a, k, arrival=False):
        ins, outs, (send_sems, recv_sems, _) = refs
        x, y, c = _place()
        peer = [(x, y, 1 - c), (1 - x, y, c), (x, 1 - y, c), (1 - x, 1 - y, c),
                (1 - x, y, 1 - c), (x, 1 - y, 1 - c), (1 - x, 1 - y, 1 - c)][k]
        mine, theirs = 4 * x + 2 * y + c, 4 * peer[0] + 2 * peer[1] + peer[2]
        return pltpu.make_async_remote_copy(
            src_ref=ins[a].at[mine if arrival else theirs], dst_ref=outs[a].at[theirs if arrival else mine],
            send_sem=send_sems.at[7 * a + k], recv_sem=recv_sems.at[7 * a + k],
            device_id=peer, device_id_type=MESH)

    def _local(self, refs, a):
        ins, outs, (_, _, local_sems) = refs
        x, y, c = _place()
        return pltpu.make_async_copy(ins[a].at[4 * x + 2 * y + c], outs[a].at[4 * x + 2 * y + c], local_sems.at[a])

    def start(self, refs):
        for a in range(self.n):
            self._local(refs, a).start()
            for k in range(7):
                self._copy(refs, a, k).start()

    def middle(self, refs):
        pass

    def finish(self, refs):
        for a in range(self.n):
            for k in range(7):
                self._copy(refs, a, k, arrival=True).wait_recv()
        for a in range(self.n):
            for k in range(7):
                self._copy(refs, a, k).wait_send()
            self._local(refs, a).wait()


def _task_args(tasks):
    hbm = pl.BlockSpec(memory_space=pl.ANY)
    operands = [o for t in tasks for o in t.operands]
    out_shape = [s for t in tasks for s in t.out_shape]
    sems = [s for t in tasks for s in t.sems]
    return operands, [hbm] * len(operands), out_shape, [hbm] * len(out_shape), sems


def _task_refs(tasks, ins, outs, sems):
    per_task = []
    for t in tasks:
        ni, no, ns = len(t.operands), len(t.out_shape), len(t.sems)
        per_task.append((ins[:ni], outs[:no], sems[:ns]))
        ins, outs, sems = ins[ni:], outs[no:], sems[ns:]
    return per_task


def _task_results(tasks, outs):
    res = []
    for t in tasks:
        res.append(list(outs[:len(t.out_shape)]))
        outs = outs[len(t.out_shape):]
    return res


def _carry(body, tasks, n_in, n_out, n_scratch, steps):
    if not tasks:
        return body
    t_in = sum(len(t.operands) for t in tasks)
    t_out = sum(len(t.out_shape) for t in tasks)

    def wrapped(*refs):
        ins, refs = refs[:n_in], refs[n_in:]
        t_ins, refs = refs[:t_in], refs[t_in:]
        outs, refs = refs[:n_out], refs[n_out:]
        t_outs, refs = refs[:t_out], refs[t_out:]
        scratch, t_sems = refs[:n_scratch], refs[n_scratch:]
        per_task = _task_refs(tasks, t_ins, t_outs, t_sems)
        step = pl.program_id(0)

        @pl.when(step == 0)
        def _():
            for t, r in zip(tasks, per_task):
                t.start(r)

        @pl.when(step == steps - 1)
        def _():
            for t, r in zip(tasks, per_task):
                t.middle(r)

        body(*ins, *outs, *scratch)

        @pl.when(step == steps - 1)
        def _():
            for t, r in zip(tasks, per_task):
                t.finish(r)

    return wrapped


def _exchange(name, tasks):
    operands, in_specs, out_shape, out_specs, sems = _task_args(tasks)

    def body(*refs):
        ni, no = len(operands), len(out_shape)
        per_task = _task_refs(tasks, refs[:ni], refs[ni:ni + no], refs[ni + no:])
        for phase in ("start", "middle", "finish"):
            for t, r in zip(tasks, per_task):
                getattr(t, phase)(r)

    outs = pl.pallas_call(body, name=name, in_specs=in_specs, out_specs=out_specs, out_shape=out_shape,
                          scratch_shapes=sems)(*operands)
    return _task_results(tasks, outs)


def _row_tile(rows, cols, whole_up_to=256 * 1024):
    if rows * cols <= whole_up_to:
        return rows
    for t in (256, 176, 128, 64, 32, 16, 8):
        if rows % t == 0:
            return t
    return rows


def _pair_sum(name, partials, from_sibling):
    _, rows, cols = partials.shape
    tile = _row_tile(rows, cols, 512 * 1024)

    def body(p_ref, s_ref, o_ref):
        mine = jnp.where(lax.axis_index("c") == 0, p_ref[0, 0].astype(F32), p_ref[0, 1].astype(F32))
        o_ref[0] = (mine + s_ref[0].astype(F32)).astype(o_ref.dtype)

    blk = pl.BlockSpec((1, tile, cols), lambda q, i: (q, i, 0))
    return pl.pallas_call(
        body, name=name, grid=(4, rows // tile),
        in_specs=[pl.BlockSpec((1, 2, tile, cols), lambda q, i: (q, 0, i, 0)), blk],
        out_specs=blk, out_shape=_sds(from_sibling.shape, from_sibling.dtype),
        compiler_params=_params("parallel", "parallel"),
    )(partials.reshape(4, 2, rows, cols), from_sibling)


def _sum_leading(name, stacked):
    parts, rows, cols = stacked.shape
    tile = _row_tile(rows, cols, (512 if parts <= 4 else 256) * 1024)

    def body(s_ref, o_ref):
        acc = s_ref[0].astype(F32)
        for d in range(1, parts):
            acc = acc + s_ref[d].astype(F32)
        o_ref[...] = acc

    return pl.pallas_call(
        body, name=name, grid=(rows // tile,),
        in_specs=[pl.BlockSpec((parts, tile, cols), lambda i: (0, i, 0))],
        out_specs=pl.BlockSpec((tile, cols), lambda i: (i, 0)),
        out_shape=_sds((rows, cols), F32),
        compiler_params=_params("parallel"),
    )(stacked)


def _adamw_math(w, g, m, v):
    nm = ADAM_B1 * m + (1.0 - ADAM_B1) * g
    nv = ADAM_B2 * v + (1.0 - ADAM_B2) * (g * g)
    m_hat = nm / (1.0 - ADAM_B1 ** ADAM_STEP)
    v_hat = nv / (1.0 - ADAM_B2 ** ADAM_STEP)
    return -ADAM_LR * (m_hat / (jnp.sqrt(v_hat) + ADAM_EPS) + ADAM_WD * w), nm, nv


def _adamw(name, w, g, m, v):
    rows, cols = w.shape
    tile = _row_tile(rows, cols)

    def body(w_ref, g_ref, m_ref, v_ref, d_ref, nm_ref, nv_ref):
        d_ref[...], nm_ref[...], nv_ref[...] = _adamw_math(w_ref[...], g_ref[...], m_ref[...], v_ref[...])

    blk = pl.BlockSpec((tile, cols), lambda i: (i, 0))
    return pl.pallas_call(
        body, name=name, grid=(rows // tile,),
        in_specs=[blk] * 4, out_specs=[blk] * 3,
        out_shape=[_sds((rows, cols), F32)] * 3,
        compiler_params=_params("parallel"),
    )(w, g, m, v)


def _sum_adamw(name, stacked, w, m, v):
    parts, rows, cols = stacked.shape
    tile = _row_tile(rows, cols)

    def body(s_ref, w_ref, m_ref, v_ref, g_ref, d_ref, nm_ref, nv_ref):
        g = s_ref[0].astype(F32)
        for d in range(1, parts):
            g = g + s_ref[d].astype(F32)
        g_ref[...] = g
        d_ref[...], nm_ref[...], nv_ref[...] = _adamw_math(w_ref[...], g, m_ref[...], v_ref[...])

    blk = pl.BlockSpec((tile, cols), lambda i: (i, 0))
    return pl.pallas_call(
        body, name=name, grid=(rows // tile,),
        in_specs=[pl.BlockSpec((parts, tile, cols), lambda i: (0, i, 0))] + [blk] * 3, out_specs=[blk] * 4,
        out_shape=[_sds((rows, cols), F32)] * 4,
        compiler_params=_params("parallel"),
    )(stacked, w, m, v)


SMALL = ("b_gate", "w_pool", "pool_scale", "ln1_g", "ln1_b", "conv_b", "ln2_g", "ln2_b")
TILE = 8 * LANE


def _pack(parts):
    tiles = []
    for p in parts:
        flat = p.reshape(-1)
        tiles.append(jnp.pad(flat, (0, -flat.size % TILE)).reshape(-1, LANE))
    return jnp.concatenate(tiles, axis=0)


def _unpack(packed, shapes):
    out, at = [], 0
    for shape in shapes:
        size = math.prod(shape)
        rows = -(-size // TILE) * 8
        out.append(packed[at:at + rows].reshape(-1)[:size].reshape(shape))
        at += rows
    return out


MIXER = ("w_branch_attn", "w_branch_pool", "w_out", "conv_w")
FFN = ("w_ffn_gate_t", "w_ffn_up_t", "w_ffn_down")


def _columns(t):
    return jnp.transpose(t, (1, 0, 2)).reshape(t.shape[1], N_DEV * t.shape[2])


def _row_blocks(t):
    return t.reshape(N_DEV * t.shape[1], t.shape[2])


def _by_owner(t):
    return t.reshape(N_DEV, t.shape[0] // N_DEV, t.shape[1])


def _reduce_halves(names, partials, from_sibling):
    return [_pair_sum("pair_sum_" + n, p, s) for n, p, s in zip(names, partials, from_sibling)]


def _local_step(x, target, shards, small):
    seq = x.shape[0]
    cos, sin = _rope_tables(seq)
    whole = lambda width: pl.BlockSpec((seq, width), lambda *_: (0, 0))
    ((w_in_all,),) = _exchange("gather_w_in", [_AllGather([shards["w_in"]])])
    (xb, q, k, v, u, g, kmean), (mixer,) = _proj_in(
        x, w_in_all, small["b_gate"], cos, sin, tasks=[_AllGather([shards[n] for n in MIXER])])
    wba, wbp, wout, conv_w = _columns(mixer[0]), _columns(mixer[1]), _row_blocks(mixer[2]), _columns(mixer[3])
    (o, lse, bias), ((wgt,),) = _attn_fwd(
        q, k, v, kmean.reshape(seq // MOBA_BLOCK, D_ATTN), tasks=[_AllGather([shards["w_ffn_gate_t"]])])
    (ya, yp, pooled, mixed, ypre, merged, xhat1, rstd1, h1, h1b), ((wut,),) = _mix(
        o, u, g, x, wba, wbp, wout, small["w_pool"], small["pool_scale"], small["ln1_g"], small["ln1_b"],
        tasks=[_AllGather([shards["w_ffn_up_t"]])])
    wgt, wut = _row_blocks(wgt), _row_blocks(wut)
    (a, uf, act), ((wd,),) = _ffn_up(
        h1b, wgt, wut, conv_w, small["conv_b"], tasks=[_AllGather([shards["w_ffn_down"]])])
    wd = _row_blocks(wd)
    dr2, dr2b, loss, dg2, db2 = _ffn_down(act, wd, h1, target, small["ln2_g"], small["ln2_b"])

    da, du, dwd, dwg, dwu, dconv = _ffn_bwd(dr2b, h1b, a, uf, wd, conv_w, small["conv_b"])
    ffn_partials = [_by_owner(dwg), _by_owner(dwu), _by_owner(dwd)]
    (dr1, dr1b, dg1, db1), (ffn_sibling,) = _ln1_bwd(
        dr2, da, du, wgt, wut, xhat1, rstd1, small["ln1_g"], tasks=[_SiblingSend(ffn_partials)])
    ffn_chip = _reduce_halves(FFN, ffn_partials, ffn_sibling)
    (dzg, dya, dyp, do, dmixed, dpooled, dbg, dps), (gate_landed,) = _mix_bwd(
        dr1b, ya, yp, g, mixed, wout, wba, wbp, small["w_pool"], small["pool_scale"],
        tasks=[_ChipScatter(ffn_chip[0:1])])
    dw_out, _ = _tn_matmul(
        "dw_out", merged, dr1b, (D_MODEL, D_MODEL), BF16, 4,
        pl.BlockSpec((seq, 256), lambda m: (0, m)), whole(D_MODEL), pl.BlockSpec((256, D_MODEL), lambda m: (m, 0)))
    dw_ba, _ = _tn_matmul(
        "dw_branch_attn", o, dya, (N_DEV, D_ATTN, LANE), BF16, N_DEV,
        whole(D_ATTN), pl.BlockSpec((seq, LANE), lambda n: (0, n)), pl.BlockSpec((1, D_ATTN, LANE), lambda n: (n, 0, 0)))
    dw_bp, _ = _tn_matmul(
        "dw_branch_pool", ypre, dyp, (N_DEV, D_POOL, LANE), BF16, N_DEV,
        whole(D_POOL), pl.BlockSpec((seq, LANE), lambda n: (0, n)), pl.BlockSpec((1, D_POOL, LANE), lambda n: (n, 0, 0)))
    dw_pool, _ = _tn_matmul(
        "dw_pool", pooled, dmixed, (len(POOL_WINDOWS), POOL_GROUP, POOL_GROUP), F32, len(POOL_WINDOWS),
        pl.BlockSpec((seq, POOL_GROUP), lambda n: (0, n)), pl.BlockSpec((seq, POOL_GROUP), lambda n: (0, n)),
        pl.BlockSpec((1, POOL_GROUP, POOL_GROUP), lambda n: (n, 0, 0)))
    (dq, dk, dv), (up_down_landed,) = _attn_bwd(
        q, k, v, bias, o, lse, do, cos, sin, tasks=[_ChipScatter(ffn_chip[1:3])])
    (grad_x, dz), _ = _in_bwd(dq, dk, dv, dpooled, dzg, dr1, w_in_all)
    dw_in, (mixer_landed,) = _tn_matmul(
        "dw_in", xb, dz, (N_DEV, D_MODEL, D_ATTN), BF16, 2 * N_DEV,
        pl.BlockSpec((seq, 512), lambda s: (0, s % 2)), pl.BlockSpec((seq, D_ATTN), lambda s: (0, s // 2)),
        pl.BlockSpec((1, 512, D_ATTN), lambda s: (s // 2, s % 2, 0)),
        tasks=[_DirectScatter([dw_ba, dw_bp, _by_owner(dw_out)])])

    landed = dict(zip(FFN + MIXER[:3], gate_landed + up_down_landed + mixer_landed))
    little = {"b_gate": dbg, "w_pool": dw_pool, "pool_scale": dps, "ln1_g": dg1, "ln1_b": db1, "conv_b": dconv[3:4],
              "ln2_g": dg2, "ln2_b": db2, "conv_w": dconv[0:3]}
    return loss[0, 0], grad_x, landed, dw_in, little


def kernel(x, w_in, b_gate, w_branch_attn, w_pool, pool_scale, w_branch_pool, w_out, ln1_g, ln1_b, w_ffn_gate, w_ffn_up, conv_w, conv_b, w_ffn_down, ln2_g, ln2_b, loss_target, m_w_in, m_b_gate, m_w_branch_attn, m_w_pool, m_pool_scale, m_w_branch_pool, m_w_out, m_ln1_g, m_ln1_b, m_w_ffn_gate, m_w_ffn_up, m_conv_w, m_conv_b, m_w_ffn_down, m_ln2_g, m_ln2_b, v_w_in, v_b_gate, v_w_branch_attn, v_w_pool, v_pool_scale, v_w_branch_pool, v_w_out, v_ln1_g, v_ln1_b, v_w_ffn_gate, v_w_ffn_up, v_conv_w, v_conv_b, v_w_ffn_down, v_ln2_g, v_ln2_b):
    me = 4 * lax.axis_index("x") + 2 * lax.axis_index("y") + lax.axis_index("c")
    weights = dict(w_in=w_in, b_gate=b_gate, w_branch_attn=w_branch_attn, w_pool=w_pool, pool_scale=pool_scale,
                   w_branch_pool=w_branch_pool, w_out=w_out, ln1_g=ln1_g, ln1_b=ln1_b, w_ffn_gate=w_ffn_gate,
                   w_ffn_up=w_ffn_up, conv_w=conv_w, conv_b=conv_b, w_ffn_down=w_ffn_down, ln2_g=ln2_g, ln2_b=ln2_b)
    m_in = dict(w_in=m_w_in, b_gate=m_b_gate, w_branch_attn=m_w_branch_attn, w_pool=m_w_pool,
                pool_scale=m_pool_scale, w_branch_pool=m_w_branch_pool, w_out=m_w_out, ln1_g=m_ln1_g, ln1_b=m_ln1_b,
                w_ffn_gate=m_w_ffn_gate, w_ffn_up=m_w_ffn_up, conv_w=m_conv_w, conv_b=m_conv_b,
                w_ffn_down=m_w_ffn_down, ln2_g=m_ln2_g, ln2_b=m_ln2_b)
    v_in = dict(w_in=v_w_in, b_gate=v_b_gate, w_branch_attn=v_w_branch_attn, w_pool=v_w_pool,
                pool_scale=v_pool_scale, w_branch_pool=v_w_branch_pool, w_out=v_w_out, ln1_g=v_ln1_g, ln1_b=v_ln1_b,
                w_ffn_gate=v_w_ffn_gate, w_ffn_up=v_w_ffn_up, conv_w=v_conv_w, conv_b=v_conv_b,
                w_ffn_down=v_w_ffn_down, ln2_g=v_ln2_g, ln2_b=v_ln2_b)
    weights = {n: a[0] for n, a in weights.items()}
    m_in = {n: a[0] for n, a in m_in.items()}
    v_in = {n: a[0] for n, a in v_in.items()}

    shards = {"w_in": weights["w_in"].astype(BF16), "w_branch_attn": weights["w_branch_attn"].astype(BF16),
              "w_branch_pool": weights["w_branch_pool"].astype(BF16), "w_out": weights["w_out"].astype(BF16),
              "w_ffn_gate_t": weights["w_ffn_gate"].T.astype(BF16), "w_ffn_up_t": weights["w_ffn_up"].T.astype(BF16),
              "w_ffn_down": weights["w_ffn_down"].astype(BF16), "conv_w": weights["conv_w"]}
    small = {"b_gate": weights["b_gate"][None], "w_pool": weights["w_pool"], "pool_scale": weights["pool_scale"][None],
             "ln1_g": weights["ln1_g"][None], "ln1_b": weights["ln1_b"][None], "conv_b": weights["conv_b"][None],
             "ln2_g": weights["ln2_g"][None], "ln2_b": weights["ln2_b"][None]}

    loss_part, grad_x, landed, dw_in, little = _local_step(x[0], loss_target[0], shards, small)
    loss = lax.psum(loss_part, ("x", "y", "c"))

    ((w_in_sibling,),) = _exchange("sibling_grads", [_SiblingSend([dw_in])])
    w_in_chip = _reduce_halves(["w_in"], [dw_in], [w_in_sibling])
    names = SMALL + ("conv_w",)
    (landed["w_in"],), (all_small,) = _exchange(
        "scatter_grads", [_ChipScatter(w_in_chip), _AllGather([_pack([little[n] for n in names])])])

    grads, delta, new_m, new_v = {}, {}, {}, {}
    for n in ("w_in", "w_branch_attn", "w_branch_pool", "w_out", "w_ffn_down"):
        grads[n], delta[n], new_m[n], new_v[n] = _sum_adamw("update_" + n, landed[n], weights[n], m_in[n], v_in[n])
    for n in ("w_ffn_gate", "w_ffn_up"):
        updated = _sum_adamw("update_" + n, landed[n + "_t"], weights[n].T, m_in[n].T, v_in[n].T)
        grads[n], delta[n], new_m[n], new_v[n] = (t.T for t in updated)
    small_sum = _sum_leading("sum_small", all_small)
    *small_grads, conv_w_grad = _unpack(small_sum, [weights[n].shape for n in SMALL] + [(3, D_FF)])
    grads.update(zip(SMALL, small_grads))
    grads["conv_w"] = lax.dynamic_slice(conv_w_grad, (0, me * FF_SHARD), (3, FF_SHARD))
    flat = lambda d: _pack([d[n] for n in names])
    shapes = [weights[n].shape for n in names]
    for out, packed in zip((delta, new_m, new_v),
                           _adamw("adamw_small", flat(weights), flat(grads), flat(m_in), flat(v_in))):
        out.update(zip(names, _unpack(packed, shapes)))

    order = ("w_in", "b_gate", "w_branch_attn", "w_pool", "pool_scale", "w_branch_pool", "w_out", "ln1_g", "ln1_b",
             "w_ffn_gate", "w_ffn_up", "conv_w", "conv_b", "w_ffn_down", "ln2_g", "ln2_b")
    lead = lambda t: t[None]
    return (loss, lead(grad_x), *[lead(grads[n]) for n in order], *[lead(delta[n]) for n in order],
            *[lead(new_m[n]) for n in order], *[lead(new_v[n]) for n in order])
```

```python
import functools
import math

import jax
import jax.numpy as jnp
from jax import lax
from jax.experimental import pallas as pl
from jax.experimental.pallas import tpu as pltpu

F32 = jnp.float32
BF16 = jnp.bfloat16

D_MODEL = 1024
N_HEADS = 8
HEAD_DIM = 64
D_ATTN = N_HEADS * HEAD_DIM
MOBA_BLOCK = 256
MOBA_TOPK = 3
ROPE_THETA = 10000.0
POOL_WINDOWS = (2, 4, 8, 16)
POOL_GROUP = 128
D_POOL = len(POOL_WINDOWS) * POOL_GROUP
POOL_HALO = 16
D_FF = 2816
D_IN_PROJ = 3 * D_ATTN + D_POOL + 2 * D_MODEL
LN_EPS = 1e-5
ALPHA = 2.0 ** 0.25
NEG = -1e30
N_DEV = 8
FF_SHARD = D_FF // N_DEV

ADAM_LR = 0.001
ADAM_B1 = 0.9
ADAM_B2 = 0.999
ADAM_EPS = 1e-08
ADAM_WD = 0.01
ADAM_STEP = 10

TOK = 256
FF_CHUNK = 256
LANE = 128
VMEM_LIMIT = 56 * 1024 * 1024

MESH = pl.DeviceIdType.MESH
NT_DIMS = (((1,), (1,)), ((), ()))
TN_DIMS = (((0,), (0,)), ((), ()))


def _params(*sem):
    return pltpu.CompilerParams(dimension_semantics=sem or None, vmem_limit_bytes=VMEM_LIMIT)


def _full(shape):
    zeros = (0,) * len(shape)
    return pl.BlockSpec(shape, lambda *_: zeros, pipeline_mode=pl.Buffered(1))


def _rows(width, tile=TOK):
    return pl.BlockSpec((tile, width), lambda i: (i, 0))


def _sds(shape, dtype):
    return jax.ShapeDtypeStruct(shape, dtype)


def _dot(a, b):
    return jnp.dot(a, b, preferred_element_type=F32)


def _dot_nt(a, b):
    return lax.dot_general(a, b, NT_DIMS, preferred_element_type=F32)


def _dot_tn(a, b):
    return lax.dot_general(a, b, TN_DIMS, preferred_element_type=F32)


def _rope_tables(seq):
    half = HEAD_DIM // 2
    inv_freq = 1.0 / (ROPE_THETA ** (jnp.arange(half, dtype=F32) / half))
    ang = jnp.arange(seq, dtype=F32)[:, None] * inv_freq[None, :]
    cos, sin = jnp.cos(ang), jnp.sin(ang)
    return jnp.tile(cos, (1, 4)), jnp.tile(jnp.concatenate([-sin, sin], axis=1), (1, 2))


def _swap_halves(t):
    lane = lax.broadcasted_iota(jnp.int32, t.shape, 1)
    return jnp.where((lane % HEAD_DIM) < HEAD_DIM // 2, pltpu.roll(t, LANE - 32, 1), pltpu.roll(t, 32, 1))


def _rope(t, cos, sin):
    return t * cos + _swap_halves(t) * sin


def _rope_transposed(g, cos, sin):
    return g * cos + _swap_halves(g * sin)


def _ln_fwd(r, g, b):
    mu = jnp.mean(r, axis=-1, keepdims=True)
    xc = r - mu
    var = jnp.mean(xc * xc, axis=-1, keepdims=True)
    rstd = lax.rsqrt(var + LN_EPS)
    xhat = xc * rstd
    return xhat * g + b, xhat, rstd


def _ln_bwd(dy, xhat, rstd, g):
    dxh = dy * g
    m1 = jnp.mean(dxh, axis=-1, keepdims=True)
    m2 = jnp.mean(dxh * xhat, axis=-1, keepdims=True)
    return rstd * (dxh - m1 - xhat * m2)


def _normal_cdf(a):
    return 0.5 * (1.0 + lax.erf(a * (1.0 / math.sqrt(2.0))))


def _gelu_derivative(a, cdf):
    return cdf + a * (jnp.exp(-0.5 * a * a) * (1.0 / math.sqrt(2.0 * math.pi)))


def _shift_down(a, k):
    row = lax.broadcasted_iota(jnp.int32, a.shape, 0)
    return jnp.where(row >= k, pltpu.roll(a, k, 0), 0.0)


def _shift_up(a, k):
    n = a.shape[0]
    row = lax.broadcasted_iota(jnp.int32, a.shape, 0)
    return jnp.where(row < n - k, pltpu.roll(a, n - k, 0), 0.0)


def _conv(a, cw, cb):
    return cw[2:3, :] * a + cw[1:2, :] * _shift_down(a, 1) + cw[0:1, :] * _shift_down(a, 2) + cb


def _pool_count(first_row, rows, window):
    t = first_row + lax.broadcasted_iota(jnp.int32, (rows, 1), 0)
    return jnp.minimum(t + 1, window).astype(F32)


def _grid_call(body, name, steps, in_specs, out_specs, out_shape, operands, scratch=(), tasks=()):
    t_operands, t_in_specs, t_out_shape, t_out_specs, t_sems = _task_args(tasks)
    outs = pl.pallas_call(
        _carry(body, tasks, len(in_specs), len(out_specs), len(scratch), steps), name=name, grid=(steps,),
        in_specs=list(in_specs) + t_in_specs, out_specs=list(out_specs) + t_out_specs,
        out_shape=list(out_shape) + t_out_shape, scratch_shapes=list(scratch) + t_sems,
        compiler_params=_params("arbitrary"),
    )(*operands, *t_operands)
    return outs[:len(out_specs)], _task_results(tasks, outs[len(out_specs):])


def _proj_in(x, win, b_gate, cos, sin, tasks=()):
    seq = x.shape[0]
    nt = seq // TOK

    def body(x_ref, win_ref, bg_ref, cos_ref, sin_ref, xb_ref, q_ref, k_ref, v_ref, u_ref, g_ref, km_ref):
        xb = x_ref[...].astype(BF16)
        xb_ref[...] = xb
        cos_t, sin_t = cos_ref[...], sin_ref[...]
        for sec, out_ref in ((0, q_ref), (1, k_ref)):
            z = _dot(xb, win_ref[sec])
            for c in range(D_ATTN // LANE):
                cols = slice(LANE * c, LANE * (c + 1))
                out_ref[:, cols] = _rope(z[:, cols], cos_t, sin_t)
        km_ref[0] = jnp.mean(k_ref[...], axis=0, keepdims=True)
        v_ref[...] = _dot(xb, win_ref[2]).astype(BF16)
        u_ref[...] = _dot(xb, win_ref[3])
        for n in range(4):
            cols = slice(D_ATTN * n, D_ATTN * (n + 1))
            g_ref[:, cols] = jax.nn.sigmoid(_dot(xb, win_ref[4 + n]) + bg_ref[:, cols])

    return _grid_call(
        body, "proj_in", nt,
        in_specs=[_rows(D_MODEL), _full(win.shape), _full((1, 2 * D_MODEL)), _rows(LANE), _rows(LANE)],
        out_specs=[_rows(D_MODEL), _rows(D_ATTN), _rows(D_ATTN), _rows(D_ATTN), _rows(D_POOL), _rows(2 * D_MODEL),
                   pl.BlockSpec((1, 1, D_ATTN), lambda i: (i, 0, 0))],
        out_shape=[_sds((seq, D_MODEL), BF16), _sds((seq, D_ATTN), F32), _sds((seq, D_ATTN), F32),
                   _sds((seq, D_ATTN), BF16), _sds((seq, D_POOL), F32), _sds((seq, 2 * D_MODEL), F32),
                   _sds((nt, 1, D_ATTN), F32)],
        operands=(x, win, b_gate, cos, sin), tasks=tasks)


SCORE_CHUNK = 128


def _store_keys(ka_sc, k_ref, ls):
    seq = ka_sc.shape[0]
    ka_sc[:, 0:HEAD_DIM] = k_ref[:, ls].astype(BF16)
    row = lax.broadcasted_iota(jnp.int32, (seq, HEAD_DIM), 0)
    lane = lax.broadcasted_iota(jnp.int32, (seq, HEAD_DIM), 1)
    in_block = (lane * MOBA_BLOCK <= row) & (row < (lane + 1) * MOBA_BLOCK)
    ka_sc[:, HEAD_DIM:] = jnp.where(in_block, 1.0, 0.0).astype(BF16)


def _block_bias(qf, km, i):
    if i <= MOBA_TOPK:
        return jnp.zeros((MOBA_BLOCK, HEAD_DIM), BF16)
    nb = km.shape[0]
    gate = lax.dot_general(km, qf, NT_DIMS, precision=lax.Precision.HIGHEST, preferred_element_type=F32)
    blk = lax.broadcasted_iota(jnp.int32, gate.shape, 0)
    rank = jnp.zeros(gate.shape, F32)
    for r in range(1, i):
        lower = pltpu.roll(gate, r, 0)
        rank = rank + jnp.where((blk >= r) & (lower >= gate), 1.0, 0.0)
        higher = pltpu.roll(gate, nb - r, 0)
        rank = rank + jnp.where((blk + r < i) & (higher > gate), 1.0, 0.0)
    bias = jnp.where((blk < i) & (rank >= MOBA_TOPK), NEG, 0.0)
    padded = jnp.concatenate([bias, jnp.zeros((LANE - nb, MOBA_BLOCK), F32)], axis=0)
    return jnp.transpose(padded)[:, 0:HEAD_DIM].astype(BF16)


def _causal(shape, transposed=False):
    row = lax.broadcasted_iota(jnp.int32, shape, 0)
    col = lax.broadcasted_iota(jnp.int32, shape, 1)
    return (row <= col) if transposed else (col <= row)


def _row_vector(col):
    return jnp.transpose(jnp.broadcast_to(col, (MOBA_BLOCK, LANE)))[0:1, :]


def _attn_fwd(q, k, v, kmean, tasks=()):
    seq = q.shape[0]
    nb = seq // MOBA_BLOCK
    assert nb == 8, "the block ranking keeps one sublane per key block"
    pair = pl.BlockSpec((seq, LANE), lambda p: (0, p))
    heads = LANE // HEAD_DIM

    def body(q_ref, k_ref, v_ref, km_ref, o_ref, lse_ref, bias_ref, ka_sc, qa_sc, s_sc, p_sc):
        lse_ref[0, heads:, :] = jnp.zeros((8 - heads, seq), F32)
        for hh in range(heads):
            ls = slice(HEAD_DIM * hh, HEAD_DIM * (hh + 1))
            _store_keys(ka_sc, k_ref, ls)
            vb = v_ref[:, ls]
            km = km_ref[:, ls]
            for i in range(nb):
                rs = slice(MOBA_BLOCK * i, MOBA_BLOCK * (i + 1))
                width = MOBA_BLOCK * (i + 1)
                qf = q_ref[rs, ls]
                bias = _block_bias(qf, km, i)
                bias_ref[rs, ls] = bias
                qa_sc[:, 0:HEAD_DIM] = (qf * HEAD_DIM ** -0.5).astype(BF16)
                qa_sc[:, HEAD_DIM:] = bias
                s_sc[:, 0:width] = _dot_nt(qa_sc[...], ka_sc[0:width, :])
                s_sc[:, rs] = jnp.where(_causal((MOBA_BLOCK, MOBA_BLOCK)), s_sc[:, rs], NEG)
                chunks = [slice(SCORE_CHUNK * c, SCORE_CHUNK * (c + 1)) for c in range(width // SCORE_CHUNK)]
                top = s_sc[:, chunks[0]]
                for c in chunks[1:]:
                    top = jnp.maximum(top, s_sc[:, c])
                m = jnp.max(top, axis=1, keepdims=True)
                total = jnp.zeros((MOBA_BLOCK, SCORE_CHUNK), F32)
                for c in chunks:
                    p = jnp.exp(s_sc[:, c] - m)
                    total = total + p
                    p_sc[:, c] = p.astype(BF16)
                l = jnp.sum(total, axis=1, keepdims=True)
                o_ref[rs, ls] = _dot(p_sc[:, 0:width], vb[0:width]) / l
                lse_ref[0, hh:hh + 1, rs] = _row_vector(m + jnp.log(l))

    return _grid_call(
        body, "attn_fwd", D_ATTN // LANE,
        in_specs=[pair, pair, pair, pl.BlockSpec((nb, LANE), lambda p: (0, p))],
        out_specs=[pair, pl.BlockSpec((1, 8, seq), lambda p: (p, 0, 0)), pair],
        out_shape=[_sds((seq, D_ATTN), F32), _sds((D_ATTN // LANE, 8, seq), F32), _sds((seq, D_ATTN), BF16)],
        operands=(q, k, v, kmean),
        scratch=[pltpu.VMEM((seq, LANE), BF16), pltpu.VMEM((MOBA_BLOCK, LANE), BF16),
                 pltpu.VMEM((MOBA_BLOCK, seq), F32), pltpu.VMEM((MOBA_BLOCK, seq), BF16)],
        tasks=tasks)


def _mix(o, u, g, x, wba, wbp, wout, w_pool, pool_scale, ln_g, ln_b, tasks=()):
    seq = x.shape[0]

    def body(o_ref, u_ref, uprev_ref, g_ref, x_ref, wba_ref, wbp_ref, wout_ref, wp_ref, ps_ref, lg_ref, lb_ref,
             ya_ref, yp_ref, pooled_ref, mixed_ref, ypre_ref, merged_ref, xhat_ref, rstd_ref, h_ref, hb_ref, ext):
        i = pl.program_id(0)
        ya = _dot(o_ref[...].astype(BF16), wba_ref[...])
        ucur = u_ref[...]
        ext[0:POOL_HALO, :] = jnp.where(i > 0, uprev_ref[...], 0.0)
        ext[POOL_HALO:, :] = ucur
        for grp, window in enumerate(POOL_WINDOWS):
            cols = slice(POOL_GROUP * grp, POOL_GROUP * (grp + 1))
            acc = ucur[:, cols]
            for kk in range(1, window):
                acc = acc + ext[pl.ds(POOL_HALO - kk, TOK), cols]
            pooled = acc / _pool_count(i * TOK, TOK, window) - ucur[:, cols]
            pooled_ref[:, cols] = pooled.astype(BF16)
            mixed_ref[:, cols] = _dot(pooled.astype(BF16), wp_ref[grp].astype(BF16))
        mixed = mixed_ref[...]
        ypre = (mixed * ps_ref[...]).astype(BF16)
        ypre_ref[...] = ypre
        yp = _dot(ypre, wbp_ref[...])
        ya_ref[...] = ya
        yp_ref[...] = yp
        merged = (g_ref[:, :D_MODEL] * ya + g_ref[:, D_MODEL:] * yp).astype(BF16)
        merged_ref[...] = merged
        r1 = ALPHA * x_ref[...] + _dot(merged, wout_ref[...])
        h, xhat, rstd = _ln_fwd(r1, lg_ref[...], lb_ref[...])
        xhat_ref[...] = xhat
        rstd_ref[...] = jnp.broadcast_to(rstd, (TOK, LANE))
        h_ref[...] = h
        hb_ref[...] = h.astype(BF16)

    halo = pl.BlockSpec((POOL_HALO, D_POOL), lambda i: (jnp.maximum(i * (TOK // POOL_HALO) - 1, 0), 0))
    return _grid_call(
        body, "mix", seq // TOK,
        in_specs=[_rows(D_ATTN), _rows(D_POOL), halo, _rows(2 * D_MODEL), _rows(D_MODEL),
                  _full(wba.shape), _full(wbp.shape), _full(wout.shape), _full(w_pool.shape),
                  _full((1, D_POOL)), _full((1, D_MODEL)), _full((1, D_MODEL))],
        out_specs=[_rows(D_MODEL), _rows(D_MODEL), _rows(D_POOL), _rows(D_POOL), _rows(D_POOL), _rows(D_MODEL),
                   _rows(D_MODEL), _rows(LANE), _rows(D_MODEL), _rows(D_MODEL)],
        out_shape=[_sds((seq, D_MODEL), F32), _sds((seq, D_MODEL), F32), _sds((seq, D_POOL), BF16),
                   _sds((seq, D_POOL), F32), _sds((seq, D_POOL), BF16), _sds((seq, D_MODEL), BF16),
                   _sds((seq, D_MODEL), F32), _sds((seq, LANE), F32), _sds((seq, D_MODEL), F32),
                   _sds((seq, D_MODEL), BF16)],
        operands=(o, u, u, g, x, wba, wbp, wout, w_pool, pool_scale, ln_g, ln_b),
        scratch=[pltpu.VMEM((TOK + POOL_HALO, D_POOL), F32)], tasks=tasks)


def _ffn_up(hb, wgt, wut, conv_w, conv_b, tasks=()):
    seq = hb.shape[0]
    wblk = pl.BlockSpec((FF_CHUNK, D_MODEL), lambda c: (c, 0))
    cblk = lambda rows: pl.BlockSpec((rows, FF_CHUNK), lambda c: (0, c))
    oblk = pl.BlockSpec((seq, FF_CHUNK), lambda c: (0, c))

    def body(h_ref, wg_ref, wu_ref, cw_ref, cb_ref, a_ref, u_ref, cdf_ref, act_ref):
        h = h_ref[...]
        a = _dot_nt(h, wg_ref[...])
        u = _dot_nt(h, wu_ref[...])
        a_ref[...] = a
        u_ref[...] = u
        ac = _conv(a, cw_ref[...], cb_ref[...])
        cdf = _normal_cdf(ac)
        cdf_ref[...] = cdf
        act_ref[...] = (ac * cdf * u).astype(BF16)

    return _grid_call(
        body, "ffn_up", D_FF // FF_CHUNK,
        in_specs=[_full(hb.shape), wblk, wblk, cblk(3), cblk(1)],
        out_specs=[oblk, oblk, oblk, oblk],
        out_shape=[_sds((seq, D_FF), F32), _sds((seq, D_FF), F32), _sds((seq, D_FF), F32), _sds((seq, D_FF), BF16)],
        operands=(hb, wgt, wut, conv_w, conv_b), tasks=tasks)


def _ffn_down(act, wd, h, target, ln_g, ln_b):
    seq = h.shape[0]

    def body(act_ref, wd_ref, h_ref, t_ref, lg_ref, lb_ref, dr_ref, drb_ref, loss_ref, dg_ref, db_ref):
        i = pl.program_id(0)

        @pl.when(i == 0)
        def _():
            loss_ref[...] = jnp.zeros_like(loss_ref)
            dg_ref[...] = jnp.zeros_like(dg_ref)
            db_ref[...] = jnp.zeros_like(db_ref)

        r2 = ALPHA * h_ref[...] + _dot(act_ref[...], wd_ref[...])
        y, xhat, rstd = _ln_fwd(r2, lg_ref[...], lb_ref[...])
        diff = y - t_ref[...]
        loss_ref[...] += jnp.sum(diff * diff) * (0.5 / D_MODEL)
        dy = diff * (1.0 / D_MODEL)
        dg_ref[...] += jnp.sum(dy * xhat, axis=0, keepdims=True)
        db_ref[...] += jnp.sum(dy, axis=0, keepdims=True)
        dr = _ln_bwd(dy, xhat, rstd, lg_ref[...])
        dr_ref[...] = dr
        drb_ref[...] = dr.astype(BF16)

    vec = pl.BlockSpec((1, D_MODEL), lambda i: (0, 0))
    return pl.pallas_call(
        body, name="ffn_down", grid=(seq // TOK,),
        in_specs=[_rows(D_FF), _full(wd.shape), _rows(D_MODEL), _rows(D_MODEL), _full((1, D_MODEL)), _full((1, D_MODEL))],
        out_specs=[_rows(D_MODEL), _rows(D_MODEL), pl.BlockSpec((8, LANE), lambda i: (0, 0)), vec, vec],
        out_shape=[_sds((seq, D_MODEL), F32), _sds((seq, D_MODEL), BF16), _sds((8, LANE), F32),
                   _sds((1, D_MODEL), F32), _sds((1, D_MODEL), F32)],
        compiler_params=_params("arbitrary"),
    )(act, wd, h, target, ln_g, ln_b)


def _ffn_bwd(drb, hb, a, u, cdf, act, wd, conv_w, conv_b):
    seq = hb.shape[0]
    wblk = pl.BlockSpec((FF_CHUNK, D_MODEL), lambda c: (c, 0))
    cblk = lambda rows: pl.BlockSpec((rows, FF_CHUNK), lambda c: (0, c))
    sblk = pl.BlockSpec((seq, FF_CHUNK), lambda c: (0, c))

    def body(dr_ref, h_ref, a_ref, u_ref, cdf_ref, act_ref, wd_ref, cw_ref, cb_ref,
             da_ref, du_ref, dwd_ref, dwg_ref, dwu_ref, dc_ref):
        dr = dr_ref[...]
        h = h_ref[...]
        a = a_ref[...]
        cw = cw_ref[...]
        dact = _dot_nt(dr, wd_ref[...])
        dwd_ref[...] = _dot_tn(act_ref[...], dr).astype(BF16)
        ac = _conv(a, cw, cb_ref[...])
        cdf = cdf_ref[...]
        du = (dact * (ac * cdf)).astype(BF16)
        dac = dact * u_ref[...] * _gelu_derivative(ac, cdf)
        da = (cw[2:3, :] * dac + cw[1:2, :] * _shift_up(dac, 1) + cw[0:1, :] * _shift_up(dac, 2)).astype(BF16)
        da_ref[...] = da
        du_ref[...] = du
        dwg_ref[...] = _dot_tn(da, h).astype(BF16)
        dwu_ref[...] = _dot_tn(du, h).astype(BF16)
        dc_ref[0:1, :] = jnp.sum(dac * _shift_down(a, 2), axis=0, keepdims=True)
        dc_ref[1:2, :] = jnp.sum(dac * _shift_down(a, 1), axis=0, keepdims=True)
        dc_ref[2:3, :] = jnp.sum(dac * a, axis=0, keepdims=True)
        dc_ref[3:4, :] = jnp.sum(dac, axis=0, keepdims=True)
        dc_ref[4:8, :] = jnp.zeros((4, FF_CHUNK), F32)

    return pl.pallas_call(
        body, name="ffn_bwd", grid=(D_FF // FF_CHUNK,),
        in_specs=[_full(drb.shape), _full(hb.shape), sblk, sblk, sblk, sblk, wblk, cblk(3), cblk(1)],
        out_specs=[sblk, sblk, wblk, wblk, wblk, cblk(8)],
        out_shape=[_sds((seq, D_FF), BF16), _sds((seq, D_FF), BF16), _sds((D_FF, D_MODEL), BF16),
                   _sds((D_FF, D_MODEL), BF16), _sds((D_FF, D_MODEL), BF16), _sds((8, D_FF), F32)],
        compiler_params=_params("parallel"),
    )(drb, hb, a, u, cdf, act, wd, conv_w, conv_b)


def _ln1_bwd(dr2, da, du, wgt, wut, xhat, rstd, ln_g, tasks=()):
    seq = dr2.shape[0]

    def body(dr2_ref, da_ref, du_ref, wg_ref, wu_ref, xhat_ref, rstd_ref, lg_ref, dr_ref, drb_ref, dg_ref, db_ref):
        @pl.when(pl.program_id(0) == 0)
        def _():
            dg_ref[...] = jnp.zeros_like(dg_ref)
            db_ref[...] = jnp.zeros_like(db_ref)

        dh = ALPHA * dr2_ref[...] + _dot(da_ref[...], wg_ref[...]) + _dot(du_ref[...], wu_ref[...])
        xhat = xhat_ref[...]
        dg_ref[...] += jnp.sum(dh * xhat, axis=0, keepdims=True)
        db_ref[...] += jnp.sum(dh, axis=0, keepdims=True)
        dr = _ln_bwd(dh, xhat, rstd_ref[:, 0:1], lg_ref[...])
        dr_ref[...] = dr
        drb_ref[...] = dr.astype(BF16)

    vec = pl.BlockSpec((1, D_MODEL), lambda i: (0, 0))
    return _grid_call(
        body, "ln1_bwd", seq // TOK,
        in_specs=[_rows(D_MODEL), _rows(D_FF), _rows(D_FF), _full(wgt.shape), _full(wut.shape), _rows(D_MODEL),
                  _rows(LANE), _full((1, D_MODEL))],
        out_specs=[_rows(D_MODEL), _rows(D_MODEL), vec, vec],
        out_shape=[_sds((seq, D_MODEL), F32), _sds((seq, D_MODEL), BF16), _sds((1, D_MODEL), F32),
                   _sds((1, D_MODEL), F32)],
        operands=(dr2, da, du, wgt, wut, xhat, rstd, ln_g), tasks=tasks)


def _mix_bwd(drb, ya, yp, g, mixed, wout, wba, wbp, w_pool, pool_scale, tasks=()):
    seq = drb.shape[0]

    def body(dr_ref, ya_ref, yp_ref, g_ref, mixed_ref, wout_ref, wba_ref, wbp_ref, wp_ref, ps_ref,
             dzg_ref, dya_ref, dyp_ref, do_ref, dmixed_ref, dpooled_ref, dbg_ref, dps_ref):
        @pl.when(pl.program_id(0) == 0)
        def _():
            dbg_ref[...] = jnp.zeros_like(dbg_ref)
            dps_ref[...] = jnp.zeros_like(dps_ref)

        dmerged = _dot_nt(dr_ref[...], wout_ref[...])
        ga, gp = g_ref[:, :D_MODEL], g_ref[:, D_MODEL:]
        dzga = dmerged * ya_ref[...] * ga * (1.0 - ga)
        dzgp = dmerged * yp_ref[...] * gp * (1.0 - gp)
        dzg_ref[:, :D_MODEL] = dzga.astype(BF16)
        dzg_ref[:, D_MODEL:] = dzgp.astype(BF16)
        dbg_ref[:, :D_MODEL] += jnp.sum(dzga, axis=0, keepdims=True)
        dbg_ref[:, D_MODEL:] += jnp.sum(dzgp, axis=0, keepdims=True)
        dya = (dmerged * ga).astype(BF16)
        dyp = (dmerged * gp).astype(BF16)
        dya_ref[...] = dya
        dyp_ref[...] = dyp
        do_ref[...] = _dot_nt(dya, wba_ref[...])
        dypre = _dot_nt(dyp, wbp_ref[...])
        dps_ref[...] += jnp.sum(dypre * mixed_ref[...], axis=0, keepdims=True)
        dmixed = (dypre * ps_ref[...]).astype(BF16)
        dmixed_ref[...] = dmixed
        for grp in range(len(POOL_WINDOWS)):
            cols = slice(POOL_GROUP * grp, POOL_GROUP * (grp + 1))
            dpooled_ref[:, cols] = _dot_nt(dmixed[:, cols], wp_ref[grp].astype(BF16))

    return _grid_call(
        body, "mix_bwd", seq // TOK,
        in_specs=[_rows(D_MODEL), _rows(D_MODEL), _rows(D_MODEL), _rows(2 * D_MODEL), _rows(D_POOL),
                  _full(wout.shape), _full(wba.shape), _full(wbp.shape), _full(w_pool.shape), _full((1, D_POOL))],
        out_specs=[_rows(2 * D_MODEL), _rows(D_MODEL), _rows(D_MODEL), _rows(D_ATTN), _rows(D_POOL), _rows(D_POOL),
                   pl.BlockSpec((1, 2 * D_MODEL), lambda i: (0, 0)), pl.BlockSpec((1, D_POOL), lambda i: (0, 0))],
        out_shape=[_sds((seq, 2 * D_MODEL), BF16), _sds((seq, D_MODEL), BF16), _sds((seq, D_MODEL), BF16),
                   _sds((seq, D_ATTN), F32), _sds((seq, D_POOL), BF16), _sds((seq, D_POOL), F32),
                   _sds((1, 2 * D_MODEL), F32), _sds((1, D_POOL), F32)],
        operands=(drb, ya, yp, g, mixed, wout, wba, wbp, w_pool, pool_scale), tasks=tasks)


def _attn_bwd(q, k, v, bias, o, lse, do, cos, sin, tasks=()):
    seq = q.shape[0]
    nb = seq // MOBA_BLOCK
    pair = pl.BlockSpec((seq, LANE), lambda p: (0, p))
    table = pl.BlockSpec((seq, LANE), lambda p: (0, 0))
    scale = HEAD_DIM ** -0.5

    def body(q_ref, k_ref, v_ref, bias_ref, o_ref, lse_ref, do_ref, cos_ref, sin_ref, dq_ref, dk_ref, dv_ref,
             dq_acc, dk_acc, dv_acc, dk_head, dv_head, ka_sc, qa_sc, s_sc, dp_sc, p_sc, ds_sc):
        for hh in range(LANE // HEAD_DIM):
            ls = slice(HEAD_DIM * hh, HEAD_DIM * (hh + 1))
            _store_keys(ka_sc, k_ref, ls)
            vb = v_ref[:, ls]
            dk_head[...] = jnp.zeros_like(dk_head)
            dv_head[...] = jnp.zeros_like(dv_head)
            for i in range(nb):
                rs = slice(MOBA_BLOCK * i, MOBA_BLOCK * (i + 1))
                width = MOBA_BLOCK * (i + 1)
                qa_sc[:, 0:HEAD_DIM] = (q_ref[rs, ls] * scale).astype(BF16)
                qa_sc[:, HEAD_DIM:] = bias_ref[rs, ls]
                s_sc[0:width, :] = _dot_nt(ka_sc[0:width, :], qa_sc[...])
                s_sc[rs, :] = jnp.where(_causal((MOBA_BLOCK, MOBA_BLOCK), transposed=True), s_sc[rs, :], NEG)
                dob = do_ref[rs, ls]
                delta = _row_vector(jnp.sum(dob * o_ref[rs, ls], axis=1, keepdims=True))
                lse_row = lse_ref[0, hh:hh + 1, rs]
                dob16 = dob.astype(BF16)
                dp_sc[0:width, :] = _dot_nt(vb[0:width], dob16)
                for c in range(width // SCORE_CHUNK):
                    rows = slice(SCORE_CHUNK * c, SCORE_CHUNK * (c + 1))
                    p = jnp.exp(s_sc[rows, :] - lse_row)
                    p_sc[rows, :] = p.astype(BF16)
                    ds_sc[rows, :] = (p * (dp_sc[rows, :] - delta)).astype(BF16)
                dv_head[0:width, :] += _dot(p_sc[0:width, :], dob16)
                dk_head[0:width, :] += _dot(ds_sc[0:width, :], qa_sc[:, 0:HEAD_DIM])
                dq_acc[rs, ls] = _dot_tn(ds_sc[0:width, :], ka_sc[0:width, 0:HEAD_DIM]) * scale
            dk_acc[:, ls] = dk_head[...]
            dv_acc[:, ls] = dv_head[...]
        cos_t, sin_t = cos_ref[...], sin_ref[...]
        dq_ref[...] = _rope_transposed(dq_acc[...], cos_t, sin_t).astype(BF16)
        dk_ref[...] = _rope_transposed(dk_acc[...], cos_t, sin_t).astype(BF16)
        dv_ref[...] = dv_acc[...].astype(BF16)

    return _grid_call(
        body, "attn_bwd", D_ATTN // LANE,
        in_specs=[pair, pair, pair, pair, pair, pl.BlockSpec((1, 8, seq), lambda p: (p, 0, 0)), pair, table, table],
        out_specs=[pair, pair, pair], out_shape=[_sds((seq, D_ATTN), BF16)] * 3,
        operands=(q, k, v, bias, o, lse, do, cos, sin),
        scratch=[pltpu.VMEM((seq, LANE), F32)] * 3 + [pltpu.VMEM((seq, HEAD_DIM), F32)] * 2
        + [pltpu.VMEM((seq, LANE), BF16), pltpu.VMEM((MOBA_BLOCK, LANE), BF16)]
        + [pltpu.VMEM((seq, MOBA_BLOCK), F32)] * 2 + [pltpu.VMEM((seq, MOBA_BLOCK), BF16)] * 2,
        tasks=tasks)


def _in_bwd(dq, dk, dv, dpooled, dzg, dr1, win, tasks=()):
    seq = dr1.shape[0]
    nt = seq // TOK

    def body(dq_ref, dk_ref, dv_ref, dp_ref, dpnext_ref, dzg_ref, dr_ref, win_ref, dx_ref, dz_ref, ext):
        i = pl.program_id(0)
        dp = dp_ref[...]
        dpn = jnp.where(i < nt - 1, dpnext_ref[...], 0.0)
        for grp, window in enumerate(POOL_WINDOWS):
            cols = slice(POOL_GROUP * grp, POOL_GROUP * (grp + 1))
            ext[0:TOK, cols] = dp[:, cols] / _pool_count(i * TOK, TOK, window)
            ext[TOK:, cols] = dpn[:, cols] / _pool_count((i + 1) * TOK, POOL_HALO, window)
        for grp, window in enumerate(POOL_WINDOWS):
            cols = slice(POOL_GROUP * grp, POOL_GROUP * (grp + 1))
            acc = ext[0:TOK, cols] - dp[:, cols]
            for kk in range(1, window):
                acc = acc + ext[pl.ds(kk, TOK), cols]
            dz_ref[:, 3 * D_ATTN + POOL_GROUP * grp:3 * D_ATTN + POOL_GROUP * (grp + 1)] = acc.astype(BF16)
        dz_ref[:, 0:D_ATTN] = dq_ref[...]
        dz_ref[:, D_ATTN:2 * D_ATTN] = dk_ref[...]
        dz_ref[:, 2 * D_ATTN:3 * D_ATTN] = dv_ref[...]
        dz_ref[:, 3 * D_ATTN + D_POOL:] = dzg_ref[...]
        dx = ALPHA * dr_ref[...]
        for n in range(N_DEV):
            dx = dx + _dot_nt(dz_ref[:, D_ATTN * n:D_ATTN * (n + 1)], win_ref[n])
        dx_ref[...] = dx

    halo = pl.BlockSpec((POOL_HALO, D_POOL),
                        lambda i: (jnp.minimum((i + 1) * (TOK // POOL_HALO), seq // POOL_HALO - 1), 0))
    return _grid_call(
        body, "in_bwd", nt,
        in_specs=[_rows(D_ATTN), _rows(D_ATTN), _rows(D_ATTN), _rows(D_POOL), halo, _rows(2 * D_MODEL),
                  _rows(D_MODEL), _full(win.shape)],
        out_specs=[_rows(D_MODEL), _rows(D_IN_PROJ)],
        out_shape=[_sds((seq, D_MODEL), F32), _sds((seq, D_IN_PROJ), BF16)],
        operands=(dq, dk, dv, dpooled, dpooled, dzg, dr1, win),
        scratch=[pltpu.VMEM((TOK + POOL_HALO, D_POOL), F32)], tasks=tasks)


def _tn_matmul(name, a, b, out_shape, out_dtype, steps, a_spec, b_spec, o_spec, tasks=()):
    def body(a_ref, b_ref, o_ref):
        r = _dot_tn(a_ref[...].astype(BF16), b_ref[...].astype(BF16))
        o_ref[...] = r.reshape(o_ref.shape).astype(o_ref.dtype)

    (out,), results = _grid_call(body, name, steps, in_specs=[a_spec, b_spec], out_specs=[o_spec],
                                 out_shape=[_sds(out_shape, out_dtype)], operands=(a, b), tasks=tasks)
    return out, results


def _place():
    return lax.axis_index("x"), lax.axis_index("y"), lax.axis_index("c")


def _other_chips(x, y):
    return [(1 - x, y), (x, 1 - y), (1 - x, 1 - y)]


DMA_SEMS = pltpu.SemaphoreType.DMA


class _AllGather:
    def __init__(self, shards):
        self.operands = list(shards)
        self.n = len(shards)
        self.out_shape = [_sds((N_DEV, *s.shape), s.dtype) for s in shards]
        self.sems = [DMA_SEMS((7 * self.n,)), DMA_SEMS((7 * self.n,)), DMA_SEMS((self.n,))]

    def _copy(self, refs, a, k, block, to, from_input=False):
        ins, outs, (send_sems, recv_sems, _) = refs
        px, py, pc = block
        dst = outs[a].at[4 * px + 2 * py + pc]
        return pltpu.make_async_remote_copy(
            src_ref=ins[a] if from_input else dst, dst_ref=dst,
            send_sem=send_sems.at[7 * a + k], recv_sem=recv_sems.at[7 * a + k],
            device_id=to, device_id_type=MESH)

    def _local(self, refs, a):
        ins, outs, (_, _, local_sems) = refs
        x, y, c = _place()
        return pltpu.make_async_copy(ins[a], outs[a].at[4 * x + 2 * y + c], local_sems.at[a])

    def start(self, refs):
        x, y, c = _place()
        for a in range(self.n):
            self._local(refs, a).start()
        for a in range(self.n):
            self._copy(refs, a, 0, (x, y, c), (x, y, 1 - c), True).start()
            for j, chip in enumerate(_other_chips(x, y)):
                self._copy(refs, a, 1 + j, (x, y, c), (*chip, c), True).start()

    def middle(self, refs):
        x, y, c = _place()
        for j, chip in enumerate(_other_chips(x, y)):
            for a in range(self.n):
                self._copy(refs, a, 1 + j, (*chip, c), (x, y, c)).wait_recv()
                self._copy(refs, a, 4 + j, (*chip, c), (x, y, 1 - c)).start()

    def finish(self, refs):
        x, y, c = _place()
        me, sibling = (x, y, c), (x, y, 1 - c)
        chips = _other_chips(x, y)
        for a in range(self.n):
            self._copy(refs, a, 0, sibling, me).wait_recv()
            for j, chip in enumerate(chips):
                self._copy(refs, a, 4 + j, (*chip, 1 - c), me).wait_recv()
        for a in range(self.n):
            self._copy(refs, a, 0, me, sibling, True).wait_send()
            for j, chip in enumerate(chips):
                self._copy(refs, a, 1 + j, me, (*chip, c), True).wait_send()
                self._copy(refs, a, 4 + j, (*chip, c), sibling).wait_send()
            self._local(refs, a).wait()


class _SiblingSend:
    def __init__(self, partials):
        self.operands = list(partials)
        self.n = len(partials)
        self.out_shape = [_sds((4, *p.shape[1:]), p.dtype) for p in partials]
        self.sems = [DMA_SEMS((4 * self.n,)), DMA_SEMS((4 * self.n,))]

    def _copy(self, refs, a, q):
        ins, outs, (send_sems, recv_sems) = refs
        x, y, c = _place()
        return pltpu.make_async_remote_copy(
            src_ref=ins[a].at[2 * q + 1 - c], dst_ref=outs[a].at[q],
            send_sem=send_sems.at[4 * a + q], recv_sem=recv_sems.at[4 * a + q],
            device_id=(x, y, 1 - c), device_id_type=MESH)

    def start(self, refs):
        for a in range(self.n):
            for q in range(4):
                self._copy(refs, a, q).start()

    def middle(self, refs):
        pass

    def finish(self, refs):
        for a in range(self.n):
            for q in range(4):
                self._copy(refs, a, q).wait()


class _ChipScatter:
    def __init__(self, chip_partials):
        self.operands = list(chip_partials)
        self.n = len(chip_partials)
        self.out_shape = [_sds(p.shape, p.dtype) for p in chip_partials]
        self.sems = [DMA_SEMS((3 * self.n,)), DMA_SEMS((3 * self.n,)), DMA_SEMS((self.n,))]

    def _copy(self, refs, a, k, arrival=False):
        ins, outs, (send_sems, recv_sems, _) = refs
        x, y, c = _place()
        px, py = _other_chips(x, y)[k]
        mine, theirs = 2 * x + y, 2 * px + py
        return pltpu.make_async_remote_copy(
            src_ref=ins[a].at[mine if arrival else theirs], dst_ref=outs[a].at[theirs if arrival else mine],
            send_sem=send_sems.at[3 * a + k], recv_sem=recv_sems.at[3 * a + k],
            device_id=(px, py, c), device_id_type=MESH)

    def _local(self, refs, a):
        ins, outs, (_, _, local_sems) = refs
        x, y, _ = _place()
        return pltpu.make_async_copy(ins[a].at[2 * x + y], outs[a].at[2 * x + y], local_sems.at[a])

    def start(self, refs):
        for a in range(self.n):
            self._local(refs, a).start()
            for k in range(3):
                self._copy(refs, a, k).start()

    def middle(self, refs):
        pass

    def finish(self, refs):
        for a in range(self.n):
            for k in range(3):
                self._copy(refs, a, k, arrival=True).wait_recv()
        for a in range(self.n):
            for k in range(3):
                self._copy(refs, a, k).wait_send()
            self._local(refs, a).wait()


class _DirectScatter:
    def __init__(self, partials):
        self.operands = list(partials)
        self.n = len(partials)
        self.out_shape = [_sds(p.shape, p.dtype) for p in partials]
        self.sems = [DMA_SEMS((7 * self.n,)), DMA_SEMS((7 * self.n,)), DMA_SEMS((self.n,))]

    def _copy(self, refs, a, k, arrival=False):
        ins, outs, (send_sems, recv_sems, _) = refs
        x, y, c = _place()
        peer = [(x, y, 1 - c), (1 - x, y, c), (x, 1 - y, c), (1 - x, 1 - y, c),
                (1 - x, y, 1 - c), (x, 1 - y, 1 - c), (1 - x, 1 - y, 1 - c)][k]
        mine, theirs = 4 * x + 2 * y + c, 4 * peer[0] + 2 * peer[1] + peer[2]
        return pltpu.make_async_remote_copy(
            src_ref=ins[a].at[mine if arrival else theirs], dst_ref=outs[a].at[theirs if arrival else mine],
            send_sem=send_sems.at[7 * a + k], recv_sem=recv_sems.at[7 * a + k],
            device_id=peer, device_id_type=MESH)

    def _local(self, refs, a):
        ins, outs, (_, _, local_sems) = refs
        x, y, c = _place()
        return pltpu.make_async_copy(ins[a].at[4 * x + 2 * y + c], outs[a].at[4 * x + 2 * y + c], local_sems.at[a])

    def start(self, refs):
        for a in range(self.n):
            self._local(refs, a).start()
            for k in range(7):
                self._copy(refs, a, k).start()

    def middle(self, refs):
        pass

    def finish(self, refs):
        for a in range(self.n):
            for k in range(7):
                self._copy(refs, a, k, arrival=True).wait_recv()
        for a in range(self.n):
            for k in range(7):
                self._copy(refs, a, k).wait_send()
            self._local(refs, a).wait()


def _task_args(tasks):
    hbm = pl.BlockSpec(memory_space=pl.ANY)
    operands = [o for t in tasks for o in t.operands]
    out_shape = [s for t in tasks for s in t.out_shape]
    sems = [s for t in tasks for s in t.sems]
    return operands, [hbm] * len(operands), out_shape, [hbm] * len(out_shape), sems


def _task_refs(tasks, ins, outs, sems):
    per_task = []
    for t in tasks:
        ni, no, ns = len(t.operands), len(t.out_shape), len(t.sems)
        per_task.append((ins[:ni], outs[:no], sems[:ns]))
        ins, outs, sems = ins[ni:], outs[no:], sems[ns:]
    return per_task


def _task_results(tasks, outs):
    res = []
    for t in tasks:
        res.append(list(outs[:len(t.out_shape)]))
        outs = outs[len(t.out_shape):]
    return res


def _carry(body, tasks, n_in, n_out, n_scratch, steps):
    if not tasks:
        return body
    t_in = sum(len(t.operands) for t in tasks)
    t_out = sum(len(t.out_shape) for t in tasks)

    def wrapped(*refs):
        ins, refs = refs[:n_in], refs[n_in:]
        t_ins, refs = refs[:t_in], refs[t_in:]
        outs, refs = refs[:n_out], refs[n_out:]
        t_outs, refs = refs[:t_out], refs[t_out:]
        scratch, t_sems = refs[:n_scratch], refs[n_scratch:]
        per_task = _task_refs(tasks, t_ins, t_outs, t_sems)
        step = pl.program_id(0)

        @pl.when(step == 0)
        def _():
            for t, r in zip(tasks, per_task):
                t.start(r)

        @pl.when(step == steps - 1)
        def _():
            for t, r in zip(tasks, per_task):
                t.middle(r)

        body(*ins, *outs, *scratch)

        @pl.when(step == steps - 1)
        def _():
            for t, r in zip(tasks, per_task):
                t.finish(r)

    return wrapped


def _exchange(name, tasks):
    operands, in_specs, out_shape, out_specs, sems = _task_args(tasks)

    def body(*refs):
        ni, no = len(operands), len(out_shape)
        per_task = _task_refs(tasks, refs[:ni], refs[ni:ni + no], refs[ni + no:])
        for phase in ("start", "middle", "finish"):
            for t, r in zip(tasks, per_task):
                getattr(t, phase)(r)

    outs = pl.pallas_call(body, name=name, in_specs=in_specs, out_specs=out_specs, out_shape=out_shape,
                          scratch_shapes=sems)(*operands)
    return _task_results(tasks, outs)


def _row_tile(rows, cols, whole_up_to=256 * 1024):
    if rows * cols <= whole_up_to:
        return rows
    for t in (256, 176, 128, 64, 32, 16, 8):
        if rows % t == 0:
            return t
    return rows


def _pair_sum(name, partials, from_sibling):
    _, rows, cols = partials.shape
    tile = _row_tile(rows, cols, 512 * 1024)

    def body(p_ref, s_ref, o_ref):
        mine = jnp.where(lax.axis_index("c") == 0, p_ref[0, 0].astype(F32), p_ref[0, 1].astype(F32))
        o_ref[0] = (mine + s_ref[0].astype(F32)).astype(o_ref.dtype)

    blk = pl.BlockSpec((1, tile, cols), lambda q, i: (q, i, 0))
    return pl.pallas_call(
        body, name=name, grid=(4, rows // tile),
        in_specs=[pl.BlockSpec((1, 2, tile, cols), lambda q, i: (q, 0, i, 0)), blk],
        out_specs=blk, out_shape=_sds(from_sibling.shape, from_sibling.dtype),
        compiler_params=_params("parallel", "parallel"),
    )(partials.reshape(4, 2, rows, cols), from_sibling)


def _sum_leading(name, stacked):
    parts, rows, cols = stacked.shape
    tile = _row_tile(rows, cols, (512 if parts <= 4 else 256) * 1024)

    def body(s_ref, o_ref):
        acc = s_ref[0].astype(F32)
        for d in range(1, parts):
            acc = acc + s_ref[d].astype(F32)
        o_ref[...] = acc

    return pl.pallas_call(
        body, name=name, grid=(rows // tile,),
        in_specs=[pl.BlockSpec((parts, tile, cols), lambda i: (0, i, 0))],
        out_specs=pl.BlockSpec((tile, cols), lambda i: (i, 0)),
        out_shape=_sds((rows, cols), F32),
        compiler_params=_params("parallel"),
    )(stacked)


def _adamw_math(w, g, m, v):
    nm = ADAM_B1 * m + (1.0 - ADAM_B1) * g
    nv = ADAM_B2 * v + (1.0 - ADAM_B2) * (g * g)
    m_hat = nm / (1.0 - ADAM_B1 ** ADAM_STEP)
    v_hat = nv / (1.0 - ADAM_B2 ** ADAM_STEP)
    return -ADAM_LR * (m_hat / (jnp.sqrt(v_hat) + ADAM_EPS) + ADAM_WD * w), nm, nv


def _adamw(name, w, g, m, v):
    rows, cols = w.shape
    tile = _row_tile(rows, cols)

    def body(w_ref, g_ref, m_ref, v_ref, d_ref, nm_ref, nv_ref):
        d_ref[...], nm_ref[...], nv_ref[...] = _adamw_math(w_ref[...], g_ref[...], m_ref[...], v_ref[...])

    blk = pl.BlockSpec((tile, cols), lambda i: (i, 0))
    return pl.pallas_call(
        body, name=name, grid=(rows // tile,),
        in_specs=[blk] * 4, out_specs=[blk] * 3,
        out_shape=[_sds((rows, cols), F32)] * 3,
        compiler_params=_params("parallel"),
    )(w, g, m, v)


def _sum_adamw(name, stacked, w, m, v):
    parts, rows, cols = stacked.shape
    tile = _row_tile(rows, cols)

    def body(s_ref, w_ref, m_ref, v_ref, g_ref, d_ref, nm_ref, nv_ref):
        g = s_ref[0].astype(F32)
        for d in range(1, parts):
            g = g + s_ref[d].astype(F32)
        g_ref[...] = g
        d_ref[...], nm_ref[...], nv_ref[...] = _adamw_math(w_ref[...], g, m_ref[...], v_ref[...])

    blk = pl.BlockSpec((tile, cols), lambda i: (i, 0))
    return pl.pallas_call(
        body, name=name, grid=(rows // tile,),
        in_specs=[pl.BlockSpec((parts, tile, cols), lambda i: (0, i, 0))] + [blk] * 3, out_specs=[blk] * 4,
        out_shape=[_sds((rows, cols), F32)] * 4,
        compiler_params=_params("parallel"),
    )(stacked, w, m, v)


SMALL = ("b_gate", "w_pool", "pool_scale", "ln1_g", "ln1_b", "conv_b", "ln2_g", "ln2_b")
TILE = 8 * LANE


def _pack(parts):
    tiles = []
    for p in parts:
        flat = p.reshape(-1)
        tiles.append(jnp.pad(flat, (0, -flat.size % TILE)).reshape(-1, LANE))
    return jnp.concatenate(tiles, axis=0)


def _unpack(packed, shapes):
    out, at = [], 0
    for shape in shapes:
        size = math.prod(shape)
        rows = -(-size // TILE) * 8
        out.append(packed[at:at + rows].reshape(-1)[:size].reshape(shape))
        at += rows
    return out


MIXER = ("w_branch_attn", "w_branch_pool", "w_out", "conv_w")
FFN = ("w_ffn_gate_t", "w_ffn_up_t", "w_ffn_down")


def _columns(t):
    return jnp.transpose(t, (1, 0, 2)).reshape(t.shape[1], N_DEV * t.shape[2])


def _row_blocks(t):
    return t.reshape(N_DEV * t.shape[1], t.shape[2])


def _by_owner(t):
    return t.reshape(N_DEV, t.shape[0] // N_DEV, t.shape[1])


def _reduce_halves(names, partials, from_sibling):
    return [_pair_sum("pair_sum_" + n, p, s) for n, p, s in zip(names, partials, from_sibling)]


def _local_step(x, target, shards, small):
    seq = x.shape[0]
    cos, sin = _rope_tables(seq)
    whole = lambda width: pl.BlockSpec((seq, width), lambda *_: (0, 0))
    ((w_in_all,),) = _exchange("gather_w_in", [_AllGather([shards["w_in"]])])
    (xb, q, k, v, u, g, kmean), (mixer,) = _proj_in(
        x, w_in_all, small["b_gate"], cos, sin, tasks=[_AllGather([shards[n] for n in MIXER])])
    wba, wbp, wout, conv_w = _columns(mixer[0]), _columns(mixer[1]), _row_blocks(mixer[2]), _columns(mixer[3])
    (o, lse, bias), ((wgt, wut),) = _attn_fwd(
        q, k, v, kmean.reshape(seq // MOBA_BLOCK, D_ATTN),
        tasks=[_AllGather([shards["w_ffn_gate_t"], shards["w_ffn_up_t"]])])
    (ya, yp, pooled, mixed, ypre, merged, xhat1, rstd1, h1, h1b), _ = _mix(
        o, u, g, x, wba, wbp, wout, small["w_pool"], small["pool_scale"], small["ln1_g"], small["ln1_b"])
    wgt, wut = _row_blocks(wgt), _row_blocks(wut)
    (a, uf, cdf, act), ((wd,),) = _ffn_up(
        h1b, wgt, wut, conv_w, small["conv_b"], tasks=[_AllGather([shards["w_ffn_down"]])])
    wd = _row_blocks(wd)
    dr2, dr2b, loss, dg2, db2 = _ffn_down(act, wd, h1, target, small["ln2_g"], small["ln2_b"])

    da, du, dwd, dwg, dwu, dconv = _ffn_bwd(dr2b, h1b, a, uf, cdf, act, wd, conv_w, small["conv_b"])
    ffn_partials = [_by_owner(dwg), _by_owner(dwu), _by_owner(dwd)]
    (dr1, dr1b, dg1, db1), (ffn_sibling,) = _ln1_bwd(
        dr2, da, du, wgt, wut, xhat1, rstd1, small["ln1_g"], tasks=[_SiblingSend(ffn_partials)])
    ffn_chip = _reduce_halves(FFN, ffn_partials, ffn_sibling)
    (dzg, dya, dyp, do, dmixed, dpooled, dbg, dps), (gate_landed,) = _mix_bwd(
        dr1b, ya, yp, g, mixed, wout, wba, wbp, small["w_pool"], small["pool_scale"],
        tasks=[_ChipScatter(ffn_chip[0:1])])
    dw_out, _ = _tn_matmul(
        "dw_out", merged, dr1b, (D_MODEL, D_MODEL), BF16, 4,
        pl.BlockSpec((seq, 256), lambda m: (0, m)), whole(D_MODEL), pl.BlockSpec((256, D_MODEL), lambda m: (m, 0)))
    dw_ba, _ = _tn_matmul(
        "dw_branch_attn", o, dya, (N_DEV, D_ATTN, LANE), BF16, N_DEV,
        whole(D_ATTN), pl.BlockSpec((seq, LANE), lambda n: (0, n)), pl.BlockSpec((1, D_ATTN, LANE), lambda n: (n, 0, 0)))
    dw_bp, _ = _tn_matmul(
        "dw_branch_pool", ypre, dyp, (N_DEV, D_POOL, LANE), BF16, N_DEV,
        whole(D_POOL), pl.BlockSpec((seq, LANE), lambda n: (0, n)), pl.BlockSpec((1, D_POOL, LANE), lambda n: (n, 0, 0)))
    dw_pool, _ = _tn_matmul(
        "dw_pool", pooled, dmixed, (len(POOL_WINDOWS), POOL_GROUP, POOL_GROUP), F32, len(POOL_WINDOWS),
        pl.BlockSpec((seq, POOL_GROUP), lambda n: (0, n)), pl.BlockSpec((seq, POOL_GROUP), lambda n: (0, n)),
        pl.BlockSpec((1, POOL_GROUP, POOL_GROUP), lambda n: (n, 0, 0)))
    mixer_partials = [dw_ba, dw_bp, _by_owner(dw_out)]
    (dq, dk, dv), (up_down_landed, mixer_sibling) = _attn_bwd(
        q, k, v, bias, o, lse, do, cos, sin, tasks=[_ChipScatter(ffn_chip[1:3]), _SiblingSend(mixer_partials)])
    mixer_chip = _reduce_halves(MIXER[:3], mixer_partials, mixer_sibling)
    (grad_x, dz), _ = _in_bwd(dq, dk, dv, dpooled, dzg, dr1, w_in_all)
    dw_in, (mixer_landed,) = _tn_matmul(
        "dw_in", xb, dz, (N_DEV, D_MODEL, D_ATTN), BF16, 2 * N_DEV,
        pl.BlockSpec((seq, 512), lambda s: (0, s % 2)), pl.BlockSpec((seq, D_ATTN), lambda s: (0, s // 2)),
        pl.BlockSpec((1, 512, D_ATTN), lambda s: (s // 2, s % 2, 0)), tasks=[_ChipScatter(mixer_chip)])

    landed = dict(zip(FFN + MIXER[:3], gate_landed + up_down_landed + mixer_landed))
    little = {"b_gate": dbg, "w_pool": dw_pool, "pool_scale": dps, "ln1_g": dg1, "ln1_b": db1, "conv_b": dconv[3:4],
              "ln2_g": dg2, "ln2_b": db2, "conv_w": dconv[0:3], "loss": loss}
    return grad_x, landed, dw_in, little


def kernel(x, w_in, b_gate, w_branch_attn, w_pool, pool_scale, w_branch_pool, w_out, ln1_g, ln1_b, w_ffn_gate, w_ffn_up, conv_w, conv_b, w_ffn_down, ln2_g, ln2_b, loss_target, m_w_in, m_b_gate, m_w_branch_attn, m_w_pool, m_pool_scale, m_w_branch_pool, m_w_out, m_ln1_g, m_ln1_b, m_w_ffn_gate, m_w_ffn_up, m_conv_w, m_conv_b, m_w_ffn_down, m_ln2_g, m_ln2_b, v_w_in, v_b_gate, v_w_branch_attn, v_w_pool, v_pool_scale, v_w_branch_pool, v_w_out, v_ln1_g, v_ln1_b, v_w_ffn_gate, v_w_ffn_up, v_conv_w, v_conv_b, v_w_ffn_down, v_ln2_g, v_ln2_b):
    me = 4 * lax.axis_index("x") + 2 * lax.axis_index("y") + lax.axis_index("c")
    weights = dict(w_in=w_in, b_gate=b_gate, w_branch_attn=w_branch_attn, w_pool=w_pool, pool_scale=pool_scale,
                   w_branch_pool=w_branch_pool, w_out=w_out, ln1_g=ln1_g, ln1_b=ln1_b, w_ffn_gate=w_ffn_gate,
                   w_ffn_up=w_ffn_up, conv_w=conv_w, conv_b=conv_b, w_ffn_down=w_ffn_down, ln2_g=ln2_g, ln2_b=ln2_b)
    m_in = dict(w_in=m_w_in, b_gate=m_b_gate, w_branch_attn=m_w_branch_attn, w_pool=m_w_pool,
                pool_scale=m_pool_scale, w_branch_pool=m_w_branch_pool, w_out=m_w_out, ln1_g=m_ln1_g, ln1_b=m_ln1_b,
                w_ffn_gate=m_w_ffn_gate, w_ffn_up=m_w_ffn_up, conv_w=m_conv_w, conv_b=m_conv_b,
                w_ffn_down=m_w_ffn_down, ln2_g=m_ln2_g, ln2_b=m_ln2_b)
    v_in = dict(w_in=v_w_in, b_gate=v_b_gate, w_branch_attn=v_w_branch_attn, w_pool=v_w_pool,
                pool_scale=v_pool_scale, w_branch_pool=v_w_branch_pool, w_out=v_w_out, ln1_g=v_ln1_g, ln1_b=v_ln1_b,
                w_ffn_gate=v_w_ffn_gate, w_ffn_up=v_w_ffn_up, conv_w=v_conv_w, conv_b=v_conv_b,
                w_ffn_down=v_w_ffn_down, ln2_g=v_ln2_g, ln2_b=v_ln2_b)
    weights = {n: a[0] for n, a in weights.items()}
    m_in = {n: a[0] for n, a in m_in.items()}
    v_in = {n: a[0] for n, a in v_in.items()}

    shards = {"w_in": weights["w_in"].astype(BF16), "w_branch_attn": weights["w_branch_attn"].astype(BF16),
              "w_branch_pool": weights["w_branch_pool"].astype(BF16), "w_out": weights["w_out"].astype(BF16),
              "w_ffn_gate_t": weights["w_ffn_gate"].T.astype(BF16), "w_ffn_up_t": weights["w_ffn_up"].T.astype(BF16),
              "w_ffn_down": weights["w_ffn_down"].astype(BF16), "conv_w": weights["conv_w"]}
    small = {"b_gate": weights["b_gate"][None], "w_pool": weights["w_pool"], "pool_scale": weights["pool_scale"][None],
             "ln1_g": weights["ln1_g"][None], "ln1_b": weights["ln1_b"][None], "conv_b": weights["conv_b"][None],
             "ln2_g": weights["ln2_g"][None], "ln2_b": weights["ln2_b"][None]}

    grad_x, landed, dw_in, little = _local_step(x[0], loss_target[0], shards, small)

    ((w_in_sibling,),) = _exchange("sibling_grads", [_SiblingSend([dw_in])])
    w_in_chip = _reduce_halves(["w_in"], [dw_in], [w_in_sibling])
    names = SMALL + ("conv_w",)
    (landed["w_in"],), (all_small,) = _exchange(
        "scatter_grads", [_ChipScatter(w_in_chip), _AllGather([_pack([little[n] for n in names + ("loss",)])])])

    grads, delta, new_m, new_v = {}, {}, {}, {}
    for n in ("w_in", "w_branch_attn", "w_branch_pool", "w_out", "w_ffn_down"):
        grads[n], delta[n], new_m[n], new_v[n] = _sum_adamw("update_" + n, landed[n], weights[n], m_in[n], v_in[n])
    for n in ("w_ffn_gate", "w_ffn_up"):
        updated = _sum_adamw("update_" + n, landed[n + "_t"], weights[n].T, m_in[n].T, v_in[n].T)
        grads[n], delta[n], new_m[n], new_v[n] = (t.T for t in updated)
    small_sum = _sum_leading("sum_small", all_small)
    *small_grads, conv_w_grad, loss = _unpack(
        small_sum, [weights[n].shape for n in SMALL] + [(3, D_FF), little["loss"].shape])
    loss = loss[0, 0]
    grads.update(zip(SMALL, small_grads))
    grads["conv_w"] = lax.dynamic_slice(conv_w_grad, (0, me * FF_SHARD), (3, FF_SHARD))
    flat = lambda d: _pack([d[n] for n in names])
    shapes = [weights[n].shape for n in names]
    for out, packed in zip((delta, new_m, new_v),
                           _adamw("adamw_small", flat(weights), flat(grads), flat(m_in), flat(v_in))):
        out.update(zip(names, _unpack(packed, shapes)))

    order = ("w_in", "b_gate", "w_branch_attn", "w_pool", "pool_scale", "w_branch_pool", "w_out", "ln1_g", "ln1_b",
             "w_ffn_gate", "w_ffn_up", "conv_w", "conv_b", "w_ffn_down", "ln2_g", "ln2_b")
    lead = lambda t: t[None]
    return (loss, lead(grad_x), *[lead(grads[n]) for n in order], *[lead(delta[n]) for n in order],
            *[lead(new_m[n]) for n in order], *[lead(new_v[n]) for n in order])
```

```python
import functools
import math

import jax
import jax.numpy as jnp
from jax import lax
from jax.experimental import pallas as pl
from jax.experimental.pallas import tpu as pltpu

F32 = jnp.float32
BF16 = jnp.bfloat16

D_MODEL = 1024
N_HEADS = 8
HEAD_DIM = 64
D_ATTN = N_HEADS * HEAD_DIM
MOBA_BLOCK = 256
MOBA_TOPK = 3
ROPE_THETA = 10000.0
POOL_WINDOWS = (2, 4, 8, 16)
POOL_GROUP = 128
D_POOL = len(POOL_WINDOWS) * POOL_GROUP
POOL_HALO = 16
D_FF = 2816
D_IN_PROJ = 3 * D_ATTN + D_POOL + 2 * D_MODEL
LN_EPS = 1e-5
ALPHA = 2.0 ** 0.25
NEG = -1e30
N_DEV = 8
FF_SHARD = D_FF // N_DEV

ADAM_LR = 0.001
ADAM_B1 = 0.9
ADAM_B2 = 0.999
ADAM_EPS = 1e-08
ADAM_WD = 0.01
ADAM_STEP = 10

TOK = 256
FF_CHUNK = 256
LANE = 128
VMEM_LIMIT = 56 * 1024 * 1024

MESH = pl.DeviceIdType.MESH
NT_DIMS = (((1,), (1,)), ((), ()))
TN_DIMS = (((0,), (0,)), ((), ()))


def _params(*sem):
    return pltpu.CompilerParams(dimension_semantics=sem or None, vmem_limit_bytes=VMEM_LIMIT)


def _full(shape):
    zeros = (0,) * len(shape)
    return pl.BlockSpec(shape, lambda *_: zeros, pipeline_mode=pl.Buffered(1))


def _rows(width, tile=TOK):
    return pl.BlockSpec((tile, width), lambda i: (i, 0))


def _sds(shape, dtype):
    return jax.ShapeDtypeStruct(shape, dtype)


def _dot(a, b):
    return jnp.dot(a, b, preferred_element_type=F32)


def _dot_nt(a, b):
    return lax.dot_general(a, b, NT_DIMS, preferred_element_type=F32)


def _dot_tn(a, b):
    return lax.dot_general(a, b, TN_DIMS, preferred_element_type=F32)


def _rope_tables(seq):
    half = HEAD_DIM // 2
    inv_freq = 1.0 / (ROPE_THETA ** (jnp.arange(half, dtype=F32) / half))
    ang = jnp.arange(seq, dtype=F32)[:, None] * inv_freq[None, :]
    cos, sin = jnp.cos(ang), jnp.sin(ang)
    return jnp.tile(cos, (1, 4)), jnp.tile(jnp.concatenate([-sin, sin], axis=1), (1, 2))


def _swap_halves(t):
    lane = lax.broadcasted_iota(jnp.int32, t.shape, 1)
    return jnp.where((lane % HEAD_DIM) < HEAD_DIM // 2, pltpu.roll(t, LANE - 32, 1), pltpu.roll(t, 32, 1))


def _rope(t, cos, sin):
    return t * cos + _swap_halves(t) * sin


def _rope_transposed(g, cos, sin):
    return g * cos + _swap_halves(g * sin)


def _ln_fwd(r, g, b):
    mu = jnp.mean(r, axis=-1, keepdims=True)
    xc = r - mu
    var = jnp.mean(xc * xc, axis=-1, keepdims=True)
    rstd = lax.rsqrt(var + LN_EPS)
    xhat = xc * rstd
    return xhat * g + b, xhat, rstd


def _ln_bwd(dy, xhat, rstd, g):
    dxh = dy * g
    m1 = jnp.mean(dxh, axis=-1, keepdims=True)
    m2 = jnp.mean(dxh * xhat, axis=-1, keepdims=True)
    return rstd * (dxh - m1 - xhat * m2)


def _normal_cdf(a):
    return 0.5 * (1.0 + lax.erf(a * (1.0 / math.sqrt(2.0))))


def _gelu_derivative(a, cdf):
    return cdf + a * (jnp.exp(-0.5 * a * a) * (1.0 / math.sqrt(2.0 * math.pi)))


def _shift_down(a, k):
    row = lax.broadcasted_iota(jnp.int32, a.shape, 0)
    return jnp.where(row >= k, pltpu.roll(a, k, 0), 0.0)


def _shift_up(a, k):
    n = a.shape[0]
    row = lax.broadcasted_iota(jnp.int32, a.shape, 0)
    return jnp.where(row < n - k, pltpu.roll(a, n - k, 0), 0.0)


def _conv(a, cw, cb):
    return cw[2:3, :] * a + cw[1:2, :] * _shift_down(a, 1) + cw[0:1, :] * _shift_down(a, 2) + cb


def _pool_count(first_row, rows, window):
    t = first_row + lax.broadcasted_iota(jnp.int32, (rows, 1), 0)
    return jnp.minimum(t + 1, window).astype(F32)


def _grid_call(body, name, steps, in_specs, out_specs, out_shape, operands, scratch=(), tasks=()):
    t_operands, t_in_specs, t_out_shape, t_out_specs, t_sems = _task_args(tasks)
    outs = pl.pallas_call(
        _carry(body, tasks, len(in_specs), len(out_specs), len(scratch), steps), name=name, grid=(steps,),
        in_specs=list(in_specs) + t_in_specs, out_specs=list(out_specs) + t_out_specs,
        out_shape=list(out_shape) + t_out_shape, scratch_shapes=list(scratch) + t_sems,
        compiler_params=_params("arbitrary"),
    )(*operands, *t_operands)
    return outs[:len(out_specs)], _task_results(tasks, outs[len(out_specs):])


def _proj_in(x, win, b_gate, cos, sin, tasks=()):
    seq = x.shape[0]
    nt = seq // TOK

    def body(x_ref, win_ref, bg_ref, cos_ref, sin_ref, xb_ref, q_ref, k_ref, v_ref, u_ref, g_ref, km_ref):
        xb = x_ref[...].astype(BF16)
        xb_ref[...] = xb
        cos_t, sin_t = cos_ref[...], sin_ref[...]
        for sec, out_ref in ((0, q_ref), (1, k_ref)):
            z = _dot(xb, win_ref[sec])
            for c in range(D_ATTN // LANE):
                cols = slice(LANE * c, LANE * (c + 1))
                out_ref[:, cols] = _rope(z[:, cols], cos_t, sin_t)
        km_ref[0] = jnp.mean(k_ref[...], axis=0, keepdims=True)
        v_ref[...] = _dot(xb, win_ref[2]).astype(BF16)
        u_ref[...] = _dot(xb, win_ref[3])
        for n in range(4):
            cols = slice(D_ATTN * n, D_ATTN * (n + 1))
            g_ref[:, cols] = jax.nn.sigmoid(_dot(xb, win_ref[4 + n]) + bg_ref[:, cols])

    return _grid_call(
        body, "proj_in", nt,
        in_specs=[_rows(D_MODEL), _full(win.shape), _full((1, 2 * D_MODEL)), _rows(LANE), _rows(LANE)],
        out_specs=[_rows(D_MODEL), _rows(D_ATTN), _rows(D_ATTN), _rows(D_ATTN), _rows(D_POOL), _rows(2 * D_MODEL),
                   pl.BlockSpec((1, 1, D_ATTN), lambda i: (i, 0, 0))],
        out_shape=[_sds((seq, D_MODEL), BF16), _sds((seq, D_ATTN), F32), _sds((seq, D_ATTN), F32),
                   _sds((seq, D_ATTN), BF16), _sds((seq, D_POOL), F32), _sds((seq, 2 * D_MODEL), F32),
                   _sds((nt, 1, D_ATTN), F32)],
        operands=(x, win, b_gate, cos, sin), tasks=tasks)


SCORE_CHUNK = 128


def _store_keys(ka_sc, k_ref, ls):
    seq = ka_sc.shape[0]
    ka_sc[:, 0:HEAD_DIM] = k_ref[:, ls].astype(BF16)
    row = lax.broadcasted_iota(jnp.int32, (seq, HEAD_DIM), 0)
    lane = lax.broadcasted_iota(jnp.int32, (seq, HEAD_DIM), 1)
    in_block = (lane * MOBA_BLOCK <= row) & (row < (lane + 1) * MOBA_BLOCK)
    ka_sc[:, HEAD_DIM:] = jnp.where(in_block, 1.0, 0.0).astype(BF16)


def _block_bias(qf, km, i):
    if i <= MOBA_TOPK:
        return jnp.zeros((MOBA_BLOCK, HEAD_DIM), BF16)
    nb = km.shape[0]
    gate = lax.dot_general(km, qf, NT_DIMS, precision=lax.Precision.HIGHEST, preferred_element_type=F32)
    blk = lax.broadcasted_iota(jnp.int32, gate.shape, 0)
    rank = jnp.zeros(gate.shape, F32)
    for r in range(1, i):
        lower = pltpu.roll(gate, r, 0)
        rank = rank + jnp.where((blk >= r) & (lower >= gate), 1.0, 0.0)
        higher = pltpu.roll(gate, nb - r, 0)
        rank = rank + jnp.where((blk + r < i) & (higher > gate), 1.0, 0.0)
    bias = jnp.where((blk < i) & (rank >= MOBA_TOPK), NEG, 0.0)
    padded = jnp.concatenate([bias, jnp.zeros((LANE - nb, MOBA_BLOCK), F32)], axis=0)
    return jnp.transpose(padded)[:, 0:HEAD_DIM].astype(BF16)


def _causal(shape, transposed=False):
    row = lax.broadcasted_iota(jnp.int32, shape, 0)
    col = lax.broadcasted_iota(jnp.int32, shape, 1)
    return (row <= col) if transposed else (col <= row)


def _row_vector(col):
    return jnp.transpose(jnp.broadcast_to(col, (MOBA_BLOCK, LANE)))[0:1, :]


def _attn_fwd(q, k, v, kmean, tasks=()):
    seq = q.shape[0]
    nb = seq // MOBA_BLOCK
    assert nb == 8, "the block ranking keeps one sublane per key block"
    pair = pl.BlockSpec((seq, LANE), lambda p: (0, p))
    heads = LANE // HEAD_DIM

    def body(q_ref, k_ref, v_ref, km_ref, o_ref, lse_ref, bias_ref, ka_sc, qa_sc, s_sc, p_sc):
        lse_ref[0, heads:, :] = jnp.zeros((8 - heads, seq), F32)
        for hh in range(heads):
            ls = slice(HEAD_DIM * hh, HEAD_DIM * (hh + 1))
            _store_keys(ka_sc, k_ref, ls)
            vb = v_ref[:, ls]
            km = km_ref[:, ls]
            for i in range(nb):
                rs = slice(MOBA_BLOCK * i, MOBA_BLOCK * (i + 1))
                width = MOBA_BLOCK * (i + 1)
                qf = q_ref[rs, ls]
                bias = _block_bias(qf, km, i)
                bias_ref[rs, ls] = bias
                qa_sc[:, 0:HEAD_DIM] = (qf * HEAD_DIM ** -0.5).astype(BF16)
                qa_sc[:, HEAD_DIM:] = bias
                s_sc[:, 0:width] = _dot_nt(qa_sc[...], ka_sc[0:width, :])
                s_sc[:, rs] = jnp.where(_causal((MOBA_BLOCK, MOBA_BLOCK)), s_sc[:, rs], NEG)
                chunks = [slice(SCORE_CHUNK * c, SCORE_CHUNK * (c + 1)) for c in range(width // SCORE_CHUNK)]
                top = s_sc[:, chunks[0]]
                for c in chunks[1:]:
                    top = jnp.maximum(top, s_sc[:, c])
                m = jnp.max(top, axis=1, keepdims=True)
                total = jnp.zeros((MOBA_BLOCK, SCORE_CHUNK), F32)
                for c in chunks:
                    p = jnp.exp(s_sc[:, c] - m)
                    total = total + p
                    p_sc[:, c] = p.astype(BF16)
                l = jnp.sum(total, axis=1, keepdims=True)
                o_ref[rs, ls] = _dot(p_sc[:, 0:width], vb[0:width]) / l
                lse_ref[0, hh:hh + 1, rs] = _row_vector(m + jnp.log(l))

    return _grid_call(
        body, "attn_fwd", D_ATTN // LANE,
        in_specs=[pair, pair, pair, pl.BlockSpec((nb, LANE), lambda p: (0, p))],
        out_specs=[pair, pl.BlockSpec((1, 8, seq), lambda p: (p, 0, 0)), pair],
        out_shape=[_sds((seq, D_ATTN), F32), _sds((D_ATTN // LANE, 8, seq), F32), _sds((seq, D_ATTN), BF16)],
        operands=(q, k, v, kmean),
        scratch=[pltpu.VMEM((seq, LANE), BF16), pltpu.VMEM((MOBA_BLOCK, LANE), BF16),
                 pltpu.VMEM((MOBA_BLOCK, seq), F32), pltpu.VMEM((MOBA_BLOCK, seq), BF16)],
        tasks=tasks)


def _mix(o, u, g, x, wba, wbp, wout, w_pool, pool_scale, ln_g, ln_b, tasks=()):
    seq = x.shape[0]

    def body(o_ref, u_ref, uprev_ref, g_ref, x_ref, wba_ref, wbp_ref, wout_ref, wp_ref, ps_ref, lg_ref, lb_ref,
             ya_ref, yp_ref, pooled_ref, mixed_ref, ypre_ref, merged_ref, xhat_ref, rstd_ref, h_ref, hb_ref, ext):
        i = pl.program_id(0)
        ya = _dot(o_ref[...].astype(BF16), wba_ref[...])
        ucur = u_ref[...]
        ext[0:POOL_HALO, :] = jnp.where(i > 0, uprev_ref[...], 0.0)
        ext[POOL_HALO:, :] = ucur
        for grp, window in enumerate(POOL_WINDOWS):
            cols = slice(POOL_GROUP * grp, POOL_GROUP * (grp + 1))
            acc = ucur[:, cols]
            for kk in range(1, window):
                acc = acc + ext[pl.ds(POOL_HALO - kk, TOK), cols]
            pooled = acc / _pool_count(i * TOK, TOK, window) - ucur[:, cols]
            pooled_ref[:, cols] = pooled.astype(BF16)
            mixed_ref[:, cols] = _dot(pooled.astype(BF16), wp_ref[grp].astype(BF16))
        mixed = mixed_ref[...]
        ypre = (mixed * ps_ref[...]).astype(BF16)
        ypre_ref[...] = ypre
        yp = _dot(ypre, wbp_ref[...])
        ya_ref[...] = ya
        yp_ref[...] = yp
        merged = (g_ref[:, :D_MODEL] * ya + g_ref[:, D_MODEL:] * yp).astype(BF16)
        merged_ref[...] = merged
        r1 = ALPHA * x_ref[...] + _dot(merged, wout_ref[...])
        h, xhat, rstd = _ln_fwd(r1, lg_ref[...], lb_ref[...])
        xhat_ref[...] = xhat
        rstd_ref[...] = jnp.broadcast_to(rstd, (TOK, LANE))
        h_ref[...] = h
        hb_ref[...] = h.astype(BF16)

    halo = pl.BlockSpec((POOL_HALO, D_POOL), lambda i: (jnp.maximum(i * (TOK // POOL_HALO) - 1, 0), 0))
    return _grid_call(
        body, "mix", seq // TOK,
        in_specs=[_rows(D_ATTN), _rows(D_POOL), halo, _rows(2 * D_MODEL), _rows(D_MODEL),
                  _full(wba.shape), _full(wbp.shape), _full(wout.shape), _full(w_pool.shape),
                  _full((1, D_POOL)), _full((1, D_MODEL)), _full((1, D_MODEL))],
        out_specs=[_rows(D_MODEL), _rows(D_MODEL), _rows(D_POOL), _rows(D_POOL), _rows(D_POOL), _rows(D_MODEL),
                   _rows(D_MODEL), _rows(LANE), _rows(D_MODEL), _rows(D_MODEL)],
        out_shape=[_sds((seq, D_MODEL), F32), _sds((seq, D_MODEL), F32), _sds((seq, D_POOL), BF16),
                   _sds((seq, D_POOL), F32), _sds((seq, D_POOL), BF16), _sds((seq, D_MODEL), BF16),
                   _sds((seq, D_MODEL), F32), _sds((seq, LANE), F32), _sds((seq, D_MODEL), F32),
                   _sds((seq, D_MODEL), BF16)],
        operands=(o, u, u, g, x, wba, wbp, wout, w_pool, pool_scale, ln_g, ln_b),
        scratch=[pltpu.VMEM((TOK + POOL_HALO, D_POOL), F32)], tasks=tasks)


def _ffn_up(hb, wgt, wut, conv_w, conv_b, tasks=()):
    seq = hb.shape[0]
    wblk = pl.BlockSpec((FF_CHUNK, D_MODEL), lambda c: (c, 0))
    cblk = lambda rows: pl.BlockSpec((rows, FF_CHUNK), lambda c: (0, c))
    oblk = pl.BlockSpec((seq, FF_CHUNK), lambda c: (0, c))

    def body(h_ref, wg_ref, wu_ref, cw_ref, cb_ref, a_ref, u_ref, cdf_ref, act_ref):
        h = h_ref[...]
        a = _dot_nt(h, wg_ref[...])
        u = _dot_nt(h, wu_ref[...])
        a_ref[...] = a
        u_ref[...] = u
        ac = _conv(a, cw_ref[...], cb_ref[...])
        cdf = _normal_cdf(ac)
        cdf_ref[...] = cdf
        act_ref[...] = (ac * cdf * u).astype(BF16)

    return _grid_call(
        body, "ffn_up", D_FF // FF_CHUNK,
        in_specs=[_full(hb.shape), wblk, wblk, cblk(3), cblk(1)],
        out_specs=[oblk, oblk, oblk, oblk],
        out_shape=[_sds((seq, D_FF), F32), _sds((seq, D_FF), F32), _sds((seq, D_FF), F32), _sds((seq, D_FF), BF16)],
        operands=(hb, wgt, wut, conv_w, conv_b), tasks=tasks)


def _ffn_down(act, wd, h, target, ln_g, ln_b):
    seq = h.shape[0]

    def body(act_ref, wd_ref, h_ref, t_ref, lg_ref, lb_ref, dr_ref, drb_ref, loss_ref, dg_ref, db_ref):
        i = pl.program_id(0)

        @pl.when(i == 0)
        def _():
            loss_ref[...] = jnp.zeros_like(loss_ref)
            dg_ref[...] = jnp.zeros_like(dg_ref)
            db_ref[...] = jnp.zeros_like(db_ref)

        r2 = ALPHA * h_ref[...] + _dot(act_ref[...], wd_ref[...])
        y, xhat, rstd = _ln_fwd(r2, lg_ref[...], lb_ref[...])
        diff = y - t_ref[...]
        loss_ref[...] += jnp.sum(diff * diff) * (0.5 / D_MODEL)
        dy = diff * (1.0 / D_MODEL)
        dg_ref[...] += jnp.sum(dy * xhat, axis=0, keepdims=True)
        db_ref[...] += jnp.sum(dy, axis=0, keepdims=True)
        dr = _ln_bwd(dy, xhat, rstd, lg_ref[...])
        dr_ref[...] = dr
        drb_ref[...] = dr.astype(BF16)

    vec = pl.BlockSpec((1, D_MODEL), lambda i: (0, 0))
    return pl.pallas_call(
        body, name="ffn_down", grid=(seq // TOK,),
        in_specs=[_rows(D_FF), _full(wd.shape), _rows(D_MODEL), _rows(D_MODEL), _full((1, D_MODEL)), _full((1, D_MODEL))],
        out_specs=[_rows(D_MODEL), _rows(D_MODEL), pl.BlockSpec((8, LANE), lambda i: (0, 0)), vec, vec],
        out_shape=[_sds((seq, D_MODEL), F32), _sds((seq, D_MODEL), BF16), _sds((8, LANE), F32),
                   _sds((1, D_MODEL), F32), _sds((1, D_MODEL), F32)],
        compiler_params=_params("arbitrary"),
    )(act, wd, h, target, ln_g, ln_b)


def _ffn_bwd(drb, hb, a, u, cdf, act, wd, conv_w, conv_b):
    seq = hb.shape[0]
    wblk = pl.BlockSpec((FF_CHUNK, D_MODEL), lambda c: (c, 0))
    cblk = lambda rows: pl.BlockSpec((rows, FF_CHUNK), lambda c: (0, c))
    sblk = pl.BlockSpec((seq, FF_CHUNK), lambda c: (0, c))

    def body(dr_ref, h_ref, a_ref, u_ref, cdf_ref, act_ref, wd_ref, cw_ref, cb_ref,
             da_ref, du_ref, dwd_ref, dwg_ref, dwu_ref, dc_ref):
        dr = dr_ref[...]
        h = h_ref[...]
        a = a_ref[...]
        cw = cw_ref[...]
        dact = _dot_nt(dr, wd_ref[...])
        dwd_ref[...] = _dot_tn(act_ref[...], dr).astype(BF16)
        ac = _conv(a, cw, cb_ref[...])
        cdf = cdf_ref[...]
        du = (dact * (ac * cdf)).astype(BF16)
        dac = dact * u_ref[...] * _gelu_derivative(ac, cdf)
        da = (cw[2:3, :] * dac + cw[1:2, :] * _shift_up(dac, 1) + cw[0:1, :] * _shift_up(dac, 2)).astype(BF16)
        da_ref[...] = da
        du_ref[...] = du
        dwg_ref[...] = _dot_tn(da, h).astype(BF16)
        dwu_ref[...] = _dot_tn(du, h).astype(BF16)
        dc_ref[0:1, :] = jnp.sum(dac * _shift_down(a, 2), axis=0, keepdims=True)
        dc_ref[1:2, :] = jnp.sum(dac * _shift_down(a, 1), axis=0, keepdims=True)
        dc_ref[2:3, :] = jnp.sum(dac * a, axis=0, keepdims=True)
        dc_ref[3:4, :] = jnp.sum(dac, axis=0, keepdims=True)
        dc_ref[4:8, :] = jnp.zeros((4, FF_CHUNK), F32)

    return pl.pallas_call(
        body, name="ffn_bwd", grid=(D_FF // FF_CHUNK,),
        in_specs=[_full(drb.shape), _full(hb.shape), sblk, sblk, sblk, sblk, wblk, cblk(3), cblk(1)],
        out_specs=[sblk, sblk, wblk, wblk, wblk, cblk(8)],
        out_shape=[_sds((seq, D_FF), BF16), _sds((seq, D_FF), BF16), _sds((D_FF, D_MODEL), BF16),
                   _sds((D_FF, D_MODEL), BF16), _sds((D_FF, D_MODEL), BF16), _sds((8, D_FF), F32)],
        compiler_params=_params("parallel"),
    )(drb, hb, a, u, cdf, act, wd, conv_w, conv_b)


def _ln1_bwd(dr2, da, du, wgt, wut, xhat, rstd, ln_g, tasks=()):
    seq = dr2.shape[0]

    def body(dr2_ref, da_ref, du_ref, wg_ref, wu_ref, xhat_ref, rstd_ref, lg_ref, dr_ref, drb_ref, dg_ref, db_ref):
        @pl.when(pl.program_id(0) == 0)
        def _():
            dg_ref[...] = jnp.zeros_like(dg_ref)
            db_ref[...] = jnp.zeros_like(db_ref)

        dh = ALPHA * dr2_ref[...] + _dot(da_ref[...], wg_ref[...]) + _dot(du_ref[...], wu_ref[...])
        xhat = xhat_ref[...]
        dg_ref[...] += jnp.sum(dh * xhat, axis=0, keepdims=True)
        db_ref[...] += jnp.sum(dh, axis=0, keepdims=True)
        dr = _ln_bwd(dh, xhat, rstd_ref[:, 0:1], lg_ref[...])
        dr_ref[...] = dr
        drb_ref[...] = dr.astype(BF16)

    vec = pl.BlockSpec((1, D_MODEL), lambda i: (0, 0))
    return _grid_call(
        body, "ln1_bwd", seq // TOK,
        in_specs=[_rows(D_MODEL), _rows(D_FF), _rows(D_FF), _full(wgt.shape), _full(wut.shape), _rows(D_MODEL),
                  _rows(LANE), _full((1, D_MODEL))],
        out_specs=[_rows(D_MODEL), _rows(D_MODEL), vec, vec],
        out_shape=[_sds((seq, D_MODEL), F32), _sds((seq, D_MODEL), BF16), _sds((1, D_MODEL), F32),
                   _sds((1, D_MODEL), F32)],
        operands=(dr2, da, du, wgt, wut, xhat, rstd, ln_g), tasks=tasks)


def _mix_bwd(drb, ya, yp, g, mixed, wout, wba, wbp, w_pool, pool_scale, tasks=()):
    seq = drb.shape[0]

    def body(dr_ref, ya_ref, yp_ref, g_ref, mixed_ref, wout_ref, wba_ref, wbp_ref, wp_ref, ps_ref,
             dzg_ref, dya_ref, dyp_ref, do_ref, dmixed_ref, dpooled_ref, dbg_ref, dps_ref):
        @pl.when(pl.program_id(0) == 0)
        def _():
            dbg_ref[...] = jnp.zeros_like(dbg_ref)
            dps_ref[...] = jnp.zeros_like(dps_ref)

        dmerged = _dot_nt(dr_ref[...], wout_ref[...])
        ga, gp = g_ref[:, :D_MODEL], g_ref[:, D_MODEL:]
        dzga = dmerged * ya_ref[...] * ga * (1.0 - ga)
        dzgp = dmerged * yp_ref[...] * gp * (1.0 - gp)
        dzg_ref[:, :D_MODEL] = dzga.astype(BF16)
        dzg_ref[:, D_MODEL:] = dzgp.astype(BF16)
        dbg_ref[:, :D_MODEL] += jnp.sum(dzga, axis=0, keepdims=True)
        dbg_ref[:, D_MODEL:] += jnp.sum(dzgp, axis=0, keepdims=True)
        dya = (dmerged * ga).astype(BF16)
        dyp = (dmerged * gp).astype(BF16)
        dya_ref[...] = dya
        dyp_ref[...] = dyp
        do_ref[...] = _dot_nt(dya, wba_ref[...])
        dypre = _dot_nt(dyp, wbp_ref[...])
        dps_ref[...] += jnp.sum(dypre * mixed_ref[...], axis=0, keepdims=True)
        dmixed = (dypre * ps_ref[...]).astype(BF16)
        dmixed_ref[...] = dmixed
        for grp in range(len(POOL_WINDOWS)):
            cols = slice(POOL_GROUP * grp, POOL_GROUP * (grp + 1))
            dpooled_ref[:, cols] = _dot_nt(dmixed[:, cols], wp_ref[grp].astype(BF16))

    return _grid_call(
        body, "mix_bwd", seq // TOK,
        in_specs=[_rows(D_MODEL), _rows(D_MODEL), _rows(D_MODEL), _rows(2 * D_MODEL), _rows(D_POOL),
                  _full(wout.shape), _full(wba.shape), _full(wbp.shape), _full(w_pool.shape), _full((1, D_POOL))],
        out_specs=[_rows(2 * D_MODEL), _rows(D_MODEL), _rows(D_MODEL), _rows(D_ATTN), _rows(D_POOL), _rows(D_POOL),
                   pl.BlockSpec((1, 2 * D_MODEL), lambda i: (0, 0)), pl.BlockSpec((1, D_POOL), lambda i: (0, 0))],
        out_shape=[_sds((seq, 2 * D_MODEL), BF16), _sds((seq, D_MODEL), BF16), _sds((seq, D_MODEL), BF16),
                   _sds((seq, D_ATTN), F32), _sds((seq, D_POOL), BF16), _sds((seq, D_POOL), F32),
                   _sds((1, 2 * D_MODEL), F32), _sds((1, D_POOL), F32)],
        operands=(drb, ya, yp, g, mixed, wout, wba, wbp, w_pool, pool_scale), tasks=tasks)


def _attn_bwd(q, k, v, bias, o, lse, do, cos, sin, tasks=()):
    seq = q.shape[0]
    nb = seq // MOBA_BLOCK
    pair = pl.BlockSpec((seq, LANE), lambda p: (0, p))
    table = pl.BlockSpec((seq, LANE), lambda p: (0, 0))
    scale = HEAD_DIM ** -0.5

    def body(q_ref, k_ref, v_ref, bias_ref, o_ref, lse_ref, do_ref, cos_ref, sin_ref, dq_ref, dk_ref, dv_ref,
             dq_acc, dk_acc, dv_acc, dk_head, dv_head, ka_sc, qa_sc, s_sc, dp_sc, p_sc, ds_sc):
        for hh in range(LANE // HEAD_DIM):
            ls = slice(HEAD_DIM * hh, HEAD_DIM * (hh + 1))
            _store_keys(ka_sc, k_ref, ls)
            vb = v_ref[:, ls]
            dk_head[...] = jnp.zeros_like(dk_head)
            dv_head[...] = jnp.zeros_like(dv_head)
            for i in range(nb):
                rs = slice(MOBA_BLOCK * i, MOBA_BLOCK * (i + 1))
                width = MOBA_BLOCK * (i + 1)
                qa_sc[:, 0:HEAD_DIM] = (q_ref[rs, ls] * scale).astype(BF16)
                qa_sc[:, HEAD_DIM:] = bias_ref[rs, ls]
                s_sc[0:width, :] = _dot_nt(ka_sc[0:width, :], qa_sc[...])
                s_sc[rs, :] = jnp.where(_causal((MOBA_BLOCK, MOBA_BLOCK), transposed=True), s_sc[rs, :], NEG)
                dob = do_ref[rs, ls]
                delta = _row_vector(jnp.sum(dob * o_ref[rs, ls], axis=1, keepdims=True))
                lse_row = lse_ref[0, hh:hh + 1, rs]
                dob16 = dob.astype(BF16)
                dp_sc[0:width, :] = _dot_nt(vb[0:width], dob16)
                for c in range(width // SCORE_CHUNK):
                    rows = slice(SCORE_CHUNK * c, SCORE_CHUNK * (c + 1))
                    p = jnp.exp(s_sc[rows, :] - lse_row)
                    p_sc[rows, :] = p.astype(BF16)
                    ds_sc[rows, :] = (p * (dp_sc[rows, :] - delta)).astype(BF16)
                dv_head[0:width, :] += _dot(p_sc[0:width, :], dob16)
                dk_head[0:width, :] += _dot(ds_sc[0:width, :], qa_sc[:, 0:HEAD_DIM])
                dq_acc[rs, ls] = _dot_tn(ds_sc[0:width, :], ka_sc[0:width, 0:HEAD_DIM]) * scale
            dk_acc[:, ls] = dk_head[...]
            dv_acc[:, ls] = dv_head[...]
        cos_t, sin_t = cos_ref[...], sin_ref[...]
        dq_ref[...] = _rope_transposed(dq_acc[...], cos_t, sin_t).astype(BF16)
        dk_ref[...] = _rope_transposed(dk_acc[...], cos_t, sin_t).astype(BF16)
        dv_ref[...] = dv_acc[...].astype(BF16)

    return _grid_call(
        body, "attn_bwd", D_ATTN // LANE,
        in_specs=[pair, pair, pair, pair, pair, pl.BlockSpec((1, 8, seq), lambda p: (p, 0, 0)), pair, table, table],
        out_specs=[pair, pair, pair], out_shape=[_sds((seq, D_ATTN), BF16)] * 3,
        operands=(q, k, v, bias, o, lse, do, cos, sin),
        scratch=[pltpu.VMEM((seq, LANE), F32)] * 3 + [pltpu.VMEM((seq, HEAD_DIM), F32)] * 2
        + [pltpu.VMEM((seq, LANE), BF16), pltpu.VMEM((MOBA_BLOCK, LANE), BF16)]
        + [pltpu.VMEM((seq, MOBA_BLOCK), F32)] * 2 + [pltpu.VMEM((seq, MOBA_BLOCK), BF16)] * 2,
        tasks=tasks)


def _in_bwd(dq, dk, dv, dpooled, dzg, dr1, win, tasks=()):
    seq = dr1.shape[0]
    nt = seq // TOK

    def body(dq_ref, dk_ref, dv_ref, dp_ref, dpnext_ref, dzg_ref, dr_ref, win_ref, dx_ref, dz_ref, ext):
        i = pl.program_id(0)
        dp = dp_ref[...]
        dpn = jnp.where(i < nt - 1, dpnext_ref[...], 0.0)
        for grp, window in enumerate(POOL_WINDOWS):
            cols = slice(POOL_GROUP * grp, POOL_GROUP * (grp + 1))
            ext[0:TOK, cols] = dp[:, cols] / _pool_count(i * TOK, TOK, window)
            ext[TOK:, cols] = dpn[:, cols] / _pool_count((i + 1) * TOK, POOL_HALO, window)
        for grp, window in enumerate(POOL_WINDOWS):
            cols = slice(POOL_GROUP * grp, POOL_GROUP * (grp + 1))
            acc = ext[0:TOK, cols] - dp[:, cols]
            for kk in range(1, window):
                acc = acc + ext[pl.ds(kk, TOK), cols]
            dz_ref[:, 3 * D_ATTN + POOL_GROUP * grp:3 * D_ATTN + POOL_GROUP * (grp + 1)] = acc.astype(BF16)
        dz_ref[:, 0:D_ATTN] = dq_ref[...]
        dz_ref[:, D_ATTN:2 * D_ATTN] = dk_ref[...]
        dz_ref[:, 2 * D_ATTN:3 * D_ATTN] = dv_ref[...]
        dz_ref[:, 3 * D_ATTN + D_POOL:] = dzg_ref[...]
        dx = ALPHA * dr_ref[...]
        for n in range(N_DEV):
            dx = dx + _dot_nt(dz_ref[:, D_ATTN * n:D_ATTN * (n + 1)], win_ref[n])
        dx_ref[...] = dx

    halo = pl.BlockSpec((POOL_HALO, D_POOL),
                        lambda i: (jnp.minimum((i + 1) * (TOK // POOL_HALO), seq // POOL_HALO - 1), 0))
    return _grid_call(
        body, "in_bwd", nt,
        in_specs=[_rows(D_ATTN), _rows(D_ATTN), _rows(D_ATTN), _rows(D_POOL), halo, _rows(2 * D_MODEL),
                  _rows(D_MODEL), _full(win.shape)],
        out_specs=[_rows(D_MODEL), _rows(D_IN_PROJ)],
        out_shape=[_sds((seq, D_MODEL), F32), _sds((seq, D_IN_PROJ), BF16)],
        operands=(dq, dk, dv, dpooled, dpooled, dzg, dr1, win),
        scratch=[pltpu.VMEM((TOK + POOL_HALO, D_POOL), F32)], tasks=tasks)


def _tn_matmul(name, a, b, out_shape, out_dtype, steps, a_spec, b_spec, o_spec, tasks=()):
    def body(a_ref, b_ref, o_ref):
        r = _dot_tn(a_ref[...].astype(BF16), b_ref[...].astype(BF16))
        o_ref[...] = r.reshape(o_ref.shape).astype(o_ref.dtype)

    (out,), results = _grid_call(body, name, steps, in_specs=[a_spec, b_spec], out_specs=[o_spec],
                                 out_shape=[_sds(out_shape, out_dtype)], operands=(a, b), tasks=tasks)
    return out, results


def _place():
    return lax.axis_index("x"), lax.axis_index("y"), lax.axis_index("c")


def _other_chips(x, y):
    return [(1 - x, y), (x, 1 - y), (1 - x, 1 - y)]


DMA_SEMS = pltpu.SemaphoreType.DMA


class _AllGather:
    def __init__(self, shards, lag=0):
        self.operands = list(shards)
        self.n = len(shards)
        self.lag = lag
        self.out_shape = [_sds((N_DEV, *s.shape), s.dtype) for s in shards]
        self.sems = [DMA_SEMS((7 * self.n,)), DMA_SEMS((7 * self.n,)), DMA_SEMS((self.n,))]

    def _copy(self, refs, a, k, block, to, from_input=False):
        ins, outs, (send_sems, recv_sems, _) = refs
        px, py, pc = block
        dst = outs[a].at[4 * px + 2 * py + pc]
        return pltpu.make_async_remote_copy(
            src_ref=ins[a] if from_input else dst, dst_ref=dst,
            send_sem=send_sems.at[7 * a + k], recv_sem=recv_sems.at[7 * a + k],
            device_id=to, device_id_type=MESH)

    def _local(self, refs, a):
        ins, outs, (_, _, local_sems) = refs
        x, y, c = _place()
        return pltpu.make_async_copy(ins[a], outs[a].at[4 * x + 2 * y + c], local_sems.at[a])

    def _pass_on(self, refs, a):
        x, y, c = _place()
        origin = ((x + 1 - c) % 2, (y + c) % 2, c)
        target = ((x + c) % 2, (y + 1 - c) % 2, c)
        return self._copy(refs, a, 3, origin, target)

    def start(self, refs):
        x, y, c = _place()
        for a in range(self.n):
            self._local(refs, a).start()
        for a in range(self.n):
            self._copy(refs, a, 0, (x, y, c), (x, y, 1 - c), True).start()
            for j, chip in enumerate(_other_chips(x, y)[:2]):
                self._copy(refs, a, 1 + j, (x, y, c), (*chip, c), True).start()

    def middle(self, refs):
        x, y, c = _place()
        me, sibling = (x, y, c), (x, y, 1 - c)
        chips = _other_chips(x, y)
        for a in range(self.n):
            for j in range(2):
                self._copy(refs, a, 1 + j, (*chips[j], c), me).wait_recv()
        for a in range(self.n):
            self._pass_on(refs, a).start()
            for j in range(2):
                self._copy(refs, a, 4 + j, (*chips[j], c), sibling).start()
        for a in range(self.n):
            self._copy(refs, a, 3, (*chips[2], c), me).wait_recv()
            self._copy(refs, a, 6, (*chips[2], c), sibling).start()

    def finish(self, refs):
        x, y, c = _place()
        me, sibling = (x, y, c), (x, y, 1 - c)
        chips = _other_chips(x, y)
        for a in range(self.n):
            self._copy(refs, a, 0, sibling, me).wait_recv()
            for j, chip in enumerate(chips):
                self._copy(refs, a, 4 + j, (*chip, 1 - c), me).wait_recv()
        for a in range(self.n):
            self._copy(refs, a, 0, me, sibling, True).wait_send()
            for j, chip in enumerate(chips[:2]):
                self._copy(refs, a, 1 + j, me, (*chip, c), True).wait_send()
            self._pass_on(refs, a).wait_send()
            for j, chip in enumerate(chips):
                self._copy(refs, a, 4 + j, (*chip, c), sibling).wait_send()
            self._local(refs, a).wait()


class _SiblingSend:
    def __init__(self, partials):
        self.operands = list(partials)
        self.n = len(partials)
        self.out_shape = [_sds((4, *p.shape[1:]), p.dtype) for p in partials]
        self.sems = [DMA_SEMS((4 * self.n,)), DMA_SEMS((4 * self.n,))]

    def _copy(self, refs, a, q):
        ins, outs, (send_sems, recv_sems) = refs
        x, y, c = _place()
        return pltpu.make_async_remote_copy(
            src_ref=ins[a].at[2 * q + 1 - c], dst_ref=outs[a].at[q],
            send_sem=send_sems.at[4 * a + q], recv_sem=recv_sems.at[4 * a + q],
            device_id=(x, y, 1 - c), device_id_type=MESH)

    def start(self, refs):
        for a in range(self.n):
            for q in range(4):
                self._copy(refs, a, q).start()

    def middle(self, refs):
        pass

    def finish(self, refs):
        for a in range(self.n):
            for q in range(4):
                self._copy(refs, a, q).wait()


class _ChipScatter:
    def __init__(self, chip_partials):
        self.operands = list(chip_partials)
        self.n = len(chip_partials)
        self.out_shape = [_sds(p.shape, p.dtype) for p in chip_partials]
        self.sems = [DMA_SEMS((3 * self.n,)), DMA_SEMS((3 * self.n,)), DMA_SEMS((self.n,))]

    def _copy(self, refs, a, k, arrival=False):
        ins, outs, (send_sems, recv_sems, _) = refs
        x, y, c = _place()
        px, py = _other_chips(x, y)[k]
        mine, theirs = 2 * x + y, 2 * px + py
        return pltpu.make_async_remote_copy(
            src_ref=ins[a].at[mine if arrival else theirs], dst_ref=outs[a].at[theirs if arrival else mine],
            send_sem=send_sems.at[3 * a + k], recv_sem=recv_sems.at[3 * a + k],
            device_id=(px, py, c), device_id_type=MESH)

    def _local(self, refs, a):
        ins, outs, (_, _, local_sems) = refs
        x, y, _ = _place()
        return pltpu.make_async_copy(ins[a].at[2 * x + y], outs[a].at[2 * x + y], local_sems.at[a])

    def start(self, refs):
        for a in range(self.n):
            self._local(refs, a).start()
            for k in range(3):
                self._copy(refs, a, k).start()

    def middle(self, refs):
        pass

    def finish(self, refs):
        for a in range(self.n):
            for k in range(3):
                self._copy(refs, a, k, arrival=True).wait_recv()
        for a in range(self.n):
            for k in range(3):
                self._copy(refs, a, k).wait_send()
            self._local(refs, a).wait()


class _DirectScatter:
    def __init__(self, partials):
        self.operands = list(partials)
        self.n = len(partials)
        self.out_shape = [_sds(p.shape, p.dtype) for p in partials]
        self.sems = [DMA_SEMS((7 * self.n,)), DMA_SEMS((7 * self.n,)), DMA_SEMS((self.n,))]

    def _copy(self, refs, a, k, arrival=False):
        ins, outs, (send_sems, recv_sems, _) = refs
        x, y, c = _place()
        peer = [(x, y, 1 - c), (1 - x, y, c), (x, 1 - y, c), (1 - x, 1 - y, c),
                (1 - x, y, 1 - c), (x, 1 - y, 1 - c), (1 - x, 1 - y, 1 - c)][k]
        mine, theirs = 4 * x + 2 * y + c, 4 * peer[0] + 2 * peer[1] + peer[2]
        return pltpu.make_async_remote_copy(
            src_ref=ins[a].at[mine if arrival else theirs], dst_ref=outs[a].at[theirs if arrival else mine],
            send_sem=send_sems.at[7 * a + k], recv_sem=recv_sems.at[7 * a + k],
            device_id=peer, device_id_type=MESH)

    def _local(self, refs, a):
        ins, outs, (_, _, local_sems) = refs
        x, y, c = _place()
        return pltpu.make_async_copy(ins[a].at[4 * x + 2 * y + c], outs[a].at[4 * x + 2 * y + c], local_sems.at[a])

    def start(self, refs):
        for a in range(self.n):
            self._local(refs, a).start()
            for k in range(7):
                self._copy(refs, a, k).start()

    def middle(self, refs):
        pass

    def finish(self, refs):
        for a in range(self.n):
            for k in range(7):
                self._copy(refs, a, k, arrival=True).wait_recv()
        for a in range(self.n):
            for k in range(7):
                self._copy(refs, a, k).wait_send()
            self._local(refs, a).wait()


def _task_args(tasks):
    hbm = pl.BlockSpec(memory_space=pl.ANY)
    operands = [o for t in tasks for o in t.operands]
    out_shape = [s for t in tasks for s in t.out_shape]
    sems = [s for t in tasks for s in t.sems]
    return operands, [hbm] * len(operands), out_shape, [hbm] * len(out_shape), sems


def _task_refs(tasks, ins, outs, sems):
    per_task = []
    for t in tasks:
        ni, no, ns = len(t.operands), len(t.out_shape), len(t.sems)
        per_task.append((ins[:ni], outs[:no], sems[:ns]))
        ins, outs, sems = ins[ni:], outs[no:], sems[ns:]
    return per_task


def _task_results(tasks, outs):
    res = []
    for t in tasks:
        res.append(list(outs[:len(t.out_shape)]))
        outs = outs[len(t.out_shape):]
    return res


def _carry(body, tasks, n_in, n_out, n_scratch, steps):
    if not tasks:
        return body
    t_in = sum(len(t.operands) for t in tasks)
    t_out = sum(len(t.out_shape) for t in tasks)

    def wrapped(*refs):
        ins, refs = refs[:n_in], refs[n_in:]
        t_ins, refs = refs[:t_in], refs[t_in:]
        outs, refs = refs[:n_out], refs[n_out:]
        t_outs, refs = refs[:t_out], refs[t_out:]
        scratch, t_sems = refs[:n_scratch], refs[n_scratch:]
        per_task = _task_refs(tasks, t_ins, t_outs, t_sems)
        step = pl.program_id(0)

        @pl.when(step == 0)
        def _():
            for t, r in zip(tasks, per_task):
                t.start(r)

        for t, r in zip(tasks, per_task):
            pl.when(step == max(steps - 1 - getattr(t, "lag", 0), 0))(functools.partial(t.middle, r))

        body(*ins, *outs, *scratch)

        @pl.when(step == steps - 1)
        def _():
            for t, r in zip(tasks, per_task):
                t.finish(r)

    return wrapped


def _exchange(name, tasks):
    operands, in_specs, out_shape, out_specs, sems = _task_args(tasks)

    def body(*refs):
        ni, no = len(operands), len(out_shape)
        per_task = _task_refs(tasks, refs[:ni], refs[ni:ni + no], refs[ni + no:])
        for phase in ("start", "middle", "finish"):
            for t, r in zip(tasks, per_task):
                getattr(t, phase)(r)

    outs = pl.pallas_call(body, name=name, in_specs=in_specs, out_specs=out_specs, out_shape=out_shape,
                          scratch_shapes=sems)(*operands)
    return _task_results(tasks, outs)


def _row_tile(rows, cols, whole_up_to=256 * 1024):
    if rows * cols <= whole_up_to:
        return rows
    for t in (256, 176, 128, 64, 32, 16, 8):
        if rows % t == 0:
            return t
    return rows


def _pair_sum(name, partials, from_sibling):
    _, rows, cols = partials.shape
    tile = _row_tile(rows, cols, 512 * 1024)

    def body(p_ref, s_ref, o_ref):
        mine = jnp.where(lax.axis_index("c") == 0, p_ref[0, 0].astype(F32), p_ref[0, 1].astype(F32))
        o_ref[0] = (mine + s_ref[0].astype(F32)).astype(o_ref.dtype)

    blk = pl.BlockSpec((1, tile, cols), lambda q, i: (q, i, 0))
    return pl.pallas_call(
        body, name=name, grid=(4, rows // tile),
        in_specs=[pl.BlockSpec((1, 2, tile, cols), lambda q, i: (q, 0, i, 0)), blk],
        out_specs=blk, out_shape=_sds(from_sibling.shape, from_sibling.dtype),
        compiler_params=_params("parallel", "parallel"),
    )(partials.reshape(4, 2, rows, cols), from_sibling)


def _sum_leading(name, stacked):
    parts, rows, cols = stacked.shape
    tile = _row_tile(rows, cols, (512 if parts <= 4 else 256) * 1024)

    def body(s_ref, o_ref):
        acc = s_ref[0].astype(F32)
        for d in range(1, parts):
            acc = acc + s_ref[d].astype(F32)
        o_ref[...] = acc

    return pl.pallas_call(
        body, name=name, grid=(rows // tile,),
        in_specs=[pl.BlockSpec((parts, tile, cols), lambda i: (0, i, 0))],
        out_specs=pl.BlockSpec((tile, cols), lambda i: (i, 0)),
        out_shape=_sds((rows, cols), F32),
        compiler_params=_params("parallel"),
    )(stacked)


def _adamw_math(w, g, m, v):
    nm = ADAM_B1 * m + (1.0 - ADAM_B1) * g
    nv = ADAM_B2 * v + (1.0 - ADAM_B2) * (g * g)
    m_hat = nm / (1.0 - ADAM_B1 ** ADAM_STEP)
    v_hat = nv / (1.0 - ADAM_B2 ** ADAM_STEP)
    return -ADAM_LR * (m_hat / (jnp.sqrt(v_hat) + ADAM_EPS) + ADAM_WD * w), nm, nv


def _adamw(name, w, g, m, v):
    rows, cols = w.shape
    tile = _row_tile(rows, cols)

    def body(w_ref, g_ref, m_ref, v_ref, d_ref, nm_ref, nv_ref):
        d_ref[...], nm_ref[...], nv_ref[...] = _adamw_math(w_ref[...], g_ref[...], m_ref[...], v_ref[...])

    blk = pl.BlockSpec((tile, cols), lambda i: (i, 0))
    return pl.pallas_call(
        body, name=name, grid=(rows // tile,),
        in_specs=[blk] * 4, out_specs=[blk] * 3,
        out_shape=[_sds((rows, cols), F32)] * 3,
        compiler_params=_params("parallel"),
    )(w, g, m, v)


def _sum_adamw(name, stacked, w, m, v):
    parts, rows, cols = stacked.shape
    tile = _row_tile(rows, cols)

    def body(s_ref, w_ref, m_ref, v_ref, g_ref, d_ref, nm_ref, nv_ref):
        g = s_ref[0].astype(F32)
        for d in range(1, parts):
            g = g + s_ref[d].astype(F32)
        g_ref[...] = g
        d_ref[...], nm_ref[...], nv_ref[...] = _adamw_math(w_ref[...], g, m_ref[...], v_ref[...])

    blk = pl.BlockSpec((tile, cols), lambda i: (i, 0))
    return pl.pallas_call(
        body, name=name, grid=(rows // tile,),
        in_specs=[pl.BlockSpec((parts, tile, cols), lambda i: (0, i, 0))] + [blk] * 3, out_specs=[blk] * 4,
        out_shape=[_sds((rows, cols), F32)] * 4,
        compiler_params=_params("parallel"),
    )(stacked, w, m, v)


SMALL = ("b_gate", "w_pool", "pool_scale", "ln1_g", "ln1_b", "conv_b", "ln2_g", "ln2_b")
TILE = 8 * LANE


def _pack(parts):
    tiles = []
    for p in parts:
        flat = p.reshape(-1)
        tiles.append(jnp.pad(flat, (0, -flat.size % TILE)).reshape(-1, LANE))
    return jnp.concatenate(tiles, axis=0)


def _unpack(packed, shapes):
    out, at = [], 0
    for shape in shapes:
        size = math.prod(shape)
        rows = -(-size // TILE) * 8
        out.append(packed[at:at + rows].reshape(-1)[:size].reshape(shape))
        at += rows
    return out


MIXER = ("w_branch_attn", "w_branch_pool", "w_out", "conv_w")
FFN = ("w_ffn_gate_t", "w_ffn_up_t", "w_ffn_down")


def _columns(t):
    return jnp.transpose(t, (1, 0, 2)).reshape(t.shape[1], N_DEV * t.shape[2])


def _row_blocks(t):
    return t.reshape(N_DEV * t.shape[1], t.shape[2])


def _by_owner(t):
    return t.reshape(N_DEV, t.shape[0] // N_DEV, t.shape[1])


def _reduce_halves(names, partials, from_sibling):
    return [_pair_sum("pair_sum_" + n, p, s) for n, p, s in zip(names, partials, from_sibling)]


def _local_step(x, target, shards, small):
    seq = x.shape[0]
    cos, sin = _rope_tables(seq)
    whole = lambda width: pl.BlockSpec((seq, width), lambda *_: (0, 0))
    ((w_in_all,),) = _exchange("gather_w_in", [_AllGather([shards["w_in"]])])
    (xb, q, k, v, u, g, kmean), (mixer,) = _proj_in(
        x, w_in_all, small["b_gate"], cos, sin, tasks=[_AllGather([shards[n] for n in MIXER], lag=1)])
    wba, wbp, wout, conv_w = _columns(mixer[0]), _columns(mixer[1]), _row_blocks(mixer[2]), _columns(mixer[3])
    (o, lse, bias), ((wgt, wut),) = _attn_fwd(
        q, k, v, kmean.reshape(seq // MOBA_BLOCK, D_ATTN),
        tasks=[_AllGather([shards["w_ffn_gate_t"], shards["w_ffn_up_t"]])])
    (ya, yp, pooled, mixed, ypre, merged, xhat1, rstd1, h1, h1b), _ = _mix(
        o, u, g, x, wba, wbp, wout, small["w_pool"], small["pool_scale"], small["ln1_g"], small["ln1_b"])
    wgt, wut = _row_blocks(wgt), _row_blocks(wut)
    (a, uf, cdf, act), ((wd,),) = _ffn_up(
        h1b, wgt, wut, conv_w, small["conv_b"], tasks=[_AllGather([shards["w_ffn_down"]], lag=3)])
    wd = _row_blocks(wd)
    dr2, dr2b, loss, dg2, db2 = _ffn_down(act, wd, h1, target, small["ln2_g"], small["ln2_b"])

    da, du, dwd, dwg, dwu, dconv = _ffn_bwd(dr2b, h1b, a, uf, cdf, act, wd, conv_w, small["conv_b"])
    ffn_partials = [_by_owner(dwg), _by_owner(dwu), _by_owner(dwd)]
    (dr1, dr1b, dg1, db1), (ffn_sibling,) = _ln1_bwd(
        dr2, da, du, wgt, wut, xhat1, rstd1, small["ln1_g"], tasks=[_SiblingSend(ffn_partials)])
    ffn_chip = _reduce_halves(FFN, ffn_partials, ffn_sibling)
    (dzg, dya, dyp, do, dmixed, dpooled, dbg, dps), (gate_landed,) = _mix_bwd(
        dr1b, ya, yp, g, mixed, wout, wba, wbp, small["w_pool"], small["pool_scale"],
        tasks=[_ChipScatter(ffn_chip[0:1])])
    dw_out, _ = _tn_matmul(
        "dw_out", merged, dr1b, (D_MODEL, D_MODEL), BF16, 4,
        pl.BlockSpec((seq, 256), lambda m: (0, m)), whole(D_MODEL), pl.BlockSpec((256, D_MODEL), lambda m: (m, 0)))
    dw_ba, _ = _tn_matmul(
        "dw_branch_attn", o, dya, (N_DEV, D_ATTN, LANE), BF16, N_DEV,
        whole(D_ATTN), pl.BlockSpec((seq, LANE), lambda n: (0, n)), pl.BlockSpec((1, D_ATTN, LANE), lambda n: (n, 0, 0)))
    dw_bp, _ = _tn_matmul(
        "dw_branch_pool", ypre, dyp, (N_DEV, D_POOL, LANE), BF16, N_DEV,
        whole(D_POOL), pl.BlockSpec((seq, LANE), lambda n: (0, n)), pl.BlockSpec((1, D_POOL, LANE), lambda n: (n, 0, 0)))
    dw_pool, _ = _tn_matmul(
        "dw_pool", pooled, dmixed, (len(POOL_WINDOWS), POOL_GROUP, POOL_GROUP), F32, len(POOL_WINDOWS),
        pl.BlockSpec((seq, POOL_GROUP), lambda n: (0, n)), pl.BlockSpec((seq, POOL_GROUP), lambda n: (0, n)),
        pl.BlockSpec((1, POOL_GROUP, POOL_GROUP), lambda n: (n, 0, 0)))
    mixer_partials = [dw_ba, dw_bp, _by_owner(dw_out)]
    (dq, dk, dv), (up_down_landed, mixer_sibling) = _attn_bwd(
        q, k, v, bias, o, lse, do, cos, sin, tasks=[_ChipScatter(ffn_chip[1:3]), _SiblingSend(mixer_partials)])
    mixer_chip = _reduce_halves(MIXER[:3], mixer_partials, mixer_sibling)
    (grad_x, dz), _ = _in_bwd(dq, dk, dv, dpooled, dzg, dr1, w_in_all)
    dw_in, (mixer_landed,) = _tn_matmul(
        "dw_in", xb, dz, (N_DEV, D_MODEL, D_ATTN), BF16, 2 * N_DEV,
        pl.BlockSpec((seq, 512), lambda s: (0, s % 2)), pl.BlockSpec((seq, D_ATTN), lambda s: (0, s // 2)),
        pl.BlockSpec((1, 512, D_ATTN), lambda s: (s // 2, s % 2, 0)), tasks=[_ChipScatter(mixer_chip)])

    landed = dict(zip(FFN + MIXER[:3], gate_landed + up_down_landed + mixer_landed))
    little = {"b_gate": dbg, "w_pool": dw_pool, "pool_scale": dps, "ln1_g": dg1, "ln1_b": db1, "conv_b": dconv[3:4],
              "ln2_g": dg2, "ln2_b": db2, "conv_w": dconv[0:3], "loss": loss}
    return grad_x, landed, dw_in, little


def kernel(x, w_in, b_gate, w_branch_attn, w_pool, pool_scale, w_branch_pool, w_out, ln1_g, ln1_b, w_ffn_gate, w_ffn_up, conv_w, conv_b, w_ffn_down, ln2_g, ln2_b, loss_target, m_w_in, m_b_gate, m_w_branch_attn, m_w_pool, m_pool_scale, m_w_branch_pool, m_w_out, m_ln1_g, m_ln1_b, m_w_ffn_gate, m_w_ffn_up, m_conv_w, m_conv_b, m_w_ffn_down, m_ln2_g, m_ln2_b, v_w_in, v_b_gate, v_w_branch_attn, v_w_pool, v_pool_scale, v_w_branch_pool, v_w_out, v_ln1_g, v_ln1_b, v_w_ffn_gate, v_w_ffn_up, v_conv_w, v_conv_b, v_w_ffn_down, v_ln2_g, v_ln2_b):
    me = 4 * lax.axis_index("x") + 2 * lax.axis_index("y") + lax.axis_index("c")
    weights = dict(w_in=w_in, b_gate=b_gate, w_branch_attn=w_branch_attn, w_pool=w_pool, pool_scale=pool_scale,
                   w_branch_pool=w_branch_pool, w_out=w_out, ln1_g=ln1_g, ln1_b=ln1_b, w_ffn_gate=w_ffn_gate,
                   w_ffn_up=w_ffn_up, conv_w=conv_w, conv_b=conv_b, w_ffn_down=w_ffn_down, ln2_g=ln2_g, ln2_b=ln2_b)
    m_in = dict(w_in=m_w_in, b_gate=m_b_gate, w_branch_attn=m_w_branch_attn, w_pool=m_w_pool,
                pool_scale=m_pool_scale, w_branch_pool=m_w_branch_pool, w_out=m_w_out, ln1_g=m_ln1_g, ln1_b=m_ln1_b,
                w_ffn_gate=m_w_ffn_gate, w_ffn_up=m_w_ffn_up, conv_w=m_conv_w, conv_b=m_conv_b,
                w_ffn_down=m_w_ffn_down, ln2_g=m_ln2_g, ln2_b=m_ln2_b)
    v_in = dict(w_in=v_w_in, b_gate=v_b_gate, w_branch_attn=v_w_branch_attn, w_pool=v_w_pool,
                pool_scale=v_pool_scale, w_branch_pool=v_w_branch_pool, w_out=v_w_out, ln1_g=v_ln1_g, ln1_b=v_ln1_b,
                w_ffn_gate=v_w_ffn_gate, w_ffn_up=v_w_ffn_up, conv_w=v_conv_w, conv_b=v_conv_b,
                w_ffn_down=v_w_ffn_down, ln2_g=v_ln2_g, ln2_b=v_ln2_b)
    weights = {n: a[0] for n, a in weights.items()}
    m_in = {n: a[0] for n, a in m_in.items()}
    v_in = {n: a[0] for n, a in v_in.items()}

    shards = {"w_in": weights["w_in"].astype(BF16), "w_branch_attn": weights["w_branch_attn"].astype(BF16),
              "w_branch_pool": weights["w_branch_pool"].astype(BF16), "w_out": weights["w_out"].astype(BF16),
              "w_ffn_gate_t": weights["w_ffn_gate"].T.astype(BF16), "w_ffn_up_t": weights["w_ffn_up"].T.astype(BF16),
              "w_ffn_down": weights["w_ffn_down"].astype(BF16), "conv_w": weights["conv_w"]}
    small = {"b_gate": weights["b_gate"][None], "w_pool": weights["w_pool"], "pool_scale": weights["pool_scale"][None],
             "ln1_g": weights["ln1_g"][None], "ln1_b": weights["ln1_b"][None], "conv_b": weights["conv_b"][None],
             "ln2_g": weights["ln2_g"][None], "ln2_b": weights["ln2_b"][None]}

    grad_x, landed, dw_in, little = _local_step(x[0], loss_target[0], shards, small)

    ((w_in_sibling,),) = _exchange("sibling_grads", [_SiblingSend([dw_in])])
    w_in_chip = _reduce_halves(["w_in"], [dw_in], [w_in_sibling])
    names = SMALL + ("conv_w",)
    (landed["w_in"],), (all_small,) = _exchange(
        "scatter_grads", [_ChipScatter(w_in_chip), _AllGather([_pack([little[n] for n in names + ("loss",)])])])

    grads, delta, new_m, new_v = {}, {}, {}, {}
    for n in ("w_in", "w_branch_attn", "w_branch_pool", "w_out", "w_ffn_down"):
        grads[n], delta[n], new_m[n], new_v[n] = _sum_adamw("update_" + n, landed[n], weights[n], m_in[n], v_in[n])
    for n in ("w_ffn_gate", "w_ffn_up"):
        updated = _sum_adamw("update_" + n, landed[n + "_t"], weights[n].T, m_in[n].T, v_in[n].T)
        grads[n], delta[n], new_m[n], new_v[n] = (t.T for t in updated)
    small_sum = _sum_leading("sum_small", all_small)
    *small_grads, conv_w_grad, loss = _unpack(
        small_sum, [weights[n].shape for n in SMALL] + [(3, D_FF), little["loss"].shape])
    loss = loss[0, 0]
    grads.update(zip(SMALL, small_grads))
    grads["conv_w"] = lax.dynamic_slice(conv_w_grad, (0, me * FF_SHARD), (3, FF_SHARD))
    flat = lambda d: _pack([d[n] for n in names])
    shapes = [weights[n].shape for n in names]
    for out, packed in zip((delta, new_m, new_v),
                           _adamw("adamw_small", flat(weights), flat(grads), flat(m_in), flat(v_in))):
        out.update(zip(names, _unpack(packed, shapes)))

    order = ("w_in", "b_gate", "w_branch_attn", "w_pool", "pool_scale", "w_branch_pool", "w_out", "ln1_g", "ln1_b",
             "w_ffn_gate", "w_ffn_up", "conv_w", "conv_b", "w_ffn_down", "ln2_g", "ln2_b")
    lead = lambda t: t[None]
    return (loss, lead(grad_x), *[lead(grads[n]) for n in order], *[lead(delta[n]) for n in order],
            *[lead(new_m[n]) for n in order], *[lead(new_v[n]) for n in order])
```

```python
import functools
import math

import jax
import jax.numpy as jnp
from jax import lax
from jax.experimental import pallas as pl
from jax.experimental.pallas import tpu as pltpu

F32 = jnp.float32
BF16 = jnp.bfloat16

D_MODEL = 1024
N_HEADS = 8
HEAD_DIM = 64
D_ATTN = N_HEADS * HEAD_DIM
MOBA_BLOCK = 256
MOBA_TOPK = 3
ROPE_THETA = 10000.0
POOL_WINDOWS = (2, 4, 8, 16)
POOL_GROUP = 128
D_POOL = len(POOL_WINDOWS) * POOL_GROUP
POOL_HALO = 16
D_FF = 2816
D_IN_PROJ = 3 * D_ATTN + D_POOL + 2 * D_MODEL
LN_EPS = 1e-5
ALPHA = 2.0 ** 0.25
NEG = -1e30
N_DEV = 8
FF_SHARD = D_FF // N_DEV

ADAM_LR = 0.001
ADAM_B1 = 0.9
ADAM_B2 = 0.999
ADAM_EPS = 1e-08
ADAM_WD = 0.01
ADAM_STEP = 10

TOK = 256
FF_CHUNK = 256
LANE = 128
VMEM_LIMIT = 56 * 1024 * 1024

MESH = pl.DeviceIdType.MESH
NT_DIMS = (((1,), (1,)), ((), ()))
TN_DIMS = (((0,), (0,)), ((), ()))


def _params(*sem):
    return pltpu.CompilerParams(dimension_semantics=sem or None, vmem_limit_bytes=VMEM_LIMIT)


def _full(shape):
    zeros = (0,) * len(shape)
    return pl.BlockSpec(shape, lambda *_: zeros, pipeline_mode=pl.Buffered(1))


def _rows(width, tile=TOK):
    return pl.BlockSpec((tile, width), lambda i: (i, 0))


def _sds(shape, dtype):
    return jax.ShapeDtypeStruct(shape, dtype)


def _dot(a, b):
    return jnp.dot(a, b, preferred_element_type=F32)


def _dot_nt(a, b):
    return lax.dot_general(a, b, NT_DIMS, preferred_element_type=F32)


def _dot_tn(a, b):
    return lax.dot_general(a, b, TN_DIMS, preferred_element_type=F32)


def _rope_tables(seq):
    half = HEAD_DIM // 2
    inv_freq = 1.0 / (ROPE_THETA ** (jnp.arange(half, dtype=F32) / half))
    ang = jnp.arange(seq, dtype=F32)[:, None] * inv_freq[None, :]
    cos, sin = jnp.cos(ang), jnp.sin(ang)
    return jnp.tile(cos, (1, 4)), jnp.tile(jnp.concatenate([-sin, sin], axis=1), (1, 2))


def _swap_halves(t):
    lane = lax.broadcasted_iota(jnp.int32, t.shape, 1)
    return jnp.where((lane % HEAD_DIM) < HEAD_DIM // 2, pltpu.roll(t, LANE - 32, 1), pltpu.roll(t, 32, 1))


def _rope(t, cos, sin):
    return t * cos + _swap_halves(t) * sin


def _rope_transposed(g, cos, sin):
    return g * cos + _swap_halves(g * sin)


def _ln_fwd(r, g, b):
    mu = jnp.mean(r, axis=-1, keepdims=True)
    xc = r - mu
    var = jnp.mean(xc * xc, axis=-1, keepdims=True)
    rstd = lax.rsqrt(var + LN_EPS)
    xhat = xc * rstd
    return xhat * g + b, xhat, rstd


def _ln_bwd(dy, xhat, rstd, g):
    dxh = dy * g
    m1 = jnp.mean(dxh, axis=-1, keepdims=True)
    m2 = jnp.mean(dxh * xhat, axis=-1, keepdims=True)
    return rstd * (dxh - m1 - xhat * m2)


def _normal_cdf(a):
    return 0.5 * (1.0 + lax.erf(a * (1.0 / math.sqrt(2.0))))


def _gelu_derivative(a, cdf):
    return cdf + a * (jnp.exp(-0.5 * a * a) * (1.0 / math.sqrt(2.0 * math.pi)))


def _shift_down(a, k):
    row = lax.broadcasted_iota(jnp.int32, a.shape, 0)
    return jnp.where(row >= k, pltpu.roll(a, k, 0), 0.0)


def _shift_up(a, k):
    n = a.shape[0]
    row = lax.broadcasted_iota(jnp.int32, a.shape, 0)
    return jnp.where(row < n - k, pltpu.roll(a, n - k, 0), 0.0)


def _conv(a, cw, cb):
    return cw[2:3, :] * a + cw[1:2, :] * _shift_down(a, 1) + cw[0:1, :] * _shift_down(a, 2) + cb


def _pool_count(first_row, rows, window):
    t = first_row + lax.broadcasted_iota(jnp.int32, (rows, 1), 0)
    return jnp.minimum(t + 1, window).astype(F32)


def _grid_call(body, name, steps, in_specs, out_specs, out_shape, operands, scratch=(), tasks=()):
    t_operands, t_in_specs, t_out_shape, t_out_specs, t_sems = _task_args(tasks)
    outs = pl.pallas_call(
        _carry(body, tasks, len(in_specs), len(out_specs), len(scratch), steps), name=name, grid=(steps,),
        in_specs=list(in_specs) + t_in_specs, out_specs=list(out_specs) + t_out_specs,
        out_shape=list(out_shape) + t_out_shape, scratch_shapes=list(scratch) + t_sems,
        compiler_params=_params("arbitrary"),
    )(*operands, *t_operands)
    return outs[:len(out_specs)], _task_results(tasks, outs[len(out_specs):])


def _proj_in(x, win, b_gate, cos, sin, tasks=()):
    seq = x.shape[0]
    nt = seq // TOK

    def body(x_ref, win_ref, bg_ref, cos_ref, sin_ref, xb_ref, q_ref, k_ref, v_ref, u_ref, g_ref, km_ref):
        xb = x_ref[...].astype(BF16)
        xb_ref[...] = xb
        cos_t, sin_t = cos_ref[...], sin_ref[...]
        for sec, out_ref in ((0, q_ref), (1, k_ref)):
            z = _dot(xb, win_ref[sec])
            for c in range(D_ATTN // LANE):
                cols = slice(LANE * c, LANE * (c + 1))
                out_ref[:, cols] = _rope(z[:, cols], cos_t, sin_t)
        km_ref[0] = jnp.mean(k_ref[...], axis=0, keepdims=True)
        v_ref[...] = _dot(xb, win_ref[2]).astype(BF16)
        u_ref[...] = _dot(xb, win_ref[3])
        for n in range(4):
            cols = slice(D_ATTN * n, D_ATTN * (n + 1))
            g_ref[:, cols] = jax.nn.sigmoid(_dot(xb, win_ref[4 + n]) + bg_ref[:, cols])

    return _grid_call(
        body, "proj_in", nt,
        in_specs=[_rows(D_MODEL), _full(win.shape), _full((1, 2 * D_MODEL)), _rows(LANE), _rows(LANE)],
        out_specs=[_rows(D_MODEL), _rows(D_ATTN), _rows(D_ATTN), _rows(D_ATTN), _rows(D_POOL), _rows(2 * D_MODEL),
                   pl.BlockSpec((1, 1, D_ATTN), lambda i: (i, 0, 0))],
        out_shape=[_sds((seq, D_MODEL), BF16), _sds((seq, D_ATTN), F32), _sds((seq, D_ATTN), F32),
                   _sds((seq, D_ATTN), BF16), _sds((seq, D_POOL), F32), _sds((seq, 2 * D_MODEL), F32),
                   _sds((nt, 1, D_ATTN), F32)],
        operands=(x, win, b_gate, cos, sin), tasks=tasks)


SCORE_CHUNK = 128


def _store_keys(ka_sc, k_ref, ls):
    seq = ka_sc.shape[0]
    ka_sc[:, 0:HEAD_DIM] = k_ref[:, ls].astype(BF16)
    row = lax.broadcasted_iota(jnp.int32, (seq, HEAD_DIM), 0)
    lane = lax.broadcasted_iota(jnp.int32, (seq, HEAD_DIM), 1)
    in_block = (lane * MOBA_BLOCK <= row) & (row < (lane + 1) * MOBA_BLOCK)
    ka_sc[:, HEAD_DIM:] = jnp.where(in_block, 1.0, 0.0).astype(BF16)


def _block_bias(qf, km, i):
    if i <= MOBA_TOPK:
        return jnp.zeros((MOBA_BLOCK, HEAD_DIM), BF16)
    nb = km.shape[0]
    gate = lax.dot_general(km, qf, NT_DIMS, precision=lax.Precision.HIGHEST, preferred_element_type=F32)
    blk = lax.broadcasted_iota(jnp.int32, gate.shape, 0)
    rank = jnp.zeros(gate.shape, F32)
    for r in range(1, i):
        lower = pltpu.roll(gate, r, 0)
        rank = rank + jnp.where((blk >= r) & (lower >= gate), 1.0, 0.0)
        higher = pltpu.roll(gate, nb - r, 0)
        rank = rank + jnp.where((blk + r < i) & (higher > gate), 1.0, 0.0)
    bias = jnp.where((blk < i) & (rank >= MOBA_TOPK), NEG, 0.0)
    padded = jnp.concatenate([bias, jnp.zeros((LANE - nb, MOBA_BLOCK), F32)], axis=0)
    return jnp.transpose(padded)[:, 0:HEAD_DIM].astype(BF16)


def _causal(shape, transposed=False):
    row = lax.broadcasted_iota(jnp.int32, shape, 0)
    col = lax.broadcasted_iota(jnp.int32, shape, 1)
    return (row <= col) if transposed else (col <= row)


def _row_vector(col):
    return jnp.transpose(jnp.broadcast_to(col, (MOBA_BLOCK, LANE)))[0:1, :]


def _attn_fwd(q, k, v, kmean, tasks=()):
    seq = q.shape[0]
    nb = seq // MOBA_BLOCK
    assert nb == 8, "the block ranking keeps one sublane per key block"
    pair = pl.BlockSpec((seq, LANE), lambda p: (0, p))
    heads = LANE // HEAD_DIM

    def body(q_ref, k_ref, v_ref, km_ref, o_ref, lse_ref, bias_ref, ka_sc, qa_sc, s_sc, p_sc):
        lse_ref[0, heads:, :] = jnp.zeros((8 - heads, seq), F32)
        for hh in range(heads):
            ls = slice(HEAD_DIM * hh, HEAD_DIM * (hh + 1))
            _store_keys(ka_sc, k_ref, ls)
            vb = v_ref[:, ls]
            km = km_ref[:, ls]
            for i in range(nb):
                rs = slice(MOBA_BLOCK * i, MOBA_BLOCK * (i + 1))
                width = MOBA_BLOCK * (i + 1)
                qf = q_ref[rs, ls]
                bias = _block_bias(qf, km, i)
                bias_ref[rs, ls] = bias
                qa_sc[:, 0:HEAD_DIM] = (qf * HEAD_DIM ** -0.5).astype(BF16)
                qa_sc[:, HEAD_DIM:] = bias
                s_sc[:, 0:width] = _dot_nt(qa_sc[...], ka_sc[0:width, :])
                s_sc[:, rs] = jnp.where(_causal((MOBA_BLOCK, MOBA_BLOCK)), s_sc[:, rs], NEG)
                chunks = [slice(SCORE_CHUNK * c, SCORE_CHUNK * (c + 1)) for c in range(width // SCORE_CHUNK)]
                top = s_sc[:, chunks[0]]
                for c in chunks[1:]:
                    top = jnp.maximum(top, s_sc[:, c])
                m = jnp.max(top, axis=1, keepdims=True)
                total = jnp.zeros((MOBA_BLOCK, SCORE_CHUNK), F32)
                for c in chunks:
                    p = jnp.exp(s_sc[:, c] - m)
                    total = total + p
                    p_sc[:, c] = p.astype(BF16)
                l = jnp.sum(total, axis=1, keepdims=True)
                o_ref[rs, ls] = _dot(p_sc[:, 0:width], vb[0:width]) / l
                lse_ref[0, hh:hh + 1, rs] = _row_vector(m + jnp.log(l))

    return _grid_call(
        body, "attn_fwd", D_ATTN // LANE,
        in_specs=[pair, pair, pair, pl.BlockSpec((nb, LANE), lambda p: (0, p))],
        out_specs=[pair, pl.BlockSpec((1, 8, seq), lambda p: (p, 0, 0)), pair],
        out_shape=[_sds((seq, D_ATTN), F32), _sds((D_ATTN // LANE, 8, seq), F32), _sds((seq, D_ATTN), BF16)],
        operands=(q, k, v, kmean),
        scratch=[pltpu.VMEM((seq, LANE), BF16), pltpu.VMEM((MOBA_BLOCK, LANE), BF16),
                 pltpu.VMEM((MOBA_BLOCK, seq), F32), pltpu.VMEM((MOBA_BLOCK, seq), BF16)],
        tasks=tasks)


def _mix(o, u, g, x, wba, wbp, wout, w_pool, pool_scale, ln_g, ln_b, tasks=()):
    seq = x.shape[0]

    def body(o_ref, u_ref, uprev_ref, g_ref, x_ref, wba_ref, wbp_ref, wout_ref, wp_ref, ps_ref, lg_ref, lb_ref,
             ya_ref, yp_ref, pooled_ref, mixed_ref, ypre_ref, merged_ref, xhat_ref, rstd_ref, h_ref, hb_ref, ext):
        i = pl.program_id(0)
        ya = _dot(o_ref[...].astype(BF16), wba_ref[...])
        ucur = u_ref[...]
        ext[0:POOL_HALO, :] = jnp.where(i > 0, uprev_ref[...], 0.0)
        ext[POOL_HALO:, :] = ucur
        for grp, window in enumerate(POOL_WINDOWS):
            cols = slice(POOL_GROUP * grp, POOL_GROUP * (grp + 1))
            acc = ucur[:, cols]
            for kk in range(1, window):
                acc = acc + ext[pl.ds(POOL_HALO - kk, TOK), cols]
            pooled = acc / _pool_count(i * TOK, TOK, window) - ucur[:, cols]
            pooled_ref[:, cols] = pooled.astype(BF16)
            mixed_ref[:, cols] = _dot(pooled.astype(BF16), wp_ref[grp].astype(BF16))
        mixed = mixed_ref[...]
        ypre = (mixed * ps_ref[...]).astype(BF16)
        ypre_ref[...] = ypre
        yp = _dot(ypre, wbp_ref[...])
        ya_ref[...] = ya
        yp_ref[...] = yp
        merged = (g_ref[:, :D_MODEL] * ya + g_ref[:, D_MODEL:] * yp).astype(BF16)
        merged_ref[...] = merged
        r1 = ALPHA * x_ref[...] + _dot(merged, wout_ref[...])
        h, xhat, rstd = _ln_fwd(r1, lg_ref[...], lb_ref[...])
        xhat_ref[...] = xhat
        rstd_ref[...] = jnp.broadcast_to(rstd, (TOK, LANE))
        h_ref[...] = h
        hb_ref[...] = h.astype(BF16)

    halo = pl.BlockSpec((POOL_HALO, D_POOL), lambda i: (jnp.maximum(i * (TOK // POOL_HALO) - 1, 0), 0))
    return _grid_call(
        body, "mix", seq // TOK,
        in_specs=[_rows(D_ATTN), _rows(D_POOL), halo, _rows(2 * D_MODEL), _rows(D_MODEL),
                  _full(wba.shape), _full(wbp.shape), _full(wout.shape), _full(w_pool.shape),
                  _full((1, D_POOL)), _full((1, D_MODEL)), _full((1, D_MODEL))],
        out_specs=[_rows(D_MODEL), _rows(D_MODEL), _rows(D_POOL), _rows(D_POOL), _rows(D_POOL), _rows(D_MODEL),
                   _rows(D_MODEL), _rows(LANE), _rows(D_MODEL), _rows(D_MODEL)],
        out_shape=[_sds((seq, D_MODEL), F32), _sds((seq, D_MODEL), F32), _sds((seq, D_POOL), BF16),
                   _sds((seq, D_POOL), F32), _sds((seq, D_POOL), BF16), _sds((seq, D_MODEL), BF16),
                   _sds((seq, D_MODEL), F32), _sds((seq, LANE), F32), _sds((seq, D_MODEL), F32),
                   _sds((seq, D_MODEL), BF16)],
        operands=(o, u, u, g, x, wba, wbp, wout, w_pool, pool_scale, ln_g, ln_b),
        scratch=[pltpu.VMEM((TOK + POOL_HALO, D_POOL), F32)], tasks=tasks)


def _ffn_up(hb, wgt, wut, conv_w, conv_b, tasks=()):
    seq = hb.shape[0]
    wblk = pl.BlockSpec((FF_CHUNK, D_MODEL), lambda c: (c, 0))
    cblk = lambda rows: pl.BlockSpec((rows, FF_CHUNK), lambda c: (0, c))
    oblk = pl.BlockSpec((seq, FF_CHUNK), lambda c: (0, c))

    def body(h_ref, wg_ref, wu_ref, cw_ref, cb_ref, a_ref, u_ref, act_ref):
        h = h_ref[...]
        a = _dot_nt(h, wg_ref[...])
        u = _dot_nt(h, wu_ref[...])
        a_ref[...] = a
        u_ref[...] = u
        ac = _conv(a, cw_ref[...], cb_ref[...])
        act_ref[...] = (ac * _normal_cdf(ac) * u).astype(BF16)

    return _grid_call(
        body, "ffn_up", D_FF // FF_CHUNK,
        in_specs=[_full(hb.shape), wblk, wblk, cblk(3), cblk(1)],
        out_specs=[oblk, oblk, oblk],
        out_shape=[_sds((seq, D_FF), F32), _sds((seq, D_FF), F32), _sds((seq, D_FF), BF16)],
        operands=(hb, wgt, wut, conv_w, conv_b), tasks=tasks)


def _ffn_down(act, wd, h, target, ln_g, ln_b):
    seq = h.shape[0]

    def body(act_ref, wd_ref, h_ref, t_ref, lg_ref, lb_ref, dr_ref, drb_ref, loss_ref, dg_ref, db_ref):
        i = pl.program_id(0)

        @pl.when(i == 0)
        def _():
            loss_ref[...] = jnp.zeros_like(loss_ref)
            dg_ref[...] = jnp.zeros_like(dg_ref)
            db_ref[...] = jnp.zeros_like(db_ref)

        r2 = ALPHA * h_ref[...] + _dot(act_ref[...], wd_ref[...])
        y, xhat, rstd = _ln_fwd(r2, lg_ref[...], lb_ref[...])
        diff = y - t_ref[...]
        loss_ref[...] += jnp.sum(diff * diff) * (0.5 / D_MODEL)
        dy = diff * (1.0 / D_MODEL)
        dg_ref[...] += jnp.sum(dy * xhat, axis=0, keepdims=True)
        db_ref[...] += jnp.sum(dy, axis=0, keepdims=True)
        dr = _ln_bwd(dy, xhat, rstd, lg_ref[...])
        dr_ref[...] = dr
        drb_ref[...] = dr.astype(BF16)

    vec = pl.BlockSpec((1, D_MODEL), lambda i: (0, 0))
    return pl.pallas_call(
        body, name="ffn_down", grid=(seq // TOK,),
        in_specs=[_rows(D_FF), _full(wd.shape), _rows(D_MODEL), _rows(D_MODEL), _full((1, D_MODEL)), _full((1, D_MODEL))],
        out_specs=[_rows(D_MODEL), _rows(D_MODEL), pl.BlockSpec((8, LANE), lambda i: (0, 0)), vec, vec],
        out_shape=[_sds((seq, D_MODEL), F32), _sds((seq, D_MODEL), BF16), _sds((8, LANE), F32),
                   _sds((1, D_MODEL), F32), _sds((1, D_MODEL), F32)],
        compiler_params=_params("arbitrary"),
    )(act, wd, h, target, ln_g, ln_b)


def _ffn_bwd(drb, hb, a, u, wd, conv_w, conv_b):
    seq = hb.shape[0]
    wblk = pl.BlockSpec((FF_CHUNK, D_MODEL), lambda c: (c, 0))
    cblk = lambda rows: pl.BlockSpec((rows, FF_CHUNK), lambda c: (0, c))
    sblk = pl.BlockSpec((seq, FF_CHUNK), lambda c: (0, c))

    def body(dr_ref, h_ref, a_ref, u_ref, wd_ref, cw_ref, cb_ref, da_ref, du_ref, dwd_ref, dwg_ref, dwu_ref, dc_ref):
        dr = dr_ref[...]
        h = h_ref[...]
        a = a_ref[...]
        u = u_ref[...]
        cw = cw_ref[...]
        dact = _dot_nt(dr, wd_ref[...])
        ac = _conv(a, cw, cb_ref[...])
        cdf = _normal_cdf(ac)
        gelu = ac * cdf
        dwd_ref[...] = _dot_tn((gelu * u).astype(BF16), dr).astype(BF16)
        du = (dact * gelu).astype(BF16)
        dac = dact * u * _gelu_derivative(ac, cdf)
        da = (cw[2:3, :] * dac + cw[1:2, :] * _shift_up(dac, 1) + cw[0:1, :] * _shift_up(dac, 2)).astype(BF16)
        da_ref[...] = da
        du_ref[...] = du
        dwg_ref[...] = _dot_tn(da, h).astype(BF16)
        dwu_ref[...] = _dot_tn(du, h).astype(BF16)
        dc_ref[0:1, :] = jnp.sum(dac * _shift_down(a, 2), axis=0, keepdims=True)
        dc_ref[1:2, :] = jnp.sum(dac * _shift_down(a, 1), axis=0, keepdims=True)
        dc_ref[2:3, :] = jnp.sum(dac * a, axis=0, keepdims=True)
        dc_ref[3:4, :] = jnp.sum(dac, axis=0, keepdims=True)
        dc_ref[4:8, :] = jnp.zeros((4, FF_CHUNK), F32)

    return pl.pallas_call(
        body, name="ffn_bwd", grid=(D_FF // FF_CHUNK,),
        in_specs=[_full(drb.shape), _full(hb.shape), sblk, sblk, wblk, cblk(3), cblk(1)],
        out_specs=[sblk, sblk, wblk, wblk, wblk, cblk(8)],
        out_shape=[_sds((seq, D_FF), BF16), _sds((seq, D_FF), BF16), _sds((D_FF, D_MODEL), BF16),
                   _sds((D_FF, D_MODEL), BF16), _sds((D_FF, D_MODEL), BF16), _sds((8, D_FF), F32)],
        compiler_params=_params("parallel"),
    )(drb, hb, a, u, wd, conv_w, conv_b)


def _ln1_bwd(dr2, da, du, wgt, wut, xhat, rstd, ln_g, tasks=()):
    seq = dr2.shape[0]

    def body(dr2_ref, da_ref, du_ref, wg_ref, wu_ref, xhat_ref, rstd_ref, lg_ref, dr_ref, drb_ref, dg_ref, db_ref):
        @pl.when(pl.program_id(0) == 0)
        def _():
            dg_ref[...] = jnp.zeros_like(dg_ref)
            db_ref[...] = jnp.zeros_like(db_ref)

        dh = ALPHA * dr2_ref[...] + _dot(da_ref[...], wg_ref[...]) + _dot(du_ref[...], wu_ref[...])
        xhat = xhat_ref[...]
        dg_ref[...] += jnp.sum(dh * xhat, axis=0, keepdims=True)
        db_ref[...] += jnp.sum(dh, axis=0, keepdims=True)
        dr = _ln_bwd(dh, xhat, rstd_ref[:, 0:1], lg_ref[...])
        dr_ref[...] = dr
        drb_ref[...] = dr.astype(BF16)

    vec = pl.BlockSpec((1, D_MODEL), lambda i: (0, 0))
    return _grid_call(
        body, "ln1_bwd", seq // TOK,
        in_specs=[_rows(D_MODEL), _rows(D_FF), _rows(D_FF), _full(wgt.shape), _full(wut.shape), _rows(D_MODEL),
                  _rows(LANE), _full((1, D_MODEL))],
        out_specs=[_rows(D_MODEL), _rows(D_MODEL), vec, vec],
        out_shape=[_sds((seq, D_MODEL), F32), _sds((seq, D_MODEL), BF16), _sds((1, D_MODEL), F32),
                   _sds((1, D_MODEL), F32)],
        operands=(dr2, da, du, wgt, wut, xhat, rstd, ln_g), tasks=tasks)


def _mix_bwd(drb, ya, yp, g, mixed, wout, wba, wbp, w_pool, pool_scale, tasks=()):
    seq = drb.shape[0]

    def body(dr_ref, ya_ref, yp_ref, g_ref, mixed_ref, wout_ref, wba_ref, wbp_ref, wp_ref, ps_ref,
             dzg_ref, dya_ref, dyp_ref, do_ref, dmixed_ref, dpooled_ref, dbg_ref, dps_ref):
        @pl.when(pl.program_id(0) == 0)
        def _():
            dbg_ref[...] = jnp.zeros_like(dbg_ref)
            dps_ref[...] = jnp.zeros_like(dps_ref)

        dmerged = _dot_nt(dr_ref[...], wout_ref[...])
        ga, gp = g_ref[:, :D_MODEL], g_ref[:, D_MODEL:]
        dzga = dmerged * ya_ref[...] * ga * (1.0 - ga)
        dzgp = dmerged * yp_ref[...] * gp * (1.0 - gp)
        dzg_ref[:, :D_MODEL] = dzga.astype(BF16)
        dzg_ref[:, D_MODEL:] = dzgp.astype(BF16)
        dbg_ref[:, :D_MODEL] += jnp.sum(dzga, axis=0, keepdims=True)
        dbg_ref[:, D_MODEL:] += jnp.sum(dzgp, axis=0, keepdims=True)
        dya = (dmerged * ga).astype(BF16)
        dyp = (dmerged * gp).astype(BF16)
        dya_ref[...] = dya
        dyp_ref[...] = dyp
        do_ref[...] = _dot_nt(dya, wba_ref[...])
        dypre = _dot_nt(dyp, wbp_ref[...])
        dps_ref[...] += jnp.sum(dypre * mixed_ref[...], axis=0, keepdims=True)
        dmixed = (dypre * ps_ref[...]).astype(BF16)
        dmixed_ref[...] = dmixed
        for grp in range(len(POOL_WINDOWS)):
            cols = slice(POOL_GROUP * grp, POOL_GROUP * (grp + 1))
            dpooled_ref[:, cols] = _dot_nt(dmixed[:, cols], wp_ref[grp].astype(BF16))

    return _grid_call(
        body, "mix_bwd", seq // TOK,
        in_specs=[_rows(D_MODEL), _rows(D_MODEL), _rows(D_MODEL), _rows(2 * D_MODEL), _rows(D_POOL),
                  _full(wout.shape), _full(wba.shape), _full(wbp.shape), _full(w_pool.shape), _full((1, D_POOL))],
        out_specs=[_rows(2 * D_MODEL), _rows(D_MODEL), _rows(D_MODEL), _rows(D_ATTN), _rows(D_POOL), _rows(D_POOL),
                   pl.BlockSpec((1, 2 * D_MODEL), lambda i: (0, 0)), pl.BlockSpec((1, D_POOL), lambda i: (0, 0))],
        out_shape=[_sds((seq, 2 * D_MODEL), BF16), _sds((seq, D_MODEL), BF16), _sds((seq, D_MODEL), BF16),
                   _sds((seq, D_ATTN), F32), _sds((seq, D_POOL), BF16), _sds((seq, D_POOL), F32),
                   _sds((1, 2 * D_MODEL), F32), _sds((1, D_POOL), F32)],
        operands=(drb, ya, yp, g, mixed, wout, wba, wbp, w_pool, pool_scale), tasks=tasks)


def _attn_bwd(q, k, v, bias, o, lse, do, cos, sin, tasks=()):
    seq = q.shape[0]
    nb = seq // MOBA_BLOCK
    pair = pl.BlockSpec((seq, LANE), lambda p: (0, p))
    table = pl.BlockSpec((seq, LANE), lambda p: (0, 0))
    scale = HEAD_DIM ** -0.5

    def body(q_ref, k_ref, v_ref, bias_ref, o_ref, lse_ref, do_ref, cos_ref, sin_ref, dq_ref, dk_ref, dv_ref,
             dq_acc, dk_acc, dv_acc, dk_head, dv_head, ka_sc, qa_sc, s_sc, dp_sc, p_sc, ds_sc):
        for hh in range(LANE // HEAD_DIM):
            ls = slice(HEAD_DIM * hh, HEAD_DIM * (hh + 1))
            _store_keys(ka_sc, k_ref, ls)
            vb = v_ref[:, ls]
            dk_head[...] = jnp.zeros_like(dk_head)
            dv_head[...] = jnp.zeros_like(dv_head)
            for i in range(nb):
                rs = slice(MOBA_BLOCK * i, MOBA_BLOCK * (i + 1))
                width = MOBA_BLOCK * (i + 1)
                qa_sc[:, 0:HEAD_DIM] = (q_ref[rs, ls] * scale).astype(BF16)
                qa_sc[:, HEAD_DIM:] = bias_ref[rs, ls]
                s_sc[0:width, :] = _dot_nt(ka_sc[0:width, :], qa_sc[...])
                s_sc[rs, :] = jnp.where(_causal((MOBA_BLOCK, MOBA_BLOCK), transposed=True), s_sc[rs, :], NEG)
                dob = do_ref[rs, ls]
                delta = _row_vector(jnp.sum(dob * o_ref[rs, ls], axis=1, keepdims=True))
                lse_row = lse_ref[0, hh:hh + 1, rs]
                dob16 = dob.astype(BF16)
                dp_sc[0:width, :] = _dot_nt(vb[0:width], dob16)
                for c in range(width // SCORE_CHUNK):
                    rows = slice(SCORE_CHUNK * c, SCORE_CHUNK * (c + 1))
                    p = jnp.exp(s_sc[rows, :] - lse_row)
                    p_sc[rows, :] = p.astype(BF16)
                    ds_sc[rows, :] = (p * (dp_sc[rows, :] - delta)).astype(BF16)
                dv_head[0:width, :] += _dot(p_sc[0:width, :], dob16)
                dk_head[0:width, :] += _dot(ds_sc[0:width, :], qa_sc[:, 0:HEAD_DIM])
                dq_acc[rs, ls] = _dot_tn(ds_sc[0:width, :], ka_sc[0:width, 0:HEAD_DIM]) * scale
            dk_acc[:, ls] = dk_head[...]
            dv_acc[:, ls] = dv_head[...]
        cos_t, sin_t = cos_ref[...], sin_ref[...]
        dq_ref[...] = _rope_transposed(dq_acc[...], cos_t, sin_t).astype(BF16)
        dk_ref[...] = _rope_transposed(dk_acc[...], cos_t, sin_t).astype(BF16)
        dv_ref[...] = dv_acc[...].astype(BF16)

    return _grid_call(
        body, "attn_bwd", D_ATTN // LANE,
        in_specs=[pair, pair, pair, pair, pair, pl.BlockSpec((1, 8, seq), lambda p: (p, 0, 0)), pair, table, table],
        out_specs=[pair, pair, pair], out_shape=[_sds((seq, D_ATTN), BF16)] * 3,
        operands=(q, k, v, bias, o, lse, do, cos, sin),
        scratch=[pltpu.VMEM((seq, LANE), F32)] * 3 + [pltpu.VMEM((seq, HEAD_DIM), F32)] * 2
        + [pltpu.VMEM((seq, LANE), BF16), pltpu.VMEM((MOBA_BLOCK, LANE), BF16)]
        + [pltpu.VMEM((seq, MOBA_BLOCK), F32)] * 2 + [pltpu.VMEM((seq, MOBA_BLOCK), BF16)] * 2,
        tasks=tasks)


def _in_bwd(dq, dk, dv, dpooled, dzg, dr1, win, tasks=()):
    seq = dr1.shape[0]
    nt = seq // TOK

    def body(dq_ref, dk_ref, dv_ref, dp_ref, dpnext_ref, dzg_ref, dr_ref, win_ref, dx_ref, dz_ref, ext):
        i = pl.program_id(0)
        dp = dp_ref[...]
        dpn = jnp.where(i < nt - 1, dpnext_ref[...], 0.0)
        for grp, window in enumerate(POOL_WINDOWS):
            cols = slice(POOL_GROUP * grp, POOL_GROUP * (grp + 1))
            ext[0:TOK, cols] = dp[:, cols] / _pool_count(i * TOK, TOK, window)
            ext[TOK:, cols] = dpn[:, cols] / _pool_count((i + 1) * TOK, POOL_HALO, window)
        for grp, window in enumerate(POOL_WINDOWS):
            cols = slice(POOL_GROUP * grp, POOL_GROUP * (grp + 1))
            acc = ext[0:TOK, cols] - dp[:, cols]
            for kk in range(1, window):
                acc = acc + ext[pl.ds(kk, TOK), cols]
            dz_ref[:, 3 * D_ATTN + POOL_GROUP * grp:3 * D_ATTN + POOL_GROUP * (grp + 1)] = acc.astype(BF16)
        dz_ref[:, 0:D_ATTN] = dq_ref[...]
        dz_ref[:, D_ATTN:2 * D_ATTN] = dk_ref[...]
        dz_ref[:, 2 * D_ATTN:3 * D_ATTN] = dv_ref[...]
        dz_ref[:, 3 * D_ATTN + D_POOL:] = dzg_ref[...]
        dx = ALPHA * dr_ref[...]
        for n in range(N_DEV):
            dx = dx + _dot_nt(dz_ref[:, D_ATTN * n:D_ATTN * (n + 1)], win_ref[n])
        dx_ref[...] = dx

    halo = pl.BlockSpec((POOL_HALO, D_POOL),
                        lambda i: (jnp.minimum((i + 1) * (TOK // POOL_HALO), seq // POOL_HALO - 1), 0))
    return _grid_call(
        body, "in_bwd", nt,
        in_specs=[_rows(D_ATTN), _rows(D_ATTN), _rows(D_ATTN), _rows(D_POOL), halo, _rows(2 * D_MODEL),
                  _rows(D_MODEL), _full(win.shape)],
        out_specs=[_rows(D_MODEL), _rows(D_IN_PROJ)],
        out_shape=[_sds((seq, D_MODEL), F32), _sds((seq, D_IN_PROJ), BF16)],
        operands=(dq, dk, dv, dpooled, dpooled, dzg, dr1, win),
        scratch=[pltpu.VMEM((TOK + POOL_HALO, D_POOL), F32)], tasks=tasks)


def _tn_matmul(name, a, b, out_shape, out_dtype, steps, a_spec, b_spec, o_spec, tasks=()):
    def body(a_ref, b_ref, o_ref):
        r = _dot_tn(a_ref[...].astype(BF16), b_ref[...].astype(BF16))
        o_ref[...] = r.reshape(o_ref.shape).astype(o_ref.dtype)

    (out,), results = _grid_call(body, name, steps, in_specs=[a_spec, b_spec], out_specs=[o_spec],
                                 out_shape=[_sds(out_shape, out_dtype)], operands=(a, b), tasks=tasks)
    return out, results


def _place():
    return lax.axis_index("x"), lax.axis_index("y"), lax.axis_index("c")


def _other_chips(x, y):
    return [(1 - x, y), (x, 1 - y), (1 - x, 1 - y)]


DMA_SEMS = pltpu.SemaphoreType.DMA


class _AllGather:
    def __init__(self, shards, lag=0):
        self.operands = list(shards)
        self.n = len(shards)
        self.lag = lag
        self.out_shape = [_sds((N_DEV, *s.shape), s.dtype) for s in shards]
        self.sems = [DMA_SEMS((7 * self.n,)), DMA_SEMS((7 * self.n,)), DMA_SEMS((self.n,))]

    def _copy(self, refs, a, k, block, to, from_input=False):
        ins, outs, (send_sems, recv_sems, _) = refs
        px, py, pc = block
        dst = outs[a].at[4 * px + 2 * py + pc]
        return pltpu.make_async_remote_copy(
            src_ref=ins[a] if from_input else dst, dst_ref=dst,
            send_sem=send_sems.at[7 * a + k], recv_sem=recv_sems.at[7 * a + k],
            device_id=to, device_id_type=MESH)

    def _local(self, refs, a):
        ins, outs, (_, _, local_sems) = refs
        x, y, c = _place()
        return pltpu.make_async_copy(ins[a], outs[a].at[4 * x + 2 * y + c], local_sems.at[a])

    def _pass_on(self, refs, a):
        x, y, c = _place()
        origin = ((x + 1 - c) % 2, (y + c) % 2, c)
        target = ((x + c) % 2, (y + 1 - c) % 2, c)
        return self._copy(refs, a, 3, origin, target)

    def start(self, refs):
        x, y, c = _place()
        for a in range(self.n):
            self._local(refs, a).start()
        for a in range(self.n):
            self._copy(refs, a, 0, (x, y, c), (x, y, 1 - c), True).start()
            for j, chip in enumerate(_other_chips(x, y)[:2]):
                self._copy(refs, a, 1 + j, (x, y, c), (*chip, c), True).start()

    def middle(self, refs):
        x, y, c = _place()
        me, sibling = (x, y, c), (x, y, 1 - c)
        chips = _other_chips(x, y)
        for a in range(self.n):
            for j in range(2):
                self._copy(refs, a, 1 + j, (*chips[j], c), me).wait_recv()
        for a in range(self.n):
            self._pass_on(refs, a).start()
            for j in range(2):
                self._copy(refs, a, 4 + j, (*chips[j], c), sibling).start()

    def late(self, refs):
        x, y, c = _place()
        diagonal = (1 - x, 1 - y, c)
        for a in range(self.n):
            self._copy(refs, a, 3, diagonal, (x, y, c)).wait_recv()
            self._copy(refs, a, 6, diagonal, (x, y, 1 - c)).start()

    def finish(self, refs):
        x, y, c = _place()
        me, sibling = (x, y, c), (x, y, 1 - c)
        chips = _other_chips(x, y)
        for a in range(self.n):
            self._copy(refs, a, 0, sibling, me).wait_recv()
            for j, chip in enumerate(chips):
                self._copy(refs, a, 4 + j, (*chip, 1 - c), me).wait_recv()
        for a in range(self.n):
            self._copy(refs, a, 0, me, sibling, True).wait_send()
            for j, chip in enumerate(chips[:2]):
                self._copy(refs, a, 1 + j, me, (*chip, c), True).wait_send()
            self._pass_on(refs, a).wait_send()
            for j, chip in enumerate(chips):
                self._copy(refs, a, 4 + j, (*chip, c), sibling).wait_send()
            self._local(refs, a).wait()


class _SiblingSend:
    def __init__(self, partials):
        self.operands = list(partials)
        self.n = len(partials)
        self.out_shape = [_sds((4, *p.shape[1:]), p.dtype) for p in partials]
        self.sems = [DMA_SEMS((4 * self.n,)), DMA_SEMS((4 * self.n,))]

    def _copy(self, refs, a, q):
        ins, outs, (send_sems, recv_sems) = refs
        x, y, c = _place()
        return pltpu.make_async_remote_copy(
            src_ref=ins[a].at[2 * q + 1 - c], dst_ref=outs[a].at[q],
            send_sem=send_sems.at[4 * a + q], recv_sem=recv_sems.at[4 * a + q],
            device_id=(x, y, 1 - c), device_id_type=MESH)

    def start(self, refs):
        for a in range(self.n):
            for q in range(4):
                self._copy(refs, a, q).start()

    def middle(self, refs):
        pass

    def finish(self, refs):
        for a in range(self.n):
            for q in range(4):
                self._copy(refs, a, q).wait()


class _ChipScatter:
    def __init__(self, chip_partials):
        self.operands = list(chip_partials)
        self.n = len(chip_partials)
        self.out_shape = [_sds(p.shape, p.dtype) for p in chip_partials]
        self.sems = [DMA_SEMS((3 * self.n,)), DMA_SEMS((3 * self.n,)), DMA_SEMS((self.n,))]

    def _copy(self, refs, a, k, arrival=False):
        ins, outs, (send_sems, recv_sems, _) = refs
        x, y, c = _place()
        px, py = _other_chips(x, y)[k]
        mine, theirs = 2 * x + y, 2 * px + py
        return pltpu.make_async_remote_copy(
            src_ref=ins[a].at[mine if arrival else theirs], dst_ref=outs[a].at[theirs if arrival else mine],
            send_sem=send_sems.at[3 * a + k], recv_sem=recv_sems.at[3 * a + k],
            device_id=(px, py, c), device_id_type=MESH)

    def _local(self, refs, a):
        ins, outs, (_, _, local_sems) = refs
        x, y, _ = _place()
        return pltpu.make_async_copy(ins[a].at[2 * x + y], outs[a].at[2 * x + y], local_sems.at[a])

    def start(self, refs):
        for a in range(self.n):
            self._local(refs, a).start()
            for k in range(3):
                self._copy(refs, a, k).start()

    def middle(self, refs):
        pass

    def finish(self, refs):
        for a in range(self.n):
            for k in range(3):
                self._copy(refs, a, k, arrival=True).wait_recv()
        for a in range(self.n):
            for k in range(3):
                self._copy(refs, a, k).wait_send()
            self._local(refs, a).wait()


class _DirectScatter:
    def __init__(self, partials):
        self.operands = list(partials)
        self.n = len(partials)
        self.out_shape = [_sds(p.shape, p.dtype) for p in partials]
        self.sems = [DMA_SEMS((7 * self.n,)), DMA_SEMS((7 * self.n,)), DMA_SEMS((self.n,))]

    def _copy(self, refs, a, k, arrival=False):
        ins, outs, (send_sems, recv_sems, _) = refs
        x, y, c = _place()
        peer = [(x, y, 1 - c), (1 - x, y, c), (x, 1 - y, c), (1 - x, 1 - y, c),
                (1 - x, y, 1 - c), (x, 1 - y, 1 - c), (1 - x, 1 - y, 1 - c)][k]
        mine, theirs = 4 * x + 2 * y + c, 4 * peer[0] + 2 * peer[1] + peer[2]
        return pltpu.make_async_remote_copy(
            src_ref=ins[a].at[mine if arrival else theirs], dst_ref=outs[a].at[theirs if arrival else mine],
            send_sem=send_sems.at[7 * a + k], recv_sem=recv_sems.at[7 * a + k],
            device_id=peer, device_id_type=MESH)

    def _local(self, refs, a):
        ins, outs, (_, _, local_sems) = refs
        x, y, c = _place()
        return pltpu.make_async_copy(ins[a].at[4 * x + 2 * y + c], outs[a].at[4 * x + 2 * y + c], local_sems.at[a])

    def start(self, refs):
        for a in range(self.n):
            self._local(refs, a).start()
            for k in range(7):
                self._copy(refs, a, k).start()

    def middle(self, refs):
        pass

    def finish(self, refs):
        for a in range(self.n):
            for k in range(7):
                self._copy(refs, a, k, arrival=True).wait_recv()
        for a in range(self.n):
            for k in range(7):
                self._copy(refs, a, k).wait_send()
            self._local(refs, a).wait()


def _task_args(tasks):
    hbm = pl.BlockSpec(memory_space=pl.ANY)
    operands = [o for t in tasks for o in t.operands]
    out_shape = [s for t in tasks for s in t.out_shape]
    sems = [s for t in tasks for s in t.sems]
    return operands, [hbm] * len(operands), out_shape, [hbm] * len(out_shape), sems


def _task_refs(tasks, ins, outs, sems):
    per_task = []
    for t in tasks:
        ni, no, ns = len(t.operands), len(t.out_shape), len(t.sems)
        per_task.append((ins[:ni], outs[:no], sems[:ns]))
        ins, outs, sems = ins[ni:], outs[no:], sems[ns:]
    return per_task


def _task_results(tasks, outs):
    res = []
    for t in tasks:
        res.append(list(outs[:len(t.out_shape)]))
        outs = outs[len(t.out_shape):]
    return res


def _carry(body, tasks, n_in, n_out, n_scratch, steps):
    if not tasks:
        return body
    t_in = sum(len(t.operands) for t in tasks)
    t_out = sum(len(t.out_shape) for t in tasks)

    def wrapped(*refs):
        ins, refs = refs[:n_in], refs[n_in:]
        t_ins, refs = refs[:t_in], refs[t_in:]
        outs, refs = refs[:n_out], refs[n_out:]
        t_outs, refs = refs[:t_out], refs[t_out:]
        scratch, t_sems = refs[:n_scratch], refs[n_scratch:]
        per_task = _task_refs(tasks, t_ins, t_outs, t_sems)
        step = pl.program_id(0)

        @pl.when(step == 0)
        def _():
            for t, r in zip(tasks, per_task):
                t.start(r)

        for t, r in zip(tasks, per_task):
            pl.when(step == max(steps - 1 - getattr(t, "lag", 0), 0))(functools.partial(t.middle, r))
            if hasattr(t, "late"):
                pl.when(step == steps - 1)(functools.partial(t.late, r))

        body(*ins, *outs, *scratch)

        @pl.when(step == steps - 1)
        def _():
            for t, r in zip(tasks, per_task):
                t.finish(r)

    return wrapped


def _exchange(name, tasks):
    operands, in_specs, out_shape, out_specs, sems = _task_args(tasks)

    def body(*refs):
        ni, no = len(operands), len(out_shape)
        per_task = _task_refs(tasks, refs[:ni], refs[ni:ni + no], refs[ni + no:])
        for phase in ("start", "middle", "late", "finish"):
            for t, r in zip(tasks, per_task):
                if hasattr(t, phase):
                    getattr(t, phase)(r)

    outs = pl.pallas_call(body, name=name, in_specs=in_specs, out_specs=out_specs, out_shape=out_shape,
                          scratch_shapes=sems)(*operands)
    return _task_results(tasks, outs)


def _row_tile(rows, cols, whole_up_to=256 * 1024):
    if rows * cols <= whole_up_to:
        return rows
    for t in (256, 176, 128, 64, 32, 16, 8):
        if rows % t == 0:
            return t
    return rows


def _pair_sum(name, partials, from_sibling):
    n = len(partials)
    _, rows, cols = partials[0].shape
    tile = _row_tile(rows, cols, 512 * 1024)

    def body(*refs):
        south = lax.axis_index("c") == 0
        for p_ref, s_ref, o_ref in zip(refs[:n], refs[n:2 * n], refs[2 * n:]):
            mine = jnp.where(south, p_ref[0, 0].astype(F32), p_ref[0, 1].astype(F32))
            o_ref[0] = (mine + s_ref[0].astype(F32)).astype(o_ref.dtype)

    blk = pl.BlockSpec((1, tile, cols), lambda q, i: (q, i, 0))
    return pl.pallas_call(
        body, name=name, grid=(4, rows // tile),
        in_specs=[pl.BlockSpec((1, 2, tile, cols), lambda q, i: (q, 0, i, 0))] * n + [blk] * n,
        out_specs=[blk] * n, out_shape=[_sds(s.shape, s.dtype) for s in from_sibling],
        compiler_params=_params("parallel", "parallel"),
    )(*[p.reshape(4, 2, rows, cols) for p in partials], *from_sibling)


def _sum_leading(name, stacked):
    parts, rows, cols = stacked.shape
    tile = _row_tile(rows, cols, (512 if parts <= 4 else 256) * 1024)

    def body(s_ref, o_ref):
        acc = s_ref[0].astype(F32)
        for d in range(1, parts):
            acc = acc + s_ref[d].astype(F32)
        o_ref[...] = acc

    return pl.pallas_call(
        body, name=name, grid=(rows // tile,),
        in_specs=[pl.BlockSpec((parts, tile, cols), lambda i: (0, i, 0))],
        out_specs=pl.BlockSpec((tile, cols), lambda i: (i, 0)),
        out_shape=_sds((rows, cols), F32),
        compiler_params=_params("parallel"),
    )(stacked)


def _adamw_math(w, g, m, v):
    nm = ADAM_B1 * m + (1.0 - ADAM_B1) * g
    nv = ADAM_B2 * v + (1.0 - ADAM_B2) * (g * g)
    m_hat = nm / (1.0 - ADAM_B1 ** ADAM_STEP)
    v_hat = nv / (1.0 - ADAM_B2 ** ADAM_STEP)
    return -ADAM_LR * (m_hat / (jnp.sqrt(v_hat) + ADAM_EPS) + ADAM_WD * w), nm, nv


def _adamw(name, w, g, m, v):
    rows, cols = w.shape
    tile = _row_tile(rows, cols)

    def body(w_ref, g_ref, m_ref, v_ref, d_ref, nm_ref, nv_ref):
        d_ref[...], nm_ref[...], nv_ref[...] = _adamw_math(w_ref[...], g_ref[...], m_ref[...], v_ref[...])

    blk = pl.BlockSpec((tile, cols), lambda i: (i, 0))
    return pl.pallas_call(
        body, name=name, grid=(rows // tile,),
        in_specs=[blk] * 4, out_specs=[blk] * 3,
        out_shape=[_sds((rows, cols), F32)] * 3,
        compiler_params=_params("parallel"),
    )(w, g, m, v)


def _sum_adamw(name, stacked, w, m, v):
    parts, rows, cols = stacked.shape
    tile = _row_tile(rows, cols)

    def body(s_ref, w_ref, m_ref, v_ref, g_ref, d_ref, nm_ref, nv_ref):
        g = s_ref[0].astype(F32)
        for d in range(1, parts):
            g = g + s_ref[d].astype(F32)
        g_ref[...] = g
        d_ref[...], nm_ref[...], nv_ref[...] = _adamw_math(w_ref[...], g, m_ref[...], v_ref[...])

    blk = pl.BlockSpec((tile, cols), lambda i: (i, 0))
    return pl.pallas_call(
        body, name=name, grid=(rows // tile,),
        in_specs=[pl.BlockSpec((parts, tile, cols), lambda i: (0, i, 0))] + [blk] * 3, out_specs=[blk] * 4,
        out_shape=[_sds((rows, cols), F32)] * 4,
        compiler_params=_params("parallel"),
    )(stacked, w, m, v)


SMALL = ("b_gate", "w_pool", "pool_scale", "ln1_g", "ln1_b", "conv_b", "ln2_g", "ln2_b")
TILE = 8 * LANE


def _pack(parts):
    tiles = []
    for p in parts:
        flat = p.reshape(-1)
        tiles.append(jnp.pad(flat, (0, -flat.size % TILE)).reshape(-1, LANE))
    return jnp.concatenate(tiles, axis=0)


def _unpack(packed, shapes):
    out, at = [], 0
    for shape in shapes:
        size = math.prod(shape)
        rows = -(-size // TILE) * 8
        out.append(packed[at:at + rows].reshape(-1)[:size].reshape(shape))
        at += rows
    return out


MIXER = ("w_branch_attn", "w_branch_pool", "w_out", "conv_w")
FFN = ("w_ffn_gate_t", "w_ffn_up_t", "w_ffn_down")


def _columns(t):
    return jnp.transpose(t, (1, 0, 2)).reshape(t.shape[1], N_DEV * t.shape[2])


def _row_blocks(t):
    return t.reshape(N_DEV * t.shape[1], t.shape[2])


def _by_owner(t):
    return t.reshape(N_DEV, t.shape[0] // N_DEV, t.shape[1])


def _reduce_halves(names, partials, from_sibling):
    out = [None] * len(names)
    for shape in dict.fromkeys(p.shape for p in partials):
        group = [i for i, p in enumerate(partials) if p.shape == shape]
        sums = _pair_sum("pair_sum_" + names[group[0]], [partials[i] for i in group], [from_sibling[i] for i in group])
        for i, s in zip(group, sums):
            out[i] = s
    return out


def _local_step(x, target, shards, small):
    seq = x.shape[0]
    cos, sin = _rope_tables(seq)
    whole = lambda width: pl.BlockSpec((seq, width), lambda *_: (0, 0))
    ((w_in_all,),) = _exchange("gather_w_in", [_AllGather([shards["w_in"]])])
    (xb, q, k, v, u, g, kmean), (mixer,) = _proj_in(
        x, w_in_all, small["b_gate"], cos, sin, tasks=[_AllGather([shards[n] for n in MIXER], lag=2)])
    wba, wbp, wout, conv_w = _columns(mixer[0]), _columns(mixer[1]), _row_blocks(mixer[2]), _columns(mixer[3])
    (o, lse, bias), ((wgt, wut),) = _attn_fwd(
        q, k, v, kmean.reshape(seq // MOBA_BLOCK, D_ATTN),
        tasks=[_AllGather([shards["w_ffn_gate_t"], shards["w_ffn_up_t"]], lag=1)])
    (ya, yp, pooled, mixed, ypre, merged, xhat1, rstd1, h1, h1b), _ = _mix(
        o, u, g, x, wba, wbp, wout, small["w_pool"], small["pool_scale"], small["ln1_g"], small["ln1_b"])
    wgt, wut = _row_blocks(wgt), _row_blocks(wut)
    (a, uf, act), ((wd,),) = _ffn_up(
        h1b, wgt, wut, conv_w, small["conv_b"], tasks=[_AllGather([shards["w_ffn_down"]], lag=4)])
    wd = _row_blocks(wd)
    dr2, dr2b, loss, dg2, db2 = _ffn_down(act, wd, h1, target, small["ln2_g"], small["ln2_b"])

    da, du, dwd, dwg, dwu, dconv = _ffn_bwd(dr2b, h1b, a, uf, wd, conv_w, small["conv_b"])
    ffn_partials = [_by_owner(dwg), _by_owner(dwu), _by_owner(dwd)]
    (dr1, dr1b, dg1, db1), (ffn_sibling,) = _ln1_bwd(
        dr2, da, du, wgt, wut, xhat1, rstd1, small["ln1_g"], tasks=[_SiblingSend(ffn_partials)])
    ffn_chip = _reduce_halves(FFN, ffn_partials, ffn_sibling)
    (dzg, dya, dyp, do, dmixed, dpooled, dbg, dps), (gate_landed,) = _mix_bwd(
        dr1b, ya, yp, g, mixed, wout, wba, wbp, small["w_pool"], small["pool_scale"],
        tasks=[_ChipScatter(ffn_chip[0:1])])
    dw_out, _ = _tn_matmul(
        "dw_out", merged, dr1b, (D_MODEL, D_MODEL), BF16, 4,
        pl.BlockSpec((seq, 256), lambda m: (0, m)), whole(D_MODEL), pl.BlockSpec((256, D_MODEL), lambda m: (m, 0)))
    dw_ba, _ = _tn_matmul(
        "dw_branch_attn", o, dya, (N_DEV, D_ATTN, LANE), BF16, N_DEV,
        whole(D_ATTN), pl.BlockSpec((seq, LANE), lambda n: (0, n)), pl.BlockSpec((1, D_ATTN, LANE), lambda n: (n, 0, 0)))
    dw_bp, _ = _tn_matmul(
        "dw_branch_pool", ypre, dyp, (N_DEV, D_POOL, LANE), BF16, N_DEV,
        whole(D_POOL), pl.BlockSpec((seq, LANE), lambda n: (0, n)), pl.BlockSpec((1, D_POOL, LANE), lambda n: (n, 0, 0)))
    dw_pool, _ = _tn_matmul(
        "dw_pool", pooled, dmixed, (len(POOL_WINDOWS), POOL_GROUP, POOL_GROUP), F32, len(POOL_WINDOWS),
        pl.BlockSpec((seq, POOL_GROUP), lambda n: (0, n)), pl.BlockSpec((seq, POOL_GROUP), lambda n: (0, n)),
        pl.BlockSpec((1, POOL_GROUP, POOL_GROUP), lambda n: (n, 0, 0)))
    mixer_partials = [dw_ba, dw_bp, _by_owner(dw_out)]
    (dq, dk, dv), (up_down_landed, mixer_sibling) = _attn_bwd(
        q, k, v, bias, o, lse, do, cos, sin, tasks=[_ChipScatter(ffn_chip[1:3]), _SiblingSend(mixer_partials)])
    mixer_chip = _reduce_halves(MIXER[:3], mixer_partials, mixer_sibling)
    (grad_x, dz), _ = _in_bwd(dq, dk, dv, dpooled, dzg, dr1, w_in_all)
    dw_in, (mixer_landed,) = _tn_matmul(
        "dw_in", xb, dz, (N_DEV, D_MODEL, D_ATTN), BF16, 2 * N_DEV,
        pl.BlockSpec((seq, 512), lambda s: (0, s % 2)), pl.BlockSpec((seq, D_ATTN), lambda s: (0, s // 2)),
        pl.BlockSpec((1, 512, D_ATTN), lambda s: (s // 2, s % 2, 0)), tasks=[_ChipScatter(mixer_chip)])

    landed = dict(zip(FFN + MIXER[:3], gate_landed + up_down_landed + mixer_landed))
    little = {"b_gate": dbg, "w_pool": dw_pool, "pool_scale": dps, "ln1_g": dg1, "ln1_b": db1, "conv_b": dconv[3:4],
              "ln2_g": dg2, "ln2_b": db2, "conv_w": dconv[0:3], "loss": loss}
    return grad_x, landed, dw_in, little


def kernel(x, w_in, b_gate, w_branch_attn, w_pool, pool_scale, w_branch_pool, w_out, ln1_g, ln1_b, w_ffn_gate, w_ffn_up, conv_w, conv_b, w_ffn_down, ln2_g, ln2_b, loss_target, m_w_in, m_b_gate, m_w_branch_attn, m_w_pool, m_pool_scale, m_w_branch_pool, m_w_out, m_ln1_g, m_ln1_b, m_w_ffn_gate, m_w_ffn_up, m_conv_w, m_conv_b, m_w_ffn_down, m_ln2_g, m_ln2_b, v_w_in, v_b_gate, v_w_branch_attn, v_w_pool, v_pool_scale, v_w_branch_pool, v_w_out, v_ln1_g, v_ln1_b, v_w_ffn_gate, v_w_ffn_up, v_conv_w, v_conv_b, v_w_ffn_down, v_ln2_g, v_ln2_b):
    me = 4 * lax.axis_index("x") + 2 * lax.axis_index("y") + lax.axis_index("c")
    weights = dict(w_in=w_in, b_gate=b_gate, w_branch_attn=w_branch_attn, w_pool=w_pool, pool_scale=pool_scale,
                   w_branch_pool=w_branch_pool, w_out=w_out, ln1_g=ln1_g, ln1_b=ln1_b, w_ffn_gate=w_ffn_gate,
                   w_ffn_up=w_ffn_up, conv_w=conv_w, conv_b=conv_b, w_ffn_down=w_ffn_down, ln2_g=ln2_g, ln2_b=ln2_b)
    m_in = dict(w_in=m_w_in, b_gate=m_b_gate, w_branch_attn=m_w_branch_attn, w_pool=m_w_pool,
                pool_scale=m_pool_scale, w_branch_pool=m_w_branch_pool, w_out=m_w_out, ln1_g=m_ln1_g, ln1_b=m_ln1_b,
                w_ffn_gate=m_w_ffn_gate, w_ffn_up=m_w_ffn_up, conv_w=m_conv_w, conv_b=m_conv_b,
                w_ffn_down=m_w_ffn_down, ln2_g=m_ln2_g, ln2_b=m_ln2_b)
    v_in = dict(w_in=v_w_in, b_gate=v_b_gate, w_branch_attn=v_w_branch_attn, w_pool=v_w_pool,
                pool_scale=v_pool_scale, w_branch_pool=v_w_branch_pool, w_out=v_w_out, ln1_g=v_ln1_g, ln1_b=v_ln1_b,
                w_ffn_gate=v_w_ffn_gate, w_ffn_up=v_w_ffn_up, conv_w=v_conv_w, conv_b=v_conv_b,
                w_ffn_down=v_w_ffn_down, ln2_g=v_ln2_g, ln2_b=v_ln2_b)
    weights = {n: a[0] for n, a in weights.items()}
    m_in = {n: a[0] for n, a in m_in.items()}
    v_in = {n: a[0] for n, a in v_in.items()}

    shards = {"w_in": weights["w_in"].astype(BF16), "w_branch_attn": weights["w_branch_attn"].astype(BF16),
              "w_branch_pool": weights["w_branch_pool"].astype(BF16), "w_out": weights["w_out"].astype(BF16),
              "w_ffn_gate_t": weights["w_ffn_gate"].T.astype(BF16), "w_ffn_up_t": weights["w_ffn_up"].T.astype(BF16),
              "w_ffn_down": weights["w_ffn_down"].astype(BF16), "conv_w": weights["conv_w"]}
    small = {"b_gate": weights["b_gate"][None], "w_pool": weights["w_pool"], "pool_scale": weights["pool_scale"][None],
             "ln1_g": weights["ln1_g"][None], "ln1_b": weights["ln1_b"][None], "conv_b": weights["conv_b"][None],
             "ln2_g": weights["ln2_g"][None], "ln2_b": weights["ln2_b"][None]}

    grad_x, landed, dw_in, little = _local_step(x[0], loss_target[0], shards, small)

    ((w_in_sibling,),) = _exchange("sibling_grads", [_SiblingSend([dw_in])])
    w_in_chip = _reduce_halves(["w_in"], [dw_in], [w_in_sibling])
    names = SMALL + ("conv_w",)
    (landed["w_in"],), (all_small,) = _exchange(
        "scatter_grads", [_ChipScatter(w_in_chip), _AllGather([_pack([little[n] for n in names + ("loss",)])])])

    grads, delta, new_m, new_v = {}, {}, {}, {}
    for n in ("w_in", "w_branch_attn", "w_branch_pool", "w_out", "w_ffn_down"):
        grads[n], delta[n], new_m[n], new_v[n] = _sum_adamw("update_" + n, landed[n], weights[n], m_in[n], v_in[n])
    for n in ("w_ffn_gate", "w_ffn_up"):
        updated = _sum_adamw("update_" + n, landed[n + "_t"], weights[n].T, m_in[n].T, v_in[n].T)
        grads[n], delta[n], new_m[n], new_v[n] = (t.T for t in updated)
    small_sum = _sum_leading("sum_small", all_small)
    *small_grads, conv_w_grad, loss = _unpack(
        small_sum, [weights[n].shape for n in SMALL] + [(3, D_FF), little["loss"].shape])
    loss = loss[0, 0]
    grads.update(zip(SMALL, small_grads))
    grads["conv_w"] = lax.dynamic_slice(conv_w_grad, (0, me * FF_SHARD), (3, FF_SHARD))
    flat = lambda d: _pack([d[n] for n in names])
    shapes = [weights[n].shape for n in names]
    for out, packed in zip((delta, new_m, new_v),
                           _adamw("adamw_small", flat(weights), flat(grads), flat(m_in), flat(v_in))):
        out.update(zip(names, _unpack(packed, shapes)))

    order = ("w_in", "b_gate", "w_branch_attn", "w_pool", "pool_scale", "w_branch_pool", "w_out", "ln1_g", "ln1_b",
             "w_ffn_gate", "w_ffn_up", "conv_w", "conv_b", "w_ffn_down", "ln2_g", "ln2_b")
    lead = lambda t: t[None]
    return (loss, lead(grad_x), *[lead(grads[n]) for n in order], *[lead(delta[n]) for n in order],
            *[lead(new_m[n]) for n in order], *[lead(new_v[n]) for n in order])
```

```python
import functools
import math

import jax
import jax.numpy as jnp
from jax import lax
from jax.experimental import pallas as pl
from jax.experimental.pallas import tpu as pltpu

F32 = jnp.float32
BF16 = jnp.bfloat16

D_MODEL = 1024
N_HEADS = 8
HEAD_DIM = 64
D_ATTN = N_HEADS * HEAD_DIM
MOBA_BLOCK = 256
MOBA_TOPK = 3
ROPE_THETA = 10000.0
POOL_WINDOWS = (2, 4, 8, 16)
POOL_GROUP = 128
D_POOL = len(POOL_WINDOWS) * POOL_GROUP
POOL_HALO = 16
D_FF = 2816
D_IN_PROJ = 3 * D_ATTN + D_POOL + 2 * D_MODEL
LN_EPS = 1e-5
ALPHA = 2.0 ** 0.25
NEG = -1e30
N_DEV = 8
FF_SHARD = D_FF // N_DEV

ADAM_LR = 0.001
ADAM_B1 = 0.9
ADAM_B2 = 0.999
ADAM_EPS = 1e-08
ADAM_WD = 0.01
ADAM_STEP = 10

TOK = 512
FF_CHUNK = 256
LANE = 128
VMEM_LIMIT = 56 * 1024 * 1024

MESH = pl.DeviceIdType.MESH
NT_DIMS = (((1,), (1,)), ((), ()))
TN_DIMS = (((0,), (0,)), ((), ()))


def _params(*sem):
    return pltpu.CompilerParams(dimension_semantics=sem or None, vmem_limit_bytes=VMEM_LIMIT)


def _full(shape):
    zeros = (0,) * len(shape)
    return pl.BlockSpec(shape, lambda *_: zeros, pipeline_mode=pl.Buffered(1))


def _rows(width, tile=TOK):
    return pl.BlockSpec((tile, width), lambda i: (i, 0))


def _sds(shape, dtype):
    return jax.ShapeDtypeStruct(shape, dtype)


def _dot(a, b):
    return jnp.dot(a, b, preferred_element_type=F32)


def _dot_nt(a, b):
    return lax.dot_general(a, b, NT_DIMS, preferred_element_type=F32)


def _dot_tn(a, b):
    return lax.dot_general(a, b, TN_DIMS, preferred_element_type=F32)


def _rope_tables(seq):
    half = HEAD_DIM // 2
    inv_freq = 1.0 / (ROPE_THETA ** (jnp.arange(half, dtype=F32) / half))
    ang = jnp.arange(seq, dtype=F32)[:, None] * inv_freq[None, :]
    cos, sin = jnp.cos(ang), jnp.sin(ang)
    return jnp.tile(cos, (1, 4)), jnp.tile(jnp.concatenate([-sin, sin], axis=1), (1, 2))


def _swap_halves(t):
    lane = lax.broadcasted_iota(jnp.int32, t.shape, 1)
    return jnp.where((lane % HEAD_DIM) < HEAD_DIM // 2, pltpu.roll(t, LANE - 32, 1), pltpu.roll(t, 32, 1))


def _rope(t, cos, sin):
    return t * cos + _swap_halves(t) * sin


def _rope_transposed(g, cos, sin):
    return g * cos + _swap_halves(g * sin)


def _ln_fwd(r, g, b):
    mu = jnp.mean(r, axis=-1, keepdims=True)
    xc = r - mu
    var = jnp.mean(xc * xc, axis=-1, keepdims=True)
    rstd = lax.rsqrt(var + LN_EPS)
    xhat = xc * rstd
    return xhat * g + b, xhat, rstd


def _ln_bwd(dy, xhat, rstd, g):
    dxh = dy * g
    m1 = jnp.mean(dxh, axis=-1, keepdims=True)
    m2 = jnp.mean(dxh * xhat, axis=-1, keepdims=True)
    return rstd * (dxh - m1 - xhat * m2)


def _normal_cdf(a):
    return 0.5 * (1.0 + lax.erf(a * (1.0 / math.sqrt(2.0))))


def _gelu_derivative(a, cdf):
    return cdf + a * (jnp.exp(-0.5 * a * a) * (1.0 / math.sqrt(2.0 * math.pi)))


def _shift_down(a, k):
    row = lax.broadcasted_iota(jnp.int32, a.shape, 0)
    return jnp.where(row >= k, pltpu.roll(a, k, 0), 0.0)


def _shift_up(a, k):
    n = a.shape[0]
    row = lax.broadcasted_iota(jnp.int32, a.shape, 0)
    return jnp.where(row < n - k, pltpu.roll(a, n - k, 0), 0.0)


def _conv(a, cw, cb):
    return cw[2:3, :] * a + cw[1:2, :] * _shift_down(a, 1) + cw[0:1, :] * _shift_down(a, 2) + cb


def _pool_count(first_row, rows, window):
    t = first_row + lax.broadcasted_iota(jnp.int32, (rows, 1), 0)
    return jnp.minimum(t + 1, window).astype(F32)


def _grid_call(body, name, steps, in_specs, out_specs, out_shape, operands, scratch=(), tasks=()):
    t_operands, t_in_specs, t_out_shape, t_out_specs, t_sems = _task_args(tasks)
    outs = pl.pallas_call(
        _carry(body, tasks, len(in_specs), len(out_specs), len(scratch), steps), name=name, grid=(steps,),
        in_specs=list(in_specs) + t_in_specs, out_specs=list(out_specs) + t_out_specs,
        out_shape=list(out_shape) + t_out_shape, scratch_shapes=list(scratch) + t_sems,
        compiler_params=_params("arbitrary"),
    )(*operands, *t_operands)
    return outs[:len(out_specs)], _task_results(tasks, outs[len(out_specs):])


def _proj_in(x, win, b_gate, cos, sin, tasks=()):
    seq = x.shape[0]
    nt = seq // TOK

    def body(x_ref, win_ref, bg_ref, cos_ref, sin_ref, xb_ref, q_ref, k_ref, v_ref, u_ref, g_ref, km_ref):
        xb = x_ref[...].astype(BF16)
        xb_ref[...] = xb
        cos_t, sin_t = cos_ref[...], sin_ref[...]
        for sec, out_ref in ((0, q_ref), (1, k_ref)):
            z = _dot(xb, win_ref[sec])
            for c in range(D_ATTN // LANE):
                cols = slice(LANE * c, LANE * (c + 1))
                out_ref[:, cols] = _rope(z[:, cols], cos_t, sin_t)
        for b in range(TOK // MOBA_BLOCK):
            km_ref[b] = jnp.mean(k_ref[MOBA_BLOCK * b:MOBA_BLOCK * (b + 1), :], axis=0, keepdims=True)
        v_ref[...] = _dot(xb, win_ref[2]).astype(BF16)
        u_ref[...] = _dot(xb, win_ref[3])
        for n in range(4):
            cols = slice(D_ATTN * n, D_ATTN * (n + 1))
            g_ref[:, cols] = jax.nn.sigmoid(_dot(xb, win_ref[4 + n]) + bg_ref[:, cols])

    return _grid_call(
        body, "proj_in", nt,
        in_specs=[_rows(D_MODEL), _full(win.shape), _full((1, 2 * D_MODEL)), _rows(LANE), _rows(LANE)],
        out_specs=[_rows(D_MODEL), _rows(D_ATTN), _rows(D_ATTN), _rows(D_ATTN), _rows(D_POOL), _rows(2 * D_MODEL),
                   pl.BlockSpec((TOK // MOBA_BLOCK, 1, D_ATTN), lambda i: (i, 0, 0))],
        out_shape=[_sds((seq, D_MODEL), BF16), _sds((seq, D_ATTN), F32), _sds((seq, D_ATTN), F32),
                   _sds((seq, D_ATTN), BF16), _sds((seq, D_POOL), F32), _sds((seq, 2 * D_MODEL), F32),
                   _sds((seq // MOBA_BLOCK, 1, D_ATTN), F32)],
        operands=(x, win, b_gate, cos, sin), tasks=tasks)


SCORE_CHUNK = 128


def _store_keys(ka_sc, k_ref, ls):
    seq = ka_sc.shape[0]
    ka_sc[:, 0:HEAD_DIM] = k_ref[:, ls].astype(BF16)
    row = lax.broadcasted_iota(jnp.int32, (seq, HEAD_DIM), 0)
    lane = lax.broadcasted_iota(jnp.int32, (seq, HEAD_DIM), 1)
    in_block = (lane * MOBA_BLOCK <= row) & (row < (lane + 1) * MOBA_BLOCK)
    ka_sc[:, HEAD_DIM:] = jnp.where(in_block, 1.0, 0.0).astype(BF16)


def _block_bias(qf, km, i):
    if i <= MOBA_TOPK:
        return jnp.zeros((MOBA_BLOCK, HEAD_DIM), BF16)
    nb = km.shape[0]
    gate = lax.dot_general(km, qf, NT_DIMS, precision=lax.Precision.HIGHEST, preferred_element_type=F32)
    blk = lax.broadcasted_iota(jnp.int32, gate.shape, 0)
    rank = jnp.zeros(gate.shape, F32)
    for r in range(1, i):
        lower = pltpu.roll(gate, r, 0)
        rank = rank + jnp.where((blk >= r) & (lower >= gate), 1.0, 0.0)
        higher = pltpu.roll(gate, nb - r, 0)
        rank = rank + jnp.where((blk + r < i) & (higher > gate), 1.0, 0.0)
    bias = jnp.where((blk < i) & (rank >= MOBA_TOPK), NEG, 0.0)
    padded = jnp.concatenate([bias, jnp.zeros((LANE - nb, MOBA_BLOCK), F32)], axis=0)
    return jnp.transpose(padded)[:, 0:HEAD_DIM].astype(BF16)


def _causal(shape, transposed=False):
    row = lax.broadcasted_iota(jnp.int32, shape, 0)
    col = lax.broadcasted_iota(jnp.int32, shape, 1)
    return (row <= col) if transposed else (col <= row)


def _row_vector(col):
    return jnp.transpose(jnp.broadcast_to(col, (MOBA_BLOCK, LANE)))[0:1, :]


def _attn_fwd(q, k, v, kmean, tasks=()):
    seq = q.shape[0]
    nb = seq // MOBA_BLOCK
    assert nb == 8, "the block ranking keeps one sublane per key block"
    pair = pl.BlockSpec((seq, LANE), lambda p: (0, p))
    heads = LANE // HEAD_DIM

    def body(q_ref, k_ref, v_ref, km_ref, o_ref, lse_ref, bias_ref, ka_sc, qa_sc, s_sc, p_sc):
        lse_ref[0, heads:, :] = jnp.zeros((8 - heads, seq), F32)
        for hh in range(heads):
            ls = slice(HEAD_DIM * hh, HEAD_DIM * (hh + 1))
            _store_keys(ka_sc, k_ref, ls)
            vb = v_ref[:, ls]
            km = km_ref[:, ls]
            for i in range(nb):
                rs = slice(MOBA_BLOCK * i, MOBA_BLOCK * (i + 1))
                width = MOBA_BLOCK * (i + 1)
                qf = q_ref[rs, ls]
                bias = _block_bias(qf, km, i)
                bias_ref[rs, ls] = bias
                qa_sc[:, 0:HEAD_DIM] = (qf * HEAD_DIM ** -0.5).astype(BF16)
                qa_sc[:, HEAD_DIM:] = bias
                s_sc[:, 0:width] = _dot_nt(qa_sc[...], ka_sc[0:width, :])
                s_sc[:, rs] = jnp.where(_causal((MOBA_BLOCK, MOBA_BLOCK)), s_sc[:, rs], NEG)
                chunks = [slice(SCORE_CHUNK * c, SCORE_CHUNK * (c + 1)) for c in range(width // SCORE_CHUNK)]
                top = s_sc[:, chunks[0]]
                for c in chunks[1:]:
                    top = jnp.maximum(top, s_sc[:, c])
                m = jnp.max(top, axis=1, keepdims=True)
                total = jnp.zeros((MOBA_BLOCK, SCORE_CHUNK), F32)
                for c in chunks:
                    p = jnp.exp(s_sc[:, c] - m)
                    total = total + p
                    p_sc[:, c] = p.astype(BF16)
                l = jnp.sum(total, axis=1, keepdims=True)
                o_ref[rs, ls] = _dot(p_sc[:, 0:width], vb[0:width]) / l
                lse_ref[0, hh:hh + 1, rs] = _row_vector(m + jnp.log(l))

    return _grid_call(
        body, "attn_fwd", D_ATTN // LANE,
        in_specs=[pair, pair, pair, pl.BlockSpec((nb, LANE), lambda p: (0, p))],
        out_specs=[pair, pl.BlockSpec((1, 8, seq), lambda p: (p, 0, 0)), pair],
        out_shape=[_sds((seq, D_ATTN), F32), _sds((D_ATTN // LANE, 8, seq), F32), _sds((seq, D_ATTN), BF16)],
        operands=(q, k, v, kmean),
        scratch=[pltpu.VMEM((seq, LANE), BF16), pltpu.VMEM((MOBA_BLOCK, LANE), BF16),
                 pltpu.VMEM((MOBA_BLOCK, seq), F32), pltpu.VMEM((MOBA_BLOCK, seq), BF16)],
        tasks=tasks)


def _mix(o, u, g, x, wba, wbp, wout, w_pool, pool_scale, ln_g, ln_b, tasks=()):
    seq = x.shape[0]

    def body(o_ref, u_ref, uprev_ref, g_ref, x_ref, wba_ref, wbp_ref, wout_ref, wp_ref, ps_ref, lg_ref, lb_ref,
             ya_ref, yp_ref, pooled_ref, mixed_ref, ypre_ref, merged_ref, xhat_ref, rstd_ref, h_ref, hb_ref, ext):
        i = pl.program_id(0)
        ya = _dot(o_ref[...].astype(BF16), wba_ref[...])
        ucur = u_ref[...]
        ext[0:POOL_HALO, :] = jnp.where(i > 0, uprev_ref[...], 0.0)
        ext[POOL_HALO:, :] = ucur
        for grp, window in enumerate(POOL_WINDOWS):
            cols = slice(POOL_GROUP * grp, POOL_GROUP * (grp + 1))
            acc = ucur[:, cols]
            for kk in range(1, window):
                acc = acc + ext[pl.ds(POOL_HALO - kk, TOK), cols]
            pooled = acc / _pool_count(i * TOK, TOK, window) - ucur[:, cols]
            pooled_ref[:, cols] = pooled.astype(BF16)
            mixed_ref[:, cols] = _dot(pooled.astype(BF16), wp_ref[grp].astype(BF16))
        mixed = mixed_ref[...]
        ypre = (mixed * ps_ref[...]).astype(BF16)
        ypre_ref[...] = ypre
        yp = _dot(ypre, wbp_ref[...])
        ya_ref[...] = ya
        yp_ref[...] = yp
        merged = (g_ref[:, :D_MODEL] * ya + g_ref[:, D_MODEL:] * yp).astype(BF16)
        merged_ref[...] = merged
        r1 = ALPHA * x_ref[...] + _dot(merged, wout_ref[...])
        h, xhat, rstd = _ln_fwd(r1, lg_ref[...], lb_ref[...])
        xhat_ref[...] = xhat
        rstd_ref[...] = jnp.broadcast_to(rstd, (TOK, LANE))
        h_ref[...] = h
        hb_ref[...] = h.astype(BF16)

    halo = pl.BlockSpec((POOL_HALO, D_POOL), lambda i: (jnp.maximum(i * (TOK // POOL_HALO) - 1, 0), 0))
    return _grid_call(
        body, "mix", seq // TOK,
        in_specs=[_rows(D_ATTN), _rows(D_POOL), halo, _rows(2 * D_MODEL), _rows(D_MODEL),
                  _full(wba.shape), _full(wbp.shape), _full(wout.shape), _full(w_pool.shape),
                  _full((1, D_POOL)), _full((1, D_MODEL)), _full((1, D_MODEL))],
        out_specs=[_rows(D_MODEL), _rows(D_MODEL), _rows(D_POOL), _rows(D_POOL), _rows(D_POOL), _rows(D_MODEL),
                   _rows(D_MODEL), _rows(LANE), _rows(D_MODEL), _rows(D_MODEL)],
        out_shape=[_sds((seq, D_MODEL), F32), _sds((seq, D_MODEL), F32), _sds((seq, D_POOL), BF16),
                   _sds((seq, D_POOL), F32), _sds((seq, D_POOL), BF16), _sds((seq, D_MODEL), BF16),
                   _sds((seq, D_MODEL), F32), _sds((seq, LANE), F32), _sds((seq, D_MODEL), F32),
                   _sds((seq, D_MODEL), BF16)],
        operands=(o, u, u, g, x, wba, wbp, wout, w_pool, pool_scale, ln_g, ln_b),
        scratch=[pltpu.VMEM((TOK + POOL_HALO, D_POOL), F32)], tasks=tasks)


def _ffn_up(hb, wgt, wut, conv_w, conv_b, tasks=()):
    seq = hb.shape[0]
    wblk = pl.BlockSpec((FF_CHUNK, D_MODEL), lambda c: (c, 0))
    cblk = lambda rows: pl.BlockSpec((rows, FF_CHUNK), lambda c: (0, c))
    oblk = pl.BlockSpec((seq, FF_CHUNK), lambda c: (0, c))

    def body(h_ref, wg_ref, wu_ref, cw_ref, cb_ref, a_ref, u_ref, act_ref):
        h = h_ref[...]
        a = _dot_nt(h, wg_ref[...])
        u = _dot_nt(h, wu_ref[...])
        a_ref[...] = a
        u_ref[...] = u
        ac = _conv(a, cw_ref[...], cb_ref[...])
        act_ref[...] = (ac * _normal_cdf(ac) * u).astype(BF16)

    return _grid_call(
        body, "ffn_up", D_FF // FF_CHUNK,
        in_specs=[_full(hb.shape), wblk, wblk, cblk(3), cblk(1)],
        out_specs=[oblk, oblk, oblk],
        out_shape=[_sds((seq, D_FF), F32), _sds((seq, D_FF), F32), _sds((seq, D_FF), BF16)],
        operands=(hb, wgt, wut, conv_w, conv_b), tasks=tasks)


def _ffn_down(act, wd, h, target, ln_g, ln_b):
    seq = h.shape[0]

    def body(act_ref, wd_ref, h_ref, t_ref, lg_ref, lb_ref, dr_ref, drb_ref, loss_ref, dg_ref, db_ref):
        i = pl.program_id(0)

        @pl.when(i == 0)
        def _():
            loss_ref[...] = jnp.zeros_like(loss_ref)
            dg_ref[...] = jnp.zeros_like(dg_ref)
            db_ref[...] = jnp.zeros_like(db_ref)

        r2 = ALPHA * h_ref[...] + _dot(act_ref[...], wd_ref[...])
        y, xhat, rstd = _ln_fwd(r2, lg_ref[...], lb_ref[...])
        diff = y - t_ref[...]
        loss_ref[...] += jnp.sum(diff * diff) * (0.5 / D_MODEL)
        dy = diff * (1.0 / D_MODEL)
        dg_ref[...] += jnp.sum(dy * xhat, axis=0, keepdims=True)
        db_ref[...] += jnp.sum(dy, axis=0, keepdims=True)
        dr = _ln_bwd(dy, xhat, rstd, lg_ref[...])
        dr_ref[...] = dr
        drb_ref[...] = dr.astype(BF16)

    vec = pl.BlockSpec((1, D_MODEL), lambda i: (0, 0))
    return pl.pallas_call(
        body, name="ffn_down", grid=(seq // TOK,),
        in_specs=[_rows(D_FF), _full(wd.shape), _rows(D_MODEL), _rows(D_MODEL), _full((1, D_MODEL)), _full((1, D_MODEL))],
        out_specs=[_rows(D_MODEL), _rows(D_MODEL), pl.BlockSpec((8, LANE), lambda i: (0, 0)), vec, vec],
        out_shape=[_sds((seq, D_MODEL), F32), _sds((seq, D_MODEL), BF16), _sds((8, LANE), F32),
                   _sds((1, D_MODEL), F32), _sds((1, D_MODEL), F32)],
        compiler_params=_params("arbitrary"),
    )(act, wd, h, target, ln_g, ln_b)


def _ffn_bwd(drb, hb, a, u, wd, conv_w, conv_b):
    seq = hb.shape[0]
    wblk = pl.BlockSpec((FF_CHUNK, D_MODEL), lambda c: (c, 0))
    cblk = lambda rows: pl.BlockSpec((rows, FF_CHUNK), lambda c: (0, c))
    sblk = pl.BlockSpec((seq, FF_CHUNK), lambda c: (0, c))

    def body(dr_ref, h_ref, a_ref, u_ref, wd_ref, cw_ref, cb_ref, da_ref, du_ref, dwd_ref, dwg_ref, dwu_ref, dc_ref):
        dr = dr_ref[...]
        h = h_ref[...]
        a = a_ref[...]
        u = u_ref[...]
        cw = cw_ref[...]
        dact = _dot_nt(dr, wd_ref[...])
        ac = _conv(a, cw, cb_ref[...])
        cdf = _normal_cdf(ac)
        gelu = ac * cdf
        dwd_ref[...] = _dot_tn((gelu * u).astype(BF16), dr).astype(BF16)
        du = (dact * gelu).astype(BF16)
        dac = dact * u * _gelu_derivative(ac, cdf)
        da = (cw[2:3, :] * dac + cw[1:2, :] * _shift_up(dac, 1) + cw[0:1, :] * _shift_up(dac, 2)).astype(BF16)
        da_ref[...] = da
        du_ref[...] = du
        dwg_ref[...] = _dot_tn(da, h).astype(BF16)
        dwu_ref[...] = _dot_tn(du, h).astype(BF16)
        dc_ref[0:1, :] = jnp.sum(dac * _shift_down(a, 2), axis=0, keepdims=True)
        dc_ref[1:2, :] = jnp.sum(dac * _shift_down(a, 1), axis=0, keepdims=True)
        dc_ref[2:3, :] = jnp.sum(dac * a, axis=0, keepdims=True)
        dc_ref[3:4, :] = jnp.sum(dac, axis=0, keepdims=True)
        dc_ref[4:8, :] = jnp.zeros((4, FF_CHUNK), F32)

    return pl.pallas_call(
        body, name="ffn_bwd", grid=(D_FF // FF_CHUNK,),
        in_specs=[_full(drb.shape), _full(hb.shape), sblk, sblk, wblk, cblk(3), cblk(1)],
        out_specs=[sblk, sblk, wblk, wblk, wblk, cblk(8)],
        out_shape=[_sds((seq, D_FF), BF16), _sds((seq, D_FF), BF16), _sds((D_FF, D_MODEL), BF16),
                   _sds((D_FF, D_MODEL), BF16), _sds((D_FF, D_MODEL), BF16), _sds((8, D_FF), F32)],
        compiler_params=_params("parallel"),
    )(drb, hb, a, u, wd, conv_w, conv_b)


def _ln1_bwd(dr2, da, du, wgt, wut, xhat, rstd, ln_g, tasks=()):
    seq = dr2.shape[0]

    def body(dr2_ref, da_ref, du_ref, wg_ref, wu_ref, xhat_ref, rstd_ref, lg_ref, dr_ref, drb_ref, dg_ref, db_ref):
        @pl.when(pl.program_id(0) == 0)
        def _():
            dg_ref[...] = jnp.zeros_like(dg_ref)
            db_ref[...] = jnp.zeros_like(db_ref)

        dh = ALPHA * dr2_ref[...] + _dot(da_ref[...], wg_ref[...]) + _dot(du_ref[...], wu_ref[...])
        xhat = xhat_ref[...]
        dg_ref[...] += jnp.sum(dh * xhat, axis=0, keepdims=True)
        db_ref[...] += jnp.sum(dh, axis=0, keepdims=True)
        dr = _ln_bwd(dh, xhat, rstd_ref[:, 0:1], lg_ref[...])
        dr_ref[...] = dr
        drb_ref[...] = dr.astype(BF16)

    vec = pl.BlockSpec((1, D_MODEL), lambda i: (0, 0))
    return _grid_call(
        body, "ln1_bwd", seq // TOK,
        in_specs=[_rows(D_MODEL), _rows(D_FF), _rows(D_FF), _full(wgt.shape), _full(wut.shape), _rows(D_MODEL),
                  _rows(LANE), _full((1, D_MODEL))],
        out_specs=[_rows(D_MODEL), _rows(D_MODEL), vec, vec],
        out_shape=[_sds((seq, D_MODEL), F32), _sds((seq, D_MODEL), BF16), _sds((1, D_MODEL), F32),
                   _sds((1, D_MODEL), F32)],
        operands=(dr2, da, du, wgt, wut, xhat, rstd, ln_g), tasks=tasks)


def _mix_bwd(drb, ya, yp, g, mixed, wout, wba, wbp, w_pool, pool_scale, tasks=()):
    seq = drb.shape[0]

    def body(dr_ref, ya_ref, yp_ref, g_ref, mixed_ref, wout_ref, wba_ref, wbp_ref, wp_ref, ps_ref,
             dzg_ref, dya_ref, dyp_ref, do_ref, dmixed_ref, dpooled_ref, dbg_ref, dps_ref):
        @pl.when(pl.program_id(0) == 0)
        def _():
            dbg_ref[...] = jnp.zeros_like(dbg_ref)
            dps_ref[...] = jnp.zeros_like(dps_ref)

        dmerged = _dot_nt(dr_ref[...], wout_ref[...])
        ga, gp = g_ref[:, :D_MODEL], g_ref[:, D_MODEL:]
        dzga = dmerged * ya_ref[...] * ga * (1.0 - ga)
        dzgp = dmerged * yp_ref[...] * gp * (1.0 - gp)
        dzg_ref[:, :D_MODEL] = dzga.astype(BF16)
        dzg_ref[:, D_MODEL:] = dzgp.astype(BF16)
        dbg_ref[:, :D_MODEL] += jnp.sum(dzga, axis=0, keepdims=True)
        dbg_ref[:, D_MODEL:] += jnp.sum(dzgp, axis=0, keepdims=True)
        dya = (dmerged * ga).astype(BF16)
        dyp = (dmerged * gp).astype(BF16)
        dya_ref[...] = dya
        dyp_ref[...] = dyp
        do_ref[...] = _dot_nt(dya, wba_ref[...])
        dypre = _dot_nt(dyp, wbp_ref[...])
        dps_ref[...] += jnp.sum(dypre * mixed_ref[...], axis=0, keepdims=True)
        dmixed = (dypre * ps_ref[...]).astype(BF16)
        dmixed_ref[...] = dmixed
        for grp in range(len(POOL_WINDOWS)):
            cols = slice(POOL_GROUP * grp, POOL_GROUP * (grp + 1))
            dpooled_ref[:, cols] = _dot_nt(dmixed[:, cols], wp_ref[grp].astype(BF16))

    return _grid_call(
        body, "mix_bwd", seq // TOK,
        in_specs=[_rows(D_MODEL), _rows(D_MODEL), _rows(D_MODEL), _rows(2 * D_MODEL), _rows(D_POOL),
                  _full(wout.shape), _full(wba.shape), _full(wbp.shape), _full(w_pool.shape), _full((1, D_POOL))],
        out_specs=[_rows(2 * D_MODEL), _rows(D_MODEL), _rows(D_MODEL), _rows(D_ATTN), _rows(D_POOL), _rows(D_POOL),
                   pl.BlockSpec((1, 2 * D_MODEL), lambda i: (0, 0)), pl.BlockSpec((1, D_POOL), lambda i: (0, 0))],
        out_shape=[_sds((seq, 2 * D_MODEL), BF16), _sds((seq, D_MODEL), BF16), _sds((seq, D_MODEL), BF16),
                   _sds((seq, D_ATTN), F32), _sds((seq, D_POOL), BF16), _sds((seq, D_POOL), F32),
                   _sds((1, 2 * D_MODEL), F32), _sds((1, D_POOL), F32)],
        operands=(drb, ya, yp, g, mixed, wout, wba, wbp, w_pool, pool_scale), tasks=tasks)


def _attn_bwd(q, k, v, bias, o, lse, do, cos, sin, tasks=()):
    seq = q.shape[0]
    nb = seq // MOBA_BLOCK
    pair = pl.BlockSpec((seq, LANE), lambda p: (0, p))
    table = pl.BlockSpec((seq, LANE), lambda p: (0, 0))
    scale = HEAD_DIM ** -0.5

    def body(q_ref, k_ref, v_ref, bias_ref, o_ref, lse_ref, do_ref, cos_ref, sin_ref, dq_ref, dk_ref, dv_ref,
             dq_acc, dk_acc, dv_acc, dk_head, dv_head, ka_sc, qa_sc, s_sc, dp_sc, p_sc, ds_sc):
        for hh in range(LANE // HEAD_DIM):
            ls = slice(HEAD_DIM * hh, HEAD_DIM * (hh + 1))
            _store_keys(ka_sc, k_ref, ls)
            vb = v_ref[:, ls]
            dk_head[...] = jnp.zeros_like(dk_head)
            dv_head[...] = jnp.zeros_like(dv_head)
            for i in range(nb):
                rs = slice(MOBA_BLOCK * i, MOBA_BLOCK * (i + 1))
                width = MOBA_BLOCK * (i + 1)
                qa_sc[:, 0:HEAD_DIM] = (q_ref[rs, ls] * scale).astype(BF16)
                qa_sc[:, HEAD_DIM:] = bias_ref[rs, ls]
                s_sc[0:width, :] = _dot_nt(ka_sc[0:width, :], qa_sc[...])
                s_sc[rs, :] = jnp.where(_causal((MOBA_BLOCK, MOBA_BLOCK), transposed=True), s_sc[rs, :], NEG)
                dob = do_ref[rs, ls]
                delta = _row_vector(jnp.sum(dob * o_ref[rs, ls], axis=1, keepdims=True))
                lse_row = lse_ref[0, hh:hh + 1, rs]
                dob16 = dob.astype(BF16)
                dp_sc[0:width, :] = _dot_nt(vb[0:width], dob16)
                for c in range(width // SCORE_CHUNK):
                    rows = slice(SCORE_CHUNK * c, SCORE_CHUNK * (c + 1))
                    p = jnp.exp(s_sc[rows, :] - lse_row)
                    p_sc[rows, :] = p.astype(BF16)
                    ds_sc[rows, :] = (p * (dp_sc[rows, :] - delta)).astype(BF16)
                dv_head[0:width, :] += _dot(p_sc[0:width, :], dob16)
                dk_head[0:width, :] += _dot(ds_sc[0:width, :], qa_sc[:, 0:HEAD_DIM])
                dq_acc[rs, ls] = _dot_tn(ds_sc[0:width, :], ka_sc[0:width, 0:HEAD_DIM]) * scale
            dk_acc[:, ls] = dk_head[...]
            dv_acc[:, ls] = dv_head[...]
        cos_t, sin_t = cos_ref[...], sin_ref[...]
        dq_ref[...] = _rope_transposed(dq_acc[...], cos_t, sin_t).astype(BF16)
        dk_ref[...] = _rope_transposed(dk_acc[...], cos_t, sin_t).astype(BF16)
        dv_ref[...] = dv_acc[...].astype(BF16)

    return _grid_call(
        body, "attn_bwd", D_ATTN // LANE,
        in_specs=[pair, pair, pair, pair, pair, pl.BlockSpec((1, 8, seq), lambda p: (p, 0, 0)), pair, table, table],
        out_specs=[pair, pair, pair], out_shape=[_sds((seq, D_ATTN), BF16)] * 3,
        operands=(q, k, v, bias, o, lse, do, cos, sin),
        scratch=[pltpu.VMEM((seq, LANE), F32)] * 3 + [pltpu.VMEM((seq, HEAD_DIM), F32)] * 2
        + [pltpu.VMEM((seq, LANE), BF16), pltpu.VMEM((MOBA_BLOCK, LANE), BF16)]
        + [pltpu.VMEM((seq, MOBA_BLOCK), F32)] * 2 + [pltpu.VMEM((seq, MOBA_BLOCK), BF16)] * 2,
        tasks=tasks)


def _in_bwd(dq, dk, dv, dpooled, dzg, dr1, win, tasks=()):
    seq = dr1.shape[0]
    nt = seq // TOK

    def body(dq_ref, dk_ref, dv_ref, dp_ref, dpnext_ref, dzg_ref, dr_ref, win_ref, dx_ref, dz_ref, ext):
        i = pl.program_id(0)
        dp = dp_ref[...]
        dpn = jnp.where(i < nt - 1, dpnext_ref[...], 0.0)
        for grp, window in enumerate(POOL_WINDOWS):
            cols = slice(POOL_GROUP * grp, POOL_GROUP * (grp + 1))
            ext[0:TOK, cols] = dp[:, cols] / _pool_count(i * TOK, TOK, window)
            ext[TOK:, cols] = dpn[:, cols] / _pool_count((i + 1) * TOK, POOL_HALO, window)
        for grp, window in enumerate(POOL_WINDOWS):
            cols = slice(POOL_GROUP * grp, POOL_GROUP * (grp + 1))
            acc = ext[0:TOK, cols] - dp[:, cols]
            for kk in range(1, window):
                acc = acc + ext[pl.ds(kk, TOK), cols]
            dz_ref[:, 3 * D_ATTN + POOL_GROUP * grp:3 * D_ATTN + POOL_GROUP * (grp + 1)] = acc.astype(BF16)
        dz_ref[:, 0:D_ATTN] = dq_ref[...]
        dz_ref[:, D_ATTN:2 * D_ATTN] = dk_ref[...]
        dz_ref[:, 2 * D_ATTN:3 * D_ATTN] = dv_ref[...]
        dz_ref[:, 3 * D_ATTN + D_POOL:] = dzg_ref[...]
        dx = ALPHA * dr_ref[...]
        for n in range(N_DEV):
            dx = dx + _dot_nt(dz_ref[:, D_ATTN * n:D_ATTN * (n + 1)], win_ref[n])
        dx_ref[...] = dx

    halo = pl.BlockSpec((POOL_HALO, D_POOL),
                        lambda i: (jnp.minimum((i + 1) * (TOK // POOL_HALO), seq // POOL_HALO - 1), 0))
    return _grid_call(
        body, "in_bwd", nt,
        in_specs=[_rows(D_ATTN), _rows(D_ATTN), _rows(D_ATTN), _rows(D_POOL), halo, _rows(2 * D_MODEL),
                  _rows(D_MODEL), _full(win.shape)],
        out_specs=[_rows(D_MODEL), _rows(D_IN_PROJ)],
        out_shape=[_sds((seq, D_MODEL), F32), _sds((seq, D_IN_PROJ), BF16)],
        operands=(dq, dk, dv, dpooled, dpooled, dzg, dr1, win),
        scratch=[pltpu.VMEM((TOK + POOL_HALO, D_POOL), F32)], tasks=tasks)


def _dw_mixers(o, ypre, merged, dya, dyp, drb, pooled, dmixed, tasks=()):
    seq = o.shape[0]
    groups = len(POOL_WINDOWS)
    col = pl.BlockSpec((seq, LANE), lambda n: (0, n))
    grp = pl.BlockSpec((seq, POOL_GROUP), lambda n: (0, jnp.minimum(n, groups - 1)))
    owner = lambda rows, cols: pl.BlockSpec((1, rows, cols), lambda n: (n, 0, 0))

    def body(o_ref, ypre_ref, merged_ref, dya_ref, dyp_ref, dr_ref, pooled_ref, dmixed_ref,
             dba_ref, dbp_ref, dout_ref, dpool_ref, ob_sc):
        n = pl.program_id(0)

        @pl.when(n == 0)
        def _():
            ob_sc[...] = o_ref[...].astype(BF16)

        dba_ref[0] = _dot_tn(ob_sc[...], dya_ref[...]).astype(BF16)
        dbp_ref[0] = _dot_tn(ypre_ref[...], dyp_ref[...]).astype(BF16)
        dout_ref[0] = _dot_tn(merged_ref[...], dr_ref[...]).astype(BF16)

        @pl.when(n < groups)
        def _():
            dpool_ref[0] = _dot_tn(pooled_ref[...], dmixed_ref[...])

    return _grid_call(
        body, "dw_mixers", N_DEV,
        in_specs=[_full(o.shape), _full(ypre.shape), col, col, col, _full(drb.shape), grp, grp],
        out_specs=[owner(D_ATTN, LANE), owner(D_POOL, LANE), owner(D_MODEL // N_DEV, D_MODEL),
                   pl.BlockSpec((1, POOL_GROUP, POOL_GROUP), lambda n: (jnp.minimum(n, groups - 1), 0, 0))],
        out_shape=[_sds((N_DEV, D_ATTN, LANE), BF16), _sds((N_DEV, D_POOL, LANE), BF16),
                   _sds((N_DEV, D_MODEL // N_DEV, D_MODEL), BF16), _sds((groups, POOL_GROUP, POOL_GROUP), F32)],
        operands=(o, ypre, merged, dya, dyp, drb, pooled, dmixed),
        scratch=[pltpu.VMEM((seq, D_ATTN), BF16)], tasks=tasks)


def _tn_matmul(name, a, b, out_shape, out_dtype, steps, a_spec, b_spec, o_spec, tasks=()):
    def body(a_ref, b_ref, o_ref):
        r = _dot_tn(a_ref[...].astype(BF16), b_ref[...].astype(BF16))
        o_ref[...] = r.reshape(o_ref.shape).astype(o_ref.dtype)

    (out,), results = _grid_call(body, name, steps, in_specs=[a_spec, b_spec], out_specs=[o_spec],
                                 out_shape=[_sds(out_shape, out_dtype)], operands=(a, b), tasks=tasks)
    return out, results


def _place():
    return lax.axis_index("x"), lax.axis_index("y"), lax.axis_index("c")


def _other_chips(x, y):
    return [(1 - x, y), (x, 1 - y), (1 - x, 1 - y)]


DMA_SEMS = pltpu.SemaphoreType.DMA


class _AllGather:
    def __init__(self, shards, lag=0):
        self.operands = list(shards)
        self.n = len(shards)
        self.lag = lag
        self.out_shape = [_sds((N_DEV, *s.shape), s.dtype) for s in shards]
        self.sems = [DMA_SEMS((7 * self.n,)), DMA_SEMS((7 * self.n,)), DMA_SEMS((self.n,))]

    def _copy(self, refs, a, k, block, to, from_input=False):
        ins, outs, (send_sems, recv_sems, _) = refs
        px, py, pc = block
        dst = outs[a].at[4 * px + 2 * py + pc]
        return pltpu.make_async_remote_copy(
            src_ref=ins[a] if from_input else dst, dst_ref=dst,
            send_sem=send_sems.at[7 * a + k], recv_sem=recv_sems.at[7 * a + k],
            device_id=to, device_id_type=MESH)

    def _local(self, refs, a):
        ins, outs, (_, _, local_sems) = refs
        x, y, c = _place()
        return pltpu.make_async_copy(ins[a], outs[a].at[4 * x + 2 * y + c], local_sems.at[a])

    def _pass_on(self, refs, a):
        x, y, c = _place()
        origin = ((x + 1 - c) % 2, (y + c) % 2, c)
        target = ((x + c) % 2, (y + 1 - c) % 2, c)
        return self._copy(refs, a, 3, origin, target)

    def start(self, refs):
        x, y, c = _place()
        for a in range(self.n):
            self._local(refs, a).start()
        for a in range(self.n):
            self._copy(refs, a, 0, (x, y, c), (x, y, 1 - c), True).start()
            for j, chip in enumerate(_other_chips(x, y)[:2]):
                self._copy(refs, a, 1 + j, (x, y, c), (*chip, c), True).start()

    def middle(self, refs):
        x, y, c = _place()
        me, sibling = (x, y, c), (x, y, 1 - c)
        chips = _other_chips(x, y)
        for a in range(self.n):
            for j in range(2):
                self._copy(refs, a, 1 + j, (*chips[j], c), me).wait_recv()
        for a in range(self.n):
            self._pass_on(refs, a).start()
            for j in range(2):
                self._copy(refs, a, 4 + j, (*chips[j], c), sibling).start()

    def late(self, refs):
        x, y, c = _place()
        diagonal = (1 - x, 1 - y, c)
        for a in range(self.n):
            self._copy(refs, a, 3, diagonal, (x, y, c)).wait_recv()
            self._copy(refs, a, 6, diagonal, (x, y, 1 - c)).start()

    def finish(self, refs):
        x, y, c = _place()
        me, sibling = (x, y, c), (x, y, 1 - c)
        chips = _other_chips(x, y)
        for a in range(self.n):
            self._copy(refs, a, 0, sibling, me).wait_recv()
            for j, chip in enumerate(chips):
                self._copy(refs, a, 4 + j, (*chip, 1 - c), me).wait_recv()
        for a in range(self.n):
            self._copy(refs, a, 0, me, sibling, True).wait_send()
            for j, chip in enumerate(chips[:2]):
                self._copy(refs, a, 1 + j, me, (*chip, c), True).wait_send()
            self._pass_on(refs, a).wait_send()
            for j, chip in enumerate(chips):
                self._copy(refs, a, 4 + j, (*chip, c), sibling).wait_send()
            self._local(refs, a).wait()


class _SiblingSend:
    def __init__(self, partials):
        self.operands = list(partials)
        self.n = len(partials)
        self.out_shape = [_sds((4, *p.shape[1:]), p.dtype) for p in partials]
        self.sems = [DMA_SEMS((4 * self.n,)), DMA_SEMS((4 * self.n,))]

    def _copy(self, refs, a, q):
        ins, outs, (send_sems, recv_sems) = refs
        x, y, c = _place()
        return pltpu.make_async_remote_copy(
            src_ref=ins[a].at[2 * q + 1 - c], dst_ref=outs[a].at[q],
            send_sem=send_sems.at[4 * a + q], recv_sem=recv_sems.at[4 * a + q],
            device_id=(x, y, 1 - c), device_id_type=MESH)

    def start(self, refs):
        for a in range(self.n):
            for q in range(4):
                self._copy(refs, a, q).start()

    def middle(self, refs):
        pass

    def finish(self, refs):
        for a in range(self.n):
            for q in range(4):
                self._copy(refs, a, q).wait()


class _ChipScatter:
    def __init__(self, chip_partials):
        self.operands = list(chip_partials)
        self.n = len(chip_partials)
        self.out_shape = [_sds(p.shape, p.dtype) for p in chip_partials]
        self.sems = [DMA_SEMS((3 * self.n,)), DMA_SEMS((3 * self.n,)), DMA_SEMS((self.n,))]

    def _copy(self, refs, a, k, arrival=False):
        ins, outs, (send_sems, recv_sems, _) = refs
        x, y, c = _place()
        px, py = _other_chips(x, y)[k]
        mine, theirs = 2 * x + y, 2 * px + py
        return pltpu.make_async_remote_copy(
            src_ref=ins[a].at[mine if arrival else theirs], dst_ref=outs[a].at[theirs if arrival else mine],
            send_sem=send_sems.at[3 * a + k], recv_sem=recv_sems.at[3 * a + k],
            device_id=(px, py, c), device_id_type=MESH)

    def _local(self, refs, a):
        ins, outs, (_, _, local_sems) = refs
        x, y, _ = _place()
        return pltpu.make_async_copy(ins[a].at[2 * x + y], outs[a].at[2 * x + y], local_sems.at[a])

    def start(self, refs):
        for a in range(self.n):
            self._local(refs, a).start()
            for k in range(3):
                self._copy(refs, a, k).start()

    def middle(self, refs):
        pass

    def finish(self, refs):
        for a in range(self.n):
            for k in range(3):
                self._copy(refs, a, k, arrival=True).wait_recv()
        for a in range(self.n):
            for k in range(3):
                self._copy(refs, a, k).wait_send()
            self._local(refs, a).wait()


class _DirectScatter:
    def __init__(self, partials):
        self.operands = list(partials)
        self.n = len(partials)
        self.out_shape = [_sds(p.shape, p.dtype) for p in partials]
        self.sems = [DMA_SEMS((7 * self.n,)), DMA_SEMS((7 * self.n,)), DMA_SEMS((self.n,))]

    def _copy(self, refs, a, k, arrival=False):
        ins, outs, (send_sems, recv_sems, _) = refs
        x, y, c = _place()
        peer = [(x, y, 1 - c), (1 - x, y, c), (x, 1 - y, c), (1 - x, 1 - y, c),
                (1 - x, y, 1 - c), (x, 1 - y, 1 - c), (1 - x, 1 - y, 1 - c)][k]
        mine, theirs = 4 * x + 2 * y + c, 4 * peer[0] + 2 * peer[1] + peer[2]
        return pltpu.make_async_remote_copy(
            src_ref=ins[a].at[mine if arrival else theirs], dst_ref=outs[a].at[theirs if arrival else mine],
            send_sem=send_sems.at[7 * a + k], recv_sem=recv_sems.at[7 * a + k],
            device_id=peer, device_id_type=MESH)

    def _local(self, refs, a):
        ins, outs, (_, _, local_sems) = refs
        x, y, c = _place()
        return pltpu.make_async_copy(ins[a].at[4 * x + 2 * y + c], outs[a].at[4 * x + 2 * y + c], local_sems.at[a])

    def start(self, refs):
        for a in range(self.n):
            self._local(refs, a).start()
            for k in range(7):
                self._copy(refs, a, k).start()

    def middle(self, refs):
        pass

    def finish(self, refs):
        for a in range(self.n):
            for k in range(7):
                self._copy(refs, a, k, arrival=True).wait_recv()
        for a in range(self.n):
            for k in range(7):
                self._copy(refs, a, k).wait_send()
            self._local(refs, a).wait()


def _task_args(tasks):
    hbm = pl.BlockSpec(memory_space=pl.ANY)
    operands = [o for t in tasks for o in t.operands]
    out_shape = [s for t in tasks for s in t.out_shape]
    sems = [s for t in tasks for s in t.sems]
    return operands, [hbm] * len(operands), out_shape, [hbm] * len(out_shape), sems


def _task_refs(tasks, ins, outs, sems):
    per_task = []
    for t in tasks:
        ni, no, ns = len(t.operands), len(t.out_shape), len(t.sems)
        per_task.append((ins[:ni], outs[:no], sems[:ns]))
        ins, outs, sems = ins[ni:], outs[no:], sems[ns:]
    return per_task


def _task_results(tasks, outs):
    res = []
    for t in tasks:
        res.append(list(outs[:len(t.out_shape)]))
        outs = outs[len(t.out_shape):]
    return res


def _carry(body, tasks, n_in, n_out, n_scratch, steps):
    if not tasks:
        return body
    t_in = sum(len(t.operands) for t in tasks)
    t_out = sum(len(t.out_shape) for t in tasks)

    def wrapped(*refs):
        ins, refs = refs[:n_in], refs[n_in:]
        t_ins, refs = refs[:t_in], refs[t_in:]
        outs, refs = refs[:n_out], refs[n_out:]
        t_outs, refs = refs[:t_out], refs[t_out:]
        scratch, t_sems = refs[:n_scratch], refs[n_scratch:]
        per_task = _task_refs(tasks, t_ins, t_outs, t_sems)
        step = pl.program_id(0)

        @pl.when(step == 0)
        def _():
            for t, r in zip(tasks, per_task):
                t.start(r)

        for t, r in zip(tasks, per_task):
            pl.when(step == max(steps - 1 - getattr(t, "lag", 0), 0))(functools.partial(t.middle, r))
            if hasattr(t, "late"):
                pl.when(step == steps - 1)(functools.partial(t.late, r))

        body(*ins, *outs, *scratch)

        @pl.when(step == steps - 1)
        def _():
            for t, r in zip(tasks, per_task):
                t.finish(r)

    return wrapped


def _exchange(name, tasks):
    operands, in_specs, out_shape, out_specs, sems = _task_args(tasks)

    def body(*refs):
        ni, no = len(operands), len(out_shape)
        per_task = _task_refs(tasks, refs[:ni], refs[ni:ni + no], refs[ni + no:])
        for phase in ("start", "middle", "late", "finish"):
            for t, r in zip(tasks, per_task):
                if hasattr(t, phase):
                    getattr(t, phase)(r)

    outs = pl.pallas_call(body, name=name, in_specs=in_specs, out_specs=out_specs, out_shape=out_shape,
                          scratch_shapes=sems)(*operands)
    return _task_results(tasks, outs)


def _row_tile(rows, cols, whole_up_to=256 * 1024):
    if rows * cols <= whole_up_to:
        return rows
    for t in (256, 176, 128, 64, 32, 16, 8):
        if rows % t == 0:
            return t
    return rows


def _pair_sum(name, partials, from_sibling):
    n = len(partials)
    _, rows, cols = partials[0].shape
    tile = _row_tile(rows, cols, 512 * 1024)

    def body(*refs):
        south = lax.axis_index("c") == 0
        for p_ref, s_ref, o_ref in zip(refs[:n], refs[n:2 * n], refs[2 * n:]):
            mine = jnp.where(south, p_ref[0, 0].astype(F32), p_ref[0, 1].astype(F32))
            o_ref[0] = (mine + s_ref[0].astype(F32)).astype(o_ref.dtype)

    blk = pl.BlockSpec((1, tile, cols), lambda q, i: (q, i, 0))
    return pl.pallas_call(
        body, name=name, grid=(4, rows // tile),
        in_specs=[pl.BlockSpec((1, 2, tile, cols), lambda q, i: (q, 0, i, 0))] * n + [blk] * n,
        out_specs=[blk] * n, out_shape=[_sds(s.shape, s.dtype) for s in from_sibling],
        compiler_params=_params("parallel", "parallel"),
    )(*[p.reshape(4, 2, rows, cols) for p in partials], *from_sibling)


def _sum_leading(name, stacked):
    parts, rows, cols = stacked.shape
    tile = _row_tile(rows, cols, (512 if parts <= 4 else 256) * 1024)

    def body(s_ref, o_ref):
        acc = s_ref[0].astype(F32)
        for d in range(1, parts):
            acc = acc + s_ref[d].astype(F32)
        o_ref[...] = acc

    return pl.pallas_call(
        body, name=name, grid=(rows // tile,),
        in_specs=[pl.BlockSpec((parts, tile, cols), lambda i: (0, i, 0))],
        out_specs=pl.BlockSpec((tile, cols), lambda i: (i, 0)),
        out_shape=_sds((rows, cols), F32),
        compiler_params=_params("parallel"),
    )(stacked)


def _adamw_math(w, g, m, v):
    nm = ADAM_B1 * m + (1.0 - ADAM_B1) * g
    nv = ADAM_B2 * v + (1.0 - ADAM_B2) * (g * g)
    m_hat = nm / (1.0 - ADAM_B1 ** ADAM_STEP)
    v_hat = nv / (1.0 - ADAM_B2 ** ADAM_STEP)
    return -ADAM_LR * (m_hat / (jnp.sqrt(v_hat) + ADAM_EPS) + ADAM_WD * w), nm, nv


def _adamw(name, w, g, m, v):
    rows, cols = w.shape
    tile = _row_tile(rows, cols)

    def body(w_ref, g_ref, m_ref, v_ref, d_ref, nm_ref, nv_ref):
        d_ref[...], nm_ref[...], nv_ref[...] = _adamw_math(w_ref[...], g_ref[...], m_ref[...], v_ref[...])

    blk = pl.BlockSpec((tile, cols), lambda i: (i, 0))
    return pl.pallas_call(
        body, name=name, grid=(rows // tile,),
        in_specs=[blk] * 4, out_specs=[blk] * 3,
        out_shape=[_sds((rows, cols), F32)] * 3,
        compiler_params=_params("parallel"),
    )(w, g, m, v)


def _sum_adamw(name, stacked, w, m, v):
    parts, rows, cols = stacked.shape
    tile = _row_tile(rows, cols)

    def body(s_ref, w_ref, m_ref, v_ref, g_ref, d_ref, nm_ref, nv_ref):
        g = s_ref[0].astype(F32)
        for d in range(1, parts):
            g = g + s_ref[d].astype(F32)
        g_ref[...] = g
        d_ref[...], nm_ref[...], nv_ref[...] = _adamw_math(w_ref[...], g, m_ref[...], v_ref[...])

    blk = pl.BlockSpec((tile, cols), lambda i: (i, 0))
    return pl.pallas_call(
        body, name=name, grid=(rows // tile,),
        in_specs=[pl.BlockSpec((parts, tile, cols), lambda i: (0, i, 0))] + [blk] * 3, out_specs=[blk] * 4,
        out_shape=[_sds((rows, cols), F32)] * 4,
        compiler_params=_params("parallel"),
    )(stacked, w, m, v)


SMALL = ("b_gate", "w_pool", "pool_scale", "ln1_g", "ln1_b", "conv_b", "ln2_g", "ln2_b")
TILE = 8 * LANE


def _pack(parts):
    tiles = []
    for p in parts:
        flat = p.reshape(-1)
        tiles.append(jnp.pad(flat, (0, -flat.size % TILE)).reshape(-1, LANE))
    return jnp.concatenate(tiles, axis=0)


def _unpack(packed, shapes):
    out, at = [], 0
    for shape in shapes:
        size = math.prod(shape)
        rows = -(-size // TILE) * 8
        out.append(packed[at:at + rows].reshape(-1)[:size].reshape(shape))
        at += rows
    return out


MIXER = ("w_branch_attn", "w_branch_pool", "w_out", "conv_w")
FFN = ("w_ffn_gate_t", "w_ffn_up_t", "w_ffn_down")


def _columns(t):
    return jnp.transpose(t, (1, 0, 2)).reshape(t.shape[1], N_DEV * t.shape[2])


def _row_blocks(t):
    return t.reshape(N_DEV * t.shape[1], t.shape[2])


def _by_owner(t):
    return t.reshape(N_DEV, t.shape[0] // N_DEV, t.shape[1])


def _reduce_halves(names, partials, from_sibling):
    out = [None] * len(names)
    for shape in dict.fromkeys(p.shape for p in partials):
        group = [i for i, p in enumerate(partials) if p.shape == shape]
        sums = _pair_sum("pair_sum_" + names[group[0]], [partials[i] for i in group], [from_sibling[i] for i in group])
        for i, s in zip(group, sums):
            out[i] = s
    return out


def _local_step(x, target, shards, small):
    seq = x.shape[0]
    cos, sin = _rope_tables(seq)
    ((w_in_all,),) = _exchange("gather_w_in", [_AllGather([shards["w_in"]])])
    (xb, q, k, v, u, g, kmean), (mixer,) = _proj_in(
        x, w_in_all, small["b_gate"], cos, sin, tasks=[_AllGather([shards[n] for n in MIXER], lag=1)])
    wba, wbp, wout, conv_w = _columns(mixer[0]), _columns(mixer[1]), _row_blocks(mixer[2]), _columns(mixer[3])
    (o, lse, bias), ((wgt, wut),) = _attn_fwd(
        q, k, v, kmean.reshape(seq // MOBA_BLOCK, D_ATTN),
        tasks=[_AllGather([shards["w_ffn_gate_t"], shards["w_ffn_up_t"]], lag=1)])
    (ya, yp, pooled, mixed, ypre, merged, xhat1, rstd1, h1, h1b), _ = _mix(
        o, u, g, x, wba, wbp, wout, small["w_pool"], small["pool_scale"], small["ln1_g"], small["ln1_b"])
    wgt, wut = _row_blocks(wgt), _row_blocks(wut)
    (a, uf, act), ((wd,),) = _ffn_up(
        h1b, wgt, wut, conv_w, small["conv_b"], tasks=[_AllGather([shards["w_ffn_down"]], lag=4)])
    wd = _row_blocks(wd)
    dr2, dr2b, loss, dg2, db2 = _ffn_down(act, wd, h1, target, small["ln2_g"], small["ln2_b"])

    da, du, dwd, dwg, dwu, dconv = _ffn_bwd(dr2b, h1b, a, uf, wd, conv_w, small["conv_b"])
    ffn_partials = [_by_owner(dwg), _by_owner(dwu), _by_owner(dwd)]
    (dr1, dr1b, dg1, db1), (ffn_sibling,) = _ln1_bwd(
        dr2, da, du, wgt, wut, xhat1, rstd1, small["ln1_g"], tasks=[_SiblingSend(ffn_partials)])
    ffn_chip = _reduce_halves(FFN, ffn_partials, ffn_sibling)
    (dzg, dya, dyp, do, dmixed, dpooled, dbg, dps), (gate_landed,) = _mix_bwd(
        dr1b, ya, yp, g, mixed, wout, wba, wbp, small["w_pool"], small["pool_scale"],
        tasks=[_ChipScatter(ffn_chip[0:1])])
    (dw_ba, dw_bp, dw_out, dw_pool), _ = _dw_mixers(o, ypre, merged, dya, dyp, dr1b, pooled, dmixed)
    mixer_partials = [dw_ba, dw_bp, dw_out]
    (dq, dk, dv), (up_down_landed, mixer_sibling) = _attn_bwd(
        q, k, v, bias, o, lse, do, cos, sin, tasks=[_ChipScatter(ffn_chip[1:3]), _SiblingSend(mixer_partials)])
    mixer_chip = _reduce_halves(MIXER[:3], mixer_partials, mixer_sibling)
    (grad_x, dz), _ = _in_bwd(dq, dk, dv, dpooled, dzg, dr1, w_in_all)
    dw_in, (mixer_landed,) = _tn_matmul(
        "dw_in", xb, dz, (N_DEV, D_MODEL, D_ATTN), BF16, 2 * N_DEV,
        pl.BlockSpec((seq, 512), lambda s: (0, s % 2)), pl.BlockSpec((seq, D_ATTN), lambda s: (0, s // 2)),
        pl.BlockSpec((1, 512, D_ATTN), lambda s: (s // 2, s % 2, 0)), tasks=[_ChipScatter(mixer_chip)])

    landed = dict(zip(FFN + MIXER[:3], gate_landed + up_down_landed + mixer_landed))
    little = {"b_gate": dbg, "w_pool": dw_pool, "pool_scale": dps, "ln1_g": dg1, "ln1_b": db1, "conv_b": dconv[3:4],
              "ln2_g": dg2, "ln2_b": db2, "conv_w": dconv[0:3], "loss": loss}
    return grad_x, landed, dw_in, little


def kernel(x, w_in, b_gate, w_branch_attn, w_pool, pool_scale, w_branch_pool, w_out, ln1_g, ln1_b, w_ffn_gate, w_ffn_up, conv_w, conv_b, w_ffn_down, ln2_g, ln2_b, loss_target, m_w_in, m_b_gate, m_w_branch_attn, m_w_pool, m_pool_scale, m_w_branch_pool, m_w_out, m_ln1_g, m_ln1_b, m_w_ffn_gate, m_w_ffn_up, m_conv_w, m_conv_b, m_w_ffn_down, m_ln2_g, m_ln2_b, v_w_in, v_b_gate, v_w_branch_attn, v_w_pool, v_pool_scale, v_w_branch_pool, v_w_out, v_ln1_g, v_ln1_b, v_w_ffn_gate, v_w_ffn_up, v_conv_w, v_conv_b, v_w_ffn_down, v_ln2_g, v_ln2_b):
    me = 4 * lax.axis_index("x") + 2 * lax.axis_index("y") + lax.axis_index("c")
    weights = dict(w_in=w_in, b_gate=b_gate, w_branch_attn=w_branch_attn, w_pool=w_pool, pool_scale=pool_scale,
                   w_branch_pool=w_branch_pool, w_out=w_out, ln1_g=ln1_g, ln1_b=ln1_b, w_ffn_gate=w_ffn_gate,
                   w_ffn_up=w_ffn_up, conv_w=conv_w, conv_b=conv_b, w_ffn_down=w_ffn_down, ln2_g=ln2_g, ln2_b=ln2_b)
    m_in = dict(w_in=m_w_in, b_gate=m_b_gate, w_branch_attn=m_w_branch_attn, w_pool=m_w_pool,
                pool_scale=m_pool_scale, w_branch_pool=m_w_branch_pool, w_out=m_w_out, ln1_g=m_ln1_g, ln1_b=m_ln1_b,
                w_ffn_gate=m_w_ffn_gate, w_ffn_up=m_w_ffn_up, conv_w=m_conv_w, conv_b=m_conv_b,
                w_ffn_down=m_w_ffn_down, ln2_g=m_ln2_g, ln2_b=m_ln2_b)
    v_in = dict(w_in=v_w_in, b_gate=v_b_gate, w_branch_attn=v_w_branch_attn, w_pool=v_w_pool,
                pool_scale=v_pool_scale, w_branch_pool=v_w_branch_pool, w_out=v_w_out, ln1_g=v_ln1_g, ln1_b=v_ln1_b,
                w_ffn_gate=v_w_ffn_gate, w_ffn_up=v_w_ffn_up, conv_w=v_conv_w, conv_b=v_conv_b,
                w_ffn_down=v_w_ffn_down, ln2_g=v_ln2_g, ln2_b=v_ln2_b)
    weights = {n: a[0] for n, a in weights.items()}
    m_in = {n: a[0] for n, a in m_in.items()}
    v_in = {n: a[0] for n, a in v_in.items()}

    shards = {"w_in": weights["w_in"].astype(BF16), "w_branch_attn": weights["w_branch_attn"].astype(BF16),
              "w_branch_pool": weights["w_branch_pool"].astype(BF16), "w_out": weights["w_out"].astype(BF16),
              "w_ffn_gate_t": weights["w_ffn_gate"].T.astype(BF16), "w_ffn_up_t": weights["w_ffn_up"].T.astype(BF16),
              "w_ffn_down": weights["w_ffn_down"].astype(BF16), "conv_w": weights["conv_w"]}
    small = {"b_gate": weights["b_gate"][None], "w_pool": weights["w_pool"], "pool_scale": weights["pool_scale"][None],
             "ln1_g": weights["ln1_g"][None], "ln1_b": weights["ln1_b"][None], "conv_b": weights["conv_b"][None],
             "ln2_g": weights["ln2_g"][None], "ln2_b": weights["ln2_b"][None]}

    grad_x, landed, dw_in, little = _local_step(x[0], loss_target[0], shards, small)

    ((w_in_sibling,),) = _exchange("sibling_grads", [_SiblingSend([dw_in])])
    w_in_chip = _reduce_halves(["w_in"], [dw_in], [w_in_sibling])
    names = SMALL + ("conv_w",)
    (landed["w_in"],), (all_small,) = _exchange(
        "scatter_grads", [_ChipScatter(w_in_chip), _AllGather([_pack([little[n] for n in names + ("loss",)])])])

    grads, delta, new_m, new_v = {}, {}, {}, {}
    for n in ("w_in", "w_branch_attn", "w_branch_pool", "w_out", "w_ffn_down"):
        grads[n], delta[n], new_m[n], new_v[n] = _sum_adamw("update_" + n, landed[n], weights[n], m_in[n], v_in[n])
    for n in ("w_ffn_gate", "w_ffn_up"):
        updated = _sum_adamw("update_" + n, landed[n + "_t"], weights[n].T, m_in[n].T, v_in[n].T)
        grads[n], delta[n], new_m[n], new_v[n] = (t.T for t in updated)
    small_sum = _sum_leading("sum_small", all_small)
    *small_grads, conv_w_grad, loss = _unpack(
        small_sum, [weights[n].shape for n in SMALL] + [(3, D_FF), little["loss"].shape])
    loss = loss[0, 0]
    grads.update(zip(SMALL, small_grads))
    grads["conv_w"] = lax.dynamic_slice(conv_w_grad, (0, me * FF_SHARD), (3, FF_SHARD))
    flat = lambda d: _pack([d[n] for n in names])
    shapes = [weights[n].shape for n in names]
    for out, packed in zip((delta, new_m, new_v),
                           _adamw("adamw_small", flat(weights), flat(grads), flat(m_in), flat(v_in))):
        out.update(zip(names, _unpack(packed, shapes)))

    order = ("w_in", "b_gate", "w_branch_attn", "w_pool", "pool_scale", "w_branch_pool", "w_out", "ln1_g", "ln1_b",
             "w_ffn_gate", "w_ffn_up", "conv_w", "conv_b", "w_ffn_down", "ln2_g", "ln2_b")
    lead = lambda t: t[None]
    return (loss, lead(grad_x), *[lead(grads[n]) for n in order], *[lead(delta[n]) for n in order],
            *[lead(new_m[n]) for n in order], *[lead(new_v[n]) for n in order])
```

```python
import functools
import math

import jax
import jax.numpy as jnp
from jax import lax
from jax.experimental import pallas as pl
from jax.experimental.pallas import tpu as pltpu

F32 = jnp.float32
BF16 = jnp.bfloat16

D_MODEL = 1024
N_HEADS = 8
HEAD_DIM = 64
D_ATTN = N_HEADS * HEAD_DIM
MOBA_BLOCK = 256
MOBA_TOPK = 3
ROPE_THETA = 10000.0
POOL_WINDOWS = (2, 4, 8, 16)
POOL_GROUP = 128
D_POOL = len(POOL_WINDOWS) * POOL_GROUP
POOL_HALO = 16
D_FF = 2816
D_IN_PROJ = 3 * D_ATTN + D_POOL + 2 * D_MODEL
LN_EPS = 1e-5
ALPHA = 2.0 ** 0.25
NEG = -1e30
N_DEV = 8
FF_SHARD = D_FF // N_DEV

ADAM_LR = 0.001
ADAM_B1 = 0.9
ADAM_B2 = 0.999
ADAM_EPS = 1e-08
ADAM_WD = 0.01
ADAM_STEP = 10

TOK = 256
FF_CHUNK = 256
LANE = 128
VMEM_LIMIT = 56 * 1024 * 1024

MESH = pl.DeviceIdType.MESH
NT_DIMS = (((1,), (1,)), ((), ()))
TN_DIMS = (((0,), (0,)), ((), ()))


def _params(*sem):
    return pltpu.CompilerParams(dimension_semantics=sem or None, vmem_limit_bytes=VMEM_LIMIT)


def _full(shape):
    zeros = (0,) * len(shape)
    return pl.BlockSpec(shape, lambda *_: zeros, pipeline_mode=pl.Buffered(1))


def _rows(width, tile=TOK):
    return pl.BlockSpec((tile, width), lambda i: (i, 0))


def _sds(shape, dtype):
    return jax.ShapeDtypeStruct(shape, dtype)


def _dot(a, b):
    return jnp.dot(a, b, preferred_element_type=F32)


def _dot_nt(a, b):
    return lax.dot_general(a, b, NT_DIMS, preferred_element_type=F32)


def _dot_tn(a, b):
    return lax.dot_general(a, b, TN_DIMS, preferred_element_type=F32)


def _rope_tables(seq):
    half = HEAD_DIM // 2
    inv_freq = 1.0 / (ROPE_THETA ** (jnp.arange(half, dtype=F32) / half))
    ang = jnp.arange(seq, dtype=F32)[:, None] * inv_freq[None, :]
    cos, sin = jnp.cos(ang), jnp.sin(ang)
    return jnp.tile(cos, (1, 4)), jnp.tile(jnp.concatenate([-sin, sin], axis=1), (1, 2))


def _swap_halves(t):
    lane = lax.broadcasted_iota(jnp.int32, t.shape, 1)
    return jnp.where((lane % HEAD_DIM) < HEAD_DIM // 2, pltpu.roll(t, LANE - 32, 1), pltpu.roll(t, 32, 1))


def _rope(t, cos, sin):
    return t * cos + _swap_halves(t) * sin


def _rope_transposed(g, cos, sin):
    return g * cos + _swap_halves(g * sin)


def _ln_fwd(r, g, b):
    mu = jnp.mean(r, axis=-1, keepdims=True)
    xc = r - mu
    var = jnp.mean(xc * xc, axis=-1, keepdims=True)
    rstd = lax.rsqrt(var + LN_EPS)
    xhat = xc * rstd
    return xhat * g + b, xhat, rstd


def _ln_bwd(dy, xhat, rstd, g):
    dxh = dy * g
    m1 = jnp.mean(dxh, axis=-1, keepdims=True)
    m2 = jnp.mean(dxh * xhat, axis=-1, keepdims=True)
    return rstd * (dxh - m1 - xhat * m2)


def _normal_cdf(a):
    return 0.5 * (1.0 + lax.erf(a * (1.0 / math.sqrt(2.0))))


def _gelu_derivative(a, cdf):
    return cdf + a * (jnp.exp(-0.5 * a * a) * (1.0 / math.sqrt(2.0 * math.pi)))


def _shift_down(a, k):
    row = lax.broadcasted_iota(jnp.int32, a.shape, 0)
    return jnp.where(row >= k, pltpu.roll(a, k, 0), 0.0)


def _shift_up(a, k):
    n = a.shape[0]
    row = lax.broadcasted_iota(jnp.int32, a.shape, 0)
    return jnp.where(row < n - k, pltpu.roll(a, n - k, 0), 0.0)


def _conv(a, cw, cb):
    return cw[2:3, :] * a + cw[1:2, :] * _shift_down(a, 1) + cw[0:1, :] * _shift_down(a, 2) + cb


def _pool_count(first_row, rows, window):
    t = first_row + lax.broadcasted_iota(jnp.int32, (rows, 1), 0)
    return jnp.minimum(t + 1, window).astype(F32)


def _grid_call(body, name, steps, in_specs, out_specs, out_shape, operands, scratch=(), tasks=()):
    t_operands, t_in_specs, t_out_shape, t_out_specs, t_sems = _task_args(tasks)
    outs = pl.pallas_call(
        _carry(body, tasks, len(in_specs), len(out_specs), len(scratch), steps), name=name, grid=(steps,),
        in_specs=list(in_specs) + t_in_specs, out_specs=list(out_specs) + t_out_specs,
        out_shape=list(out_shape) + t_out_shape, scratch_shapes=list(scratch) + t_sems,
        compiler_params=_params("arbitrary"),
    )(*operands, *t_operands)
    return outs[:len(out_specs)], _task_results(tasks, outs[len(out_specs):])


def _proj_in(x, win, b_gate, cos, sin, tasks=()):
    seq = x.shape[0]
    nt = seq // TOK

    def body(x_ref, win_ref, bg_ref, cos_ref, sin_ref, xb_ref, q_ref, k_ref, v_ref, u_ref, g_ref, km_ref):
        xb = x_ref[...].astype(BF16)
        xb_ref[...] = xb
        cos_t, sin_t = cos_ref[...], sin_ref[...]
        for sec, out_ref in ((0, q_ref), (1, k_ref)):
            z = _dot(xb, win_ref[sec])
            for c in range(D_ATTN // LANE):
                cols = slice(LANE * c, LANE * (c + 1))
                out_ref[:, cols] = _rope(z[:, cols], cos_t, sin_t)
        for b in range(TOK // MOBA_BLOCK):
            km_ref[b] = jnp.mean(k_ref[MOBA_BLOCK * b:MOBA_BLOCK * (b + 1), :], axis=0, keepdims=True)
        v_ref[...] = _dot(xb, win_ref[2]).astype(BF16)
        u_ref[...] = _dot(xb, win_ref[3])
        for n in range(4):
            cols = slice(D_ATTN * n, D_ATTN * (n + 1))
            g_ref[:, cols] = jax.nn.sigmoid(_dot(xb, win_ref[4 + n]) + bg_ref[:, cols])

    return _grid_call(
        body, "proj_in", nt,
        in_specs=[_rows(D_MODEL), _full(win.shape), _full((1, 2 * D_MODEL)), _rows(LANE), _rows(LANE)],
        out_specs=[_rows(D_MODEL), _rows(D_ATTN), _rows(D_ATTN), _rows(D_ATTN), _rows(D_POOL), _rows(2 * D_MODEL),
                   pl.BlockSpec((TOK // MOBA_BLOCK, 1, D_ATTN), lambda i: (i, 0, 0))],
        out_shape=[_sds((seq, D_MODEL), BF16), _sds((seq, D_ATTN), F32), _sds((seq, D_ATTN), F32),
                   _sds((seq, D_ATTN), BF16), _sds((seq, D_POOL), F32), _sds((seq, 2 * D_MODEL), F32),
                   _sds((seq // MOBA_BLOCK, 1, D_ATTN), F32)],
        operands=(x, win, b_gate, cos, sin), tasks=tasks)


SCORE_CHUNK = 128


def _store_keys(ka_sc, k_ref, ls):
    seq = ka_sc.shape[0]
    ka_sc[:, 0:HEAD_DIM] = k_ref[:, ls].astype(BF16)
    row = lax.broadcasted_iota(jnp.int32, (seq, HEAD_DIM), 0)
    lane = lax.broadcasted_iota(jnp.int32, (seq, HEAD_DIM), 1)
    in_block = (lane * MOBA_BLOCK <= row) & (row < (lane + 1) * MOBA_BLOCK)
    ka_sc[:, HEAD_DIM:] = jnp.where(in_block, 1.0, 0.0).astype(BF16)


def _block_bias(qf, km, i):
    if i <= MOBA_TOPK:
        return jnp.zeros((MOBA_BLOCK, HEAD_DIM), BF16)
    nb = km.shape[0]
    gate = lax.dot_general(km, qf, NT_DIMS, precision=lax.Precision.HIGHEST, preferred_element_type=F32)
    blk = lax.broadcasted_iota(jnp.int32, gate.shape, 0)
    rank = jnp.zeros(gate.shape, F32)
    for r in range(1, i):
        lower = pltpu.roll(gate, r, 0)
        rank = rank + jnp.where((blk >= r) & (lower >= gate), 1.0, 0.0)
        higher = pltpu.roll(gate, nb - r, 0)
        rank = rank + jnp.where((blk + r < i) & (higher > gate), 1.0, 0.0)
    bias = jnp.where((blk < i) & (rank >= MOBA_TOPK), NEG, 0.0)
    padded = jnp.concatenate([bias, jnp.zeros((LANE - nb, MOBA_BLOCK), F32)], axis=0)
    return jnp.transpose(padded)[:, 0:HEAD_DIM].astype(BF16)


def _causal(shape, transposed=False):
    row = lax.broadcasted_iota(jnp.int32, shape, 0)
    col = lax.broadcasted_iota(jnp.int32, shape, 1)
    return (row <= col) if transposed else (col <= row)


def _row_vector(col):
    return jnp.transpose(jnp.broadcast_to(col, (MOBA_BLOCK, LANE)))[0:1, :]


def _attn_fwd(q, k, v, kmean, tasks=()):
    seq = q.shape[0]
    nb = seq // MOBA_BLOCK
    assert nb == 8, "the block ranking keeps one sublane per key block"
    pair = pl.BlockSpec((seq, LANE), lambda p: (0, p))
    heads = LANE // HEAD_DIM

    def body(q_ref, k_ref, v_ref, km_ref, o_ref, lse_ref, bias_ref, ka_sc, qa_sc, s_sc, p_sc):
        lse_ref[0, heads:, :] = jnp.zeros((8 - heads, seq), F32)
        for hh in range(heads):
            ls = slice(HEAD_DIM * hh, HEAD_DIM * (hh + 1))
            _store_keys(ka_sc, k_ref, ls)
            vb = v_ref[:, ls]
            km = km_ref[:, ls]
            for i in range(nb):
                rs = slice(MOBA_BLOCK * i, MOBA_BLOCK * (i + 1))
                width = MOBA_BLOCK * (i + 1)
                qf = q_ref[rs, ls]
                bias = _block_bias(qf, km, i)
                bias_ref[rs, ls] = bias
                qa_sc[:, 0:HEAD_DIM] = (qf * HEAD_DIM ** -0.5).astype(BF16)
                qa_sc[:, HEAD_DIM:] = bias
                s_sc[:, 0:width] = _dot_nt(qa_sc[...], ka_sc[0:width, :])
                s_sc[:, rs] = jnp.where(_causal((MOBA_BLOCK, MOBA_BLOCK)), s_sc[:, rs], NEG)
                chunks = [slice(SCORE_CHUNK * c, SCORE_CHUNK * (c + 1)) for c in range(width // SCORE_CHUNK)]
                top = s_sc[:, chunks[0]]
                for c in chunks[1:]:
                    top = jnp.maximum(top, s_sc[:, c])
                m = jnp.max(top, axis=1, keepdims=True)
                total = jnp.zeros((MOBA_BLOCK, SCORE_CHUNK), F32)
                for c in chunks:
                    p = jnp.exp(s_sc[:, c] - m)
                    total = total + p
                    p_sc[:, c] = p.astype(BF16)
                l = jnp.sum(total, axis=1, keepdims=True)
                o_ref[rs, ls] = _dot(p_sc[:, 0:width], vb[0:width]) / l
                lse_ref[0, hh:hh + 1, rs] = _row_vector(m + jnp.log(l))

    return _grid_call(
        body, "attn_fwd", D_ATTN // LANE,
        in_specs=[pair, pair, pair, pl.BlockSpec((nb, LANE), lambda p: (0, p))],
        out_specs=[pair, pl.BlockSpec((1, 8, seq), lambda p: (p, 0, 0)), pair],
        out_shape=[_sds((seq, D_ATTN), F32), _sds((D_ATTN // LANE, 8, seq), F32), _sds((seq, D_ATTN), BF16)],
        operands=(q, k, v, kmean),
        scratch=[pltpu.VMEM((seq, LANE), BF16), pltpu.VMEM((MOBA_BLOCK, LANE), BF16),
                 pltpu.VMEM((MOBA_BLOCK, seq), F32), pltpu.VMEM((MOBA_BLOCK, seq), BF16)],
        tasks=tasks)


def _mix(o, u, g, x, wba, wbp, wout, w_pool, pool_scale, ln_g, ln_b, tasks=()):
    seq = x.shape[0]

    def body(o_ref, u_ref, uprev_ref, g_ref, x_ref, wba_ref, wbp_ref, wout_ref, wp_ref, ps_ref, lg_ref, lb_ref,
             ya_ref, yp_ref, pooled_ref, mixed_ref, ypre_ref, merged_ref, xhat_ref, rstd_ref, h_ref, hb_ref, ext):
        i = pl.program_id(0)
        ya = _dot(o_ref[...].astype(BF16), wba_ref[...])
        ucur = u_ref[...]
        ext[0:POOL_HALO, :] = jnp.where(i > 0, uprev_ref[...], 0.0)
        ext[POOL_HALO:, :] = ucur
        for grp, window in enumerate(POOL_WINDOWS):
            cols = slice(POOL_GROUP * grp, POOL_GROUP * (grp + 1))
            acc = ucur[:, cols]
            for kk in range(1, window):
                acc = acc + ext[pl.ds(POOL_HALO - kk, TOK), cols]
            pooled = acc / _pool_count(i * TOK, TOK, window) - ucur[:, cols]
            pooled_ref[:, cols] = pooled.astype(BF16)
            mixed_ref[:, cols] = _dot(pooled.astype(BF16), wp_ref[grp].astype(BF16))
        mixed = mixed_ref[...]
        ypre = (mixed * ps_ref[...]).astype(BF16)
        ypre_ref[...] = ypre
        yp = _dot(ypre, wbp_ref[...])
        ya_ref[...] = ya
        yp_ref[...] = yp
        merged = (g_ref[:, :D_MODEL] * ya + g_ref[:, D_MODEL:] * yp).astype(BF16)
        merged_ref[...] = merged
        r1 = ALPHA * x_ref[...] + _dot(merged, wout_ref[...])
        h, xhat, rstd = _ln_fwd(r1, lg_ref[...], lb_ref[...])
        xhat_ref[...] = xhat
        rstd_ref[...] = jnp.broadcast_to(rstd, (TOK, LANE))
        h_ref[...] = h
        hb_ref[...] = h.astype(BF16)

    halo = pl.BlockSpec((POOL_HALO, D_POOL), lambda i: (jnp.maximum(i * (TOK // POOL_HALO) - 1, 0), 0))
    return _grid_call(
        body, "mix", seq // TOK,
        in_specs=[_rows(D_ATTN), _rows(D_POOL), halo, _rows(2 * D_MODEL), _rows(D_MODEL),
                  _full(wba.shape), _full(wbp.shape), _full(wout.shape), _full(w_pool.shape),
                  _full((1, D_POOL)), _full((1, D_MODEL)), _full((1, D_MODEL))],
        out_specs=[_rows(D_MODEL), _rows(D_MODEL), _rows(D_POOL), _rows(D_POOL), _rows(D_POOL), _rows(D_MODEL),
                   _rows(D_MODEL), _rows(LANE), _rows(D_MODEL), _rows(D_MODEL)],
        out_shape=[_sds((seq, D_MODEL), F32), _sds((seq, D_MODEL), F32), _sds((seq, D_POOL), BF16),
                   _sds((seq, D_POOL), F32), _sds((seq, D_POOL), BF16), _sds((seq, D_MODEL), BF16),
                   _sds((seq, D_MODEL), F32), _sds((seq, LANE), F32), _sds((seq, D_MODEL), F32),
                   _sds((seq, D_MODEL), BF16)],
        operands=(o, u, u, g, x, wba, wbp, wout, w_pool, pool_scale, ln_g, ln_b),
        scratch=[pltpu.VMEM((TOK + POOL_HALO, D_POOL), F32)], tasks=tasks)


def _ffn_up(hb, wgt, wut, conv_w, conv_b, tasks=()):
    seq = hb.shape[0]
    wblk = pl.BlockSpec((FF_CHUNK, D_MODEL), lambda c: (c, 0))
    cblk = lambda rows: pl.BlockSpec((rows, FF_CHUNK), lambda c: (0, c))
    oblk = pl.BlockSpec((seq, FF_CHUNK), lambda c: (0, c))

    def body(h_ref, wg_ref, wu_ref, cw_ref, cb_ref, a_ref, u_ref, act_ref):
        h = h_ref[...]
        a = _dot_nt(h, wg_ref[...])
        u = _dot_nt(h, wu_ref[...])
        a_ref[...] = a
        u_ref[...] = u
        ac = _conv(a, cw_ref[...], cb_ref[...])
        act_ref[...] = (ac * _normal_cdf(ac) * u).astype(BF16)

    return _grid_call(
        body, "ffn_up", D_FF // FF_CHUNK,
        in_specs=[_full(hb.shape), wblk, wblk, cblk(3), cblk(1)],
        out_specs=[oblk, oblk, oblk],
        out_shape=[_sds((seq, D_FF), F32), _sds((seq, D_FF), F32), _sds((seq, D_FF), BF16)],
        operands=(hb, wgt, wut, conv_w, conv_b), tasks=tasks)


def _ffn_down(act, wd, h, target, ln_g, ln_b):
    seq = h.shape[0]

    def body(act_ref, wd_ref, h_ref, t_ref, lg_ref, lb_ref, dr_ref, drb_ref, loss_ref, dg_ref, db_ref):
        i = pl.program_id(0)

        @pl.when(i == 0)
        def _():
            loss_ref[...] = jnp.zeros_like(loss_ref)
            dg_ref[...] = jnp.zeros_like(dg_ref)
            db_ref[...] = jnp.zeros_like(db_ref)

        r2 = ALPHA * h_ref[...] + _dot(act_ref[...], wd_ref[...])
        y, xhat, rstd = _ln_fwd(r2, lg_ref[...], lb_ref[...])
        diff = y - t_ref[...]
        loss_ref[...] += jnp.sum(diff * diff) * (0.5 / D_MODEL)
        dy = diff * (1.0 / D_MODEL)
        dg_ref[...] += jnp.sum(dy * xhat, axis=0, keepdims=True)
        db_ref[...] += jnp.sum(dy, axis=0, keepdims=True)
        dr = _ln_bwd(dy, xhat, rstd, lg_ref[...])
        dr_ref[...] = dr
        drb_ref[...] = dr.astype(BF16)

    vec = pl.BlockSpec((1, D_MODEL), lambda i: (0, 0))
    return pl.pallas_call(
        body, name="ffn_down", grid=(seq // TOK,),
        in_specs=[_rows(D_FF), _full(wd.shape), _rows(D_MODEL), _rows(D_MODEL), _full((1, D_MODEL)), _full((1, D_MODEL))],
        out_specs=[_rows(D_MODEL), _rows(D_MODEL), pl.BlockSpec((8, LANE), lambda i: (0, 0)), vec, vec],
        out_shape=[_sds((seq, D_MODEL), F32), _sds((seq, D_MODEL), BF16), _sds((8, LANE), F32),
                   _sds((1, D_MODEL), F32), _sds((1, D_MODEL), F32)],
        compiler_params=_params("arbitrary"),
    )(act, wd, h, target, ln_g, ln_b)


def _ffn_bwd(drb, hb, a, u, wd, conv_w, conv_b):
    seq = hb.shape[0]
    wblk = pl.BlockSpec((FF_CHUNK, D_MODEL), lambda c: (c, 0))
    cblk = lambda rows: pl.BlockSpec((rows, FF_CHUNK), lambda c: (0, c))
    sblk = pl.BlockSpec((seq, FF_CHUNK), lambda c: (0, c))

    def body(dr_ref, h_ref, a_ref, u_ref, wd_ref, cw_ref, cb_ref, da_ref, du_ref, dwd_ref, dwg_ref, dwu_ref, dc_ref):
        dr = dr_ref[...]
        h = h_ref[...]
        a = a_ref[...]
        u = u_ref[...]
        cw = cw_ref[...]
        dact = _dot_nt(dr, wd_ref[...])
        ac = _conv(a, cw, cb_ref[...])
        cdf = _normal_cdf(ac)
        gelu = ac * cdf
        dwd_ref[...] = _dot_tn((gelu * u).astype(BF16), dr).astype(BF16)
        du = (dact * gelu).astype(BF16)
        dac = dact * u * _gelu_derivative(ac, cdf)
        da = (cw[2:3, :] * dac + cw[1:2, :] * _shift_up(dac, 1) + cw[0:1, :] * _shift_up(dac, 2)).astype(BF16)
        da_ref[...] = da
        du_ref[...] = du
        dwg_ref[...] = _dot_tn(da, h).astype(BF16)
        dwu_ref[...] = _dot_tn(du, h).astype(BF16)
        dc_ref[0:1, :] = jnp.sum(dac * _shift_down(a, 2), axis=0, keepdims=True)
        dc_ref[1:2, :] = jnp.sum(dac * _shift_down(a, 1), axis=0, keepdims=True)
        dc_ref[2:3, :] = jnp.sum(dac * a, axis=0, keepdims=True)
        dc_ref[3:4, :] = jnp.sum(dac, axis=0, keepdims=True)
        dc_ref[4:8, :] = jnp.zeros((4, FF_CHUNK), F32)

    return pl.pallas_call(
        body, name="ffn_bwd", grid=(D_FF // FF_CHUNK,),
        in_specs=[_full(drb.shape), _full(hb.shape), sblk, sblk, wblk, cblk(3), cblk(1)],
        out_specs=[sblk, sblk, wblk, wblk, wblk, cblk(8)],
        out_shape=[_sds((seq, D_FF), BF16), _sds((seq, D_FF), BF16), _sds((D_FF, D_MODEL), BF16),
                   _sds((D_FF, D_MODEL), BF16), _sds((D_FF, D_MODEL), BF16), _sds((8, D_FF), F32)],
        compiler_params=_params("parallel"),
    )(drb, hb, a, u, wd, conv_w, conv_b)


def _ln1_bwd(dr2, da, du, wgt, wut, xhat, rstd, ln_g, tasks=()):
    seq = dr2.shape[0]

    def body(dr2_ref, da_ref, du_ref, wg_ref, wu_ref, xhat_ref, rstd_ref, lg_ref, dr_ref, drb_ref, dg_ref, db_ref):
        @pl.when(pl.program_id(0) == 0)
        def _():
            dg_ref[...] = jnp.zeros_like(dg_ref)
            db_ref[...] = jnp.zeros_like(db_ref)

        dh = ALPHA * dr2_ref[...] + _dot(da_ref[...], wg_ref[...]) + _dot(du_ref[...], wu_ref[...])
        xhat = xhat_ref[...]
        dg_ref[...] += jnp.sum(dh * xhat, axis=0, keepdims=True)
        db_ref[...] += jnp.sum(dh, axis=0, keepdims=True)
        dr = _ln_bwd(dh, xhat, rstd_ref[:, 0:1], lg_ref[...])
        dr_ref[...] = dr
        drb_ref[...] = dr.astype(BF16)

    vec = pl.BlockSpec((1, D_MODEL), lambda i: (0, 0))
    return _grid_call(
        body, "ln1_bwd", seq // TOK,
        in_specs=[_rows(D_MODEL), _rows(D_FF), _rows(D_FF), _full(wgt.shape), _full(wut.shape), _rows(D_MODEL),
                  _rows(LANE), _full((1, D_MODEL))],
        out_specs=[_rows(D_MODEL), _rows(D_MODEL), vec, vec],
        out_shape=[_sds((seq, D_MODEL), F32), _sds((seq, D_MODEL), BF16), _sds((1, D_MODEL), F32),
                   _sds((1, D_MODEL), F32)],
        operands=(dr2, da, du, wgt, wut, xhat, rstd, ln_g), tasks=tasks)


def _mix_bwd(drb, ya, yp, g, mixed, wout, wba, wbp, w_pool, pool_scale, tasks=()):
    seq = drb.shape[0]

    def body(dr_ref, ya_ref, yp_ref, g_ref, mixed_ref, wout_ref, wba_ref, wbp_ref, wp_ref, ps_ref,
             dzg_ref, dya_ref, dyp_ref, do_ref, dmixed_ref, dpooled_ref, dbg_ref, dps_ref):
        @pl.when(pl.program_id(0) == 0)
        def _():
            dbg_ref[...] = jnp.zeros_like(dbg_ref)
            dps_ref[...] = jnp.zeros_like(dps_ref)

        dmerged = _dot_nt(dr_ref[...], wout_ref[...])
        ga, gp = g_ref[:, :D_MODEL], g_ref[:, D_MODEL:]
        dzga = dmerged * ya_ref[...] * ga * (1.0 - ga)
        dzgp = dmerged * yp_ref[...] * gp * (1.0 - gp)
        dzg_ref[:, :D_MODEL] = dzga.astype(BF16)
        dzg_ref[:, D_MODEL:] = dzgp.astype(BF16)
        dbg_ref[:, :D_MODEL] += jnp.sum(dzga, axis=0, keepdims=True)
        dbg_ref[:, D_MODEL:] += jnp.sum(dzgp, axis=0, keepdims=True)
        dya = (dmerged * ga).astype(BF16)
        dyp = (dmerged * gp).astype(BF16)
        dya_ref[...] = dya
        dyp_ref[...] = dyp
        do_ref[...] = _dot_nt(dya, wba_ref[...])
        dypre = _dot_nt(dyp, wbp_ref[...])
        dps_ref[...] += jnp.sum(dypre * mixed_ref[...], axis=0, keepdims=True)
        dmixed = (dypre * ps_ref[...]).astype(BF16)
        dmixed_ref[...] = dmixed
        for grp in range(len(POOL_WINDOWS)):
            cols = slice(POOL_GROUP * grp, POOL_GROUP * (grp + 1))
            dpooled_ref[:, cols] = _dot_nt(dmixed[:, cols], wp_ref[grp].astype(BF16))

    return _grid_call(
        body, "mix_bwd", seq // TOK,
        in_specs=[_rows(D_MODEL), _rows(D_MODEL), _rows(D_MODEL), _rows(2 * D_MODEL), _rows(D_POOL),
                  _full(wout.shape), _full(wba.shape), _full(wbp.shape), _full(w_pool.shape), _full((1, D_POOL))],
        out_specs=[_rows(2 * D_MODEL), _rows(D_MODEL), _rows(D_MODEL), _rows(D_ATTN), _rows(D_POOL), _rows(D_POOL),
                   pl.BlockSpec((1, 2 * D_MODEL), lambda i: (0, 0)), pl.BlockSpec((1, D_POOL), lambda i: (0, 0))],
        out_shape=[_sds((seq, 2 * D_MODEL), BF16), _sds((seq, D_MODEL), BF16), _sds((seq, D_MODEL), BF16),
                   _sds((seq, D_ATTN), F32), _sds((seq, D_POOL), BF16), _sds((seq, D_POOL), F32),
                   _sds((1, 2 * D_MODEL), F32), _sds((1, D_POOL), F32)],
        operands=(drb, ya, yp, g, mixed, wout, wba, wbp, w_pool, pool_scale), tasks=tasks)


def _attn_bwd(q, k, v, bias, o, lse, do, cos, sin, tasks=()):
    seq = q.shape[0]
    nb = seq // MOBA_BLOCK
    pair = pl.BlockSpec((seq, LANE), lambda p: (0, p))
    table = pl.BlockSpec((seq, LANE), lambda p: (0, 0))
    scale = HEAD_DIM ** -0.5

    def body(q_ref, k_ref, v_ref, bias_ref, o_ref, lse_ref, do_ref, cos_ref, sin_ref, dq_ref, dk_ref, dv_ref,
             dq_acc, dk_acc, dv_acc, dk_head, dv_head, ka_sc, qa_sc, s_sc, dp_sc, p_sc, ds_sc):
        for hh in range(LANE // HEAD_DIM):
            ls = slice(HEAD_DIM * hh, HEAD_DIM * (hh + 1))
            _store_keys(ka_sc, k_ref, ls)
            vb = v_ref[:, ls]
            dk_head[...] = jnp.zeros_like(dk_head)
            dv_head[...] = jnp.zeros_like(dv_head)
            for i in range(nb):
                rs = slice(MOBA_BLOCK * i, MOBA_BLOCK * (i + 1))
                width = MOBA_BLOCK * (i + 1)
                qa_sc[:, 0:HEAD_DIM] = (q_ref[rs, ls] * scale).astype(BF16)
                qa_sc[:, HEAD_DIM:] = bias_ref[rs, ls]
                s_sc[0:width, :] = _dot_nt(ka_sc[0:width, :], qa_sc[...])
                s_sc[rs, :] = jnp.where(_causal((MOBA_BLOCK, MOBA_BLOCK), transposed=True), s_sc[rs, :], NEG)
                dob = do_ref[rs, ls]
                delta = _row_vector(jnp.sum(dob * o_ref[rs, ls], axis=1, keepdims=True))
                lse_row = lse_ref[0, hh:hh + 1, rs]
                dob16 = dob.astype(BF16)
                dp_sc[0:width, :] = _dot_nt(vb[0:width], dob16)
                for c in range(width // SCORE_CHUNK):
                    rows = slice(SCORE_CHUNK * c, SCORE_CHUNK * (c + 1))
                    p = jnp.exp(s_sc[rows, :] - lse_row)
                    p_sc[rows, :] = p.astype(BF16)
                    ds_sc[rows, :] = (p * (dp_sc[rows, :] - delta)).astype(BF16)
                dv_head[0:width, :] += _dot(p_sc[0:width, :], dob16)
                dk_head[0:width, :] += _dot(ds_sc[0:width, :], qa_sc[:, 0:HEAD_DIM])
                dq_acc[rs, ls] = _dot_tn(ds_sc[0:width, :], ka_sc[0:width, 0:HEAD_DIM]) * scale
            dk_acc[:, ls] = dk_head[...]
            dv_acc[:, ls] = dv_head[...]
        cos_t, sin_t = cos_ref[...], sin_ref[...]
        dq_ref[...] = _rope_transposed(dq_acc[...], cos_t, sin_t).astype(BF16)
        dk_ref[...] = _rope_transposed(dk_acc[...], cos_t, sin_t).astype(BF16)
        dv_ref[...] = dv_acc[...].astype(BF16)

    return _grid_call(
        body, "attn_bwd", D_ATTN // LANE,
        in_specs=[pair, pair, pair, pair, pair, pl.BlockSpec((1, 8, seq), lambda p: (p, 0, 0)), pair, table, table],
        out_specs=[pair, pair, pair], out_shape=[_sds((seq, D_ATTN), BF16)] * 3,
        operands=(q, k, v, bias, o, lse, do, cos, sin),
        scratch=[pltpu.VMEM((seq, LANE), F32)] * 3 + [pltpu.VMEM((seq, HEAD_DIM), F32)] * 2
        + [pltpu.VMEM((seq, LANE), BF16), pltpu.VMEM((MOBA_BLOCK, LANE), BF16)]
        + [pltpu.VMEM((seq, MOBA_BLOCK), F32)] * 2 + [pltpu.VMEM((seq, MOBA_BLOCK), BF16)] * 2,
        tasks=tasks)


def _in_bwd(dq, dk, dv, dpooled, dzg, dr1, win, tasks=()):
    seq = dr1.shape[0]
    nt = seq // TOK

    def body(dq_ref, dk_ref, dv_ref, dp_ref, dpnext_ref, dzg_ref, dr_ref, win_ref, dx_ref, dz_ref, ext):
        i = pl.program_id(0)
        dp = dp_ref[...]
        dpn = jnp.where(i < nt - 1, dpnext_ref[...], 0.0)
        for grp, window in enumerate(POOL_WINDOWS):
            cols = slice(POOL_GROUP * grp, POOL_GROUP * (grp + 1))
            ext[0:TOK, cols] = dp[:, cols] / _pool_count(i * TOK, TOK, window)
            ext[TOK:, cols] = dpn[:, cols] / _pool_count((i + 1) * TOK, POOL_HALO, window)
        for grp, window in enumerate(POOL_WINDOWS):
            cols = slice(POOL_GROUP * grp, POOL_GROUP * (grp + 1))
            acc = ext[0:TOK, cols] - dp[:, cols]
            for kk in range(1, window):
                acc = acc + ext[pl.ds(kk, TOK), cols]
            dz_ref[:, 3 * D_ATTN + POOL_GROUP * grp:3 * D_ATTN + POOL_GROUP * (grp + 1)] = acc.astype(BF16)
        dz_ref[:, 0:D_ATTN] = dq_ref[...]
        dz_ref[:, D_ATTN:2 * D_ATTN] = dk_ref[...]
        dz_ref[:, 2 * D_ATTN:3 * D_ATTN] = dv_ref[...]
        dz_ref[:, 3 * D_ATTN + D_POOL:] = dzg_ref[...]
        dx = ALPHA * dr_ref[...]
        for n in range(N_DEV):
            dx = dx + _dot_nt(dz_ref[:, D_ATTN * n:D_ATTN * (n + 1)], win_ref[n])
        dx_ref[...] = dx

    halo = pl.BlockSpec((POOL_HALO, D_POOL),
                        lambda i: (jnp.minimum((i + 1) * (TOK // POOL_HALO), seq // POOL_HALO - 1), 0))
    return _grid_call(
        body, "in_bwd", nt,
        in_specs=[_rows(D_ATTN), _rows(D_ATTN), _rows(D_ATTN), _rows(D_POOL), halo, _rows(2 * D_MODEL),
                  _rows(D_MODEL), _full(win.shape)],
        out_specs=[_rows(D_MODEL), _rows(D_IN_PROJ)],
        out_shape=[_sds((seq, D_MODEL), F32), _sds((seq, D_IN_PROJ), BF16)],
        operands=(dq, dk, dv, dpooled, dpooled, dzg, dr1, win),
        scratch=[pltpu.VMEM((TOK + POOL_HALO, D_POOL), F32)], tasks=tasks)


def _dw_mixers(o, ypre, merged, dya, dyp, drb, pooled, dmixed, tasks=()):
    seq = o.shape[0]
    groups = len(POOL_WINDOWS)
    col = pl.BlockSpec((seq, LANE), lambda n: (0, n))
    grp = pl.BlockSpec((seq, POOL_GROUP), lambda n: (0, jnp.minimum(n, groups - 1)))
    owner = lambda rows, cols: pl.BlockSpec((1, rows, cols), lambda n: (n, 0, 0))

    def body(o_ref, ypre_ref, merged_ref, dya_ref, dyp_ref, dr_ref, pooled_ref, dmixed_ref,
             dba_ref, dbp_ref, dout_ref, dpool_ref, ob_sc):
        n = pl.program_id(0)

        @pl.when(n == 0)
        def _():
            ob_sc[...] = o_ref[...].astype(BF16)

        dba_ref[0] = _dot_tn(ob_sc[...], dya_ref[...]).astype(BF16)
        dbp_ref[0] = _dot_tn(ypre_ref[...], dyp_ref[...]).astype(BF16)
        dout_ref[0] = _dot_tn(merged_ref[...], dr_ref[...]).astype(BF16)

        @pl.when(n < groups)
        def _():
            dpool_ref[0] = _dot_tn(pooled_ref[...], dmixed_ref[...])

    return _grid_call(
        body, "dw_mixers", N_DEV,
        in_specs=[_full(o.shape), _full(ypre.shape), col, col, col, _full(drb.shape), grp, grp],
        out_specs=[owner(D_ATTN, LANE), owner(D_POOL, LANE), owner(D_MODEL // N_DEV, D_MODEL),
                   pl.BlockSpec((1, POOL_GROUP, POOL_GROUP), lambda n: (jnp.minimum(n, groups - 1), 0, 0))],
        out_shape=[_sds((N_DEV, D_ATTN, LANE), BF16), _sds((N_DEV, D_POOL, LANE), BF16),
                   _sds((N_DEV, D_MODEL // N_DEV, D_MODEL), BF16), _sds((groups, POOL_GROUP, POOL_GROUP), F32)],
        operands=(o, ypre, merged, dya, dyp, drb, pooled, dmixed),
        scratch=[pltpu.VMEM((seq, D_ATTN), BF16)], tasks=tasks)


def _tn_matmul(name, a, b, out_shape, out_dtype, steps, a_spec, b_spec, o_spec, tasks=()):
    def body(a_ref, b_ref, o_ref):
        r = _dot_tn(a_ref[...].astype(BF16), b_ref[...].astype(BF16))
        o_ref[...] = r.reshape(o_ref.shape).astype(o_ref.dtype)

    (out,), results = _grid_call(body, name, steps, in_specs=[a_spec, b_spec], out_specs=[o_spec],
                                 out_shape=[_sds(out_shape, out_dtype)], operands=(a, b), tasks=tasks)
    return out, results


def _place():
    return lax.axis_index("x"), lax.axis_index("y"), lax.axis_index("c")


def _other_chips(x, y):
    return [(1 - x, y), (x, 1 - y), (1 - x, 1 - y)]


DMA_SEMS = pltpu.SemaphoreType.DMA


class _AllGather:
    def __init__(self, shards, lag=0):
        self.operands = list(shards)
        self.n = len(shards)
        self.lag = lag
        self.out_shape = [_sds((N_DEV, *s.shape), s.dtype) for s in shards]
        self.sems = [DMA_SEMS((7 * self.n,)), DMA_SEMS((7 * self.n,)), DMA_SEMS((self.n,))]

    def _copy(self, refs, a, k, block, to, from_input=False):
        ins, outs, (send_sems, recv_sems, _) = refs
        px, py, pc = block
        dst = outs[a].at[4 * px + 2 * py + pc]
        return pltpu.make_async_remote_copy(
            src_ref=ins[a] if from_input else dst, dst_ref=dst,
            send_sem=send_sems.at[7 * a + k], recv_sem=recv_sems.at[7 * a + k],
            device_id=to, device_id_type=MESH)

    def _local(self, refs, a):
        ins, outs, (_, _, local_sems) = refs
        x, y, c = _place()
        return pltpu.make_async_copy(ins[a], outs[a].at[4 * x + 2 * y + c], local_sems.at[a])

    def _pass_on(self, refs, a):
        x, y, c = _place()
        origin = ((x + 1 - c) % 2, (y + c) % 2, c)
        target = ((x + c) % 2, (y + 1 - c) % 2, c)
        return self._copy(refs, a, 3, origin, target)

    def start(self, refs):
        x, y, c = _place()
        for a in range(self.n):
            self._local(refs, a).start()
        for a in range(self.n):
            self._copy(refs, a, 0, (x, y, c), (x, y, 1 - c), True).start()
            for j, chip in enumerate(_other_chips(x, y)[:2]):
                self._copy(refs, a, 1 + j, (x, y, c), (*chip, c), True).start()

    def middle(self, refs):
        x, y, c = _place()
        me, sibling = (x, y, c), (x, y, 1 - c)
        chips = _other_chips(x, y)
        for a in range(self.n):
            for j in range(2):
                self._copy(refs, a, 1 + j, (*chips[j], c), me).wait_recv()
        for a in range(self.n):
            self._pass_on(refs, a).start()
            for j in range(2):
                self._copy(refs, a, 4 + j, (*chips[j], c), sibling).start()

    def late(self, refs):
        x, y, c = _place()
        diagonal = (1 - x, 1 - y, c)
        for a in range(self.n):
            self._copy(refs, a, 3, diagonal, (x, y, c)).wait_recv()
            self._copy(refs, a, 6, diagonal, (x, y, 1 - c)).start()

    def finish(self, refs):
        x, y, c = _place()
        me, sibling = (x, y, c), (x, y, 1 - c)
        chips = _other_chips(x, y)
        for a in range(self.n):
            self._copy(refs, a, 0, sibling, me).wait_recv()
            for j, chip in enumerate(chips):
                self._copy(refs, a, 4 + j, (*chip, 1 - c), me).wait_recv()
        for a in range(self.n):
            self._copy(refs, a, 0, me, sibling, True).wait_send()
            for j, chip in enumerate(chips[:2]):
                self._copy(refs, a, 1 + j, me, (*chip, c), True).wait_send()
            self._pass_on(refs, a).wait_send()
            for j, chip in enumerate(chips):
                self._copy(refs, a, 4 + j, (*chip, c), sibling).wait_send()
            self._local(refs, a).wait()


class _SiblingSend:
    def __init__(self, partials):
        self.operands = list(partials)
        self.n = len(partials)
        self.out_shape = [_sds((4, *p.shape[1:]), p.dtype) for p in partials]
        self.sems = [DMA_SEMS((4 * self.n,)), DMA_SEMS((4 * self.n,))]

    def _copy(self, refs, a, q):
        ins, outs, (send_sems, recv_sems) = refs
        x, y, c = _place()
        return pltpu.make_async_remote_copy(
            src_ref=ins[a].at[2 * q + 1 - c], dst_ref=outs[a].at[q],
            send_sem=send_sems.at[4 * a + q], recv_sem=recv_sems.at[4 * a + q],
            device_id=(x, y, 1 - c), device_id_type=MESH)

    def start(self, refs):
        for a in range(self.n):
            for q in range(4):
                self._copy(refs, a, q).start()

    def middle(self, refs):
        pass

    def finish(self, refs):
        for a in range(self.n):
            for q in range(4):
                self._copy(refs, a, q).wait()


class _ChipScatter:
    def __init__(self, chip_partials):
        self.operands = list(chip_partials)
        self.n = len(chip_partials)
        self.out_shape = [_sds(p.shape, p.dtype) for p in chip_partials]
        self.sems = [DMA_SEMS((3 * self.n,)), DMA_SEMS((3 * self.n,)), DMA_SEMS((self.n,))]

    def _copy(self, refs, a, k, arrival=False):
        ins, outs, (send_sems, recv_sems, _) = refs
        x, y, c = _place()
        px, py = _other_chips(x, y)[k]
        mine, theirs = 2 * x + y, 2 * px + py
        return pltpu.make_async_remote_copy(
            src_ref=ins[a].at[mine if arrival else theirs], dst_ref=outs[a].at[theirs if arrival else mine],
            send_sem=send_sems.at[3 * a + k], recv_sem=recv_sems.at[3 * a + k],
            device_id=(px, py, c), device_id_type=MESH)

    def _local(self, refs, a):
        ins, outs, (_, _, local_sems) = refs
        x, y, _ = _place()
        return pltpu.make_async_copy(ins[a].at[2 * x + y], outs[a].at[2 * x + y], local_sems.at[a])

    def start(self, refs):
        for a in range(self.n):
            self._local(refs, a).start()
            for k in range(3):
                self._copy(refs, a, k).start()

    def middle(self, refs):
        pass

    def finish(self, refs):
        for a in range(self.n):
            for k in range(3):
                self._copy(refs, a, k, arrival=True).wait_recv()
        for a in range(self.n):
            for k in range(3):
                self._copy(refs, a, k).wait_send()
            self._local(refs, a).wait()


class _DirectScatter:
    def __init__(self, partials):
        self.operands = list(partials)
        self.n = len(partials)
        self.out_shape = [_sds(p.shape, p.dtype) for p in partials]
        self.sems = [DMA_SEMS((7 * self.n,)), DMA_SEMS((7 * self.n,)), DMA_SEMS((self.n,))]

    def _copy(self, refs, a, k, arrival=False):
        ins, outs, (send_sems, recv_sems, _) = refs
        x, y, c = _place()
        peer = [(x, y, 1 - c), (1 - x, y, c), (x, 1 - y, c), (1 - x, 1 - y, c),
                (1 - x, y, 1 - c), (x, 1 - y, 1 - c), (1 - x, 1 - y, 1 - c)][k]
        mine, theirs = 4 * x + 2 * y + c, 4 * peer[0] + 2 * peer[1] + peer[2]
        return pltpu.make_async_remote_copy(
            src_ref=ins[a].at[mine if arrival else theirs], dst_ref=outs[a].at[theirs if arrival else mine],
            send_sem=send_sems.at[7 * a + k], recv_sem=recv_sems.at[7 * a + k],
            device_id=peer, device_id_type=MESH)

    def _local(self, refs, a):
        ins, outs, (_, _, local_sems) = refs
        x, y, c = _place()
        return pltpu.make_async_copy(ins[a].at[4 * x + 2 * y + c], outs[a].at[4 * x + 2 * y + c], local_sems.at[a])

    def start(self, refs):
        for a in range(self.n):
            self._local(refs, a).start()
            for k in range(7):
                self._copy(refs, a, k).start()

    def middle(self, refs):
        pass

    def finish(self, refs):
        for a in range(self.n):
            for k in range(7):
                self._copy(refs, a, k, arrival=True).wait_recv()
        for a in range(self.n):
            for k in range(7):
                self._copy(refs, a, k).wait_send()
            self._local(refs, a).wait()


def _task_args(tasks):
    hbm = pl.BlockSpec(memory_space=pl.ANY)
    operands = [o for t in tasks for o in t.operands]
    out_shape = [s for t in tasks for s in t.out_shape]
    sems = [s for t in tasks for s in t.sems]
    return operands, [hbm] * len(operands), out_shape, [hbm] * len(out_shape), sems


def _task_refs(tasks, ins, outs, sems):
    per_task = []
    for t in tasks:
        ni, no, ns = len(t.operands), len(t.out_shape), len(t.sems)
        per_task.append((ins[:ni], outs[:no], sems[:ns]))
        ins, outs, sems = ins[ni:], outs[no:], sems[ns:]
    return per_task


def _task_results(tasks, outs):
    res = []
    for t in tasks:
        res.append(list(outs[:len(t.out_shape)]))
        outs = outs[len(t.out_shape):]
    return res


def _carry(body, tasks, n_in, n_out, n_scratch, steps):
    if not tasks:
        return body
    t_in = sum(len(t.operands) for t in tasks)
    t_out = sum(len(t.out_shape) for t in tasks)

    def wrapped(*refs):
        ins, refs = refs[:n_in], refs[n_in:]
        t_ins, refs = refs[:t_in], refs[t_in:]
        outs, refs = refs[:n_out], refs[n_out:]
        t_outs, refs = refs[:t_out], refs[t_out:]
        scratch, t_sems = refs[:n_scratch], refs[n_scratch:]
        per_task = _task_refs(tasks, t_ins, t_outs, t_sems)
        step = pl.program_id(0)

        @pl.when(step == 0)
        def _():
            for t, r in zip(tasks, per_task):
                t.start(r)

        for t, r in zip(tasks, per_task):
            pl.when(step == max(steps - 1 - getattr(t, "lag", 0), 0))(functools.partial(t.middle, r))
            if hasattr(t, "late"):
                pl.when(step == steps - 1)(functools.partial(t.late, r))

        body(*ins, *outs, *scratch)

        @pl.when(step == steps - 1)
        def _():
            for t, r in zip(tasks, per_task):
                t.finish(r)

    return wrapped


def _exchange(name, tasks):
    operands, in_specs, out_shape, out_specs, sems = _task_args(tasks)

    def body(*refs):
        ni, no = len(operands), len(out_shape)
        per_task = _task_refs(tasks, refs[:ni], refs[ni:ni + no], refs[ni + no:])
        for phase in ("start", "middle", "late", "finish"):
            for t, r in zip(tasks, per_task):
                if hasattr(t, phase):
                    getattr(t, phase)(r)

    outs = pl.pallas_call(body, name=name, in_specs=in_specs, out_specs=out_specs, out_shape=out_shape,
                          scratch_shapes=sems)(*operands)
    return _task_results(tasks, outs)


def _row_tile(rows, cols, whole_up_to=256 * 1024):
    if rows * cols <= whole_up_to:
        return rows
    for t in (256, 176, 128, 64, 32, 16, 8):
        if rows % t == 0:
            return t
    return rows


def _pair_sum(name, partials, from_sibling):
    n = len(partials)
    _, rows, cols = partials[0].shape
    tile = _row_tile(rows, cols, 512 * 1024)

    def body(*refs):
        south = lax.axis_index("c") == 0
        for p_ref, s_ref, o_ref in zip(refs[:n], refs[n:2 * n], refs[2 * n:]):
            mine = jnp.where(south, p_ref[0, 0].astype(F32), p_ref[0, 1].astype(F32))
            o_ref[0] = (mine + s_ref[0].astype(F32)).astype(o_ref.dtype)

    blk = pl.BlockSpec((1, tile, cols), lambda q, i: (q, i, 0))
    return pl.pallas_call(
        body, name=name, grid=(4, rows // tile),
        in_specs=[pl.BlockSpec((1, 2, tile, cols), lambda q, i: (q, 0, i, 0))] * n + [blk] * n,
        out_specs=[blk] * n, out_shape=[_sds(s.shape, s.dtype) for s in from_sibling],
        compiler_params=_params("parallel", "parallel"),
    )(*[p.reshape(4, 2, rows, cols) for p in partials], *from_sibling)


def _sum_leading(name, stacked):
    parts, rows, cols = stacked.shape
    tile = _row_tile(rows, cols, (512 if parts <= 4 else 256) * 1024)

    def body(s_ref, o_ref):
        acc = s_ref[0].astype(F32)
        for d in range(1, parts):
            acc = acc + s_ref[d].astype(F32)
        o_ref[...] = acc

    return pl.pallas_call(
        body, name=name, grid=(rows // tile,),
        in_specs=[pl.BlockSpec((parts, tile, cols), lambda i: (0, i, 0))],
        out_specs=pl.BlockSpec((tile, cols), lambda i: (i, 0)),
        out_shape=_sds((rows, cols), F32),
        compiler_params=_params("parallel"),
    )(stacked)


def _adamw_math(w, g, m, v):
    nm = ADAM_B1 * m + (1.0 - ADAM_B1) * g
    nv = ADAM_B2 * v + (1.0 - ADAM_B2) * (g * g)
    m_hat = nm / (1.0 - ADAM_B1 ** ADAM_STEP)
    v_hat = nv / (1.0 - ADAM_B2 ** ADAM_STEP)
    return -ADAM_LR * (m_hat / (jnp.sqrt(v_hat) + ADAM_EPS) + ADAM_WD * w), nm, nv


def _adamw(name, w, g, m, v):
    rows, cols = w.shape
    tile = _row_tile(rows, cols)

    def body(w_ref, g_ref, m_ref, v_ref, d_ref, nm_ref, nv_ref):
        d_ref[...], nm_ref[...], nv_ref[...] = _adamw_math(w_ref[...], g_ref[...], m_ref[...], v_ref[...])

    blk = pl.BlockSpec((tile, cols), lambda i: (i, 0))
    return pl.pallas_call(
        body, name=name, grid=(rows // tile,),
        in_specs=[blk] * 4, out_specs=[blk] * 3,
        out_shape=[_sds((rows, cols), F32)] * 3,
        compiler_params=_params("parallel"),
    )(w, g, m, v)


def _sum_adamw(name, params, tasks=()):
    n = len(params)
    parts, rows, cols = params[0][0].shape
    tile = _row_tile(rows, cols)

    def body(*refs):
        ins, outs = refs[:4 * n], refs[4 * n:]
        for p in range(n):
            s_ref, w_ref, m_ref, v_ref = ins[4 * p:4 * p + 4]
            g_ref, d_ref, nm_ref, nv_ref = outs[4 * p:4 * p + 4]
            g = s_ref[0].astype(F32)
            for d in range(1, parts):
                g = g + s_ref[d].astype(F32)
            g_ref[...] = g
            d_ref[...], nm_ref[...], nv_ref[...] = _adamw_math(w_ref[...], g, m_ref[...], v_ref[...])

    blk = pl.BlockSpec((tile, cols), lambda i: (i, 0))
    outs, results = _grid_call(
        body, name, rows // tile,
        in_specs=([pl.BlockSpec((parts, tile, cols), lambda i: (0, i, 0))] + [blk] * 3) * n, out_specs=[blk] * (4 * n),
        out_shape=[_sds((rows, cols), F32)] * (4 * n),
        operands=[t for p in params for t in p], tasks=tasks)
    return [outs[4 * p:4 * p + 4] for p in range(n)], results


SMALL = ("b_gate", "w_pool", "pool_scale", "ln1_g", "ln1_b", "conv_b", "ln2_g", "ln2_b")
PACKED = SMALL + ("conv_w", "loss")
TILE = 8 * LANE


def _pack(parts):
    tiles = []
    for p in parts:
        flat = p.reshape(-1)
        tiles.append(jnp.pad(flat, (0, -flat.size % TILE)).reshape(-1, LANE))
    return jnp.concatenate(tiles, axis=0)


def _unpack(packed, shapes):
    out, at = [], 0
    for shape in shapes:
        size = math.prod(shape)
        rows = -(-size // TILE) * 8
        out.append(packed[at:at + rows].reshape(-1)[:size].reshape(shape))
        at += rows
    return out


MIXER = ("w_branch_attn", "w_branch_pool", "w_out", "conv_w")
FFN = ("w_ffn_gate_t", "w_ffn_up_t", "w_ffn_down")


def _columns(t):
    return jnp.transpose(t, (1, 0, 2)).reshape(t.shape[1], N_DEV * t.shape[2])


def _row_blocks(t):
    return t.reshape(N_DEV * t.shape[1], t.shape[2])


def _by_owner(t):
    return t.reshape(N_DEV, t.shape[0] // N_DEV, t.shape[1])


def _reduce_halves(names, partials, from_sibling):
    out = [None] * len(names)
    for shape in dict.fromkeys(p.shape for p in partials):
        group = [i for i, p in enumerate(partials) if p.shape == shape]
        sums = _pair_sum("pair_sum_" + names[group[0]], [partials[i] for i in group], [from_sibling[i] for i in group])
        for i, s in zip(group, sums):
            out[i] = s
    return out


def _local_step(x, target, shards, small):
    seq = x.shape[0]
    cos, sin = _rope_tables(seq)
    ((w_in_all,),) = _exchange("gather_w_in", [_AllGather([shards["w_in"]])])
    (xb, q, k, v, u, g, kmean), (mixer,) = _proj_in(
        x, w_in_all, small["b_gate"], cos, sin, tasks=[_AllGather([shards[n] for n in MIXER], lag=2)])
    wba, wbp, wout, conv_w = _columns(mixer[0]), _columns(mixer[1]), _row_blocks(mixer[2]), _columns(mixer[3])
    (o, lse, bias), ((wgt, wut),) = _attn_fwd(
        q, k, v, kmean.reshape(seq // MOBA_BLOCK, D_ATTN),
        tasks=[_AllGather([shards["w_ffn_gate_t"], shards["w_ffn_up_t"]], lag=1)])
    (ya, yp, pooled, mixed, ypre, merged, xhat1, rstd1, h1, h1b), _ = _mix(
        o, u, g, x, wba, wbp, wout, small["w_pool"], small["pool_scale"], small["ln1_g"], small["ln1_b"])
    wgt, wut = _row_blocks(wgt), _row_blocks(wut)
    (a, uf, act), ((wd,),) = _ffn_up(
        h1b, wgt, wut, conv_w, small["conv_b"], tasks=[_AllGather([shards["w_ffn_down"]], lag=4)])
    wd = _row_blocks(wd)
    dr2, dr2b, loss, dg2, db2 = _ffn_down(act, wd, h1, target, small["ln2_g"], small["ln2_b"])

    da, du, dwd, dwg, dwu, dconv = _ffn_bwd(dr2b, h1b, a, uf, wd, conv_w, small["conv_b"])
    ffn_partials = [_by_owner(dwg), _by_owner(dwu), _by_owner(dwd)]
    (dr1, dr1b, dg1, db1), (ffn_sibling,) = _ln1_bwd(
        dr2, da, du, wgt, wut, xhat1, rstd1, small["ln1_g"], tasks=[_SiblingSend(ffn_partials)])
    ffn_chip = _reduce_halves(FFN, ffn_partials, ffn_sibling)
    (dzg, dya, dyp, do, dmixed, dpooled, dbg, dps), (gate_landed,) = _mix_bwd(
        dr1b, ya, yp, g, mixed, wout, wba, wbp, small["w_pool"], small["pool_scale"],
        tasks=[_ChipScatter(ffn_chip[0:1])])
    (dw_ba, dw_bp, dw_out, dw_pool), _ = _dw_mixers(o, ypre, merged, dya, dyp, dr1b, pooled, dmixed)
    mixer_partials = [dw_ba, dw_bp, dw_out]
    (dq, dk, dv), (up_down_landed, mixer_sibling) = _attn_bwd(
        q, k, v, bias, o, lse, do, cos, sin, tasks=[_ChipScatter(ffn_chip[1:3]), _SiblingSend(mixer_partials)])
    mixer_chip = _reduce_halves(MIXER[:3], mixer_partials, mixer_sibling)
    (grad_x, dz), _ = _in_bwd(dq, dk, dv, dpooled, dzg, dr1, w_in_all)
    little = {"b_gate": dbg, "w_pool": dw_pool, "pool_scale": dps, "ln1_g": dg1, "ln1_b": db1, "conv_b": dconv[3:4],
              "ln2_g": dg2, "ln2_b": db2, "conv_w": dconv[0:3], "loss": loss}
    dw_in, (mixer_landed, (all_small,)) = _tn_matmul(
        "dw_in", xb, dz, (N_DEV, D_MODEL, D_ATTN), BF16, 2 * N_DEV,
        pl.BlockSpec((seq, 512), lambda s: (0, s % 2)), pl.BlockSpec((seq, D_ATTN), lambda s: (0, s // 2)),
        pl.BlockSpec((1, 512, D_ATTN), lambda s: (s // 2, s % 2, 0)),
        tasks=[_ChipScatter(mixer_chip), _AllGather([_pack([little[n] for n in PACKED])], lag=4)])

    landed = dict(zip(FFN + MIXER[:3], gate_landed + up_down_landed + mixer_landed))
    return grad_x, landed, dw_in, all_small


def kernel(x, w_in, b_gate, w_branch_attn, w_pool, pool_scale, w_branch_pool, w_out, ln1_g, ln1_b, w_ffn_gate, w_ffn_up, conv_w, conv_b, w_ffn_down, ln2_g, ln2_b, loss_target, m_w_in, m_b_gate, m_w_branch_attn, m_w_pool, m_pool_scale, m_w_branch_pool, m_w_out, m_ln1_g, m_ln1_b, m_w_ffn_gate, m_w_ffn_up, m_conv_w, m_conv_b, m_w_ffn_down, m_ln2_g, m_ln2_b, v_w_in, v_b_gate, v_w_branch_attn, v_w_pool, v_pool_scale, v_w_branch_pool, v_w_out, v_ln1_g, v_ln1_b, v_w_ffn_gate, v_w_ffn_up, v_conv_w, v_conv_b, v_w_ffn_down, v_ln2_g, v_ln2_b):
    me = 4 * lax.axis_index("x") + 2 * lax.axis_index("y") + lax.axis_index("c")
    weights = dict(w_in=w_in, b_gate=b_gate, w_branch_attn=w_branch_attn, w_pool=w_pool, pool_scale=pool_scale,
                   w_branch_pool=w_branch_pool, w_out=w_out, ln1_g=ln1_g, ln1_b=ln1_b, w_ffn_gate=w_ffn_gate,
                   w_ffn_up=w_ffn_up, conv_w=conv_w, conv_b=conv_b, w_ffn_down=w_ffn_down, ln2_g=ln2_g, ln2_b=ln2_b)
    m_in = dict(w_in=m_w_in, b_gate=m_b_gate, w_branch_attn=m_w_branch_attn, w_pool=m_w_pool,
                pool_scale=m_pool_scale, w_branch_pool=m_w_branch_pool, w_out=m_w_out, ln1_g=m_ln1_g, ln1_b=m_ln1_b,
                w_ffn_gate=m_w_ffn_gate, w_ffn_up=m_w_ffn_up, conv_w=m_conv_w, conv_b=m_conv_b,
                w_ffn_down=m_w_ffn_down, ln2_g=m_ln2_g, ln2_b=m_ln2_b)
    v_in = dict(w_in=v_w_in, b_gate=v_b_gate, w_branch_attn=v_w_branch_attn, w_pool=v_w_pool,
                pool_scale=v_pool_scale, w_branch_pool=v_w_branch_pool, w_out=v_w_out, ln1_g=v_ln1_g, ln1_b=v_ln1_b,
                w_ffn_gate=v_w_ffn_gate, w_ffn_up=v_w_ffn_up, conv_w=v_conv_w, conv_b=v_conv_b,
                w_ffn_down=v_w_ffn_down, ln2_g=v_ln2_g, ln2_b=v_ln2_b)
    weights = {n: a[0] for n, a in weights.items()}
    m_in = {n: a[0] for n, a in m_in.items()}
    v_in = {n: a[0] for n, a in v_in.items()}

    shards = {"w_in": weights["w_in"].astype(BF16), "w_branch_attn": weights["w_branch_attn"].astype(BF16),
              "w_branch_pool": weights["w_branch_pool"].astype(BF16), "w_out": weights["w_out"].astype(BF16),
              "w_ffn_gate_t": weights["w_ffn_gate"].T.astype(BF16), "w_ffn_up_t": weights["w_ffn_up"].T.astype(BF16),
              "w_ffn_down": weights["w_ffn_down"].astype(BF16), "conv_w": weights["conv_w"]}
    small = {"b_gate": weights["b_gate"][None], "w_pool": weights["w_pool"], "pool_scale": weights["pool_scale"][None],
             "ln1_g": weights["ln1_g"][None], "ln1_b": weights["ln1_b"][None], "conv_b": weights["conv_b"][None],
             "ln2_g": weights["ln2_g"][None], "ln2_b": weights["ln2_b"][None]}

    grad_x, landed, dw_in, all_small = _local_step(x[0], loss_target[0], shards, small)

    grads, delta, new_m, new_v = {}, {}, {}, {}

    def param(n, transposed=False):
        if transposed:
            return landed[n + "_t"], weights[n].T, m_in[n].T, v_in[n].T
        return landed[n], weights[n], m_in[n], v_in[n]

    def keep(n, updated, transposed=False):
        grads[n], delta[n], new_m[n], new_v[n] = (t.T for t in updated) if transposed else updated

    (gate,), ((w_in_sibling,),) = _sum_adamw(
        "update_w_ffn_gate", [param("w_ffn_gate", True)], tasks=[_SiblingSend([dw_in])])
    keep("w_ffn_gate", gate, True)
    w_in_chip = _reduce_halves(["w_in"], [dw_in], [w_in_sibling])
    (up, down), ((landed["w_in"],),) = _sum_adamw(
        "update_w_ffn_up_down", [param("w_ffn_up", True), param("w_ffn_down")], tasks=[_ChipScatter(w_in_chip)])
    keep("w_ffn_up", up, True)
    keep("w_ffn_down", down)
    (attn, pool), _ = _sum_adamw("update_w_branch", [param("w_branch_attn"), param("w_branch_pool")])
    keep("w_branch_attn", attn)
    keep("w_branch_pool", pool)
    for n in ("w_out", "w_in"):
        (updated,), _ = _sum_adamw("update_" + n, [param(n)])
        keep(n, updated)
    names = SMALL + ("conv_w",)
    small_sum = _sum_leading("sum_small", all_small)
    *small_grads, conv_w_grad, loss = _unpack(small_sum, [weights[n].shape for n in SMALL] + [(3, D_FF), (8, LANE)])
    loss = loss[0, 0]
    grads.update(zip(SMALL, small_grads))
    grads["conv_w"] = lax.dynamic_slice(conv_w_grad, (0, me * FF_SHARD), (3, FF_SHARD))
    flat = lambda d: _pack([d[n] for n in names])
    shapes = [weights[n].shape for n in names]
    for out, packed in zip((delta, new_m, new_v),
                           _adamw("adamw_small", flat(weights), flat(grads), flat(m_in), flat(v_in))):
        out.update(zip(names, _unpack(packed, shapes)))

    order = ("w_in", "b_gate", "w_branch_attn", "w_pool", "pool_scale", "w_branch_pool", "w_out", "ln1_g", "ln1_b",
             "w_ffn_gate", "w_ffn_up", "conv_w", "conv_b", "w_ffn_down", "ln2_g", "ln2_b")
    lead = lambda t: t[None]
    return (loss, lead(grad_x), *[lead(grads[n]) for n in order], *[lead(delta[n]) for n in order],
            *[lead(new_m[n]) for n in order], *[lead(new_v[n]) for n in order])
```

```python
import functools
import math

import jax
import jax.numpy as jnp
from jax import lax
from jax.experimental import pallas as pl
from jax.experimental.pallas import tpu as pltpu

F32 = jnp.float32
BF16 = jnp.bfloat16

D_MODEL = 1024
N_HEADS = 8
HEAD_DIM = 64
D_ATTN = N_HEADS * HEAD_DIM
MOBA_BLOCK = 256
MOBA_TOPK = 3
ROPE_THETA = 10000.0
POOL_WINDOWS = (2, 4, 8, 16)
POOL_GROUP = 128
D_POOL = len(POOL_WINDOWS) * POOL_GROUP
POOL_HALO = 16
D_FF = 2816
D_IN_PROJ = 3 * D_ATTN + D_POOL + 2 * D_MODEL
LN_EPS = 1e-5
ALPHA = 2.0 ** 0.25
NEG = -1e30
N_DEV = 8
FF_SHARD = D_FF // N_DEV

ADAM_LR = 0.001
ADAM_B1 = 0.9
ADAM_B2 = 0.999
ADAM_EPS = 1e-08
ADAM_WD = 0.01
ADAM_STEP = 10

TOK = 256
FF_CHUNK = 256
LANE = 128
VMEM_LIMIT = 56 * 1024 * 1024

MESH = pl.DeviceIdType.MESH
NT_DIMS = (((1,), (1,)), ((), ()))
TN_DIMS = (((0,), (0,)), ((), ()))


def _params(*sem):
    return pltpu.CompilerParams(dimension_semantics=sem or None, vmem_limit_bytes=VMEM_LIMIT)


def _full(shape):
    zeros = (0,) * len(shape)
    return pl.BlockSpec(shape, lambda *_: zeros, pipeline_mode=pl.Buffered(1))


def _rows(width, tile=TOK):
    return pl.BlockSpec((tile, width), lambda i: (i, 0))


def _sds(shape, dtype):
    return jax.ShapeDtypeStruct(shape, dtype)


def _dot(a, b):
    return jnp.dot(a, b, preferred_element_type=F32)


def _dot_nt(a, b):
    return lax.dot_general(a, b, NT_DIMS, preferred_element_type=F32)


def _dot_tn(a, b):
    return lax.dot_general(a, b, TN_DIMS, preferred_element_type=F32)


def _rope_tables(seq):
    half = HEAD_DIM // 2
    inv_freq = 1.0 / (ROPE_THETA ** (jnp.arange(half, dtype=F32) / half))
    ang = jnp.arange(seq, dtype=F32)[:, None] * inv_freq[None, :]
    cos, sin = jnp.cos(ang), jnp.sin(ang)
    return jnp.tile(cos, (1, 4)), jnp.tile(jnp.concatenate([-sin, sin], axis=1), (1, 2))


def _swap_halves(t):
    lane = lax.broadcasted_iota(jnp.int32, t.shape, 1)
    return jnp.where((lane % HEAD_DIM) < HEAD_DIM // 2, pltpu.roll(t, LANE - 32, 1), pltpu.roll(t, 32, 1))


def _rope(t, cos, sin):
    return t * cos + _swap_halves(t) * sin


def _rope_transposed(g, cos, sin):
    return g * cos + _swap_halves(g * sin)


def _ln_fwd(r, g, b):
    mu = jnp.mean(r, axis=-1, keepdims=True)
    xc = r - mu
    var = jnp.mean(xc * xc, axis=-1, keepdims=True)
    rstd = lax.rsqrt(var + LN_EPS)
    xhat = xc * rstd
    return xhat * g + b, xhat, rstd


def _ln_bwd(dy, xhat, rstd, g):
    dxh = dy * g
    m1 = jnp.mean(dxh, axis=-1, keepdims=True)
    m2 = jnp.mean(dxh * xhat, axis=-1, keepdims=True)
    return rstd * (dxh - m1 - xhat * m2)


def _normal_cdf(a):
    return 0.5 * (1.0 + lax.erf(a * (1.0 / math.sqrt(2.0))))


def _gelu_derivative(a, cdf):
    return cdf + a * (jnp.exp(-0.5 * a * a) * (1.0 / math.sqrt(2.0 * math.pi)))


def _shift_down(a, k):
    row = lax.broadcasted_iota(jnp.int32, a.shape, 0)
    return jnp.where(row >= k, pltpu.roll(a, k, 0), 0.0)


def _shift_up(a, k):
    n = a.shape[0]
    row = lax.broadcasted_iota(jnp.int32, a.shape, 0)
    return jnp.where(row < n - k, pltpu.roll(a, n - k, 0), 0.0)


def _conv(a, cw, cb):
    return cw[2:3, :] * a + cw[1:2, :] * _shift_down(a, 1) + cw[0:1, :] * _shift_down(a, 2) + cb


def _pool_count(first_row, rows, window):
    t = first_row + lax.broadcasted_iota(jnp.int32, (rows, 1), 0)
    return jnp.minimum(t + 1, window).astype(F32)


def _grid_call(body, name, steps, in_specs, out_specs, out_shape, operands, scratch=(), tasks=()):
    t_operands, t_in_specs, t_out_shape, t_out_specs, t_sems = _task_args(tasks)
    outs = pl.pallas_call(
        _carry(body, tasks, len(in_specs), len(out_specs), len(scratch), steps), name=name, grid=(steps,),
        in_specs=list(in_specs) + t_in_specs, out_specs=list(out_specs) + t_out_specs,
        out_shape=list(out_shape) + t_out_shape, scratch_shapes=list(scratch) + t_sems,
        compiler_params=_params("arbitrary"),
    )(*operands, *t_operands)
    return outs[:len(out_specs)], _task_results(tasks, outs[len(out_specs):])


def _proj_in(x, win, b_gate, cos, sin, tasks=()):
    seq = x.shape[0]
    nt = seq // TOK

    def body(x_ref, win_ref, bg_ref, cos_ref, sin_ref, xt_ref, q_ref, k_ref, v_ref, u_ref, g_ref, km_ref):
        xb = x_ref[...].astype(BF16)
        xt_ref[...] = x_ref[...].T.astype(BF16)
        cos_t, sin_t = cos_ref[...], sin_ref[...]
        for sec, out_ref in ((0, q_ref), (1, k_ref)):
            z = _dot(xb, win_ref[sec])
            for c in range(D_ATTN // LANE):
                cols = slice(LANE * c, LANE * (c + 1))
                out_ref[:, cols] = _rope(z[:, cols], cos_t, sin_t)
        for b in range(TOK // MOBA_BLOCK):
            km_ref[b] = jnp.mean(k_ref[MOBA_BLOCK * b:MOBA_BLOCK * (b + 1), :], axis=0, keepdims=True)
        v_ref[...] = _dot(xb, win_ref[2]).astype(BF16)
        u_ref[...] = _dot(xb, win_ref[3])
        for n in range(4):
            cols = slice(D_ATTN * n, D_ATTN * (n + 1))
            g_ref[:, cols] = jax.nn.sigmoid(_dot(xb, win_ref[4 + n]) + bg_ref[:, cols])

    return _grid_call(
        body, "proj_in", nt,
        in_specs=[_rows(D_MODEL), _full(win.shape), _full((1, 2 * D_MODEL)), _rows(LANE), _rows(LANE)],
        out_specs=[pl.BlockSpec((D_MODEL, TOK), lambda i: (0, i)), _rows(D_ATTN), _rows(D_ATTN), _rows(D_ATTN),
                   _rows(D_POOL), _rows(2 * D_MODEL),
                   pl.BlockSpec((TOK // MOBA_BLOCK, 1, D_ATTN), lambda i: (i, 0, 0))],
        out_shape=[_sds((D_MODEL, seq), BF16), _sds((seq, D_ATTN), F32), _sds((seq, D_ATTN), F32),
                   _sds((seq, D_ATTN), BF16), _sds((seq, D_POOL), F32), _sds((seq, 2 * D_MODEL), F32),
                   _sds((seq // MOBA_BLOCK, 1, D_ATTN), F32)],
        operands=(x, win, b_gate, cos, sin), tasks=tasks)


SCORE_CHUNK = 128


def _store_keys(ka_sc, k_ref, ls):
    seq = ka_sc.shape[0]
    ka_sc[:, 0:HEAD_DIM] = k_ref[:, ls].astype(BF16)
    row = lax.broadcasted_iota(jnp.int32, (seq, HEAD_DIM), 0)
    lane = lax.broadcasted_iota(jnp.int32, (seq, HEAD_DIM), 1)
    in_block = (lane * MOBA_BLOCK <= row) & (row < (lane + 1) * MOBA_BLOCK)
    ka_sc[:, HEAD_DIM:] = jnp.where(in_block, 1.0, 0.0).astype(BF16)


def _block_bias(qf, km, i):
    if i <= MOBA_TOPK:
        return jnp.zeros((MOBA_BLOCK, HEAD_DIM), BF16)
    nb = km.shape[0]
    gate = lax.dot_general(km, qf, NT_DIMS, precision=lax.Precision.HIGHEST, preferred_element_type=F32)
    blk = lax.broadcasted_iota(jnp.int32, gate.shape, 0)
    rank = jnp.zeros(gate.shape, F32)
    for r in range(1, i):
        lower = pltpu.roll(gate, r, 0)
        rank = rank + jnp.where((blk >= r) & (lower >= gate), 1.0, 0.0)
        higher = pltpu.roll(gate, nb - r, 0)
        rank = rank + jnp.where((blk + r < i) & (higher > gate), 1.0, 0.0)
    bias = jnp.where((blk < i) & (rank >= MOBA_TOPK), NEG, 0.0)
    padded = jnp.concatenate([bias, jnp.zeros((LANE - nb, MOBA_BLOCK), F32)], axis=0)
    return jnp.transpose(padded)[:, 0:HEAD_DIM].astype(BF16)


def _causal(shape, transposed=False):
    row = lax.broadcasted_iota(jnp.int32, shape, 0)
    col = lax.broadcasted_iota(jnp.int32, shape, 1)
    return (row <= col) if transposed else (col <= row)


def _row_vector(col):
    return jnp.transpose(jnp.broadcast_to(col, (MOBA_BLOCK, LANE)))[0:1, :]


def _attn_fwd(q, k, v, kmean, tasks=()):
    seq = q.shape[0]
    nb = seq // MOBA_BLOCK
    assert nb == 8, "the block ranking keeps one sublane per key block"
    pair = pl.BlockSpec((seq, LANE), lambda p: (0, p))
    heads = LANE // HEAD_DIM

    def body(q_ref, k_ref, v_ref, km_ref, o_ref, lse_ref, bias_ref, ka_sc, qa_sc, s_sc, p_sc):
        lse_ref[0, heads:, :] = jnp.zeros((8 - heads, seq), F32)
        for hh in range(heads):
            ls = slice(HEAD_DIM * hh, HEAD_DIM * (hh + 1))
            _store_keys(ka_sc, k_ref, ls)
            vb = v_ref[:, ls]
            km = km_ref[:, ls]
            for i in range(nb):
                rs = slice(MOBA_BLOCK * i, MOBA_BLOCK * (i + 1))
                width = MOBA_BLOCK * (i + 1)
                qf = q_ref[rs, ls]
                bias = _block_bias(qf, km, i)
                bias_ref[rs, ls] = bias
                qa_sc[:, 0:HEAD_DIM] = (qf * HEAD_DIM ** -0.5).astype(BF16)
                qa_sc[:, HEAD_DIM:] = bias
                s_sc[:, 0:width] = _dot_nt(qa_sc[...], ka_sc[0:width, :])
                s_sc[:, rs] = jnp.where(_causal((MOBA_BLOCK, MOBA_BLOCK)), s_sc[:, rs], NEG)
                chunks = [slice(SCORE_CHUNK * c, SCORE_CHUNK * (c + 1)) for c in range(width // SCORE_CHUNK)]
                top = s_sc[:, chunks[0]]
                for c in chunks[1:]:
                    top = jnp.maximum(top, s_sc[:, c])
                m = jnp.max(top, axis=1, keepdims=True)
                total = jnp.zeros((MOBA_BLOCK, SCORE_CHUNK), F32)
                for c in chunks:
                    p = jnp.exp(s_sc[:, c] - m)
                    total = total + p
                    p_sc[:, c] = p.astype(BF16)
                l = jnp.sum(total, axis=1, keepdims=True)
                o_ref[rs, ls] = _dot(p_sc[:, 0:width], vb[0:width]) / l
                lse_ref[0, hh:hh + 1, rs] = _row_vector(m + jnp.log(l))

    return _grid_call(
        body, "attn_fwd", D_ATTN // LANE,
        in_specs=[pair, pair, pair, pl.BlockSpec((nb, LANE), lambda p: (0, p))],
        out_specs=[pair, pl.BlockSpec((1, 8, seq), lambda p: (p, 0, 0)), pair],
        out_shape=[_sds((seq, D_ATTN), F32), _sds((D_ATTN // LANE, 8, seq), F32), _sds((seq, D_ATTN), BF16)],
        operands=(q, k, v, kmean),
        scratch=[pltpu.VMEM((seq, LANE), BF16), pltpu.VMEM((MOBA_BLOCK, LANE), BF16),
                 pltpu.VMEM((MOBA_BLOCK, seq), F32), pltpu.VMEM((MOBA_BLOCK, seq), BF16)],
        tasks=tasks)


def _mix(o, u, g, x, wba, wbp, wout, w_pool, pool_scale, ln_g, ln_b, tasks=()):
    seq = x.shape[0]

    def body(o_ref, u_ref, uprev_ref, g_ref, x_ref, wba_ref, wbp_ref, wout_ref, wp_ref, ps_ref, lg_ref, lb_ref,
             ya_ref, yp_ref, pooled_ref, mixed_ref, ypre_ref, merged_ref, xhat_ref, rstd_ref, h_ref, hb_ref, ext):
        i = pl.program_id(0)
        ya = _dot(o_ref[...].astype(BF16), wba_ref[...])
        ucur = u_ref[...]
        ext[0:POOL_HALO, :] = jnp.where(i > 0, uprev_ref[...], 0.0)
        ext[POOL_HALO:, :] = ucur
        for grp, window in enumerate(POOL_WINDOWS):
            cols = slice(POOL_GROUP * grp, POOL_GROUP * (grp + 1))
            acc = ucur[:, cols]
            for kk in range(1, window):
                acc = acc + ext[pl.ds(POOL_HALO - kk, TOK), cols]
            pooled = acc / _pool_count(i * TOK, TOK, window) - ucur[:, cols]
            pooled_ref[:, cols] = pooled.astype(BF16)
            mixed_ref[:, cols] = _dot(pooled.astype(BF16), wp_ref[grp].astype(BF16))
        mixed = mixed_ref[...]
        ypre = (mixed * ps_ref[...]).astype(BF16)
        ypre_ref[...] = ypre
        yp = _dot(ypre, wbp_ref[...])
        ya_ref[...] = ya
        yp_ref[...] = yp
        merged = (g_ref[:, :D_MODEL] * ya + g_ref[:, D_MODEL:] * yp).astype(BF16)
        merged_ref[...] = merged
        r1 = ALPHA * x_ref[...] + _dot(merged, wout_ref[...])
        h, xhat, rstd = _ln_fwd(r1, lg_ref[...], lb_ref[...])
        xhat_ref[...] = xhat
        rstd_ref[...] = jnp.broadcast_to(rstd, (TOK, LANE))
        h_ref[...] = h
        hb_ref[...] = h.astype(BF16)

    halo = pl.BlockSpec((POOL_HALO, D_POOL), lambda i: (jnp.maximum(i * (TOK // POOL_HALO) - 1, 0), 0))
    return _grid_call(
        body, "mix", seq // TOK,
        in_specs=[_rows(D_ATTN), _rows(D_POOL), halo, _rows(2 * D_MODEL), _rows(D_MODEL),
                  _full(wba.shape), _full(wbp.shape), _full(wout.shape), _full(w_pool.shape),
                  _full((1, D_POOL)), _full((1, D_MODEL)), _full((1, D_MODEL))],
        out_specs=[_rows(D_MODEL), _rows(D_MODEL), _rows(D_POOL), _rows(D_POOL), _rows(D_POOL), _rows(D_MODEL),
                   _rows(D_MODEL), _rows(LANE), _rows(D_MODEL), _rows(D_MODEL)],
        out_shape=[_sds((seq, D_MODEL), F32), _sds((seq, D_MODEL), F32), _sds((seq, D_POOL), BF16),
                   _sds((seq, D_POOL), F32), _sds((seq, D_POOL), BF16), _sds((seq, D_MODEL), BF16),
                   _sds((seq, D_MODEL), F32), _sds((seq, LANE), F32), _sds((seq, D_MODEL), F32),
                   _sds((seq, D_MODEL), BF16)],
        operands=(o, u, u, g, x, wba, wbp, wout, w_pool, pool_scale, ln_g, ln_b),
        scratch=[pltpu.VMEM((TOK + POOL_HALO, D_POOL), F32)], tasks=tasks)


def _ffn_up(hb, wgt, wut, conv_w, conv_b, tasks=()):
    seq = hb.shape[0]
    wblk = pl.BlockSpec((FF_CHUNK, D_MODEL), lambda c: (c, 0))
    cblk = lambda rows: pl.BlockSpec((rows, FF_CHUNK), lambda c: (0, c))
    oblk = pl.BlockSpec((seq, FF_CHUNK), lambda c: (0, c))

    def body(h_ref, wg_ref, wu_ref, cw_ref, cb_ref, a_ref, u_ref, act_ref):
        h = h_ref[...]
        a = _dot_nt(h, wg_ref[...])
        u = _dot_nt(h, wu_ref[...])
        a_ref[...] = a
        u_ref[...] = u
        ac = _conv(a, cw_ref[...], cb_ref[...])
        act_ref[...] = (ac * _normal_cdf(ac) * u).astype(BF16)

    return _grid_call(
        body, "ffn_up", D_FF // FF_CHUNK,
        in_specs=[_full(hb.shape), wblk, wblk, cblk(3), cblk(1)],
        out_specs=[oblk, oblk, oblk],
        out_shape=[_sds((seq, D_FF), F32), _sds((seq, D_FF), F32), _sds((seq, D_FF), BF16)],
        operands=(hb, wgt, wut, conv_w, conv_b), tasks=tasks)


def _ffn_down(act, wd, h, target, ln_g, ln_b):
    seq = h.shape[0]

    def body(act_ref, wd_ref, h_ref, t_ref, lg_ref, lb_ref, dr_ref, drb_ref, loss_ref, dg_ref, db_ref):
        i = pl.program_id(0)

        @pl.when(i == 0)
        def _():
            loss_ref[...] = jnp.zeros_like(loss_ref)
            dg_ref[...] = jnp.zeros_like(dg_ref)
            db_ref[...] = jnp.zeros_like(db_ref)

        r2 = ALPHA * h_ref[...] + _dot(act_ref[...], wd_ref[...])
        y, xhat, rstd = _ln_fwd(r2, lg_ref[...], lb_ref[...])
        diff = y - t_ref[...]
        loss_ref[...] += jnp.sum(diff * diff) * (0.5 / D_MODEL)
        dy = diff * (1.0 / D_MODEL)
        dg_ref[...] += jnp.sum(dy * xhat, axis=0, keepdims=True)
        db_ref[...] += jnp.sum(dy, axis=0, keepdims=True)
        dr = _ln_bwd(dy, xhat, rstd, lg_ref[...])
        dr_ref[...] = dr
        drb_ref[...] = dr.astype(BF16)

    vec = pl.BlockSpec((1, D_MODEL), lambda i: (0, 0))
    return pl.pallas_call(
        body, name="ffn_down", grid=(seq // TOK,),
        in_specs=[_rows(D_FF), _full(wd.shape), _rows(D_MODEL), _rows(D_MODEL), _full((1, D_MODEL)), _full((1, D_MODEL))],
        out_specs=[_rows(D_MODEL), _rows(D_MODEL), pl.BlockSpec((8, LANE), lambda i: (0, 0)), vec, vec],
        out_shape=[_sds((seq, D_MODEL), F32), _sds((seq, D_MODEL), BF16), _sds((8, LANE), F32),
                   _sds((1, D_MODEL), F32), _sds((1, D_MODEL), F32)],
        compiler_params=_params("arbitrary"),
    )(act, wd, h, target, ln_g, ln_b)


def _ffn_bwd(drb, hb, a, u, wd, conv_w, conv_b):
    seq = hb.shape[0]
    wblk = pl.BlockSpec((FF_CHUNK, D_MODEL), lambda c: (c, 0))
    cblk = lambda rows: pl.BlockSpec((rows, FF_CHUNK), lambda c: (0, c))
    sblk = pl.BlockSpec((seq, FF_CHUNK), lambda c: (0, c))

    def body(dr_ref, h_ref, a_ref, u_ref, wd_ref, cw_ref, cb_ref, da_ref, du_ref, dwd_ref, dwg_ref, dwu_ref, dc_ref):
        dr = dr_ref[...]
        h = h_ref[...]
        a = a_ref[...]
        u = u_ref[...]
        cw = cw_ref[...]
        dact = _dot_nt(dr, wd_ref[...])
        ac = _conv(a, cw, cb_ref[...])
        cdf = _normal_cdf(ac)
        gelu = ac * cdf
        dwd_ref[...] = _dot_tn((gelu * u).astype(BF16), dr).astype(BF16)
        du = (dact * gelu).astype(BF16)
        dac = dact * u * _gelu_derivative(ac, cdf)
        da = (cw[2:3, :] * dac + cw[1:2, :] * _shift_up(dac, 1) + cw[0:1, :] * _shift_up(dac, 2)).astype(BF16)
        da_ref[...] = da
        du_ref[...] = du
        dwg_ref[...] = _dot_tn(da, h).astype(BF16)
        dwu_ref[...] = _dot_tn(du, h).astype(BF16)
        dc_ref[0:1, :] = jnp.sum(dac * _shift_down(a, 2), axis=0, keepdims=True)
        dc_ref[1:2, :] = jnp.sum(dac * _shift_down(a, 1), axis=0, keepdims=True)
        dc_ref[2:3, :] = jnp.sum(dac * a, axis=0, keepdims=True)
        dc_ref[3:4, :] = jnp.sum(dac, axis=0, keepdims=True)
        dc_ref[4:8, :] = jnp.zeros((4, FF_CHUNK), F32)

    return pl.pallas_call(
        body, name="ffn_bwd", grid=(D_FF // FF_CHUNK,),
        in_specs=[_full(drb.shape), _full(hb.shape), sblk, sblk, wblk, cblk(3), cblk(1)],
        out_specs=[sblk, sblk, wblk, wblk, wblk, cblk(8)],
        out_shape=[_sds((seq, D_FF), BF16), _sds((seq, D_FF), BF16), _sds((D_FF, D_MODEL), BF16),
                   _sds((D_FF, D_MODEL), BF16), _sds((D_FF, D_MODEL), BF16), _sds((8, D_FF), F32)],
        compiler_params=_params("parallel"),
    )(drb, hb, a, u, wd, conv_w, conv_b)


def _ln1_bwd(dr2, da, du, wgt, wut, xhat, rstd, ln_g, tasks=()):
    seq = dr2.shape[0]

    def body(dr2_ref, da_ref, du_ref, wg_ref, wu_ref, xhat_ref, rstd_ref, lg_ref, dr_ref, drb_ref, dg_ref, db_ref):
        @pl.when(pl.program_id(0) == 0)
        def _():
            dg_ref[...] = jnp.zeros_like(dg_ref)
            db_ref[...] = jnp.zeros_like(db_ref)

        dh = ALPHA * dr2_ref[...] + _dot(da_ref[...], wg_ref[...]) + _dot(du_ref[...], wu_ref[...])
        xhat = xhat_ref[...]
        dg_ref[...] += jnp.sum(dh * xhat, axis=0, keepdims=True)
        db_ref[...] += jnp.sum(dh, axis=0, keepdims=True)
        dr = _ln_bwd(dh, xhat, rstd_ref[:, 0:1], lg_ref[...])
        dr_ref[...] = dr
        drb_ref[...] = dr.astype(BF16)

    vec = pl.BlockSpec((1, D_MODEL), lambda i: (0, 0))
    return _grid_call(
        body, "ln1_bwd", seq // TOK,
        in_specs=[_rows(D_MODEL), _rows(D_FF), _rows(D_FF), _full(wgt.shape), _full(wut.shape), _rows(D_MODEL),
                  _rows(LANE), _full((1, D_MODEL))],
        out_specs=[_rows(D_MODEL), _rows(D_MODEL), vec, vec],
        out_shape=[_sds((seq, D_MODEL), F32), _sds((seq, D_MODEL), BF16), _sds((1, D_MODEL), F32),
                   _sds((1, D_MODEL), F32)],
        operands=(dr2, da, du, wgt, wut, xhat, rstd, ln_g), tasks=tasks)


def _mix_bwd(drb, ya, yp, g, mixed, wout, wba, wbp, w_pool, pool_scale, tasks=()):
    seq = drb.shape[0]

    def body(dr_ref, ya_ref, yp_ref, g_ref, mixed_ref, wout_ref, wba_ref, wbp_ref, wp_ref, ps_ref,
             dzg_ref, dya_ref, dyp_ref, do_ref, dmixed_ref, dpooled_ref, dbg_ref, dps_ref):
        @pl.when(pl.program_id(0) == 0)
        def _():
            dbg_ref[...] = jnp.zeros_like(dbg_ref)
            dps_ref[...] = jnp.zeros_like(dps_ref)

        dmerged = _dot_nt(dr_ref[...], wout_ref[...])
        ga, gp = g_ref[:, :D_MODEL], g_ref[:, D_MODEL:]
        dzga = dmerged * ya_ref[...] * ga * (1.0 - ga)
        dzgp = dmerged * yp_ref[...] * gp * (1.0 - gp)
        dzg_ref[:, :D_MODEL] = dzga.astype(BF16)
        dzg_ref[:, D_MODEL:] = dzgp.astype(BF16)
        dbg_ref[:, :D_MODEL] += jnp.sum(dzga, axis=0, keepdims=True)
        dbg_ref[:, D_MODEL:] += jnp.sum(dzgp, axis=0, keepdims=True)
        dya = (dmerged * ga).astype(BF16)
        dyp = (dmerged * gp).astype(BF16)
        dya_ref[...] = dya
        dyp_ref[...] = dyp
        do_ref[...] = _dot_nt(dya, wba_ref[...])
        dypre = _dot_nt(dyp, wbp_ref[...])
        dps_ref[...] += jnp.sum(dypre * mixed_ref[...], axis=0, keepdims=True)
        dmixed = (dypre * ps_ref[...]).astype(BF16)
        dmixed_ref[...] = dmixed
        for grp in range(len(POOL_WINDOWS)):
            cols = slice(POOL_GROUP * grp, POOL_GROUP * (grp + 1))
            dpooled_ref[:, cols] = _dot_nt(dmixed[:, cols], wp_ref[grp].astype(BF16))

    return _grid_call(
        body, "mix_bwd", seq // TOK,
        in_specs=[_rows(D_MODEL), _rows(D_MODEL), _rows(D_MODEL), _rows(2 * D_MODEL), _rows(D_POOL),
                  _full(wout.shape), _full(wba.shape), _full(wbp.shape), _full(w_pool.shape), _full((1, D_POOL))],
        out_specs=[_rows(2 * D_MODEL), _rows(D_MODEL), _rows(D_MODEL), _rows(D_ATTN), _rows(D_POOL), _rows(D_POOL),
                   pl.BlockSpec((1, 2 * D_MODEL), lambda i: (0, 0)), pl.BlockSpec((1, D_POOL), lambda i: (0, 0))],
        out_shape=[_sds((seq, 2 * D_MODEL), BF16), _sds((seq, D_MODEL), BF16), _sds((seq, D_MODEL), BF16),
                   _sds((seq, D_ATTN), F32), _sds((seq, D_POOL), BF16), _sds((seq, D_POOL), F32),
                   _sds((1, 2 * D_MODEL), F32), _sds((1, D_POOL), F32)],
        operands=(drb, ya, yp, g, mixed, wout, wba, wbp, w_pool, pool_scale), tasks=tasks)


def _attn_bwd(q, k, v, bias, o, lse, do, cos, sin, tasks=()):
    seq = q.shape[0]
    nb = seq // MOBA_BLOCK
    pair = pl.BlockSpec((seq, LANE), lambda p: (0, p))
    table = pl.BlockSpec((seq, LANE), lambda p: (0, 0))
    scale = HEAD_DIM ** -0.5

    def body(q_ref, k_ref, v_ref, bias_ref, o_ref, lse_ref, do_ref, cos_ref, sin_ref, dq_ref, dk_ref, dv_ref,
             dq_acc, dk_acc, dv_acc, dk_head, dv_head, ka_sc, qa_sc, s_sc, dp_sc, p_sc, ds_sc):
        for hh in range(LANE // HEAD_DIM):
            ls = slice(HEAD_DIM * hh, HEAD_DIM * (hh + 1))
            _store_keys(ka_sc, k_ref, ls)
            vb = v_ref[:, ls]
            dk_head[...] = jnp.zeros_like(dk_head)
            dv_head[...] = jnp.zeros_like(dv_head)
            for i in range(nb):
                rs = slice(MOBA_BLOCK * i, MOBA_BLOCK * (i + 1))
                width = MOBA_BLOCK * (i + 1)
                qa_sc[:, 0:HEAD_DIM] = (q_ref[rs, ls] * scale).astype(BF16)
                qa_sc[:, HEAD_DIM:] = bias_ref[rs, ls]
                s_sc[0:width, :] = _dot_nt(ka_sc[0:width, :], qa_sc[...])
                s_sc[rs, :] = jnp.where(_causal((MOBA_BLOCK, MOBA_BLOCK), transposed=True), s_sc[rs, :], NEG)
                dob = do_ref[rs, ls]
                delta = _row_vector(jnp.sum(dob * o_ref[rs, ls], axis=1, keepdims=True))
                lse_row = lse_ref[0, hh:hh + 1, rs]
                dob16 = dob.astype(BF16)
                dp_sc[0:width, :] = _dot_nt(vb[0:width], dob16)
                for c in range(width // SCORE_CHUNK):
                    rows = slice(SCORE_CHUNK * c, SCORE_CHUNK * (c + 1))
                    p = jnp.exp(s_sc[rows, :] - lse_row)
                    p_sc[rows, :] = p.astype(BF16)
                    ds_sc[rows, :] = (p * (dp_sc[rows, :] - delta)).astype(BF16)
                dv_head[0:width, :] += _dot(p_sc[0:width, :], dob16)
                dk_head[0:width, :] += _dot(ds_sc[0:width, :], qa_sc[:, 0:HEAD_DIM])
                dq_acc[rs, ls] = _dot_tn(ds_sc[0:width, :], ka_sc[0:width, 0:HEAD_DIM]) * scale
            dk_acc[:, ls] = dk_head[...]
            dv_acc[:, ls] = dv_head[...]
        cos_t, sin_t = cos_ref[...], sin_ref[...]
        dq_ref[...] = _rope_transposed(dq_acc[...], cos_t, sin_t).astype(BF16)
        dk_ref[...] = _rope_transposed(dk_acc[...], cos_t, sin_t).astype(BF16)
        dv_ref[...] = dv_acc[...].astype(BF16)

    return _grid_call(
        body, "attn_bwd", D_ATTN // LANE,
        in_specs=[pair, pair, pair, pair, pair, pl.BlockSpec((1, 8, seq), lambda p: (p, 0, 0)), pair, table, table],
        out_specs=[pair, pair, pair], out_shape=[_sds((seq, D_ATTN), BF16)] * 3,
        operands=(q, k, v, bias, o, lse, do, cos, sin),
        scratch=[pltpu.VMEM((seq, LANE), F32)] * 3 + [pltpu.VMEM((seq, HEAD_DIM), F32)] * 2
        + [pltpu.VMEM((seq, LANE), BF16), pltpu.VMEM((MOBA_BLOCK, LANE), BF16)]
        + [pltpu.VMEM((seq, MOBA_BLOCK), F32)] * 2 + [pltpu.VMEM((seq, MOBA_BLOCK), BF16)] * 2,
        tasks=tasks)


def _in_bwd(dq, dk, dv, dpooled, dzg, dr1, win, tasks=()):
    seq = dr1.shape[0]
    nt = seq // TOK

    def body(dq_ref, dk_ref, dv_ref, dp_ref, dpnext_ref, dzg_ref, dr_ref, win_ref, dx_ref, dz_ref, ext):
        i = pl.program_id(0)
        dp = dp_ref[...]
        dpn = jnp.where(i < nt - 1, dpnext_ref[...], 0.0)
        for grp, window in enumerate(POOL_WINDOWS):
            cols = slice(POOL_GROUP * grp, POOL_GROUP * (grp + 1))
            ext[0:TOK, cols] = dp[:, cols] / _pool_count(i * TOK, TOK, window)
            ext[TOK:, cols] = dpn[:, cols] / _pool_count((i + 1) * TOK, POOL_HALO, window)
        for grp, window in enumerate(POOL_WINDOWS):
            cols = slice(POOL_GROUP * grp, POOL_GROUP * (grp + 1))
            acc = ext[0:TOK, cols] - dp[:, cols]
            for kk in range(1, window):
                acc = acc + ext[pl.ds(kk, TOK), cols]
            dz_ref[:, 3 * D_ATTN + POOL_GROUP * grp:3 * D_ATTN + POOL_GROUP * (grp + 1)] = acc.astype(BF16)
        dz_ref[:, 0:D_ATTN] = dq_ref[...]
        dz_ref[:, D_ATTN:2 * D_ATTN] = dk_ref[...]
        dz_ref[:, 2 * D_ATTN:3 * D_ATTN] = dv_ref[...]
        dz_ref[:, 3 * D_ATTN + D_POOL:] = dzg_ref[...]
        dx = ALPHA * dr_ref[...]
        for n in range(N_DEV):
            dx = dx + _dot_nt(dz_ref[:, D_ATTN * n:D_ATTN * (n + 1)], win_ref[n])
        dx_ref[...] = dx

    halo = pl.BlockSpec((POOL_HALO, D_POOL),
                        lambda i: (jnp.minimum((i + 1) * (TOK // POOL_HALO), seq // POOL_HALO - 1), 0))
    return _grid_call(
        body, "in_bwd", nt,
        in_specs=[_rows(D_ATTN), _rows(D_ATTN), _rows(D_ATTN), _rows(D_POOL), halo, _rows(2 * D_MODEL),
                  _rows(D_MODEL), _full(win.shape)],
        out_specs=[_rows(D_MODEL), _rows(D_IN_PROJ)],
        out_shape=[_sds((seq, D_MODEL), F32), _sds((seq, D_IN_PROJ), BF16)],
        operands=(dq, dk, dv, dpooled, dpooled, dzg, dr1, win),
        scratch=[pltpu.VMEM((TOK + POOL_HALO, D_POOL), F32)], tasks=tasks)


def _dw_mixers(o, ypre, merged, dya, dyp, drb, pooled, dmixed, tasks=()):
    seq = o.shape[0]
    groups = len(POOL_WINDOWS)
    col = pl.BlockSpec((seq, LANE), lambda n: (0, n))
    grp = pl.BlockSpec((seq, POOL_GROUP), lambda n: (0, jnp.minimum(n, groups - 1)))
    owner = lambda rows, cols: pl.BlockSpec((1, rows, cols), lambda n: (n, 0, 0))

    def body(o_ref, ypre_ref, merged_ref, dya_ref, dyp_ref, dr_ref, pooled_ref, dmixed_ref,
             dba_ref, dbp_ref, dout_ref, dpool_ref, ob_sc):
        n = pl.program_id(0)

        @pl.when(n == 0)
        def _():
            ob_sc[...] = o_ref[...].astype(BF16)

        dba_ref[0] = _dot_tn(dya_ref[...], ob_sc[...]).T.astype(BF16)
        dbp_ref[0] = _dot_tn(dyp_ref[...], ypre_ref[...]).T.astype(BF16)
        dout_ref[0] = _dot_tn(merged_ref[...], dr_ref[...]).astype(BF16)

        @pl.when(n < groups)
        def _():
            dpool_ref[0] = _dot_tn(pooled_ref[...], dmixed_ref[...])

    return _grid_call(
        body, "dw_mixers", N_DEV,
        in_specs=[_full(o.shape), _full(ypre.shape), col, col, col, _full(drb.shape), grp, grp],
        out_specs=[owner(D_ATTN, LANE), owner(D_POOL, LANE), owner(D_MODEL // N_DEV, D_MODEL),
                   pl.BlockSpec((1, POOL_GROUP, POOL_GROUP), lambda n: (jnp.minimum(n, groups - 1), 0, 0))],
        out_shape=[_sds((N_DEV, D_ATTN, LANE), BF16), _sds((N_DEV, D_POOL, LANE), BF16),
                   _sds((N_DEV, D_MODEL // N_DEV, D_MODEL), BF16), _sds((groups, POOL_GROUP, POOL_GROUP), F32)],
        operands=(o, ypre, merged, dya, dyp, drb, pooled, dmixed),
        scratch=[pltpu.VMEM((seq, D_ATTN), BF16)], tasks=tasks)


def _matmul(name, a, b, out_shape, out_dtype, steps, a_spec, b_spec, o_spec, tasks=()):
    def body(a_ref, b_ref, o_ref):
        o_ref[...] = _dot(a_ref[...], b_ref[...]).reshape(o_ref.shape).astype(o_ref.dtype)

    (out,), results = _grid_call(body, name, steps, in_specs=[a_spec, b_spec], out_specs=[o_spec],
                                 out_shape=[_sds(out_shape, out_dtype)], operands=(a, b), tasks=tasks)
    return out, results


def _place():
    return lax.axis_index("x"), lax.axis_index("y"), lax.axis_index("c")


def _other_chips(x, y):
    return [(1 - x, y), (x, 1 - y), (1 - x, 1 - y)]


DMA_SEMS = pltpu.SemaphoreType.DMA


class _AllGather:
    def __init__(self, shards, lag=0):
        self.operands = list(shards)
        self.n = len(shards)
        self.lag = lag
        self.out_shape = [_sds((N_DEV, *s.shape), s.dtype) for s in shards]
        self.sems = [DMA_SEMS((7 * self.n,)), DMA_SEMS((7 * self.n,)), DMA_SEMS((self.n,))]

    def _copy(self, refs, a, k, block, to, from_input=False):
        ins, outs, (send_sems, recv_sems, _) = refs
        px, py, pc = block
        dst = outs[a].at[4 * px + 2 * py + pc]
        return pltpu.make_async_remote_copy(
            src_ref=ins[a] if from_input else dst, dst_ref=dst,
            send_sem=send_sems.at[7 * a + k], recv_sem=recv_sems.at[7 * a + k],
            device_id=to, device_id_type=MESH)

    def _local(self, refs, a):
        ins, outs, (_, _, local_sems) = refs
        x, y, c = _place()
        return pltpu.make_async_copy(ins[a], outs[a].at[4 * x + 2 * y + c], local_sems.at[a])

    def _pass_on(self, refs, a):
        x, y, c = _place()
        origin = ((x + 1 - c) % 2, (y + c) % 2, c)
        target = ((x + c) % 2, (y + 1 - c) % 2, c)
        return self._copy(refs, a, 3, origin, target)

    def start(self, refs):
        x, y, c = _place()
        for a in range(self.n):
            self._local(refs, a).start()
        for a in range(self.n):
            self._copy(refs, a, 0, (x, y, c), (x, y, 1 - c), True).start()
            for j, chip in enumerate(_other_chips(x, y)[:2]):
                self._copy(refs, a, 1 + j, (x, y, c), (*chip, c), True).start()

    def middle(self, refs):
        x, y, c = _place()
        me, sibling = (x, y, c), (x, y, 1 - c)
        chips = _other_chips(x, y)
        for a in range(self.n):
            for j in range(2):
                self._copy(refs, a, 1 + j, (*chips[j], c), me).wait_recv()
        for a in range(self.n):
            self._pass_on(refs, a).start()
            for j in range(2):
                self._copy(refs, a, 4 + j, (*chips[j], c), sibling).start()

    def late(self, refs):
        x, y, c = _place()
        diagonal = (1 - x, 1 - y, c)
        for a in range(self.n):
            self._copy(refs, a, 3, diagonal, (x, y, c)).wait_recv()
            self._copy(refs, a, 6, diagonal, (x, y, 1 - c)).start()

    def finish(self, refs):
        x, y, c = _place()
        me, sibling = (x, y, c), (x, y, 1 - c)
        chips = _other_chips(x, y)
        for a in range(self.n):
            self._copy(refs, a, 0, sibling, me).wait_recv()
            for j, chip in enumerate(chips):
                self._copy(refs, a, 4 + j, (*chip, 1 - c), me).wait_recv()
        for a in range(self.n):
            self._copy(refs, a, 0, me, sibling, True).wait_send()
            for j, chip in enumerate(chips[:2]):
                self._copy(refs, a, 1 + j, me, (*chip, c), True).wait_send()
            self._pass_on(refs, a).wait_send()
            for j, chip in enumerate(chips):
                self._copy(refs, a, 4 + j, (*chip, c), sibling).wait_send()
            self._local(refs, a).wait()


class _SiblingSend:
    def __init__(self, partials):
        self.operands = list(partials)
        self.n = len(partials)
        self.out_shape = [_sds((4, *p.shape[1:]), p.dtype) for p in partials]
        self.sems = [DMA_SEMS((4 * self.n,)), DMA_SEMS((4 * self.n,))]

    def _copy(self, refs, a, q):
        ins, outs, (send_sems, recv_sems) = refs
        x, y, c = _place()
        return pltpu.make_async_remote_copy(
            src_ref=ins[a].at[2 * q + 1 - c], dst_ref=outs[a].at[q],
            send_sem=send_sems.at[4 * a + q], recv_sem=recv_sems.at[4 * a + q],
            device_id=(x, y, 1 - c), device_id_type=MESH)

    def start(self, refs):
        for a in range(self.n):
            for q in range(4):
                self._copy(refs, a, q).start()

    def middle(self, refs):
        pass

    def finish(self, refs):
        for a in range(self.n):
            for q in range(4):
                self._copy(refs, a, q).wait()


class _ChipScatter:
    def __init__(self, chip_partials):
        self.operands = list(chip_partials)
        self.n = len(chip_partials)
        self.out_shape = [_sds(p.shape, p.dtype) for p in chip_partials]
        self.sems = [DMA_SEMS((3 * self.n,)), DMA_SEMS((3 * self.n,)), DMA_SEMS((self.n,))]

    def _copy(self, refs, a, k, arrival=False):
        ins, outs, (send_sems, recv_sems, _) = refs
        x, y, c = _place()
        px, py = _other_chips(x, y)[k]
        mine, theirs = 2 * x + y, 2 * px + py
        return pltpu.make_async_remote_copy(
            src_ref=ins[a].at[mine if arrival else theirs], dst_ref=outs[a].at[theirs if arrival else mine],
            send_sem=send_sems.at[3 * a + k], recv_sem=recv_sems.at[3 * a + k],
            device_id=(px, py, c), device_id_type=MESH)

    def _local(self, refs, a):
        ins, outs, (_, _, local_sems) = refs
        x, y, _ = _place()
        return pltpu.make_async_copy(ins[a].at[2 * x + y], outs[a].at[2 * x + y], local_sems.at[a])

    def start(self, refs):
        for a in range(self.n):
            self._local(refs, a).start()
            for k in range(3):
                self._copy(refs, a, k).start()

    def middle(self, refs):
        pass

    def finish(self, refs):
        for a in range(self.n):
            for k in range(3):
                self._copy(refs, a, k, arrival=True).wait_recv()
        for a in range(self.n):
            for k in range(3):
                self._copy(refs, a, k).wait_send()
            self._local(refs, a).wait()


class _DirectScatter:
    def __init__(self, partials):
        self.operands = list(partials)
        self.n = len(partials)
        self.out_shape = [_sds(p.shape, p.dtype) for p in partials]
        self.sems = [DMA_SEMS((7 * self.n,)), DMA_SEMS((7 * self.n,)), DMA_SEMS((self.n,))]

    def _copy(self, refs, a, k, arrival=False):
        ins, outs, (send_sems, recv_sems, _) = refs
        x, y, c = _place()
        peer = [(x, y, 1 - c), (1 - x, y, c), (x, 1 - y, c), (1 - x, 1 - y, c),
                (1 - x, y, 1 - c), (x, 1 - y, 1 - c), (1 - x, 1 - y, 1 - c)][k]
        mine, theirs = 4 * x + 2 * y + c, 4 * peer[0] + 2 * peer[1] + peer[2]
        return pltpu.make_async_remote_copy(
            src_ref=ins[a].at[mine if arrival else theirs], dst_ref=outs[a].at[theirs if arrival else mine],
            send_sem=send_sems.at[7 * a + k], recv_sem=recv_sems.at[7 * a + k],
            device_id=peer, device_id_type=MESH)

    def _local(self, refs, a):
        ins, outs, (_, _, local_sems) = refs
        x, y, c = _place()
        return pltpu.make_async_copy(ins[a].at[4 * x + 2 * y + c], outs[a].at[4 * x + 2 * y + c], local_sems.at[a])

    def start(self, refs):
        for a in range(self.n):
            self._local(refs, a).start()
            for k in range(7):
                self._copy(refs, a, k).start()

    def middle(self, refs):
        pass

    def finish(self, refs):
        for a in range(self.n):
            for k in range(7):
                self._copy(refs, a, k, arrival=True).wait_recv()
        for a in range(self.n):
            for k in range(7):
                self._copy(refs, a, k).wait_send()
            self._local(refs, a).wait()


def _task_args(tasks):
    hbm = pl.BlockSpec(memory_space=pl.ANY)
    operands = [o for t in tasks for o in t.operands]
    out_shape = [s for t in tasks for s in t.out_shape]
    sems = [s for t in tasks for s in t.sems]
    return operands, [hbm] * len(operands), out_shape, [hbm] * len(out_shape), sems


def _task_refs(tasks, ins, outs, sems):
    per_task = []
    for t in tasks:
        ni, no, ns = len(t.operands), len(t.out_shape), len(t.sems)
        per_task.append((ins[:ni], outs[:no], sems[:ns]))
        ins, outs, sems = ins[ni:], outs[no:], sems[ns:]
    return per_task


def _task_results(tasks, outs):
    res = []
    for t in tasks:
        res.append(list(outs[:len(t.out_shape)]))
        outs = outs[len(t.out_shape):]
    return res


def _carry(body, tasks, n_in, n_out, n_scratch, steps):
    if not tasks:
        return body
    t_in = sum(len(t.operands) for t in tasks)
    t_out = sum(len(t.out_shape) for t in tasks)

    def wrapped(*refs):
        ins, refs = refs[:n_in], refs[n_in:]
        t_ins, refs = refs[:t_in], refs[t_in:]
        outs, refs = refs[:n_out], refs[n_out:]
        t_outs, refs = refs[:t_out], refs[t_out:]
        scratch, t_sems = refs[:n_scratch], refs[n_scratch:]
        per_task = _task_refs(tasks, t_ins, t_outs, t_sems)
        step = pl.program_id(0)

        @pl.when(step == 0)
        def _():
            for t, r in zip(tasks, per_task):
                t.start(r)

        for t, r in zip(tasks, per_task):
            pl.when(step == max(steps - 1 - getattr(t, "lag", 0), 0))(functools.partial(t.middle, r))
            if hasattr(t, "late"):
                pl.when(step == steps - 1)(functools.partial(t.late, r))

        body(*ins, *outs, *scratch)

        @pl.when(step == steps - 1)
        def _():
            for t, r in zip(tasks, per_task):
                t.finish(r)

    return wrapped


def _exchange(name, tasks):
    operands, in_specs, out_shape, out_specs, sems = _task_args(tasks)

    def body(*refs):
        ni, no = len(operands), len(out_shape)
        per_task = _task_refs(tasks, refs[:ni], refs[ni:ni + no], refs[ni + no:])
        for phase in ("start", "middle", "late", "finish"):
            for t, r in zip(tasks, per_task):
                if hasattr(t, phase):
                    getattr(t, phase)(r)

    outs = pl.pallas_call(body, name=name, in_specs=in_specs, out_specs=out_specs, out_shape=out_shape,
                          scratch_shapes=sems)(*operands)
    return _task_results(tasks, outs)


def _row_tile(rows, cols, whole_up_to=256 * 1024):
    if rows * cols <= whole_up_to:
        return rows
    for t in (256, 176, 128, 64, 32, 16, 8):
        if rows % t == 0:
            return t
    return rows


def _pair_sum(name, partials, from_sibling):
    n = len(partials)
    _, rows, cols = partials[0].shape
    tile = _row_tile(rows, cols, 512 * 1024)

    def body(*refs):
        south = lax.axis_index("c") == 0
        for p_ref, s_ref, o_ref in zip(refs[:n], refs[n:2 * n], refs[2 * n:]):
            mine = jnp.where(south, p_ref[0, 0].astype(F32), p_ref[0, 1].astype(F32))
            o_ref[0] = (mine + s_ref[0].astype(F32)).astype(o_ref.dtype)

    blk = pl.BlockSpec((1, tile, cols), lambda q, i: (q, i, 0))
    return pl.pallas_call(
        body, name=name, grid=(4, rows // tile),
        in_specs=[pl.BlockSpec((1, 2, tile, cols), lambda q, i: (q, 0, i, 0))] * n + [blk] * n,
        out_specs=[blk] * n, out_shape=[_sds(s.shape, s.dtype) for s in from_sibling],
        compiler_params=_params("parallel", "parallel"),
    )(*[p.reshape(4, 2, rows, cols) for p in partials], *from_sibling)


def _sum_leading(name, stacked):
    parts, rows, cols = stacked.shape
    tile = _row_tile(rows, cols, (512 if parts <= 4 else 256) * 1024)

    def body(s_ref, o_ref):
        acc = s_ref[0].astype(F32)
        for d in range(1, parts):
            acc = acc + s_ref[d].astype(F32)
        o_ref[...] = acc

    return pl.pallas_call(
        body, name=name, grid=(rows // tile,),
        in_specs=[pl.BlockSpec((parts, tile, cols), lambda i: (0, i, 0))],
        out_specs=pl.BlockSpec((tile, cols), lambda i: (i, 0)),
        out_shape=_sds((rows, cols), F32),
        compiler_params=_params("parallel"),
    )(stacked)


def _adamw_math(w, g, m, v):
    nm = ADAM_B1 * m + (1.0 - ADAM_B1) * g
    nv = ADAM_B2 * v + (1.0 - ADAM_B2) * (g * g)
    m_hat = nm / (1.0 - ADAM_B1 ** ADAM_STEP)
    v_hat = nv / (1.0 - ADAM_B2 ** ADAM_STEP)
    return -ADAM_LR * (m_hat / (jnp.sqrt(v_hat) + ADAM_EPS) + ADAM_WD * w), nm, nv


def _adamw(name, w, g, m, v):
    rows, cols = w.shape
    tile = _row_tile(rows, cols)

    def body(w_ref, g_ref, m_ref, v_ref, d_ref, nm_ref, nv_ref):
        d_ref[...], nm_ref[...], nv_ref[...] = _adamw_math(w_ref[...], g_ref[...], m_ref[...], v_ref[...])

    blk = pl.BlockSpec((tile, cols), lambda i: (i, 0))
    return pl.pallas_call(
        body, name=name, grid=(rows // tile,),
        in_specs=[blk] * 4, out_specs=[blk] * 3,
        out_shape=[_sds((rows, cols), F32)] * 3,
        compiler_params=_params("parallel"),
    )(w, g, m, v)


def _sum_adamw(name, params, tasks=()):
    n = len(params)
    parts, rows, cols = params[0][0].shape
    tile = _row_tile(rows, cols)

    def body(*refs):
        ins, outs = refs[:4 * n], refs[4 * n:]
        for p in range(n):
            s_ref, w_ref, m_ref, v_ref = ins[4 * p:4 * p + 4]
            g_ref, d_ref, nm_ref, nv_ref = outs[4 * p:4 * p + 4]
            g = s_ref[0].astype(F32)
            for d in range(1, parts):
                g = g + s_ref[d].astype(F32)
            g_ref[...] = g
            d_ref[...], nm_ref[...], nv_ref[...] = _adamw_math(w_ref[...], g, m_ref[...], v_ref[...])

    blk = pl.BlockSpec((tile, cols), lambda i: (i, 0))
    outs, results = _grid_call(
        body, name, rows // tile,
        in_specs=([pl.BlockSpec((parts, tile, cols), lambda i: (0, i, 0))] + [blk] * 3) * n, out_specs=[blk] * (4 * n),
        out_shape=[_sds((rows, cols), F32)] * (4 * n),
        operands=[t for p in params for t in p], tasks=tasks)
    return [outs[4 * p:4 * p + 4] for p in range(n)], results


SMALL = ("b_gate", "w_pool", "pool_scale", "ln1_g", "ln1_b", "conv_b", "ln2_g", "ln2_b")
PACKED = SMALL + ("conv_w", "loss")
TILE = 8 * LANE


def _pack(parts):
    tiles = []
    for p in parts:
        flat = p.reshape(-1)
        tiles.append(jnp.pad(flat, (0, -flat.size % TILE)).reshape(-1, LANE))
    return jnp.concatenate(tiles, axis=0)


def _unpack(packed, shapes):
    out, at = [], 0
    for shape in shapes:
        size = math.prod(shape)
        rows = -(-size // TILE) * 8
        out.append(packed[at:at + rows].reshape(-1)[:size].reshape(shape))
        at += rows
    return out


MIXER = ("w_branch_attn", "w_branch_pool", "w_out", "conv_w")
FFN = ("w_ffn_gate_t", "w_ffn_up_t", "w_ffn_down")


def _columns(t):
    return jnp.transpose(t, (1, 0, 2)).reshape(t.shape[1], N_DEV * t.shape[2])


def _row_blocks(t):
    return t.reshape(N_DEV * t.shape[1], t.shape[2])


def _by_owner(t):
    return t.reshape(N_DEV, t.shape[0] // N_DEV, t.shape[1])


def _reduce_halves(names, partials, from_sibling):
    out = [None] * len(names)
    for shape in dict.fromkeys(p.shape for p in partials):
        group = [i for i, p in enumerate(partials) if p.shape == shape]
        sums = _pair_sum("pair_sum_" + names[group[0]], [partials[i] for i in group], [from_sibling[i] for i in group])
        for i, s in zip(group, sums):
            out[i] = s
    return out


def _local_step(x, target, shards, small):
    seq = x.shape[0]
    cos, sin = _rope_tables(seq)
    ((w_in_all,),) = _exchange("gather_w_in", [_AllGather([shards["w_in"]])])
    (xt, q, k, v, u, g, kmean), (mixer,) = _proj_in(
        x, w_in_all, small["b_gate"], cos, sin, tasks=[_AllGather([shards[n] for n in MIXER], lag=2)])
    wba, wbp, wout, conv_w = _columns(mixer[0]), _columns(mixer[1]), _row_blocks(mixer[2]), _columns(mixer[3])
    (o, lse, bias), ((wgt, wut),) = _attn_fwd(
        q, k, v, kmean.reshape(seq // MOBA_BLOCK, D_ATTN),
        tasks=[_AllGather([shards["w_ffn_gate_t"], shards["w_ffn_up_t"]], lag=1)])
    (ya, yp, pooled, mixed, ypre, merged, xhat1, rstd1, h1, h1b), _ = _mix(
        o, u, g, x, wba, wbp, wout, small["w_pool"], small["pool_scale"], small["ln1_g"], small["ln1_b"])
    wgt, wut = _row_blocks(wgt), _row_blocks(wut)
    (a, uf, act), ((wd,),) = _ffn_up(
        h1b, wgt, wut, conv_w, small["conv_b"], tasks=[_AllGather([shards["w_ffn_down"]], lag=4)])
    wd = _row_blocks(wd)
    dr2, dr2b, loss, dg2, db2 = _ffn_down(act, wd, h1, target, small["ln2_g"], small["ln2_b"])

    da, du, dwd, dwg, dwu, dconv = _ffn_bwd(dr2b, h1b, a, uf, wd, conv_w, small["conv_b"])
    ffn_partials = [_by_owner(dwg), _by_owner(dwu), _by_owner(dwd)]
    (dr1, dr1b, dg1, db1), (ffn_sibling,) = _ln1_bwd(
        dr2, da, du, wgt, wut, xhat1, rstd1, small["ln1_g"], tasks=[_SiblingSend(ffn_partials)])
    ffn_chip = _reduce_halves(FFN, ffn_partials, ffn_sibling)
    (dzg, dya, dyp, do, dmixed, dpooled, dbg, dps), (gate_landed,) = _mix_bwd(
        dr1b, ya, yp, g, mixed, wout, wba, wbp, small["w_pool"], small["pool_scale"],
        tasks=[_ChipScatter(ffn_chip[0:1])])
    (dw_ba, dw_bp, dw_out, dw_pool), _ = _dw_mixers(o, ypre, merged, dya, dyp, dr1b, pooled, dmixed)
    mixer_partials = [dw_ba, dw_bp, dw_out]
    (dq, dk, dv), (up_down_landed, mixer_sibling) = _attn_bwd(
        q, k, v, bias, o, lse, do, cos, sin, tasks=[_ChipScatter(ffn_chip[1:3]), _SiblingSend(mixer_partials)])
    mixer_chip = _reduce_halves(MIXER[:3], mixer_partials, mixer_sibling)
    (grad_x, dz), _ = _in_bwd(dq, dk, dv, dpooled, dzg, dr1, w_in_all)
    little = {"b_gate": dbg, "w_pool": dw_pool, "pool_scale": dps, "ln1_g": dg1, "ln1_b": db1, "conv_b": dconv[3:4],
              "ln2_g": dg2, "ln2_b": db2, "conv_w": dconv[0:3], "loss": loss}
    dw_in, (mixer_landed, (all_small,)) = _matmul(
        "dw_in", xt, dz, (N_DEV, D_MODEL, D_ATTN), BF16, N_DEV,
        _full(xt.shape), pl.BlockSpec((seq, D_ATTN), lambda n: (0, n)),
        pl.BlockSpec((1, D_MODEL, D_ATTN), lambda n: (n, 0, 0)),
        tasks=[_ChipScatter(mixer_chip), _AllGather([_pack([little[n] for n in PACKED])], lag=3)])

    landed = dict(zip(FFN + MIXER[:3], gate_landed + up_down_landed + mixer_landed))
    return grad_x, landed, dw_in, all_small


def kernel(x, w_in, b_gate, w_branch_attn, w_pool, pool_scale, w_branch_pool, w_out, ln1_g, ln1_b, w_ffn_gate, w_ffn_up, conv_w, conv_b, w_ffn_down, ln2_g, ln2_b, loss_target, m_w_in, m_b_gate, m_w_branch_attn, m_w_pool, m_pool_scale, m_w_branch_pool, m_w_out, m_ln1_g, m_ln1_b, m_w_ffn_gate, m_w_ffn_up, m_conv_w, m_conv_b, m_w_ffn_down, m_ln2_g, m_ln2_b, v_w_in, v_b_gate, v_w_branch_attn, v_w_pool, v_pool_scale, v_w_branch_pool, v_w_out, v_ln1_g, v_ln1_b, v_w_ffn_gate, v_w_ffn_up, v_conv_w, v_conv_b, v_w_ffn_down, v_ln2_g, v_ln2_b):
    me = 4 * lax.axis_index("x") + 2 * lax.axis_index("y") + lax.axis_index("c")
    weights = dict(w_in=w_in, b_gate=b_gate, w_branch_attn=w_branch_attn, w_pool=w_pool, pool_scale=pool_scale,
                   w_branch_pool=w_branch_pool, w_out=w_out, ln1_g=ln1_g, ln1_b=ln1_b, w_ffn_gate=w_ffn_gate,
                   w_ffn_up=w_ffn_up, conv_w=conv_w, conv_b=conv_b, w_ffn_down=w_ffn_down, ln2_g=ln2_g, ln2_b=ln2_b)
    m_in = dict(w_in=m_w_in, b_gate=m_b_gate, w_branch_attn=m_w_branch_attn, w_pool=m_w_pool,
                pool_scale=m_pool_scale, w_branch_pool=m_w_branch_pool, w_out=m_w_out, ln1_g=m_ln1_g, ln1_b=m_ln1_b,
                w_ffn_gate=m_w_ffn_gate, w_ffn_up=m_w_ffn_up, conv_w=m_conv_w, conv_b=m_conv_b,
                w_ffn_down=m_w_ffn_down, ln2_g=m_ln2_g, ln2_b=m_ln2_b)
    v_in = dict(w_in=v_w_in, b_gate=v_b_gate, w_branch_attn=v_w_branch_attn, w_pool=v_w_pool,
                pool_scale=v_pool_scale, w_branch_pool=v_w_branch_pool, w_out=v_w_out, ln1_g=v_ln1_g, ln1_b=v_ln1_b,
                w_ffn_gate=v_w_ffn_gate, w_ffn_up=v_w_ffn_up, conv_w=v_conv_w, conv_b=v_conv_b,
                w_ffn_down=v_w_ffn_down, ln2_g=v_ln2_g, ln2_b=v_ln2_b)
    weights = {n: a[0] for n, a in weights.items()}
    m_in = {n: a[0] for n, a in m_in.items()}
    v_in = {n: a[0] for n, a in v_in.items()}

    shards = {"w_in": weights["w_in"].astype(BF16), "w_branch_attn": weights["w_branch_attn"].astype(BF16),
              "w_branch_pool": weights["w_branch_pool"].astype(BF16), "w_out": weights["w_out"].astype(BF16),
              "w_ffn_gate_t": weights["w_ffn_gate"].T.astype(BF16), "w_ffn_up_t": weights["w_ffn_up"].T.astype(BF16),
              "w_ffn_down": weights["w_ffn_down"].astype(BF16), "conv_w": weights["conv_w"]}
    small = {"b_gate": weights["b_gate"][None], "w_pool": weights["w_pool"], "pool_scale": weights["pool_scale"][None],
             "ln1_g": weights["ln1_g"][None], "ln1_b": weights["ln1_b"][None], "conv_b": weights["conv_b"][None],
             "ln2_g": weights["ln2_g"][None], "ln2_b": weights["ln2_b"][None]}

    grad_x, landed, dw_in, all_small = _local_step(x[0], loss_target[0], shards, small)

    grads, delta, new_m, new_v = {}, {}, {}, {}

    def param(n, transposed=False):
        if transposed:
            return landed[n + "_t"], weights[n].T, m_in[n].T, v_in[n].T
        return landed[n], weights[n], m_in[n], v_in[n]

    def keep(n, updated, transposed=False):
        grads[n], delta[n], new_m[n], new_v[n] = (t.T for t in updated) if transposed else updated

    (gate,), ((w_in_sibling,),) = _sum_adamw(
        "update_w_ffn_gate", [param("w_ffn_gate", True)], tasks=[_SiblingSend([dw_in])])
    keep("w_ffn_gate", gate, True)
    w_in_chip = _reduce_halves(["w_in"], [dw_in], [w_in_sibling])
    (up, down), ((landed["w_in"],),) = _sum_adamw(
        "update_w_ffn_up_down", [param("w_ffn_up", True), param("w_ffn_down")], tasks=[_ChipScatter(w_in_chip)])
    keep("w_ffn_up", up, True)
    keep("w_ffn_down", down)
    (attn, pool), _ = _sum_adamw("update_w_branch", [param("w_branch_attn"), param("w_branch_pool")])
    keep("w_branch_attn", attn)
    keep("w_branch_pool", pool)
    for n in ("w_out", "w_in"):
        (updated,), _ = _sum_adamw("update_" + n, [param(n)])
        keep(n, updated)
    names = SMALL + ("conv_w",)
    small_sum = _sum_leading("sum_small", all_small)
    *small_grads, conv_w_grad, loss = _unpack(small_sum, [weights[n].shape for n in SMALL] + [(3, D_FF), (8, LANE)])
    loss = loss[0, 0]
    grads.update(zip(SMALL, small_grads))
    grads["conv_w"] = lax.dynamic_slice(conv_w_grad, (0, me * FF_SHARD), (3, FF_SHARD))
    flat = lambda d: _pack([d[n] for n in names])
    shapes = [weights[n].shape for n in names]
    for out, packed in zip((delta, new_m, new_v),
                           _adamw("adamw_small", flat(weights), flat(grads), flat(m_in), flat(v_in))):
        out.update(zip(names, _unpack(packed, shapes)))

    order = ("w_in", "b_gate", "w_branch_attn", "w_pool", "pool_scale", "w_branch_pool", "w_out", "ln1_g", "ln1_b",
             "w_ffn_gate", "w_ffn_up", "conv_w", "conv_b", "w_ffn_down", "ln2_g", "ln2_b")
    lead = lambda t: t[None]
    return (loss, lead(grad_x), *[lead(grads[n]) for n in order], *[lead(delta[n]) for n in order],
            *[lead(new_m[n]) for n in order], *[lead(new_v[n]) for n in order])
```

```python
import functools
import math

import jax
import jax.numpy as jnp
from jax import lax
from jax.experimental import pallas as pl
from jax.experimental.pallas import tpu as pltpu

F32 = jnp.float32
BF16 = jnp.bfloat16

D_MODEL = 1024
N_HEADS = 8
HEAD_DIM = 64
D_ATTN = N_HEADS * HEAD_DIM
MOBA_BLOCK = 256
MOBA_TOPK = 3
ROPE_THETA = 10000.0
POOL_WINDOWS = (2, 4, 8, 16)
POOL_GROUP = 128
D_POOL = len(POOL_WINDOWS) * POOL_GROUP
POOL_HALO = 16
D_FF = 2816
D_IN_PROJ = 3 * D_ATTN + D_POOL + 2 * D_MODEL
LN_EPS = 1e-5
ALPHA = 2.0 ** 0.25
NEG = -1e30
N_DEV = 8
FF_SHARD = D_FF // N_DEV

ADAM_LR = 0.001
ADAM_B1 = 0.9
ADAM_B2 = 0.999
ADAM_EPS = 1e-08
ADAM_WD = 0.01
ADAM_STEP = 10

TOK = 256
FF_CHUNK = 256
LANE = 128
VMEM_LIMIT = 56 * 1024 * 1024

MESH = pl.DeviceIdType.MESH
NT_DIMS = (((1,), (1,)), ((), ()))
TN_DIMS = (((0,), (0,)), ((), ()))


def _params(*sem):
    return pltpu.CompilerParams(dimension_semantics=sem or None, vmem_limit_bytes=VMEM_LIMIT)


def _full(shape):
    zeros = (0,) * len(shape)
    return pl.BlockSpec(shape, lambda *_: zeros, pipeline_mode=pl.Buffered(1))


def _rows(width, tile=TOK):
    return pl.BlockSpec((tile, width), lambda i: (i, 0))


def _sds(shape, dtype):
    return jax.ShapeDtypeStruct(shape, dtype)


def _dot(a, b):
    return jnp.dot(a, b, preferred_element_type=F32)


def _dot_nt(a, b):
    return lax.dot_general(a, b, NT_DIMS, preferred_element_type=F32)


def _dot_tn(a, b):
    return lax.dot_general(a, b, TN_DIMS, preferred_element_type=F32)


def _rope_tables(seq):
    half = HEAD_DIM // 2
    inv_freq = 1.0 / (ROPE_THETA ** (jnp.arange(half, dtype=F32) / half))
    ang = jnp.arange(seq, dtype=F32)[:, None] * inv_freq[None, :]
    cos, sin = jnp.cos(ang), jnp.sin(ang)
    return jnp.tile(cos, (1, 4)), jnp.tile(jnp.concatenate([-sin, sin], axis=1), (1, 2))


def _swap_halves(t):
    lane = lax.broadcasted_iota(jnp.int32, t.shape, 1)
    return jnp.where((lane % HEAD_DIM) < HEAD_DIM // 2, pltpu.roll(t, LANE - 32, 1), pltpu.roll(t, 32, 1))


def _rope(t, cos, sin):
    return t * cos + _swap_halves(t) * sin


def _rope_transposed(g, cos, sin):
    return g * cos + _swap_halves(g * sin)


def _ln_fwd(r, g, b):
    mu = jnp.mean(r, axis=-1, keepdims=True)
    xc = r - mu
    var = jnp.mean(xc * xc, axis=-1, keepdims=True)
    rstd = lax.rsqrt(var + LN_EPS)
    xhat = xc * rstd
    return xhat * g + b, xhat, rstd


def _ln_bwd(dy, xhat, rstd, g):
    dxh = dy * g
    m1 = jnp.mean(dxh, axis=-1, keepdims=True)
    m2 = jnp.mean(dxh * xhat, axis=-1, keepdims=True)
    return rstd * (dxh - m1 - xhat * m2)


def _normal_cdf(a):
    return 0.5 * (1.0 + lax.erf(a * (1.0 / math.sqrt(2.0))))


def _gelu_derivative(a, cdf):
    return cdf + a * (jnp.exp(-0.5 * a * a) * (1.0 / math.sqrt(2.0 * math.pi)))


def _shift_down(a, k):
    row = lax.broadcasted_iota(jnp.int32, a.shape, 0)
    return jnp.where(row >= k, pltpu.roll(a, k, 0), 0.0)


def _shift_up(a, k):
    n = a.shape[0]
    row = lax.broadcasted_iota(jnp.int32, a.shape, 0)
    return jnp.where(row < n - k, pltpu.roll(a, n - k, 0), 0.0)


def _conv(a, cw, cb):
    return cw[2:3, :] * a + cw[1:2, :] * _shift_down(a, 1) + cw[0:1, :] * _shift_down(a, 2) + cb


def _pool_count(first_row, rows, window):
    t = first_row + lax.broadcasted_iota(jnp.int32, (rows, 1), 0)
    return jnp.minimum(t + 1, window).astype(F32)


def _grid_call(body, name, steps, in_specs, out_specs, out_shape, operands, scratch=(), tasks=()):
    t_operands, t_in_specs, t_out_shape, t_out_specs, t_sems = _task_args(tasks)
    outs = pl.pallas_call(
        _carry(body, tasks, len(in_specs), len(out_specs), len(scratch), steps), name=name, grid=(steps,),
        in_specs=list(in_specs) + t_in_specs, out_specs=list(out_specs) + t_out_specs,
        out_shape=list(out_shape) + t_out_shape, scratch_shapes=list(scratch) + t_sems,
        compiler_params=_params("arbitrary"),
    )(*operands, *t_operands)
    return outs[:len(out_specs)], _task_results(tasks, outs[len(out_specs):])


def _proj_in(x, win, b_gate, cos, sin, tasks=()):
    seq = x.shape[0]
    nt = seq // TOK

    def body(x_ref, win_ref, bg_ref, cos_ref, sin_ref, xt_ref, q_ref, k_ref, v_ref, u_ref, g_ref, km_ref):
        xb = x_ref[...].astype(BF16)
        xt_ref[...] = x_ref[...].T.astype(BF16)
        cos_t, sin_t = cos_ref[...], sin_ref[...]
        for sec, out_ref in ((0, q_ref), (1, k_ref)):
            z = _dot(xb, win_ref[sec])
            for c in range(D_ATTN // LANE):
                cols = slice(LANE * c, LANE * (c + 1))
                out_ref[:, cols] = _rope(z[:, cols], cos_t, sin_t)
        for b in range(TOK // MOBA_BLOCK):
            km_ref[b] = jnp.mean(k_ref[MOBA_BLOCK * b:MOBA_BLOCK * (b + 1), :], axis=0, keepdims=True)
        v_ref[...] = _dot(xb, win_ref[2]).astype(BF16)
        u_ref[...] = _dot(xb, win_ref[3])
        for n in range(4):
            cols = slice(D_ATTN * n, D_ATTN * (n + 1))
            g_ref[:, cols] = jax.nn.sigmoid(_dot(xb, win_ref[4 + n]) + bg_ref[:, cols])

    return _grid_call(
        body, "proj_in", nt,
        in_specs=[_rows(D_MODEL), _full(win.shape), _full((1, 2 * D_MODEL)), _rows(LANE), _rows(LANE)],
        out_specs=[pl.BlockSpec((D_MODEL, TOK), lambda i: (0, i)), _rows(D_ATTN), _rows(D_ATTN), _rows(D_ATTN),
                   _rows(D_POOL), _rows(2 * D_MODEL),
                   pl.BlockSpec((TOK // MOBA_BLOCK, 1, D_ATTN), lambda i: (i, 0, 0))],
        out_shape=[_sds((D_MODEL, seq), BF16), _sds((seq, D_ATTN), F32), _sds((seq, D_ATTN), F32),
                   _sds((seq, D_ATTN), BF16), _sds((seq, D_POOL), F32), _sds((seq, 2 * D_MODEL), F32),
                   _sds((seq // MOBA_BLOCK, 1, D_ATTN), F32)],
        operands=(x, win, b_gate, cos, sin), tasks=tasks)


SCORE_CHUNK = 128


def _store_keys(ka_sc, k_ref, ls):
    seq = ka_sc.shape[0]
    ka_sc[:, 0:HEAD_DIM] = k_ref[:, ls].astype(BF16)
    row = lax.broadcasted_iota(jnp.int32, (seq, HEAD_DIM), 0)
    lane = lax.broadcasted_iota(jnp.int32, (seq, HEAD_DIM), 1)
    in_block = (lane * MOBA_BLOCK <= row) & (row < (lane + 1) * MOBA_BLOCK)
    ka_sc[:, HEAD_DIM:] = jnp.where(in_block, 1.0, 0.0).astype(BF16)


def _block_bias(qf, km, i):
    if i <= MOBA_TOPK:
        return jnp.zeros((MOBA_BLOCK, HEAD_DIM), BF16)
    nb = km.shape[0]
    gate = lax.dot_general(km, qf, NT_DIMS, precision=lax.Precision.HIGHEST, preferred_element_type=F32)
    blk = lax.broadcasted_iota(jnp.int32, gate.shape, 0)
    rank = jnp.zeros(gate.shape, F32)
    for r in range(1, i):
        lower = pltpu.roll(gate, r, 0)
        rank = rank + jnp.where((blk >= r) & (lower >= gate), 1.0, 0.0)
        higher = pltpu.roll(gate, nb - r, 0)
        rank = rank + jnp.where((blk + r < i) & (higher > gate), 1.0, 0.0)
    bias = jnp.where((blk < i) & (rank >= MOBA_TOPK), NEG, 0.0)
    padded = jnp.concatenate([bias, jnp.zeros((LANE - nb, MOBA_BLOCK), F32)], axis=0)
    return jnp.transpose(padded)[:, 0:HEAD_DIM].astype(BF16)


def _causal(shape, transposed=False):
    row = lax.broadcasted_iota(jnp.int32, shape, 0)
    col = lax.broadcasted_iota(jnp.int32, shape, 1)
    return (row <= col) if transposed else (col <= row)


def _row_vector(col):
    return jnp.transpose(jnp.broadcast_to(col, (MOBA_BLOCK, LANE)))[0:1, :]


def _attn_fwd(q, k, v, kmean, tasks=()):
    seq = q.shape[0]
    nb = seq // MOBA_BLOCK
    assert nb == 8, "the block ranking keeps one sublane per key block"
    pair = pl.BlockSpec((seq, LANE), lambda p: (0, p))
    heads = LANE // HEAD_DIM

    def body(q_ref, k_ref, v_ref, km_ref, o_ref, lse_ref, bias_ref, ka_sc, qa_sc, s_sc, p_sc):
        lse_ref[0, heads:, :] = jnp.zeros((8 - heads, seq), F32)
        for hh in range(heads):
            ls = slice(HEAD_DIM * hh, HEAD_DIM * (hh + 1))
            _store_keys(ka_sc, k_ref, ls)
            vb = v_ref[:, ls]
            km = km_ref[:, ls]
            for i in range(nb):
                rs = slice(MOBA_BLOCK * i, MOBA_BLOCK * (i + 1))
                width = MOBA_BLOCK * (i + 1)
                qf = q_ref[rs, ls]
                bias = _block_bias(qf, km, i)
                bias_ref[rs, ls] = bias
                qa_sc[:, 0:HEAD_DIM] = (qf * HEAD_DIM ** -0.5).astype(BF16)
                qa_sc[:, HEAD_DIM:] = bias
                s_sc[:, 0:width] = _dot_nt(qa_sc[...], ka_sc[0:width, :])
                s_sc[:, rs] = jnp.where(_causal((MOBA_BLOCK, MOBA_BLOCK)), s_sc[:, rs], NEG)
                chunks = [slice(SCORE_CHUNK * c, SCORE_CHUNK * (c + 1)) for c in range(width // SCORE_CHUNK)]
                top = s_sc[:, chunks[0]]
                for c in chunks[1:]:
                    top = jnp.maximum(top, s_sc[:, c])
                m = jnp.max(top, axis=1, keepdims=True)
                total = jnp.zeros((MOBA_BLOCK, SCORE_CHUNK), F32)
                for c in chunks:
                    p = jnp.exp(s_sc[:, c] - m)
                    total = total + p
                    p_sc[:, c] = p.astype(BF16)
                l = jnp.sum(total, axis=1, keepdims=True)
                o_ref[rs, ls] = _dot(p_sc[:, 0:width], vb[0:width]) / l
                lse_ref[0, hh:hh + 1, rs] = _row_vector(m + jnp.log(l))

    return _grid_call(
        body, "attn_fwd", D_ATTN // LANE,
        in_specs=[pair, pair, pair, pl.BlockSpec((nb, LANE), lambda p: (0, p))],
        out_specs=[pair, pl.BlockSpec((1, 8, seq), lambda p: (p, 0, 0)), pair],
        out_shape=[_sds((seq, D_ATTN), F32), _sds((D_ATTN // LANE, 8, seq), F32), _sds((seq, D_ATTN), BF16)],
        operands=(q, k, v, kmean),
        scratch=[pltpu.VMEM((seq, LANE), BF16), pltpu.VMEM((MOBA_BLOCK, LANE), BF16),
                 pltpu.VMEM((MOBA_BLOCK, seq), F32), pltpu.VMEM((MOBA_BLOCK, seq), BF16)],
        tasks=tasks)


def _mix(o, u, g, x, wba, wbp, wout, w_pool, pool_scale, ln_g, ln_b, tasks=()):
    seq = x.shape[0]

    def body(o_ref, u_ref, uprev_ref, g_ref, x_ref, wba_ref, wbp_ref, wout_ref, wp_ref, ps_ref, lg_ref, lb_ref,
             ya_ref, yp_ref, pooled_ref, mixed_ref, ypre_ref, merged_ref, xhat_ref, rstd_ref, h_ref, hb_ref, ext):
        i = pl.program_id(0)
        ya = _dot(o_ref[...].astype(BF16), wba_ref[...])
        ucur = u_ref[...]
        ext[0:POOL_HALO, :] = jnp.where(i > 0, uprev_ref[...], 0.0)
        ext[POOL_HALO:, :] = ucur
        for grp, window in enumerate(POOL_WINDOWS):
            cols = slice(POOL_GROUP * grp, POOL_GROUP * (grp + 1))
            acc = ucur[:, cols]
            for kk in range(1, window):
                acc = acc + ext[pl.ds(POOL_HALO - kk, TOK), cols]
            pooled = acc / _pool_count(i * TOK, TOK, window) - ucur[:, cols]
            pooled_ref[:, cols] = pooled.astype(BF16)
            mixed_ref[:, cols] = _dot(pooled.astype(BF16), wp_ref[grp].astype(BF16))
        mixed = mixed_ref[...]
        ypre = (mixed * ps_ref[...]).astype(BF16)
        ypre_ref[...] = ypre
        yp = _dot(ypre, wbp_ref[...])
        ya_ref[...] = ya
        yp_ref[...] = yp
        merged = (g_ref[:, :D_MODEL] * ya + g_ref[:, D_MODEL:] * yp).astype(BF16)
        merged_ref[...] = merged
        r1 = ALPHA * x_ref[...] + _dot(merged, wout_ref[...])
        h, xhat, rstd = _ln_fwd(r1, lg_ref[...], lb_ref[...])
        xhat_ref[...] = xhat
        rstd_ref[...] = jnp.broadcast_to(rstd, (TOK, LANE))
        h_ref[...] = h
        hb_ref[...] = h.astype(BF16)

    halo = pl.BlockSpec((POOL_HALO, D_POOL), lambda i: (jnp.maximum(i * (TOK // POOL_HALO) - 1, 0), 0))
    return _grid_call(
        body, "mix", seq // TOK,
        in_specs=[_rows(D_ATTN), _rows(D_POOL), halo, _rows(2 * D_MODEL), _rows(D_MODEL),
                  _full(wba.shape), _full(wbp.shape), _full(wout.shape), _full(w_pool.shape),
                  _full((1, D_POOL)), _full((1, D_MODEL)), _full((1, D_MODEL))],
        out_specs=[_rows(D_MODEL), _rows(D_MODEL), _rows(D_POOL), _rows(D_POOL), _rows(D_POOL), _rows(D_MODEL),
                   _rows(D_MODEL), _rows(LANE), _rows(D_MODEL), _rows(D_MODEL)],
        out_shape=[_sds((seq, D_MODEL), F32), _sds((seq, D_MODEL), F32), _sds((seq, D_POOL), BF16),
                   _sds((seq, D_POOL), F32), _sds((seq, D_POOL), BF16), _sds((seq, D_MODEL), BF16),
                   _sds((seq, D_MODEL), F32), _sds((seq, LANE), F32), _sds((seq, D_MODEL), F32),
                   _sds((seq, D_MODEL), BF16)],
        operands=(o, u, u, g, x, wba, wbp, wout, w_pool, pool_scale, ln_g, ln_b),
        scratch=[pltpu.VMEM((TOK + POOL_HALO, D_POOL), F32)], tasks=tasks)


def _ffn_up(hb, wgt, wut, conv_w, conv_b, tasks=()):
    seq = hb.shape[0]
    wblk = pl.BlockSpec((FF_CHUNK, D_MODEL), lambda c: (c, 0))
    cblk = lambda rows: pl.BlockSpec((rows, FF_CHUNK), lambda c: (0, c))
    oblk = pl.BlockSpec((seq, FF_CHUNK), lambda c: (0, c))

    def body(h_ref, wg_ref, wu_ref, cw_ref, cb_ref, a_ref, u_ref, act_ref):
        h = h_ref[...]
        a = _dot_nt(h, wg_ref[...])
        u = _dot_nt(h, wu_ref[...])
        a_ref[...] = a
        u_ref[...] = u
        ac = _conv(a, cw_ref[...], cb_ref[...])
        act_ref[...] = (ac * _normal_cdf(ac) * u).astype(BF16)

    return _grid_call(
        body, "ffn_up", D_FF // FF_CHUNK,
        in_specs=[_full(hb.shape), wblk, wblk, cblk(3), cblk(1)],
        out_specs=[oblk, oblk, oblk],
        out_shape=[_sds((seq, D_FF), F32), _sds((seq, D_FF), F32), _sds((seq, D_FF), BF16)],
        operands=(hb, wgt, wut, conv_w, conv_b), tasks=tasks)


def _ffn_down(act, wd, h, target, ln_g, ln_b):
    seq = h.shape[0]

    def body(act_ref, wd_ref, h_ref, t_ref, lg_ref, lb_ref, dr_ref, drb_ref, loss_ref, dg_ref, db_ref):
        i = pl.program_id(0)

        @pl.when(i == 0)
        def _():
            loss_ref[...] = jnp.zeros_like(loss_ref)
            dg_ref[...] = jnp.zeros_like(dg_ref)
            db_ref[...] = jnp.zeros_like(db_ref)

        r2 = ALPHA * h_ref[...] + _dot(act_ref[...], wd_ref[...])
        y, xhat, rstd = _ln_fwd(r2, lg_ref[...], lb_ref[...])
        diff = y - t_ref[...]
        loss_ref[...] += jnp.sum(diff * diff) * (0.5 / D_MODEL)
        dy = diff * (1.0 / D_MODEL)
        dg_ref[...] += jnp.sum(dy * xhat, axis=0, keepdims=True)
        db_ref[...] += jnp.sum(dy, axis=0, keepdims=True)
        dr = _ln_bwd(dy, xhat, rstd, lg_ref[...])
        dr_ref[...] = dr
        drb_ref[...] = dr.astype(BF16)

    vec = pl.BlockSpec((1, D_MODEL), lambda i: (0, 0))
    return pl.pallas_call(
        body, name="ffn_down", grid=(seq // TOK,),
        in_specs=[_rows(D_FF), _full(wd.shape), _rows(D_MODEL), _rows(D_MODEL), _full((1, D_MODEL)), _full((1, D_MODEL))],
        out_specs=[_rows(D_MODEL), _rows(D_MODEL), pl.BlockSpec((8, LANE), lambda i: (0, 0)), vec, vec],
        out_shape=[_sds((seq, D_MODEL), F32), _sds((seq, D_MODEL), BF16), _sds((8, LANE), F32),
                   _sds((1, D_MODEL), F32), _sds((1, D_MODEL), F32)],
        compiler_params=_params("arbitrary"),
    )(act, wd, h, target, ln_g, ln_b)


def _ffn_bwd(drb, hb, a, u, wd, conv_w, conv_b):
    seq = hb.shape[0]
    wblk = pl.BlockSpec((FF_CHUNK, D_MODEL), lambda c: (c, 0))
    cblk = lambda rows: pl.BlockSpec((rows, FF_CHUNK), lambda c: (0, c))
    sblk = pl.BlockSpec((seq, FF_CHUNK), lambda c: (0, c))

    def body(dr_ref, h_ref, a_ref, u_ref, wd_ref, cw_ref, cb_ref, da_ref, du_ref, dwd_ref, dwg_ref, dwu_ref, dc_ref):
        dr = dr_ref[...]
        h = h_ref[...]
        a = a_ref[...]
        u = u_ref[...]
        cw = cw_ref[...]
        dact = _dot_nt(dr, wd_ref[...])
        ac = _conv(a, cw, cb_ref[...])
        cdf = _normal_cdf(ac)
        gelu = ac * cdf
        dwd_ref[...] = _dot_tn((gelu * u).astype(BF16), dr).astype(BF16)
        du = (dact * gelu).astype(BF16)
        dac = dact * u * _gelu_derivative(ac, cdf)
        da = (cw[2:3, :] * dac + cw[1:2, :] * _shift_up(dac, 1) + cw[0:1, :] * _shift_up(dac, 2)).astype(BF16)
        da_ref[...] = da
        du_ref[...] = du
        dwg_ref[...] = _dot_tn(da, h).astype(BF16)
        dwu_ref[...] = _dot_tn(du, h).astype(BF16)
        dc_ref[0:1, :] = jnp.sum(dac * _shift_down(a, 2), axis=0, keepdims=True)
        dc_ref[1:2, :] = jnp.sum(dac * _shift_down(a, 1), axis=0, keepdims=True)
        dc_ref[2:3, :] = jnp.sum(dac * a, axis=0, keepdims=True)
        dc_ref[3:4, :] = jnp.sum(dac, axis=0, keepdims=True)
        dc_ref[4:8, :] = jnp.zeros((4, FF_CHUNK), F32)

    return pl.pallas_call(
        body, name="ffn_bwd", grid=(D_FF // FF_CHUNK,),
        in_specs=[_full(drb.shape), _full(hb.shape), sblk, sblk, wblk, cblk(3), cblk(1)],
        out_specs=[sblk, sblk, wblk, wblk, wblk, cblk(8)],
        out_shape=[_sds((seq, D_FF), BF16), _sds((seq, D_FF), BF16), _sds((D_FF, D_MODEL), BF16),
                   _sds((D_FF, D_MODEL), BF16), _sds((D_FF, D_MODEL), BF16), _sds((8, D_FF), F32)],
        compiler_params=_params("parallel"),
    )(drb, hb, a, u, wd, conv_w, conv_b)


def _ln1_bwd(dr2, da, du, wgt, wut, xhat, rstd, ln_g, tasks=()):
    seq = dr2.shape[0]

    def body(dr2_ref, da_ref, du_ref, wg_ref, wu_ref, xhat_ref, rstd_ref, lg_ref, dr_ref, drb_ref, dg_ref, db_ref):
        @pl.when(pl.program_id(0) == 0)
        def _():
            dg_ref[...] = jnp.zeros_like(dg_ref)
            db_ref[...] = jnp.zeros_like(db_ref)

        dh = ALPHA * dr2_ref[...] + _dot(da_ref[...], wg_ref[...]) + _dot(du_ref[...], wu_ref[...])
        xhat = xhat_ref[...]
        dg_ref[...] += jnp.sum(dh * xhat, axis=0, keepdims=True)
        db_ref[...] += jnp.sum(dh, axis=0, keepdims=True)
        dr = _ln_bwd(dh, xhat, rstd_ref[:, 0:1], lg_ref[...])
        dr_ref[...] = dr
        drb_ref[...] = dr.astype(BF16)

    vec = pl.BlockSpec((1, D_MODEL), lambda i: (0, 0))
    return _grid_call(
        body, "ln1_bwd", seq // TOK,
        in_specs=[_rows(D_MODEL), _rows(D_FF), _rows(D_FF), _full(wgt.shape), _full(wut.shape), _rows(D_MODEL),
                  _rows(LANE), _full((1, D_MODEL))],
        out_specs=[_rows(D_MODEL), _rows(D_MODEL), vec, vec],
        out_shape=[_sds((seq, D_MODEL), F32), _sds((seq, D_MODEL), BF16), _sds((1, D_MODEL), F32),
                   _sds((1, D_MODEL), F32)],
        operands=(dr2, da, du, wgt, wut, xhat, rstd, ln_g), tasks=tasks)


def _mix_bwd(drb, ya, yp, g, mixed, wout, wba, wbp, w_pool, pool_scale, tasks=()):
    seq = drb.shape[0]

    def body(dr_ref, ya_ref, yp_ref, g_ref, mixed_ref, wout_ref, wba_ref, wbp_ref, wp_ref, ps_ref,
             dzg_ref, dya_ref, dyp_ref, do_ref, dmixed_ref, dpooled_ref, dbg_ref, dps_ref):
        @pl.when(pl.program_id(0) == 0)
        def _():
            dbg_ref[...] = jnp.zeros_like(dbg_ref)
            dps_ref[...] = jnp.zeros_like(dps_ref)

        dmerged = _dot_nt(dr_ref[...], wout_ref[...])
        ga, gp = g_ref[:, :D_MODEL], g_ref[:, D_MODEL:]
        dzga = dmerged * ya_ref[...] * ga * (1.0 - ga)
        dzgp = dmerged * yp_ref[...] * gp * (1.0 - gp)
        dzg_ref[:, :D_MODEL] = dzga.astype(BF16)
        dzg_ref[:, D_MODEL:] = dzgp.astype(BF16)
        dbg_ref[:, :D_MODEL] += jnp.sum(dzga, axis=0, keepdims=True)
        dbg_ref[:, D_MODEL:] += jnp.sum(dzgp, axis=0, keepdims=True)
        dya = (dmerged * ga).astype(BF16)
        dyp = (dmerged * gp).astype(BF16)
        dya_ref[...] = dya
        dyp_ref[...] = dyp
        do_ref[...] = _dot_nt(dya, wba_ref[...])
        dypre = _dot_nt(dyp, wbp_ref[...])
        dps_ref[...] += jnp.sum(dypre * mixed_ref[...], axis=0, keepdims=True)
        dmixed = (dypre * ps_ref[...]).astype(BF16)
        dmixed_ref[...] = dmixed
        for grp in range(len(POOL_WINDOWS)):
            cols = slice(POOL_GROUP * grp, POOL_GROUP * (grp + 1))
            dpooled_ref[:, cols] = _dot_nt(dmixed[:, cols], wp_ref[grp].astype(BF16))

    return _grid_call(
        body, "mix_bwd", seq // TOK,
        in_specs=[_rows(D_MODEL), _rows(D_MODEL), _rows(D_MODEL), _rows(2 * D_MODEL), _rows(D_POOL),
                  _full(wout.shape), _full(wba.shape), _full(wbp.shape), _full(w_pool.shape), _full((1, D_POOL))],
        out_specs=[_rows(2 * D_MODEL), _rows(D_MODEL), _rows(D_MODEL), _rows(D_ATTN), _rows(D_POOL), _rows(D_POOL),
                   pl.BlockSpec((1, 2 * D_MODEL), lambda i: (0, 0)), pl.BlockSpec((1, D_POOL), lambda i: (0, 0))],
        out_shape=[_sds((seq, 2 * D_MODEL), BF16), _sds((seq, D_MODEL), BF16), _sds((seq, D_MODEL), BF16),
                   _sds((seq, D_ATTN), F32), _sds((seq, D_POOL), BF16), _sds((seq, D_POOL), F32),
                   _sds((1, 2 * D_MODEL), F32), _sds((1, D_POOL), F32)],
        operands=(drb, ya, yp, g, mixed, wout, wba, wbp, w_pool, pool_scale), tasks=tasks)


def _attn_bwd(q, k, v, bias, o, lse, do, cos, sin, tasks=()):
    seq = q.shape[0]
    nb = seq // MOBA_BLOCK
    pair = pl.BlockSpec((seq, LANE), lambda p: (0, p))
    table = pl.BlockSpec((seq, LANE), lambda p: (0, 0))
    scale = HEAD_DIM ** -0.5

    def body(q_ref, k_ref, v_ref, bias_ref, o_ref, lse_ref, do_ref, cos_ref, sin_ref, dq_ref, dk_ref, dv_ref,
             dq_acc, dk_acc, dv_acc, dk_head, dv_head, ka_sc, qa_sc, s_sc, dp_sc, p_sc, ds_sc):
        for hh in range(LANE // HEAD_DIM):
            ls = slice(HEAD_DIM * hh, HEAD_DIM * (hh + 1))
            _store_keys(ka_sc, k_ref, ls)
            vb = v_ref[:, ls]
            dk_head[...] = jnp.zeros_like(dk_head)
            dv_head[...] = jnp.zeros_like(dv_head)
            for i in range(nb):
                rs = slice(MOBA_BLOCK * i, MOBA_BLOCK * (i + 1))
                width = MOBA_BLOCK * (i + 1)
                qa_sc[:, 0:HEAD_DIM] = (q_ref[rs, ls] * scale).astype(BF16)
                qa_sc[:, HEAD_DIM:] = bias_ref[rs, ls]
                s_sc[0:width, :] = _dot_nt(ka_sc[0:width, :], qa_sc[...])
                s_sc[rs, :] = jnp.where(_causal((MOBA_BLOCK, MOBA_BLOCK), transposed=True), s_sc[rs, :], NEG)
                dob = do_ref[rs, ls]
                delta = _row_vector(jnp.sum(dob * o_ref[rs, ls], axis=1, keepdims=True))
                lse_row = lse_ref[0, hh:hh + 1, rs]
                dob16 = dob.astype(BF16)
                dp_sc[0:width, :] = _dot_nt(vb[0:width], dob16)
                for c in range(width // SCORE_CHUNK):
                    rows = slice(SCORE_CHUNK * c, SCORE_CHUNK * (c + 1))
                    p = jnp.exp(s_sc[rows, :] - lse_row)
                    p_sc[rows, :] = p.astype(BF16)
                    ds_sc[rows, :] = (p * (dp_sc[rows, :] - delta)).astype(BF16)
                dv_head[0:width, :] += _dot(p_sc[0:width, :], dob16)
                dk_head[0:width, :] += _dot(ds_sc[0:width, :], qa_sc[:, 0:HEAD_DIM])
                dq_acc[rs, ls] = _dot_tn(ds_sc[0:width, :], ka_sc[0:width, 0:HEAD_DIM]) * scale
            dk_acc[:, ls] = dk_head[...]
            dv_acc[:, ls] = dv_head[...]
        cos_t, sin_t = cos_ref[...], sin_ref[...]
        dq_ref[...] = _rope_transposed(dq_acc[...], cos_t, sin_t).astype(BF16)
        dk_ref[...] = _rope_transposed(dk_acc[...], cos_t, sin_t).astype(BF16)
        dv_ref[...] = dv_acc[...].astype(BF16)

    return _grid_call(
        body, "attn_bwd", D_ATTN // LANE,
        in_specs=[pair, pair, pair, pair, pair, pl.BlockSpec((1, 8, seq), lambda p: (p, 0, 0)), pair, table, table],
        out_specs=[pair, pair, pair], out_shape=[_sds((seq, D_ATTN), BF16)] * 3,
        operands=(q, k, v, bias, o, lse, do, cos, sin),
        scratch=[pltpu.VMEM((seq, LANE), F32)] * 3 + [pltpu.VMEM((seq, HEAD_DIM), F32)] * 2
        + [pltpu.VMEM((seq, LANE), BF16), pltpu.VMEM((MOBA_BLOCK, LANE), BF16)]
        + [pltpu.VMEM((seq, MOBA_BLOCK), F32)] * 2 + [pltpu.VMEM((seq, MOBA_BLOCK), BF16)] * 2,
        tasks=tasks)


def _in_bwd(dq, dk, dv, dpooled, dzg, dr1, win, tasks=()):
    seq = dr1.shape[0]
    nt = seq // TOK

    def body(dq_ref, dk_ref, dv_ref, dp_ref, dpnext_ref, dzg_ref, dr_ref, win_ref, dx_ref, dz_ref, ext):
        i = pl.program_id(0)
        dp = dp_ref[...]
        dpn = jnp.where(i < nt - 1, dpnext_ref[...], 0.0)
        for grp, window in enumerate(POOL_WINDOWS):
            cols = slice(POOL_GROUP * grp, POOL_GROUP * (grp + 1))
            ext[0:TOK, cols] = dp[:, cols] / _pool_count(i * TOK, TOK, window)
            ext[TOK:, cols] = dpn[:, cols] / _pool_count((i + 1) * TOK, POOL_HALO, window)
        for grp, window in enumerate(POOL_WINDOWS):
            cols = slice(POOL_GROUP * grp, POOL_GROUP * (grp + 1))
            acc = ext[0:TOK, cols] - dp[:, cols]
            for kk in range(1, window):
                acc = acc + ext[pl.ds(kk, TOK), cols]
            dz_ref[:, 3 * D_ATTN + POOL_GROUP * grp:3 * D_ATTN + POOL_GROUP * (grp + 1)] = acc.astype(BF16)
        dz_ref[:, 0:D_ATTN] = dq_ref[...]
        dz_ref[:, D_ATTN:2 * D_ATTN] = dk_ref[...]
        dz_ref[:, 2 * D_ATTN:3 * D_ATTN] = dv_ref[...]
        dz_ref[:, 3 * D_ATTN + D_POOL:] = dzg_ref[...]
        dx = ALPHA * dr_ref[...]
        for n in range(N_DEV):
            dx = dx + _dot_nt(dz_ref[:, D_ATTN * n:D_ATTN * (n + 1)], win_ref[n])
        dx_ref[...] = dx

    halo = pl.BlockSpec((POOL_HALO, D_POOL),
                        lambda i: (jnp.minimum((i + 1) * (TOK // POOL_HALO), seq // POOL_HALO - 1), 0))
    return _grid_call(
        body, "in_bwd", nt,
        in_specs=[_rows(D_ATTN), _rows(D_ATTN), _rows(D_ATTN), _rows(D_POOL), halo, _rows(2 * D_MODEL),
                  _rows(D_MODEL), _full(win.shape)],
        out_specs=[_rows(D_MODEL), _rows(D_IN_PROJ)],
        out_shape=[_sds((seq, D_MODEL), F32), _sds((seq, D_IN_PROJ), BF16)],
        operands=(dq, dk, dv, dpooled, dpooled, dzg, dr1, win),
        scratch=[pltpu.VMEM((TOK + POOL_HALO, D_POOL), F32)], tasks=tasks)


def _dw_mixers(o, ypre, merged, dya, dyp, drb, pooled, dmixed, tasks=()):
    seq = o.shape[0]
    groups = len(POOL_WINDOWS)
    col = pl.BlockSpec((seq, LANE), lambda n: (0, n))
    grp = pl.BlockSpec((seq, POOL_GROUP), lambda n: (0, jnp.minimum(n, groups - 1)))
    owner = lambda rows, cols: pl.BlockSpec((1, rows, cols), lambda n: (n, 0, 0))

    def body(o_ref, ypre_ref, merged_ref, dya_ref, dyp_ref, dr_ref, pooled_ref, dmixed_ref,
             dba_ref, dbp_ref, dout_ref, dpool_ref, ob_sc):
        n = pl.program_id(0)

        @pl.when(n == 0)
        def _():
            ob_sc[...] = o_ref[...].astype(BF16)

        dba_ref[0] = _dot_tn(dya_ref[...], ob_sc[...]).T.astype(BF16)
        dbp_ref[0] = _dot_tn(dyp_ref[...], ypre_ref[...]).T.astype(BF16)
        dout_ref[0] = _dot_tn(merged_ref[...], dr_ref[...]).astype(BF16)

        @pl.when(n < groups)
        def _():
            dpool_ref[0] = _dot_tn(pooled_ref[...], dmixed_ref[...])

    return _grid_call(
        body, "dw_mixers", N_DEV,
        in_specs=[_full(o.shape), _full(ypre.shape), col, col, col, _full(drb.shape), grp, grp],
        out_specs=[owner(D_ATTN, LANE), owner(D_POOL, LANE), owner(D_MODEL // N_DEV, D_MODEL),
                   pl.BlockSpec((1, POOL_GROUP, POOL_GROUP), lambda n: (jnp.minimum(n, groups - 1), 0, 0))],
        out_shape=[_sds((N_DEV, D_ATTN, LANE), BF16), _sds((N_DEV, D_POOL, LANE), BF16),
                   _sds((N_DEV, D_MODEL // N_DEV, D_MODEL), BF16), _sds((groups, POOL_GROUP, POOL_GROUP), F32)],
        operands=(o, ypre, merged, dya, dyp, drb, pooled, dmixed),
        scratch=[pltpu.VMEM((seq, D_ATTN), BF16)], tasks=tasks)


def _to_bf16(arrays, tasks=()):
    n = len(arrays)

    def body(*refs):
        for src, dst in zip(refs[:n], refs[n:]):
            dst[...] = src[...].astype(BF16)

    return _grid_call(
        body, "to_bf16", 1, in_specs=[_full(a.shape) for a in arrays],
        out_specs=[pl.BlockSpec(a.shape, lambda i: (0, 0)) for a in arrays],
        out_shape=[_sds(a.shape, BF16) for a in arrays], operands=arrays, tasks=tasks)


def _matmul(name, a, b, out_shape, out_dtype, steps, a_spec, b_spec, o_spec, tasks=()):
    def body(a_ref, b_ref, o_ref):
        o_ref[...] = _dot(a_ref[...], b_ref[...]).reshape(o_ref.shape).astype(o_ref.dtype)

    (out,), results = _grid_call(body, name, steps, in_specs=[a_spec, b_spec], out_specs=[o_spec],
                                 out_shape=[_sds(out_shape, out_dtype)], operands=(a, b), tasks=tasks)
    return out, results


def _place():
    return lax.axis_index("x"), lax.axis_index("y"), lax.axis_index("c")


def _other_chips(x, y):
    return [(1 - x, y), (x, 1 - y), (1 - x, 1 - y)]


DMA_SEMS = pltpu.SemaphoreType.DMA


class _AllGather:
    def __init__(self, shards, lag=0):
        self.operands = list(shards)
        self.n = len(shards)
        self.lag = lag
        self.out_shape = [_sds((N_DEV, *s.shape), s.dtype) for s in shards]
        self.sems = [DMA_SEMS((7 * self.n,)), DMA_SEMS((7 * self.n,)), DMA_SEMS((self.n,))]

    def _copy(self, refs, a, k, block, to, from_input=False):
        ins, outs, (send_sems, recv_sems, _) = refs
        px, py, pc = block
        dst = outs[a].at[4 * px + 2 * py + pc]
        return pltpu.make_async_remote_copy(
            src_ref=ins[a] if from_input else dst, dst_ref=dst,
            send_sem=send_sems.at[7 * a + k], recv_sem=recv_sems.at[7 * a + k],
            device_id=to, device_id_type=MESH)

    def _local(self, refs, a):
        ins, outs, (_, _, local_sems) = refs
        x, y, c = _place()
        return pltpu.make_async_copy(ins[a], outs[a].at[4 * x + 2 * y + c], local_sems.at[a])

    def _pass_on(self, refs, a):
        x, y, c = _place()
        origin = ((x + 1 - c) % 2, (y + c) % 2, c)
        target = ((x + c) % 2, (y + 1 - c) % 2, c)
        return self._copy(refs, a, 3, origin, target)

    def start(self, refs):
        x, y, c = _place()
        for a in range(self.n):
            self._local(refs, a).start()
        for a in range(self.n):
            self._copy(refs, a, 0, (x, y, c), (x, y, 1 - c), True).start()
            for j, chip in enumerate(_other_chips(x, y)[:2]):
                self._copy(refs, a, 1 + j, (x, y, c), (*chip, c), True).start()

    def middle(self, refs):
        x, y, c = _place()
        me, sibling = (x, y, c), (x, y, 1 - c)
        chips = _other_chips(x, y)
        for a in range(self.n):
            for j in range(2):
                self._copy(refs, a, 1 + j, (*chips[j], c), me).wait_recv()
        for a in range(self.n):
            self._pass_on(refs, a).start()
            for j in range(2):
                self._copy(refs, a, 4 + j, (*chips[j], c), sibling).start()

    def late(self, refs):
        x, y, c = _place()
        diagonal = (1 - x, 1 - y, c)
        for a in range(self.n):
            self._copy(refs, a, 3, diagonal, (x, y, c)).wait_recv()
            self._copy(refs, a, 6, diagonal, (x, y, 1 - c)).start()

    def finish(self, refs):
        x, y, c = _place()
        me, sibling = (x, y, c), (x, y, 1 - c)
        chips = _other_chips(x, y)
        for a in range(self.n):
            self._copy(refs, a, 0, sibling, me).wait_recv()
            for j, chip in enumerate(chips):
                self._copy(refs, a, 4 + j, (*chip, 1 - c), me).wait_recv()
        for a in range(self.n):
            self._copy(refs, a, 0, me, sibling, True).wait_send()
            for j, chip in enumerate(chips[:2]):
                self._copy(refs, a, 1 + j, me, (*chip, c), True).wait_send()
            self._pass_on(refs, a).wait_send()
            for j, chip in enumerate(chips):
                self._copy(refs, a, 4 + j, (*chip, c), sibling).wait_send()
            self._local(refs, a).wait()


class _SiblingSend:
    def __init__(self, partials):
        self.operands = list(partials)
        self.n = len(partials)
        self.out_shape = [_sds((4, *p.shape[1:]), p.dtype) for p in partials]
        self.sems = [DMA_SEMS((4 * self.n,)), DMA_SEMS((4 * self.n,))]

    def _copy(self, refs, a, q):
        ins, outs, (send_sems, recv_sems) = refs
        x, y, c = _place()
        return pltpu.make_async_remote_copy(
            src_ref=ins[a].at[2 * q + 1 - c], dst_ref=outs[a].at[q],
            send_sem=send_sems.at[4 * a + q], recv_sem=recv_sems.at[4 * a + q],
            device_id=(x, y, 1 - c), device_id_type=MESH)

    def start(self, refs):
        for a in range(self.n):
            for q in range(4):
                self._copy(refs, a, q).start()

    def middle(self, refs):
        pass

    def finish(self, refs):
        for a in range(self.n):
            for q in range(4):
                self._copy(refs, a, q).wait()


class _ChipScatter:
    def __init__(self, chip_partials):
        self.operands = list(chip_partials)
        self.n = len(chip_partials)
        self.out_shape = [_sds(p.shape, p.dtype) for p in chip_partials]
        self.sems = [DMA_SEMS((3 * self.n,)), DMA_SEMS((3 * self.n,)), DMA_SEMS((self.n,))]

    def _copy(self, refs, a, k, arrival=False):
        ins, outs, (send_sems, recv_sems, _) = refs
        x, y, c = _place()
        px, py = _other_chips(x, y)[k]
        mine, theirs = 2 * x + y, 2 * px + py
        return pltpu.make_async_remote_copy(
            src_ref=ins[a].at[mine if arrival else theirs], dst_ref=outs[a].at[theirs if arrival else mine],
            send_sem=send_sems.at[3 * a + k], recv_sem=recv_sems.at[3 * a + k],
            device_id=(px, py, c), device_id_type=MESH)

    def _local(self, refs, a):
        ins, outs, (_, _, local_sems) = refs
        x, y, _ = _place()
        return pltpu.make_async_copy(ins[a].at[2 * x + y], outs[a].at[2 * x + y], local_sems.at[a])

    def start(self, refs):
        for a in range(self.n):
            self._local(refs, a).start()
            for k in range(3):
                self._copy(refs, a, k).start()

    def middle(self, refs):
        pass

    def finish(self, refs):
        for a in range(self.n):
            for k in range(3):
                self._copy(refs, a, k, arrival=True).wait_recv()
        for a in range(self.n):
            for k in range(3):
                self._copy(refs, a, k).wait_send()
            self._local(refs, a).wait()


class _DirectScatter:
    def __init__(self, partials):
        self.operands = list(partials)
        self.n = len(partials)
        self.out_shape = [_sds(p.shape, p.dtype) for p in partials]
        self.sems = [DMA_SEMS((7 * self.n,)), DMA_SEMS((7 * self.n,)), DMA_SEMS((self.n,))]

    def _copy(self, refs, a, k, arrival=False):
        ins, outs, (send_sems, recv_sems, _) = refs
        x, y, c = _place()
        peer = [(x, y, 1 - c), (1 - x, y, c), (x, 1 - y, c), (1 - x, 1 - y, c),
                (1 - x, y, 1 - c), (x, 1 - y, 1 - c), (1 - x, 1 - y, 1 - c)][k]
        mine, theirs = 4 * x + 2 * y + c, 4 * peer[0] + 2 * peer[1] + peer[2]
        return pltpu.make_async_remote_copy(
            src_ref=ins[a].at[mine if arrival else theirs], dst_ref=outs[a].at[theirs if arrival else mine],
            send_sem=send_sems.at[7 * a + k], recv_sem=recv_sems.at[7 * a + k],
            device_id=peer, device_id_type=MESH)

    def _local(self, refs, a):
        ins, outs, (_, _, local_sems) = refs
        x, y, c = _place()
        return pltpu.make_async_copy(ins[a].at[4 * x + 2 * y + c], outs[a].at[4 * x + 2 * y + c], local_sems.at[a])

    def start(self, refs):
        for a in range(self.n):
            self._local(refs, a).start()
            for k in range(7):
                self._copy(refs, a, k).start()

    def middle(self, refs):
        pass

    def finish(self, refs):
        for a in range(self.n):
            for k in range(7):
                self._copy(refs, a, k, arrival=True).wait_recv()
        for a in range(self.n):
            for k in range(7):
                self._copy(refs, a, k).wait_send()
            self._local(refs, a).wait()


def _task_args(tasks):
    hbm = pl.BlockSpec(memory_space=pl.ANY)
    operands = [o for t in tasks for o in t.operands]
    out_shape = [s for t in tasks for s in t.out_shape]
    sems = [s for t in tasks for s in t.sems]
    return operands, [hbm] * len(operands), out_shape, [hbm] * len(out_shape), sems


def _task_refs(tasks, ins, outs, sems):
    per_task = []
    for t in tasks:
        ni, no, ns = len(t.operands), len(t.out_shape), len(t.sems)
        per_task.append((ins[:ni], outs[:no], sems[:ns]))
        ins, outs, sems = ins[ni:], outs[no:], sems[ns:]
    return per_task


def _task_results(tasks, outs):
    res = []
    for t in tasks:
        res.append(list(outs[:len(t.out_shape)]))
        outs = outs[len(t.out_shape):]
    return res


def _carry(body, tasks, n_in, n_out, n_scratch, steps):
    if not tasks:
        return body
    t_in = sum(len(t.operands) for t in tasks)
    t_out = sum(len(t.out_shape) for t in tasks)

    def wrapped(*refs):
        ins, refs = refs[:n_in], refs[n_in:]
        t_ins, refs = refs[:t_in], refs[t_in:]
        outs, refs = refs[:n_out], refs[n_out:]
        t_outs, refs = refs[:t_out], refs[t_out:]
        scratch, t_sems = refs[:n_scratch], refs[n_scratch:]
        per_task = _task_refs(tasks, t_ins, t_outs, t_sems)
        step = pl.program_id(0)

        @pl.when(step == 0)
        def _():
            for t, r in zip(tasks, per_task):
                t.start(r)

        for t, r in zip(tasks, per_task):
            pl.when(step == max(steps - 1 - getattr(t, "lag", 0), 0))(functools.partial(t.middle, r))
            if hasattr(t, "late"):
                pl.when(step == steps - 1)(functools.partial(t.late, r))

        body(*ins, *outs, *scratch)

        @pl.when(step == steps - 1)
        def _():
            for t, r in zip(tasks, per_task):
                t.finish(r)

    return wrapped


def _exchange(name, tasks):
    operands, in_specs, out_shape, out_specs, sems = _task_args(tasks)

    def body(*refs):
        ni, no = len(operands), len(out_shape)
        per_task = _task_refs(tasks, refs[:ni], refs[ni:ni + no], refs[ni + no:])
        for phase in ("start", "middle", "late", "finish"):
            for t, r in zip(tasks, per_task):
                if hasattr(t, phase):
                    getattr(t, phase)(r)

    outs = pl.pallas_call(body, name=name, in_specs=in_specs, out_specs=out_specs, out_shape=out_shape,
                          scratch_shapes=sems)(*operands)
    return _task_results(tasks, outs)


def _row_tile(rows, cols, whole_up_to=256 * 1024):
    if rows * cols <= whole_up_to:
        return rows
    for t in (256, 176, 128, 64, 32, 16, 8):
        if rows % t == 0:
            return t
    return rows


def _pair_sum(name, partials, from_sibling):
    n = len(partials)
    _, rows, cols = partials[0].shape
    tile = _row_tile(rows, cols, 512 * 1024)

    def body(*refs):
        south = lax.axis_index("c") == 0
        for p_ref, s_ref, o_ref in zip(refs[:n], refs[n:2 * n], refs[2 * n:]):
            mine = jnp.where(south, p_ref[0, 0].astype(F32), p_ref[0, 1].astype(F32))
            o_ref[0] = (mine + s_ref[0].astype(F32)).astype(o_ref.dtype)

    blk = pl.BlockSpec((1, tile, cols), lambda q, i: (q, i, 0))
    return pl.pallas_call(
        body, name=name, grid=(4, rows // tile),
        in_specs=[pl.BlockSpec((1, 2, tile, cols), lambda q, i: (q, 0, i, 0))] * n + [blk] * n,
        out_specs=[blk] * n, out_shape=[_sds(s.shape, s.dtype) for s in from_sibling],
        compiler_params=_params("parallel", "parallel"),
    )(*[p.reshape(4, 2, rows, cols) for p in partials], *from_sibling)


def _sum_leading(name, stacked):
    parts, rows, cols = stacked.shape
    tile = _row_tile(rows, cols, (512 if parts <= 4 else 256) * 1024)

    def body(s_ref, o_ref):
        acc = s_ref[0].astype(F32)
        for d in range(1, parts):
            acc = acc + s_ref[d].astype(F32)
        o_ref[...] = acc

    return pl.pallas_call(
        body, name=name, grid=(rows // tile,),
        in_specs=[pl.BlockSpec((parts, tile, cols), lambda i: (0, i, 0))],
        out_specs=pl.BlockSpec((tile, cols), lambda i: (i, 0)),
        out_shape=_sds((rows, cols), F32),
        compiler_params=_params("parallel"),
    )(stacked)


def _adamw_math(w, g, m, v):
    nm = ADAM_B1 * m + (1.0 - ADAM_B1) * g
    nv = ADAM_B2 * v + (1.0 - ADAM_B2) * (g * g)
    m_hat = nm / (1.0 - ADAM_B1 ** ADAM_STEP)
    v_hat = nv / (1.0 - ADAM_B2 ** ADAM_STEP)
    return -ADAM_LR * (m_hat / (jnp.sqrt(v_hat) + ADAM_EPS) + ADAM_WD * w), nm, nv


def _adamw(name, w, g, m, v):
    rows, cols = w.shape
    tile = _row_tile(rows, cols)

    def body(w_ref, g_ref, m_ref, v_ref, d_ref, nm_ref, nv_ref):
        d_ref[...], nm_ref[...], nv_ref[...] = _adamw_math(w_ref[...], g_ref[...], m_ref[...], v_ref[...])

    blk = pl.BlockSpec((tile, cols), lambda i: (i, 0))
    return pl.pallas_call(
        body, name=name, grid=(rows // tile,),
        in_specs=[blk] * 4, out_specs=[blk] * 3,
        out_shape=[_sds((rows, cols), F32)] * 3,
        compiler_params=_params("parallel"),
    )(w, g, m, v)


def _sum_adamw(name, params, tasks=()):
    n = len(params)
    parts, rows, cols = params[0][0].shape
    tile = _row_tile(rows, cols)

    def body(*refs):
        ins, outs = refs[:4 * n], refs[4 * n:]
        for p in range(n):
            s_ref, w_ref, m_ref, v_ref = ins[4 * p:4 * p + 4]
            g_ref, d_ref, nm_ref, nv_ref = outs[4 * p:4 * p + 4]
            g = s_ref[0].astype(F32)
            for d in range(1, parts):
                g = g + s_ref[d].astype(F32)
            g_ref[...] = g
            d_ref[...], nm_ref[...], nv_ref[...] = _adamw_math(w_ref[...], g, m_ref[...], v_ref[...])

    blk = pl.BlockSpec((tile, cols), lambda i: (i, 0))
    outs, results = _grid_call(
        body, name, rows // tile,
        in_specs=([pl.BlockSpec((parts, tile, cols), lambda i: (0, i, 0))] + [blk] * 3) * n, out_specs=[blk] * (4 * n),
        out_shape=[_sds((rows, cols), F32)] * (4 * n),
        operands=[t for p in params for t in p], tasks=tasks)
    return [outs[4 * p:4 * p + 4] for p in range(n)], results


SMALL = ("b_gate", "w_pool", "pool_scale", "ln1_g", "ln1_b", "conv_b", "ln2_g", "ln2_b")
PACKED = SMALL + ("conv_w", "loss")
TILE = 8 * LANE


def _pack(parts):
    tiles = []
    for p in parts:
        flat = p.reshape(-1)
        tiles.append(jnp.pad(flat, (0, -flat.size % TILE)).reshape(-1, LANE))
    return jnp.concatenate(tiles, axis=0)


def _unpack(packed, shapes):
    out, at = [], 0
    for shape in shapes:
        size = math.prod(shape)
        rows = -(-size // TILE) * 8
        out.append(packed[at:at + rows].reshape(-1)[:size].reshape(shape))
        at += rows
    return out


MIXER = ("w_branch_attn", "w_branch_pool", "w_out", "conv_w")
FFN = ("w_ffn_gate_t", "w_ffn_up_t", "w_ffn_down")


def _columns(t):
    return jnp.transpose(t, (1, 0, 2)).reshape(t.shape[1], N_DEV * t.shape[2])


def _row_blocks(t):
    return t.reshape(N_DEV * t.shape[1], t.shape[2])


def _by_owner(t):
    return t.reshape(N_DEV, t.shape[0] // N_DEV, t.shape[1])


def _reduce_halves(names, partials, from_sibling):
    out = [None] * len(names)
    for shape in dict.fromkeys(p.shape for p in partials):
        group = [i for i, p in enumerate(partials) if p.shape == shape]
        sums = _pair_sum("pair_sum_" + names[group[0]], [partials[i] for i in group], [from_sibling[i] for i in group])
        for i, s in zip(group, sums):
            out[i] = s
    return out


def _local_step(x, target, shards, small):
    seq = x.shape[0]
    cos, sin = _rope_tables(seq)
    cast, ((w_in_all,),) = _to_bf16([shards[n] for n in MIXER[:3] + FFN], tasks=[_AllGather([shards["w_in"]])])
    shards = {**shards, **dict(zip(MIXER[:3] + FFN, cast))}
    (xt, q, k, v, u, g, kmean), (mixer,) = _proj_in(
        x, w_in_all, small["b_gate"], cos, sin, tasks=[_AllGather([shards[n] for n in MIXER], lag=2)])
    wba, wbp, wout, conv_w = _columns(mixer[0]), _columns(mixer[1]), _row_blocks(mixer[2]), _columns(mixer[3])
    (o, lse, bias), ((wgt, wut),) = _attn_fwd(
        q, k, v, kmean.reshape(seq // MOBA_BLOCK, D_ATTN),
        tasks=[_AllGather([shards["w_ffn_gate_t"], shards["w_ffn_up_t"]], lag=1)])
    (ya, yp, pooled, mixed, ypre, merged, xhat1, rstd1, h1, h1b), _ = _mix(
        o, u, g, x, wba, wbp, wout, small["w_pool"], small["pool_scale"], small["ln1_g"], small["ln1_b"])
    wgt, wut = _row_blocks(wgt), _row_blocks(wut)
    (a, uf, act), ((wd,),) = _ffn_up(
        h1b, wgt, wut, conv_w, small["conv_b"], tasks=[_AllGather([shards["w_ffn_down"]], lag=4)])
    wd = _row_blocks(wd)
    dr2, dr2b, loss, dg2, db2 = _ffn_down(act, wd, h1, target, small["ln2_g"], small["ln2_b"])

    da, du, dwd, dwg, dwu, dconv = _ffn_bwd(dr2b, h1b, a, uf, wd, conv_w, small["conv_b"])
    ffn_partials = [_by_owner(dwg), _by_owner(dwu), _by_owner(dwd)]
    (dr1, dr1b, dg1, db1), (ffn_sibling,) = _ln1_bwd(
        dr2, da, du, wgt, wut, xhat1, rstd1, small["ln1_g"], tasks=[_SiblingSend(ffn_partials)])
    ffn_chip = _reduce_halves(FFN, ffn_partials, ffn_sibling)
    (dzg, dya, dyp, do, dmixed, dpooled, dbg, dps), (gate_landed,) = _mix_bwd(
        dr1b, ya, yp, g, mixed, wout, wba, wbp, small["w_pool"], small["pool_scale"],
        tasks=[_ChipScatter(ffn_chip[0:1])])
    (dw_ba, dw_bp, dw_out, dw_pool), _ = _dw_mixers(o, ypre, merged, dya, dyp, dr1b, pooled, dmixed)
    mixer_partials = [dw_ba, dw_bp, dw_out]
    (dq, dk, dv), (up_down_landed, mixer_sibling) = _attn_bwd(
        q, k, v, bias, o, lse, do, cos, sin, tasks=[_ChipScatter(ffn_chip[1:3]), _SiblingSend(mixer_partials)])
    mixer_chip = _reduce_halves(MIXER[:3], mixer_partials, mixer_sibling)
    (grad_x, dz), _ = _in_bwd(dq, dk, dv, dpooled, dzg, dr1, w_in_all)
    little = {"b_gate": dbg, "w_pool": dw_pool, "pool_scale": dps, "ln1_g": dg1, "ln1_b": db1, "conv_b": dconv[3:4],
              "ln2_g": dg2, "ln2_b": db2, "conv_w": dconv[0:3], "loss": loss}
    dw_in, (mixer_landed,) = _matmul(
        "dw_in", xt, dz, (N_DEV, D_MODEL, D_ATTN), BF16, N_DEV,
        _full(xt.shape), pl.BlockSpec((seq, D_ATTN), lambda n: (0, n)),
        pl.BlockSpec((1, D_MODEL, D_ATTN), lambda n: (n, 0, 0)), tasks=[_ChipScatter(mixer_chip)])

    landed = dict(zip(FFN + MIXER[:3], gate_landed + up_down_landed + mixer_landed))
    return grad_x, landed, dw_in, _pack([little[n] for n in PACKED])


def kernel(x, w_in, b_gate, w_branch_attn, w_pool, pool_scale, w_branch_pool, w_out, ln1_g, ln1_b, w_ffn_gate, w_ffn_up, conv_w, conv_b, w_ffn_down, ln2_g, ln2_b, loss_target, m_w_in, m_b_gate, m_w_branch_attn, m_w_pool, m_pool_scale, m_w_branch_pool, m_w_out, m_ln1_g, m_ln1_b, m_w_ffn_gate, m_w_ffn_up, m_conv_w, m_conv_b, m_w_ffn_down, m_ln2_g, m_ln2_b, v_w_in, v_b_gate, v_w_branch_attn, v_w_pool, v_pool_scale, v_w_branch_pool, v_w_out, v_ln1_g, v_ln1_b, v_w_ffn_gate, v_w_ffn_up, v_conv_w, v_conv_b, v_w_ffn_down, v_ln2_g, v_ln2_b):
    me = 4 * lax.axis_index("x") + 2 * lax.axis_index("y") + lax.axis_index("c")
    weights = dict(w_in=w_in, b_gate=b_gate, w_branch_attn=w_branch_attn, w_pool=w_pool, pool_scale=pool_scale,
                   w_branch_pool=w_branch_pool, w_out=w_out, ln1_g=ln1_g, ln1_b=ln1_b, w_ffn_gate=w_ffn_gate,
                   w_ffn_up=w_ffn_up, conv_w=conv_w, conv_b=conv_b, w_ffn_down=w_ffn_down, ln2_g=ln2_g, ln2_b=ln2_b)
    m_in = dict(w_in=m_w_in, b_gate=m_b_gate, w_branch_attn=m_w_branch_attn, w_pool=m_w_pool,
                pool_scale=m_pool_scale, w_branch_pool=m_w_branch_pool, w_out=m_w_out, ln1_g=m_ln1_g, ln1_b=m_ln1_b,
                w_ffn_gate=m_w_ffn_gate, w_ffn_up=m_w_ffn_up, conv_w=m_conv_w, conv_b=m_conv_b,
                w_ffn_down=m_w_ffn_down, ln2_g=m_ln2_g, ln2_b=m_ln2_b)
    v_in = dict(w_in=v_w_in, b_gate=v_b_gate, w_branch_attn=v_w_branch_attn, w_pool=v_w_pool,
                pool_scale=v_pool_scale, w_branch_pool=v_w_branch_pool, w_out=v_w_out, ln1_g=v_ln1_g, ln1_b=v_ln1_b,
                w_ffn_gate=v_w_ffn_gate, w_ffn_up=v_w_ffn_up, conv_w=v_conv_w, conv_b=v_conv_b,
                w_ffn_down=v_w_ffn_down, ln2_g=v_ln2_g, ln2_b=v_ln2_b)
    weights = {n: a[0] for n, a in weights.items()}
    m_in = {n: a[0] for n, a in m_in.items()}
    v_in = {n: a[0] for n, a in v_in.items()}

    shards = {"w_in": weights["w_in"].astype(BF16), "w_branch_attn": weights["w_branch_attn"],
              "w_branch_pool": weights["w_branch_pool"], "w_out": weights["w_out"],
              "w_ffn_gate_t": weights["w_ffn_gate"].T, "w_ffn_up_t": weights["w_ffn_up"].T,
              "w_ffn_down": weights["w_ffn_down"], "conv_w": weights["conv_w"]}
    small = {"b_gate": weights["b_gate"][None], "w_pool": weights["w_pool"], "pool_scale": weights["pool_scale"][None],
             "ln1_g": weights["ln1_g"][None], "ln1_b": weights["ln1_b"][None], "conv_b": weights["conv_b"][None],
             "ln2_g": weights["ln2_g"][None], "ln2_b": weights["ln2_b"][None]}

    grad_x, landed, dw_in, packed = _local_step(x[0], loss_target[0], shards, small)

    grads, delta, new_m, new_v = {}, {}, {}, {}

    def param(n, transposed=False):
        if transposed:
            return landed[n + "_t"], weights[n].T, m_in[n].T, v_in[n].T
        return landed[n], weights[n], m_in[n], v_in[n]

    def keep(n, updated, transposed=False):
        grads[n], delta[n], new_m[n], new_v[n] = (t.T for t in updated) if transposed else updated

    ((w_in_sibling,),) = _exchange("sibling_grads", [_SiblingSend([dw_in])])
    w_in_chip = _reduce_halves(["w_in"], [dw_in], [w_in_sibling])
    (landed["w_in"],), (all_small,) = _exchange("scatter_grads", [_ChipScatter(w_in_chip), _AllGather([packed])])
    (gate, up, down), _ = _sum_adamw(
        "update_w_ffn", [param("w_ffn_gate", True), param("w_ffn_up", True), param("w_ffn_down")])
    keep("w_ffn_gate", gate, True)
    keep("w_ffn_up", up, True)
    keep("w_ffn_down", down)
    (attn, pool), _ = _sum_adamw("update_w_branch", [param("w_branch_attn"), param("w_branch_pool")])
    keep("w_branch_attn", attn)
    keep("w_branch_pool", pool)
    for n in ("w_out", "w_in"):
        (updated,), _ = _sum_adamw("update_" + n, [param(n)])
        keep(n, updated)
    names = SMALL + ("conv_w",)
    small_sum = _sum_leading("sum_small", all_small)
    *small_grads, conv_w_grad, loss = _unpack(small_sum, [weights[n].shape for n in SMALL] + [(3, D_FF), (8, LANE)])
    loss = loss[0, 0]
    grads.update(zip(SMALL, small_grads))
    grads["conv_w"] = lax.dynamic_slice(conv_w_grad, (0, me * FF_SHARD), (3, FF_SHARD))
    flat = lambda d: _pack([d[n] for n in names])
    shapes = [weights[n].shape for n in names]
    for out, packed in zip((delta, new_m, new_v),
                           _adamw("adamw_small", flat(weights), flat(grads), flat(m_in), flat(v_in))):
        out.update(zip(names, _unpack(packed, shapes)))

    order = ("w_in", "b_gate", "w_branch_attn", "w_pool", "pool_scale", "w_branch_pool", "w_out", "ln1_g", "ln1_b",
             "w_ffn_gate", "w_ffn_up", "conv_w", "conv_b", "w_ffn_down", "ln2_g", "ln2_b")
    lead = lambda t: t[None]
    return (loss, lead(grad_x), *[lead(grads[n]) for n in order], *[lead(delta[n]) for n in order],
            *[lead(new_m[n]) for n in order], *[lead(new_v[n]) for n in order])
```

```python
import functools
import math

import jax
import jax.numpy as jnp
from jax import lax
from jax.experimental import pallas as pl
from jax.experimental.pallas import tpu as pltpu

F32 = jnp.float32
BF16 = jnp.bfloat16

D_MODEL = 1024
N_HEADS = 8
HEAD_DIM = 64
D_ATTN = N_HEADS * HEAD_DIM
MOBA_BLOCK = 256
MOBA_TOPK = 3
ROPE_THETA = 10000.0
POOL_WINDOWS = (2, 4, 8, 16)
POOL_GROUP = 128
D_POOL = len(POOL_WINDOWS) * POOL_GROUP
POOL_HALO = 16
D_FF = 2816
D_IN_PROJ = 3 * D_ATTN + D_POOL + 2 * D_MODEL
LN_EPS = 1e-5
ALPHA = 2.0 ** 0.25
NEG = -1e30
N_DEV = 8
FF_SHARD = D_FF // N_DEV

ADAM_LR = 0.001
ADAM_B1 = 0.9
ADAM_B2 = 0.999
ADAM_EPS = 1e-08
ADAM_WD = 0.01
ADAM_STEP = 10

TOK = 256
FF_CHUNK = 256
LANE = 128
VMEM_LIMIT = 56 * 1024 * 1024

MESH = pl.DeviceIdType.MESH
NT_DIMS = (((1,), (1,)), ((), ()))
TN_DIMS = (((0,), (0,)), ((), ()))


def _params(*sem):
    return pltpu.CompilerParams(dimension_semantics=sem or None, vmem_limit_bytes=VMEM_LIMIT)


def _full(shape):
    zeros = (0,) * len(shape)
    return pl.BlockSpec(shape, lambda *_: zeros, pipeline_mode=pl.Buffered(1))


def _rows(width, tile=TOK):
    return pl.BlockSpec((tile, width), lambda i: (i, 0))


def _sds(shape, dtype):
    return jax.ShapeDtypeStruct(shape, dtype)


def _dot(a, b):
    return jnp.dot(a, b, preferred_element_type=F32)


def _dot_nt(a, b):
    return lax.dot_general(a, b, NT_DIMS, preferred_element_type=F32)


def _dot_tn(a, b):
    return lax.dot_general(a, b, TN_DIMS, preferred_element_type=F32)


def _rope_tables(seq):
    half = HEAD_DIM // 2
    inv_freq = 1.0 / (ROPE_THETA ** (jnp.arange(half, dtype=F32) / half))
    ang = jnp.arange(seq, dtype=F32)[:, None] * inv_freq[None, :]
    cos, sin = jnp.cos(ang), jnp.sin(ang)
    return jnp.tile(cos, (1, 4)), jnp.tile(jnp.concatenate([-sin, sin], axis=1), (1, 2))


def _swap_halves(t):
    lane = lax.broadcasted_iota(jnp.int32, t.shape, 1)
    return jnp.where((lane % HEAD_DIM) < HEAD_DIM // 2, pltpu.roll(t, LANE - 32, 1), pltpu.roll(t, 32, 1))


def _rope(t, cos, sin):
    return t * cos + _swap_halves(t) * sin


def _rope_transposed(g, cos, sin):
    return g * cos + _swap_halves(g * sin)


def _ln_fwd(r, g, b):
    mu = jnp.mean(r, axis=-1, keepdims=True)
    xc = r - mu
    var = jnp.mean(xc * xc, axis=-1, keepdims=True)
    rstd = lax.rsqrt(var + LN_EPS)
    xhat = xc * rstd
    return xhat * g + b, xhat, rstd


def _ln_bwd(dy, xhat, rstd, g):
    dxh = dy * g
    m1 = jnp.mean(dxh, axis=-1, keepdims=True)
    m2 = jnp.mean(dxh * xhat, axis=-1, keepdims=True)
    return rstd * (dxh - m1 - xhat * m2)


def _normal_cdf(a):
    return 0.5 * (1.0 + lax.erf(a * (1.0 / math.sqrt(2.0))))


def _gelu_derivative(a, cdf):
    return cdf + a * (jnp.exp(-0.5 * a * a) * (1.0 / math.sqrt(2.0 * math.pi)))


def _shift_down(a, k):
    row = lax.broadcasted_iota(jnp.int32, a.shape, 0)
    return jnp.where(row >= k, pltpu.roll(a, k, 0), 0.0)


def _shift_up(a, k):
    n = a.shape[0]
    row = lax.broadcasted_iota(jnp.int32, a.shape, 0)
    return jnp.where(row < n - k, pltpu.roll(a, n - k, 0), 0.0)


def _conv(a, cw, cb):
    return cw[2:3, :] * a + cw[1:2, :] * _shift_down(a, 1) + cw[0:1, :] * _shift_down(a, 2) + cb


def _pool_count(first_row, rows, window):
    t = first_row + lax.broadcasted_iota(jnp.int32, (rows, 1), 0)
    return jnp.minimum(t + 1, window).astype(F32)


def _grid_call(body, name, steps, in_specs, out_specs, out_shape, operands, scratch=(), tasks=()):
    t_operands, t_in_specs, t_out_shape, t_out_specs, t_sems = _task_args(tasks)
    outs = pl.pallas_call(
        _carry(body, tasks, len(in_specs), len(out_specs), len(scratch), steps), name=name, grid=(steps,),
        in_specs=list(in_specs) + t_in_specs, out_specs=list(out_specs) + t_out_specs,
        out_shape=list(out_shape) + t_out_shape, scratch_shapes=list(scratch) + t_sems,
        compiler_params=_params("arbitrary"),
    )(*operands, *t_operands)
    return outs[:len(out_specs)], _task_results(tasks, outs[len(out_specs):])


def _proj_in(x, win, b_gate, cos, sin, tasks=()):
    seq = x.shape[0]
    nt = seq // TOK

    def body(x_ref, win_ref, bg_ref, cos_ref, sin_ref, xt_ref, q_ref, k_ref, v_ref, u_ref, g_ref, km_ref):
        xb = x_ref[...].astype(BF16)
        xt_ref[...] = x_ref[...].T.astype(BF16)
        cos_t, sin_t = cos_ref[...], sin_ref[...]
        for sec, out_ref in ((0, q_ref), (1, k_ref)):
            z = _dot(xb, win_ref[sec])
            for c in range(D_ATTN // LANE):
                cols = slice(LANE * c, LANE * (c + 1))
                out_ref[:, cols] = _rope(z[:, cols], cos_t, sin_t)
        for b in range(TOK // MOBA_BLOCK):
            km_ref[b] = jnp.mean(k_ref[MOBA_BLOCK * b:MOBA_BLOCK * (b + 1), :], axis=0, keepdims=True)
        v_ref[...] = _dot(xb, win_ref[2]).astype(BF16)
        u_ref[...] = _dot(xb, win_ref[3])
        for n in range(4):
            cols = slice(D_ATTN * n, D_ATTN * (n + 1))
            g_ref[:, cols] = jax.nn.sigmoid(_dot(xb, win_ref[4 + n]) + bg_ref[:, cols])

    return _grid_call(
        body, "proj_in", nt,
        in_specs=[_rows(D_MODEL), _full(win.shape), _full((1, 2 * D_MODEL)), _rows(LANE), _rows(LANE)],
        out_specs=[pl.BlockSpec((D_MODEL, TOK), lambda i: (0, i)), _rows(D_ATTN), _rows(D_ATTN), _rows(D_ATTN),
                   _rows(D_POOL), _rows(2 * D_MODEL),
                   pl.BlockSpec((TOK // MOBA_BLOCK, 1, D_ATTN), lambda i: (i, 0, 0))],
        out_shape=[_sds((D_MODEL, seq), BF16), _sds((seq, D_ATTN), F32), _sds((seq, D_ATTN), F32),
                   _sds((seq, D_ATTN), BF16), _sds((seq, D_POOL), F32), _sds((seq, 2 * D_MODEL), F32),
                   _sds((seq // MOBA_BLOCK, 1, D_ATTN), F32)],
        operands=(x, win, b_gate, cos, sin), tasks=tasks)


SCORE_CHUNK = 128


def _store_keys(ka_sc, k_ref, ls):
    seq = ka_sc.shape[0]
    ka_sc[:, 0:HEAD_DIM] = k_ref[:, ls].astype(BF16)
    row = lax.broadcasted_iota(jnp.int32, (seq, HEAD_DIM), 0)
    lane = lax.broadcasted_iota(jnp.int32, (seq, HEAD_DIM), 1)
    in_block = (lane * MOBA_BLOCK <= row) & (row < (lane + 1) * MOBA_BLOCK)
    ka_sc[:, HEAD_DIM:] = jnp.where(in_block, 1.0, 0.0).astype(BF16)


def _block_bias(qf, km, i):
    if i <= MOBA_TOPK:
        return jnp.zeros((MOBA_BLOCK, HEAD_DIM), BF16)
    nb = km.shape[0]
    gate = lax.dot_general(km, qf, NT_DIMS, precision=lax.Precision.HIGHEST, preferred_element_type=F32)
    blk = lax.broadcasted_iota(jnp.int32, gate.shape, 0)
    rank = jnp.zeros(gate.shape, F32)
    for r in range(1, i):
        lower = pltpu.roll(gate, r, 0)
        rank = rank + jnp.where((blk >= r) & (lower >= gate), 1.0, 0.0)
        higher = pltpu.roll(gate, nb - r, 0)
        rank = rank + jnp.where((blk + r < i) & (higher > gate), 1.0, 0.0)
    bias = jnp.where((blk < i) & (rank >= MOBA_TOPK), NEG, 0.0)
    padded = jnp.concatenate([bias, jnp.zeros((LANE - nb, MOBA_BLOCK), F32)], axis=0)
    return jnp.transpose(padded)[:, 0:HEAD_DIM].astype(BF16)


def _causal(shape, transposed=False):
    row = lax.broadcasted_iota(jnp.int32, shape, 0)
    col = lax.broadcasted_iota(jnp.int32, shape, 1)
    return (row <= col) if transposed else (col <= row)


def _row_vector(col):
    return jnp.transpose(jnp.broadcast_to(col, (MOBA_BLOCK, LANE)))[0:1, :]


def _attn_fwd(q, k, v, kmean, tasks=()):
    seq = q.shape[0]
    nb = seq // MOBA_BLOCK
    assert nb == 8, "the block ranking keeps one sublane per key block"
    pair = pl.BlockSpec((seq, LANE), lambda p: (0, p))
    heads = LANE // HEAD_DIM

    def body(q_ref, k_ref, v_ref, km_ref, o_ref, lse_ref, bias_ref, ka_sc, qa_sc, s_sc, p_sc):
        lse_ref[0, heads:, :] = jnp.zeros((8 - heads, seq), F32)
        for hh in range(heads):
            ls = slice(HEAD_DIM * hh, HEAD_DIM * (hh + 1))
            _store_keys(ka_sc, k_ref, ls)
            vb = v_ref[:, ls]
            km = km_ref[:, ls]
            for i in range(nb):
                rs = slice(MOBA_BLOCK * i, MOBA_BLOCK * (i + 1))
                width = MOBA_BLOCK * (i + 1)
                qf = q_ref[rs, ls]
                bias = _block_bias(qf, km, i)
                bias_ref[rs, ls] = bias
                qa_sc[:, 0:HEAD_DIM] = (qf * HEAD_DIM ** -0.5).astype(BF16)
                qa_sc[:, HEAD_DIM:] = bias
                s_sc[:, 0:width] = _dot_nt(qa_sc[...], ka_sc[0:width, :])
                s_sc[:, rs] = jnp.where(_causal((MOBA_BLOCK, MOBA_BLOCK)), s_sc[:, rs], NEG)
                chunks = [slice(SCORE_CHUNK * c, SCORE_CHUNK * (c + 1)) for c in range(width // SCORE_CHUNK)]
                top = s_sc[:, chunks[0]]
                for c in chunks[1:]:
                    top = jnp.maximum(top, s_sc[:, c])
                m = jnp.max(top, axis=1, keepdims=True)
                total = jnp.zeros((MOBA_BLOCK, SCORE_CHUNK), F32)
                for c in chunks:
                    p = jnp.exp(s_sc[:, c] - m)
                    total = total + p
                    p_sc[:, c] = p.astype(BF16)
                l = jnp.sum(total, axis=1, keepdims=True)
                o_ref[rs, ls] = _dot(p_sc[:, 0:width], vb[0:width]) / l
                lse_ref[0, hh:hh + 1, rs] = _row_vector(m + jnp.log(l))

    return _grid_call(
        body, "attn_fwd", D_ATTN // LANE,
        in_specs=[pair, pair, pair, pl.BlockSpec((nb, LANE), lambda p: (0, p))],
        out_specs=[pair, pl.BlockSpec((1, 8, seq), lambda p: (p, 0, 0)), pair],
        out_shape=[_sds((seq, D_ATTN), F32), _sds((D_ATTN // LANE, 8, seq), F32), _sds((seq, D_ATTN), BF16)],
        operands=(q, k, v, kmean),
        scratch=[pltpu.VMEM((seq, LANE), BF16), pltpu.VMEM((MOBA_BLOCK, LANE), BF16),
                 pltpu.VMEM((MOBA_BLOCK, seq), F32), pltpu.VMEM((MOBA_BLOCK, seq), BF16)],
        tasks=tasks)


def _mix(o, u, g, x, wba, wbp, wout, w_pool, pool_scale, ln_g, ln_b, tasks=()):
    seq = x.shape[0]

    def body(o_ref, u_ref, uprev_ref, g_ref, x_ref, wba_ref, wbp_ref, wout_ref, wp_ref, ps_ref, lg_ref, lb_ref,
             ya_ref, yp_ref, pooled_ref, mixed_ref, ypre_ref, merged_ref, xhat_ref, rstd_ref, h_ref, hb_ref, ext):
        i = pl.program_id(0)
        ya = _dot(o_ref[...].astype(BF16), wba_ref[...])
        ucur = u_ref[...]
        ext[0:POOL_HALO, :] = jnp.where(i > 0, uprev_ref[...], 0.0)
        ext[POOL_HALO:, :] = ucur
        for grp, window in enumerate(POOL_WINDOWS):
            cols = slice(POOL_GROUP * grp, POOL_GROUP * (grp + 1))
            acc = ucur[:, cols]
            for kk in range(1, window):
                acc = acc + ext[pl.ds(POOL_HALO - kk, TOK), cols]
            pooled = acc / _pool_count(i * TOK, TOK, window) - ucur[:, cols]
            pooled_ref[:, cols] = pooled.astype(BF16)
            mixed_ref[:, cols] = _dot(pooled.astype(BF16), wp_ref[grp].astype(BF16))
        mixed = mixed_ref[...]
        ypre = (mixed * ps_ref[...]).astype(BF16)
        ypre_ref[...] = ypre
        yp = _dot(ypre, wbp_ref[...])
        ya_ref[...] = ya
        yp_ref[...] = yp
        merged = (g_ref[:, :D_MODEL] * ya + g_ref[:, D_MODEL:] * yp).astype(BF16)
        merged_ref[...] = merged
        r1 = ALPHA * x_ref[...] + _dot(merged, wout_ref[...])
        h, xhat, rstd = _ln_fwd(r1, lg_ref[...], lb_ref[...])
        xhat_ref[...] = xhat
        rstd_ref[...] = jnp.broadcast_to(rstd, (TOK, LANE))
        h_ref[...] = h
        hb_ref[...] = h.astype(BF16)

    halo = pl.BlockSpec((POOL_HALO, D_POOL), lambda i: (jnp.maximum(i * (TOK // POOL_HALO) - 1, 0), 0))
    return _grid_call(
        body, "mix", seq // TOK,
        in_specs=[_rows(D_ATTN), _rows(D_POOL), halo, _rows(2 * D_MODEL), _rows(D_MODEL),
                  _full(wba.shape), _full(wbp.shape), _full(wout.shape), _full(w_pool.shape),
                  _full((1, D_POOL)), _full((1, D_MODEL)), _full((1, D_MODEL))],
        out_specs=[_rows(D_MODEL), _rows(D_MODEL), _rows(D_POOL), _rows(D_POOL), _rows(D_POOL), _rows(D_MODEL),
                   _rows(D_MODEL), _rows(LANE), _rows(D_MODEL), _rows(D_MODEL)],
        out_shape=[_sds((seq, D_MODEL), F32), _sds((seq, D_MODEL), F32), _sds((seq, D_POOL), BF16),
                   _sds((seq, D_POOL), F32), _sds((seq, D_POOL), BF16), _sds((seq, D_MODEL), BF16),
                   _sds((seq, D_MODEL), F32), _sds((seq, LANE), F32), _sds((seq, D_MODEL), F32),
                   _sds((seq, D_MODEL), BF16)],
        operands=(o, u, u, g, x, wba, wbp, wout, w_pool, pool_scale, ln_g, ln_b),
        scratch=[pltpu.VMEM((TOK + POOL_HALO, D_POOL), F32)], tasks=tasks)


def _ffn_up(hb, wgt, wut, conv_w, conv_b, tasks=()):
    seq = hb.shape[0]
    wblk = pl.BlockSpec((FF_CHUNK, D_MODEL), lambda c: (c, 0))
    cblk = lambda rows: pl.BlockSpec((rows, FF_CHUNK), lambda c: (0, c))
    oblk = pl.BlockSpec((seq, FF_CHUNK), lambda c: (0, c))

    def body(h_ref, wg_ref, wu_ref, cw_ref, cb_ref, a_ref, u_ref, act_ref):
        h = h_ref[...]
        a = _dot_nt(h, wg_ref[...])
        u = _dot_nt(h, wu_ref[...])
        a_ref[...] = a
        u_ref[...] = u
        ac = _conv(a, cw_ref[...], cb_ref[...])
        act_ref[...] = (ac * _normal_cdf(ac) * u).astype(BF16)

    return _grid_call(
        body, "ffn_up", D_FF // FF_CHUNK,
        in_specs=[_full(hb.shape), wblk, wblk, cblk(3), cblk(1)],
        out_specs=[oblk, oblk, oblk],
        out_shape=[_sds((seq, D_FF), F32), _sds((seq, D_FF), F32), _sds((seq, D_FF), BF16)],
        operands=(hb, wgt, wut, conv_w, conv_b), tasks=tasks)


def _ffn_down(act, wd, h, target, ln_g, ln_b):
    seq = h.shape[0]

    def body(act_ref, wd_ref, h_ref, t_ref, lg_ref, lb_ref, dr_ref, drb_ref, loss_ref, dg_ref, db_ref):
        i = pl.program_id(0)

        @pl.when(i == 0)
        def _():
            loss_ref[...] = jnp.zeros_like(loss_ref)
            dg_ref[...] = jnp.zeros_like(dg_ref)
            db_ref[...] = jnp.zeros_like(db_ref)

        r2 = ALPHA * h_ref[...] + _dot(act_ref[...], wd_ref[...])
        y, xhat, rstd = _ln_fwd(r2, lg_ref[...], lb_ref[...])
        diff = y - t_ref[...]
        loss_ref[...] += jnp.sum(diff * diff) * (0.5 / D_MODEL)
        dy = diff * (1.0 / D_MODEL)
        dg_ref[...] += jnp.sum(dy * xhat, axis=0, keepdims=True)
        db_ref[...] += jnp.sum(dy, axis=0, keepdims=True)
        dr = _ln_bwd(dy, xhat, rstd, lg_ref[...])
        dr_ref[...] = dr
        drb_ref[...] = dr.astype(BF16)

    vec = pl.BlockSpec((1, D_MODEL), lambda i: (0, 0))
    return pl.pallas_call(
        body, name="ffn_down", grid=(seq // TOK,),
        in_specs=[_rows(D_FF), _full(wd.shape), _rows(D_MODEL), _rows(D_MODEL), _full((1, D_MODEL)), _full((1, D_MODEL))],
        out_specs=[_rows(D_MODEL), _rows(D_MODEL), pl.BlockSpec((8, LANE), lambda i: (0, 0)), vec, vec],
        out_shape=[_sds((seq, D_MODEL), F32), _sds((seq, D_MODEL), BF16), _sds((8, LANE), F32),
                   _sds((1, D_MODEL), F32), _sds((1, D_MODEL), F32)],
        compiler_params=_params("arbitrary"),
    )(act, wd, h, target, ln_g, ln_b)


def _ffn_bwd(drb, hb, a, u, wd, conv_w, conv_b):
    seq = hb.shape[0]
    wblk = pl.BlockSpec((FF_CHUNK, D_MODEL), lambda c: (c, 0))
    cblk = lambda rows: pl.BlockSpec((rows, FF_CHUNK), lambda c: (0, c))
    sblk = pl.BlockSpec((seq, FF_CHUNK), lambda c: (0, c))

    def body(dr_ref, h_ref, a_ref, u_ref, wd_ref, cw_ref, cb_ref, da_ref, du_ref, dwd_ref, dwg_ref, dwu_ref, dc_ref):
        dr = dr_ref[...]
        h = h_ref[...]
        a = a_ref[...]
        u = u_ref[...]
        cw = cw_ref[...]
        dact = _dot_nt(dr, wd_ref[...])
        ac = _conv(a, cw, cb_ref[...])
        cdf = _normal_cdf(ac)
        gelu = ac * cdf
        dwd_ref[...] = _dot_tn((gelu * u).astype(BF16), dr).astype(BF16)
        du = (dact * gelu).astype(BF16)
        dac = dact * u * _gelu_derivative(ac, cdf)
        da = (cw[2:3, :] * dac + cw[1:2, :] * _shift_up(dac, 1) + cw[0:1, :] * _shift_up(dac, 2)).astype(BF16)
        da_ref[...] = da
        du_ref[...] = du
        dwg_ref[...] = _dot_tn(da, h).astype(BF16)
        dwu_ref[...] = _dot_tn(du, h).astype(BF16)
        dc_ref[0:1, :] = jnp.sum(dac * _shift_down(a, 2), axis=0, keepdims=True)
        dc_ref[1:2, :] = jnp.sum(dac * _shift_down(a, 1), axis=0, keepdims=True)
        dc_ref[2:3, :] = jnp.sum(dac * a, axis=0, keepdims=True)
        dc_ref[3:4, :] = jnp.sum(dac, axis=0, keepdims=True)
        dc_ref[4:8, :] = jnp.zeros((4, FF_CHUNK), F32)

    return pl.pallas_call(
        body, name="ffn_bwd", grid=(D_FF // FF_CHUNK,),
        in_specs=[_full(drb.shape), _full(hb.shape), sblk, sblk, wblk, cblk(3), cblk(1)],
        out_specs=[sblk, sblk, wblk, wblk, wblk, cblk(8)],
        out_shape=[_sds((seq, D_FF), BF16), _sds((seq, D_FF), BF16), _sds((D_FF, D_MODEL), BF16),
                   _sds((D_FF, D_MODEL), BF16), _sds((D_FF, D_MODEL), BF16), _sds((8, D_FF), F32)],
        compiler_params=_params("parallel"),
    )(drb, hb, a, u, wd, conv_w, conv_b)


def _ln1_bwd(dr2, da, du, wgt, wut, xhat, rstd, ln_g, tasks=()):
    seq = dr2.shape[0]

    def body(dr2_ref, da_ref, du_ref, wg_ref, wu_ref, xhat_ref, rstd_ref, lg_ref, dr_ref, drb_ref, dg_ref, db_ref):
        @pl.when(pl.program_id(0) == 0)
        def _():
            dg_ref[...] = jnp.zeros_like(dg_ref)
            db_ref[...] = jnp.zeros_like(db_ref)

        dh = ALPHA * dr2_ref[...] + _dot(da_ref[...], wg_ref[...]) + _dot(du_ref[...], wu_ref[...])
        xhat = xhat_ref[...]
        dg_ref[...] += jnp.sum(dh * xhat, axis=0, keepdims=True)
        db_ref[...] += jnp.sum(dh, axis=0, keepdims=True)
        dr = _ln_bwd(dh, xhat, rstd_ref[:, 0:1], lg_ref[...])
        dr_ref[...] = dr
        drb_ref[...] = dr.astype(BF16)

    vec = pl.BlockSpec((1, D_MODEL), lambda i: (0, 0))
    return _grid_call(
        body, "ln1_bwd", seq // TOK,
        in_specs=[_rows(D_MODEL), _rows(D_FF), _rows(D_FF), _full(wgt.shape), _full(wut.shape), _rows(D_MODEL),
                  _rows(LANE), _full((1, D_MODEL))],
        out_specs=[_rows(D_MODEL), _rows(D_MODEL), vec, vec],
        out_shape=[_sds((seq, D_MODEL), F32), _sds((seq, D_MODEL), BF16), _sds((1, D_MODEL), F32),
                   _sds((1, D_MODEL), F32)],
        operands=(dr2, da, du, wgt, wut, xhat, rstd, ln_g), tasks=tasks)


def _mix_bwd(drb, ya, yp, g, mixed, wout, wba, wbp, w_pool, pool_scale, tasks=()):
    seq = drb.shape[0]

    def body(dr_ref, ya_ref, yp_ref, g_ref, mixed_ref, wout_ref, wba_ref, wbp_ref, wp_ref, ps_ref,
             dzg_ref, dya_ref, dyp_ref, do_ref, dmixed_ref, dpooled_ref, dbg_ref, dps_ref):
        @pl.when(pl.program_id(0) == 0)
        def _():
            dbg_ref[...] = jnp.zeros_like(dbg_ref)
            dps_ref[...] = jnp.zeros_like(dps_ref)

        dmerged = _dot_nt(dr_ref[...], wout_ref[...])
        ga, gp = g_ref[:, :D_MODEL], g_ref[:, D_MODEL:]
        dzga = dmerged * ya_ref[...] * ga * (1.0 - ga)
        dzgp = dmerged * yp_ref[...] * gp * (1.0 - gp)
        dzg_ref[:, :D_MODEL] = dzga.astype(BF16)
        dzg_ref[:, D_MODEL:] = dzgp.astype(BF16)
        dbg_ref[:, :D_MODEL] += jnp.sum(dzga, axis=0, keepdims=True)
        dbg_ref[:, D_MODEL:] += jnp.sum(dzgp, axis=0, keepdims=True)
        dya = (dmerged * ga).astype(BF16)
        dyp = (dmerged * gp).astype(BF16)
        dya_ref[...] = dya
        dyp_ref[...] = dyp
        do_ref[...] = _dot_nt(dya, wba_ref[...])
        dypre = _dot_nt(dyp, wbp_ref[...])
        dps_ref[...] += jnp.sum(dypre * mixed_ref[...], axis=0, keepdims=True)
        dmixed = (dypre * ps_ref[...]).astype(BF16)
        dmixed_ref[...] = dmixed
        for grp in range(len(POOL_WINDOWS)):
            cols = slice(POOL_GROUP * grp, POOL_GROUP * (grp + 1))
            dpooled_ref[:, cols] = _dot_nt(dmixed[:, cols], wp_ref[grp].astype(BF16))

    return _grid_call(
        body, "mix_bwd", seq // TOK,
        in_specs=[_rows(D_MODEL), _rows(D_MODEL), _rows(D_MODEL), _rows(2 * D_MODEL), _rows(D_POOL),
                  _full(wout.shape), _full(wba.shape), _full(wbp.shape), _full(w_pool.shape), _full((1, D_POOL))],
        out_specs=[_rows(2 * D_MODEL), _rows(D_MODEL), _rows(D_MODEL), _rows(D_ATTN), _rows(D_POOL), _rows(D_POOL),
                   pl.BlockSpec((1, 2 * D_MODEL), lambda i: (0, 0)), pl.BlockSpec((1, D_POOL), lambda i: (0, 0))],
        out_shape=[_sds((seq, 2 * D_MODEL), BF16), _sds((seq, D_MODEL), BF16), _sds((seq, D_MODEL), BF16),
                   _sds((seq, D_ATTN), F32), _sds((seq, D_POOL), BF16), _sds((seq, D_POOL), F32),
                   _sds((1, 2 * D_MODEL), F32), _sds((1, D_POOL), F32)],
        operands=(drb, ya, yp, g, mixed, wout, wba, wbp, w_pool, pool_scale), tasks=tasks)


def _attn_bwd(q, k, v, bias, o, lse, do, cos, sin, tasks=()):
    seq = q.shape[0]
    nb = seq // MOBA_BLOCK
    pair = pl.BlockSpec((seq, LANE), lambda p: (0, p))
    table = pl.BlockSpec((seq, LANE), lambda p: (0, 0))
    scale = HEAD_DIM ** -0.5

    def body(q_ref, k_ref, v_ref, bias_ref, o_ref, lse_ref, do_ref, cos_ref, sin_ref, dq_ref, dk_ref, dv_ref,
             dq_acc, dk_acc, dv_acc, dk_head, dv_head, ka_sc, qa_sc, s_sc, dp_sc, p_sc, ds_sc):
        for hh in range(LANE // HEAD_DIM):
            ls = slice(HEAD_DIM * hh, HEAD_DIM * (hh + 1))
            _store_keys(ka_sc, k_ref, ls)
            vb = v_ref[:, ls]
            dk_head[...] = jnp.zeros_like(dk_head)
            dv_head[...] = jnp.zeros_like(dv_head)
            for i in range(nb):
                rs = slice(MOBA_BLOCK * i, MOBA_BLOCK * (i + 1))
                width = MOBA_BLOCK * (i + 1)
                qa_sc[:, 0:HEAD_DIM] = (q_ref[rs, ls] * scale).astype(BF16)
                qa_sc[:, HEAD_DIM:] = bias_ref[rs, ls]
                s_sc[0:width, :] = _dot_nt(ka_sc[0:width, :], qa_sc[...])
                s_sc[rs, :] = jnp.where(_causal((MOBA_BLOCK, MOBA_BLOCK), transposed=True), s_sc[rs, :], NEG)
                dob = do_ref[rs, ls]
                delta = _row_vector(jnp.sum(dob * o_ref[rs, ls], axis=1, keepdims=True))
                lse_row = lse_ref[0, hh:hh + 1, rs]
                dob16 = dob.astype(BF16)
                dp_sc[0:width, :] = _dot_nt(vb[0:width], dob16)
                for c in range(width // SCORE_CHUNK):
                    rows = slice(SCORE_CHUNK * c, SCORE_CHUNK * (c + 1))
                    p = jnp.exp(s_sc[rows, :] - lse_row)
                    p_sc[rows, :] = p.astype(BF16)
                    ds_sc[rows, :] = (p * (dp_sc[rows, :] - delta)).astype(BF16)
                dv_head[0:width, :] += _dot(p_sc[0:width, :], dob16)
                dk_head[0:width, :] += _dot(ds_sc[0:width, :], qa_sc[:, 0:HEAD_DIM])
                dq_acc[rs, ls] = _dot_tn(ds_sc[0:width, :], ka_sc[0:width, 0:HEAD_DIM]) * scale
            dk_acc[:, ls] = dk_head[...]
            dv_acc[:, ls] = dv_head[...]
        cos_t, sin_t = cos_ref[...], sin_ref[...]
        dq_ref[...] = _rope_transposed(dq_acc[...], cos_t, sin_t).astype(BF16)
        dk_ref[...] = _rope_transposed(dk_acc[...], cos_t, sin_t).astype(BF16)
        dv_ref[...] = dv_acc[...].astype(BF16)

    return _grid_call(
        body, "attn_bwd", D_ATTN // LANE,
        in_specs=[pair, pair, pair, pair, pair, pl.BlockSpec((1, 8, seq), lambda p: (p, 0, 0)), pair, table, table],
        out_specs=[pair, pair, pair], out_shape=[_sds((seq, D_ATTN), BF16)] * 3,
        operands=(q, k, v, bias, o, lse, do, cos, sin),
        scratch=[pltpu.VMEM((seq, LANE), F32)] * 3 + [pltpu.VMEM((seq, HEAD_DIM), F32)] * 2
        + [pltpu.VMEM((seq, LANE), BF16), pltpu.VMEM((MOBA_BLOCK, LANE), BF16)]
        + [pltpu.VMEM((seq, MOBA_BLOCK), F32)] * 2 + [pltpu.VMEM((seq, MOBA_BLOCK), BF16)] * 2,
        tasks=tasks)


def _in_bwd(dq, dk, dv, dpooled, dzg, dr1, win, tasks=()):
    seq = dr1.shape[0]
    nt = seq // TOK

    def body(dq_ref, dk_ref, dv_ref, dp_ref, dpnext_ref, dzg_ref, dr_ref, win_ref, dx_ref, dz_ref, ext):
        i = pl.program_id(0)
        dp = dp_ref[...]
        dpn = jnp.where(i < nt - 1, dpnext_ref[...], 0.0)
        for grp, window in enumerate(POOL_WINDOWS):
            cols = slice(POOL_GROUP * grp, POOL_GROUP * (grp + 1))
            ext[0:TOK, cols] = dp[:, cols] / _pool_count(i * TOK, TOK, window)
            ext[TOK:, cols] = dpn[:, cols] / _pool_count((i + 1) * TOK, POOL_HALO, window)
        for grp, window in enumerate(POOL_WINDOWS):
            cols = slice(POOL_GROUP * grp, POOL_GROUP * (grp + 1))
            acc = ext[0:TOK, cols] - dp[:, cols]
            for kk in range(1, window):
                acc = acc + ext[pl.ds(kk, TOK), cols]
            dz_ref[:, 3 * D_ATTN + POOL_GROUP * grp:3 * D_ATTN + POOL_GROUP * (grp + 1)] = acc.astype(BF16)
        dz_ref[:, 0:D_ATTN] = dq_ref[...]
        dz_ref[:, D_ATTN:2 * D_ATTN] = dk_ref[...]
        dz_ref[:, 2 * D_ATTN:3 * D_ATTN] = dv_ref[...]
        dz_ref[:, 3 * D_ATTN + D_POOL:] = dzg_ref[...]
        dx = ALPHA * dr_ref[...]
        for n in range(N_DEV):
            dx = dx + _dot_nt(dz_ref[:, D_ATTN * n:D_ATTN * (n + 1)], win_ref[n])
        dx_ref[...] = dx

    halo = pl.BlockSpec((POOL_HALO, D_POOL),
                        lambda i: (jnp.minimum((i + 1) * (TOK // POOL_HALO), seq // POOL_HALO - 1), 0))
    return _grid_call(
        body, "in_bwd", nt,
        in_specs=[_rows(D_ATTN), _rows(D_ATTN), _rows(D_ATTN), _rows(D_POOL), halo, _rows(2 * D_MODEL),
                  _rows(D_MODEL), _full(win.shape)],
        out_specs=[_rows(D_MODEL), _rows(D_IN_PROJ)],
        out_shape=[_sds((seq, D_MODEL), F32), _sds((seq, D_IN_PROJ), BF16)],
        operands=(dq, dk, dv, dpooled, dpooled, dzg, dr1, win),
        scratch=[pltpu.VMEM((TOK + POOL_HALO, D_POOL), F32)], tasks=tasks)


def _dw_mixers(o, ypre, merged, dya, dyp, drb, pooled, dmixed, tasks=()):
    seq = o.shape[0]
    groups = len(POOL_WINDOWS)
    col = pl.BlockSpec((seq, LANE), lambda n: (0, n))
    grp = pl.BlockSpec((seq, POOL_GROUP), lambda n: (0, jnp.minimum(n, groups - 1)))
    owner = lambda rows, cols: pl.BlockSpec((1, rows, cols), lambda n: (n, 0, 0))

    def body(o_ref, ypre_ref, merged_ref, dya_ref, dyp_ref, dr_ref, pooled_ref, dmixed_ref,
             dba_ref, dbp_ref, dout_ref, dpool_ref, ob_sc):
        n = pl.program_id(0)

        @pl.when(n == 0)
        def _():
            ob_sc[...] = o_ref[...].astype(BF16)

        dba_ref[0] = _dot_tn(dya_ref[...], ob_sc[...]).T.astype(BF16)
        dbp_ref[0] = _dot_tn(dyp_ref[...], ypre_ref[...]).T.astype(BF16)
        dout_ref[0] = _dot_tn(merged_ref[...], dr_ref[...]).astype(BF16)

        @pl.when(n < groups)
        def _():
            dpool_ref[0] = _dot_tn(pooled_ref[...], dmixed_ref[...])

    return _grid_call(
        body, "dw_mixers", N_DEV,
        in_specs=[_full(o.shape), _full(ypre.shape), col, col, col, _full(drb.shape), grp, grp],
        out_specs=[owner(D_ATTN, LANE), owner(D_POOL, LANE), owner(D_MODEL // N_DEV, D_MODEL),
                   pl.BlockSpec((1, POOL_GROUP, POOL_GROUP), lambda n: (jnp.minimum(n, groups - 1), 0, 0))],
        out_shape=[_sds((N_DEV, D_ATTN, LANE), BF16), _sds((N_DEV, D_POOL, LANE), BF16),
                   _sds((N_DEV, D_MODEL // N_DEV, D_MODEL), BF16), _sds((groups, POOL_GROUP, POOL_GROUP), F32)],
        operands=(o, ypre, merged, dya, dyp, drb, pooled, dmixed),
        scratch=[pltpu.VMEM((seq, D_ATTN), BF16)], tasks=tasks)


def _to_bf16(arrays, tasks=()):
    n = len(arrays)

    def body(*refs):
        for src, dst in zip(refs[:n], refs[n:]):
            dst[...] = src[...].astype(BF16)

    return _grid_call(
        body, "to_bf16", 1, in_specs=[_full(a.shape) for a in arrays],
        out_specs=[pl.BlockSpec(a.shape, lambda i: (0, 0)) for a in arrays],
        out_shape=[_sds(a.shape, BF16) for a in arrays], operands=arrays, tasks=tasks)


def _matmul(name, a, b, out_shape, out_dtype, steps, a_spec, b_spec, o_spec, tasks=()):
    def body(a_ref, b_ref, o_ref):
        o_ref[...] = _dot(a_ref[...], b_ref[...]).reshape(o_ref.shape).astype(o_ref.dtype)

    (out,), results = _grid_call(body, name, steps, in_specs=[a_spec, b_spec], out_specs=[o_spec],
                                 out_shape=[_sds(out_shape, out_dtype)], operands=(a, b), tasks=tasks)
    return out, results


def _place():
    return lax.axis_index("x"), lax.axis_index("y"), lax.axis_index("c")


def _other_chips(x, y):
    return [(1 - x, y), (x, 1 - y), (1 - x, 1 - y)]


DMA_SEMS = pltpu.SemaphoreType.DMA


class _AllGather:
    def __init__(self, shards, lag=0):
        self.operands = list(shards)
        self.n = len(shards)
        self.lag = lag
        self.out_shape = [_sds((N_DEV, *s.shape), s.dtype) for s in shards]
        self.sems = [DMA_SEMS((7 * self.n,)), DMA_SEMS((7 * self.n,)), DMA_SEMS((self.n,))]

    def _copy(self, refs, a, k, block, to, from_input=False):
        ins, outs, (send_sems, recv_sems, _) = refs
        px, py, pc = block
        dst = outs[a].at[4 * px + 2 * py + pc]
        return pltpu.make_async_remote_copy(
            src_ref=ins[a] if from_input else dst, dst_ref=dst,
            send_sem=send_sems.at[7 * a + k], recv_sem=recv_sems.at[7 * a + k],
            device_id=to, device_id_type=MESH)

    def _local(self, refs, a):
        ins, outs, (_, _, local_sems) = refs
        x, y, c = _place()
        return pltpu.make_async_copy(ins[a], outs[a].at[4 * x + 2 * y + c], local_sems.at[a])

    def _pass_on(self, refs, a):
        x, y, c = _place()
        origin = ((x + 1 - c) % 2, (y + c) % 2, c)
        target = ((x + c) % 2, (y + 1 - c) % 2, c)
        return self._copy(refs, a, 3, origin, target)

    def start(self, refs):
        x, y, c = _place()
        for a in range(self.n):
            self._local(refs, a).start()
        for a in range(self.n):
            self._copy(refs, a, 0, (x, y, c), (x, y, 1 - c), True).start()
            for j, chip in enumerate(_other_chips(x, y)[:2]):
                self._copy(refs, a, 1 + j, (x, y, c), (*chip, c), True).start()

    def middle(self, refs):
        x, y, c = _place()
        me, sibling = (x, y, c), (x, y, 1 - c)
        chips = _other_chips(x, y)
        for a in range(self.n):
            for j in range(2):
                self._copy(refs, a, 1 + j, (*chips[j], c), me).wait_recv()
        for a in range(self.n):
            self._pass_on(refs, a).start()
            for j in range(2):
                self._copy(refs, a, 4 + j, (*chips[j], c), sibling).start()

    def late(self, refs):
        x, y, c = _place()
        diagonal = (1 - x, 1 - y, c)
        for a in range(self.n):
            self._copy(refs, a, 3, diagonal, (x, y, c)).wait_recv()
            self._copy(refs, a, 6, diagonal, (x, y, 1 - c)).start()

    def finish(self, refs):
        x, y, c = _place()
        me, sibling = (x, y, c), (x, y, 1 - c)
        chips = _other_chips(x, y)
        for a in range(self.n):
            self._copy(refs, a, 0, sibling, me).wait_recv()
            for j, chip in enumerate(chips):
                self._copy(refs, a, 4 + j, (*chip, 1 - c), me).wait_recv()
        for a in range(self.n):
            self._copy(refs, a, 0, me, sibling, True).wait_send()
            for j, chip in enumerate(chips[:2]):
                self._copy(refs, a, 1 + j, me, (*chip, c), True).wait_send()
            self._pass_on(refs, a).wait_send()
            for j, chip in enumerate(chips):
                self._copy(refs, a, 4 + j, (*chip, c), sibling).wait_send()
            self._local(refs, a).wait()


class _SiblingSend:
    def __init__(self, partials):
        self.operands = list(partials)
        self.n = len(partials)
        self.out_shape = [_sds((4, *p.shape[1:]), p.dtype) for p in partials]
        self.sems = [DMA_SEMS((4 * self.n,)), DMA_SEMS((4 * self.n,))]

    def _copy(self, refs, a, q):
        ins, outs, (send_sems, recv_sems) = refs
        x, y, c = _place()
        return pltpu.make_async_remote_copy(
            src_ref=ins[a].at[2 * q + 1 - c], dst_ref=outs[a].at[q],
            send_sem=send_sems.at[4 * a + q], recv_sem=recv_sems.at[4 * a + q],
            device_id=(x, y, 1 - c), device_id_type=MESH)

    def start(self, refs):
        for a in range(self.n):
            for q in range(4):
                self._copy(refs, a, q).start()

    def middle(self, refs):
        pass

    def finish(self, refs):
        for a in range(self.n):
            for q in range(4):
                self._copy(refs, a, q).wait()


class _ChipScatter:
    def __init__(self, chip_partials):
        self.operands = list(chip_partials)
        self.n = len(chip_partials)
        self.out_shape = [_sds(p.shape, p.dtype) for p in chip_partials]
        self.sems = [DMA_SEMS((3 * self.n,)), DMA_SEMS((3 * self.n,)), DMA_SEMS((self.n,))]

    def _copy(self, refs, a, k, arrival=False):
        ins, outs, (send_sems, recv_sems, _) = refs
        x, y, c = _place()
        px, py = _other_chips(x, y)[k]
        mine, theirs = 2 * x + y, 2 * px + py
        return pltpu.make_async_remote_copy(
            src_ref=ins[a].at[mine if arrival else theirs], dst_ref=outs[a].at[theirs if arrival else mine],
            send_sem=send_sems.at[3 * a + k], recv_sem=recv_sems.at[3 * a + k],
            device_id=(px, py, c), device_id_type=MESH)

    def _local(self, refs, a):
        ins, outs, (_, _, local_sems) = refs
        x, y, _ = _place()
        return pltpu.make_async_copy(ins[a].at[2 * x + y], outs[a].at[2 * x + y], local_sems.at[a])

    def start(self, refs):
        for a in range(self.n):
            self._local(refs, a).start()
            for k in range(3):
                self._copy(refs, a, k).start()

    def middle(self, refs):
        pass

    def finish(self, refs):
        for a in range(self.n):
            for k in range(3):
                self._copy(refs, a, k, arrival=True).wait_recv()
        for a in range(self.n):
            for k in range(3):
                self._copy(refs, a, k).wait_send()
            self._local(refs, a).wait()


class _DirectScatter:
    def __init__(self, partials):
        self.operands = list(partials)
        self.n = len(partials)
        self.out_shape = [_sds(p.shape, p.dtype) for p in partials]
        self.sems = [DMA_SEMS((7 * self.n,)), DMA_SEMS((7 * self.n,)), DMA_SEMS((self.n,))]

    def _copy(self, refs, a, k, arrival=False):
        ins, outs, (send_sems, recv_sems, _) = refs
        x, y, c = _place()
        peer = [(x, y, 1 - c), (1 - x, y, c), (x, 1 - y, c), (1 - x, 1 - y, c),
                (1 - x, y, 1 - c), (x, 1 - y, 1 - c), (1 - x, 1 - y, 1 - c)][k]
        mine, theirs = 4 * x + 2 * y + c, 4 * peer[0] + 2 * peer[1] + peer[2]
        return pltpu.make_async_remote_copy(
            src_ref=ins[a].at[mine if arrival else theirs], dst_ref=outs[a].at[theirs if arrival else mine],
            send_sem=send_sems.at[7 * a + k], recv_sem=recv_sems.at[7 * a + k],
            device_id=peer, device_id_type=MESH)

    def _local(self, refs, a):
        ins, outs, (_, _, local_sems) = refs
        x, y, c = _place()
        return pltpu.make_async_copy(ins[a].at[4 * x + 2 * y + c], outs[a].at[4 * x + 2 * y + c], local_sems.at[a])

    def start(self, refs):
        for a in range(self.n):
            self._local(refs, a).start()
            for k in range(7):
                self._copy(refs, a, k).start()

    def middle(self, refs):
        pass

    def finish(self, refs):
        for a in range(self.n):
            for k in range(7):
                self._copy(refs, a, k, arrival=True).wait_recv()
        for a in range(self.n):
            for k in range(7):
                self._copy(refs, a, k).wait_send()
            self._local(refs, a).wait()


def _task_args(tasks):
    hbm = pl.BlockSpec(memory_space=pl.ANY)
    operands = [o for t in tasks for o in t.operands]
    out_shape = [s for t in tasks for s in t.out_shape]
    sems = [s for t in tasks for s in t.sems]
    return operands, [hbm] * len(operands), out_shape, [hbm] * len(out_shape), sems


def _task_refs(tasks, ins, outs, sems):
    per_task = []
    for t in tasks:
        ni, no, ns = len(t.operands), len(t.out_shape), len(t.sems)
        per_task.append((ins[:ni], outs[:no], sems[:ns]))
        ins, outs, sems = ins[ni:], outs[no:], sems[ns:]
    return per_task


def _task_results(tasks, outs):
    res = []
    for t in tasks:
        res.append(list(outs[:len(t.out_shape)]))
        outs = outs[len(t.out_shape):]
    return res


def _carry(body, tasks, n_in, n_out, n_scratch, steps):
    if not tasks:
        return body
    t_in = sum(len(t.operands) for t in tasks)
    t_out = sum(len(t.out_shape) for t in tasks)

    def wrapped(*refs):
        ins, refs = refs[:n_in], refs[n_in:]
        t_ins, refs = refs[:t_in], refs[t_in:]
        outs, refs = refs[:n_out], refs[n_out:]
        t_outs, refs = refs[:t_out], refs[t_out:]
        scratch, t_sems = refs[:n_scratch], refs[n_scratch:]
        per_task = _task_refs(tasks, t_ins, t_outs, t_sems)
        step = pl.program_id(0)

        @pl.when(step == 0)
        def _():
            for t, r in zip(tasks, per_task):
                t.start(r)

        if steps == 1:
            body(*ins, *outs, *scratch)
        for t, r in zip(tasks, per_task):
            pl.when(step == max(steps - 1 - getattr(t, "lag", 0), 0))(functools.partial(t.middle, r))
            if hasattr(t, "late"):
                pl.when(step == steps - 1)(functools.partial(t.late, r))
        if steps > 1:
            body(*ins, *outs, *scratch)

        @pl.when(step == steps - 1)
        def _():
            for t, r in zip(tasks, per_task):
                t.finish(r)

    return wrapped


def _exchange(name, tasks):
    operands, in_specs, out_shape, out_specs, sems = _task_args(tasks)

    def body(*refs):
        ni, no = len(operands), len(out_shape)
        per_task = _task_refs(tasks, refs[:ni], refs[ni:ni + no], refs[ni + no:])
        for phase in ("start", "middle", "late", "finish"):
            for t, r in zip(tasks, per_task):
                if hasattr(t, phase):
                    getattr(t, phase)(r)

    outs = pl.pallas_call(body, name=name, in_specs=in_specs, out_specs=out_specs, out_shape=out_shape,
                          scratch_shapes=sems)(*operands)
    return _task_results(tasks, outs)


def _row_tile(rows, cols, whole_up_to=256 * 1024):
    if rows * cols <= whole_up_to:
        return rows
    for t in (256, 176, 128, 64, 32, 16, 8):
        if rows % t == 0:
            return t
    return rows


def _pair_sum(name, partials, from_sibling):
    n = len(partials)
    _, rows, cols = partials[0].shape
    tile = _row_tile(rows, cols, 512 * 1024)

    def body(*refs):
        south = lax.axis_index("c") == 0
        for p_ref, s_ref, o_ref in zip(refs[:n], refs[n:2 * n], refs[2 * n:]):
            mine = jnp.where(south, p_ref[0, 0].astype(F32), p_ref[0, 1].astype(F32))
            o_ref[0] = (mine + s_ref[0].astype(F32)).astype(o_ref.dtype)

    blk = pl.BlockSpec((1, tile, cols), lambda q, i: (q, i, 0))
    return pl.pallas_call(
        body, name=name, grid=(4, rows // tile),
        in_specs=[pl.BlockSpec((1, 2, tile, cols), lambda q, i: (q, 0, i, 0))] * n + [blk] * n,
        out_specs=[blk] * n, out_shape=[_sds(s.shape, s.dtype) for s in from_sibling],
        compiler_params=_params("parallel", "parallel"),
    )(*[p.reshape(4, 2, rows, cols) for p in partials], *from_sibling)


def _sum_leading(name, stacked):
    parts, rows, cols = stacked.shape
    tile = _row_tile(rows, cols, (512 if parts <= 4 else 256) * 1024)

    def body(s_ref, o_ref):
        acc = s_ref[0].astype(F32)
        for d in range(1, parts):
            acc = acc + s_ref[d].astype(F32)
        o_ref[...] = acc

    return pl.pallas_call(
        body, name=name, grid=(rows // tile,),
        in_specs=[pl.BlockSpec((parts, tile, cols), lambda i: (0, i, 0))],
        out_specs=pl.BlockSpec((tile, cols), lambda i: (i, 0)),
        out_shape=_sds((rows, cols), F32),
        compiler_params=_params("parallel"),
    )(stacked)


def _adamw_math(w, g, m, v):
    nm = ADAM_B1 * m + (1.0 - ADAM_B1) * g
    nv = ADAM_B2 * v + (1.0 - ADAM_B2) * (g * g)
    m_hat = nm / (1.0 - ADAM_B1 ** ADAM_STEP)
    v_hat = nv / (1.0 - ADAM_B2 ** ADAM_STEP)
    return -ADAM_LR * (m_hat / (jnp.sqrt(v_hat) + ADAM_EPS) + ADAM_WD * w), nm, nv


def _adamw(name, w, g, m, v):
    rows, cols = w.shape
    tile = _row_tile(rows, cols)

    def body(w_ref, g_ref, m_ref, v_ref, d_ref, nm_ref, nv_ref):
        d_ref[...], nm_ref[...], nv_ref[...] = _adamw_math(w_ref[...], g_ref[...], m_ref[...], v_ref[...])

    blk = pl.BlockSpec((tile, cols), lambda i: (i, 0))
    return pl.pallas_call(
        body, name=name, grid=(rows // tile,),
        in_specs=[blk] * 4, out_specs=[blk] * 3,
        out_shape=[_sds((rows, cols), F32)] * 3,
        compiler_params=_params("parallel"),
    )(w, g, m, v)


def _sum_adamw(name, params, tasks=()):
    n = len(params)
    parts, rows, cols = params[0][0].shape
    tile = _row_tile(rows, cols)

    def body(*refs):
        ins, outs = refs[:4 * n], refs[4 * n:]
        for p in range(n):
            s_ref, w_ref, m_ref, v_ref = ins[4 * p:4 * p + 4]
            g_ref, d_ref, nm_ref, nv_ref = outs[4 * p:4 * p + 4]
            g = s_ref[0].astype(F32)
            for d in range(1, parts):
                g = g + s_ref[d].astype(F32)
            g_ref[...] = g
            d_ref[...], nm_ref[...], nv_ref[...] = _adamw_math(w_ref[...], g, m_ref[...], v_ref[...])

    blk = pl.BlockSpec((tile, cols), lambda i: (i, 0))
    outs, results = _grid_call(
        body, name, rows // tile,
        in_specs=([pl.BlockSpec((parts, tile, cols), lambda i: (0, i, 0))] + [blk] * 3) * n, out_specs=[blk] * (4 * n),
        out_shape=[_sds((rows, cols), F32)] * (4 * n),
        operands=[t for p in params for t in p], tasks=tasks)
    return [outs[4 * p:4 * p + 4] for p in range(n)], results


SMALL = ("b_gate", "w_pool", "pool_scale", "ln1_g", "ln1_b", "conv_b", "ln2_g", "ln2_b")
PACKED = SMALL + ("conv_w", "loss")
TILE = 8 * LANE


def _pack(parts):
    tiles = []
    for p in parts:
        flat = p.reshape(-1)
        tiles.append(jnp.pad(flat, (0, -flat.size % TILE)).reshape(-1, LANE))
    return jnp.concatenate(tiles, axis=0)


def _unpack(packed, shapes):
    out, at = [], 0
    for shape in shapes:
        size = math.prod(shape)
        rows = -(-size // TILE) * 8
        out.append(packed[at:at + rows].reshape(-1)[:size].reshape(shape))
        at += rows
    return out


MIXER = ("w_branch_attn", "w_branch_pool", "w_out", "conv_w")
FFN = ("w_ffn_gate_t", "w_ffn_up_t", "w_ffn_down")


def _columns(t):
    return jnp.transpose(t, (1, 0, 2)).reshape(t.shape[1], N_DEV * t.shape[2])


def _row_blocks(t):
    return t.reshape(N_DEV * t.shape[1], t.shape[2])


def _by_owner(t):
    return t.reshape(N_DEV, t.shape[0] // N_DEV, t.shape[1])


def _reduce_halves(names, partials, from_sibling):
    out = [None] * len(names)
    for shape in dict.fromkeys(p.shape for p in partials):
        group = [i for i, p in enumerate(partials) if p.shape == shape]
        sums = _pair_sum("pair_sum_" + names[group[0]], [partials[i] for i in group], [from_sibling[i] for i in group])
        for i, s in zip(group, sums):
            out[i] = s
    return out


def _local_step(x, target, shards, small):
    seq = x.shape[0]
    cos, sin = _rope_tables(seq)
    cast, ((w_in_all,),) = _to_bf16([shards[n] for n in MIXER[:3] + FFN], tasks=[_AllGather([shards["w_in"]])])
    shards = {**shards, **dict(zip(MIXER[:3] + FFN, cast))}
    (xt, q, k, v, u, g, kmean), (mixer,) = _proj_in(
        x, w_in_all, small["b_gate"], cos, sin, tasks=[_AllGather([shards[n] for n in MIXER], lag=2)])
    wba, wbp, wout, conv_w = _columns(mixer[0]), _columns(mixer[1]), _row_blocks(mixer[2]), _columns(mixer[3])
    half = FF_SHARD // 2
    (o, lse, bias), ((wgt, wut_top),) = _attn_fwd(
        q, k, v, kmean.reshape(seq // MOBA_BLOCK, D_ATTN),
        tasks=[_AllGather([shards["w_ffn_gate_t"], shards["w_ffn_up_t"][:half]], lag=1)])
    (ya, yp, pooled, mixed, ypre, merged, xhat1, rstd1, h1, h1b), ((wut_bottom,),) = _mix(
        o, u, g, x, wba, wbp, wout, small["w_pool"], small["pool_scale"], small["ln1_g"], small["ln1_b"],
        tasks=[_AllGather([shards["w_ffn_up_t"][half:]], lag=3)])
    wgt, wut = _row_blocks(wgt), _row_blocks(jnp.concatenate([wut_top, wut_bottom], axis=1))
    (a, uf, act), ((wd,),) = _ffn_up(
        h1b, wgt, wut, conv_w, small["conv_b"], tasks=[_AllGather([shards["w_ffn_down"]], lag=4)])
    wd = _row_blocks(wd)
    dr2, dr2b, loss, dg2, db2 = _ffn_down(act, wd, h1, target, small["ln2_g"], small["ln2_b"])

    da, du, dwd, dwg, dwu, dconv = _ffn_bwd(dr2b, h1b, a, uf, wd, conv_w, small["conv_b"])
    ffn_partials = [_by_owner(dwg), _by_owner(dwu), _by_owner(dwd)]
    (dr1, dr1b, dg1, db1), (ffn_sibling,) = _ln1_bwd(
        dr2, da, du, wgt, wut, xhat1, rstd1, small["ln1_g"], tasks=[_SiblingSend(ffn_partials)])
    ffn_chip = _reduce_halves(FFN, ffn_partials, ffn_sibling)
    (dzg, dya, dyp, do, dmixed, dpooled, dbg, dps), (gate_top,) = _mix_bwd(
        dr1b, ya, yp, g, mixed, wout, wba, wbp, small["w_pool"], small["pool_scale"],
        tasks=[_ChipScatter([ffn_chip[0][:, :half]])])
    (dw_ba, dw_bp, dw_out, dw_pool), (gate_bottom,) = _dw_mixers(
        o, ypre, merged, dya, dyp, dr1b, pooled, dmixed, tasks=[_ChipScatter([ffn_chip[0][:, half:]])])
    gate_landed = [jnp.concatenate([gate_top[0], gate_bottom[0]], axis=1)]
    mixer_partials = [dw_ba, dw_bp, dw_out]
    (dq, dk, dv), (up_down_landed, mixer_sibling) = _attn_bwd(
        q, k, v, bias, o, lse, do, cos, sin, tasks=[_ChipScatter(ffn_chip[1:3]), _SiblingSend(mixer_partials)])
    mixer_chip = _reduce_halves(MIXER[:3], mixer_partials, mixer_sibling)
    (grad_x, dz), _ = _in_bwd(dq, dk, dv, dpooled, dzg, dr1, w_in_all)
    little = {"b_gate": dbg, "w_pool": dw_pool, "pool_scale": dps, "ln1_g": dg1, "ln1_b": db1, "conv_b": dconv[3:4],
              "ln2_g": dg2, "ln2_b": db2, "conv_w": dconv[0:3], "loss": loss}
    dw_in, (mixer_landed,) = _matmul(
        "dw_in", xt, dz, (N_DEV, D_MODEL, D_ATTN), BF16, N_DEV,
        _full(xt.shape), pl.BlockSpec((seq, D_ATTN), lambda n: (0, n)),
        pl.BlockSpec((1, D_MODEL, D_ATTN), lambda n: (n, 0, 0)), tasks=[_ChipScatter(mixer_chip)])

    landed = dict(zip(FFN + MIXER[:3], gate_landed + up_down_landed + mixer_landed))
    return grad_x, landed, dw_in, _pack([little[n] for n in PACKED])


def kernel(x, w_in, b_gate, w_branch_attn, w_pool, pool_scale, w_branch_pool, w_out, ln1_g, ln1_b, w_ffn_gate, w_ffn_up, conv_w, conv_b, w_ffn_down, ln2_g, ln2_b, loss_target, m_w_in, m_b_gate, m_w_branch_attn, m_w_pool, m_pool_scale, m_w_branch_pool, m_w_out, m_ln1_g, m_ln1_b, m_w_ffn_gate, m_w_ffn_up, m_conv_w, m_conv_b, m_w_ffn_down, m_ln2_g, m_ln2_b, v_w_in, v_b_gate, v_w_branch_attn, v_w_pool, v_pool_scale, v_w_branch_pool, v_w_out, v_ln1_g, v_ln1_b, v_w_ffn_gate, v_w_ffn_up, v_conv_w, v_conv_b, v_w_ffn_down, v_ln2_g, v_ln2_b):
    me = 4 * lax.axis_index("x") + 2 * lax.axis_index("y") + lax.axis_index("c")
    weights = dict(w_in=w_in, b_gate=b_gate, w_branch_attn=w_branch_attn, w_pool=w_pool, pool_scale=pool_scale,
                   w_branch_pool=w_branch_pool, w_out=w_out, ln1_g=ln1_g, ln1_b=ln1_b, w_ffn_gate=w_ffn_gate,
                   w_ffn_up=w_ffn_up, conv_w=conv_w, conv_b=conv_b, w_ffn_down=w_ffn_down, ln2_g=ln2_g, ln2_b=ln2_b)
    m_in = dict(w_in=m_w_in, b_gate=m_b_gate, w_branch_attn=m_w_branch_attn, w_pool=m_w_pool,
                pool_scale=m_pool_scale, w_branch_pool=m_w_branch_pool, w_out=m_w_out, ln1_g=m_ln1_g, ln1_b=m_ln1_b,
                w_ffn_gate=m_w_ffn_gate, w_ffn_up=m_w_ffn_up, conv_w=m_conv_w, conv_b=m_conv_b,
                w_ffn_down=m_w_ffn_down, ln2_g=m_ln2_g, ln2_b=m_ln2_b)
    v_in = dict(w_in=v_w_in, b_gate=v_b_gate, w_branch_attn=v_w_branch_attn, w_pool=v_w_pool,
                pool_scale=v_pool_scale, w_branch_pool=v_w_branch_pool, w_out=v_w_out, ln1_g=v_ln1_g, ln1_b=v_ln1_b,
                w_ffn_gate=v_w_ffn_gate, w_ffn_up=v_w_ffn_up, conv_w=v_conv_w, conv_b=v_conv_b,
                w_ffn_down=v_w_ffn_down, ln2_g=v_ln2_g, ln2_b=v_ln2_b)
    weights = {n: a[0] for n, a in weights.items()}
    m_in = {n: a[0] for n, a in m_in.items()}
    v_in = {n: a[0] for n, a in v_in.items()}

    shards = {"w_in": weights["w_in"].astype(BF16), "w_branch_attn": weights["w_branch_attn"],
              "w_branch_pool": weights["w_branch_pool"], "w_out": weights["w_out"],
              "w_ffn_gate_t": weights["w_ffn_gate"].T, "w_ffn_up_t": weights["w_ffn_up"].T,
              "w_ffn_down": weights["w_ffn_down"], "conv_w": weights["conv_w"]}
    small = {"b_gate": weights["b_gate"][None], "w_pool": weights["w_pool"], "pool_scale": weights["pool_scale"][None],
             "ln1_g": weights["ln1_g"][None], "ln1_b": weights["ln1_b"][None], "conv_b": weights["conv_b"][None],
             "ln2_g": weights["ln2_g"][None], "ln2_b": weights["ln2_b"][None]}

    grad_x, landed, dw_in, packed = _local_step(x[0], loss_target[0], shards, small)

    grads, delta, new_m, new_v = {}, {}, {}, {}

    def param(n, transposed=False):
        if transposed:
            return landed[n + "_t"], weights[n].T, m_in[n].T, v_in[n].T
        return landed[n], weights[n], m_in[n], v_in[n]

    def keep(n, updated, transposed=False):
        grads[n], delta[n], new_m[n], new_v[n] = (t.T for t in updated) if transposed else updated

    ((w_in_sibling,),) = _exchange("sibling_grads", [_SiblingSend([dw_in])])
    w_in_chip = _reduce_halves(["w_in"], [dw_in], [w_in_sibling])
    (landed["w_in"],), (all_small,) = _exchange("scatter_grads", [_ChipScatter(w_in_chip), _AllGather([packed])])
    (gate, up, down), _ = _sum_adamw(
        "update_w_ffn", [param("w_ffn_gate", True), param("w_ffn_up", True), param("w_ffn_down")])
    keep("w_ffn_gate", gate, True)
    keep("w_ffn_up", up, True)
    keep("w_ffn_down", down)
    (attn, pool), _ = _sum_adamw("update_w_branch", [param("w_branch_attn"), param("w_branch_pool")])
    keep("w_branch_attn", attn)
    keep("w_branch_pool", pool)
    for n in ("w_out", "w_in"):
        (updated,), _ = _sum_adamw("update_" + n, [param(n)])
        keep(n, updated)
    names = SMALL + ("conv_w",)
    small_sum = _sum_leading("sum_small", all_small)
    *small_grads, conv_w_grad, loss = _unpack(small_sum, [weights[n].shape for n in SMALL] + [(3, D_FF), (8, LANE)])
    loss = loss[0, 0]
    grads.update(zip(SMALL, small_grads))
    grads["conv_w"] = lax.dynamic_slice(conv_w_grad, (0, me * FF_SHARD), (3, FF_SHARD))
    flat = lambda d: _pack([d[n] for n in names])
    shapes = [weights[n].shape for n in names]
    for out, packed in zip((delta, new_m, new_v),
                           _adamw("adamw_small", flat(weights), flat(grads), flat(m_in), flat(v_in))):
        out.update(zip(names, _unpack(packed, shapes)))

    order = ("w_in", "b_gate", "w_branch_attn", "w_pool", "pool_scale", "w_branch_pool", "w_out", "ln1_g", "ln1_b",
             "w_ffn_gate", "w_ffn_up", "conv_w", "conv_b", "w_ffn_down", "ln2_g", "ln2_b")
    lead = lambda t: t[None]
    return (loss, lead(grad_x), *[lead(grads[n]) for n in order], *[lead(delta[n]) for n in order],
            *[lead(new_m[n]) for n in order], *[lead(new_v[n]) for n in order])
```

```python
import functools
import math

import jax
import jax.numpy as jnp
from jax import lax
from jax.experimental import pallas as pl
from jax.experimental.pallas import tpu as pltpu

F32 = jnp.float32
BF16 = jnp.bfloat16

D_MODEL = 1024
N_HEADS = 8
HEAD_DIM = 64
D_ATTN = N_HEADS * HEAD_DIM
MOBA_BLOCK = 256
MOBA_TOPK = 3
ROPE_THETA = 10000.0
POOL_WINDOWS = (2, 4, 8, 16)
POOL_GROUP = 128
D_POOL = len(POOL_WINDOWS) * POOL_GROUP
POOL_HALO = 16
D_FF = 2816
D_IN_PROJ = 3 * D_ATTN + D_POOL + 2 * D_MODEL
LN_EPS = 1e-5
ALPHA = 2.0 ** 0.25
NEG = -1e30
N_DEV = 8
FF_SHARD = D_FF // N_DEV

ADAM_LR = 0.001
ADAM_B1 = 0.9
ADAM_B2 = 0.999
ADAM_EPS = 1e-08
ADAM_WD = 0.01
ADAM_STEP = 10

TOK = 256
FF_CHUNK = 256
LANE = 128
VMEM_LIMIT = 56 * 1024 * 1024

MESH = pl.DeviceIdType.MESH
NT_DIMS = (((1,), (1,)), ((), ()))
TN_DIMS = (((0,), (0,)), ((), ()))


def _params(*sem):
    return pltpu.CompilerParams(dimension_semantics=sem or None, vmem_limit_bytes=VMEM_LIMIT)


def _full(shape):
    zeros = (0,) * len(shape)
    return pl.BlockSpec(shape, lambda *_: zeros, pipeline_mode=pl.Buffered(1))


def _rows(width, tile=TOK):
    return pl.BlockSpec((tile, width), lambda i: (i, 0))


def _sds(shape, dtype):
    return jax.ShapeDtypeStruct(shape, dtype)


def _dot(a, b):
    return jnp.dot(a, b, preferred_element_type=F32)


def _dot_nt(a, b):
    return lax.dot_general(a, b, NT_DIMS, preferred_element_type=F32)


def _dot_tn(a, b):
    return lax.dot_general(a, b, TN_DIMS, preferred_element_type=F32)


def _rope_tables(seq):
    half = HEAD_DIM // 2
    inv_freq = 1.0 / (ROPE_THETA ** (jnp.arange(half, dtype=F32) / half))
    ang = jnp.arange(seq, dtype=F32)[:, None] * inv_freq[None, :]
    cos, sin = jnp.cos(ang), jnp.sin(ang)
    return jnp.tile(cos, (1, 4)), jnp.tile(jnp.concatenate([-sin, sin], axis=1), (1, 2))


def _swap_halves(t):
    lane = lax.broadcasted_iota(jnp.int32, t.shape, 1)
    return jnp.where((lane % HEAD_DIM) < HEAD_DIM // 2, pltpu.roll(t, LANE - 32, 1), pltpu.roll(t, 32, 1))


def _rope(t, cos, sin):
    return t * cos + _swap_halves(t) * sin


def _rope_transposed(g, cos, sin):
    return g * cos + _swap_halves(g * sin)


def _ln_fwd(r, g, b):
    mu = jnp.mean(r, axis=-1, keepdims=True)
    xc = r - mu
    var = jnp.mean(xc * xc, axis=-1, keepdims=True)
    rstd = lax.rsqrt(var + LN_EPS)
    xhat = xc * rstd
    return xhat * g + b, xhat, rstd


def _ln_bwd(dy, xhat, rstd, g):
    dxh = dy * g
    m1 = jnp.mean(dxh, axis=-1, keepdims=True)
    m2 = jnp.mean(dxh * xhat, axis=-1, keepdims=True)
    return rstd * (dxh - m1 - xhat * m2)


def _normal_cdf(a):
    return 0.5 * (1.0 + lax.erf(a * (1.0 / math.sqrt(2.0))))


def _gelu_derivative(a, cdf):
    return cdf + a * (jnp.exp(-0.5 * a * a) * (1.0 / math.sqrt(2.0 * math.pi)))


def _shift_down(a, k):
    row = lax.broadcasted_iota(jnp.int32, a.shape, 0)
    return jnp.where(row >= k, pltpu.roll(a, k, 0), 0.0)


def _shift_up(a, k):
    n = a.shape[0]
    row = lax.broadcasted_iota(jnp.int32, a.shape, 0)
    return jnp.where(row < n - k, pltpu.roll(a, n - k, 0), 0.0)


def _conv(a, cw, cb):
    return cw[2:3, :] * a + cw[1:2, :] * _shift_down(a, 1) + cw[0:1, :] * _shift_down(a, 2) + cb


def _pool_count(first_row, rows, window):
    t = first_row + lax.broadcasted_iota(jnp.int32, (rows, 1), 0)
    return jnp.minimum(t + 1, window).astype(F32)


def _grid_call(body, name, steps, in_specs, out_specs, out_shape, operands, scratch=(), tasks=()):
    t_operands, t_in_specs, t_out_shape, t_out_specs, t_sems = _task_args(tasks)
    outs = pl.pallas_call(
        _carry(body, tasks, len(in_specs), len(out_specs), len(scratch), steps), name=name, grid=(steps,),
        in_specs=list(in_specs) + t_in_specs, out_specs=list(out_specs) + t_out_specs,
        out_shape=list(out_shape) + t_out_shape, scratch_shapes=list(scratch) + t_sems,
        compiler_params=_params("arbitrary"),
    )(*operands, *t_operands)
    return outs[:len(out_specs)], _task_results(tasks, outs[len(out_specs):])


def _proj_in(x, win, b_gate, cos, sin, tasks=()):
    seq = x.shape[0]
    nt = seq // TOK

    def body(x_ref, win_ref, bg_ref, cos_ref, sin_ref, xt_ref, q_ref, k_ref, v_ref, u_ref, g_ref, km_ref):
        xb = x_ref[...].astype(BF16)
        xt_ref[...] = x_ref[...].T.astype(BF16)
        cos_t, sin_t = cos_ref[...], sin_ref[...]
        for sec, out_ref in ((0, q_ref), (1, k_ref)):
            z = _dot(xb, win_ref[sec])
            for c in range(D_ATTN // LANE):
                cols = slice(LANE * c, LANE * (c + 1))
                out_ref[:, cols] = _rope(z[:, cols], cos_t, sin_t)
        for b in range(TOK // MOBA_BLOCK):
            km_ref[b] = jnp.mean(k_ref[MOBA_BLOCK * b:MOBA_BLOCK * (b + 1), :], axis=0, keepdims=True)
        v_ref[...] = _dot(xb, win_ref[2]).astype(BF16)
        u_ref[...] = _dot(xb, win_ref[3])
        for n in range(4):
            cols = slice(D_ATTN * n, D_ATTN * (n + 1))
            g_ref[:, cols] = jax.nn.sigmoid(_dot(xb, win_ref[4 + n]) + bg_ref[:, cols])

    return _grid_call(
        body, "proj_in", nt,
        in_specs=[_rows(D_MODEL), _full(win.shape), _full((1, 2 * D_MODEL)), _rows(LANE), _rows(LANE)],
        out_specs=[pl.BlockSpec((D_MODEL, TOK), lambda i: (0, i)), _rows(D_ATTN), _rows(D_ATTN), _rows(D_ATTN),
                   _rows(D_POOL), _rows(2 * D_MODEL),
                   pl.BlockSpec((TOK // MOBA_BLOCK, 1, D_ATTN), lambda i: (i, 0, 0))],
        out_shape=[_sds((D_MODEL, seq), BF16), _sds((seq, D_ATTN), F32), _sds((seq, D_ATTN), F32),
                   _sds((seq, D_ATTN), BF16), _sds((seq, D_POOL), F32), _sds((seq, 2 * D_MODEL), F32),
                   _sds((seq // MOBA_BLOCK, 1, D_ATTN), F32)],
        operands=(x, win, b_gate, cos, sin), tasks=tasks)


SCORE_CHUNK = 128


def _store_keys(ka_sc, k_ref, ls):
    seq = ka_sc.shape[0]
    ka_sc[:, 0:HEAD_DIM] = k_ref[:, ls].astype(BF16)
    row = lax.broadcasted_iota(jnp.int32, (seq, HEAD_DIM), 0)
    lane = lax.broadcasted_iota(jnp.int32, (seq, HEAD_DIM), 1)
    in_block = (lane * MOBA_BLOCK <= row) & (row < (lane + 1) * MOBA_BLOCK)
    ka_sc[:, HEAD_DIM:] = jnp.where(in_block, 1.0, 0.0).astype(BF16)


def _block_bias(qf, km, i):
    if i <= MOBA_TOPK:
        return jnp.zeros((MOBA_BLOCK, HEAD_DIM), BF16)
    nb = km.shape[0]
    gate = lax.dot_general(km, qf, NT_DIMS, precision=lax.Precision.HIGHEST, preferred_element_type=F32)
    blk = lax.broadcasted_iota(jnp.int32, gate.shape, 0)
    rank = jnp.zeros(gate.shape, F32)
    for r in range(1, i):
        lower = pltpu.roll(gate, r, 0)
        rank = rank + jnp.where((blk >= r) & (lower >= gate), 1.0, 0.0)
        higher = pltpu.roll(gate, nb - r, 0)
        rank = rank + jnp.where((blk + r < i) & (higher > gate), 1.0, 0.0)
    bias = jnp.where((blk < i) & (rank >= MOBA_TOPK), NEG, 0.0)
    padded = jnp.concatenate([bias, jnp.zeros((LANE - nb, MOBA_BLOCK), F32)], axis=0)
    return jnp.transpose(padded)[:, 0:HEAD_DIM].astype(BF16)


def _causal(shape, transposed=False):
    row = lax.broadcasted_iota(jnp.int32, shape, 0)
    col = lax.broadcasted_iota(jnp.int32, shape, 1)
    return (row <= col) if transposed else (col <= row)


def _row_vector(col):
    return jnp.transpose(jnp.broadcast_to(col, (MOBA_BLOCK, LANE)))[0:1, :]


def _attn_fwd(q, k, v, kmean, tasks=()):
    seq = q.shape[0]
    nb = seq // MOBA_BLOCK
    assert nb == 8, "the block ranking keeps one sublane per key block"
    pair = pl.BlockSpec((seq, LANE), lambda p: (0, p))
    heads = LANE // HEAD_DIM

    def body(q_ref, k_ref, v_ref, km_ref, o_ref, lse_ref, bias_ref, ka_sc, qa_sc, s_sc, p_sc):
        lse_ref[0, heads:, :] = jnp.zeros((8 - heads, seq), F32)
        for hh in range(heads):
            ls = slice(HEAD_DIM * hh, HEAD_DIM * (hh + 1))
            _store_keys(ka_sc, k_ref, ls)
            vb = v_ref[:, ls]
            km = km_ref[:, ls]
            for i in range(nb):
                rs = slice(MOBA_BLOCK * i, MOBA_BLOCK * (i + 1))
                width = MOBA_BLOCK * (i + 1)
                qf = q_ref[rs, ls]
                bias = _block_bias(qf, km, i)
                bias_ref[rs, ls] = bias
                qa_sc[:, 0:HEAD_DIM] = (qf * HEAD_DIM ** -0.5).astype(BF16)
                qa_sc[:, HEAD_DIM:] = bias
                s_sc[:, 0:width] = _dot_nt(qa_sc[...], ka_sc[0:width, :])
                s_sc[:, rs] = jnp.where(_causal((MOBA_BLOCK, MOBA_BLOCK)), s_sc[:, rs], NEG)
                chunks = [slice(SCORE_CHUNK * c, SCORE_CHUNK * (c + 1)) for c in range(width // SCORE_CHUNK)]
                top = s_sc[:, chunks[0]]
                for c in chunks[1:]:
                    top = jnp.maximum(top, s_sc[:, c])
                m = jnp.max(top, axis=1, keepdims=True)
                total = jnp.zeros((MOBA_BLOCK, SCORE_CHUNK), F32)
                for c in chunks:
                    p = jnp.exp(s_sc[:, c] - m)
                    total = total + p
                    p_sc[:, c] = p.astype(BF16)
                l = jnp.sum(total, axis=1, keepdims=True)
                o_ref[rs, ls] = _dot(p_sc[:, 0:width], vb[0:width]) / l
                lse_ref[0, hh:hh + 1, rs] = _row_vector(m + jnp.log(l))

    return _grid_call(
        body, "attn_fwd", D_ATTN // LANE,
        in_specs=[pair, pair, pair, pl.BlockSpec((nb, LANE), lambda p: (0, p))],
        out_specs=[pair, pl.BlockSpec((1, 8, seq), lambda p: (p, 0, 0)), pair],
        out_shape=[_sds((seq, D_ATTN), F32), _sds((D_ATTN // LANE, 8, seq), F32), _sds((seq, D_ATTN), BF16)],
        operands=(q, k, v, kmean),
        scratch=[pltpu.VMEM((seq, LANE), BF16), pltpu.VMEM((MOBA_BLOCK, LANE), BF16),
                 pltpu.VMEM((MOBA_BLOCK, seq), F32), pltpu.VMEM((MOBA_BLOCK, seq), BF16)],
        tasks=tasks)


def _mix(o, u, g, x, wba, wbp, wout, w_pool, pool_scale, ln_g, ln_b, tasks=()):
    seq = x.shape[0]

    def body(o_ref, u_ref, uprev_ref, g_ref, x_ref, wba_ref, wbp_ref, wout_ref, wp_ref, ps_ref, lg_ref, lb_ref,
             ya_ref, yp_ref, pooled_ref, mixed_ref, ypre_ref, merged_ref, xhat_ref, rstd_ref, h_ref, hb_ref, ext):
        i = pl.program_id(0)
        ya = _dot(o_ref[...].astype(BF16), wba_ref[...])
        ucur = u_ref[...]
        ext[0:POOL_HALO, :] = jnp.where(i > 0, uprev_ref[...], 0.0)
        ext[POOL_HALO:, :] = ucur
        for grp, window in enumerate(POOL_WINDOWS):
            cols = slice(POOL_GROUP * grp, POOL_GROUP * (grp + 1))
            acc = ucur[:, cols]
            for kk in range(1, window):
                acc = acc + ext[pl.ds(POOL_HALO - kk, TOK), cols]
            pooled = acc / _pool_count(i * TOK, TOK, window) - ucur[:, cols]
            pooled_ref[:, cols] = pooled.astype(BF16)
            mixed_ref[:, cols] = _dot(pooled.astype(BF16), wp_ref[grp].astype(BF16))
        mixed = mixed_ref[...]
        ypre = (mixed * ps_ref[...]).astype(BF16)
        ypre_ref[...] = ypre
        yp = _dot(ypre, wbp_ref[...])
        ya_ref[...] = ya
        yp_ref[...] = yp
        merged = (g_ref[:, :D_MODEL] * ya + g_ref[:, D_MODEL:] * yp).astype(BF16)
        merged_ref[...] = merged
        r1 = ALPHA * x_ref[...] + _dot(merged, wout_ref[...])
        h, xhat, rstd = _ln_fwd(r1, lg_ref[...], lb_ref[...])
        xhat_ref[...] = xhat
        rstd_ref[...] = jnp.broadcast_to(rstd, (TOK, LANE))
        h_ref[...] = h
        hb_ref[...] = h.astype(BF16)

    halo = pl.BlockSpec((POOL_HALO, D_POOL), lambda i: (jnp.maximum(i * (TOK // POOL_HALO) - 1, 0), 0))
    return _grid_call(
        body, "mix", seq // TOK,
        in_specs=[_rows(D_ATTN), _rows(D_POOL), halo, _rows(2 * D_MODEL), _rows(D_MODEL),
                  _full(wba.shape), _full(wbp.shape), _full(wout.shape), _full(w_pool.shape),
                  _full((1, D_POOL)), _full((1, D_MODEL)), _full((1, D_MODEL))],
        out_specs=[_rows(D_MODEL), _rows(D_MODEL), _rows(D_POOL), _rows(D_POOL), _rows(D_POOL), _rows(D_MODEL),
                   _rows(D_MODEL), _rows(LANE), _rows(D_MODEL), _rows(D_MODEL)],
        out_shape=[_sds((seq, D_MODEL), F32), _sds((seq, D_MODEL), F32), _sds((seq, D_POOL), BF16),
                   _sds((seq, D_POOL), F32), _sds((seq, D_POOL), BF16), _sds((seq, D_MODEL), BF16),
                   _sds((seq, D_MODEL), F32), _sds((seq, LANE), F32), _sds((seq, D_MODEL), F32),
                   _sds((seq, D_MODEL), BF16)],
        operands=(o, u, u, g, x, wba, wbp, wout, w_pool, pool_scale, ln_g, ln_b),
        scratch=[pltpu.VMEM((TOK + POOL_HALO, D_POOL), F32)], tasks=tasks)


def _ffn_up(hb, wgt, wut, conv_w, conv_b, tasks=()):
    seq = hb.shape[0]
    wblk = pl.BlockSpec((FF_CHUNK, D_MODEL), lambda c: (c, 0))
    cblk = lambda rows: pl.BlockSpec((rows, FF_CHUNK), lambda c: (0, c))
    oblk = pl.BlockSpec((seq, FF_CHUNK), lambda c: (0, c))

    def body(h_ref, wg_ref, wu_ref, cw_ref, cb_ref, a_ref, u_ref, act_ref):
        h = h_ref[...]
        a = _dot_nt(h, wg_ref[...])
        u = _dot_nt(h, wu_ref[...])
        a_ref[...] = a
        u_ref[...] = u
        ac = _conv(a, cw_ref[...], cb_ref[...])
        act_ref[...] = (ac * _normal_cdf(ac) * u).astype(BF16)

    return _grid_call(
        body, "ffn_up", D_FF // FF_CHUNK,
        in_specs=[_full(hb.shape), wblk, wblk, cblk(3), cblk(1)],
        out_specs=[oblk, oblk, oblk],
        out_shape=[_sds((seq, D_FF), F32), _sds((seq, D_FF), F32), _sds((seq, D_FF), BF16)],
        operands=(hb, wgt, wut, conv_w, conv_b), tasks=tasks)


def _ffn_down(act, wd, h, target, ln_g, ln_b):
    seq = h.shape[0]

    def body(act_ref, wd_ref, h_ref, t_ref, lg_ref, lb_ref, dr_ref, drb_ref, loss_ref, dg_ref, db_ref):
        i = pl.program_id(0)

        @pl.when(i == 0)
        def _():
            loss_ref[...] = jnp.zeros_like(loss_ref)
            dg_ref[...] = jnp.zeros_like(dg_ref)
            db_ref[...] = jnp.zeros_like(db_ref)

        r2 = ALPHA * h_ref[...] + _dot(act_ref[...], wd_ref[...])
        y, xhat, rstd = _ln_fwd(r2, lg_ref[...], lb_ref[...])
        diff = y - t_ref[...]
        loss_ref[...] += jnp.sum(diff * diff) * (0.5 / D_MODEL)
        dy = diff * (1.0 / D_MODEL)
        dg_ref[...] += jnp.sum(dy * xhat, axis=0, keepdims=True)
        db_ref[...] += jnp.sum(dy, axis=0, keepdims=True)
        dr = _ln_bwd(dy, xhat, rstd, lg_ref[...])
        dr_ref[...] = dr
        drb_ref[...] = dr.astype(BF16)

    vec = pl.BlockSpec((1, D_MODEL), lambda i: (0, 0))
    return pl.pallas_call(
        body, name="ffn_down", grid=(seq // TOK,),
        in_specs=[_rows(D_FF), _full(wd.shape), _rows(D_MODEL), _rows(D_MODEL), _full((1, D_MODEL)), _full((1, D_MODEL))],
        out_specs=[_rows(D_MODEL), _rows(D_MODEL), pl.BlockSpec((8, LANE), lambda i: (0, 0)), vec, vec],
        out_shape=[_sds((seq, D_MODEL), F32), _sds((seq, D_MODEL), BF16), _sds((8, LANE), F32),
                   _sds((1, D_MODEL), F32), _sds((1, D_MODEL), F32)],
        compiler_params=_params("arbitrary"),
    )(act, wd, h, target, ln_g, ln_b)


def _ffn_bwd(drb, hb, a, u, wd, conv_w, conv_b):
    seq = hb.shape[0]
    wblk = pl.BlockSpec((FF_CHUNK, D_MODEL), lambda c: (c, 0))
    cblk = lambda rows: pl.BlockSpec((rows, FF_CHUNK), lambda c: (0, c))
    sblk = pl.BlockSpec((seq, FF_CHUNK), lambda c: (0, c))

    def body(dr_ref, h_ref, a_ref, u_ref, wd_ref, cw_ref, cb_ref, da_ref, du_ref, dwd_ref, dwg_ref, dwu_ref, dc_ref):
        dr = dr_ref[...]
        h = h_ref[...]
        a = a_ref[...]
        u = u_ref[...]
        cw = cw_ref[...]
        dact = _dot_nt(dr, wd_ref[...])
        ac = _conv(a, cw, cb_ref[...])
        cdf = _normal_cdf(ac)
        gelu = ac * cdf
        dwd_ref[...] = _dot_tn((gelu * u).astype(BF16), dr).astype(BF16)
        du = (dact * gelu).astype(BF16)
        dac = dact * u * _gelu_derivative(ac, cdf)
        da = (cw[2:3, :] * dac + cw[1:2, :] * _shift_up(dac, 1) + cw[0:1, :] * _shift_up(dac, 2)).astype(BF16)
        da_ref[...] = da
        du_ref[...] = du
        dwg_ref[...] = _dot_tn(da, h).astype(BF16)
        dwu_ref[...] = _dot_tn(du, h).astype(BF16)
        dc_ref[0:1, :] = jnp.sum(dac * _shift_down(a, 2), axis=0, keepdims=True)
        dc_ref[1:2, :] = jnp.sum(dac * _shift_down(a, 1), axis=0, keepdims=True)
        dc_ref[2:3, :] = jnp.sum(dac * a, axis=0, keepdims=True)
        dc_ref[3:4, :] = jnp.sum(dac, axis=0, keepdims=True)
        dc_ref[4:8, :] = jnp.zeros((4, FF_CHUNK), F32)

    return pl.pallas_call(
        body, name="ffn_bwd", grid=(D_FF // FF_CHUNK,),
        in_specs=[_full(drb.shape), _full(hb.shape), sblk, sblk, wblk, cblk(3), cblk(1)],
        out_specs=[sblk, sblk, wblk, wblk, wblk, cblk(8)],
        out_shape=[_sds((seq, D_FF), BF16), _sds((seq, D_FF), BF16), _sds((D_FF, D_MODEL), BF16),
                   _sds((D_FF, D_MODEL), BF16), _sds((D_FF, D_MODEL), BF16), _sds((8, D_FF), F32)],
        compiler_params=_params("parallel"),
    )(drb, hb, a, u, wd, conv_w, conv_b)


def _ln1_bwd(dr2, da, du, wgt, wut, xhat, rstd, ln_g, tasks=()):
    seq = dr2.shape[0]

    def body(dr2_ref, da_ref, du_ref, wg_ref, wu_ref, xhat_ref, rstd_ref, lg_ref, dr_ref, drb_ref, dg_ref, db_ref):
        @pl.when(pl.program_id(0) == 0)
        def _():
            dg_ref[...] = jnp.zeros_like(dg_ref)
            db_ref[...] = jnp.zeros_like(db_ref)

        dh = ALPHA * dr2_ref[...] + _dot(da_ref[...], wg_ref[...]) + _dot(du_ref[...], wu_ref[...])
        xhat = xhat_ref[...]
        dg_ref[...] += jnp.sum(dh * xhat, axis=0, keepdims=True)
        db_ref[...] += jnp.sum(dh, axis=0, keepdims=True)
        dr = _ln_bwd(dh, xhat, rstd_ref[:, 0:1], lg_ref[...])
        dr_ref[...] = dr
        drb_ref[...] = dr.astype(BF16)

    vec = pl.BlockSpec((1, D_MODEL), lambda i: (0, 0))
    return _grid_call(
        body, "ln1_bwd", seq // TOK,
        in_specs=[_rows(D_MODEL), _rows(D_FF), _rows(D_FF), _full(wgt.shape), _full(wut.shape), _rows(D_MODEL),
                  _rows(LANE), _full((1, D_MODEL))],
        out_specs=[_rows(D_MODEL), _rows(D_MODEL), vec, vec],
        out_shape=[_sds((seq, D_MODEL), F32), _sds((seq, D_MODEL), BF16), _sds((1, D_MODEL), F32),
                   _sds((1, D_MODEL), F32)],
        operands=(dr2, da, du, wgt, wut, xhat, rstd, ln_g), tasks=tasks)


def _mix_bwd(drb, ya, yp, g, mixed, wout, wba, wbp, w_pool, pool_scale, tasks=()):
    seq = drb.shape[0]

    def body(dr_ref, ya_ref, yp_ref, g_ref, mixed_ref, wout_ref, wba_ref, wbp_ref, wp_ref, ps_ref,
             dzg_ref, dya_ref, dyp_ref, do_ref, dmixed_ref, dpooled_ref, dbg_ref, dps_ref):
        @pl.when(pl.program_id(0) == 0)
        def _():
            dbg_ref[...] = jnp.zeros_like(dbg_ref)
            dps_ref[...] = jnp.zeros_like(dps_ref)

        dmerged = _dot_nt(dr_ref[...], wout_ref[...])
        ga, gp = g_ref[:, :D_MODEL], g_ref[:, D_MODEL:]
        dzga = dmerged * ya_ref[...] * ga * (1.0 - ga)
        dzgp = dmerged * yp_ref[...] * gp * (1.0 - gp)
        dzg_ref[:, :D_MODEL] = dzga.astype(BF16)
        dzg_ref[:, D_MODEL:] = dzgp.astype(BF16)
        dbg_ref[:, :D_MODEL] += jnp.sum(dzga, axis=0, keepdims=True)
        dbg_ref[:, D_MODEL:] += jnp.sum(dzgp, axis=0, keepdims=True)
        dya = (dmerged * ga).astype(BF16)
        dyp = (dmerged * gp).astype(BF16)
        dya_ref[...] = dya
        dyp_ref[...] = dyp
        do_ref[...] = _dot_nt(dya, wba_ref[...])
        dypre = _dot_nt(dyp, wbp_ref[...])
        dps_ref[...] += jnp.sum(dypre * mixed_ref[...], axis=0, keepdims=True)
        dmixed = (dypre * ps_ref[...]).astype(BF16)
        dmixed_ref[...] = dmixed
        for grp in range(len(POOL_WINDOWS)):
            cols = slice(POOL_GROUP * grp, POOL_GROUP * (grp + 1))
            dpooled_ref[:, cols] = _dot_nt(dmixed[:, cols], wp_ref[grp].astype(BF16))

    return _grid_call(
        body, "mix_bwd", seq // TOK,
        in_specs=[_rows(D_MODEL), _rows(D_MODEL), _rows(D_MODEL), _rows(2 * D_MODEL), _rows(D_POOL),
                  _full(wout.shape), _full(wba.shape), _full(wbp.shape), _full(w_pool.shape), _full((1, D_POOL))],
        out_specs=[_rows(2 * D_MODEL), _rows(D_MODEL), _rows(D_MODEL), _rows(D_ATTN), _rows(D_POOL), _rows(D_POOL),
                   pl.BlockSpec((1, 2 * D_MODEL), lambda i: (0, 0)), pl.BlockSpec((1, D_POOL), lambda i: (0, 0))],
        out_shape=[_sds((seq, 2 * D_MODEL), BF16), _sds((seq, D_MODEL), BF16), _sds((seq, D_MODEL), BF16),
                   _sds((seq, D_ATTN), F32), _sds((seq, D_POOL), BF16), _sds((seq, D_POOL), F32),
                   _sds((1, 2 * D_MODEL), F32), _sds((1, D_POOL), F32)],
        operands=(drb, ya, yp, g, mixed, wout, wba, wbp, w_pool, pool_scale), tasks=tasks)


def _attn_bwd(q, k, v, bias, o, lse, do, cos, sin, tasks=()):
    seq = q.shape[0]
    nb = seq // MOBA_BLOCK
    pair = pl.BlockSpec((seq, LANE), lambda p: (0, p))
    table = pl.BlockSpec((seq, LANE), lambda p: (0, 0))
    scale = HEAD_DIM ** -0.5

    def body(q_ref, k_ref, v_ref, bias_ref, o_ref, lse_ref, do_ref, cos_ref, sin_ref, dq_ref, dk_ref, dv_ref,
             dq_acc, dk_acc, dv_acc, dk_head, dv_head, ka_sc, qa_sc, s_sc, dp_sc, p_sc, ds_sc):
        for hh in range(LANE // HEAD_DIM):
            ls = slice(HEAD_DIM * hh, HEAD_DIM * (hh + 1))
            _store_keys(ka_sc, k_ref, ls)
            vb = v_ref[:, ls]
            dk_head[...] = jnp.zeros_like(dk_head)
            dv_head[...] = jnp.zeros_like(dv_head)
            for i in range(nb):
                rs = slice(MOBA_BLOCK * i, MOBA_BLOCK * (i + 1))
                width = MOBA_BLOCK * (i + 1)
                qa_sc[:, 0:HEAD_DIM] = (q_ref[rs, ls] * scale).astype(BF16)
                qa_sc[:, HEAD_DIM:] = bias_ref[rs, ls]
                s_sc[0:width, :] = _dot_nt(ka_sc[0:width, :], qa_sc[...])
                s_sc[rs, :] = jnp.where(_causal((MOBA_BLOCK, MOBA_BLOCK), transposed=True), s_sc[rs, :], NEG)
                dob = do_ref[rs, ls]
                delta = _row_vector(jnp.sum(dob * o_ref[rs, ls], axis=1, keepdims=True))
                lse_row = lse_ref[0, hh:hh + 1, rs]
                dob16 = dob.astype(BF16)
                dp_sc[0:width, :] = _dot_nt(vb[0:width], dob16)
                for c in range(width // SCORE_CHUNK):
                    rows = slice(SCORE_CHUNK * c, SCORE_CHUNK * (c + 1))
                    p = jnp.exp(s_sc[rows, :] - lse_row)
                    p_sc[rows, :] = p.astype(BF16)
                    ds_sc[rows, :] = (p * (dp_sc[rows, :] - delta)).astype(BF16)
                dv_head[0:width, :] += _dot(p_sc[0:width, :], dob16)
                dk_head[0:width, :] += _dot(ds_sc[0:width, :], qa_sc[:, 0:HEAD_DIM])
                dq_acc[rs, ls] = _dot_tn(ds_sc[0:width, :], ka_sc[0:width, 0:HEAD_DIM]) * scale
            dk_acc[:, ls] = dk_head[...]
            dv_acc[:, ls] = dv_head[...]
        cos_t, sin_t = cos_ref[...], sin_ref[...]
        dq_ref[...] = _rope_transposed(dq_acc[...], cos_t, sin_t).astype(BF16)
        dk_ref[...] = _rope_transposed(dk_acc[...], cos_t, sin_t).astype(BF16)
        dv_ref[...] = dv_acc[...].astype(BF16)

    return _grid_call(
        body, "attn_bwd", D_ATTN // LANE,
        in_specs=[pair, pair, pair, pair, pair, pl.BlockSpec((1, 8, seq), lambda p: (p, 0, 0)), pair, table, table],
        out_specs=[pair, pair, pair], out_shape=[_sds((seq, D_ATTN), BF16)] * 3,
        operands=(q, k, v, bias, o, lse, do, cos, sin),
        scratch=[pltpu.VMEM((seq, LANE), F32)] * 3 + [pltpu.VMEM((seq, HEAD_DIM), F32)] * 2
        + [pltpu.VMEM((seq, LANE), BF16), pltpu.VMEM((MOBA_BLOCK, LANE), BF16)]
        + [pltpu.VMEM((seq, MOBA_BLOCK), F32)] * 2 + [pltpu.VMEM((seq, MOBA_BLOCK), BF16)] * 2,
        tasks=tasks)


def _in_bwd(dq, dk, dv, dpooled, dzg, dr1, win, tasks=()):
    seq = dr1.shape[0]
    nt = seq // TOK

    def body(dq_ref, dk_ref, dv_ref, dp_ref, dpnext_ref, dzg_ref, dr_ref, win_ref, dx_ref, dz_ref, ext):
        i = pl.program_id(0)
        dp = dp_ref[...]
        dpn = jnp.where(i < nt - 1, dpnext_ref[...], 0.0)
        for grp, window in enumerate(POOL_WINDOWS):
            cols = slice(POOL_GROUP * grp, POOL_GROUP * (grp + 1))
            ext[0:TOK, cols] = dp[:, cols] / _pool_count(i * TOK, TOK, window)
            ext[TOK:, cols] = dpn[:, cols] / _pool_count((i + 1) * TOK, POOL_HALO, window)
        for grp, window in enumerate(POOL_WINDOWS):
            cols = slice(POOL_GROUP * grp, POOL_GROUP * (grp + 1))
            acc = ext[0:TOK, cols] - dp[:, cols]
            for kk in range(1, window):
                acc = acc + ext[pl.ds(kk, TOK), cols]
            dz_ref[:, 3 * D_ATTN + POOL_GROUP * grp:3 * D_ATTN + POOL_GROUP * (grp + 1)] = acc.astype(BF16)
        dz_ref[:, 0:D_ATTN] = dq_ref[...]
        dz_ref[:, D_ATTN:2 * D_ATTN] = dk_ref[...]
        dz_ref[:, 2 * D_ATTN:3 * D_ATTN] = dv_ref[...]
        dz_ref[:, 3 * D_ATTN + D_POOL:] = dzg_ref[...]
        dx = ALPHA * dr_ref[...]
        for n in range(N_DEV):
            dx = dx + _dot_nt(dz_ref[:, D_ATTN * n:D_ATTN * (n + 1)], win_ref[n])
        dx_ref[...] = dx

    halo = pl.BlockSpec((POOL_HALO, D_POOL),
                        lambda i: (jnp.minimum((i + 1) * (TOK // POOL_HALO), seq // POOL_HALO - 1), 0))
    return _grid_call(
        body, "in_bwd", nt,
        in_specs=[_rows(D_ATTN), _rows(D_ATTN), _rows(D_ATTN), _rows(D_POOL), halo, _rows(2 * D_MODEL),
                  _rows(D_MODEL), _full(win.shape)],
        out_specs=[_rows(D_MODEL), _rows(D_IN_PROJ)],
        out_shape=[_sds((seq, D_MODEL), F32), _sds((seq, D_IN_PROJ), BF16)],
        operands=(dq, dk, dv, dpooled, dpooled, dzg, dr1, win),
        scratch=[pltpu.VMEM((TOK + POOL_HALO, D_POOL), F32)], tasks=tasks)


def _dw_mixers(o, ypre, merged, dya, dyp, drb, pooled, dmixed, tasks=()):
    seq = o.shape[0]
    groups = len(POOL_WINDOWS)
    col = pl.BlockSpec((seq, LANE), lambda n: (0, n))
    grp = pl.BlockSpec((seq, POOL_GROUP), lambda n: (0, jnp.minimum(n, groups - 1)))
    owner = lambda rows, cols: pl.BlockSpec((1, rows, cols), lambda n: (n, 0, 0))

    def body(o_ref, ypre_ref, merged_ref, dya_ref, dyp_ref, dr_ref, pooled_ref, dmixed_ref,
             dba_ref, dbp_ref, dout_ref, dpool_ref, ob_sc):
        n = pl.program_id(0)

        @pl.when(n == 0)
        def _():
            ob_sc[...] = o_ref[...].astype(BF16)

        dba_ref[0] = _dot_tn(dya_ref[...], ob_sc[...]).T.astype(BF16)
        dbp_ref[0] = _dot_tn(dyp_ref[...], ypre_ref[...]).T.astype(BF16)
        dout_ref[0] = _dot_tn(merged_ref[...], dr_ref[...]).astype(BF16)

        @pl.when(n < groups)
        def _():
            dpool_ref[0] = _dot_tn(pooled_ref[...], dmixed_ref[...])

    return _grid_call(
        body, "dw_mixers", N_DEV,
        in_specs=[_full(o.shape), _full(ypre.shape), col, col, col, _full(drb.shape), grp, grp],
        out_specs=[owner(D_ATTN, LANE), owner(D_POOL, LANE), owner(D_MODEL // N_DEV, D_MODEL),
                   pl.BlockSpec((1, POOL_GROUP, POOL_GROUP), lambda n: (jnp.minimum(n, groups - 1), 0, 0))],
        out_shape=[_sds((N_DEV, D_ATTN, LANE), BF16), _sds((N_DEV, D_POOL, LANE), BF16),
                   _sds((N_DEV, D_MODEL // N_DEV, D_MODEL), BF16), _sds((groups, POOL_GROUP, POOL_GROUP), F32)],
        operands=(o, ypre, merged, dya, dyp, drb, pooled, dmixed),
        scratch=[pltpu.VMEM((seq, D_ATTN), BF16)], tasks=tasks)


def _to_bf16(arrays, tasks=()):
    n = len(arrays)

    def body(*refs):
        for src, dst in zip(refs[:n], refs[n:]):
            dst[...] = src[...].astype(BF16)

    return _grid_call(
        body, "to_bf16", 1, in_specs=[_full(a.shape) for a in arrays],
        out_specs=[pl.BlockSpec(a.shape, lambda i: (0, 0)) for a in arrays],
        out_shape=[_sds(a.shape, BF16) for a in arrays], operands=arrays, tasks=tasks)


def _matmul(name, a, b, out_shape, out_dtype, steps, a_spec, b_spec, o_spec, tasks=()):
    def body(a_ref, b_ref, o_ref):
        o_ref[...] = _dot(a_ref[...], b_ref[...]).reshape(o_ref.shape).astype(o_ref.dtype)

    (out,), results = _grid_call(body, name, steps, in_specs=[a_spec, b_spec], out_specs=[o_spec],
                                 out_shape=[_sds(out_shape, out_dtype)], operands=(a, b), tasks=tasks)
    return out, results


def _place():
    return lax.axis_index("x"), lax.axis_index("y"), lax.axis_index("c")


def _other_chips(x, y):
    return [(1 - x, y), (x, 1 - y), (1 - x, 1 - y)]


DMA_SEMS = pltpu.SemaphoreType.DMA


class _AllGather:
    def __init__(self, shards, lag=0):
        self.operands = list(shards)
        self.n = len(shards)
        self.lag = lag
        self.out_shape = [_sds((N_DEV, *s.shape), s.dtype) for s in shards]
        self.sems = [DMA_SEMS((7 * self.n,)), DMA_SEMS((7 * self.n,)), DMA_SEMS((self.n,))]

    def _copy(self, refs, a, k, block, to, from_input=False):
        ins, outs, (send_sems, recv_sems, _) = refs
        px, py, pc = block
        dst = outs[a].at[4 * px + 2 * py + pc]
        return pltpu.make_async_remote_copy(
            src_ref=ins[a] if from_input else dst, dst_ref=dst,
            send_sem=send_sems.at[7 * a + k], recv_sem=recv_sems.at[7 * a + k],
            device_id=to, device_id_type=MESH)

    def _local(self, refs, a):
        ins, outs, (_, _, local_sems) = refs
        x, y, c = _place()
        return pltpu.make_async_copy(ins[a], outs[a].at[4 * x + 2 * y + c], local_sems.at[a])

    def _pass_on(self, refs, a):
        x, y, c = _place()
        origin = ((x + 1 - c) % 2, (y + c) % 2, c)
        target = ((x + c) % 2, (y + 1 - c) % 2, c)
        return self._copy(refs, a, 3, origin, target)

    def start(self, refs):
        x, y, c = _place()
        for a in range(self.n):
            self._local(refs, a).start()
        for a in range(self.n):
            self._copy(refs, a, 0, (x, y, c), (x, y, 1 - c), True).start()
            for j, chip in enumerate(_other_chips(x, y)[:2]):
                self._copy(refs, a, 1 + j, (x, y, c), (*chip, c), True).start()

    def middle(self, refs):
        x, y, c = _place()
        me, sibling = (x, y, c), (x, y, 1 - c)
        chips = _other_chips(x, y)
        for a in range(self.n):
            for j in range(2):
                self._copy(refs, a, 1 + j, (*chips[j], c), me).wait_recv()
        for a in range(self.n):
            self._pass_on(refs, a).start()
            for j in range(2):
                self._copy(refs, a, 4 + j, (*chips[j], c), sibling).start()

    def late(self, refs):
        x, y, c = _place()
        diagonal = (1 - x, 1 - y, c)
        for a in range(self.n):
            self._copy(refs, a, 3, diagonal, (x, y, c)).wait_recv()
            self._copy(refs, a, 6, diagonal, (x, y, 1 - c)).start()

    def finish(self, refs):
        x, y, c = _place()
        me, sibling = (x, y, c), (x, y, 1 - c)
        chips = _other_chips(x, y)
        for a in range(self.n):
            self._copy(refs, a, 0, sibling, me).wait_recv()
            for j, chip in enumerate(chips):
                self._copy(refs, a, 4 + j, (*chip, 1 - c), me).wait_recv()
        for a in range(self.n):
            self._copy(refs, a, 0, me, sibling, True).wait_send()
            for j, chip in enumerate(chips[:2]):
                self._copy(refs, a, 1 + j, me, (*chip, c), True).wait_send()
            self._pass_on(refs, a).wait_send()
            for j, chip in enumerate(chips):
                self._copy(refs, a, 4 + j, (*chip, c), sibling).wait_send()
            self._local(refs, a).wait()


class _SiblingSend:
    def __init__(self, partials):
        self.operands = list(partials)
        self.n = len(partials)
        self.out_shape = [_sds((4, *p.shape[1:]), p.dtype) for p in partials]
        self.sems = [DMA_SEMS((4 * self.n,)), DMA_SEMS((4 * self.n,))]

    def _copy(self, refs, a, q):
        ins, outs, (send_sems, recv_sems) = refs
        x, y, c = _place()
        return pltpu.make_async_remote_copy(
            src_ref=ins[a].at[2 * q + 1 - c], dst_ref=outs[a].at[q],
            send_sem=send_sems.at[4 * a + q], recv_sem=recv_sems.at[4 * a + q],
            device_id=(x, y, 1 - c), device_id_type=MESH)

    def start(self, refs):
        for a in range(self.n):
            for q in range(4):
                self._copy(refs, a, q).start()

    def middle(self, refs):
        pass

    def finish(self, refs):
        for a in range(self.n):
            for q in range(4):
                self._copy(refs, a, q).wait()


class _ChipScatter:
    def __init__(self, chip_partials):
        self.operands = list(chip_partials)
        self.n = len(chip_partials)
        self.out_shape = [_sds(p.shape, p.dtype) for p in chip_partials]
        self.sems = [DMA_SEMS((3 * self.n,)), DMA_SEMS((3 * self.n,)), DMA_SEMS((self.n,))]

    def _copy(self, refs, a, k, arrival=False):
        ins, outs, (send_sems, recv_sems, _) = refs
        x, y, c = _place()
        px, py = _other_chips(x, y)[k]
        mine, theirs = 2 * x + y, 2 * px + py
        return pltpu.make_async_remote_copy(
            src_ref=ins[a].at[mine if arrival else theirs], dst_ref=outs[a].at[theirs if arrival else mine],
            send_sem=send_sems.at[3 * a + k], recv_sem=recv_sems.at[3 * a + k],
            device_id=(px, py, c), device_id_type=MESH)

    def _local(self, refs, a):
        ins, outs, (_, _, local_sems) = refs
        x, y, _ = _place()
        return pltpu.make_async_copy(ins[a].at[2 * x + y], outs[a].at[2 * x + y], local_sems.at[a])

    def start(self, refs):
        for a in range(self.n):
            self._local(refs, a).start()
            for k in range(3):
                self._copy(refs, a, k).start()

    def middle(self, refs):
        pass

    def finish(self, refs):
        for a in range(self.n):
            for k in range(3):
                self._copy(refs, a, k, arrival=True).wait_recv()
        for a in range(self.n):
            for k in range(3):
                self._copy(refs, a, k).wait_send()
            self._local(refs, a).wait()


class _DirectScatter:
    def __init__(self, partials):
        self.operands = list(partials)
        self.n = len(partials)
        self.out_shape = [_sds(p.shape, p.dtype) for p in partials]
        self.sems = [DMA_SEMS((7 * self.n,)), DMA_SEMS((7 * self.n,)), DMA_SEMS((self.n,))]

    def _copy(self, refs, a, k, arrival=False):
        ins, outs, (send_sems, recv_sems, _) = refs
        x, y, c = _place()
        peer = [(x, y, 1 - c), (1 - x, y, c), (x, 1 - y, c), (1 - x, 1 - y, c),
                (1 - x, y, 1 - c), (x, 1 - y, 1 - c), (1 - x, 1 - y, 1 - c)][k]
        mine, theirs = 4 * x + 2 * y + c, 4 * peer[0] + 2 * peer[1] + peer[2]
        return pltpu.make_async_remote_copy(
            src_ref=ins[a].at[mine if arrival else theirs], dst_ref=outs[a].at[theirs if arrival else mine],
            send_sem=send_sems.at[7 * a + k], recv_sem=recv_sems.at[7 * a + k],
            device_id=peer, device_id_type=MESH)

    def _local(self, refs, a):
        ins, outs, (_, _, local_sems) = refs
        x, y, c = _place()
        return pltpu.make_async_copy(ins[a].at[4 * x + 2 * y + c], outs[a].at[4 * x + 2 * y + c], local_sems.at[a])

    def start(self, refs):
        for a in range(self.n):
            self._local(refs, a).start()
            for k in range(7):
                self._copy(refs, a, k).start()

    def middle(self, refs):
        pass

    def finish(self, refs):
        for a in range(self.n):
            for k in range(7):
                self._copy(refs, a, k, arrival=True).wait_recv()
        for a in range(self.n):
            for k in range(7):
                self._copy(refs, a, k).wait_send()
            self._local(refs, a).wait()


def _task_args(tasks):
    hbm = pl.BlockSpec(memory_space=pl.ANY)
    operands = [o for t in tasks for o in t.operands]
    out_shape = [s for t in tasks for s in t.out_shape]
    sems = [s for t in tasks for s in t.sems]
    return operands, [hbm] * len(operands), out_shape, [hbm] * len(out_shape), sems


def _task_refs(tasks, ins, outs, sems):
    per_task = []
    for t in tasks:
        ni, no, ns = len(t.operands), len(t.out_shape), len(t.sems)
        per_task.append((ins[:ni], outs[:no], sems[:ns]))
        ins, outs, sems = ins[ni:], outs[no:], sems[ns:]
    return per_task


def _task_results(tasks, outs):
    res = []
    for t in tasks:
        res.append(list(outs[:len(t.out_shape)]))
        outs = outs[len(t.out_shape):]
    return res


def _carry(body, tasks, n_in, n_out, n_scratch, steps):
    if not tasks:
        return body
    t_in = sum(len(t.operands) for t in tasks)
    t_out = sum(len(t.out_shape) for t in tasks)

    def wrapped(*refs):
        ins, refs = refs[:n_in], refs[n_in:]
        t_ins, refs = refs[:t_in], refs[t_in:]
        outs, refs = refs[:n_out], refs[n_out:]
        t_outs, refs = refs[:t_out], refs[t_out:]
        scratch, t_sems = refs[:n_scratch], refs[n_scratch:]
        per_task = _task_refs(tasks, t_ins, t_outs, t_sems)
        step = pl.program_id(0)

        @pl.when(step == 0)
        def _():
            for t, r in zip(tasks, per_task):
                t.start(r)

        for t, r in zip(tasks, per_task):
            pl.when(step == max(steps - 1 - getattr(t, "lag", 0), 0))(functools.partial(t.middle, r))
            if hasattr(t, "late"):
                pl.when(step == steps - 1)(functools.partial(t.late, r))

        body(*ins, *outs, *scratch)

        @pl.when(step == steps - 1)
        def _():
            for t, r in zip(tasks, per_task):
                t.finish(r)

    return wrapped


def _exchange(name, tasks):
    operands, in_specs, out_shape, out_specs, sems = _task_args(tasks)

    def body(*refs):
        ni, no = len(operands), len(out_shape)
        per_task = _task_refs(tasks, refs[:ni], refs[ni:ni + no], refs[ni + no:])
        for phase in ("start", "middle", "late", "finish"):
            for t, r in zip(tasks, per_task):
                if hasattr(t, phase):
                    getattr(t, phase)(r)

    outs = pl.pallas_call(body, name=name, in_specs=in_specs, out_specs=out_specs, out_shape=out_shape,
                          scratch_shapes=sems)(*operands)
    return _task_results(tasks, outs)


def _row_tile(rows, cols, whole_up_to=256 * 1024):
    if rows * cols <= whole_up_to:
        return rows
    for t in (256, 176, 128, 64, 32, 16, 8):
        if rows % t == 0:
            return t
    return rows


def _pair_sum(name, partials, from_sibling):
    n = len(partials)
    _, rows, cols = partials[0].shape
    tile = _row_tile(rows, cols, 512 * 1024)

    def body(*refs):
        south = lax.axis_index("c") == 0
        for p_ref, s_ref, o_ref in zip(refs[:n], refs[n:2 * n], refs[2 * n:]):
            mine = jnp.where(south, p_ref[0, 0].astype(F32), p_ref[0, 1].astype(F32))
            o_ref[0] = (mine + s_ref[0].astype(F32)).astype(o_ref.dtype)

    blk = pl.BlockSpec((1, tile, cols), lambda q, i: (q, i, 0))
    return pl.pallas_call(
        body, name=name, grid=(4, rows // tile),
        in_specs=[pl.BlockSpec((1, 2, tile, cols), lambda q, i: (q, 0, i, 0))] * n + [blk] * n,
        out_specs=[blk] * n, out_shape=[_sds(s.shape, s.dtype) for s in from_sibling],
        compiler_params=_params("parallel", "parallel"),
    )(*[p.reshape(4, 2, rows, cols) for p in partials], *from_sibling)


def _adamw_math(w, g, m, v):
    nm = ADAM_B1 * m + (1.0 - ADAM_B1) * g
    nv = ADAM_B2 * v + (1.0 - ADAM_B2) * (g * g)
    m_hat = nm / (1.0 - ADAM_B1 ** ADAM_STEP)
    v_hat = nv / (1.0 - ADAM_B2 ** ADAM_STEP)
    return -ADAM_LR * (m_hat / (jnp.sqrt(v_hat) + ADAM_EPS) + ADAM_WD * w), nm, nv


def _update_small(params, loss_parts):
    n = len(params)

    def whole(shape):
        return pl.BlockSpec(shape, lambda i, rank=len(shape): (0,) * rank)

    def total(ref):
        acc = ref[0]
        for d in range(1, N_DEV):
            acc = acc + ref[d]
        return acc

    def body(*refs):
        ins, loss_ref, outs, loss_out = refs[:4 * n], refs[4 * n], refs[4 * n + 1:8 * n + 1], refs[8 * n + 1]
        for p in range(n):
            s_ref, w_ref, m_ref, v_ref = ins[4 * p:4 * p + 4]
            g_ref, d_ref, nm_ref, nv_ref = outs[4 * p:4 * p + 4]
            g = total(s_ref)
            g_ref[...] = g
            d_ref[...], nm_ref[...], nv_ref[...] = _adamw_math(w_ref[...], g, m_ref[...], v_ref[...])
        loss_out[...] = total(loss_ref)

    outs = pl.pallas_call(
        body, name="update_small", grid=(1,),
        in_specs=[whole(t.shape) for p in params for t in p] + [whole(loss_parts.shape)],
        out_specs=[whole(p[1].shape) for p in params for _ in range(4)] + [whole(loss_parts.shape[1:])],
        out_shape=[_sds(p[1].shape, F32) for p in params for _ in range(4)] + [_sds(loss_parts.shape[1:], F32)],
        compiler_params=_params("arbitrary"),
    )(*[t for p in params for t in p], loss_parts)
    return [outs[4 * p:4 * p + 4] for p in range(n)], outs[4 * n]


def _sum_adamw(name, params, tasks=()):
    n = len(params)
    parts, rows, cols = params[0][0].shape
    tile = _row_tile(rows, cols)

    def body(*refs):
        ins, outs = refs[:4 * n], refs[4 * n:]
        for p in range(n):
            s_ref, w_ref, m_ref, v_ref = ins[4 * p:4 * p + 4]
            g_ref, d_ref, nm_ref, nv_ref = outs[4 * p:4 * p + 4]
            g = s_ref[0].astype(F32)
            for d in range(1, parts):
                g = g + s_ref[d].astype(F32)
            g_ref[...] = g
            d_ref[...], nm_ref[...], nv_ref[...] = _adamw_math(w_ref[...], g, m_ref[...], v_ref[...])

    blk = pl.BlockSpec((tile, cols), lambda i: (i, 0))
    outs, results = _grid_call(
        body, name, rows // tile,
        in_specs=([pl.BlockSpec((parts, tile, cols), lambda i: (0, i, 0))] + [blk] * 3) * n, out_specs=[blk] * (4 * n),
        out_shape=[_sds((rows, cols), F32)] * (4 * n),
        operands=[t for p in params for t in p], tasks=tasks)
    return [outs[4 * p:4 * p + 4] for p in range(n)], results


SMALL = ("b_gate", "w_pool", "pool_scale", "ln1_g", "ln1_b", "conv_b", "ln2_g", "ln2_b")
MIXER = ("w_branch_attn", "w_branch_pool", "w_out", "conv_w")
FFN = ("w_ffn_gate_t", "w_ffn_up_t", "w_ffn_down")


def _columns(t):
    return jnp.transpose(t, (1, 0, 2)).reshape(t.shape[1], N_DEV * t.shape[2])


def _row_blocks(t):
    return t.reshape(N_DEV * t.shape[1], t.shape[2])


def _by_owner(t):
    return t.reshape(N_DEV, t.shape[0] // N_DEV, t.shape[1])


def _reduce_halves(names, partials, from_sibling):
    out = [None] * len(names)
    for shape in dict.fromkeys(p.shape for p in partials):
        group = [i for i, p in enumerate(partials) if p.shape == shape]
        sums = _pair_sum("pair_sum_" + names[group[0]], [partials[i] for i in group], [from_sibling[i] for i in group])
        for i, s in zip(group, sums):
            out[i] = s
    return out


def _local_step(x, target, shards, small):
    seq = x.shape[0]
    cos, sin = _rope_tables(seq)
    cast, ((w_in_all,),) = _to_bf16([shards[n] for n in MIXER[:3] + FFN], tasks=[_AllGather([shards["w_in"]])])
    shards = {**shards, **dict(zip(MIXER[:3] + FFN, cast))}
    (xt, q, k, v, u, g, kmean), (mixer,) = _proj_in(
        x, w_in_all, small["b_gate"], cos, sin, tasks=[_AllGather([shards[n] for n in MIXER], lag=2)])
    wba, wbp, wout, conv_w = _columns(mixer[0]), _columns(mixer[1]), _row_blocks(mixer[2]), _columns(mixer[3])
    (o, lse, bias), ((wgt, wut),) = _attn_fwd(
        q, k, v, kmean.reshape(seq // MOBA_BLOCK, D_ATTN),
        tasks=[_AllGather([shards["w_ffn_gate_t"], shards["w_ffn_up_t"]], lag=1)])
    (ya, yp, pooled, mixed, ypre, merged, xhat1, rstd1, h1, h1b), _ = _mix(
        o, u, g, x, wba, wbp, wout, small["w_pool"], small["pool_scale"], small["ln1_g"], small["ln1_b"])
    wgt, wut = _row_blocks(wgt), _row_blocks(wut)
    (a, uf, act), ((wd,),) = _ffn_up(
        h1b, wgt, wut, conv_w, small["conv_b"], tasks=[_AllGather([shards["w_ffn_down"]], lag=4)])
    wd = _row_blocks(wd)
    dr2, dr2b, loss, dg2, db2 = _ffn_down(act, wd, h1, target, small["ln2_g"], small["ln2_b"])

    da, du, dwd, dwg, dwu, dconv = _ffn_bwd(dr2b, h1b, a, uf, wd, conv_w, small["conv_b"])
    ffn_partials = [_by_owner(dwg), _by_owner(dwu), _by_owner(dwd)]
    (dr1, dr1b, dg1, db1), (ffn_sibling,) = _ln1_bwd(
        dr2, da, du, wgt, wut, xhat1, rstd1, small["ln1_g"], tasks=[_SiblingSend(ffn_partials)])
    ffn_chip = _reduce_halves(FFN, ffn_partials, ffn_sibling)
    (dzg, dya, dyp, do, dmixed, dpooled, dbg, dps), (gate_landed,) = _mix_bwd(
        dr1b, ya, yp, g, mixed, wout, wba, wbp, small["w_pool"], small["pool_scale"],
        tasks=[_ChipScatter(ffn_chip[0:1])])
    (dw_ba, dw_bp, dw_out, dw_pool), _ = _dw_mixers(o, ypre, merged, dya, dyp, dr1b, pooled, dmixed)
    mixer_partials = [dw_ba, dw_bp, dw_out]
    (dq, dk, dv), (up_down_landed, mixer_sibling) = _attn_bwd(
        q, k, v, bias, o, lse, do, cos, sin, tasks=[_ChipScatter(ffn_chip[1:3]), _SiblingSend(mixer_partials)])
    mixer_chip = _reduce_halves(MIXER[:3], mixer_partials, mixer_sibling)
    (grad_x, dz), _ = _in_bwd(dq, dk, dv, dpooled, dzg, dr1, w_in_all)
    little = [dbg, dw_pool, dps, dg1, db1, dconv[3:4], dg2, db2]
    conv_w_partials = dconv[0:3].reshape(3, N_DEV, FF_SHARD).transpose(1, 0, 2)
    dw_in, (mixer_landed,) = _matmul(
        "dw_in", xt, dz, (N_DEV, D_MODEL, D_ATTN), BF16, N_DEV,
        _full(xt.shape), pl.BlockSpec((seq, D_ATTN), lambda n: (0, n)),
        pl.BlockSpec((1, D_MODEL, D_ATTN), lambda n: (n, 0, 0)), tasks=[_ChipScatter(mixer_chip)])

    landed = dict(zip(FFN + MIXER[:3], gate_landed + up_down_landed + mixer_landed))
    return grad_x, landed, dw_in, little, conv_w_partials, loss


def kernel(x, w_in, b_gate, w_branch_attn, w_pool, pool_scale, w_branch_pool, w_out, ln1_g, ln1_b, w_ffn_gate, w_ffn_up, conv_w, conv_b, w_ffn_down, ln2_g, ln2_b, loss_target, m_w_in, m_b_gate, m_w_branch_attn, m_w_pool, m_pool_scale, m_w_branch_pool, m_w_out, m_ln1_g, m_ln1_b, m_w_ffn_gate, m_w_ffn_up, m_conv_w, m_conv_b, m_w_ffn_down, m_ln2_g, m_ln2_b, v_w_in, v_b_gate, v_w_branch_attn, v_w_pool, v_pool_scale, v_w_branch_pool, v_w_out, v_ln1_g, v_ln1_b, v_w_ffn_gate, v_w_ffn_up, v_conv_w, v_conv_b, v_w_ffn_down, v_ln2_g, v_ln2_b):
    weights = dict(w_in=w_in, b_gate=b_gate, w_branch_attn=w_branch_attn, w_pool=w_pool, pool_scale=pool_scale,
                   w_branch_pool=w_branch_pool, w_out=w_out, ln1_g=ln1_g, ln1_b=ln1_b, w_ffn_gate=w_ffn_gate,
                   w_ffn_up=w_ffn_up, conv_w=conv_w, conv_b=conv_b, w_ffn_down=w_ffn_down, ln2_g=ln2_g, ln2_b=ln2_b)
    m_in = dict(w_in=m_w_in, b_gate=m_b_gate, w_branch_attn=m_w_branch_attn, w_pool=m_w_pool,
                pool_scale=m_pool_scale, w_branch_pool=m_w_branch_pool, w_out=m_w_out, ln1_g=m_ln1_g, ln1_b=m_ln1_b,
                w_ffn_gate=m_w_ffn_gate, w_ffn_up=m_w_ffn_up, conv_w=m_conv_w, conv_b=m_conv_b,
                w_ffn_down=m_w_ffn_down, ln2_g=m_ln2_g, ln2_b=m_ln2_b)
    v_in = dict(w_in=v_w_in, b_gate=v_b_gate, w_branch_attn=v_w_branch_attn, w_pool=v_w_pool,
                pool_scale=v_pool_scale, w_branch_pool=v_w_branch_pool, w_out=v_w_out, ln1_g=v_ln1_g, ln1_b=v_ln1_b,
                w_ffn_gate=v_w_ffn_gate, w_ffn_up=v_w_ffn_up, conv_w=v_conv_w, conv_b=v_conv_b,
                w_ffn_down=v_w_ffn_down, ln2_g=v_ln2_g, ln2_b=v_ln2_b)
    weights = {n: a[0] for n, a in weights.items()}
    m_in = {n: a[0] for n, a in m_in.items()}
    v_in = {n: a[0] for n, a in v_in.items()}

    shards = {"w_in": weights["w_in"].astype(BF16), "w_branch_attn": weights["w_branch_attn"],
              "w_branch_pool": weights["w_branch_pool"], "w_out": weights["w_out"],
              "w_ffn_gate_t": weights["w_ffn_gate"].T, "w_ffn_up_t": weights["w_ffn_up"].T,
              "w_ffn_down": weights["w_ffn_down"], "conv_w": weights["conv_w"]}
    small = {"b_gate": weights["b_gate"][None], "w_pool": weights["w_pool"], "pool_scale": weights["pool_scale"][None],
             "ln1_g": weights["ln1_g"][None], "ln1_b": weights["ln1_b"][None], "conv_b": weights["conv_b"][None],
             "ln2_g": weights["ln2_g"][None], "ln2_b": weights["ln2_b"][None]}

    grad_x, landed, dw_in, little, conv_w_partials, loss_part = _local_step(x[0], loss_target[0], shards, small)

    grads, delta, new_m, new_v = {}, {}, {}, {}

    def param(n, transposed=False):
        if transposed:
            return landed[n + "_t"], weights[n].T, m_in[n].T, v_in[n].T
        return landed[n], weights[n], m_in[n], v_in[n]

    def keep(n, updated, transposed=False):
        grads[n], delta[n], new_m[n], new_v[n] = (t.T for t in updated) if transposed else updated

    ((w_in_sibling,),) = _exchange("sibling_grads", [_SiblingSend([dw_in])])
    w_in_chip = _reduce_halves(["w_in"], [dw_in], [w_in_sibling])
    (landed["w_in"],), (*small_all, loss_all), (conv_w_all,) = _exchange(
        "scatter_grads",
        [_ChipScatter(w_in_chip), _AllGather(little + [loss_part]), _DirectScatter([conv_w_partials])])
    (gate, up, down), _ = _sum_adamw(
        "update_w_ffn", [param("w_ffn_gate", True), param("w_ffn_up", True), param("w_ffn_down")])
    keep("w_ffn_gate", gate, True)
    keep("w_ffn_up", up, True)
    keep("w_ffn_down", down)
    (attn, pool), _ = _sum_adamw("update_w_branch", [param("w_branch_attn"), param("w_branch_pool")])
    keep("w_branch_attn", attn)
    keep("w_branch_pool", pool)
    for n in ("w_out", "w_in"):
        (updated,), _ = _sum_adamw("update_" + n, [param(n)])
        keep(n, updated)
    names = SMALL + ("conv_w",)
    rows = lambda t: t if t.ndim > 1 else t[None]
    small_updates, loss = _update_small(
        [(s, rows(weights[n]), rows(m_in[n]), rows(v_in[n])) for n, s in zip(names, small_all + [conv_w_all])],
        loss_all)
    for n, updated in zip(names, small_updates):
        keep(n, [t.reshape(weights[n].shape) for t in updated])
    loss = loss[0, 0]

    order = ("w_in", "b_gate", "w_branch_attn", "w_pool", "pool_scale", "w_branch_pool", "w_out", "ln1_g", "ln1_b",
             "w_ffn_gate", "w_ffn_up", "conv_w", "conv_b", "w_ffn_down", "ln2_g", "ln2_b")
    lead = lambda t: t[None]
    return (loss, lead(grad_x), *[lead(grads[n]) for n in order], *[lead(delta[n]) for n in order],
            *[lead(new_m[n]) for n in order], *[lead(new_v[n]) for n in order])
```

```python
import functools
import math

import jax
import jax.numpy as jnp
from jax import lax
from jax.experimental import pallas as pl
from jax.experimental.pallas import tpu as pltpu

F32 = jnp.float32
BF16 = jnp.bfloat16

D_MODEL = 1024
N_HEADS = 8
HEAD_DIM = 64
D_ATTN = N_HEADS * HEAD_DIM
MOBA_BLOCK = 256
MOBA_TOPK = 3
ROPE_THETA = 10000.0
POOL_WINDOWS = (2, 4, 8, 16)
POOL_GROUP = 128
D_POOL = len(POOL_WINDOWS) * POOL_GROUP
POOL_HALO = 16
D_FF = 2816
D_IN_PROJ = 3 * D_ATTN + D_POOL + 2 * D_MODEL
LN_EPS = 1e-5
ALPHA = 2.0 ** 0.25
NEG = -1e30
N_DEV = 8
FF_SHARD = D_FF // N_DEV

ADAM_LR = 0.001
ADAM_B1 = 0.9
ADAM_B2 = 0.999
ADAM_EPS = 1e-08
ADAM_WD = 0.01
ADAM_STEP = 10

TOK = 256
FF_CHUNK = 256
LANE = 128
VMEM_LIMIT = 56 * 1024 * 1024

MESH = pl.DeviceIdType.MESH
NT_DIMS = (((1,), (1,)), ((), ()))
TN_DIMS = (((0,), (0,)), ((), ()))


def _params(*sem):
    return pltpu.CompilerParams(dimension_semantics=sem or None, vmem_limit_bytes=VMEM_LIMIT)


def _full(shape):
    zeros = (0,) * len(shape)
    return pl.BlockSpec(shape, lambda *_: zeros, pipeline_mode=pl.Buffered(1))


def _rows(width, tile=TOK):
    return pl.BlockSpec((tile, width), lambda i: (i, 0))


def _sds(shape, dtype):
    return jax.ShapeDtypeStruct(shape, dtype)


def _dot(a, b):
    return jnp.dot(a, b, preferred_element_type=F32)


def _dot_nt(a, b):
    return lax.dot_general(a, b, NT_DIMS, preferred_element_type=F32)


def _dot_tn(a, b):
    return lax.dot_general(a, b, TN_DIMS, preferred_element_type=F32)


def _rope_tables(seq):
    half = HEAD_DIM // 2
    inv_freq = 1.0 / (ROPE_THETA ** (jnp.arange(half, dtype=F32) / half))
    ang = jnp.arange(seq, dtype=F32)[:, None] * inv_freq[None, :]
    cos, sin = jnp.cos(ang), jnp.sin(ang)
    return jnp.tile(cos, (1, 4)), jnp.tile(jnp.concatenate([-sin, sin], axis=1), (1, 2))


def _swap_halves(t):
    lane = lax.broadcasted_iota(jnp.int32, t.shape, 1)
    return jnp.where((lane % HEAD_DIM) < HEAD_DIM // 2, pltpu.roll(t, LANE - 32, 1), pltpu.roll(t, 32, 1))


def _rope(t, cos, sin):
    return t * cos + _swap_halves(t) * sin


def _rope_transposed(g, cos, sin):
    return g * cos + _swap_halves(g * sin)


def _ln_fwd(r, g, b):
    mu = jnp.mean(r, axis=-1, keepdims=True)
    xc = r - mu
    var = jnp.mean(xc * xc, axis=-1, keepdims=True)
    rstd = lax.rsqrt(var + LN_EPS)
    xhat = xc * rstd
    return xhat * g + b, xhat, rstd


def _ln_bwd(dy, xhat, rstd, g):
    dxh = dy * g
    m1 = jnp.mean(dxh, axis=-1, keepdims=True)
    m2 = jnp.mean(dxh * xhat, axis=-1, keepdims=True)
    return rstd * (dxh - m1 - xhat * m2)


def _normal_cdf(a):
    return 0.5 * (1.0 + lax.erf(a * (1.0 / math.sqrt(2.0))))


def _gelu_derivative(a, cdf):
    return cdf + a * (jnp.exp(-0.5 * a * a) * (1.0 / math.sqrt(2.0 * math.pi)))


def _shift_down(a, k):
    row = lax.broadcasted_iota(jnp.int32, a.shape, 0)
    return jnp.where(row >= k, pltpu.roll(a, k, 0), 0.0)


def _shift_up(a, k):
    n = a.shape[0]
    row = lax.broadcasted_iota(jnp.int32, a.shape, 0)
    return jnp.where(row < n - k, pltpu.roll(a, n - k, 0), 0.0)


def _conv(a, cw, cb):
    return cw[2:3, :] * a + cw[1:2, :] * _shift_down(a, 1) + cw[0:1, :] * _shift_down(a, 2) + cb


def _pool_count(first_row, rows, window):
    t = first_row + lax.broadcasted_iota(jnp.int32, (rows, 1), 0)
    return jnp.minimum(t + 1, window).astype(F32)


def _grid_call(body, name, steps, in_specs, out_specs, out_shape, operands, scratch=(), tasks=()):
    t_operands, t_in_specs, t_out_shape, t_out_specs, t_sems = _task_args(tasks)
    outs = pl.pallas_call(
        _carry(body, tasks, len(in_specs), len(out_specs), len(scratch), steps), name=name, grid=(steps,),
        in_specs=list(in_specs) + t_in_specs, out_specs=list(out_specs) + t_out_specs,
        out_shape=list(out_shape) + t_out_shape, scratch_shapes=list(scratch) + t_sems,
        compiler_params=_params("arbitrary"),
    )(*operands, *t_operands)
    return outs[:len(out_specs)], _task_results(tasks, outs[len(out_specs):])


def _proj_in(x, win, b_gate, cos, sin, tasks=()):
    seq = x.shape[0]
    nt = seq // TOK

    def body(x_ref, win_ref, bg_ref, cos_ref, sin_ref, xt_ref, q_ref, k_ref, v_ref, u_ref, g_ref, km_ref):
        xb = x_ref[...].astype(BF16)
        xt_ref[...] = x_ref[...].T.astype(BF16)
        cos_t, sin_t = cos_ref[...], sin_ref[...]
        for sec, out_ref in ((0, q_ref), (1, k_ref)):
            z = _dot(xb, win_ref[sec])
            for c in range(D_ATTN // LANE):
                cols = slice(LANE * c, LANE * (c + 1))
                out_ref[:, cols] = _rope(z[:, cols], cos_t, sin_t)
        for b in range(TOK // MOBA_BLOCK):
            km_ref[b] = jnp.mean(k_ref[MOBA_BLOCK * b:MOBA_BLOCK * (b + 1), :], axis=0, keepdims=True)
        v_ref[...] = _dot(xb, win_ref[2]).astype(BF16)
        u_ref[...] = _dot(xb, win_ref[3])
        for n in range(4):
            cols = slice(D_ATTN * n, D_ATTN * (n + 1))
            g_ref[:, cols] = jax.nn.sigmoid(_dot(xb, win_ref[4 + n]) + bg_ref[:, cols])

    return _grid_call(
        body, "proj_in", nt,
        in_specs=[_rows(D_MODEL), _full(win.shape), _full((1, 2 * D_MODEL)), _rows(LANE), _rows(LANE)],
        out_specs=[pl.BlockSpec((D_MODEL, TOK), lambda i: (0, i)), _rows(D_ATTN), _rows(D_ATTN), _rows(D_ATTN),
                   _rows(D_POOL), _rows(2 * D_MODEL),
                   pl.BlockSpec((TOK // MOBA_BLOCK, 1, D_ATTN), lambda i: (i, 0, 0))],
        out_shape=[_sds((D_MODEL, seq), BF16), _sds((seq, D_ATTN), F32), _sds((seq, D_ATTN), F32),
                   _sds((seq, D_ATTN), BF16), _sds((seq, D_POOL), F32), _sds((seq, 2 * D_MODEL), F32),
                   _sds((seq // MOBA_BLOCK, 1, D_ATTN), F32)],
        operands=(x, win, b_gate, cos, sin), tasks=tasks)


SCORE_CHUNK = 128


def _store_keys(ka_sc, k_ref, ls):
    seq = ka_sc.shape[0]
    ka_sc[:, 0:HEAD_DIM] = k_ref[:, ls].astype(BF16)
    row = lax.broadcasted_iota(jnp.int32, (seq, HEAD_DIM), 0)
    lane = lax.broadcasted_iota(jnp.int32, (seq, HEAD_DIM), 1)
    in_block = (lane * MOBA_BLOCK <= row) & (row < (lane + 1) * MOBA_BLOCK)
    ka_sc[:, HEAD_DIM:] = jnp.where(in_block, 1.0, 0.0).astype(BF16)


def _block_bias(qf, km, i):
    if i <= MOBA_TOPK:
        return jnp.zeros((MOBA_BLOCK, HEAD_DIM), BF16)
    nb = km.shape[0]
    gate = lax.dot_general(km, qf, NT_DIMS, precision=lax.Precision.HIGHEST, preferred_element_type=F32)
    blk = lax.broadcasted_iota(jnp.int32, gate.shape, 0)
    rank = jnp.zeros(gate.shape, F32)
    for r in range(1, i):
        lower = pltpu.roll(gate, r, 0)
        rank = rank + jnp.where((blk >= r) & (lower >= gate), 1.0, 0.0)
        higher = pltpu.roll(gate, nb - r, 0)
        rank = rank + jnp.where((blk + r < i) & (higher > gate), 1.0, 0.0)
    bias = jnp.where((blk < i) & (rank >= MOBA_TOPK), NEG, 0.0)
    padded = jnp.concatenate([bias, jnp.zeros((LANE - nb, MOBA_BLOCK), F32)], axis=0)
    return jnp.transpose(padded)[:, 0:HEAD_DIM].astype(BF16)


def _causal(shape, transposed=False):
    row = lax.broadcasted_iota(jnp.int32, shape, 0)
    col = lax.broadcasted_iota(jnp.int32, shape, 1)
    return (row <= col) if transposed else (col <= row)


def _row_vector(col):
    return jnp.transpose(jnp.broadcast_to(col, (MOBA_BLOCK, LANE)))[0:1, :]


def _attn_fwd(q, k, v, kmean, tasks=()):
    seq = q.shape[0]
    nb = seq // MOBA_BLOCK
    assert nb == 8, "the block ranking keeps one sublane per key block"
    pair = pl.BlockSpec((seq, LANE), lambda p: (0, p))
    heads = LANE // HEAD_DIM

    def body(q_ref, k_ref, v_ref, km_ref, o_ref, lse_ref, bias_ref, ka_sc, qa_sc, s_sc, p_sc):
        lse_ref[0, heads:, :] = jnp.zeros((8 - heads, seq), F32)
        for hh in range(heads):
            ls = slice(HEAD_DIM * hh, HEAD_DIM * (hh + 1))
            _store_keys(ka_sc, k_ref, ls)
            vb = v_ref[:, ls]
            km = km_ref[:, ls]
            for i in range(nb):
                rs = slice(MOBA_BLOCK * i, MOBA_BLOCK * (i + 1))
                width = MOBA_BLOCK * (i + 1)
                qf = q_ref[rs, ls]
                bias = _block_bias(qf, km, i)
                bias_ref[rs, ls] = bias
                qa_sc[:, 0:HEAD_DIM] = (qf * HEAD_DIM ** -0.5).astype(BF16)
                qa_sc[:, HEAD_DIM:] = bias
                s_sc[:, 0:width] = _dot_nt(qa_sc[...], ka_sc[0:width, :])
                s_sc[:, rs] = jnp.where(_causal((MOBA_BLOCK, MOBA_BLOCK)), s_sc[:, rs], NEG)
                chunks = [slice(SCORE_CHUNK * c, SCORE_CHUNK * (c + 1)) for c in range(width // SCORE_CHUNK)]
                top = s_sc[:, chunks[0]]
                for c in chunks[1:]:
                    top = jnp.maximum(top, s_sc[:, c])
                m = jnp.max(top, axis=1, keepdims=True)
                total = jnp.zeros((MOBA_BLOCK, SCORE_CHUNK), F32)
                for c in chunks:
                    p = jnp.exp(s_sc[:, c] - m)
                    total = total + p
                    p_sc[:, c] = p.astype(BF16)
                l = jnp.sum(total, axis=1, keepdims=True)
                o_ref[rs, ls] = _dot(p_sc[:, 0:width], vb[0:width]) / l
                lse_ref[0, hh:hh + 1, rs] = _row_vector(m + jnp.log(l))

    return _grid_call(
        body, "attn_fwd", D_ATTN // LANE,
        in_specs=[pair, pair, pair, pl.BlockSpec((nb, LANE), lambda p: (0, p))],
        out_specs=[pair, pl.BlockSpec((1, 8, seq), lambda p: (p, 0, 0)), pair],
        out_shape=[_sds((seq, D_ATTN), F32), _sds((D_ATTN // LANE, 8, seq), F32), _sds((seq, D_ATTN), BF16)],
        operands=(q, k, v, kmean),
        scratch=[pltpu.VMEM((seq, LANE), BF16), pltpu.VMEM((MOBA_BLOCK, LANE), BF16),
                 pltpu.VMEM((MOBA_BLOCK, seq), F32), pltpu.VMEM((MOBA_BLOCK, seq), BF16)],
        tasks=tasks)


def _mix(o, u, g, x, wba, wbp, wout, w_pool, pool_scale, ln_g, ln_b, tasks=()):
    seq = x.shape[0]

    def body(o_ref, u_ref, uprev_ref, g_ref, x_ref, wba_ref, wbp_ref, wout_ref, wp_ref, ps_ref, lg_ref, lb_ref,
             ya_ref, yp_ref, pooled_ref, mixed_ref, ypre_ref, merged_ref, xhat_ref, rstd_ref, h_ref, hb_ref, ext):
        i = pl.program_id(0)
        ya = _dot(o_ref[...].astype(BF16), wba_ref[...])
        ucur = u_ref[...]
        ext[0:POOL_HALO, :] = jnp.where(i > 0, uprev_ref[...], 0.0)
        ext[POOL_HALO:, :] = ucur
        for grp, window in enumerate(POOL_WINDOWS):
            cols = slice(POOL_GROUP * grp, POOL_GROUP * (grp + 1))
            acc = ucur[:, cols]
            for kk in range(1, window):
                acc = acc + ext[pl.ds(POOL_HALO - kk, TOK), cols]
            pooled = acc / _pool_count(i * TOK, TOK, window) - ucur[:, cols]
            pooled_ref[:, cols] = pooled.astype(BF16)
            mixed_ref[:, cols] = _dot(pooled.astype(BF16), wp_ref[grp].astype(BF16))
        mixed = mixed_ref[...]
        ypre = (mixed * ps_ref[...]).astype(BF16)
        ypre_ref[...] = ypre
        yp = _dot(ypre, wbp_ref[...])
        ya_ref[...] = ya
        yp_ref[...] = yp
        merged = (g_ref[:, :D_MODEL] * ya + g_ref[:, D_MODEL:] * yp).astype(BF16)
        merged_ref[...] = merged
        r1 = ALPHA * x_ref[...] + _dot(merged, wout_ref[...])
        h, xhat, rstd = _ln_fwd(r1, lg_ref[...], lb_ref[...])
        xhat_ref[...] = xhat
        rstd_ref[...] = jnp.broadcast_to(rstd, (TOK, LANE))
        h_ref[...] = h
        hb_ref[...] = h.astype(BF16)

    halo = pl.BlockSpec((POOL_HALO, D_POOL), lambda i: (jnp.maximum(i * (TOK // POOL_HALO) - 1, 0), 0))
    return _grid_call(
        body, "mix", seq // TOK,
        in_specs=[_rows(D_ATTN), _rows(D_POOL), halo, _rows(2 * D_MODEL), _rows(D_MODEL),
                  _full(wba.shape), _full(wbp.shape), _full(wout.shape), _full(w_pool.shape),
                  _full((1, D_POOL)), _full((1, D_MODEL)), _full((1, D_MODEL))],
        out_specs=[_rows(D_MODEL), _rows(D_MODEL), _rows(D_POOL), _rows(D_POOL), _rows(D_POOL), _rows(D_MODEL),
                   _rows(D_MODEL), _rows(LANE), _rows(D_MODEL), _rows(D_MODEL)],
        out_shape=[_sds((seq, D_MODEL), F32), _sds((seq, D_MODEL), F32), _sds((seq, D_POOL), BF16),
                   _sds((seq, D_POOL), F32), _sds((seq, D_POOL), BF16), _sds((seq, D_MODEL), BF16),
                   _sds((seq, D_MODEL), F32), _sds((seq, LANE), F32), _sds((seq, D_MODEL), F32),
                   _sds((seq, D_MODEL), BF16)],
        operands=(o, u, u, g, x, wba, wbp, wout, w_pool, pool_scale, ln_g, ln_b),
        scratch=[pltpu.VMEM((TOK + POOL_HALO, D_POOL), F32)], tasks=tasks)


def _ffn_up(hb, wgt, wut, conv_w, conv_b, tasks=()):
    seq = hb.shape[0]
    wblk = pl.BlockSpec((FF_CHUNK, D_MODEL), lambda c: (c, 0))
    cblk = lambda rows: pl.BlockSpec((rows, FF_CHUNK), lambda c: (0, c))
    oblk = pl.BlockSpec((seq, FF_CHUNK), lambda c: (0, c))

    def body(h_ref, wg_ref, wu_ref, cw_ref, cb_ref, a_ref, u_ref, act_ref):
        h = h_ref[...]
        a = _dot_nt(h, wg_ref[...])
        u = _dot_nt(h, wu_ref[...])
        a_ref[...] = a
        u_ref[...] = u
        ac = _conv(a, cw_ref[...], cb_ref[...])
        act_ref[...] = (ac * _normal_cdf(ac) * u).astype(BF16)

    return _grid_call(
        body, "ffn_up", D_FF // FF_CHUNK,
        in_specs=[_full(hb.shape), wblk, wblk, cblk(3), cblk(1)],
        out_specs=[oblk, oblk, oblk],
        out_shape=[_sds((seq, D_FF), F32), _sds((seq, D_FF), F32), _sds((seq, D_FF), BF16)],
        operands=(hb, wgt, wut, conv_w, conv_b), tasks=tasks)


def _ffn_down(act, wd, h, target, ln_g, ln_b):
    seq = h.shape[0]

    def body(act_ref, wd_ref, h_ref, t_ref, lg_ref, lb_ref, dr_ref, drb_ref, loss_ref, dg_ref, db_ref):
        i = pl.program_id(0)

        @pl.when(i == 0)
        def _():
            loss_ref[...] = jnp.zeros_like(loss_ref)
            dg_ref[...] = jnp.zeros_like(dg_ref)
            db_ref[...] = jnp.zeros_like(db_ref)

        r2 = ALPHA * h_ref[...] + _dot(act_ref[...], wd_ref[...])
        y, xhat, rstd = _ln_fwd(r2, lg_ref[...], lb_ref[...])
        diff = y - t_ref[...]
        loss_ref[...] += jnp.sum(diff * diff) * (0.5 / D_MODEL)
        dy = diff * (1.0 / D_MODEL)
        dg_ref[...] += jnp.sum(dy * xhat, axis=0, keepdims=True)
        db_ref[...] += jnp.sum(dy, axis=0, keepdims=True)
        dr = _ln_bwd(dy, xhat, rstd, lg_ref[...])
        dr_ref[...] = dr
        drb_ref[...] = dr.astype(BF16)

    vec = pl.BlockSpec((1, D_MODEL), lambda i: (0, 0))
    return pl.pallas_call(
        body, name="ffn_down", grid=(seq // TOK,),
        in_specs=[_rows(D_FF), _full(wd.shape), _rows(D_MODEL), _rows(D_MODEL), _full((1, D_MODEL)), _full((1, D_MODEL))],
        out_specs=[_rows(D_MODEL), _rows(D_MODEL), pl.BlockSpec((8, LANE), lambda i: (0, 0)), vec, vec],
        out_shape=[_sds((seq, D_MODEL), F32), _sds((seq, D_MODEL), BF16), _sds((8, LANE), F32),
                   _sds((1, D_MODEL), F32), _sds((1, D_MODEL), F32)],
        compiler_params=_params("arbitrary"),
    )(act, wd, h, target, ln_g, ln_b)


def _ffn_bwd(drb, hb, a, u, wd, conv_w, conv_b):
    seq = hb.shape[0]
    wblk = pl.BlockSpec((FF_CHUNK, D_MODEL), lambda c: (c, 0))
    cblk = lambda rows: pl.BlockSpec((rows, FF_CHUNK), lambda c: (0, c))
    sblk = pl.BlockSpec((seq, FF_CHUNK), lambda c: (0, c))

    def body(dr_ref, h_ref, a_ref, u_ref, wd_ref, cw_ref, cb_ref, da_ref, du_ref, dwd_ref, dwg_ref, dwu_ref, dc_ref):
        dr = dr_ref[...]
        h = h_ref[...]
        a = a_ref[...]
        u = u_ref[...]
        cw = cw_ref[...]
        dact = _dot_nt(dr, wd_ref[...])
        ac = _conv(a, cw, cb_ref[...])
        cdf = _normal_cdf(ac)
        gelu = ac * cdf
        dwd_ref[...] = _dot_tn((gelu * u).astype(BF16), dr).astype(BF16)
        du = (dact * gelu).astype(BF16)
        dac = dact * u * _gelu_derivative(ac, cdf)
        da = (cw[2:3, :] * dac + cw[1:2, :] * _shift_up(dac, 1) + cw[0:1, :] * _shift_up(dac, 2)).astype(BF16)
        da_ref[...] = da
        du_ref[...] = du
        dwg_ref[...] = _dot_tn(da, h).astype(BF16)
        dwu_ref[...] = _dot_tn(du, h).astype(BF16)
        dc_ref[0:1, :] = jnp.sum(dac * _shift_down(a, 2), axis=0, keepdims=True)
        dc_ref[1:2, :] = jnp.sum(dac * _shift_down(a, 1), axis=0, keepdims=True)
        dc_ref[2:3, :] = jnp.sum(dac * a, axis=0, keepdims=True)
        dc_ref[3:4, :] = jnp.sum(dac, axis=0, keepdims=True)
        dc_ref[4:8, :] = jnp.zeros((4, FF_CHUNK), F32)

    return pl.pallas_call(
        body, name="ffn_bwd", grid=(D_FF // FF_CHUNK,),
        in_specs=[_full(drb.shape), _full(hb.shape), sblk, sblk, wblk, cblk(3), cblk(1)],
        out_specs=[sblk, sblk, wblk, wblk, wblk, cblk(8)],
        out_shape=[_sds((seq, D_FF), BF16), _sds((seq, D_FF), BF16), _sds((D_FF, D_MODEL), BF16),
                   _sds((D_FF, D_MODEL), BF16), _sds((D_FF, D_MODEL), BF16), _sds((8, D_FF), F32)],
        compiler_params=_params("parallel"),
    )(drb, hb, a, u, wd, conv_w, conv_b)


def _ln1_bwd(dr2, da, du, wgt, wut, xhat, rstd, ln_g, tasks=()):
    seq = dr2.shape[0]

    def body(dr2_ref, da_ref, du_ref, wg_ref, wu_ref, xhat_ref, rstd_ref, lg_ref, dr_ref, drb_ref, dg_ref, db_ref):
        @pl.when(pl.program_id(0) == 0)
        def _():
            dg_ref[...] = jnp.zeros_like(dg_ref)
            db_ref[...] = jnp.zeros_like(db_ref)

        dh = ALPHA * dr2_ref[...] + _dot(da_ref[...], wg_ref[...]) + _dot(du_ref[...], wu_ref[...])
        xhat = xhat_ref[...]
        dg_ref[...] += jnp.sum(dh * xhat, axis=0, keepdims=True)
        db_ref[...] += jnp.sum(dh, axis=0, keepdims=True)
        dr = _ln_bwd(dh, xhat, rstd_ref[:, 0:1], lg_ref[...])
        dr_ref[...] = dr
        drb_ref[...] = dr.astype(BF16)

    vec = pl.BlockSpec((1, D_MODEL), lambda i: (0, 0))
    return _grid_call(
        body, "ln1_bwd", seq // TOK,
        in_specs=[_rows(D_MODEL), _rows(D_FF), _rows(D_FF), _full(wgt.shape), _full(wut.shape), _rows(D_MODEL),
                  _rows(LANE), _full((1, D_MODEL))],
        out_specs=[_rows(D_MODEL), _rows(D_MODEL), vec, vec],
        out_shape=[_sds((seq, D_MODEL), F32), _sds((seq, D_MODEL), BF16), _sds((1, D_MODEL), F32),
                   _sds((1, D_MODEL), F32)],
        operands=(dr2, da, du, wgt, wut, xhat, rstd, ln_g), tasks=tasks)


def _mix_bwd(drb, ya, yp, g, mixed, wout, wba, wbp, w_pool, pool_scale, tasks=()):
    seq = drb.shape[0]

    def body(dr_ref, ya_ref, yp_ref, g_ref, mixed_ref, wout_ref, wba_ref, wbp_ref, wp_ref, ps_ref,
             dzg_ref, dya_ref, dyp_ref, do_ref, dmixed_ref, dpooled_ref, dbg_ref, dps_ref):
        @pl.when(pl.program_id(0) == 0)
        def _():
            dbg_ref[...] = jnp.zeros_like(dbg_ref)
            dps_ref[...] = jnp.zeros_like(dps_ref)

        dmerged = _dot_nt(dr_ref[...], wout_ref[...])
        ga, gp = g_ref[:, :D_MODEL], g_ref[:, D_MODEL:]
        dzga = dmerged * ya_ref[...] * ga * (1.0 - ga)
        dzgp = dmerged * yp_ref[...] * gp * (1.0 - gp)
        dzg_ref[:, :D_MODEL] = dzga.astype(BF16)
        dzg_ref[:, D_MODEL:] = dzgp.astype(BF16)
        dbg_ref[:, :D_MODEL] += jnp.sum(dzga, axis=0, keepdims=True)
        dbg_ref[:, D_MODEL:] += jnp.sum(dzgp, axis=0, keepdims=True)
        dya = (dmerged * ga).astype(BF16)
        dyp = (dmerged * gp).astype(BF16)
        dya_ref[...] = dya
        dyp_ref[...] = dyp
        do_ref[...] = _dot_nt(dya, wba_ref[...])
        dypre = _dot_nt(dyp, wbp_ref[...])
        dps_ref[...] += jnp.sum(dypre * mixed_ref[...], axis=0, keepdims=True)
        dmixed = (dypre * ps_ref[...]).astype(BF16)
        dmixed_ref[...] = dmixed
        for grp in range(len(POOL_WINDOWS)):
            cols = slice(POOL_GROUP * grp, POOL_GROUP * (grp + 1))
            dpooled_ref[:, cols] = _dot_nt(dmixed[:, cols], wp_ref[grp].astype(BF16))

    return _grid_call(
        body, "mix_bwd", seq // TOK,
        in_specs=[_rows(D_MODEL), _rows(D_MODEL), _rows(D_MODEL), _rows(2 * D_MODEL), _rows(D_POOL),
                  _full(wout.shape), _full(wba.shape), _full(wbp.shape), _full(w_pool.shape), _full((1, D_POOL))],
        out_specs=[_rows(2 * D_MODEL), _rows(D_MODEL), _rows(D_MODEL), _rows(D_ATTN), _rows(D_POOL), _rows(D_POOL),
                   pl.BlockSpec((1, 2 * D_MODEL), lambda i: (0, 0)), pl.BlockSpec((1, D_POOL), lambda i: (0, 0))],
        out_shape=[_sds((seq, 2 * D_MODEL), BF16), _sds((seq, D_MODEL), BF16), _sds((seq, D_MODEL), BF16),
                   _sds((seq, D_ATTN), F32), _sds((seq, D_POOL), BF16), _sds((seq, D_POOL), F32),
                   _sds((1, 2 * D_MODEL), F32), _sds((1, D_POOL), F32)],
        operands=(drb, ya, yp, g, mixed, wout, wba, wbp, w_pool, pool_scale), tasks=tasks)


def _attn_bwd(q, k, v, bias, o, lse, do, cos, sin, tasks=()):
    seq = q.shape[0]
    nb = seq // MOBA_BLOCK
    pair = pl.BlockSpec((seq, LANE), lambda p: (0, p))
    table = pl.BlockSpec((seq, LANE), lambda p: (0, 0))
    scale = HEAD_DIM ** -0.5

    def body(q_ref, k_ref, v_ref, bias_ref, o_ref, lse_ref, do_ref, cos_ref, sin_ref, dq_ref, dk_ref, dv_ref,
             dq_acc, dk_acc, dv_acc, dk_head, dv_head, ka_sc, qa_sc, s_sc, dp_sc, p_sc, ds_sc):
        for hh in range(LANE // HEAD_DIM):
            ls = slice(HEAD_DIM * hh, HEAD_DIM * (hh + 1))
            _store_keys(ka_sc, k_ref, ls)
            vb = v_ref[:, ls]
            dk_head[...] = jnp.zeros_like(dk_head)
            dv_head[...] = jnp.zeros_like(dv_head)
            for i in range(nb):
                rs = slice(MOBA_BLOCK * i, MOBA_BLOCK * (i + 1))
                width = MOBA_BLOCK * (i + 1)
                qa_sc[:, 0:HEAD_DIM] = (q_ref[rs, ls] * scale).astype(BF16)
                qa_sc[:, HEAD_DIM:] = bias_ref[rs, ls]
                s_sc[0:width, :] = _dot_nt(ka_sc[0:width, :], qa_sc[...])
                s_sc[rs, :] = jnp.where(_causal((MOBA_BLOCK, MOBA_BLOCK), transposed=True), s_sc[rs, :], NEG)
                dob = do_ref[rs, ls]
                delta = _row_vector(jnp.sum(dob * o_ref[rs, ls], axis=1, keepdims=True))
                lse_row = lse_ref[0, hh:hh + 1, rs]
                dob16 = dob.astype(BF16)
                dp_sc[0:width, :] = _dot_nt(vb[0:width], dob16)
                for c in range(width // SCORE_CHUNK):
                    rows = slice(SCORE_CHUNK * c, SCORE_CHUNK * (c + 1))
                    p = jnp.exp(s_sc[rows, :] - lse_row)
                    p_sc[rows, :] = p.astype(BF16)
                    ds_sc[rows, :] = (p * (dp_sc[rows, :] - delta)).astype(BF16)
                dv_head[0:width, :] += _dot(p_sc[0:width, :], dob16)
                dk_head[0:width, :] += _dot(ds_sc[0:width, :], qa_sc[:, 0:HEAD_DIM])
                dq_acc[rs, ls] = _dot_tn(ds_sc[0:width, :], ka_sc[0:width, 0:HEAD_DIM]) * scale
            dk_acc[:, ls] = dk_head[...]
            dv_acc[:, ls] = dv_head[...]
        cos_t, sin_t = cos_ref[...], sin_ref[...]
        dq_ref[...] = _rope_transposed(dq_acc[...], cos_t, sin_t).astype(BF16)
        dk_ref[...] = _rope_transposed(dk_acc[...], cos_t, sin_t).astype(BF16)
        dv_ref[...] = dv_acc[...].astype(BF16)

    return _grid_call(
        body, "attn_bwd", D_ATTN // LANE,
        in_specs=[pair, pair, pair, pair, pair, pl.BlockSpec((1, 8, seq), lambda p: (p, 0, 0)), pair, table, table],
        out_specs=[pair, pair, pair], out_shape=[_sds((seq, D_ATTN), BF16)] * 3,
        operands=(q, k, v, bias, o, lse, do, cos, sin),
        scratch=[pltpu.VMEM((seq, LANE), F32)] * 3 + [pltpu.VMEM((seq, HEAD_DIM), F32)] * 2
        + [pltpu.VMEM((seq, LANE), BF16), pltpu.VMEM((MOBA_BLOCK, LANE), BF16)]
        + [pltpu.VMEM((seq, MOBA_BLOCK), F32)] * 2 + [pltpu.VMEM((seq, MOBA_BLOCK), BF16)] * 2,
        tasks=tasks)


def _in_bwd(dq, dk, dv, dpooled, dzg, dr1, win, tasks=()):
    seq = dr1.shape[0]
    nt = seq // TOK

    def body(dq_ref, dk_ref, dv_ref, dp_ref, dpnext_ref, dzg_ref, dr_ref, win_ref, dx_ref, dz_ref, ext):
        i = pl.program_id(0)
        dp = dp_ref[...]
        dpn = jnp.where(i < nt - 1, dpnext_ref[...], 0.0)
        for grp, window in enumerate(POOL_WINDOWS):
            cols = slice(POOL_GROUP * grp, POOL_GROUP * (grp + 1))
            ext[0:TOK, cols] = dp[:, cols] / _pool_count(i * TOK, TOK, window)
            ext[TOK:, cols] = dpn[:, cols] / _pool_count((i + 1) * TOK, POOL_HALO, window)
        for grp, window in enumerate(POOL_WINDOWS):
            cols = slice(POOL_GROUP * grp, POOL_GROUP * (grp + 1))
            acc = ext[0:TOK, cols] - dp[:, cols]
            for kk in range(1, window):
                acc = acc + ext[pl.ds(kk, TOK), cols]
            dz_ref[:, 3 * D_ATTN + POOL_GROUP * grp:3 * D_ATTN + POOL_GROUP * (grp + 1)] = acc.astype(BF16)
        dz_ref[:, 0:D_ATTN] = dq_ref[...]
        dz_ref[:, D_ATTN:2 * D_ATTN] = dk_ref[...]
        dz_ref[:, 2 * D_ATTN:3 * D_ATTN] = dv_ref[...]
        dz_ref[:, 3 * D_ATTN + D_POOL:] = dzg_ref[...]
        dx = ALPHA * dr_ref[...]
        for n in range(N_DEV):
            dx = dx + _dot_nt(dz_ref[:, D_ATTN * n:D_ATTN * (n + 1)], win_ref[n])
        dx_ref[...] = dx

    halo = pl.BlockSpec((POOL_HALO, D_POOL),
                        lambda i: (jnp.minimum((i + 1) * (TOK // POOL_HALO), seq // POOL_HALO - 1), 0))
    return _grid_call(
        body, "in_bwd", nt,
        in_specs=[_rows(D_ATTN), _rows(D_ATTN), _rows(D_ATTN), _rows(D_POOL), halo, _rows(2 * D_MODEL),
                  _rows(D_MODEL), _full(win.shape)],
        out_specs=[_rows(D_MODEL), _rows(D_IN_PROJ)],
        out_shape=[_sds((seq, D_MODEL), F32), _sds((seq, D_IN_PROJ), BF16)],
        operands=(dq, dk, dv, dpooled, dpooled, dzg, dr1, win),
        scratch=[pltpu.VMEM((TOK + POOL_HALO, D_POOL), F32)], tasks=tasks)


def _dw_mixers(o, ypre, merged, dya, dyp, drb, pooled, dmixed, tasks=()):
    seq = o.shape[0]
    groups = len(POOL_WINDOWS)
    col = pl.BlockSpec((seq, LANE), lambda n: (0, n))
    grp = pl.BlockSpec((seq, POOL_GROUP), lambda n: (0, jnp.minimum(n, groups - 1)))
    owner = lambda rows, cols: pl.BlockSpec((1, rows, cols), lambda n: (n, 0, 0))

    def body(o_ref, ypre_ref, merged_ref, dya_ref, dyp_ref, dr_ref, pooled_ref, dmixed_ref,
             dba_ref, dbp_ref, dout_ref, dpool_ref, ob_sc):
        n = pl.program_id(0)

        @pl.when(n == 0)
        def _():
            ob_sc[...] = o_ref[...].astype(BF16)

        dba_ref[0] = _dot_tn(dya_ref[...], ob_sc[...]).T.astype(BF16)
        dbp_ref[0] = _dot_tn(dyp_ref[...], ypre_ref[...]).T.astype(BF16)
        dout_ref[0] = _dot_tn(merged_ref[...], dr_ref[...]).astype(BF16)

        @pl.when(n < groups)
        def _():
            dpool_ref[0] = _dot_tn(pooled_ref[...], dmixed_ref[...])

    return _grid_call(
        body, "dw_mixers", N_DEV,
        in_specs=[_full(o.shape), _full(ypre.shape), col, col, col, _full(drb.shape), grp, grp],
        out_specs=[owner(D_ATTN, LANE), owner(D_POOL, LANE), owner(D_MODEL // N_DEV, D_MODEL),
                   pl.BlockSpec((1, POOL_GROUP, POOL_GROUP), lambda n: (jnp.minimum(n, groups - 1), 0, 0))],
        out_shape=[_sds((N_DEV, D_ATTN, LANE), BF16), _sds((N_DEV, D_POOL, LANE), BF16),
                   _sds((N_DEV, D_MODEL // N_DEV, D_MODEL), BF16), _sds((groups, POOL_GROUP, POOL_GROUP), F32)],
        operands=(o, ypre, merged, dya, dyp, drb, pooled, dmixed),
        scratch=[pltpu.VMEM((seq, D_ATTN), BF16)], tasks=tasks)


def _to_bf16(arrays, tasks=()):
    n = len(arrays)

    def body(*refs):
        for src, dst in zip(refs[:n], refs[n:]):
            dst[...] = src[...].astype(BF16)

    return _grid_call(
        body, "to_bf16", 1, in_specs=[_full(a.shape) for a in arrays],
        out_specs=[pl.BlockSpec(a.shape, lambda i: (0, 0)) for a in arrays],
        out_shape=[_sds(a.shape, BF16) for a in arrays], operands=arrays, tasks=tasks)


def _matmul(name, a, b, out_shape, out_dtype, steps, a_spec, b_spec, o_spec, tasks=()):
    def body(a_ref, b_ref, o_ref):
        o_ref[...] = _dot(a_ref[...], b_ref[...]).reshape(o_ref.shape).astype(o_ref.dtype)

    (out,), results = _grid_call(body, name, steps, in_specs=[a_spec, b_spec], out_specs=[o_spec],
                                 out_shape=[_sds(out_shape, out_dtype)], operands=(a, b), tasks=tasks)
    return out, results


def _place():
    return lax.axis_index("x"), lax.axis_index("y"), lax.axis_index("c")


def _other_chips(x, y):
    return [(1 - x, y), (x, 1 - y), (1 - x, 1 - y)]


DMA_SEMS = pltpu.SemaphoreType.DMA


class _AllGather:
    def __init__(self, shards, lag=0):
        self.operands = list(shards)
        self.n = len(shards)
        self.lag = lag
        self.out_shape = [_sds((N_DEV, *s.shape), s.dtype) for s in shards]
        self.sems = [DMA_SEMS((7 * self.n,)), DMA_SEMS((7 * self.n,)), DMA_SEMS((self.n,))]

    def _copy(self, refs, a, k, block, to, from_input=False):
        ins, outs, (send_sems, recv_sems, _) = refs
        px, py, pc = block
        dst = outs[a].at[4 * px + 2 * py + pc]
        return pltpu.make_async_remote_copy(
            src_ref=ins[a] if from_input else dst, dst_ref=dst,
            send_sem=send_sems.at[7 * a + k], recv_sem=recv_sems.at[7 * a + k],
            device_id=to, device_id_type=MESH)

    def _local(self, refs, a):
        ins, outs, (_, _, local_sems) = refs
        x, y, c = _place()
        return pltpu.make_async_copy(ins[a], outs[a].at[4 * x + 2 * y + c], local_sems.at[a])

    def _pass_on(self, refs, a):
        x, y, c = _place()
        origin = ((x + 1 - c) % 2, (y + c) % 2, c)
        target = ((x + c) % 2, (y + 1 - c) % 2, c)
        return self._copy(refs, a, 3, origin, target)

    def start(self, refs):
        x, y, c = _place()
        for a in range(self.n):
            self._local(refs, a).start()
        for a in range(self.n):
            self._copy(refs, a, 0, (x, y, c), (x, y, 1 - c), True).start()
            for j, chip in enumerate(_other_chips(x, y)[:2]):
                self._copy(refs, a, 1 + j, (x, y, c), (*chip, c), True).start()

    def middle(self, refs):
        x, y, c = _place()
        me, sibling = (x, y, c), (x, y, 1 - c)
        chips = _other_chips(x, y)
        for a in range(self.n):
            for j in range(2):
                self._copy(refs, a, 1 + j, (*chips[j], c), me).wait_recv()
        for a in range(self.n):
            self._pass_on(refs, a).start()
            for j in range(2):
                self._copy(refs, a, 4 + j, (*chips[j], c), sibling).start()

    def late(self, refs):
        x, y, c = _place()
        diagonal = (1 - x, 1 - y, c)
        for a in range(self.n):
            self._copy(refs, a, 3, diagonal, (x, y, c)).wait_recv()
            self._copy(refs, a, 6, diagonal, (x, y, 1 - c)).start()

    def finish(self, refs):
        x, y, c = _place()
        me, sibling = (x, y, c), (x, y, 1 - c)
        chips = _other_chips(x, y)
        for a in range(self.n):
            self._copy(refs, a, 0, sibling, me).wait_recv()
            for j, chip in enumerate(chips):
                self._copy(refs, a, 4 + j, (*chip, 1 - c), me).wait_recv()
        for a in range(self.n):
            self._copy(refs, a, 0, me, sibling, True).wait_send()
            for j, chip in enumerate(chips[:2]):
                self._copy(refs, a, 1 + j, me, (*chip, c), True).wait_send()
            self._pass_on(refs, a).wait_send()
            for j, chip in enumerate(chips):
                self._copy(refs, a, 4 + j, (*chip, c), sibling).wait_send()
            self._local(refs, a).wait()


class _SiblingSend:
    def __init__(self, partials):
        self.operands = list(partials)
        self.n = len(partials)
        self.out_shape = [_sds((4, *p.shape[1:]), p.dtype) for p in partials]
        self.sems = [DMA_SEMS((4 * self.n,)), DMA_SEMS((4 * self.n,))]

    def _copy(self, refs, a, q):
        ins, outs, (send_sems, recv_sems) = refs
        x, y, c = _place()
        return pltpu.make_async_remote_copy(
            src_ref=ins[a].at[2 * q + 1 - c], dst_ref=outs[a].at[q],
            send_sem=send_sems.at[4 * a + q], recv_sem=recv_sems.at[4 * a + q],
            device_id=(x, y, 1 - c), device_id_type=MESH)

    def start(self, refs):
        for a in range(self.n):
            for q in range(4):
                self._copy(refs, a, q).start()

    def middle(self, refs):
        pass

    def finish(self, refs):
        for a in range(self.n):
            for q in range(4):
                self._copy(refs, a, q).wait()


class _ChipScatter:
    def __init__(self, chip_partials):
        self.operands = list(chip_partials)
        self.n = len(chip_partials)
        self.out_shape = [_sds(p.shape, p.dtype) for p in chip_partials]
        self.sems = [DMA_SEMS((3 * self.n,)), DMA_SEMS((3 * self.n,)), DMA_SEMS((self.n,))]

    def _copy(self, refs, a, k, arrival=False):
        ins, outs, (send_sems, recv_sems, _) = refs
        x, y, c = _place()
        px, py = _other_chips(x, y)[k]
        mine, theirs = 2 * x + y, 2 * px + py
        return pltpu.make_async_remote_copy(
            src_ref=ins[a].at[mine if arrival else theirs], dst_ref=outs[a].at[theirs if arrival else mine],
            send_sem=send_sems.at[3 * a + k], recv_sem=recv_sems.at[3 * a + k],
            device_id=(px, py, c), device_id_type=MESH)

    def _local(self, refs, a):
        ins, outs, (_, _, local_sems) = refs
        x, y, _ = _place()
        return pltpu.make_async_copy(ins[a].at[2 * x + y], outs[a].at[2 * x + y], local_sems.at[a])

    def start(self, refs):
        for a in range(self.n):
            self._local(refs, a).start()
            for k in range(3):
                self._copy(refs, a, k).start()

    def middle(self, refs):
        pass

    def finish(self, refs):
        for a in range(self.n):
            for k in range(3):
                self._copy(refs, a, k, arrival=True).wait_recv()
        for a in range(self.n):
            for k in range(3):
                self._copy(refs, a, k).wait_send()
            self._local(refs, a).wait()


class _DirectScatter:
    def __init__(self, partials):
        self.operands = list(partials)
        self.n = len(partials)
        self.out_shape = [_sds(p.shape, p.dtype) for p in partials]
        self.sems = [DMA_SEMS((7 * self.n,)), DMA_SEMS((7 * self.n,)), DMA_SEMS((self.n,))]

    def _copy(self, refs, a, k, arrival=False):
        ins, outs, (send_sems, recv_sems, _) = refs
        x, y, c = _place()
        peer = [(x, y, 1 - c), (1 - x, y, c), (x, 1 - y, c), (1 - x, 1 - y, c),
                (1 - x, y, 1 - c), (x, 1 - y, 1 - c), (1 - x, 1 - y, 1 - c)][k]
        mine, theirs = 4 * x + 2 * y + c, 4 * peer[0] + 2 * peer[1] + peer[2]
        return pltpu.make_async_remote_copy(
            src_ref=ins[a].at[mine if arrival else theirs], dst_ref=outs[a].at[theirs if arrival else mine],
            send_sem=send_sems.at[7 * a + k], recv_sem=recv_sems.at[7 * a + k],
            device_id=peer, device_id_type=MESH)

    def _local(self, refs, a):
        ins, outs, (_, _, local_sems) = refs
        x, y, c = _place()
        return pltpu.make_async_copy(ins[a].at[4 * x + 2 * y + c], outs[a].at[4 * x + 2 * y + c], local_sems.at[a])

    def start(self, refs):
        for a in range(self.n):
            self._local(refs, a).start()
            for k in range(7):
                self._copy(refs, a, k).start()

    def middle(self, refs):
        pass

    def finish(self, refs):
        for a in range(self.n):
            for k in range(7):
                self._copy(refs, a, k, arrival=True).wait_recv()
        for a in range(self.n):
            for k in range(7):
                self._copy(refs, a, k).wait_send()
            self._local(refs, a).wait()


def _task_args(tasks):
    hbm = pl.BlockSpec(memory_space=pl.ANY)
    operands = [o for t in tasks for o in t.operands]
    out_shape = [s for t in tasks for s in t.out_shape]
    sems = [s for t in tasks for s in t.sems]
    return operands, [hbm] * len(operands), out_shape, [hbm] * len(out_shape), sems


def _task_refs(tasks, ins, outs, sems):
    per_task = []
    for t in tasks:
        ni, no, ns = len(t.operands), len(t.out_shape), len(t.sems)
        per_task.append((ins[:ni], outs[:no], sems[:ns]))
        ins, outs, sems = ins[ni:], outs[no:], sems[ns:]
    return per_task


def _task_results(tasks, outs):
    res = []
    for t in tasks:
        res.append(list(outs[:len(t.out_shape)]))
        outs = outs[len(t.out_shape):]
    return res


def _carry(body, tasks, n_in, n_out, n_scratch, steps):
    if not tasks:
        return body
    t_in = sum(len(t.operands) for t in tasks)
    t_out = sum(len(t.out_shape) for t in tasks)

    def wrapped(*refs):
        ins, refs = refs[:n_in], refs[n_in:]
        t_ins, refs = refs[:t_in], refs[t_in:]
        outs, refs = refs[:n_out], refs[n_out:]
        t_outs, refs = refs[:t_out], refs[t_out:]
        scratch, t_sems = refs[:n_scratch], refs[n_scratch:]
        per_task = _task_refs(tasks, t_ins, t_outs, t_sems)
        step = pl.program_id(0)

        @pl.when(step == 0)
        def _():
            for t, r in zip(tasks, per_task):
                t.start(r)

        for t, r in zip(tasks, per_task):
            pl.when(step == max(steps - 1 - getattr(t, "lag", 0), 0))(functools.partial(t.middle, r))
            if hasattr(t, "late"):
                pl.when(step == steps - 1)(functools.partial(t.late, r))

        body(*ins, *outs, *scratch)

        @pl.when(step == steps - 1)
        def _():
            for t, r in zip(tasks, per_task):
                t.finish(r)

    return wrapped


def _exchange(name, tasks):
    operands, in_specs, out_shape, out_specs, sems = _task_args(tasks)

    def body(*refs):
        ni, no = len(operands), len(out_shape)
        per_task = _task_refs(tasks, refs[:ni], refs[ni:ni + no], refs[ni + no:])
        for phase in ("start", "middle", "late", "finish"):
            for t, r in zip(tasks, per_task):
                if hasattr(t, phase):
                    getattr(t, phase)(r)

    outs = pl.pallas_call(body, name=name, in_specs=in_specs, out_specs=out_specs, out_shape=out_shape,
                          scratch_shapes=sems)(*operands)
    return _task_results(tasks, outs)


def _row_tile(rows, cols, whole_up_to=256 * 1024):
    if rows * cols <= whole_up_to:
        return rows
    for t in (256, 176, 128, 64, 32, 16, 8):
        if rows % t == 0:
            return t
    return rows


def _pair_sum(name, partials, from_sibling):
    n = len(partials)
    _, rows, cols = partials[0].shape
    tile = _row_tile(rows, cols, 512 * 1024)

    def body(core_ref, *refs):
        for p_ref, s_ref, o_ref in zip(refs[:n], refs[n:2 * n], refs[2 * n:]):
            o_ref[0] = (p_ref[0, 0].astype(F32) + s_ref[0].astype(F32)).astype(o_ref.dtype)

    blk = pl.BlockSpec((1, tile, cols), lambda q, i, core: (q, i, 0))
    mine = pl.BlockSpec((1, 1, tile, cols), lambda q, i, core: (q, core[0], i, 0))
    return pl.pallas_call(
        body, name=name,
        grid_spec=pltpu.PrefetchScalarGridSpec(
            num_scalar_prefetch=1, grid=(4, rows // tile), in_specs=[mine] * n + [blk] * n, out_specs=[blk] * n),
        out_shape=[_sds(s.shape, s.dtype) for s in from_sibling],
        compiler_params=_params("parallel", "parallel"),
    )(lax.axis_index("c").astype(jnp.int32).reshape(1), *[p.reshape(4, 2, rows, cols) for p in partials],
      *from_sibling)


def _adamw_math(w, g, m, v):
    nm = ADAM_B1 * m + (1.0 - ADAM_B1) * g
    nv = ADAM_B2 * v + (1.0 - ADAM_B2) * (g * g)
    m_hat = nm / (1.0 - ADAM_B1 ** ADAM_STEP)
    v_hat = nv / (1.0 - ADAM_B2 ** ADAM_STEP)
    return -ADAM_LR * (m_hat / (jnp.sqrt(v_hat) + ADAM_EPS) + ADAM_WD * w), nm, nv


def _update_small(params, loss_parts):
    n = len(params)

    def whole(shape):
        return pl.BlockSpec(shape, lambda i, rank=len(shape): (0,) * rank)

    def total(ref):
        acc = ref[0]
        for d in range(1, N_DEV):
            acc = acc + ref[d]
        return acc

    def body(*refs):
        ins, loss_ref, outs, loss_out = refs[:4 * n], refs[4 * n], refs[4 * n + 1:8 * n + 1], refs[8 * n + 1]
        for p in range(n):
            s_ref, w_ref, m_ref, v_ref = ins[4 * p:4 * p + 4]
            g_ref, d_ref, nm_ref, nv_ref = outs[4 * p:4 * p + 4]
            g = total(s_ref)
            g_ref[...] = g
            d_ref[...], nm_ref[...], nv_ref[...] = _adamw_math(w_ref[...], g, m_ref[...], v_ref[...])
        loss_out[...] = total(loss_ref)

    outs = pl.pallas_call(
        body, name="update_small", grid=(1,),
        in_specs=[whole(t.shape) for p in params for t in p] + [whole(loss_parts.shape)],
        out_specs=[whole(p[1].shape) for p in params for _ in range(4)] + [whole(loss_parts.shape[1:])],
        out_shape=[_sds(p[1].shape, F32) for p in params for _ in range(4)] + [_sds(loss_parts.shape[1:], F32)],
        compiler_params=_params("arbitrary"),
    )(*[t for p in params for t in p], loss_parts)
    return [outs[4 * p:4 * p + 4] for p in range(n)], outs[4 * n]


def _sum_adamw(name, params, tasks=()):
    n = len(params)
    parts, rows, cols = params[0][0].shape
    tile = _row_tile(rows, cols)

    def body(*refs):
        ins, outs = refs[:4 * n], refs[4 * n:]
        for p in range(n):
            s_ref, w_ref, m_ref, v_ref = ins[4 * p:4 * p + 4]
            g_ref, d_ref, nm_ref, nv_ref = outs[4 * p:4 * p + 4]
            g = s_ref[0].astype(F32)
            for d in range(1, parts):
                g = g + s_ref[d].astype(F32)
            g_ref[...] = g
            d_ref[...], nm_ref[...], nv_ref[...] = _adamw_math(w_ref[...], g, m_ref[...], v_ref[...])

    blk = pl.BlockSpec((tile, cols), lambda i: (i, 0))
    outs, results = _grid_call(
        body, name, rows // tile,
        in_specs=([pl.BlockSpec((parts, tile, cols), lambda i: (0, i, 0))] + [blk] * 3) * n, out_specs=[blk] * (4 * n),
        out_shape=[_sds((rows, cols), F32)] * (4 * n),
        operands=[t for p in params for t in p], tasks=tasks)
    return [outs[4 * p:4 * p + 4] for p in range(n)], results


SMALL = ("b_gate", "w_pool", "pool_scale", "ln1_g", "ln1_b", "conv_b", "ln2_g", "ln2_b")
MIXER = ("w_branch_attn", "w_branch_pool", "w_out", "conv_w")
FFN = ("w_ffn_gate_t", "w_ffn_up_t", "w_ffn_down")


def _columns(t):
    return jnp.transpose(t, (1, 0, 2)).reshape(t.shape[1], N_DEV * t.shape[2])


def _row_blocks(t):
    return t.reshape(N_DEV * t.shape[1], t.shape[2])


def _by_owner(t):
    return t.reshape(N_DEV, t.shape[0] // N_DEV, t.shape[1])


def _reduce_halves(names, partials, from_sibling):
    out = [None] * len(names)
    for shape in dict.fromkeys(p.shape for p in partials):
        group = [i for i, p in enumerate(partials) if p.shape == shape]
        sums = _pair_sum("pair_sum_" + names[group[0]], [partials[i] for i in group], [from_sibling[i] for i in group])
        for i, s in zip(group, sums):
            out[i] = s
    return out


def _local_step(x, target, shards, small):
    seq = x.shape[0]
    cos, sin = _rope_tables(seq)
    cast, ((w_in_all,),) = _to_bf16([shards[n] for n in MIXER[:3] + FFN], tasks=[_AllGather([shards["w_in"]])])
    shards = {**shards, **dict(zip(MIXER[:3] + FFN, cast))}
    (xt, q, k, v, u, g, kmean), (mixer,) = _proj_in(
        x, w_in_all, small["b_gate"], cos, sin, tasks=[_AllGather([shards[n] for n in MIXER], lag=2)])
    wba, wbp, wout, conv_w = _columns(mixer[0]), _columns(mixer[1]), _row_blocks(mixer[2]), _columns(mixer[3])
    (o, lse, bias), ((wgt, wut),) = _attn_fwd(
        q, k, v, kmean.reshape(seq // MOBA_BLOCK, D_ATTN),
        tasks=[_AllGather([shards["w_ffn_gate_t"], shards["w_ffn_up_t"]], lag=1)])
    (ya, yp, pooled, mixed, ypre, merged, xhat1, rstd1, h1, h1b), _ = _mix(
        o, u, g, x, wba, wbp, wout, small["w_pool"], small["pool_scale"], small["ln1_g"], small["ln1_b"])
    wgt, wut = _row_blocks(wgt), _row_blocks(wut)
    (a, uf, act), ((wd,),) = _ffn_up(
        h1b, wgt, wut, conv_w, small["conv_b"], tasks=[_AllGather([shards["w_ffn_down"]], lag=4)])
    wd = _row_blocks(wd)
    dr2, dr2b, loss, dg2, db2 = _ffn_down(act, wd, h1, target, small["ln2_g"], small["ln2_b"])

    da, du, dwd, dwg, dwu, dconv = _ffn_bwd(dr2b, h1b, a, uf, wd, conv_w, small["conv_b"])
    ffn_partials = [_by_owner(dwg), _by_owner(dwu), _by_owner(dwd)]
    (dr1, dr1b, dg1, db1), (ffn_sibling,) = _ln1_bwd(
        dr2, da, du, wgt, wut, xhat1, rstd1, small["ln1_g"], tasks=[_SiblingSend(ffn_partials)])
    ffn_chip = _reduce_halves(FFN, ffn_partials, ffn_sibling)
    (dzg, dya, dyp, do, dmixed, dpooled, dbg, dps), (gate_landed,) = _mix_bwd(
        dr1b, ya, yp, g, mixed, wout, wba, wbp, small["w_pool"], small["pool_scale"],
        tasks=[_ChipScatter(ffn_chip[0:1])])
    (dw_ba, dw_bp, dw_out, dw_pool), _ = _dw_mixers(o, ypre, merged, dya, dyp, dr1b, pooled, dmixed)
    mixer_partials = [dw_ba, dw_bp, dw_out]
    (dq, dk, dv), (up_down_landed, mixer_sibling) = _attn_bwd(
        q, k, v, bias, o, lse, do, cos, sin, tasks=[_ChipScatter(ffn_chip[1:3]), _SiblingSend(mixer_partials)])
    mixer_chip = _reduce_halves(MIXER[:3], mixer_partials, mixer_sibling)
    (grad_x, dz), _ = _in_bwd(dq, dk, dv, dpooled, dzg, dr1, w_in_all)
    little = [dbg, dw_pool, dps, dg1, db1, dconv[3:4], dg2, db2]
    conv_w_partials = dconv[0:3].reshape(3, N_DEV, FF_SHARD).transpose(1, 0, 2)
    dw_in, (mixer_landed,) = _matmul(
        "dw_in", xt, dz, (N_DEV, D_MODEL, D_ATTN), BF16, N_DEV,
        _full(xt.shape), pl.BlockSpec((seq, D_ATTN), lambda n: (0, n)),
        pl.BlockSpec((1, D_MODEL, D_ATTN), lambda n: (n, 0, 0)), tasks=[_ChipScatter(mixer_chip)])

    landed = dict(zip(FFN + MIXER[:3], gate_landed + up_down_landed + mixer_landed))
    return grad_x, landed, dw_in, little, conv_w_partials, loss


def kernel(x, w_in, b_gate, w_branch_attn, w_pool, pool_scale, w_branch_pool, w_out, ln1_g, ln1_b, w_ffn_gate, w_ffn_up, conv_w, conv_b, w_ffn_down, ln2_g, ln2_b, loss_target, m_w_in, m_b_gate, m_w_branch_attn, m_w_pool, m_pool_scale, m_w_branch_pool, m_w_out, m_ln1_g, m_ln1_b, m_w_ffn_gate, m_w_ffn_up, m_conv_w, m_conv_b, m_w_ffn_down, m_ln2_g, m_ln2_b, v_w_in, v_b_gate, v_w_branch_attn, v_w_pool, v_pool_scale, v_w_branch_pool, v_w_out, v_ln1_g, v_ln1_b, v_w_ffn_gate, v_w_ffn_up, v_conv_w, v_conv_b, v_w_ffn_down, v_ln2_g, v_ln2_b):
    weights = dict(w_in=w_in, b_gate=b_gate, w_branch_attn=w_branch_attn, w_pool=w_pool, pool_scale=pool_scale,
                   w_branch_pool=w_branch_pool, w_out=w_out, ln1_g=ln1_g, ln1_b=ln1_b, w_ffn_gate=w_ffn_gate,
                   w_ffn_up=w_ffn_up, conv_w=conv_w, conv_b=conv_b, w_ffn_down=w_ffn_down, ln2_g=ln2_g, ln2_b=ln2_b)
    m_in = dict(w_in=m_w_in, b_gate=m_b_gate, w_branch_attn=m_w_branch_attn, w_pool=m_w_pool,
                pool_scale=m_pool_scale, w_branch_pool=m_w_branch_pool, w_out=m_w_out, ln1_g=m_ln1_g, ln1_b=m_ln1_b,
                w_ffn_gate=m_w_ffn_gate, w_ffn_up=m_w_ffn_up, conv_w=m_conv_w, conv_b=m_conv_b,
                w_ffn_down=m_w_ffn_down, ln2_g=m_ln2_g, ln2_b=m_ln2_b)
    v_in = dict(w_in=v_w_in, b_gate=v_b_gate, w_branch_attn=v_w_branch_attn, w_pool=v_w_pool,
                pool_scale=v_pool_scale, w_branch_pool=v_w_branch_pool, w_out=v_w_out, ln1_g=v_ln1_g, ln1_b=v_ln1_b,
                w_ffn_gate=v_w_ffn_gate, w_ffn_up=v_w_ffn_up, conv_w=v_conv_w, conv_b=v_conv_b,
                w_ffn_down=v_w_ffn_down, ln2_g=v_ln2_g, ln2_b=v_ln2_b)
    weights = {n: a[0] for n, a in weights.items()}
    m_in = {n: a[0] for n, a in m_in.items()}
    v_in = {n: a[0] for n, a in v_in.items()}

    shards = {"w_in": weights["w_in"].astype(BF16), "w_branch_attn": weights["w_branch_attn"],
              "w_branch_pool": weights["w_branch_pool"], "w_out": weights["w_out"],
              "w_ffn_gate_t": weights["w_ffn_gate"].T, "w_ffn_up_t": weights["w_ffn_up"].T,
              "w_ffn_down": weights["w_ffn_down"], "conv_w": weights["conv_w"]}
    small = {"b_gate": weights["b_gate"][None], "w_pool": weights["w_pool"], "pool_scale": weights["pool_scale"][None],
             "ln1_g": weights["ln1_g"][None], "ln1_b": weights["ln1_b"][None], "conv_b": weights["conv_b"][None],
             "ln2_g": weights["ln2_g"][None], "ln2_b": weights["ln2_b"][None]}

    grad_x, landed, dw_in, little, conv_w_partials, loss_part = _local_step(x[0], loss_target[0], shards, small)

    grads, delta, new_m, new_v = {}, {}, {}, {}

    def param(n, transposed=False):
        if transposed:
            return landed[n + "_t"], weights[n].T, m_in[n].T, v_in[n].T
        return landed[n], weights[n], m_in[n], v_in[n]

    def keep(n, updated, transposed=False):
        grads[n], delta[n], new_m[n], new_v[n] = (t.T for t in updated) if transposed else updated

    ((w_in_sibling,),) = _exchange("sibling_grads", [_SiblingSend([dw_in])])
    w_in_chip = _reduce_halves(["w_in"], [dw_in], [w_in_sibling])
    (landed["w_in"],), (*small_all, loss_all), (conv_w_all,) = _exchange(
        "scatter_grads",
        [_ChipScatter(w_in_chip), _AllGather(little + [loss_part]), _DirectScatter([conv_w_partials])])
    (gate, up, down), _ = _sum_adamw(
        "update_w_ffn", [param("w_ffn_gate", True), param("w_ffn_up", True), param("w_ffn_down")])
    keep("w_ffn_gate", gate, True)
    keep("w_ffn_up", up, True)
    keep("w_ffn_down", down)
    (attn, pool), _ = _sum_adamw("update_w_branch", [param("w_branch_attn"), param("w_branch_pool")])
    keep("w_branch_attn", attn)
    keep("w_branch_pool", pool)
    for n in ("w_out", "w_in"):
        (updated,), _ = _sum_adamw("update_" + n, [param(n)])
        keep(n, updated)
    names = SMALL + ("conv_w",)
    rows = lambda t: t if t.ndim > 1 else t[None]
    small_updates, loss = _update_small(
        [(s, rows(weights[n]), rows(m_in[n]), rows(v_in[n])) for n, s in zip(names, small_all + [conv_w_all])],
        loss_all)
    for n, updated in zip(names, small_updates):
        keep(n, [t.reshape(weights[n].shape) for t in updated])
    loss = loss[0, 0]

    order = ("w_in", "b_gate", "w_branch_attn", "w_pool", "pool_scale", "w_branch_pool", "w_out", "ln1_g", "ln1_b",
             "w_ffn_gate", "w_ffn_up", "conv_w", "conv_b", "w_ffn_down", "ln2_g", "ln2_b")
    lead = lambda t: t[None]
    return (loss, lead(grad_x), *[lead(grads[n]) for n in order], *[lead(delta[n]) for n in order],
            *[lead(new_m[n]) for n in order], *[lead(new_v[n]) for n in order])
```

```python
import functools
import math

import jax
import jax.numpy as jnp
from jax import lax
from jax.experimental import pallas as pl
from jax.experimental.pallas import tpu as pltpu

F32 = jnp.float32
BF16 = jnp.bfloat16

D_MODEL = 1024
N_HEADS = 8
HEAD_DIM = 64
D_ATTN = N_HEADS * HEAD_DIM
MOBA_BLOCK = 256
MOBA_TOPK = 3
ROPE_THETA = 10000.0
POOL_WINDOWS = (2, 4, 8, 16)
POOL_GROUP = 128
D_POOL = len(POOL_WINDOWS) * POOL_GROUP
POOL_HALO = 16
D_FF = 2816
D_IN_PROJ = 3 * D_ATTN + D_POOL + 2 * D_MODEL
LN_EPS = 1e-5
ALPHA = 2.0 ** 0.25
NEG = -1e30
N_DEV = 8
FF_SHARD = D_FF // N_DEV

ADAM_LR = 0.001
ADAM_B1 = 0.9
ADAM_B2 = 0.999
ADAM_EPS = 1e-08
ADAM_WD = 0.01
ADAM_STEP = 10

TOK = 256
FF_CHUNK = 256
LANE = 128
VMEM_LIMIT = 56 * 1024 * 1024

MESH = pl.DeviceIdType.MESH
NT_DIMS = (((1,), (1,)), ((), ()))
TN_DIMS = (((0,), (0,)), ((), ()))


def _params(*sem):
    return pltpu.CompilerParams(dimension_semantics=sem or None, vmem_limit_bytes=VMEM_LIMIT)


def _full(shape):
    zeros = (0,) * len(shape)
    return pl.BlockSpec(shape, lambda *_: zeros, pipeline_mode=pl.Buffered(1))


def _rows(width, tile=TOK):
    return pl.BlockSpec((tile, width), lambda i: (i, 0))


def _sds(shape, dtype):
    return jax.ShapeDtypeStruct(shape, dtype)


def _dot(a, b):
    return jnp.dot(a, b, preferred_element_type=F32)


def _dot_nt(a, b):
    return lax.dot_general(a, b, NT_DIMS, preferred_element_type=F32)


def _dot_tn(a, b):
    return lax.dot_general(a, b, TN_DIMS, preferred_element_type=F32)


def _rope_tables(seq):
    half = HEAD_DIM // 2
    inv_freq = 1.0 / (ROPE_THETA ** (jnp.arange(half, dtype=F32) / half))
    ang = jnp.arange(seq, dtype=F32)[:, None] * inv_freq[None, :]
    cos, sin = jnp.cos(ang), jnp.sin(ang)
    return jnp.tile(cos, (1, 4)), jnp.tile(jnp.concatenate([-sin, sin], axis=1), (1, 2))


def _swap_halves(t):
    lane = lax.broadcasted_iota(jnp.int32, t.shape, 1)
    return jnp.where((lane % HEAD_DIM) < HEAD_DIM // 2, pltpu.roll(t, LANE - 32, 1), pltpu.roll(t, 32, 1))


def _rope(t, cos, sin):
    return t * cos + _swap_halves(t) * sin


def _rope_transposed(g, cos, sin):
    return g * cos + _swap_halves(g * sin)


def _ln_fwd(r, g, b):
    mu = jnp.mean(r, axis=-1, keepdims=True)
    xc = r - mu
    var = jnp.mean(xc * xc, axis=-1, keepdims=True)
    rstd = lax.rsqrt(var + LN_EPS)
    xhat = xc * rstd
    return xhat * g + b, xhat, rstd


def _ln_bwd(dy, xhat, rstd, g):
    dxh = dy * g
    m1 = jnp.mean(dxh, axis=-1, keepdims=True)
    m2 = jnp.mean(dxh * xhat, axis=-1, keepdims=True)
    return rstd * (dxh - m1 - xhat * m2)


def _normal_cdf(a):
    return 0.5 * (1.0 + lax.erf(a * (1.0 / math.sqrt(2.0))))


def _gelu_derivative(a, cdf):
    return cdf + a * (jnp.exp(-0.5 * a * a) * (1.0 / math.sqrt(2.0 * math.pi)))


def _shift_down(a, k):
    row = lax.broadcasted_iota(jnp.int32, a.shape, 0)
    return jnp.where(row >= k, pltpu.roll(a, k, 0), 0.0)


def _shift_up(a, k):
    n = a.shape[0]
    row = lax.broadcasted_iota(jnp.int32, a.shape, 0)
    return jnp.where(row < n - k, pltpu.roll(a, n - k, 0), 0.0)


def _conv(a, cw, cb):
    return cw[2:3, :] * a + cw[1:2, :] * _shift_down(a, 1) + cw[0:1, :] * _shift_down(a, 2) + cb


def _pool_count(first_row, rows, window):
    t = first_row + lax.broadcasted_iota(jnp.int32, (rows, 1), 0)
    return jnp.minimum(t + 1, window).astype(F32)


def _grid_call(body, name, steps, in_specs, out_specs, out_shape, operands, scratch=(), tasks=()):
    t_operands, t_in_specs, t_out_shape, t_out_specs, t_sems = _task_args(tasks)
    outs = pl.pallas_call(
        _carry(body, tasks, len(in_specs), len(out_specs), len(scratch), steps), name=name, grid=(steps,),
        in_specs=list(in_specs) + t_in_specs, out_specs=list(out_specs) + t_out_specs,
        out_shape=list(out_shape) + t_out_shape, scratch_shapes=list(scratch) + t_sems,
        compiler_params=_params("arbitrary"),
    )(*operands, *t_operands)
    return outs[:len(out_specs)], _task_results(tasks, outs[len(out_specs):])


def _proj_in(x, win, b_gate, cos, sin, tasks=()):
    seq = x.shape[0]
    nt = seq // TOK

    def body(x_ref, win_ref, bg_ref, cos_ref, sin_ref, xt_ref, q_ref, k_ref, v_ref, u_ref, g_ref, km_ref):
        xb = x_ref[...].astype(BF16)
        xt_ref[...] = x_ref[...].T.astype(BF16)
        cos_t, sin_t = cos_ref[...], sin_ref[...]
        for sec, out_ref in ((0, q_ref), (1, k_ref)):
            z = _dot(xb, win_ref[sec])
            for c in range(D_ATTN // LANE):
                cols = slice(LANE * c, LANE * (c + 1))
                out_ref[:, cols] = _rope(z[:, cols], cos_t, sin_t)
        for b in range(TOK // MOBA_BLOCK):
            km_ref[b] = jnp.mean(k_ref[MOBA_BLOCK * b:MOBA_BLOCK * (b + 1), :], axis=0, keepdims=True)
        v_ref[...] = _dot(xb, win_ref[2]).astype(BF16)
        u_ref[...] = _dot(xb, win_ref[3])
        for n in range(4):
            cols = slice(D_ATTN * n, D_ATTN * (n + 1))
            g_ref[:, cols] = jax.nn.sigmoid(_dot(xb, win_ref[4 + n]) + bg_ref[:, cols])

    return _grid_call(
        body, "proj_in", nt,
        in_specs=[_rows(D_MODEL), _full(win.shape), _full((1, 2 * D_MODEL)), _rows(LANE), _rows(LANE)],
        out_specs=[pl.BlockSpec((D_MODEL, TOK), lambda i: (0, i)), _rows(D_ATTN), _rows(D_ATTN), _rows(D_ATTN),
                   _rows(D_POOL), _rows(2 * D_MODEL),
                   pl.BlockSpec((TOK // MOBA_BLOCK, 1, D_ATTN), lambda i: (i, 0, 0))],
        out_shape=[_sds((D_MODEL, seq), BF16), _sds((seq, D_ATTN), F32), _sds((seq, D_ATTN), F32),
                   _sds((seq, D_ATTN), BF16), _sds((seq, D_POOL), F32), _sds((seq, 2 * D_MODEL), F32),
                   _sds((seq // MOBA_BLOCK, 1, D_ATTN), F32)],
        operands=(x, win, b_gate, cos, sin), tasks=tasks)


SCORE_CHUNK = 128


def _store_keys(ka_sc, k_ref, ls):
    seq = ka_sc.shape[0]
    ka_sc[:, 0:HEAD_DIM] = k_ref[:, ls].astype(BF16)
    row = lax.broadcasted_iota(jnp.int32, (seq, HEAD_DIM), 0)
    lane = lax.broadcasted_iota(jnp.int32, (seq, HEAD_DIM), 1)
    in_block = (lane * MOBA_BLOCK <= row) & (row < (lane + 1) * MOBA_BLOCK)
    ka_sc[:, HEAD_DIM:] = jnp.where(in_block, 1.0, 0.0).astype(BF16)


def _block_bias(qf, km, i):
    if i <= MOBA_TOPK:
        return jnp.zeros((MOBA_BLOCK, HEAD_DIM), BF16)
    nb = km.shape[0]
    gate = lax.dot_general(km, qf, NT_DIMS, precision=lax.Precision.HIGHEST, preferred_element_type=F32)
    blk = lax.broadcasted_iota(jnp.int32, gate.shape, 0)
    rank = jnp.zeros(gate.shape, F32)
    for r in range(1, i):
        lower = pltpu.roll(gate, r, 0)
        rank = rank + jnp.where((blk >= r) & (lower >= gate), 1.0, 0.0)
        higher = pltpu.roll(gate, nb - r, 0)
        rank = rank + jnp.where((blk + r < i) & (higher > gate), 1.0, 0.0)
    bias = jnp.where((blk < i) & (rank >= MOBA_TOPK), NEG, 0.0)
    padded = jnp.concatenate([bias, jnp.zeros((LANE - nb, MOBA_BLOCK), F32)], axis=0)
    return jnp.transpose(padded)[:, 0:HEAD_DIM].astype(BF16)


def _causal(shape, transposed=False):
    row = lax.broadcasted_iota(jnp.int32, shape, 0)
    col = lax.broadcasted_iota(jnp.int32, shape, 1)
    return (row <= col) if transposed else (col <= row)


def _row_vector(col):
    return jnp.transpose(jnp.broadcast_to(col, (MOBA_BLOCK, LANE)))[0:1, :]


def _attn_fwd(q, k, v, kmean, tasks=()):
    seq = q.shape[0]
    nb = seq // MOBA_BLOCK
    assert nb == 8, "the block ranking keeps one sublane per key block"
    pair = pl.BlockSpec((seq, LANE), lambda p: (0, p))
    heads = LANE // HEAD_DIM

    def body(q_ref, k_ref, v_ref, km_ref, o_ref, lse_ref, bias_ref, ka_sc, qa_sc, s_sc, p_sc):
        lse_ref[0, heads:, :] = jnp.zeros((8 - heads, seq), F32)
        for hh in range(heads):
            ls = slice(HEAD_DIM * hh, HEAD_DIM * (hh + 1))
            _store_keys(ka_sc, k_ref, ls)
            vb = v_ref[:, ls]
            km = km_ref[:, ls]
            for i in range(nb):
                rs = slice(MOBA_BLOCK * i, MOBA_BLOCK * (i + 1))
                width = MOBA_BLOCK * (i + 1)
                qf = q_ref[rs, ls]
                bias = _block_bias(qf, km, i)
                bias_ref[rs, ls] = bias
                qa_sc[:, 0:HEAD_DIM] = (qf * HEAD_DIM ** -0.5).astype(BF16)
                qa_sc[:, HEAD_DIM:] = bias
                s_sc[:, 0:width] = _dot_nt(qa_sc[...], ka_sc[0:width, :])
                s_sc[:, rs] = jnp.where(_causal((MOBA_BLOCK, MOBA_BLOCK)), s_sc[:, rs], NEG)
                chunks = [slice(SCORE_CHUNK * c, SCORE_CHUNK * (c + 1)) for c in range(width // SCORE_CHUNK)]
                top = s_sc[:, chunks[0]]
                for c in chunks[1:]:
                    top = jnp.maximum(top, s_sc[:, c])
                m = jnp.max(top, axis=1, keepdims=True)
                total = jnp.zeros((MOBA_BLOCK, SCORE_CHUNK), F32)
                for c in chunks:
                    p = jnp.exp(s_sc[:, c] - m)
                    total = total + p
                    p_sc[:, c] = p.astype(BF16)
                l = jnp.sum(total, axis=1, keepdims=True)
                o_ref[rs, ls] = _dot(p_sc[:, 0:width], vb[0:width]) / l
                lse_ref[0, hh:hh + 1, rs] = _row_vector(m + jnp.log(l))

    return _grid_call(
        body, "attn_fwd", D_ATTN // LANE,
        in_specs=[pair, pair, pair, pl.BlockSpec((nb, LANE), lambda p: (0, p))],
        out_specs=[pair, pl.BlockSpec((1, 8, seq), lambda p: (p, 0, 0)), pair],
        out_shape=[_sds((seq, D_ATTN), F32), _sds((D_ATTN // LANE, 8, seq), F32), _sds((seq, D_ATTN), BF16)],
        operands=(q, k, v, kmean),
        scratch=[pltpu.VMEM((seq, LANE), BF16), pltpu.VMEM((MOBA_BLOCK, LANE), BF16),
                 pltpu.VMEM((MOBA_BLOCK, seq), F32), pltpu.VMEM((MOBA_BLOCK, seq), BF16)],
        tasks=tasks)


def _mix(o, u, g, x, wba, wbp, wout, w_pool, pool_scale, ln_g, ln_b, tasks=()):
    seq = x.shape[0]

    def body(o_ref, u_ref, uprev_ref, g_ref, x_ref, wba_ref, wbp_ref, wout_ref, wp_ref, ps_ref, lg_ref, lb_ref,
             ya_ref, yp_ref, pooled_ref, mixed_ref, ypre_ref, merged_ref, xhat_ref, rstd_ref, h_ref, hb_ref, ext):
        i = pl.program_id(0)
        ya = _dot(o_ref[...].astype(BF16), wba_ref[...])
        ucur = u_ref[...]
        ext[0:POOL_HALO, :] = jnp.where(i > 0, uprev_ref[...], 0.0)
        ext[POOL_HALO:, :] = ucur
        for grp, window in enumerate(POOL_WINDOWS):
            cols = slice(POOL_GROUP * grp, POOL_GROUP * (grp + 1))
            acc = ucur[:, cols]
            for kk in range(1, window):
                acc = acc + ext[pl.ds(POOL_HALO - kk, TOK), cols]
            pooled = acc / _pool_count(i * TOK, TOK, window) - ucur[:, cols]
            pooled_ref[:, cols] = pooled.astype(BF16)
            mixed_ref[:, cols] = _dot(pooled.astype(BF16), wp_ref[grp].astype(BF16))
        mixed = mixed_ref[...]
        ypre = (mixed * ps_ref[...]).astype(BF16)
        ypre_ref[...] = ypre
        yp = _dot(ypre, wbp_ref[...])
        ya_ref[...] = ya
        yp_ref[...] = yp
        merged = (g_ref[:, :D_MODEL] * ya + g_ref[:, D_MODEL:] * yp).astype(BF16)
        merged_ref[...] = merged
        r1 = ALPHA * x_ref[...] + _dot(merged, wout_ref[...])
        h, xhat, rstd = _ln_fwd(r1, lg_ref[...], lb_ref[...])
        xhat_ref[...] = xhat
        rstd_ref[...] = jnp.broadcast_to(rstd, (TOK, LANE))
        h_ref[...] = h
        hb_ref[...] = h.astype(BF16)

    halo = pl.BlockSpec((POOL_HALO, D_POOL), lambda i: (jnp.maximum(i * (TOK // POOL_HALO) - 1, 0), 0))
    return _grid_call(
        body, "mix", seq // TOK,
        in_specs=[_rows(D_ATTN), _rows(D_POOL), halo, _rows(2 * D_MODEL), _rows(D_MODEL),
                  _full(wba.shape), _full(wbp.shape), _full(wout.shape), _full(w_pool.shape),
                  _full((1, D_POOL)), _full((1, D_MODEL)), _full((1, D_MODEL))],
        out_specs=[_rows(D_MODEL), _rows(D_MODEL), _rows(D_POOL), _rows(D_POOL), _rows(D_POOL), _rows(D_MODEL),
                   _rows(D_MODEL), _rows(LANE), _rows(D_MODEL), _rows(D_MODEL)],
        out_shape=[_sds((seq, D_MODEL), F32), _sds((seq, D_MODEL), F32), _sds((seq, D_POOL), BF16),
                   _sds((seq, D_POOL), F32), _sds((seq, D_POOL), BF16), _sds((seq, D_MODEL), BF16),
                   _sds((seq, D_MODEL), F32), _sds((seq, LANE), F32), _sds((seq, D_MODEL), F32),
                   _sds((seq, D_MODEL), BF16)],
        operands=(o, u, u, g, x, wba, wbp, wout, w_pool, pool_scale, ln_g, ln_b),
        scratch=[pltpu.VMEM((TOK + POOL_HALO, D_POOL), F32)], tasks=tasks)


def _ffn_up(hb, wgt, wut, conv_w, conv_b, tasks=()):
    seq = hb.shape[0]
    wblk = pl.BlockSpec((FF_CHUNK, D_MODEL), lambda c: (c, 0))
    cblk = lambda rows: pl.BlockSpec((rows, FF_CHUNK), lambda c: (0, c))
    oblk = pl.BlockSpec((seq, FF_CHUNK), lambda c: (0, c))

    def body(h_ref, wg_ref, wu_ref, cw_ref, cb_ref, a_ref, u_ref, act_ref):
        h = h_ref[...]
        a = _dot_nt(h, wg_ref[...])
        u = _dot_nt(h, wu_ref[...])
        a_ref[...] = a
        u_ref[...] = u
        ac = _conv(a, cw_ref[...], cb_ref[...])
        act_ref[...] = (ac * _normal_cdf(ac) * u).astype(BF16)

    return _grid_call(
        body, "ffn_up", D_FF // FF_CHUNK,
        in_specs=[_full(hb.shape), wblk, wblk, cblk(3), cblk(1)],
        out_specs=[oblk, oblk, oblk],
        out_shape=[_sds((seq, D_FF), F32), _sds((seq, D_FF), F32), _sds((seq, D_FF), BF16)],
        operands=(hb, wgt, wut, conv_w, conv_b), tasks=tasks)


def _ffn_down(act, wd, h, target, ln_g, ln_b):
    seq = h.shape[0]

    def body(act_ref, wd_ref, h_ref, t_ref, lg_ref, lb_ref, dr_ref, drb_ref, loss_ref, dg_ref, db_ref):
        i = pl.program_id(0)

        @pl.when(i == 0)
        def _():
            loss_ref[...] = jnp.zeros_like(loss_ref)
            dg_ref[...] = jnp.zeros_like(dg_ref)
            db_ref[...] = jnp.zeros_like(db_ref)

        r2 = ALPHA * h_ref[...] + _dot(act_ref[...], wd_ref[...])
        y, xhat, rstd = _ln_fwd(r2, lg_ref[...], lb_ref[...])
        diff = y - t_ref[...]
        loss_ref[...] += jnp.sum(diff * diff) * (0.5 / D_MODEL)
        dy = diff * (1.0 / D_MODEL)
        dg_ref[...] += jnp.sum(dy * xhat, axis=0, keepdims=True)
        db_ref[...] += jnp.sum(dy, axis=0, keepdims=True)
        dr = _ln_bwd(dy, xhat, rstd, lg_ref[...])
        dr_ref[...] = dr
        drb_ref[...] = dr.astype(BF16)

    vec = pl.BlockSpec((1, D_MODEL), lambda i: (0, 0))
    return pl.pallas_call(
        body, name="ffn_down", grid=(seq // TOK,),
        in_specs=[_rows(D_FF), _full(wd.shape), _rows(D_MODEL), _rows(D_MODEL), _full((1, D_MODEL)), _full((1, D_MODEL))],
        out_specs=[_rows(D_MODEL), _rows(D_MODEL), pl.BlockSpec((8, LANE), lambda i: (0, 0)), vec, vec],
        out_shape=[_sds((seq, D_MODEL), F32), _sds((seq, D_MODEL), BF16), _sds((8, LANE), F32),
                   _sds((1, D_MODEL), F32), _sds((1, D_MODEL), F32)],
        compiler_params=_params("arbitrary"),
    )(act, wd, h, target, ln_g, ln_b)


def _ffn_bwd(drb, hb, a, u, wd, conv_w, conv_b):
    seq = hb.shape[0]
    wblk = pl.BlockSpec((FF_CHUNK, D_MODEL), lambda c: (c, 0))
    cblk = lambda rows: pl.BlockSpec((rows, FF_CHUNK), lambda c: (0, c))
    sblk = pl.BlockSpec((seq, FF_CHUNK), lambda c: (0, c))

    def body(dr_ref, h_ref, a_ref, u_ref, wd_ref, cw_ref, cb_ref, da_ref, du_ref, dwd_ref, dwg_ref, dwu_ref, dc_ref):
        dr = dr_ref[...]
        h = h_ref[...]
        a = a_ref[...]
        u = u_ref[...]
        cw = cw_ref[...]
        dact = _dot_nt(dr, wd_ref[...])
        ac = _conv(a, cw, cb_ref[...])
        cdf = _normal_cdf(ac)
        gelu = ac * cdf
        dwd_ref[...] = _dot_tn((gelu * u).astype(BF16), dr).astype(BF16)
        du = (dact * gelu).astype(BF16)
        dac = dact * u * _gelu_derivative(ac, cdf)
        da = (cw[2:3, :] * dac + cw[1:2, :] * _shift_up(dac, 1) + cw[0:1, :] * _shift_up(dac, 2)).astype(BF16)
        da_ref[...] = da
        du_ref[...] = du
        dwg_ref[...] = _dot_tn(da, h).astype(BF16)
        dwu_ref[...] = _dot_tn(du, h).astype(BF16)
        dc_ref[0:1, :] = jnp.sum(dac * _shift_down(a, 2), axis=0, keepdims=True)
        dc_ref[1:2, :] = jnp.sum(dac * _shift_down(a, 1), axis=0, keepdims=True)
        dc_ref[2:3, :] = jnp.sum(dac * a, axis=0, keepdims=True)
        dc_ref[3:4, :] = jnp.sum(dac, axis=0, keepdims=True)
        dc_ref[4:8, :] = jnp.zeros((4, FF_CHUNK), F32)

    return pl.pallas_call(
        body, name="ffn_bwd", grid=(D_FF // FF_CHUNK,),
        in_specs=[_full(drb.shape), _full(hb.shape), sblk, sblk, wblk, cblk(3), cblk(1)],
        out_specs=[sblk, sblk, wblk, wblk, wblk, cblk(8)],
        out_shape=[_sds((seq, D_FF), BF16), _sds((seq, D_FF), BF16), _sds((D_FF, D_MODEL), BF16),
                   _sds((D_FF, D_MODEL), BF16), _sds((D_FF, D_MODEL), BF16), _sds((8, D_FF), F32)],
        compiler_params=_params("parallel"),
    )(drb, hb, a, u, wd, conv_w, conv_b)


def _ln1_bwd(dr2, da, du, wgt, wut, xhat, rstd, ln_g, tasks=()):
    seq = dr2.shape[0]

    def body(dr2_ref, da_ref, du_ref, wg_ref, wu_ref, xhat_ref, rstd_ref, lg_ref, dr_ref, drb_ref, dg_ref, db_ref):
        @pl.when(pl.program_id(0) == 0)
        def _():
            dg_ref[...] = jnp.zeros_like(dg_ref)
            db_ref[...] = jnp.zeros_like(db_ref)

        dh = ALPHA * dr2_ref[...] + _dot(da_ref[...], wg_ref[...]) + _dot(du_ref[...], wu_ref[...])
        xhat = xhat_ref[...]
        dg_ref[...] += jnp.sum(dh * xhat, axis=0, keepdims=True)
        db_ref[...] += jnp.sum(dh, axis=0, keepdims=True)
        dr = _ln_bwd(dh, xhat, rstd_ref[:, 0:1], lg_ref[...])
        dr_ref[...] = dr
        drb_ref[...] = dr.astype(BF16)

    vec = pl.BlockSpec((1, D_MODEL), lambda i: (0, 0))
    return _grid_call(
        body, "ln1_bwd", seq // TOK,
        in_specs=[_rows(D_MODEL), _rows(D_FF), _rows(D_FF), _full(wgt.shape), _full(wut.shape), _rows(D_MODEL),
                  _rows(LANE), _full((1, D_MODEL))],
        out_specs=[_rows(D_MODEL), _rows(D_MODEL), vec, vec],
        out_shape=[_sds((seq, D_MODEL), F32), _sds((seq, D_MODEL), BF16), _sds((1, D_MODEL), F32),
                   _sds((1, D_MODEL), F32)],
        operands=(dr2, da, du, wgt, wut, xhat, rstd, ln_g), tasks=tasks)


def _mix_bwd(drb, ya, yp, g, mixed, wout, wba, wbp, w_pool, pool_scale, tasks=()):
    seq = drb.shape[0]

    def body(dr_ref, ya_ref, yp_ref, g_ref, mixed_ref, wout_ref, wba_ref, wbp_ref, wp_ref, ps_ref,
             dzg_ref, dya_ref, dyp_ref, do_ref, dmixed_ref, dpooled_ref, dbg_ref, dps_ref):
        @pl.when(pl.program_id(0) == 0)
        def _():
            dbg_ref[...] = jnp.zeros_like(dbg_ref)
            dps_ref[...] = jnp.zeros_like(dps_ref)

        dmerged = _dot_nt(dr_ref[...], wout_ref[...])
        ga, gp = g_ref[:, :D_MODEL], g_ref[:, D_MODEL:]
        dzga = dmerged * ya_ref[...] * ga * (1.0 - ga)
        dzgp = dmerged * yp_ref[...] * gp * (1.0 - gp)
        dzg_ref[:, :D_MODEL] = dzga.astype(BF16)
        dzg_ref[:, D_MODEL:] = dzgp.astype(BF16)
        dbg_ref[:, :D_MODEL] += jnp.sum(dzga, axis=0, keepdims=True)
        dbg_ref[:, D_MODEL:] += jnp.sum(dzgp, axis=0, keepdims=True)
        dya = (dmerged * ga).astype(BF16)
        dyp = (dmerged * gp).astype(BF16)
        dya_ref[...] = dya
        dyp_ref[...] = dyp
        do_ref[...] = _dot_nt(dya, wba_ref[...])
        dypre = _dot_nt(dyp, wbp_ref[...])
        dps_ref[...] += jnp.sum(dypre * mixed_ref[...], axis=0, keepdims=True)
        dmixed = (dypre * ps_ref[...]).astype(BF16)
        dmixed_ref[...] = dmixed
        for grp in range(len(POOL_WINDOWS)):
            cols = slice(POOL_GROUP * grp, POOL_GROUP * (grp + 1))
            dpooled_ref[:, cols] = _dot_nt(dmixed[:, cols], wp_ref[grp].astype(BF16))

    return _grid_call(
        body, "mix_bwd", seq // TOK,
        in_specs=[_rows(D_MODEL), _rows(D_MODEL), _rows(D_MODEL), _rows(2 * D_MODEL), _rows(D_POOL),
                  _full(wout.shape), _full(wba.shape), _full(wbp.shape), _full(w_pool.shape), _full((1, D_POOL))],
        out_specs=[_rows(2 * D_MODEL), _rows(D_MODEL), _rows(D_MODEL), _rows(D_ATTN), _rows(D_POOL), _rows(D_POOL),
                   pl.BlockSpec((1, 2 * D_MODEL), lambda i: (0, 0)), pl.BlockSpec((1, D_POOL), lambda i: (0, 0))],
        out_shape=[_sds((seq, 2 * D_MODEL), BF16), _sds((seq, D_MODEL), BF16), _sds((seq, D_MODEL), BF16),
                   _sds((seq, D_ATTN), F32), _sds((seq, D_POOL), BF16), _sds((seq, D_POOL), F32),
                   _sds((1, 2 * D_MODEL), F32), _sds((1, D_POOL), F32)],
        operands=(drb, ya, yp, g, mixed, wout, wba, wbp, w_pool, pool_scale), tasks=tasks)


def _attn_bwd(q, k, v, bias, o, lse, do, cos, sin, tasks=()):
    seq = q.shape[0]
    nb = seq // MOBA_BLOCK
    pair = pl.BlockSpec((seq, LANE), lambda p: (0, p))
    table = pl.BlockSpec((seq, LANE), lambda p: (0, 0))
    scale = HEAD_DIM ** -0.5

    def body(q_ref, k_ref, v_ref, bias_ref, o_ref, lse_ref, do_ref, cos_ref, sin_ref, dq_ref, dk_ref, dv_ref,
             dq_acc, dk_acc, dv_acc, dk_head, dv_head, ka_sc, qa_sc, s_sc, dp_sc, p_sc, ds_sc):
        for hh in range(LANE // HEAD_DIM):
            ls = slice(HEAD_DIM * hh, HEAD_DIM * (hh + 1))
            _store_keys(ka_sc, k_ref, ls)
            vb = v_ref[:, ls]
            dk_head[...] = jnp.zeros_like(dk_head)
            dv_head[...] = jnp.zeros_like(dv_head)
            for i in range(nb):
                rs = slice(MOBA_BLOCK * i, MOBA_BLOCK * (i + 1))
                width = MOBA_BLOCK * (i + 1)
                qa_sc[:, 0:HEAD_DIM] = (q_ref[rs, ls] * scale).astype(BF16)
                qa_sc[:, HEAD_DIM:] = bias_ref[rs, ls]
                s_sc[0:width, :] = _dot_nt(ka_sc[0:width, :], qa_sc[...])
                s_sc[rs, :] = jnp.where(_causal((MOBA_BLOCK, MOBA_BLOCK), transposed=True), s_sc[rs, :], NEG)
                dob = do_ref[rs, ls]
                delta = _row_vector(jnp.sum(dob * o_ref[rs, ls], axis=1, keepdims=True))
                lse_row = lse_ref[0, hh:hh + 1, rs]
                dob16 = dob.astype(BF16)
                dp_sc[0:width, :] = _dot_nt(vb[0:width], dob16)
                for c in range(width // SCORE_CHUNK):
                    rows = slice(SCORE_CHUNK * c, SCORE_CHUNK * (c + 1))
                    p = jnp.exp(s_sc[rows, :] - lse_row)
                    p_sc[rows, :] = p.astype(BF16)
                    ds_sc[rows, :] = (p * (dp_sc[rows, :] - delta)).astype(BF16)
                dv_head[0:width, :] += _dot(p_sc[0:width, :], dob16)
                dk_head[0:width, :] += _dot(ds_sc[0:width, :], qa_sc[:, 0:HEAD_DIM])
                dq_acc[rs, ls] = _dot_tn(ds_sc[0:width, :], ka_sc[0:width, 0:HEAD_DIM]) * scale
            dk_acc[:, ls] = dk_head[...]
            dv_acc[:, ls] = dv_head[...]
        cos_t, sin_t = cos_ref[...], sin_ref[...]
        dq_ref[...] = _rope_transposed(dq_acc[...], cos_t, sin_t).astype(BF16)
        dk_ref[...] = _rope_transposed(dk_acc[...], cos_t, sin_t).astype(BF16)
        dv_ref[...] = dv_acc[...].astype(BF16)

    return _grid_call(
        body, "attn_bwd", D_ATTN // LANE,
        in_specs=[pair, pair, pair, pair, pair, pl.BlockSpec((1, 8, seq), lambda p: (p, 0, 0)), pair, table, table],
        out_specs=[pair, pair, pair], out_shape=[_sds((seq, D_ATTN), BF16)] * 3,
        operands=(q, k, v, bias, o, lse, do, cos, sin),
        scratch=[pltpu.VMEM((seq, LANE), F32)] * 3 + [pltpu.VMEM((seq, HEAD_DIM), F32)] * 2
        + [pltpu.VMEM((seq, LANE), BF16), pltpu.VMEM((MOBA_BLOCK, LANE), BF16)]
        + [pltpu.VMEM((seq, MOBA_BLOCK), F32)] * 2 + [pltpu.VMEM((seq, MOBA_BLOCK), BF16)] * 2,
        tasks=tasks)


def _in_bwd(dq, dk, dv, dpooled, dzg, dr1, win, tasks=()):
    seq = dr1.shape[0]
    nt = seq // TOK

    def body(dq_ref, dk_ref, dv_ref, dp_ref, dpnext_ref, dzg_ref, dr_ref, win_ref, dx_ref, dz_ref, ext):
        i = pl.program_id(0)
        dp = dp_ref[...]
        dpn = jnp.where(i < nt - 1, dpnext_ref[...], 0.0)
        for grp, window in enumerate(POOL_WINDOWS):
            cols = slice(POOL_GROUP * grp, POOL_GROUP * (grp + 1))
            ext[0:TOK, cols] = dp[:, cols] / _pool_count(i * TOK, TOK, window)
            ext[TOK:, cols] = dpn[:, cols] / _pool_count((i + 1) * TOK, POOL_HALO, window)
        for grp, window in enumerate(POOL_WINDOWS):
            cols = slice(POOL_GROUP * grp, POOL_GROUP * (grp + 1))
            acc = ext[0:TOK, cols] - dp[:, cols]
            for kk in range(1, window):
                acc = acc + ext[pl.ds(kk, TOK), cols]
            dz_ref[:, 3 * D_ATTN + POOL_GROUP * grp:3 * D_ATTN + POOL_GROUP * (grp + 1)] = acc.astype(BF16)
        dz_ref[:, 0:D_ATTN] = dq_ref[...]
        dz_ref[:, D_ATTN:2 * D_ATTN] = dk_ref[...]
        dz_ref[:, 2 * D_ATTN:3 * D_ATTN] = dv_ref[...]
        dz_ref[:, 3 * D_ATTN + D_POOL:] = dzg_ref[...]
        dx = ALPHA * dr_ref[...]
        for n in range(N_DEV):
            dx = dx + _dot_nt(dz_ref[:, D_ATTN * n:D_ATTN * (n + 1)], win_ref[n])
        dx_ref[...] = dx

    halo = pl.BlockSpec((POOL_HALO, D_POOL),
                        lambda i: (jnp.minimum((i + 1) * (TOK // POOL_HALO), seq // POOL_HALO - 1), 0))
    return _grid_call(
        body, "in_bwd", nt,
        in_specs=[_rows(D_ATTN), _rows(D_ATTN), _rows(D_ATTN), _rows(D_POOL), halo, _rows(2 * D_MODEL),
                  _rows(D_MODEL), _full(win.shape)],
        out_specs=[_rows(D_MODEL), _rows(D_IN_PROJ)],
        out_shape=[_sds((seq, D_MODEL), F32), _sds((seq, D_IN_PROJ), BF16)],
        operands=(dq, dk, dv, dpooled, dpooled, dzg, dr1, win),
        scratch=[pltpu.VMEM((TOK + POOL_HALO, D_POOL), F32)], tasks=tasks)


def _dw_mixers(o, ypre, merged, dya, dyp, drb, pooled, dmixed, tasks=()):
    seq = o.shape[0]
    groups = len(POOL_WINDOWS)
    col = pl.BlockSpec((seq, LANE), lambda n: (0, n))
    grp = pl.BlockSpec((seq, POOL_GROUP), lambda n: (0, jnp.minimum(n, groups - 1)))
    owner = lambda rows, cols: pl.BlockSpec((1, rows, cols), lambda n: (n, 0, 0))

    def body(o_ref, ypre_ref, merged_ref, dya_ref, dyp_ref, dr_ref, pooled_ref, dmixed_ref,
             dba_ref, dbp_ref, dout_ref, dpool_ref, ob_sc):
        n = pl.program_id(0)

        @pl.when(n == 0)
        def _():
            ob_sc[...] = o_ref[...].astype(BF16)

        dba_ref[0] = _dot_tn(dya_ref[...], ob_sc[...]).T.astype(BF16)
        dbp_ref[0] = _dot_tn(dyp_ref[...], ypre_ref[...]).T.astype(BF16)
        dout_ref[0] = _dot_tn(merged_ref[...], dr_ref[...]).astype(BF16)

        @pl.when(n < groups)
        def _():
            dpool_ref[0] = _dot_tn(pooled_ref[...], dmixed_ref[...])

    return _grid_call(
        body, "dw_mixers", N_DEV,
        in_specs=[_full(o.shape), _full(ypre.shape), col, col, col, _full(drb.shape), grp, grp],
        out_specs=[owner(D_ATTN, LANE), owner(D_POOL, LANE), owner(D_MODEL // N_DEV, D_MODEL),
                   pl.BlockSpec((1, POOL_GROUP, POOL_GROUP), lambda n: (jnp.minimum(n, groups - 1), 0, 0))],
        out_shape=[_sds((N_DEV, D_ATTN, LANE), BF16), _sds((N_DEV, D_POOL, LANE), BF16),
                   _sds((N_DEV, D_MODEL // N_DEV, D_MODEL), BF16), _sds((groups, POOL_GROUP, POOL_GROUP), F32)],
        operands=(o, ypre, merged, dya, dyp, drb, pooled, dmixed),
        scratch=[pltpu.VMEM((seq, D_ATTN), BF16)], tasks=tasks)


def _to_bf16(arrays, tasks=()):
    n = len(arrays)

    def body(*refs):
        for src, dst in zip(refs[:n], refs[n:]):
            dst[...] = src[...].astype(BF16)

    return _grid_call(
        body, "to_bf16", 1, in_specs=[_full(a.shape) for a in arrays],
        out_specs=[pl.BlockSpec(a.shape, lambda i: (0, 0)) for a in arrays],
        out_shape=[_sds(a.shape, BF16) for a in arrays], operands=arrays, tasks=tasks)


def _matmul(name, a, b, out_shape, out_dtype, steps, a_spec, b_spec, o_spec, tasks=()):
    def body(a_ref, b_ref, o_ref):
        o_ref[...] = _dot(a_ref[...], b_ref[...]).reshape(o_ref.shape).astype(o_ref.dtype)

    (out,), results = _grid_call(body, name, steps, in_specs=[a_spec, b_spec], out_specs=[o_spec],
                                 out_shape=[_sds(out_shape, out_dtype)], operands=(a, b), tasks=tasks)
    return out, results


def _place():
    return lax.axis_index("x"), lax.axis_index("y"), lax.axis_index("c")


def _other_chips(x, y):
    return [(1 - x, y), (x, 1 - y), (1 - x, 1 - y)]


DMA_SEMS = pltpu.SemaphoreType.DMA


class _AllGather:
    def __init__(self, shards, lag=0):
        self.operands = list(shards)
        self.n = len(shards)
        self.lag = lag
        self.out_shape = [_sds((N_DEV, *s.shape), s.dtype) for s in shards]
        self.sems = [DMA_SEMS((7 * self.n,)), DMA_SEMS((7 * self.n,)), DMA_SEMS((self.n,))]

    def _copy(self, refs, a, k, block, to, from_input=False):
        ins, outs, (send_sems, recv_sems, _) = refs
        px, py, pc = block
        dst = outs[a].at[4 * px + 2 * py + pc]
        return pltpu.make_async_remote_copy(
            src_ref=ins[a] if from_input else dst, dst_ref=dst,
            send_sem=send_sems.at[7 * a + k], recv_sem=recv_sems.at[7 * a + k],
            device_id=to, device_id_type=MESH)

    def _local(self, refs, a):
        ins, outs, (_, _, local_sems) = refs
        x, y, c = _place()
        return pltpu.make_async_copy(ins[a], outs[a].at[4 * x + 2 * y + c], local_sems.at[a])

    def _pass_on(self, refs, a):
        x, y, c = _place()
        origin = ((x + 1 - c) % 2, (y + c) % 2, c)
        target = ((x + c) % 2, (y + 1 - c) % 2, c)
        return self._copy(refs, a, 3, origin, target)

    def start(self, refs):
        x, y, c = _place()
        for a in range(self.n):
            self._local(refs, a).start()
        for a in range(self.n):
            self._copy(refs, a, 0, (x, y, c), (x, y, 1 - c), True).start()
            for j, chip in enumerate(_other_chips(x, y)[:2]):
                self._copy(refs, a, 1 + j, (x, y, c), (*chip, c), True).start()

    def middle(self, refs):
        x, y, c = _place()
        me, sibling = (x, y, c), (x, y, 1 - c)
        chips = _other_chips(x, y)
        for a in range(self.n):
            for j in range(2):
                self._copy(refs, a, 1 + j, (*chips[j], c), me).wait_recv()
        for a in range(self.n):
            self._pass_on(refs, a).start()
            for j in range(2):
                self._copy(refs, a, 4 + j, (*chips[j], c), sibling).start()

    def late(self, refs):
        x, y, c = _place()
        diagonal = (1 - x, 1 - y, c)
        for a in range(self.n):
            self._copy(refs, a, 3, diagonal, (x, y, c)).wait_recv()
            self._copy(refs, a, 6, diagonal, (x, y, 1 - c)).start()

    def finish(self, refs):
        x, y, c = _place()
        me, sibling = (x, y, c), (x, y, 1 - c)
        chips = _other_chips(x, y)
        for a in range(self.n):
            self._copy(refs, a, 0, sibling, me).wait_recv()
            for j, chip in enumerate(chips):
                self._copy(refs, a, 4 + j, (*chip, 1 - c), me).wait_recv()
        for a in range(self.n):
            self._copy(refs, a, 0, me, sibling, True).wait_send()
            for j, chip in enumerate(chips[:2]):
                self._copy(refs, a, 1 + j, me, (*chip, c), True).wait_send()
            self._pass_on(refs, a).wait_send()
            for j, chip in enumerate(chips):
                self._copy(refs, a, 4 + j, (*chip, c), sibling).wait_send()
            self._local(refs, a).wait()


class _SiblingSend:
    def __init__(self, partials):
        self.operands = list(partials)
        self.n = len(partials)
        self.out_shape = [_sds((4, *p.shape[1:]), p.dtype) for p in partials]
        self.sems = [DMA_SEMS((4 * self.n,)), DMA_SEMS((4 * self.n,))]

    def _copy(self, refs, a, q):
        ins, outs, (send_sems, recv_sems) = refs
        x, y, c = _place()
        return pltpu.make_async_remote_copy(
            src_ref=ins[a].at[2 * q + 1 - c], dst_ref=outs[a].at[q],
            send_sem=send_sems.at[4 * a + q], recv_sem=recv_sems.at[4 * a + q],
            device_id=(x, y, 1 - c), device_id_type=MESH)

    def start(self, refs):
        for a in range(self.n):
            for q in range(4):
                self._copy(refs, a, q).start()

    def middle(self, refs):
        pass

    def finish(self, refs):
        for a in range(self.n):
            for q in range(4):
                self._copy(refs, a, q).wait()


class _ChipScatter:
    def __init__(self, chip_partials):
        self.operands = list(chip_partials)
        self.n = len(chip_partials)
        self.out_shape = [_sds(p.shape, p.dtype) for p in chip_partials]
        self.sems = [DMA_SEMS((3 * self.n,)), DMA_SEMS((3 * self.n,)), DMA_SEMS((self.n,))]

    def _copy(self, refs, a, k, arrival=False):
        ins, outs, (send_sems, recv_sems, _) = refs
        x, y, c = _place()
        px, py = _other_chips(x, y)[k]
        mine, theirs = 2 * x + y, 2 * px + py
        return pltpu.make_async_remote_copy(
            src_ref=ins[a].at[mine if arrival else theirs], dst_ref=outs[a].at[theirs if arrival else mine],
            send_sem=send_sems.at[3 * a + k], recv_sem=recv_sems.at[3 * a + k],
            device_id=(px, py, c), device_id_type=MESH)

    def _local(self, refs, a):
        ins, outs, (_, _, local_sems) = refs
        x, y, _ = _place()
        return pltpu.make_async_copy(ins[a].at[2 * x + y], outs[a].at[2 * x + y], local_sems.at[a])

    def start(self, refs):
        for a in range(self.n):
            self._local(refs, a).start()
            for k in range(3):
                self._copy(refs, a, k).start()

    def middle(self, refs):
        pass

    def finish(self, refs):
        for a in range(self.n):
            for k in range(3):
                self._copy(refs, a, k, arrival=True).wait_recv()
        for a in range(self.n):
            for k in range(3):
                self._copy(refs, a, k).wait_send()
            self._local(refs, a).wait()


class _DirectScatter:
    def __init__(self, partials):
        self.operands = list(partials)
        self.n = len(partials)
        self.out_shape = [_sds(p.shape, p.dtype) for p in partials]
        self.sems = [DMA_SEMS((7 * self.n,)), DMA_SEMS((7 * self.n,)), DMA_SEMS((self.n,))]

    def _copy(self, refs, a, k, arrival=False):
        ins, outs, (send_sems, recv_sems, _) = refs
        x, y, c = _place()
        peer = [(x, y, 1 - c), (1 - x, y, c), (x, 1 - y, c), (1 - x, 1 - y, c),
                (1 - x, y, 1 - c), (x, 1 - y, 1 - c), (1 - x, 1 - y, 1 - c)][k]
        mine, theirs = 4 * x + 2 * y + c, 4 * peer[0] + 2 * peer[1] + peer[2]
        return pltpu.make_async_remote_copy(
            src_ref=ins[a].at[mine if arrival else theirs], dst_ref=outs[a].at[theirs if arrival else mine],
            send_sem=send_sems.at[7 * a + k], recv_sem=recv_sems.at[7 * a + k],
            device_id=peer, device_id_type=MESH)

    def _local(self, refs, a):
        ins, outs, (_, _, local_sems) = refs
        x, y, c = _place()
        return pltpu.make_async_copy(ins[a].at[4 * x + 2 * y + c], outs[a].at[4 * x + 2 * y + c], local_sems.at[a])

    def start(self, refs):
        for a in range(self.n):
            self._local(refs, a).start()
            for k in range(7):
                self._copy(refs, a, k).start()

    def middle(self, refs):
        pass

    def finish(self, refs):
        for a in range(self.n):
            for k in range(7):
                self._copy(refs, a, k, arrival=True).wait_recv()
        for a in range(self.n):
            for k in range(7):
                self._copy(refs, a, k).wait_send()
            self._local(refs, a).wait()


def _task_args(tasks):
    hbm = pl.BlockSpec(memory_space=pl.ANY)
    operands = [o for t in tasks for o in t.operands]
    out_shape = [s for t in tasks for s in t.out_shape]
    sems = [s for t in tasks for s in t.sems]
    return operands, [hbm] * len(operands), out_shape, [hbm] * len(out_shape), sems


def _task_refs(tasks, ins, outs, sems):
    per_task = []
    for t in tasks:
        ni, no, ns = len(t.operands), len(t.out_shape), len(t.sems)
        per_task.append((ins[:ni], outs[:no], sems[:ns]))
        ins, outs, sems = ins[ni:], outs[no:], sems[ns:]
    return per_task


def _task_results(tasks, outs):
    res = []
    for t in tasks:
        res.append(list(outs[:len(t.out_shape)]))
        outs = outs[len(t.out_shape):]
    return res


def _carry(body, tasks, n_in, n_out, n_scratch, steps):
    if not tasks:
        return body
    t_in = sum(len(t.operands) for t in tasks)
    t_out = sum(len(t.out_shape) for t in tasks)

    def wrapped(*refs):
        ins, refs = refs[:n_in], refs[n_in:]
        t_ins, refs = refs[:t_in], refs[t_in:]
        outs, refs = refs[:n_out], refs[n_out:]
        t_outs, refs = refs[:t_out], refs[t_out:]
        scratch, t_sems = refs[:n_scratch], refs[n_scratch:]
        per_task = _task_refs(tasks, t_ins, t_outs, t_sems)
        step = pl.program_id(0)

        @pl.when(step == 0)
        def _():
            for t, r in zip(tasks, per_task):
                t.start(r)

        for t, r in zip(tasks, per_task):
            pl.when(step == max(steps - 1 - getattr(t, "lag", 0), 0))(functools.partial(t.middle, r))
            if hasattr(t, "late"):
                pl.when(step == steps - 1)(functools.partial(t.late, r))

        body(*ins, *outs, *scratch)

        @pl.when(step == steps - 1)
        def _():
            for t, r in zip(tasks, per_task):
                t.finish(r)

    return wrapped


def _exchange(name, tasks):
    operands, in_specs, out_shape, out_specs, sems = _task_args(tasks)

    def body(*refs):
        ni, no = len(operands), len(out_shape)
        per_task = _task_refs(tasks, refs[:ni], refs[ni:ni + no], refs[ni + no:])
        for phase in ("start", "middle", "late", "finish"):
            for t, r in zip(tasks, per_task):
                if hasattr(t, phase):
                    getattr(t, phase)(r)

    outs = pl.pallas_call(body, name=name, in_specs=in_specs, out_specs=out_specs, out_shape=out_shape,
                          scratch_shapes=sems)(*operands)
    return _task_results(tasks, outs)


def _row_tile(rows, cols, whole_up_to=256 * 1024):
    if rows * cols <= whole_up_to:
        return rows
    for t in (256, 176, 128, 64, 32, 16, 8):
        if rows % t == 0:
            return t
    return rows


def _pair_sum(name, partials, from_sibling):
    n = len(partials)

    def body(core_ref, *refs):
        for p_ref, s_ref, o_ref in zip(refs[:n], refs[n:2 * n], refs[2 * n:]):
            o_ref[0] = (p_ref[0, 0].astype(F32) + s_ref[0].astype(F32)).astype(o_ref.dtype)

    blk = [pl.BlockSpec((1, *s.shape[1:]), lambda q, core: (q, 0, 0)) for s in from_sibling]
    mine = [pl.BlockSpec((1, 1, *p.shape[1:]), lambda q, core: (q, core[0], 0, 0)) for p in partials]
    return pl.pallas_call(
        body, name=name,
        grid_spec=pltpu.PrefetchScalarGridSpec(num_scalar_prefetch=1, grid=(4,), in_specs=mine + blk, out_specs=blk),
        out_shape=[_sds(s.shape, s.dtype) for s in from_sibling],
        compiler_params=_params("parallel"),
    )(lax.axis_index("c").astype(jnp.int32).reshape(1), *[p.reshape(4, 2, *p.shape[1:]) for p in partials],
      *from_sibling)


def _adamw_math(w, g, m, v):
    nm = ADAM_B1 * m + (1.0 - ADAM_B1) * g
    nv = ADAM_B2 * v + (1.0 - ADAM_B2) * (g * g)
    m_hat = nm / (1.0 - ADAM_B1 ** ADAM_STEP)
    v_hat = nv / (1.0 - ADAM_B2 ** ADAM_STEP)
    return -ADAM_LR * (m_hat / (jnp.sqrt(v_hat) + ADAM_EPS) + ADAM_WD * w), nm, nv


def _update_small(params, loss_parts):
    n = len(params)

    def whole(shape):
        return pl.BlockSpec(shape, lambda i, rank=len(shape): (0,) * rank)

    def total(ref):
        acc = ref[0]
        for d in range(1, N_DEV):
            acc = acc + ref[d]
        return acc

    def body(*refs):
        ins, loss_ref, outs, loss_out = refs[:4 * n], refs[4 * n], refs[4 * n + 1:8 * n + 1], refs[8 * n + 1]
        for p in range(n):
            s_ref, w_ref, m_ref, v_ref = ins[4 * p:4 * p + 4]
            g_ref, d_ref, nm_ref, nv_ref = outs[4 * p:4 * p + 4]
            g = total(s_ref)
            g_ref[...] = g
            d_ref[...], nm_ref[...], nv_ref[...] = _adamw_math(w_ref[...], g, m_ref[...], v_ref[...])
        loss_out[...] = total(loss_ref)

    outs = pl.pallas_call(
        body, name="update_small", grid=(1,),
        in_specs=[whole(t.shape) for p in params for t in p] + [whole(loss_parts.shape)],
        out_specs=[whole(p[1].shape) for p in params for _ in range(4)] + [whole(loss_parts.shape[1:])],
        out_shape=[_sds(p[1].shape, F32) for p in params for _ in range(4)] + [_sds(loss_parts.shape[1:], F32)],
        compiler_params=_params("arbitrary"),
    )(*[t for p in params for t in p], loss_parts)
    return [outs[4 * p:4 * p + 4] for p in range(n)], outs[4 * n]


def _sum_adamw(name, params, tasks=()):
    n = len(params)
    parts = params[0][0].shape[0]
    shapes = [p[1].shape for p in params]
    tiles = [_row_tile(*shapes[0])] * n if len(set(shapes)) == 1 else [rows for rows, _ in shapes]
    steps = shapes[0][0] // tiles[0]

    def body(*refs):
        ins, outs = refs[:4 * n], refs[4 * n:]
        for p in range(n):
            s_ref, w_ref, m_ref, v_ref = ins[4 * p:4 * p + 4]
            g_ref, d_ref, nm_ref, nv_ref = outs[4 * p:4 * p + 4]
            g = s_ref[0].astype(F32)
            for d in range(1, parts):
                g = g + s_ref[d].astype(F32)
            g_ref[...] = g
            d_ref[...], nm_ref[...], nv_ref[...] = _adamw_math(w_ref[...], g, m_ref[...], v_ref[...])

    blk = [pl.BlockSpec((t, cols), lambda i: (i, 0)) for t, (_, cols) in zip(tiles, shapes)]
    stacked = [pl.BlockSpec((parts, t, cols), lambda i: (0, i, 0)) for t, (_, cols) in zip(tiles, shapes)]
    outs, results = _grid_call(
        body, name, steps,
        in_specs=[spec for p in range(n) for spec in (stacked[p], blk[p], blk[p], blk[p])],
        out_specs=[blk[p] for p in range(n) for _ in range(4)],
        out_shape=[_sds(shapes[p], F32) for p in range(n) for _ in range(4)],
        operands=[t for p in params for t in p], tasks=tasks)
    return [outs[4 * p:4 * p + 4] for p in range(n)], results


SMALL = ("b_gate", "w_pool", "pool_scale", "ln1_g", "ln1_b", "conv_b", "ln2_g", "ln2_b")
MIXER = ("w_branch_attn", "w_branch_pool", "w_out", "conv_w")
FFN = ("w_ffn_gate_t", "w_ffn_up_t", "w_ffn_down")


def _columns(t):
    return jnp.transpose(t, (1, 0, 2)).reshape(t.shape[1], N_DEV * t.shape[2])


def _row_blocks(t):
    return t.reshape(N_DEV * t.shape[1], t.shape[2])


def _by_owner(t):
    return t.reshape(N_DEV, t.shape[0] // N_DEV, t.shape[1])


def _reduce_halves(names, partials, from_sibling):
    return _pair_sum("pair_sum_" + names[0], partials, from_sibling)


def _local_step(x, target, shards, small):
    seq = x.shape[0]
    cos, sin = _rope_tables(seq)
    cast, ((w_in_all,),) = _to_bf16([shards[n] for n in MIXER[:3] + FFN], tasks=[_AllGather([shards["w_in"]])])
    shards = {**shards, **dict(zip(MIXER[:3] + FFN, cast))}
    (xt, q, k, v, u, g, kmean), (mixer,) = _proj_in(
        x, w_in_all, small["b_gate"], cos, sin, tasks=[_AllGather([shards[n] for n in MIXER], lag=2)])
    wba, wbp, wout, conv_w = _columns(mixer[0]), _columns(mixer[1]), _row_blocks(mixer[2]), _columns(mixer[3])
    (o, lse, bias), ((wgt, wut),) = _attn_fwd(
        q, k, v, kmean.reshape(seq // MOBA_BLOCK, D_ATTN),
        tasks=[_AllGather([shards["w_ffn_gate_t"], shards["w_ffn_up_t"]], lag=1)])
    (ya, yp, pooled, mixed, ypre, merged, xhat1, rstd1, h1, h1b), _ = _mix(
        o, u, g, x, wba, wbp, wout, small["w_pool"], small["pool_scale"], small["ln1_g"], small["ln1_b"])
    wgt, wut = _row_blocks(wgt), _row_blocks(wut)
    (a, uf, act), ((wd,),) = _ffn_up(
        h1b, wgt, wut, conv_w, small["conv_b"], tasks=[_AllGather([shards["w_ffn_down"]], lag=4)])
    wd = _row_blocks(wd)
    dr2, dr2b, loss, dg2, db2 = _ffn_down(act, wd, h1, target, small["ln2_g"], small["ln2_b"])

    da, du, dwd, dwg, dwu, dconv = _ffn_bwd(dr2b, h1b, a, uf, wd, conv_w, small["conv_b"])
    ffn_partials = [_by_owner(dwg), _by_owner(dwu), _by_owner(dwd)]
    (dr1, dr1b, dg1, db1), (ffn_sibling,) = _ln1_bwd(
        dr2, da, du, wgt, wut, xhat1, rstd1, small["ln1_g"], tasks=[_SiblingSend(ffn_partials)])
    ffn_chip = _reduce_halves(FFN, ffn_partials, ffn_sibling)
    (dzg, dya, dyp, do, dmixed, dpooled, dbg, dps), (gate_landed,) = _mix_bwd(
        dr1b, ya, yp, g, mixed, wout, wba, wbp, small["w_pool"], small["pool_scale"],
        tasks=[_ChipScatter(ffn_chip[0:1])])
    (dw_ba, dw_bp, dw_out, dw_pool), _ = _dw_mixers(o, ypre, merged, dya, dyp, dr1b, pooled, dmixed)
    mixer_partials = [dw_ba, dw_bp, dw_out]
    (dq, dk, dv), (up_down_landed, mixer_sibling) = _attn_bwd(
        q, k, v, bias, o, lse, do, cos, sin, tasks=[_ChipScatter(ffn_chip[1:3]), _SiblingSend(mixer_partials)])
    mixer_chip = _reduce_halves(MIXER[:3], mixer_partials, mixer_sibling)
    (grad_x, dz), _ = _in_bwd(dq, dk, dv, dpooled, dzg, dr1, w_in_all)
    little = [dbg, dw_pool, dps, dg1, db1, dconv[3:4], dg2, db2]
    conv_w_partials = dconv[0:3].reshape(3, N_DEV, FF_SHARD).transpose(1, 0, 2)
    dw_in, (mixer_landed,) = _matmul(
        "dw_in", xt, dz, (N_DEV, D_MODEL, D_ATTN), BF16, N_DEV,
        _full(xt.shape), pl.BlockSpec((seq, D_ATTN), lambda n: (0, n)),
        pl.BlockSpec((1, D_MODEL, D_ATTN), lambda n: (n, 0, 0)), tasks=[_ChipScatter(mixer_chip)])

    landed = dict(zip(FFN + MIXER[:3], gate_landed + up_down_landed + mixer_landed))
    return grad_x, landed, dw_in, little, conv_w_partials, loss


def kernel(x, w_in, b_gate, w_branch_attn, w_pool, pool_scale, w_branch_pool, w_out, ln1_g, ln1_b, w_ffn_gate, w_ffn_up, conv_w, conv_b, w_ffn_down, ln2_g, ln2_b, loss_target, m_w_in, m_b_gate, m_w_branch_attn, m_w_pool, m_pool_scale, m_w_branch_pool, m_w_out, m_ln1_g, m_ln1_b, m_w_ffn_gate, m_w_ffn_up, m_conv_w, m_conv_b, m_w_ffn_down, m_ln2_g, m_ln2_b, v_w_in, v_b_gate, v_w_branch_attn, v_w_pool, v_pool_scale, v_w_branch_pool, v_w_out, v_ln1_g, v_ln1_b, v_w_ffn_gate, v_w_ffn_up, v_conv_w, v_conv_b, v_w_ffn_down, v_ln2_g, v_ln2_b):
    weights = dict(w_in=w_in, b_gate=b_gate, w_branch_attn=w_branch_attn, w_pool=w_pool, pool_scale=pool_scale,
                   w_branch_pool=w_branch_pool, w_out=w_out, ln1_g=ln1_g, ln1_b=ln1_b, w_ffn_gate=w_ffn_gate,
                   w_ffn_up=w_ffn_up, conv_w=conv_w, conv_b=conv_b, w_ffn_down=w_ffn_down, ln2_g=ln2_g, ln2_b=ln2_b)
    m_in = dict(w_in=m_w_in, b_gate=m_b_gate, w_branch_attn=m_w_branch_attn, w_pool=m_w_pool,
                pool_scale=m_pool_scale, w_branch_pool=m_w_branch_pool, w_out=m_w_out, ln1_g=m_ln1_g, ln1_b=m_ln1_b,
                w_ffn_gate=m_w_ffn_gate, w_ffn_up=m_w_ffn_up, conv_w=m_conv_w, conv_b=m_conv_b,
                w_ffn_down=m_w_ffn_down, ln2_g=m_ln2_g, ln2_b=m_ln2_b)
    v_in = dict(w_in=v_w_in, b_gate=v_b_gate, w_branch_attn=v_w_branch_attn, w_pool=v_w_pool,
                pool_scale=v_pool_scale, w_branch_pool=v_w_branch_pool, w_out=v_w_out, ln1_g=v_ln1_g, ln1_b=v_ln1_b,
                w_ffn_gate=v_w_ffn_gate, w_ffn_up=v_w_ffn_up, conv_w=v_conv_w, conv_b=v_conv_b,
                w_ffn_down=v_w_ffn_down, ln2_g=v_ln2_g, ln2_b=v_ln2_b)
    weights = {n: a[0] for n, a in weights.items()}
    m_in = {n: a[0] for n, a in m_in.items()}
    v_in = {n: a[0] for n, a in v_in.items()}

    shards = {"w_in": weights["w_in"].astype(BF16), "w_branch_attn": weights["w_branch_attn"],
              "w_branch_pool": weights["w_branch_pool"], "w_out": weights["w_out"],
              "w_ffn_gate_t": weights["w_ffn_gate"].T, "w_ffn_up_t": weights["w_ffn_up"].T,
              "w_ffn_down": weights["w_ffn_down"], "conv_w": weights["conv_w"]}
    small = {"b_gate": weights["b_gate"][None], "w_pool": weights["w_pool"], "pool_scale": weights["pool_scale"][None],
             "ln1_g": weights["ln1_g"][None], "ln1_b": weights["ln1_b"][None], "conv_b": weights["conv_b"][None],
             "ln2_g": weights["ln2_g"][None], "ln2_b": weights["ln2_b"][None]}

    grad_x, landed, dw_in, little, conv_w_partials, loss_part = _local_step(x[0], loss_target[0], shards, small)

    grads, delta, new_m, new_v = {}, {}, {}, {}

    def param(n, transposed=False):
        if transposed:
            return landed[n + "_t"], weights[n].T, m_in[n].T, v_in[n].T
        return landed[n], weights[n], m_in[n], v_in[n]

    def keep(n, updated, transposed=False):
        grads[n], delta[n], new_m[n], new_v[n] = (t.T for t in updated) if transposed else updated

    ((w_in_sibling,),) = _exchange("sibling_grads", [_SiblingSend([dw_in])])
    w_in_chip = _reduce_halves(["w_in"], [dw_in], [w_in_sibling])
    (landed["w_in"],), (*small_all, loss_all), (conv_w_all,) = _exchange(
        "scatter_grads",
        [_ChipScatter(w_in_chip), _AllGather(little + [loss_part]), _DirectScatter([conv_w_partials])])
    (gate, up, down), _ = _sum_adamw(
        "update_w_ffn", [param("w_ffn_gate", True), param("w_ffn_up", True), param("w_ffn_down")])
    keep("w_ffn_gate", gate, True)
    keep("w_ffn_up", up, True)
    keep("w_ffn_down", down)
    mixers = ("w_branch_attn", "w_branch_pool", "w_out")
    for n, updated in zip(mixers, _sum_adamw("update_mixers", [param(n) for n in mixers])[0]):
        keep(n, updated)
    keep("w_in", _sum_adamw("update_w_in", [param("w_in")])[0][0])
    names = SMALL + ("conv_w",)
    rows = lambda t: t if t.ndim > 1 else t[None]
    small_updates, loss = _update_small(
        [(s, rows(weights[n]), rows(m_in[n]), rows(v_in[n])) for n, s in zip(names, small_all + [conv_w_all])],
        loss_all)
    for n, updated in zip(names, small_updates):
        keep(n, [t.reshape(weights[n].shape) for t in updated])
    loss = loss[0, 0]

    order = ("w_in", "b_gate", "w_branch_attn", "w_pool", "pool_scale", "w_branch_pool", "w_out", "ln1_g", "ln1_b",
             "w_ffn_gate", "w_ffn_up", "conv_w", "conv_b", "w_ffn_down", "ln2_g", "ln2_b")
    lead = lambda t: t[None]
    return (loss, lead(grad_x), *[lead(grads[n]) for n in order], *[lead(delta[n]) for n in order],
            *[lead(new_m[n]) for n in order], *[lead(new_v[n]) for n in order])
```

```python
import functools
import math

import jax
import jax.numpy as jnp
from jax import lax
from jax.experimental import pallas as pl
from jax.experimental.pallas import tpu as pltpu

F32 = jnp.float32
BF16 = jnp.bfloat16

D_MODEL = 1024
N_HEADS = 8
HEAD_DIM = 64
D_ATTN = N_HEADS * HEAD_DIM
MOBA_BLOCK = 256
MOBA_TOPK = 3
ROPE_THETA = 10000.0
POOL_WINDOWS = (2, 4, 8, 16)
POOL_GROUP = 128
D_POOL = len(POOL_WINDOWS) * POOL_GROUP
POOL_HALO = 16
D_FF = 2816
D_IN_PROJ = 3 * D_ATTN + D_POOL + 2 * D_MODEL
LN_EPS = 1e-5
ALPHA = 2.0 ** 0.25
NEG = -1e30
N_DEV = 8
FF_SHARD = D_FF // N_DEV

ADAM_LR = 0.001
ADAM_B1 = 0.9
ADAM_B2 = 0.999
ADAM_EPS = 1e-08
ADAM_WD = 0.01
ADAM_STEP = 10

TOK = 256
FF_CHUNK = 256
LANE = 128
VMEM_LIMIT = 56 * 1024 * 1024

MESH = pl.DeviceIdType.MESH
NT_DIMS = (((1,), (1,)), ((), ()))
TN_DIMS = (((0,), (0,)), ((), ()))


def _params(*sem):
    return pltpu.CompilerParams(dimension_semantics=sem or None, vmem_limit_bytes=VMEM_LIMIT)


def _full(shape):
    zeros = (0,) * len(shape)
    return pl.BlockSpec(shape, lambda *_: zeros, pipeline_mode=pl.Buffered(1))


def _rows(width, tile=TOK):
    return pl.BlockSpec((tile, width), lambda i: (i, 0))


def _sds(shape, dtype):
    return jax.ShapeDtypeStruct(shape, dtype)


def _dot(a, b):
    return jnp.dot(a, b, preferred_element_type=F32)


def _dot_nt(a, b):
    return lax.dot_general(a, b, NT_DIMS, preferred_element_type=F32)


def _dot_tn(a, b):
    return lax.dot_general(a, b, TN_DIMS, preferred_element_type=F32)


def _rope_tables(seq):
    half = HEAD_DIM // 2
    inv_freq = 1.0 / (ROPE_THETA ** (jnp.arange(half, dtype=F32) / half))
    ang = jnp.arange(seq, dtype=F32)[:, None] * inv_freq[None, :]
    cos, sin = jnp.cos(ang), jnp.sin(ang)
    return jnp.tile(cos, (1, 4)), jnp.tile(jnp.concatenate([-sin, sin], axis=1), (1, 2))


def _swap_halves(t):
    lane = lax.broadcasted_iota(jnp.int32, t.shape, 1)
    return jnp.where((lane % HEAD_DIM) < HEAD_DIM // 2, pltpu.roll(t, LANE - 32, 1), pltpu.roll(t, 32, 1))


def _rope(t, cos, sin):
    return t * cos + _swap_halves(t) * sin


def _rope_transposed(g, cos, sin):
    return g * cos + _swap_halves(g * sin)


def _ln_fwd(r, g, b):
    mu = jnp.mean(r, axis=-1, keepdims=True)
    xc = r - mu
    var = jnp.mean(xc * xc, axis=-1, keepdims=True)
    rstd = lax.rsqrt(var + LN_EPS)
    xhat = xc * rstd
    return xhat * g + b, xhat, rstd


def _ln_bwd(dy, xhat, rstd, g):
    dxh = dy * g
    m1 = jnp.mean(dxh, axis=-1, keepdims=True)
    m2 = jnp.mean(dxh * xhat, axis=-1, keepdims=True)
    return rstd * (dxh - m1 - xhat * m2)


def _normal_cdf(a):
    return 0.5 * (1.0 + lax.erf(a * (1.0 / math.sqrt(2.0))))


def _gelu_derivative(a, cdf):
    return cdf + a * (jnp.exp(-0.5 * a * a) * (1.0 / math.sqrt(2.0 * math.pi)))


def _shift_down(a, k):
    row = lax.broadcasted_iota(jnp.int32, a.shape, 0)
    return jnp.where(row >= k, pltpu.roll(a, k, 0), 0.0)


def _shift_up(a, k):
    n = a.shape[0]
    row = lax.broadcasted_iota(jnp.int32, a.shape, 0)
    return jnp.where(row < n - k, pltpu.roll(a, n - k, 0), 0.0)


def _conv(a, cw, cb):
    return cw[2:3, :] * a + cw[1:2, :] * _shift_down(a, 1) + cw[0:1, :] * _shift_down(a, 2) + cb


def _pool_count(first_row, rows, window):
    t = first_row + lax.broadcasted_iota(jnp.int32, (rows, 1), 0)
    return jnp.minimum(t + 1, window).astype(F32)


def _grid_call(body, name, steps, in_specs, out_specs, out_shape, operands, scratch=(), tasks=()):
    t_operands, t_in_specs, t_out_shape, t_out_specs, t_sems = _task_args(tasks)
    outs = pl.pallas_call(
        _carry(body, tasks, len(in_specs), len(out_specs), len(scratch), steps), name=name, grid=(steps,),
        in_specs=list(in_specs) + t_in_specs, out_specs=list(out_specs) + t_out_specs,
        out_shape=list(out_shape) + t_out_shape, scratch_shapes=list(scratch) + t_sems,
        compiler_params=_params("arbitrary"),
    )(*operands, *t_operands)
    return outs[:len(out_specs)], _task_results(tasks, outs[len(out_specs):])


def _proj_in(x, win, b_gate, cos, sin, tasks=()):
    seq = x.shape[0]
    nt = seq // TOK

    def body(x_ref, win_ref, bg_ref, cos_ref, sin_ref, xt_ref, q_ref, k_ref, v_ref, u_ref, g_ref, km_ref):
        xb = x_ref[...].astype(BF16)
        xt_ref[...] = x_ref[...].T.astype(BF16)
        cos_t, sin_t = cos_ref[...], sin_ref[...]
        for sec, out_ref in ((0, q_ref), (1, k_ref)):
            z = _dot(xb, win_ref[sec])
            for c in range(D_ATTN // LANE):
                cols = slice(LANE * c, LANE * (c + 1))
                out_ref[:, cols] = _rope(z[:, cols], cos_t, sin_t)
        for b in range(TOK // MOBA_BLOCK):
            km_ref[b] = jnp.mean(k_ref[MOBA_BLOCK * b:MOBA_BLOCK * (b + 1), :], axis=0, keepdims=True)
        v_ref[...] = _dot(xb, win_ref[2]).astype(BF16)
        u_ref[...] = _dot(xb, win_ref[3])
        for n in range(4):
            cols = slice(D_ATTN * n, D_ATTN * (n + 1))
            g_ref[:, cols] = jax.nn.sigmoid(_dot(xb, win_ref[4 + n]) + bg_ref[:, cols])

    return _grid_call(
        body, "proj_in", nt,
        in_specs=[_rows(D_MODEL), _full(win.shape), _full((1, 2 * D_MODEL)), _rows(LANE), _rows(LANE)],
        out_specs=[pl.BlockSpec((D_MODEL, TOK), lambda i: (0, i)), _rows(D_ATTN), _rows(D_ATTN), _rows(D_ATTN),
                   _rows(D_POOL), _rows(2 * D_MODEL),
                   pl.BlockSpec((TOK // MOBA_BLOCK, 1, D_ATTN), lambda i: (i, 0, 0))],
        out_shape=[_sds((D_MODEL, seq), BF16), _sds((seq, D_ATTN), F32), _sds((seq, D_ATTN), F32),
                   _sds((seq, D_ATTN), BF16), _sds((seq, D_POOL), F32), _sds((seq, 2 * D_MODEL), F32),
                   _sds((seq // MOBA_BLOCK, 1, D_ATTN), F32)],
        operands=(x, win, b_gate, cos, sin), tasks=tasks)


SCORE_CHUNK = 128


def _store_keys(ka_sc, k_ref, ls):
    seq = ka_sc.shape[0]
    ka_sc[:, 0:HEAD_DIM] = k_ref[:, ls].astype(BF16)
    row = lax.broadcasted_iota(jnp.int32, (seq, HEAD_DIM), 0)
    lane = lax.broadcasted_iota(jnp.int32, (seq, HEAD_DIM), 1)
    in_block = (lane * MOBA_BLOCK <= row) & (row < (lane + 1) * MOBA_BLOCK)
    ka_sc[:, HEAD_DIM:] = jnp.where(in_block, 1.0, 0.0).astype(BF16)


def _block_bias(qf, km, i):
    if i <= MOBA_TOPK:
        return jnp.zeros((MOBA_BLOCK, HEAD_DIM), BF16)
    nb = km.shape[0]
    gate = lax.dot_general(km, qf, NT_DIMS, precision=lax.Precision.HIGHEST, preferred_element_type=F32)
    blk = lax.broadcasted_iota(jnp.int32, gate.shape, 0)
    rank = jnp.zeros(gate.shape, F32)
    for r in range(1, i):
        lower = pltpu.roll(gate, r, 0)
        rank = rank + jnp.where((blk >= r) & (lower >= gate), 1.0, 0.0)
        higher = pltpu.roll(gate, nb - r, 0)
        rank = rank + jnp.where((blk + r < i) & (higher > gate), 1.0, 0.0)
    bias = jnp.where((blk < i) & (rank >= MOBA_TOPK), NEG, 0.0)
    padded = jnp.concatenate([bias, jnp.zeros((LANE - nb, MOBA_BLOCK), F32)], axis=0)
    return jnp.transpose(padded)[:, 0:HEAD_DIM].astype(BF16)


def _causal(shape, transposed=False):
    row = lax.broadcasted_iota(jnp.int32, shape, 0)
    col = lax.broadcasted_iota(jnp.int32, shape, 1)
    return (row <= col) if transposed else (col <= row)


def _row_vector(col):
    return jnp.transpose(jnp.broadcast_to(col, (MOBA_BLOCK, LANE)))[0:1, :]


def _attn_fwd(q, k, v, kmean, tasks=()):
    seq = q.shape[0]
    nb = seq // MOBA_BLOCK
    assert nb == 8, "the block ranking keeps one sublane per key block"
    pair = pl.BlockSpec((seq, LANE), lambda p: (0, p))
    heads = LANE // HEAD_DIM

    def body(q_ref, k_ref, v_ref, km_ref, o_ref, lse_ref, bias_ref, ka_sc, qa_sc, s_sc, p_sc, hook=None):
        lse_ref[0, heads:, :] = jnp.zeros((8 - heads, seq), F32)
        for hh in range(heads):
            if hh == 1 and hook is not None:
                hook()
            ls = slice(HEAD_DIM * hh, HEAD_DIM * (hh + 1))
            _store_keys(ka_sc, k_ref, ls)
            vb = v_ref[:, ls]
            km = km_ref[:, ls]
            for i in range(nb):
                rs = slice(MOBA_BLOCK * i, MOBA_BLOCK * (i + 1))
                width = MOBA_BLOCK * (i + 1)
                qf = q_ref[rs, ls]
                bias = _block_bias(qf, km, i)
                bias_ref[rs, ls] = bias
                qa_sc[:, 0:HEAD_DIM] = (qf * HEAD_DIM ** -0.5).astype(BF16)
                qa_sc[:, HEAD_DIM:] = bias
                s_sc[:, 0:width] = _dot_nt(qa_sc[...], ka_sc[0:width, :])
                s_sc[:, rs] = jnp.where(_causal((MOBA_BLOCK, MOBA_BLOCK)), s_sc[:, rs], NEG)
                chunks = [slice(SCORE_CHUNK * c, SCORE_CHUNK * (c + 1)) for c in range(width // SCORE_CHUNK)]
                top = s_sc[:, chunks[0]]
                for c in chunks[1:]:
                    top = jnp.maximum(top, s_sc[:, c])
                m = jnp.max(top, axis=1, keepdims=True)
                total = jnp.zeros((MOBA_BLOCK, SCORE_CHUNK), F32)
                for c in chunks:
                    p = jnp.exp(s_sc[:, c] - m)
                    total = total + p
                    p_sc[:, c] = p.astype(BF16)
                l = jnp.sum(total, axis=1, keepdims=True)
                o_ref[rs, ls] = _dot(p_sc[:, 0:width], vb[0:width]) / l
                lse_ref[0, hh:hh + 1, rs] = _row_vector(m + jnp.log(l))

    return _grid_call(
        body, "attn_fwd", D_ATTN // LANE,
        in_specs=[pair, pair, pair, pl.BlockSpec((nb, LANE), lambda p: (0, p))],
        out_specs=[pair, pl.BlockSpec((1, 8, seq), lambda p: (p, 0, 0)), pair],
        out_shape=[_sds((seq, D_ATTN), F32), _sds((D_ATTN // LANE, 8, seq), F32), _sds((seq, D_ATTN), BF16)],
        operands=(q, k, v, kmean),
        scratch=[pltpu.VMEM((seq, LANE), BF16), pltpu.VMEM((MOBA_BLOCK, LANE), BF16),
                 pltpu.VMEM((MOBA_BLOCK, seq), F32), pltpu.VMEM((MOBA_BLOCK, seq), BF16)],
        tasks=tasks)


def _mix(o, u, g, x, wba, wbp, wout, w_pool, pool_scale, ln_g, ln_b, tasks=()):
    seq = x.shape[0]

    def body(o_ref, u_ref, uprev_ref, g_ref, x_ref, wba_ref, wbp_ref, wout_ref, wp_ref, ps_ref, lg_ref, lb_ref,
             ya_ref, yp_ref, pooled_ref, mixed_ref, ypre_ref, merged_ref, xhat_ref, rstd_ref, h_ref, hb_ref, ext):
        i = pl.program_id(0)
        ya = _dot(o_ref[...].astype(BF16), wba_ref[...])
        ucur = u_ref[...]
        ext[0:POOL_HALO, :] = jnp.where(i > 0, uprev_ref[...], 0.0)
        ext[POOL_HALO:, :] = ucur
        for grp, window in enumerate(POOL_WINDOWS):
            cols = slice(POOL_GROUP * grp, POOL_GROUP * (grp + 1))
            acc = ucur[:, cols]
            for kk in range(1, window):
                acc = acc + ext[pl.ds(POOL_HALO - kk, TOK), cols]
            pooled = acc / _pool_count(i * TOK, TOK, window) - ucur[:, cols]
            pooled_ref[:, cols] = pooled.astype(BF16)
            mixed_ref[:, cols] = _dot(pooled.astype(BF16), wp_ref[grp].astype(BF16))
        mixed = mixed_ref[...]
        ypre = (mixed * ps_ref[...]).astype(BF16)
        ypre_ref[...] = ypre
        yp = _dot(ypre, wbp_ref[...])
        ya_ref[...] = ya
        yp_ref[...] = yp
        merged = (g_ref[:, :D_MODEL] * ya + g_ref[:, D_MODEL:] * yp).astype(BF16)
        merged_ref[...] = merged
        r1 = ALPHA * x_ref[...] + _dot(merged, wout_ref[...])
        h, xhat, rstd = _ln_fwd(r1, lg_ref[...], lb_ref[...])
        xhat_ref[...] = xhat
        rstd_ref[...] = jnp.broadcast_to(rstd, (TOK, LANE))
        h_ref[...] = h
        hb_ref[...] = h.astype(BF16)

    halo = pl.BlockSpec((POOL_HALO, D_POOL), lambda i: (jnp.maximum(i * (TOK // POOL_HALO) - 1, 0), 0))
    return _grid_call(
        body, "mix", seq // TOK,
        in_specs=[_rows(D_ATTN), _rows(D_POOL), halo, _rows(2 * D_MODEL), _rows(D_MODEL),
                  _full(wba.shape), _full(wbp.shape), _full(wout.shape), _full(w_pool.shape),
                  _full((1, D_POOL)), _full((1, D_MODEL)), _full((1, D_MODEL))],
        out_specs=[_rows(D_MODEL), _rows(D_MODEL), _rows(D_POOL), _rows(D_POOL), _rows(D_POOL), _rows(D_MODEL),
                   _rows(D_MODEL), _rows(LANE), _rows(D_MODEL), _rows(D_MODEL)],
        out_shape=[_sds((seq, D_MODEL), F32), _sds((seq, D_MODEL), F32), _sds((seq, D_POOL), BF16),
                   _sds((seq, D_POOL), F32), _sds((seq, D_POOL), BF16), _sds((seq, D_MODEL), BF16),
                   _sds((seq, D_MODEL), F32), _sds((seq, LANE), F32), _sds((seq, D_MODEL), F32),
                   _sds((seq, D_MODEL), BF16)],
        operands=(o, u, u, g, x, wba, wbp, wout, w_pool, pool_scale, ln_g, ln_b),
        scratch=[pltpu.VMEM((TOK + POOL_HALO, D_POOL), F32)], tasks=tasks)


def _ffn_up(hb, wgt, wut, conv_w, conv_b, tasks=()):
    seq = hb.shape[0]
    wblk = pl.BlockSpec((FF_CHUNK, D_MODEL), lambda c: (c, 0))
    cblk = lambda rows: pl.BlockSpec((rows, FF_CHUNK), lambda c: (0, c))
    oblk = pl.BlockSpec((seq, FF_CHUNK), lambda c: (0, c))

    def body(h_ref, wg_ref, wu_ref, cw_ref, cb_ref, a_ref, u_ref, act_ref):
        h = h_ref[...]
        a = _dot_nt(h, wg_ref[...])
        u = _dot_nt(h, wu_ref[...])
        a_ref[...] = a
        u_ref[...] = u
        ac = _conv(a, cw_ref[...], cb_ref[...])
        act_ref[...] = (ac * _normal_cdf(ac) * u).astype(BF16)

    return _grid_call(
        body, "ffn_up", D_FF // FF_CHUNK,
        in_specs=[_full(hb.shape), wblk, wblk, cblk(3), cblk(1)],
        out_specs=[oblk, oblk, oblk],
        out_shape=[_sds((seq, D_FF), F32), _sds((seq, D_FF), F32), _sds((seq, D_FF), BF16)],
        operands=(hb, wgt, wut, conv_w, conv_b), tasks=tasks)


def _ffn_down(act, wd, h, target, ln_g, ln_b):
    seq = h.shape[0]

    def body(act_ref, wd_ref, h_ref, t_ref, lg_ref, lb_ref, dr_ref, drb_ref, loss_ref, dg_ref, db_ref):
        i = pl.program_id(0)

        @pl.when(i == 0)
        def _():
            loss_ref[...] = jnp.zeros_like(loss_ref)
            dg_ref[...] = jnp.zeros_like(dg_ref)
            db_ref[...] = jnp.zeros_like(db_ref)

        r2 = ALPHA * h_ref[...] + _dot(act_ref[...], wd_ref[...])
        y, xhat, rstd = _ln_fwd(r2, lg_ref[...], lb_ref[...])
        diff = y - t_ref[...]
        loss_ref[...] += jnp.sum(diff * diff) * (0.5 / D_MODEL)
        dy = diff * (1.0 / D_MODEL)
        dg_ref[...] += jnp.sum(dy * xhat, axis=0, keepdims=True)
        db_ref[...] += jnp.sum(dy, axis=0, keepdims=True)
        dr = _ln_bwd(dy, xhat, rstd, lg_ref[...])
        dr_ref[...] = dr
        drb_ref[...] = dr.astype(BF16)

    vec = pl.BlockSpec((1, D_MODEL), lambda i: (0, 0))
    return pl.pallas_call(
        body, name="ffn_down", grid=(seq // TOK,),
        in_specs=[_rows(D_FF), _full(wd.shape), _rows(D_MODEL), _rows(D_MODEL), _full((1, D_MODEL)), _full((1, D_MODEL))],
        out_specs=[_rows(D_MODEL), _rows(D_MODEL), pl.BlockSpec((8, LANE), lambda i: (0, 0)), vec, vec],
        out_shape=[_sds((seq, D_MODEL), F32), _sds((seq, D_MODEL), BF16), _sds((8, LANE), F32),
                   _sds((1, D_MODEL), F32), _sds((1, D_MODEL), F32)],
        compiler_params=_params("arbitrary"),
    )(act, wd, h, target, ln_g, ln_b)


def _ffn_bwd(drb, hb, a, u, wd, conv_w, conv_b):
    seq = hb.shape[0]
    wblk = pl.BlockSpec((FF_CHUNK, D_MODEL), lambda c: (c, 0))
    cblk = lambda rows: pl.BlockSpec((rows, FF_CHUNK), lambda c: (0, c))
    sblk = pl.BlockSpec((seq, FF_CHUNK), lambda c: (0, c))

    def body(dr_ref, h_ref, a_ref, u_ref, wd_ref, cw_ref, cb_ref, da_ref, du_ref, dwd_ref, dwg_ref, dwu_ref, dc_ref):
        dr = dr_ref[...]
        h = h_ref[...]
        a = a_ref[...]
        u = u_ref[...]
        cw = cw_ref[...]
        dact = _dot_nt(dr, wd_ref[...])
        ac = _conv(a, cw, cb_ref[...])
        cdf = _normal_cdf(ac)
        gelu = ac * cdf
        dwd_ref[...] = _dot_tn((gelu * u).astype(BF16), dr).astype(BF16)
        du = (dact * gelu).astype(BF16)
        dac = dact * u * _gelu_derivative(ac, cdf)
        da = (cw[2:3, :] * dac + cw[1:2, :] * _shift_up(dac, 1) + cw[0:1, :] * _shift_up(dac, 2)).astype(BF16)
        da_ref[...] = da
        du_ref[...] = du
        dwg_ref[...] = _dot_tn(da, h).astype(BF16)
        dwu_ref[...] = _dot_tn(du, h).astype(BF16)
        dc_ref[0:1, :] = jnp.sum(dac * _shift_down(a, 2), axis=0, keepdims=True)
        dc_ref[1:2, :] = jnp.sum(dac * _shift_down(a, 1), axis=0, keepdims=True)
        dc_ref[2:3, :] = jnp.sum(dac * a, axis=0, keepdims=True)
        dc_ref[3:4, :] = jnp.sum(dac, axis=0, keepdims=True)
        dc_ref[4:8, :] = jnp.zeros((4, FF_CHUNK), F32)

    return pl.pallas_call(
        body, name="ffn_bwd", grid=(D_FF // FF_CHUNK,),
        in_specs=[_full(drb.shape), _full(hb.shape), sblk, sblk, wblk, cblk(3), cblk(1)],
        out_specs=[sblk, sblk, wblk, wblk, wblk, cblk(8)],
        out_shape=[_sds((seq, D_FF), BF16), _sds((seq, D_FF), BF16), _sds((D_FF, D_MODEL), BF16),
                   _sds((D_FF, D_MODEL), BF16), _sds((D_FF, D_MODEL), BF16), _sds((8, D_FF), F32)],
        compiler_params=_params("parallel"),
    )(drb, hb, a, u, wd, conv_w, conv_b)


def _ln1_bwd(dr2, da, du, wgt, wut, xhat, rstd, ln_g, tasks=()):
    seq = dr2.shape[0]

    def body(dr2_ref, da_ref, du_ref, wg_ref, wu_ref, xhat_ref, rstd_ref, lg_ref, dr_ref, drb_ref, dg_ref, db_ref):
        @pl.when(pl.program_id(0) == 0)
        def _():
            dg_ref[...] = jnp.zeros_like(dg_ref)
            db_ref[...] = jnp.zeros_like(db_ref)

        dh = ALPHA * dr2_ref[...] + _dot(da_ref[...], wg_ref[...]) + _dot(du_ref[...], wu_ref[...])
        xhat = xhat_ref[...]
        dg_ref[...] += jnp.sum(dh * xhat, axis=0, keepdims=True)
        db_ref[...] += jnp.sum(dh, axis=0, keepdims=True)
        dr = _ln_bwd(dh, xhat, rstd_ref[:, 0:1], lg_ref[...])
        dr_ref[...] = dr
        drb_ref[...] = dr.astype(BF16)

    vec = pl.BlockSpec((1, D_MODEL), lambda i: (0, 0))
    return _grid_call(
        body, "ln1_bwd", seq // TOK,
        in_specs=[_rows(D_MODEL), _rows(D_FF), _rows(D_FF), _full(wgt.shape), _full(wut.shape), _rows(D_MODEL),
                  _rows(LANE), _full((1, D_MODEL))],
        out_specs=[_rows(D_MODEL), _rows(D_MODEL), vec, vec],
        out_shape=[_sds((seq, D_MODEL), F32), _sds((seq, D_MODEL), BF16), _sds((1, D_MODEL), F32),
                   _sds((1, D_MODEL), F32)],
        operands=(dr2, da, du, wgt, wut, xhat, rstd, ln_g), tasks=tasks)


def _mix_bwd(drb, ya, yp, g, mixed, wout, wba, wbp, w_pool, pool_scale, tasks=()):
    seq = drb.shape[0]

    def body(dr_ref, ya_ref, yp_ref, g_ref, mixed_ref, wout_ref, wba_ref, wbp_ref, wp_ref, ps_ref,
             dzg_ref, dya_ref, dyp_ref, do_ref, dmixed_ref, dpooled_ref, dbg_ref, dps_ref):
        @pl.when(pl.program_id(0) == 0)
        def _():
            dbg_ref[...] = jnp.zeros_like(dbg_ref)
            dps_ref[...] = jnp.zeros_like(dps_ref)

        dmerged = _dot_nt(dr_ref[...], wout_ref[...])
        ga, gp = g_ref[:, :D_MODEL], g_ref[:, D_MODEL:]
        dzga = dmerged * ya_ref[...] * ga * (1.0 - ga)
        dzgp = dmerged * yp_ref[...] * gp * (1.0 - gp)
        dzg_ref[:, :D_MODEL] = dzga.astype(BF16)
        dzg_ref[:, D_MODEL:] = dzgp.astype(BF16)
        dbg_ref[:, :D_MODEL] += jnp.sum(dzga, axis=0, keepdims=True)
        dbg_ref[:, D_MODEL:] += jnp.sum(dzgp, axis=0, keepdims=True)
        dya = (dmerged * ga).astype(BF16)
        dyp = (dmerged * gp).astype(BF16)
        dya_ref[...] = dya
        dyp_ref[...] = dyp
        do_ref[...] = _dot_nt(dya, wba_ref[...])
        dypre = _dot_nt(dyp, wbp_ref[...])
        dps_ref[...] += jnp.sum(dypre * mixed_ref[...], axis=0, keepdims=True)
        dmixed = (dypre * ps_ref[...]).astype(BF16)
        dmixed_ref[...] = dmixed
        for grp in range(len(POOL_WINDOWS)):
            cols = slice(POOL_GROUP * grp, POOL_GROUP * (grp + 1))
            dpooled_ref[:, cols] = _dot_nt(dmixed[:, cols], wp_ref[grp].astype(BF16))

    return _grid_call(
        body, "mix_bwd", seq // TOK,
        in_specs=[_rows(D_MODEL), _rows(D_MODEL), _rows(D_MODEL), _rows(2 * D_MODEL), _rows(D_POOL),
                  _full(wout.shape), _full(wba.shape), _full(wbp.shape), _full(w_pool.shape), _full((1, D_POOL))],
        out_specs=[_rows(2 * D_MODEL), _rows(D_MODEL), _rows(D_MODEL), _rows(D_ATTN), _rows(D_POOL), _rows(D_POOL),
                   pl.BlockSpec((1, 2 * D_MODEL), lambda i: (0, 0)), pl.BlockSpec((1, D_POOL), lambda i: (0, 0))],
        out_shape=[_sds((seq, 2 * D_MODEL), BF16), _sds((seq, D_MODEL), BF16), _sds((seq, D_MODEL), BF16),
                   _sds((seq, D_ATTN), F32), _sds((seq, D_POOL), BF16), _sds((seq, D_POOL), F32),
                   _sds((1, 2 * D_MODEL), F32), _sds((1, D_POOL), F32)],
        operands=(drb, ya, yp, g, mixed, wout, wba, wbp, w_pool, pool_scale), tasks=tasks)


def _attn_bwd(q, k, v, bias, o, lse, do, cos, sin, tasks=()):
    seq = q.shape[0]
    nb = seq // MOBA_BLOCK
    pair = pl.BlockSpec((seq, LANE), lambda p: (0, p))
    table = pl.BlockSpec((seq, LANE), lambda p: (0, 0))
    scale = HEAD_DIM ** -0.5

    def body(q_ref, k_ref, v_ref, bias_ref, o_ref, lse_ref, do_ref, cos_ref, sin_ref, dq_ref, dk_ref, dv_ref,
             dq_acc, dk_acc, dv_acc, dk_head, dv_head, ka_sc, qa_sc, s_sc, dp_sc, p_sc, ds_sc):
        for hh in range(LANE // HEAD_DIM):
            ls = slice(HEAD_DIM * hh, HEAD_DIM * (hh + 1))
            _store_keys(ka_sc, k_ref, ls)
            vb = v_ref[:, ls]
            dk_head[...] = jnp.zeros_like(dk_head)
            dv_head[...] = jnp.zeros_like(dv_head)
            for i in range(nb):
                rs = slice(MOBA_BLOCK * i, MOBA_BLOCK * (i + 1))
                width = MOBA_BLOCK * (i + 1)
                qa_sc[:, 0:HEAD_DIM] = (q_ref[rs, ls] * scale).astype(BF16)
                qa_sc[:, HEAD_DIM:] = bias_ref[rs, ls]
                s_sc[0:width, :] = _dot_nt(ka_sc[0:width, :], qa_sc[...])
                s_sc[rs, :] = jnp.where(_causal((MOBA_BLOCK, MOBA_BLOCK), transposed=True), s_sc[rs, :], NEG)
                dob = do_ref[rs, ls]
                delta = _row_vector(jnp.sum(dob * o_ref[rs, ls], axis=1, keepdims=True))
                lse_row = lse_ref[0, hh:hh + 1, rs]
                dob16 = dob.astype(BF16)
                dp_sc[0:width, :] = _dot_nt(vb[0:width], dob16)
                for c in range(width // SCORE_CHUNK):
                    rows = slice(SCORE_CHUNK * c, SCORE_CHUNK * (c + 1))
                    p = jnp.exp(s_sc[rows, :] - lse_row)
                    p_sc[rows, :] = p.astype(BF16)
                    ds_sc[rows, :] = (p * (dp_sc[rows, :] - delta)).astype(BF16)
                dv_head[0:width, :] += _dot(p_sc[0:width, :], dob16)
                dk_head[0:width, :] += _dot(ds_sc[0:width, :], qa_sc[:, 0:HEAD_DIM])
                dq_acc[rs, ls] = _dot_tn(ds_sc[0:width, :], ka_sc[0:width, 0:HEAD_DIM]) * scale
            dk_acc[:, ls] = dk_head[...]
            dv_acc[:, ls] = dv_head[...]
        cos_t, sin_t = cos_ref[...], sin_ref[...]
        dq_ref[...] = _rope_transposed(dq_acc[...], cos_t, sin_t).astype(BF16)
        dk_ref[...] = _rope_transposed(dk_acc[...], cos_t, sin_t).astype(BF16)
        dv_ref[...] = dv_acc[...].astype(BF16)

    return _grid_call(
        body, "attn_bwd", D_ATTN // LANE,
        in_specs=[pair, pair, pair, pair, pair, pl.BlockSpec((1, 8, seq), lambda p: (p, 0, 0)), pair, table, table],
        out_specs=[pair, pair, pair], out_shape=[_sds((seq, D_ATTN), BF16)] * 3,
        operands=(q, k, v, bias, o, lse, do, cos, sin),
        scratch=[pltpu.VMEM((seq, LANE), F32)] * 3 + [pltpu.VMEM((seq, HEAD_DIM), F32)] * 2
        + [pltpu.VMEM((seq, LANE), BF16), pltpu.VMEM((MOBA_BLOCK, LANE), BF16)]
        + [pltpu.VMEM((seq, MOBA_BLOCK), F32)] * 2 + [pltpu.VMEM((seq, MOBA_BLOCK), BF16)] * 2,
        tasks=tasks)


def _in_bwd(dq, dk, dv, dpooled, dzg, dr1, win, tasks=()):
    seq = dr1.shape[0]
    nt = seq // TOK

    def body(dq_ref, dk_ref, dv_ref, dp_ref, dpnext_ref, dzg_ref, dr_ref, win_ref, dx_ref, dz_ref, ext):
        i = pl.program_id(0)
        dp = dp_ref[...]
        dpn = jnp.where(i < nt - 1, dpnext_ref[...], 0.0)
        for grp, window in enumerate(POOL_WINDOWS):
            cols = slice(POOL_GROUP * grp, POOL_GROUP * (grp + 1))
            ext[0:TOK, cols] = dp[:, cols] / _pool_count(i * TOK, TOK, window)
            ext[TOK:, cols] = dpn[:, cols] / _pool_count((i + 1) * TOK, POOL_HALO, window)
        for grp, window in enumerate(POOL_WINDOWS):
            cols = slice(POOL_GROUP * grp, POOL_GROUP * (grp + 1))
            acc = ext[0:TOK, cols] - dp[:, cols]
            for kk in range(1, window):
                acc = acc + ext[pl.ds(kk, TOK), cols]
            dz_ref[:, 3 * D_ATTN + POOL_GROUP * grp:3 * D_ATTN + POOL_GROUP * (grp + 1)] = acc.astype(BF16)
        dz_ref[:, 0:D_ATTN] = dq_ref[...]
        dz_ref[:, D_ATTN:2 * D_ATTN] = dk_ref[...]
        dz_ref[:, 2 * D_ATTN:3 * D_ATTN] = dv_ref[...]
        dz_ref[:, 3 * D_ATTN + D_POOL:] = dzg_ref[...]
        dx = ALPHA * dr_ref[...]
        for n in range(N_DEV):
            dx = dx + _dot_nt(dz_ref[:, D_ATTN * n:D_ATTN * (n + 1)], win_ref[n])
        dx_ref[...] = dx

    halo = pl.BlockSpec((POOL_HALO, D_POOL),
                        lambda i: (jnp.minimum((i + 1) * (TOK // POOL_HALO), seq // POOL_HALO - 1), 0))
    return _grid_call(
        body, "in_bwd", nt,
        in_specs=[_rows(D_ATTN), _rows(D_ATTN), _rows(D_ATTN), _rows(D_POOL), halo, _rows(2 * D_MODEL),
                  _rows(D_MODEL), _full(win.shape)],
        out_specs=[_rows(D_MODEL), _rows(D_IN_PROJ)],
        out_shape=[_sds((seq, D_MODEL), F32), _sds((seq, D_IN_PROJ), BF16)],
        operands=(dq, dk, dv, dpooled, dpooled, dzg, dr1, win),
        scratch=[pltpu.VMEM((TOK + POOL_HALO, D_POOL), F32)], tasks=tasks)


def _dw_mixers(o, ypre, merged, dya, dyp, drb, pooled, dmixed, tasks=()):
    seq = o.shape[0]
    groups = len(POOL_WINDOWS)
    col = pl.BlockSpec((seq, LANE), lambda n: (0, n))
    grp = pl.BlockSpec((seq, POOL_GROUP), lambda n: (0, jnp.minimum(n, groups - 1)))
    owner = lambda rows, cols: pl.BlockSpec((1, rows, cols), lambda n: (n, 0, 0))

    def body(o_ref, ypre_ref, merged_ref, dya_ref, dyp_ref, dr_ref, pooled_ref, dmixed_ref,
             dba_ref, dbp_ref, dout_ref, dpool_ref, ob_sc):
        n = pl.program_id(0)

        @pl.when(n == 0)
        def _():
            ob_sc[...] = o_ref[...].astype(BF16)

        dba_ref[0] = _dot_tn(dya_ref[...], ob_sc[...]).T.astype(BF16)
        dbp_ref[0] = _dot_tn(dyp_ref[...], ypre_ref[...]).T.astype(BF16)
        dout_ref[0] = _dot_tn(merged_ref[...], dr_ref[...]).astype(BF16)

        @pl.when(n < groups)
        def _():
            dpool_ref[0] = _dot_tn(pooled_ref[...], dmixed_ref[...])

    return _grid_call(
        body, "dw_mixers", N_DEV,
        in_specs=[_full(o.shape), _full(ypre.shape), col, col, col, _full(drb.shape), grp, grp],
        out_specs=[owner(D_ATTN, LANE), owner(D_POOL, LANE), owner(D_MODEL // N_DEV, D_MODEL),
                   pl.BlockSpec((1, POOL_GROUP, POOL_GROUP), lambda n: (jnp.minimum(n, groups - 1), 0, 0))],
        out_shape=[_sds((N_DEV, D_ATTN, LANE), BF16), _sds((N_DEV, D_POOL, LANE), BF16),
                   _sds((N_DEV, D_MODEL // N_DEV, D_MODEL), BF16), _sds((groups, POOL_GROUP, POOL_GROUP), F32)],
        operands=(o, ypre, merged, dya, dyp, drb, pooled, dmixed),
        scratch=[pltpu.VMEM((seq, D_ATTN), BF16)], tasks=tasks)


def _to_bf16(arrays, tasks=()):
    n = len(arrays)

    def body(*refs):
        for src, dst in zip(refs[:n], refs[n:]):
            dst[...] = src[...].astype(BF16)

    return _grid_call(
        body, "to_bf16", 1, in_specs=[_full(a.shape) for a in arrays],
        out_specs=[pl.BlockSpec(a.shape, lambda i: (0, 0)) for a in arrays],
        out_shape=[_sds(a.shape, BF16) for a in arrays], operands=arrays, tasks=tasks)


def _matmul(name, a, b, out_shape, out_dtype, steps, a_spec, b_spec, o_spec, tasks=()):
    def body(a_ref, b_ref, o_ref):
        o_ref[...] = _dot(a_ref[...], b_ref[...]).reshape(o_ref.shape).astype(o_ref.dtype)

    (out,), results = _grid_call(body, name, steps, in_specs=[a_spec, b_spec], out_specs=[o_spec],
                                 out_shape=[_sds(out_shape, out_dtype)], operands=(a, b), tasks=tasks)
    return out, results


def _place():
    return lax.axis_index("x"), lax.axis_index("y"), lax.axis_index("c")


def _other_chips(x, y):
    return [(1 - x, y), (x, 1 - y), (1 - x, 1 - y)]


DMA_SEMS = pltpu.SemaphoreType.DMA


class _AllGather:
    def __init__(self, shards, lag=0, hooked=False):
        self.operands = list(shards)
        self.n = len(shards)
        self.lag = lag
        self.hooked = hooked
        self.out_shape = [_sds((N_DEV, *s.shape), s.dtype) for s in shards]
        self.sems = [DMA_SEMS((7 * self.n,)), DMA_SEMS((7 * self.n,)), DMA_SEMS((self.n,))]

    def _copy(self, refs, a, k, block, to, from_input=False):
        ins, outs, (send_sems, recv_sems, _) = refs
        px, py, pc = block
        dst = outs[a].at[4 * px + 2 * py + pc]
        return pltpu.make_async_remote_copy(
            src_ref=ins[a] if from_input else dst, dst_ref=dst,
            send_sem=send_sems.at[7 * a + k], recv_sem=recv_sems.at[7 * a + k],
            device_id=to, device_id_type=MESH)

    def _local(self, refs, a):
        ins, outs, (_, _, local_sems) = refs
        x, y, c = _place()
        return pltpu.make_async_copy(ins[a], outs[a].at[4 * x + 2 * y + c], local_sems.at[a])

    def _pass_on(self, refs, a):
        x, y, c = _place()
        origin = ((x + 1 - c) % 2, (y + c) % 2, c)
        target = ((x + c) % 2, (y + 1 - c) % 2, c)
        return self._copy(refs, a, 3, origin, target)

    def start(self, refs):
        x, y, c = _place()
        for a in range(self.n):
            self._local(refs, a).start()
        for a in range(self.n):
            self._copy(refs, a, 0, (x, y, c), (x, y, 1 - c), True).start()
            for j, chip in enumerate(_other_chips(x, y)[:2]):
                self._copy(refs, a, 1 + j, (x, y, c), (*chip, c), True).start()

    def middle(self, refs):
        x, y, c = _place()
        me, sibling = (x, y, c), (x, y, 1 - c)
        chips = _other_chips(x, y)
        for a in range(self.n):
            for j in range(2):
                self._copy(refs, a, 1 + j, (*chips[j], c), me).wait_recv()
        for a in range(self.n):
            self._pass_on(refs, a).start()
            for j in range(2):
                self._copy(refs, a, 4 + j, (*chips[j], c), sibling).start()

    def late(self, refs):
        x, y, c = _place()
        diagonal = (1 - x, 1 - y, c)
        for a in range(self.n):
            self._copy(refs, a, 3, diagonal, (x, y, c)).wait_recv()
            self._copy(refs, a, 6, diagonal, (x, y, 1 - c)).start()

    def finish(self, refs):
        x, y, c = _place()
        me, sibling = (x, y, c), (x, y, 1 - c)
        chips = _other_chips(x, y)
        for a in range(self.n):
            self._copy(refs, a, 0, sibling, me).wait_recv()
            for j, chip in enumerate(chips):
                self._copy(refs, a, 4 + j, (*chip, 1 - c), me).wait_recv()
        for a in range(self.n):
            self._copy(refs, a, 0, me, sibling, True).wait_send()
            for j, chip in enumerate(chips[:2]):
                self._copy(refs, a, 1 + j, me, (*chip, c), True).wait_send()
            self._pass_on(refs, a).wait_send()
            for j, chip in enumerate(chips):
                self._copy(refs, a, 4 + j, (*chip, c), sibling).wait_send()
            self._local(refs, a).wait()


class _SiblingSend:
    def __init__(self, partials):
        self.operands = list(partials)
        self.n = len(partials)
        self.out_shape = [_sds((4, *p.shape[1:]), p.dtype) for p in partials]
        self.sems = [DMA_SEMS((4 * self.n,)), DMA_SEMS((4 * self.n,))]

    def _copy(self, refs, a, q):
        ins, outs, (send_sems, recv_sems) = refs
        x, y, c = _place()
        return pltpu.make_async_remote_copy(
            src_ref=ins[a].at[2 * q + 1 - c], dst_ref=outs[a].at[q],
            send_sem=send_sems.at[4 * a + q], recv_sem=recv_sems.at[4 * a + q],
            device_id=(x, y, 1 - c), device_id_type=MESH)

    def start(self, refs):
        for a in range(self.n):
            for q in range(4):
                self._copy(refs, a, q).start()

    def middle(self, refs):
        pass

    def finish(self, refs):
        for a in range(self.n):
            for q in range(4):
                self._copy(refs, a, q).wait()


class _ChipScatter:
    def __init__(self, chip_partials):
        self.operands = list(chip_partials)
        self.n = len(chip_partials)
        self.out_shape = [_sds(p.shape, p.dtype) for p in chip_partials]
        self.sems = [DMA_SEMS((3 * self.n,)), DMA_SEMS((3 * self.n,)), DMA_SEMS((self.n,))]

    def _copy(self, refs, a, k, arrival=False):
        ins, outs, (send_sems, recv_sems, _) = refs
        x, y, c = _place()
        px, py = _other_chips(x, y)[k]
        mine, theirs = 2 * x + y, 2 * px + py
        return pltpu.make_async_remote_copy(
            src_ref=ins[a].at[mine if arrival else theirs], dst_ref=outs[a].at[theirs if arrival else mine],
            send_sem=send_sems.at[3 * a + k], recv_sem=recv_sems.at[3 * a + k],
            device_id=(px, py, c), device_id_type=MESH)

    def _local(self, refs, a):
        ins, outs, (_, _, local_sems) = refs
        x, y, _ = _place()
        return pltpu.make_async_copy(ins[a].at[2 * x + y], outs[a].at[2 * x + y], local_sems.at[a])

    def start(self, refs):
        for a in range(self.n):
            self._local(refs, a).start()
            for k in range(3):
                self._copy(refs, a, k).start()

    def middle(self, refs):
        pass

    def finish(self, refs):
        for a in range(self.n):
            for k in range(3):
                self._copy(refs, a, k, arrival=True).wait_recv()
        for a in range(self.n):
            for k in range(3):
                self._copy(refs, a, k).wait_send()
            self._local(refs, a).wait()


class _DirectScatter:
    def __init__(self, partials):
        self.operands = list(partials)
        self.n = len(partials)
        self.out_shape = [_sds(p.shape, p.dtype) for p in partials]
        self.sems = [DMA_SEMS((7 * self.n,)), DMA_SEMS((7 * self.n,)), DMA_SEMS((self.n,))]

    def _copy(self, refs, a, k, arrival=False):
        ins, outs, (send_sems, recv_sems, _) = refs
        x, y, c = _place()
        peer = [(x, y, 1 - c), (1 - x, y, c), (x, 1 - y, c), (1 - x, 1 - y, c),
                (1 - x, y, 1 - c), (x, 1 - y, 1 - c), (1 - x, 1 - y, 1 - c)][k]
        mine, theirs = 4 * x + 2 * y + c, 4 * peer[0] + 2 * peer[1] + peer[2]
        return pltpu.make_async_remote_copy(
            src_ref=ins[a].at[mine if arrival else theirs], dst_ref=outs[a].at[theirs if arrival else mine],
            send_sem=send_sems.at[7 * a + k], recv_sem=recv_sems.at[7 * a + k],
            device_id=peer, device_id_type=MESH)

    def _local(self, refs, a):
        ins, outs, (_, _, local_sems) = refs
        x, y, c = _place()
        return pltpu.make_async_copy(ins[a].at[4 * x + 2 * y + c], outs[a].at[4 * x + 2 * y + c], local_sems.at[a])

    def start(self, refs):
        for a in range(self.n):
            self._local(refs, a).start()
            for k in range(7):
                self._copy(refs, a, k).start()

    def middle(self, refs):
        pass

    def finish(self, refs):
        for a in range(self.n):
            for k in range(7):
                self._copy(refs, a, k, arrival=True).wait_recv()
        for a in range(self.n):
            for k in range(7):
                self._copy(refs, a, k).wait_send()
            self._local(refs, a).wait()


def _task_args(tasks):
    hbm = pl.BlockSpec(memory_space=pl.ANY)
    operands = [o for t in tasks for o in t.operands]
    out_shape = [s for t in tasks for s in t.out_shape]
    sems = [s for t in tasks for s in t.sems]
    return operands, [hbm] * len(operands), out_shape, [hbm] * len(out_shape), sems


def _task_refs(tasks, ins, outs, sems):
    per_task = []
    for t in tasks:
        ni, no, ns = len(t.operands), len(t.out_shape), len(t.sems)
        per_task.append((ins[:ni], outs[:no], sems[:ns]))
        ins, outs, sems = ins[ni:], outs[no:], sems[ns:]
    return per_task


def _task_results(tasks, outs):
    res = []
    for t in tasks:
        res.append(list(outs[:len(t.out_shape)]))
        outs = outs[len(t.out_shape):]
    return res


def _carry(body, tasks, n_in, n_out, n_scratch, steps):
    if not tasks:
        return body
    t_in = sum(len(t.operands) for t in tasks)
    t_out = sum(len(t.out_shape) for t in tasks)

    def wrapped(*refs):
        ins, refs = refs[:n_in], refs[n_in:]
        t_ins, refs = refs[:t_in], refs[t_in:]
        outs, refs = refs[:n_out], refs[n_out:]
        t_outs, refs = refs[:t_out], refs[t_out:]
        scratch, t_sems = refs[:n_scratch], refs[n_scratch:]
        per_task = _task_refs(tasks, t_ins, t_outs, t_sems)
        step = pl.program_id(0)

        @pl.when(step == 0)
        def _():
            for t, r in zip(tasks, per_task):
                t.start(r)

        def second_phase(hooked):
            for t, r in zip(tasks, per_task):
                if getattr(t, "hooked", False) == hooked:
                    pl.when(step == max(steps - 1 - getattr(t, "lag", 0), 0))(functools.partial(t.middle, r))

        second_phase(False)
        for t, r in zip(tasks, per_task):
            if hasattr(t, "late") and not getattr(t, "hooked", False):
                pl.when(step == steps - 1)(functools.partial(t.late, r))

        if any(getattr(t, "hooked", False) for t in tasks):
            body(*ins, *outs, *scratch, hook=functools.partial(second_phase, True))
        else:
            body(*ins, *outs, *scratch)

        @pl.when(step == steps - 1)
        def _():
            for t, r in zip(tasks, per_task):
                if hasattr(t, "late") and getattr(t, "hooked", False):
                    t.late(r)
                t.finish(r)

    return wrapped


def _exchange(name, tasks):
    operands, in_specs, out_shape, out_specs, sems = _task_args(tasks)

    def body(*refs):
        ni, no = len(operands), len(out_shape)
        per_task = _task_refs(tasks, refs[:ni], refs[ni:ni + no], refs[ni + no:])
        for phase in ("start", "middle", "late", "finish"):
            for t, r in zip(tasks, per_task):
                if hasattr(t, phase):
                    getattr(t, phase)(r)

    outs = pl.pallas_call(body, name=name, in_specs=in_specs, out_specs=out_specs, out_shape=out_shape,
                          scratch_shapes=sems)(*operands)
    return _task_results(tasks, outs)


def _row_tile(rows, cols, whole_up_to=256 * 1024):
    if rows * cols <= whole_up_to:
        return rows
    for t in (256, 176, 128, 64, 32, 16, 8):
        if rows % t == 0:
            return t
    return rows


def _pair_sum(name, partials, from_sibling):
    n = len(partials)

    def body(core_ref, *refs):
        for p_ref, s_ref, o_ref in zip(refs[:n], refs[n:2 * n], refs[2 * n:]):
            o_ref[0] = (p_ref[0, 0].astype(F32) + s_ref[0].astype(F32)).astype(o_ref.dtype)

    blk = [pl.BlockSpec((1, *s.shape[1:]), lambda q, core: (q, 0, 0)) for s in from_sibling]
    mine = [pl.BlockSpec((1, 1, *p.shape[1:]), lambda q, core: (q, core[0], 0, 0)) for p in partials]
    return pl.pallas_call(
        body, name=name,
        grid_spec=pltpu.PrefetchScalarGridSpec(num_scalar_prefetch=1, grid=(4,), in_specs=mine + blk, out_specs=blk),
        out_shape=[_sds(s.shape, s.dtype) for s in from_sibling],
        compiler_params=_params("parallel"),
    )(lax.axis_index("c").astype(jnp.int32).reshape(1), *[p.reshape(4, 2, *p.shape[1:]) for p in partials],
      *from_sibling)


def _adamw_math(w, g, m, v):
    nm = ADAM_B1 * m + (1.0 - ADAM_B1) * g
    nv = ADAM_B2 * v + (1.0 - ADAM_B2) * (g * g)
    m_hat = nm / (1.0 - ADAM_B1 ** ADAM_STEP)
    v_hat = nv / (1.0 - ADAM_B2 ** ADAM_STEP)
    return -ADAM_LR * (m_hat / (jnp.sqrt(v_hat) + ADAM_EPS) + ADAM_WD * w), nm, nv


def _update_small(params, loss_parts):
    n = len(params)

    def whole(shape):
        return pl.BlockSpec(shape, lambda i, rank=len(shape): (0,) * rank)

    def total(ref):
        acc = ref[0]
        for d in range(1, N_DEV):
            acc = acc + ref[d]
        return acc

    def body(*refs):
        ins, loss_ref, outs, loss_out = refs[:4 * n], refs[4 * n], refs[4 * n + 1:8 * n + 1], refs[8 * n + 1]
        for p in range(n):
            s_ref, w_ref, m_ref, v_ref = ins[4 * p:4 * p + 4]
            g_ref, d_ref, nm_ref, nv_ref = outs[4 * p:4 * p + 4]
            g = total(s_ref)
            g_ref[...] = g
            d_ref[...], nm_ref[...], nv_ref[...] = _adamw_math(w_ref[...], g, m_ref[...], v_ref[...])
        loss_out[...] = total(loss_ref)

    outs = pl.pallas_call(
        body, name="update_small", grid=(1,),
        in_specs=[whole(t.shape) for p in params for t in p] + [whole(loss_parts.shape)],
        out_specs=[whole(p[1].shape) for p in params for _ in range(4)] + [whole(loss_parts.shape[1:])],
        out_shape=[_sds(p[1].shape, F32) for p in params for _ in range(4)] + [_sds(loss_parts.shape[1:], F32)],
        compiler_params=_params("arbitrary"),
    )(*[t for p in params for t in p], loss_parts)
    return [outs[4 * p:4 * p + 4] for p in range(n)], outs[4 * n]


def _sum_adamw(name, params, tasks=()):
    n = len(params)
    parts = params[0][0].shape[0]
    shapes = [p[1].shape for p in params]
    tiles = [_row_tile(*shapes[0])] * n if len(set(shapes)) == 1 else [rows for rows, _ in shapes]
    steps = shapes[0][0] // tiles[0]

    def body(*refs):
        ins, outs = refs[:4 * n], refs[4 * n:]
        for p in range(n):
            s_ref, w_ref, m_ref, v_ref = ins[4 * p:4 * p + 4]
            g_ref, d_ref, nm_ref, nv_ref = outs[4 * p:4 * p + 4]
            g = s_ref[0].astype(F32)
            for d in range(1, parts):
                g = g + s_ref[d].astype(F32)
            g_ref[...] = g
            d_ref[...], nm_ref[...], nv_ref[...] = _adamw_math(w_ref[...], g, m_ref[...], v_ref[...])

    blk = [pl.BlockSpec((t, cols), lambda i: (i, 0)) for t, (_, cols) in zip(tiles, shapes)]
    stacked = [pl.BlockSpec((parts, t, cols), lambda i: (0, i, 0)) for t, (_, cols) in zip(tiles, shapes)]
    outs, results = _grid_call(
        body, name, steps,
        in_specs=[spec for p in range(n) for spec in (stacked[p], blk[p], blk[p], blk[p])],
        out_specs=[blk[p] for p in range(n) for _ in range(4)],
        out_shape=[_sds(shapes[p], F32) for p in range(n) for _ in range(4)],
        operands=[t for p in params for t in p], tasks=tasks)
    return [outs[4 * p:4 * p + 4] for p in range(n)], results


SMALL = ("b_gate", "w_pool", "pool_scale", "ln1_g", "ln1_b", "conv_b", "ln2_g", "ln2_b")
MIXER = ("w_branch_attn", "w_branch_pool", "w_out", "conv_w")
FFN = ("w_ffn_gate_t", "w_ffn_up_t", "w_ffn_down")


def _columns(t):
    return jnp.transpose(t, (1, 0, 2)).reshape(t.shape[1], N_DEV * t.shape[2])


def _row_blocks(t):
    return t.reshape(N_DEV * t.shape[1], t.shape[2])


def _by_owner(t):
    return t.reshape(N_DEV, t.shape[0] // N_DEV, t.shape[1])


def _reduce_halves(names, partials, from_sibling):
    return _pair_sum("pair_sum_" + names[0], partials, from_sibling)


def _local_step(x, target, shards, small):
    seq = x.shape[0]
    cos, sin = _rope_tables(seq)
    cast, ((w_in_all,),) = _to_bf16([shards[n] for n in MIXER[:3] + FFN], tasks=[_AllGather([shards["w_in"]])])
    shards = {**shards, **dict(zip(MIXER[:3] + FFN, cast))}
    (xt, q, k, v, u, g, kmean), (mixer,) = _proj_in(
        x, w_in_all, small["b_gate"], cos, sin, tasks=[_AllGather([shards[n] for n in MIXER], lag=2)])
    wba, wbp, wout, conv_w = _columns(mixer[0]), _columns(mixer[1]), _row_blocks(mixer[2]), _columns(mixer[3])
    (o, lse, bias), ((wgt, wut),) = _attn_fwd(
        q, k, v, kmean.reshape(seq // MOBA_BLOCK, D_ATTN),
        tasks=[_AllGather([shards["w_ffn_gate_t"], shards["w_ffn_up_t"]], lag=1, hooked=True)])
    (ya, yp, pooled, mixed, ypre, merged, xhat1, rstd1, h1, h1b), _ = _mix(
        o, u, g, x, wba, wbp, wout, small["w_pool"], small["pool_scale"], small["ln1_g"], small["ln1_b"])
    wgt, wut = _row_blocks(wgt), _row_blocks(wut)
    (a, uf, act), ((wd,),) = _ffn_up(
        h1b, wgt, wut, conv_w, small["conv_b"], tasks=[_AllGather([shards["w_ffn_down"]], lag=4)])
    wd = _row_blocks(wd)
    dr2, dr2b, loss, dg2, db2 = _ffn_down(act, wd, h1, target, small["ln2_g"], small["ln2_b"])

    da, du, dwd, dwg, dwu, dconv = _ffn_bwd(dr2b, h1b, a, uf, wd, conv_w, small["conv_b"])
    ffn_partials = [_by_owner(dwg), _by_owner(dwu), _by_owner(dwd)]
    (dr1, dr1b, dg1, db1), (ffn_sibling,) = _ln1_bwd(
        dr2, da, du, wgt, wut, xhat1, rstd1, small["ln1_g"], tasks=[_SiblingSend(ffn_partials)])
    ffn_chip = _reduce_halves(FFN, ffn_partials, ffn_sibling)
    (dzg, dya, dyp, do, dmixed, dpooled, dbg, dps), (gate_landed,) = _mix_bwd(
        dr1b, ya, yp, g, mixed, wout, wba, wbp, small["w_pool"], small["pool_scale"],
        tasks=[_ChipScatter(ffn_chip[0:1])])
    (dw_ba, dw_bp, dw_out, dw_pool), _ = _dw_mixers(o, ypre, merged, dya, dyp, dr1b, pooled, dmixed)
    mixer_partials = [dw_ba, dw_bp, dw_out]
    (dq, dk, dv), (up_down_landed, mixer_sibling) = _attn_bwd(
        q, k, v, bias, o, lse, do, cos, sin, tasks=[_ChipScatter(ffn_chip[1:3]), _SiblingSend(mixer_partials)])
    mixer_chip = _reduce_halves(MIXER[:3], mixer_partials, mixer_sibling)
    (grad_x, dz), _ = _in_bwd(dq, dk, dv, dpooled, dzg, dr1, w_in_all)
    little = [dbg, dw_pool, dps, dg1, db1, dconv[3:4], dg2, db2]
    conv_w_partials = dconv[0:3].reshape(3, N_DEV, FF_SHARD).transpose(1, 0, 2)
    dw_in, (mixer_landed,) = _matmul(
        "dw_in", xt, dz, (N_DEV, D_MODEL, D_ATTN), BF16, N_DEV,
        _full(xt.shape), pl.BlockSpec((seq, D_ATTN), lambda n: (0, n)),
        pl.BlockSpec((1, D_MODEL, D_ATTN), lambda n: (n, 0, 0)), tasks=[_ChipScatter(mixer_chip)])

    landed = dict(zip(FFN + MIXER[:3], gate_landed + up_down_landed + mixer_landed))
    return grad_x, landed, dw_in, little, conv_w_partials, loss


def kernel(x, w_in, b_gate, w_branch_attn, w_pool, pool_scale, w_branch_pool, w_out, ln1_g, ln1_b, w_ffn_gate, w_ffn_up, conv_w, conv_b, w_ffn_down, ln2_g, ln2_b, loss_target, m_w_in, m_b_gate, m_w_branch_attn, m_w_pool, m_pool_scale, m_w_branch_pool, m_w_out, m_ln1_g, m_ln1_b, m_w_ffn_gate, m_w_ffn_up, m_conv_w, m_conv_b, m_w_ffn_down, m_ln2_g, m_ln2_b, v_w_in, v_b_gate, v_w_branch_attn, v_w_pool, v_pool_scale, v_w_branch_pool, v_w_out, v_ln1_g, v_ln1_b, v_w_ffn_gate, v_w_ffn_up, v_conv_w, v_conv_b, v_w_ffn_down, v_ln2_g, v_ln2_b):
    weights = dict(w_in=w_in, b_gate=b_gate, w_branch_attn=w_branch_attn, w_pool=w_pool, pool_scale=pool_scale,
                   w_branch_pool=w_branch_pool, w_out=w_out, ln1_g=ln1_g, ln1_b=ln1_b, w_ffn_gate=w_ffn_gate,
                   w_ffn_up=w_ffn_up, conv_w=conv_w, conv_b=conv_b, w_ffn_down=w_ffn_down, ln2_g=ln2_g, ln2_b=ln2_b)
    m_in = dict(w_in=m_w_in, b_gate=m_b_gate, w_branch_attn=m_w_branch_attn, w_pool=m_w_pool,
                pool_scale=m_pool_scale, w_branch_pool=m_w_branch_pool, w_out=m_w_out, ln1_g=m_ln1_g, ln1_b=m_ln1_b,
                w_ffn_gate=m_w_ffn_gate, w_ffn_up=m_w_ffn_up, conv_w=m_conv_w, conv_b=m_conv_b,
                w_ffn_down=m_w_ffn_down, ln2_g=m_ln2_g, ln2_b=m_ln2_b)
    v_in = dict(w_in=v_w_in, b_gate=v_b_gate, w_branch_attn=v_w_branch_attn, w_pool=v_w_pool,
                pool_scale=v_pool_scale, w_branch_pool=v_w_branch_pool, w_out=v_w_out, ln1_g=v_ln1_g, ln1_b=v_ln1_b,
                w_ffn_gate=v_w_ffn_gate, w_ffn_up=v_w_ffn_up, conv_w=v_conv_w, conv_b=v_conv_b,
                w_ffn_down=v_w_ffn_down, ln2_g=v_ln2_g, ln2_b=v_ln2_b)
    weights = {n: a[0] for n, a in weights.items()}
    m_in = {n: a[0] for n, a in m_in.items()}
    v_in = {n: a[0] for n, a in v_in.items()}

    shards = {"w_in": weights["w_in"].astype(BF16), "w_branch_attn": weights["w_branch_attn"],
              "w_branch_pool": weights["w_branch_pool"], "w_out": weights["w_out"],
              "w_ffn_gate_t": weights["w_ffn_gate"].T, "w_ffn_up_t": weights["w_ffn_up"].T,
              "w_ffn_down": weights["w_ffn_down"], "conv_w": weights["conv_w"]}
    small = {"b_gate": weights["b_gate"][None], "w_pool": weights["w_pool"], "pool_scale": weights["pool_scale"][None],
             "ln1_g": weights["ln1_g"][None], "ln1_b": weights["ln1_b"][None], "conv_b": weights["conv_b"][None],
             "ln2_g": weights["ln2_g"][None], "ln2_b": weights["ln2_b"][None]}

    grad_x, landed, dw_in, little, conv_w_partials, loss_part = _local_step(x[0], loss_target[0], shards, small)

    grads, delta, new_m, new_v = {}, {}, {}, {}

    def param(n, transposed=False):
        if transposed:
            return landed[n + "_t"], weights[n].T, m_in[n].T, v_in[n].T
        return landed[n], weights[n], m_in[n], v_in[n]

    def keep(n, updated, transposed=False):
        grads[n], delta[n], new_m[n], new_v[n] = (t.T for t in updated) if transposed else updated

    ((w_in_sibling,),) = _exchange("sibling_grads", [_SiblingSend([dw_in])])
    w_in_chip = _reduce_halves(["w_in"], [dw_in], [w_in_sibling])
    (landed["w_in"],), (*small_all, loss_all), (conv_w_all,) = _exchange(
        "scatter_grads",
        [_ChipScatter(w_in_chip), _AllGather(little + [loss_part]), _DirectScatter([conv_w_partials])])
    (gate, up, down), _ = _sum_adamw(
        "update_w_ffn", [param("w_ffn_gate", True), param("w_ffn_up", True), param("w_ffn_down")])
    keep("w_ffn_gate", gate, True)
    keep("w_ffn_up", up, True)
    keep("w_ffn_down", down)
    mixers = ("w_branch_attn", "w_branch_pool", "w_out")
    for n, updated in zip(mixers, _sum_adamw("update_mixers", [param(n) for n in mixers])[0]):
        keep(n, updated)
    keep("w_in", _sum_adamw("update_w_in", [param("w_in")])[0][0])
    names = SMALL + ("conv_w",)
    rows = lambda t: t if t.ndim > 1 else t[None]
    small_updates, loss = _update_small(
        [(s, rows(weights[n]), rows(m_in[n]), rows(v_in[n])) for n, s in zip(names, small_all + [conv_w_all])],
        loss_all)
    for n, updated in zip(names, small_updates):
        keep(n, [t.reshape(weights[n].shape) for t in updated])
    loss = loss[0, 0]

    order = ("w_in", "b_gate", "w_branch_attn", "w_pool", "pool_scale", "w_branch_pool", "w_out", "ln1_g", "ln1_b",
             "w_ffn_gate", "w_ffn_up", "conv_w", "conv_b", "w_ffn_down", "ln2_g", "ln2_b")
    lead = lambda t: t[None]
    return (loss, lead(grad_x), *[lead(grads[n]) for n in order], *[lead(delta[n]) for n in order],
            *[lead(new_m[n]) for n in order], *[lead(new_v[n]) for n in order])
```

```python
import functools
import math

import jax
import jax.numpy as jnp
from jax import lax
from jax.experimental import pallas as pl
from jax.experimental.pallas import tpu as pltpu

F32 = jnp.float32
BF16 = jnp.bfloat16

D_MODEL = 1024
N_HEADS = 8
HEAD_DIM = 64
D_ATTN = N_HEADS * HEAD_DIM
MOBA_BLOCK = 256
MOBA_TOPK = 3
ROPE_THETA = 10000.0
POOL_WINDOWS = (2, 4, 8, 16)
POOL_GROUP = 128
D_POOL = len(POOL_WINDOWS) * POOL_GROUP
POOL_HALO = 16
D_FF = 2816
D_IN_PROJ = 3 * D_ATTN + D_POOL + 2 * D_MODEL
LN_EPS = 1e-5
ALPHA = 2.0 ** 0.25
NEG = -1e30
N_DEV = 8
FF_SHARD = D_FF // N_DEV

ADAM_LR = 0.001
ADAM_B1 = 0.9
ADAM_B2 = 0.999
ADAM_EPS = 1e-08
ADAM_WD = 0.01
ADAM_STEP = 10

TOK = 256
FF_CHUNK = 256
LANE = 128
VMEM_LIMIT = 56 * 1024 * 1024

MESH = pl.DeviceIdType.MESH
NT_DIMS = (((1,), (1,)), ((), ()))
TN_DIMS = (((0,), (0,)), ((), ()))


def _params(*sem):
    return pltpu.CompilerParams(dimension_semantics=sem or None, vmem_limit_bytes=VMEM_LIMIT)


def _full(shape):
    zeros = (0,) * len(shape)
    return pl.BlockSpec(shape, lambda *_: zeros, pipeline_mode=pl.Buffered(1))


def _rows(width, tile=TOK):
    return pl.BlockSpec((tile, width), lambda i: (i, 0))


def _sds(shape, dtype):
    return jax.ShapeDtypeStruct(shape, dtype)


def _dot(a, b):
    return jnp.dot(a, b, preferred_element_type=F32)


def _dot_nt(a, b):
    return lax.dot_general(a, b, NT_DIMS, preferred_element_type=F32)


def _dot_tn(a, b):
    return lax.dot_general(a, b, TN_DIMS, preferred_element_type=F32)


def _rope_tables(seq):
    half = HEAD_DIM // 2
    inv_freq = 1.0 / (ROPE_THETA ** (jnp.arange(half, dtype=F32) / half))
    ang = jnp.arange(seq, dtype=F32)[:, None] * inv_freq[None, :]
    cos, sin = jnp.cos(ang), jnp.sin(ang)
    return jnp.tile(cos, (1, 4)), jnp.tile(jnp.concatenate([-sin, sin], axis=1), (1, 2))


def _swap_halves(t):
    lane = lax.broadcasted_iota(jnp.int32, t.shape, 1)
    return jnp.where((lane % HEAD_DIM) < HEAD_DIM // 2, pltpu.roll(t, LANE - 32, 1), pltpu.roll(t, 32, 1))


def _rope(t, cos, sin):
    return t * cos + _swap_halves(t) * sin


def _rope_transposed(g, cos, sin):
    return g * cos + _swap_halves(g * sin)


def _ln_fwd(r, g, b):
    mu = jnp.mean(r, axis=-1, keepdims=True)
    xc = r - mu
    var = jnp.mean(xc * xc, axis=-1, keepdims=True)
    rstd = lax.rsqrt(var + LN_EPS)
    xhat = xc * rstd
    return xhat * g + b, xhat, rstd


def _ln_bwd(dy, xhat, rstd, g):
    dxh = dy * g
    m1 = jnp.mean(dxh, axis=-1, keepdims=True)
    m2 = jnp.mean(dxh * xhat, axis=-1, keepdims=True)
    return rstd * (dxh - m1 - xhat * m2)


def _normal_cdf(a):
    return 0.5 * (1.0 + lax.erf(a * (1.0 / math.sqrt(2.0))))


def _gelu_derivative(a, cdf):
    return cdf + a * (jnp.exp(-0.5 * a * a) * (1.0 / math.sqrt(2.0 * math.pi)))


def _shift_down(a, k):
    row = lax.broadcasted_iota(jnp.int32, a.shape, 0)
    return jnp.where(row >= k, pltpu.roll(a, k, 0), 0.0)


def _shift_up(a, k):
    n = a.shape[0]
    row = lax.broadcasted_iota(jnp.int32, a.shape, 0)
    return jnp.where(row < n - k, pltpu.roll(a, n - k, 0), 0.0)


def _conv(a, cw, cb):
    return cw[2:3, :] * a + cw[1:2, :] * _shift_down(a, 1) + cw[0:1, :] * _shift_down(a, 2) + cb


def _pool_count(first_row, rows, window):
    t = first_row + lax.broadcasted_iota(jnp.int32, (rows, 1), 0)
    return jnp.minimum(t + 1, window).astype(F32)


def _grid_call(body, name, steps, in_specs, out_specs, out_shape, operands, scratch=(), tasks=()):
    t_operands, t_in_specs, t_out_shape, t_out_specs, t_sems = _task_args(tasks)
    outs = pl.pallas_call(
        _carry(body, tasks, len(in_specs), len(out_specs), len(scratch), steps), name=name, grid=(steps,),
        in_specs=list(in_specs) + t_in_specs, out_specs=list(out_specs) + t_out_specs,
        out_shape=list(out_shape) + t_out_shape, scratch_shapes=list(scratch) + t_sems,
        compiler_params=_params("arbitrary"),
    )(*operands, *t_operands)
    return outs[:len(out_specs)], _task_results(tasks, outs[len(out_specs):])


def _proj_in(x, win, b_gate, cos, sin, tasks=()):
    seq = x.shape[0]
    nt = seq // TOK

    def body(x_ref, win_ref, bg_ref, cos_ref, sin_ref, xt_ref, q_ref, k_ref, v_ref, u_ref, g_ref, km_ref):
        xb = x_ref[...].astype(BF16)
        xt_ref[...] = x_ref[...].T.astype(BF16)
        cos_t, sin_t = cos_ref[...], sin_ref[...]
        for sec, out_ref in ((0, q_ref), (1, k_ref)):
            z = _dot(xb, win_ref[sec])
            for c in range(D_ATTN // LANE):
                cols = slice(LANE * c, LANE * (c + 1))
                out_ref[:, cols] = _rope(z[:, cols], cos_t, sin_t)
        for b in range(TOK // MOBA_BLOCK):
            km_ref[b] = jnp.mean(k_ref[MOBA_BLOCK * b:MOBA_BLOCK * (b + 1), :], axis=0, keepdims=True)
        v_ref[...] = _dot(xb, win_ref[2]).astype(BF16)
        u_ref[...] = _dot(xb, win_ref[3])
        for n in range(4):
            cols = slice(D_ATTN * n, D_ATTN * (n + 1))
            g_ref[:, cols] = jax.nn.sigmoid(_dot(xb, win_ref[4 + n]) + bg_ref[:, cols])

    return _grid_call(
        body, "proj_in", nt,
        in_specs=[_rows(D_MODEL), _full(win.shape), _full((1, 2 * D_MODEL)), _rows(LANE), _rows(LANE)],
        out_specs=[pl.BlockSpec((D_MODEL, TOK), lambda i: (0, i)), _rows(D_ATTN), _rows(D_ATTN), _rows(D_ATTN),
                   _rows(D_POOL), _rows(2 * D_MODEL),
                   pl.BlockSpec((TOK // MOBA_BLOCK, 1, D_ATTN), lambda i: (i, 0, 0))],
        out_shape=[_sds((D_MODEL, seq), BF16), _sds((seq, D_ATTN), F32), _sds((seq, D_ATTN), F32),
                   _sds((seq, D_ATTN), BF16), _sds((seq, D_POOL), F32), _sds((seq, 2 * D_MODEL), F32),
                   _sds((seq // MOBA_BLOCK, 1, D_ATTN), F32)],
        operands=(x, win, b_gate, cos, sin), tasks=tasks)


SCORE_CHUNK = 128


def _store_keys(ka_sc, k_ref, ls):
    seq = ka_sc.shape[0]
    ka_sc[:, 0:HEAD_DIM] = k_ref[:, ls].astype(BF16)
    row = lax.broadcasted_iota(jnp.int32, (seq, HEAD_DIM), 0)
    lane = lax.broadcasted_iota(jnp.int32, (seq, HEAD_DIM), 1)
    in_block = (lane * MOBA_BLOCK <= row) & (row < (lane + 1) * MOBA_BLOCK)
    ka_sc[:, HEAD_DIM:] = jnp.where(in_block, 1.0, 0.0).astype(BF16)


def _block_bias(qf, km, i):
    if i <= MOBA_TOPK:
        return jnp.zeros((MOBA_BLOCK, HEAD_DIM), BF16)
    nb = km.shape[0]
    gate = lax.dot_general(km, qf, NT_DIMS, precision=lax.Precision.HIGHEST, preferred_element_type=F32)
    blk = lax.broadcasted_iota(jnp.int32, gate.shape, 0)
    rank = jnp.zeros(gate.shape, F32)
    for r in range(1, i):
        lower = pltpu.roll(gate, r, 0)
        rank = rank + jnp.where((blk >= r) & (lower >= gate), 1.0, 0.0)
        higher = pltpu.roll(gate, nb - r, 0)
        rank = rank + jnp.where((blk + r < i) & (higher > gate), 1.0, 0.0)
    bias = jnp.where((blk < i) & (rank >= MOBA_TOPK), NEG, 0.0)
    padded = jnp.concatenate([bias, jnp.zeros((LANE - nb, MOBA_BLOCK), F32)], axis=0)
    return jnp.transpose(padded)[:, 0:HEAD_DIM].astype(BF16)


def _causal(shape, transposed=False):
    row = lax.broadcasted_iota(jnp.int32, shape, 0)
    col = lax.broadcasted_iota(jnp.int32, shape, 1)
    return (row <= col) if transposed else (col <= row)


def _row_vector(col):
    return jnp.transpose(jnp.broadcast_to(col, (MOBA_BLOCK, LANE)))[0:1, :]


def _attn_fwd(q, k, v, kmean, tasks=()):
    seq = q.shape[0]
    nb = seq // MOBA_BLOCK
    assert nb == 8, "the block ranking keeps one sublane per key block"
    pair = pl.BlockSpec((seq, LANE), lambda p: (0, p))
    heads = LANE // HEAD_DIM

    def body(q_ref, k_ref, v_ref, km_ref, o_ref, lse_ref, bias_ref, ka_sc, qa_sc, s_sc, p_sc):
        lse_ref[0, heads:, :] = jnp.zeros((8 - heads, seq), F32)
        for hh in range(heads):
            ls = slice(HEAD_DIM * hh, HEAD_DIM * (hh + 1))
            _store_keys(ka_sc, k_ref, ls)
            vb = v_ref[:, ls]
            km = km_ref[:, ls]
            for i in range(nb):
                rs = slice(MOBA_BLOCK * i, MOBA_BLOCK * (i + 1))
                width = MOBA_BLOCK * (i + 1)
                qf = q_ref[rs, ls]
                bias = _block_bias(qf, km, i)
                bias_ref[rs, ls] = bias
                qa_sc[:, 0:HEAD_DIM] = (qf * HEAD_DIM ** -0.5).astype(BF16)
                qa_sc[:, HEAD_DIM:] = bias
                s_sc[:, 0:width] = _dot_nt(qa_sc[...], ka_sc[0:width, :])
                s_sc[:, rs] = jnp.where(_causal((MOBA_BLOCK, MOBA_BLOCK)), s_sc[:, rs], NEG)
                chunks = [slice(SCORE_CHUNK * c, SCORE_CHUNK * (c + 1)) for c in range(width // SCORE_CHUNK)]
                top = s_sc[:, chunks[0]]
                for c in chunks[1:]:
                    top = jnp.maximum(top, s_sc[:, c])
                m = jnp.max(top, axis=1, keepdims=True)
                total = jnp.zeros((MOBA_BLOCK, SCORE_CHUNK), F32)
                for c in chunks:
                    p = jnp.exp(s_sc[:, c] - m)
                    total = total + p
                    p_sc[:, c] = p.astype(BF16)
                l = jnp.sum(total, axis=1, keepdims=True)
                o_ref[rs, ls] = _dot(p_sc[:, 0:width], vb[0:width]) / l
                lse_ref[0, hh:hh + 1, rs] = _row_vector(m + jnp.log(l))

    return _grid_call(
        body, "attn_fwd", D_ATTN // LANE,
        in_specs=[pair, pair, pair, pl.BlockSpec((nb, LANE), lambda p: (0, p))],
        out_specs=[pair, pl.BlockSpec((1, 8, seq), lambda p: (p, 0, 0)), pair],
        out_shape=[_sds((seq, D_ATTN), F32), _sds((D_ATTN // LANE, 8, seq), F32), _sds((seq, D_ATTN), BF16)],
        operands=(q, k, v, kmean),
        scratch=[pltpu.VMEM((seq, LANE), BF16), pltpu.VMEM((MOBA_BLOCK, LANE), BF16),
                 pltpu.VMEM((MOBA_BLOCK, seq), F32), pltpu.VMEM((MOBA_BLOCK, seq), BF16)],
        tasks=tasks)


def _mix(o, u, g, x, wba, wbp, wout, w_pool, pool_scale, ln_g, ln_b, tasks=()):
    seq = x.shape[0]

    def body(o_ref, u_ref, uprev_ref, g_ref, x_ref, wba_ref, wbp_ref, wout_ref, wp_ref, ps_ref, lg_ref, lb_ref,
             ya_ref, yp_ref, pooled_ref, mixed_ref, ypre_ref, merged_ref, xhat_ref, rstd_ref, h_ref, hb_ref, ext):
        i = pl.program_id(0)
        ya = _dot(o_ref[...].astype(BF16), wba_ref[...])
        ucur = u_ref[...]
        ext[0:POOL_HALO, :] = jnp.where(i > 0, uprev_ref[...], 0.0)
        ext[POOL_HALO:, :] = ucur
        for grp, window in enumerate(POOL_WINDOWS):
            cols = slice(POOL_GROUP * grp, POOL_GROUP * (grp + 1))
            acc = ucur[:, cols]
            for kk in range(1, window):
                acc = acc + ext[pl.ds(POOL_HALO - kk, TOK), cols]
            pooled = acc / _pool_count(i * TOK, TOK, window) - ucur[:, cols]
            pooled_ref[:, cols] = pooled.astype(BF16)
            mixed_ref[:, cols] = _dot(pooled.astype(BF16), wp_ref[grp].astype(BF16))
        mixed = mixed_ref[...]
        ypre = (mixed * ps_ref[...]).astype(BF16)
        ypre_ref[...] = ypre
        yp = _dot(ypre, wbp_ref[...])
        ya_ref[...] = ya.astype(BF16)
        yp_ref[...] = yp.astype(BF16)
        merged = (g_ref[:, :D_MODEL] * ya + g_ref[:, D_MODEL:] * yp).astype(BF16)
        merged_ref[...] = merged
        r1 = ALPHA * x_ref[...] + _dot(merged, wout_ref[...])
        h, xhat, rstd = _ln_fwd(r1, lg_ref[...], lb_ref[...])
        xhat_ref[...] = xhat
        rstd_ref[...] = jnp.broadcast_to(rstd, (TOK, LANE))
        h_ref[...] = h
        hb_ref[...] = h.astype(BF16)

    halo = pl.BlockSpec((POOL_HALO, D_POOL), lambda i: (jnp.maximum(i * (TOK // POOL_HALO) - 1, 0), 0))
    return _grid_call(
        body, "mix", seq // TOK,
        in_specs=[_rows(D_ATTN), _rows(D_POOL), halo, _rows(2 * D_MODEL), _rows(D_MODEL),
                  _full(wba.shape), _full(wbp.shape), _full(wout.shape), _full(w_pool.shape),
                  _full((1, D_POOL)), _full((1, D_MODEL)), _full((1, D_MODEL))],
        out_specs=[_rows(D_MODEL), _rows(D_MODEL), _rows(D_POOL), _rows(D_POOL), _rows(D_POOL), _rows(D_MODEL),
                   _rows(D_MODEL), _rows(LANE), _rows(D_MODEL), _rows(D_MODEL)],
        out_shape=[_sds((seq, D_MODEL), BF16), _sds((seq, D_MODEL), BF16), _sds((seq, D_POOL), BF16),
                   _sds((seq, D_POOL), F32), _sds((seq, D_POOL), BF16), _sds((seq, D_MODEL), BF16),
                   _sds((seq, D_MODEL), F32), _sds((seq, LANE), F32), _sds((seq, D_MODEL), F32),
                   _sds((seq, D_MODEL), BF16)],
        operands=(o, u, u, g, x, wba, wbp, wout, w_pool, pool_scale, ln_g, ln_b),
        scratch=[pltpu.VMEM((TOK + POOL_HALO, D_POOL), F32)], tasks=tasks)


def _ffn_up(hb, wgt, wut, conv_w, conv_b, tasks=()):
    seq = hb.shape[0]
    wblk = pl.BlockSpec((FF_CHUNK, D_MODEL), lambda c: (c, 0))
    cblk = lambda rows: pl.BlockSpec((rows, FF_CHUNK), lambda c: (0, c))
    oblk = pl.BlockSpec((seq, FF_CHUNK), lambda c: (0, c))

    def body(h_ref, wg_ref, wu_ref, cw_ref, cb_ref, a_ref, u_ref, act_ref):
        h = h_ref[...]
        a = _dot_nt(h, wg_ref[...])
        u = _dot_nt(h, wu_ref[...])
        a_ref[...] = a
        u_ref[...] = u
        ac = _conv(a, cw_ref[...], cb_ref[...])
        act_ref[...] = (ac * _normal_cdf(ac) * u).astype(BF16)

    return _grid_call(
        body, "ffn_up", D_FF // FF_CHUNK,
        in_specs=[_full(hb.shape), wblk, wblk, cblk(3), cblk(1)],
        out_specs=[oblk, oblk, oblk],
        out_shape=[_sds((seq, D_FF), F32), _sds((seq, D_FF), F32), _sds((seq, D_FF), BF16)],
        operands=(hb, wgt, wut, conv_w, conv_b), tasks=tasks)


def _ffn_down(act, wd, h, target, ln_g, ln_b):
    seq = h.shape[0]

    def body(act_ref, wd_ref, h_ref, t_ref, lg_ref, lb_ref, dr_ref, drb_ref, loss_ref, dg_ref, db_ref):
        i = pl.program_id(0)

        @pl.when(i == 0)
        def _():
            loss_ref[...] = jnp.zeros_like(loss_ref)
            dg_ref[...] = jnp.zeros_like(dg_ref)
            db_ref[...] = jnp.zeros_like(db_ref)

        r2 = ALPHA * h_ref[...] + _dot(act_ref[...], wd_ref[...])
        y, xhat, rstd = _ln_fwd(r2, lg_ref[...], lb_ref[...])
        diff = y - t_ref[...]
        loss_ref[...] += jnp.sum(diff * diff) * (0.5 / D_MODEL)
        dy = diff * (1.0 / D_MODEL)
        dg_ref[...] += jnp.sum(dy * xhat, axis=0, keepdims=True)
        db_ref[...] += jnp.sum(dy, axis=0, keepdims=True)
        dr = _ln_bwd(dy, xhat, rstd, lg_ref[...])
        dr_ref[...] = dr
        drb_ref[...] = dr.astype(BF16)

    vec = pl.BlockSpec((1, D_MODEL), lambda i: (0, 0))
    return pl.pallas_call(
        body, name="ffn_down", grid=(seq // TOK,),
        in_specs=[_rows(D_FF), _full(wd.shape), _rows(D_MODEL), _rows(D_MODEL), _full((1, D_MODEL)), _full((1, D_MODEL))],
        out_specs=[_rows(D_MODEL), _rows(D_MODEL), pl.BlockSpec((8, LANE), lambda i: (0, 0)), vec, vec],
        out_shape=[_sds((seq, D_MODEL), F32), _sds((seq, D_MODEL), BF16), _sds((8, LANE), F32),
                   _sds((1, D_MODEL), F32), _sds((1, D_MODEL), F32)],
        compiler_params=_params("arbitrary"),
    )(act, wd, h, target, ln_g, ln_b)


def _ffn_bwd(drb, hb, a, u, wd, conv_w, conv_b):
    seq = hb.shape[0]
    wblk = pl.BlockSpec((FF_CHUNK, D_MODEL), lambda c: (c, 0))
    cblk = lambda rows: pl.BlockSpec((rows, FF_CHUNK), lambda c: (0, c))
    sblk = pl.BlockSpec((seq, FF_CHUNK), lambda c: (0, c))

    def body(dr_ref, h_ref, a_ref, u_ref, wd_ref, cw_ref, cb_ref, da_ref, du_ref, dwd_ref, dwg_ref, dwu_ref, dc_ref):
        dr = dr_ref[...]
        h = h_ref[...]
        a = a_ref[...]
        u = u_ref[...]
        cw = cw_ref[...]
        dact = _dot_nt(dr, wd_ref[...])
        ac = _conv(a, cw, cb_ref[...])
        cdf = _normal_cdf(ac)
        gelu = ac * cdf
        dwd_ref[...] = _dot_tn((gelu * u).astype(BF16), dr).astype(BF16)
        du = (dact * gelu).astype(BF16)
        dac = dact * u * _gelu_derivative(ac, cdf)
        da = (cw[2:3, :] * dac + cw[1:2, :] * _shift_up(dac, 1) + cw[0:1, :] * _shift_up(dac, 2)).astype(BF16)
        da_ref[...] = da
        du_ref[...] = du
        dwg_ref[...] = _dot_tn(da, h).astype(BF16)
        dwu_ref[...] = _dot_tn(du, h).astype(BF16)
        dc_ref[0:1, :] = jnp.sum(dac * _shift_down(a, 2), axis=0, keepdims=True)
        dc_ref[1:2, :] = jnp.sum(dac * _shift_down(a, 1), axis=0, keepdims=True)
        dc_ref[2:3, :] = jnp.sum(dac * a, axis=0, keepdims=True)
        dc_ref[3:4, :] = jnp.sum(dac, axis=0, keepdims=True)
        dc_ref[4:8, :] = jnp.zeros((4, FF_CHUNK), F32)

    return pl.pallas_call(
        body, name="ffn_bwd", grid=(D_FF // FF_CHUNK,),
        in_specs=[_full(drb.shape), _full(hb.shape), sblk, sblk, wblk, cblk(3), cblk(1)],
        out_specs=[sblk, sblk, wblk, wblk, wblk, cblk(8)],
        out_shape=[_sds((seq, D_FF), BF16), _sds((seq, D_FF), BF16), _sds((D_FF, D_MODEL), BF16),
                   _sds((D_FF, D_MODEL), BF16), _sds((D_FF, D_MODEL), BF16), _sds((8, D_FF), F32)],
        compiler_params=_params("parallel"),
    )(drb, hb, a, u, wd, conv_w, conv_b)


def _ln1_bwd(dr2, da, du, wgt, wut, xhat, rstd, ln_g, tasks=()):
    seq = dr2.shape[0]

    def body(dr2_ref, da_ref, du_ref, wg_ref, wu_ref, xhat_ref, rstd_ref, lg_ref, dr_ref, drb_ref, dg_ref, db_ref):
        @pl.when(pl.program_id(0) == 0)
        def _():
            dg_ref[...] = jnp.zeros_like(dg_ref)
            db_ref[...] = jnp.zeros_like(db_ref)

        dh = ALPHA * dr2_ref[...] + _dot(da_ref[...], wg_ref[...]) + _dot(du_ref[...], wu_ref[...])
        xhat = xhat_ref[...]
        dg_ref[...] += jnp.sum(dh * xhat, axis=0, keepdims=True)
        db_ref[...] += jnp.sum(dh, axis=0, keepdims=True)
        dr = _ln_bwd(dh, xhat, rstd_ref[:, 0:1], lg_ref[...])
        dr_ref[...] = dr
        drb_ref[...] = dr.astype(BF16)

    vec = pl.BlockSpec((1, D_MODEL), lambda i: (0, 0))
    return _grid_call(
        body, "ln1_bwd", seq // TOK,
        in_specs=[_rows(D_MODEL), _rows(D_FF), _rows(D_FF), _full(wgt.shape), _full(wut.shape), _rows(D_MODEL),
                  _rows(LANE), _full((1, D_MODEL))],
        out_specs=[_rows(D_MODEL), _rows(D_MODEL), vec, vec],
        out_shape=[_sds((seq, D_MODEL), F32), _sds((seq, D_MODEL), BF16), _sds((1, D_MODEL), F32),
                   _sds((1, D_MODEL), F32)],
        operands=(dr2, da, du, wgt, wut, xhat, rstd, ln_g), tasks=tasks)


def _mix_bwd(drb, ya, yp, g, mixed, wout, wba, wbp, w_pool, pool_scale, tasks=()):
    seq = drb.shape[0]

    def body(dr_ref, ya_ref, yp_ref, g_ref, mixed_ref, wout_ref, wba_ref, wbp_ref, wp_ref, ps_ref,
             dzg_ref, dya_ref, dyp_ref, do_ref, dmixed_ref, dpooled_ref, dbg_ref, dps_ref):
        @pl.when(pl.program_id(0) == 0)
        def _():
            dbg_ref[...] = jnp.zeros_like(dbg_ref)
            dps_ref[...] = jnp.zeros_like(dps_ref)

        dmerged = _dot_nt(dr_ref[...], wout_ref[...])
        ga, gp = g_ref[:, :D_MODEL], g_ref[:, D_MODEL:]
        dzga = dmerged * ya_ref[...].astype(F32) * ga * (1.0 - ga)
        dzgp = dmerged * yp_ref[...].astype(F32) * gp * (1.0 - gp)
        dzg_ref[:, :D_MODEL] = dzga.astype(BF16)
        dzg_ref[:, D_MODEL:] = dzgp.astype(BF16)
        dbg_ref[:, :D_MODEL] += jnp.sum(dzga, axis=0, keepdims=True)
        dbg_ref[:, D_MODEL:] += jnp.sum(dzgp, axis=0, keepdims=True)
        dya = (dmerged * ga).astype(BF16)
        dyp = (dmerged * gp).astype(BF16)
        dya_ref[...] = dya
        dyp_ref[...] = dyp
        do_ref[...] = _dot_nt(dya, wba_ref[...])
        dypre = _dot_nt(dyp, wbp_ref[...])
        dps_ref[...] += jnp.sum(dypre * mixed_ref[...], axis=0, keepdims=True)
        dmixed = (dypre * ps_ref[...]).astype(BF16)
        dmixed_ref[...] = dmixed
        for grp in range(len(POOL_WINDOWS)):
            cols = slice(POOL_GROUP * grp, POOL_GROUP * (grp + 1))
            dpooled_ref[:, cols] = _dot_nt(dmixed[:, cols], wp_ref[grp].astype(BF16))

    return _grid_call(
        body, "mix_bwd", seq // TOK,
        in_specs=[_rows(D_MODEL), _rows(D_MODEL), _rows(D_MODEL), _rows(2 * D_MODEL), _rows(D_POOL),
                  _full(wout.shape), _full(wba.shape), _full(wbp.shape), _full(w_pool.shape), _full((1, D_POOL))],
        out_specs=[_rows(2 * D_MODEL), _rows(D_MODEL), _rows(D_MODEL), _rows(D_ATTN), _rows(D_POOL), _rows(D_POOL),
                   pl.BlockSpec((1, 2 * D_MODEL), lambda i: (0, 0)), pl.BlockSpec((1, D_POOL), lambda i: (0, 0))],
        out_shape=[_sds((seq, 2 * D_MODEL), BF16), _sds((seq, D_MODEL), BF16), _sds((seq, D_MODEL), BF16),
                   _sds((seq, D_ATTN), F32), _sds((seq, D_POOL), BF16), _sds((seq, D_POOL), F32),
                   _sds((1, 2 * D_MODEL), F32), _sds((1, D_POOL), F32)],
        operands=(drb, ya, yp, g, mixed, wout, wba, wbp, w_pool, pool_scale), tasks=tasks)


def _attn_bwd(q, k, v, bias, o, lse, do, cos, sin, tasks=()):
    seq = q.shape[0]
    nb = seq // MOBA_BLOCK
    pair = pl.BlockSpec((seq, LANE), lambda p: (0, p))
    table = pl.BlockSpec((seq, LANE), lambda p: (0, 0))
    scale = HEAD_DIM ** -0.5

    def body(q_ref, k_ref, v_ref, bias_ref, o_ref, lse_ref, do_ref, cos_ref, sin_ref, dq_ref, dk_ref, dv_ref,
             dq_acc, dk_acc, dv_acc, dk_head, dv_head, ka_sc, qa_sc, s_sc, dp_sc, p_sc, ds_sc):
        for hh in range(LANE // HEAD_DIM):
            ls = slice(HEAD_DIM * hh, HEAD_DIM * (hh + 1))
            _store_keys(ka_sc, k_ref, ls)
            vb = v_ref[:, ls]
            dk_head[...] = jnp.zeros_like(dk_head)
            dv_head[...] = jnp.zeros_like(dv_head)
            for i in range(nb):
                rs = slice(MOBA_BLOCK * i, MOBA_BLOCK * (i + 1))
                width = MOBA_BLOCK * (i + 1)
                qa_sc[:, 0:HEAD_DIM] = (q_ref[rs, ls] * scale).astype(BF16)
                qa_sc[:, HEAD_DIM:] = bias_ref[rs, ls]
                s_sc[0:width, :] = _dot_nt(ka_sc[0:width, :], qa_sc[...])
                s_sc[rs, :] = jnp.where(_causal((MOBA_BLOCK, MOBA_BLOCK), transposed=True), s_sc[rs, :], NEG)
                dob = do_ref[rs, ls]
                delta = _row_vector(jnp.sum(dob * o_ref[rs, ls], axis=1, keepdims=True))
                lse_row = lse_ref[0, hh:hh + 1, rs]
                dob16 = dob.astype(BF16)
                dp_sc[0:width, :] = _dot_nt(vb[0:width], dob16)
                for c in range(width // SCORE_CHUNK):
                    rows = slice(SCORE_CHUNK * c, SCORE_CHUNK * (c + 1))
                    p = jnp.exp(s_sc[rows, :] - lse_row)
                    p_sc[rows, :] = p.astype(BF16)
                    ds_sc[rows, :] = (p * (dp_sc[rows, :] - delta)).astype(BF16)
                dv_head[0:width, :] += _dot(p_sc[0:width, :], dob16)
                dk_head[0:width, :] += _dot(ds_sc[0:width, :], qa_sc[:, 0:HEAD_DIM])
                dq_acc[rs, ls] = _dot_tn(ds_sc[0:width, :], ka_sc[0:width, 0:HEAD_DIM]) * scale
            dk_acc[:, ls] = dk_head[...]
            dv_acc[:, ls] = dv_head[...]
        cos_t, sin_t = cos_ref[...], sin_ref[...]
        dq_ref[...] = _rope_transposed(dq_acc[...], cos_t, sin_t).astype(BF16)
        dk_ref[...] = _rope_transposed(dk_acc[...], cos_t, sin_t).astype(BF16)
        dv_ref[...] = dv_acc[...].astype(BF16)

    return _grid_call(
        body, "attn_bwd", D_ATTN // LANE,
        in_specs=[pair, pair, pair, pair, pair, pl.BlockSpec((1, 8, seq), lambda p: (p, 0, 0)), pair, table, table],
        out_specs=[pair, pair, pair], out_shape=[_sds((seq, D_ATTN), BF16)] * 3,
        operands=(q, k, v, bias, o, lse, do, cos, sin),
        scratch=[pltpu.VMEM((seq, LANE), F32)] * 3 + [pltpu.VMEM((seq, HEAD_DIM), F32)] * 2
        + [pltpu.VMEM((seq, LANE), BF16), pltpu.VMEM((MOBA_BLOCK, LANE), BF16)]
        + [pltpu.VMEM((seq, MOBA_BLOCK), F32)] * 2 + [pltpu.VMEM((seq, MOBA_BLOCK), BF16)] * 2,
        tasks=tasks)


def _in_bwd(dq, dk, dv, dpooled, dzg, dr1, win, tasks=()):
    seq = dr1.shape[0]
    nt = seq // TOK

    def body(dq_ref, dk_ref, dv_ref, dp_ref, dpnext_ref, dzg_ref, dr_ref, win_ref, dx_ref, dz_ref, ext):
        i = pl.program_id(0)
        dp = dp_ref[...]
        dpn = jnp.where(i < nt - 1, dpnext_ref[...], 0.0)
        for grp, window in enumerate(POOL_WINDOWS):
            cols = slice(POOL_GROUP * grp, POOL_GROUP * (grp + 1))
            ext[0:TOK, cols] = dp[:, cols] / _pool_count(i * TOK, TOK, window)
            ext[TOK:, cols] = dpn[:, cols] / _pool_count((i + 1) * TOK, POOL_HALO, window)
        for grp, window in enumerate(POOL_WINDOWS):
            cols = slice(POOL_GROUP * grp, POOL_GROUP * (grp + 1))
            acc = ext[0:TOK, cols] - dp[:, cols]
            for kk in range(1, window):
                acc = acc + ext[pl.ds(kk, TOK), cols]
            dz_ref[:, 3 * D_ATTN + POOL_GROUP * grp:3 * D_ATTN + POOL_GROUP * (grp + 1)] = acc.astype(BF16)
        dz_ref[:, 0:D_ATTN] = dq_ref[...]
        dz_ref[:, D_ATTN:2 * D_ATTN] = dk_ref[...]
        dz_ref[:, 2 * D_ATTN:3 * D_ATTN] = dv_ref[...]
        dz_ref[:, 3 * D_ATTN + D_POOL:] = dzg_ref[...]
        dx = ALPHA * dr_ref[...]
        for n in range(N_DEV):
            dx = dx + _dot_nt(dz_ref[:, D_ATTN * n:D_ATTN * (n + 1)], win_ref[n])
        dx_ref[...] = dx

    halo = pl.BlockSpec((POOL_HALO, D_POOL),
                        lambda i: (jnp.minimum((i + 1) * (TOK // POOL_HALO), seq // POOL_HALO - 1), 0))
    return _grid_call(
        body, "in_bwd", nt,
        in_specs=[_rows(D_ATTN), _rows(D_ATTN), _rows(D_ATTN), _rows(D_POOL), halo, _rows(2 * D_MODEL),
                  _rows(D_MODEL), _full(win.shape)],
        out_specs=[_rows(D_MODEL), _rows(D_IN_PROJ)],
        out_shape=[_sds((seq, D_MODEL), F32), _sds((seq, D_IN_PROJ), BF16)],
        operands=(dq, dk, dv, dpooled, dpooled, dzg, dr1, win),
        scratch=[pltpu.VMEM((TOK + POOL_HALO, D_POOL), F32)], tasks=tasks)


def _dw_mixers(o, ypre, merged, dya, dyp, drb, pooled, dmixed, tasks=()):
    seq = o.shape[0]
    groups = len(POOL_WINDOWS)
    col = pl.BlockSpec((seq, LANE), lambda n: (0, n))
    grp = pl.BlockSpec((seq, POOL_GROUP), lambda n: (0, jnp.minimum(n, groups - 1)))
    owner = lambda rows, cols: pl.BlockSpec((1, rows, cols), lambda n: (n, 0, 0))

    def body(o_ref, ypre_ref, merged_ref, dya_ref, dyp_ref, dr_ref, pooled_ref, dmixed_ref,
             dba_ref, dbp_ref, dout_ref, dpool_ref, ob_sc):
        n = pl.program_id(0)

        @pl.when(n == 0)
        def _():
            ob_sc[...] = o_ref[...].astype(BF16)

        dba_ref[0] = _dot_tn(dya_ref[...], ob_sc[...]).T.astype(BF16)
        dbp_ref[0] = _dot_tn(dyp_ref[...], ypre_ref[...]).T.astype(BF16)
        dout_ref[0] = _dot_tn(merged_ref[...], dr_ref[...]).astype(BF16)

        @pl.when(n < groups)
        def _():
            dpool_ref[0] = _dot_tn(pooled_ref[...], dmixed_ref[...])

    return _grid_call(
        body, "dw_mixers", N_DEV,
        in_specs=[_full(o.shape), _full(ypre.shape), col, col, col, _full(drb.shape), grp, grp],
        out_specs=[owner(D_ATTN, LANE), owner(D_POOL, LANE), owner(D_MODEL // N_DEV, D_MODEL),
                   pl.BlockSpec((1, POOL_GROUP, POOL_GROUP), lambda n: (jnp.minimum(n, groups - 1), 0, 0))],
        out_shape=[_sds((N_DEV, D_ATTN, LANE), BF16), _sds((N_DEV, D_POOL, LANE), BF16),
                   _sds((N_DEV, D_MODEL // N_DEV, D_MODEL), BF16), _sds((groups, POOL_GROUP, POOL_GROUP), F32)],
        operands=(o, ypre, merged, dya, dyp, drb, pooled, dmixed),
        scratch=[pltpu.VMEM((seq, D_ATTN), BF16)], tasks=tasks)


def _to_bf16(arrays, tasks=()):
    n = len(arrays)

    def body(*refs):
        for src, dst in zip(refs[:n], refs[n:]):
            dst[...] = src[...].astype(BF16)

    return _grid_call(
        body, "to_bf16", 1, in_specs=[_full(a.shape) for a in arrays],
        out_specs=[pl.BlockSpec(a.shape, lambda i: (0, 0)) for a in arrays],
        out_shape=[_sds(a.shape, BF16) for a in arrays], operands=arrays, tasks=tasks)


def _matmul(name, a, b, out_shape, out_dtype, steps, a_spec, b_spec, o_spec, tasks=()):
    def body(a_ref, b_ref, o_ref):
        o_ref[...] = _dot(a_ref[...], b_ref[...]).reshape(o_ref.shape).astype(o_ref.dtype)

    (out,), results = _grid_call(body, name, steps, in_specs=[a_spec, b_spec], out_specs=[o_spec],
                                 out_shape=[_sds(out_shape, out_dtype)], operands=(a, b), tasks=tasks)
    return out, results


def _place():
    return lax.axis_index("x"), lax.axis_index("y"), lax.axis_index("c")


def _other_chips(x, y):
    return [(1 - x, y), (x, 1 - y), (1 - x, 1 - y)]


DMA_SEMS = pltpu.SemaphoreType.DMA


class _AllGather:
    def __init__(self, shards, lag=0):
        self.operands = list(shards)
        self.n = len(shards)
        self.lag = lag
        self.out_shape = [_sds((N_DEV, *s.shape), s.dtype) for s in shards]
        self.sems = [DMA_SEMS((7 * self.n,)), DMA_SEMS((7 * self.n,)), DMA_SEMS((self.n,))]

    def _copy(self, refs, a, k, block, to, from_input=False):
        ins, outs, (send_sems, recv_sems, _) = refs
        px, py, pc = block
        dst = outs[a].at[4 * px + 2 * py + pc]
        return pltpu.make_async_remote_copy(
            src_ref=ins[a] if from_input else dst, dst_ref=dst,
            send_sem=send_sems.at[7 * a + k], recv_sem=recv_sems.at[7 * a + k],
            device_id=to, device_id_type=MESH)

    def _local(self, refs, a):
        ins, outs, (_, _, local_sems) = refs
        x, y, c = _place()
        return pltpu.make_async_copy(ins[a], outs[a].at[4 * x + 2 * y + c], local_sems.at[a])

    def _pass_on(self, refs, a):
        x, y, c = _place()
        origin = ((x + 1 - c) % 2, (y + c) % 2, c)
        target = ((x + c) % 2, (y + 1 - c) % 2, c)
        return self._copy(refs, a, 3, origin, target)

    def start(self, refs):
        x, y, c = _place()
        for a in range(self.n):
            self._local(refs, a).start()
        for a in range(self.n):
            self._copy(refs, a, 0, (x, y, c), (x, y, 1 - c), True).start()
            for j, chip in enumerate(_other_chips(x, y)[:2]):
                self._copy(refs, a, 1 + j, (x, y, c), (*chip, c), True).start()

    def middle(self, refs):
        x, y, c = _place()
        me, sibling = (x, y, c), (x, y, 1 - c)
        chips = _other_chips(x, y)
        for a in range(self.n):
            for j in range(2):
                self._copy(refs, a, 1 + j, (*chips[j], c), me).wait_recv()
        for a in range(self.n):
            self._pass_on(refs, a).start()
            for j in range(2):
                self._copy(refs, a, 4 + j, (*chips[j], c), sibling).start()

    def late(self, refs):
        x, y, c = _place()
        diagonal = (1 - x, 1 - y, c)
        for a in range(self.n):
            self._copy(refs, a, 3, diagonal, (x, y, c)).wait_recv()
            self._copy(refs, a, 6, diagonal, (x, y, 1 - c)).start()

    def finish(self, refs):
        x, y, c = _place()
        me, sibling = (x, y, c), (x, y, 1 - c)
        chips = _other_chips(x, y)
        for a in range(self.n):
            self._copy(refs, a, 0, sibling, me).wait_recv()
            for j, chip in enumerate(chips):
                self._copy(refs, a, 4 + j, (*chip, 1 - c), me).wait_recv()
        for a in range(self.n):
            self._copy(refs, a, 0, me, sibling, True).wait_send()
            for j, chip in enumerate(chips[:2]):
                self._copy(refs, a, 1 + j, me, (*chip, c), True).wait_send()
            self._pass_on(refs, a).wait_send()
            for j, chip in enumerate(chips):
                self._copy(refs, a, 4 + j, (*chip, c), sibling).wait_send()
            self._local(refs, a).wait()


class _SiblingSend:
    def __init__(self, partials):
        self.operands = list(partials)
        self.n = len(partials)
        self.out_shape = [_sds((4, *p.shape[1:]), p.dtype) for p in partials]
        self.sems = [DMA_SEMS((4 * self.n,)), DMA_SEMS((4 * self.n,))]

    def _copy(self, refs, a, q):
        ins, outs, (send_sems, recv_sems) = refs
        x, y, c = _place()
        return pltpu.make_async_remote_copy(
            src_ref=ins[a].at[2 * q + 1 - c], dst_ref=outs[a].at[q],
            send_sem=send_sems.at[4 * a + q], recv_sem=recv_sems.at[4 * a + q],
            device_id=(x, y, 1 - c), device_id_type=MESH)

    def start(self, refs):
        for a in range(self.n):
            for q in range(4):
                self._copy(refs, a, q).start()

    def middle(self, refs):
        pass

    def finish(self, refs):
        for a in range(self.n):
            for q in range(4):
                self._copy(refs, a, q).wait()


class _ChipScatter:
    def __init__(self, chip_partials):
        self.operands = list(chip_partials)
        self.n = len(chip_partials)
        self.out_shape = [_sds(p.shape, p.dtype) for p in chip_partials]
        self.sems = [DMA_SEMS((3 * self.n,)), DMA_SEMS((3 * self.n,)), DMA_SEMS((self.n,))]

    def _copy(self, refs, a, k, arrival=False):
        ins, outs, (send_sems, recv_sems, _) = refs
        x, y, c = _place()
        px, py = _other_chips(x, y)[k]
        mine, theirs = 2 * x + y, 2 * px + py
        return pltpu.make_async_remote_copy(
            src_ref=ins[a].at[mine if arrival else theirs], dst_ref=outs[a].at[theirs if arrival else mine],
            send_sem=send_sems.at[3 * a + k], recv_sem=recv_sems.at[3 * a + k],
            device_id=(px, py, c), device_id_type=MESH)

    def _local(self, refs, a):
        ins, outs, (_, _, local_sems) = refs
        x, y, _ = _place()
        return pltpu.make_async_copy(ins[a].at[2 * x + y], outs[a].at[2 * x + y], local_sems.at[a])

    def start(self, refs):
        for a in range(self.n):
            self._local(refs, a).start()
            for k in range(3):
                self._copy(refs, a, k).start()

    def middle(self, refs):
        pass

    def finish(self, refs):
        for a in range(self.n):
            for k in range(3):
                self._copy(refs, a, k, arrival=True).wait_recv()
        for a in range(self.n):
            for k in range(3):
                self._copy(refs, a, k).wait_send()
            self._local(refs, a).wait()


class _DirectScatter:
    def __init__(self, partials):
        self.operands = list(partials)
        self.n = len(partials)
        self.out_shape = [_sds(p.shape, p.dtype) for p in partials]
        self.sems = [DMA_SEMS((7 * self.n,)), DMA_SEMS((7 * self.n,)), DMA_SEMS((self.n,))]

    def _copy(self, refs, a, k, arrival=False):
        ins, outs, (send_sems, recv_sems, _) = refs
        x, y, c = _place()
        peer = [(x, y, 1 - c), (1 - x, y, c), (x, 1 - y, c), (1 - x, 1 - y, c),
                (1 - x, y, 1 - c), (x, 1 - y, 1 - c), (1 - x, 1 - y, 1 - c)][k]
        mine, theirs = 4 * x + 2 * y + c, 4 * peer[0] + 2 * peer[1] + peer[2]
        return pltpu.make_async_remote_copy(
            src_ref=ins[a].at[mine if arrival else theirs], dst_ref=outs[a].at[theirs if arrival else mine],
            send_sem=send_sems.at[7 * a + k], recv_sem=recv_sems.at[7 * a + k],
            device_id=peer, device_id_type=MESH)

    def _local(self, refs, a):
        ins, outs, (_, _, local_sems) = refs
        x, y, c = _place()
        return pltpu.make_async_copy(ins[a].at[4 * x + 2 * y + c], outs[a].at[4 * x + 2 * y + c], local_sems.at[a])

    def start(self, refs):
        for a in range(self.n):
            self._local(refs, a).start()
            for k in range(7):
                self._copy(refs, a, k).start()

    def middle(self, refs):
        pass

    def finish(self, refs):
        for a in range(self.n):
            for k in range(7):
                self._copy(refs, a, k, arrival=True).wait_recv()
        for a in range(self.n):
            for k in range(7):
                self._copy(refs, a, k).wait_send()
            self._local(refs, a).wait()


def _task_args(tasks):
    hbm = pl.BlockSpec(memory_space=pl.ANY)
    operands = [o for t in tasks for o in t.operands]
    out_shape = [s for t in tasks for s in t.out_shape]
    sems = [s for t in tasks for s in t.sems]
    return operands, [hbm] * len(operands), out_shape, [hbm] * len(out_shape), sems


def _task_refs(tasks, ins, outs, sems):
    per_task = []
    for t in tasks:
        ni, no, ns = len(t.operands), len(t.out_shape), len(t.sems)
        per_task.append((ins[:ni], outs[:no], sems[:ns]))
        ins, outs, sems = ins[ni:], outs[no:], sems[ns:]
    return per_task


def _task_results(tasks, outs):
    res = []
    for t in tasks:
        res.append(list(outs[:len(t.out_shape)]))
        outs = outs[len(t.out_shape):]
    return res


def _carry(body, tasks, n_in, n_out, n_scratch, steps):
    if not tasks:
        return body
    t_in = sum(len(t.operands) for t in tasks)
    t_out = sum(len(t.out_shape) for t in tasks)

    def wrapped(*refs):
        ins, refs = refs[:n_in], refs[n_in:]
        t_ins, refs = refs[:t_in], refs[t_in:]
        outs, refs = refs[:n_out], refs[n_out:]
        t_outs, refs = refs[:t_out], refs[t_out:]
        scratch, t_sems = refs[:n_scratch], refs[n_scratch:]
        per_task = _task_refs(tasks, t_ins, t_outs, t_sems)
        step = pl.program_id(0)

        @pl.when(step == 0)
        def _():
            for t, r in zip(tasks, per_task):
                t.start(r)

        for t, r in zip(tasks, per_task):
            pl.when(step == max(steps - 1 - getattr(t, "lag", 0), 0))(functools.partial(t.middle, r))
            if hasattr(t, "late"):
                pl.when(step == steps - 1)(functools.partial(t.late, r))

        body(*ins, *outs, *scratch)

        @pl.when(step == steps - 1)
        def _():
            for t, r in zip(tasks, per_task):
                t.finish(r)

    return wrapped


def _exchange(name, tasks):
    operands, in_specs, out_shape, out_specs, sems = _task_args(tasks)

    def body(*refs):
        ni, no = len(operands), len(out_shape)
        per_task = _task_refs(tasks, refs[:ni], refs[ni:ni + no], refs[ni + no:])
        for phase in ("start", "middle", "late", "finish"):
            for t, r in zip(tasks, per_task):
                if hasattr(t, phase):
                    getattr(t, phase)(r)

    outs = pl.pallas_call(body, name=name, in_specs=in_specs, out_specs=out_specs, out_shape=out_shape,
                          scratch_shapes=sems)(*operands)
    return _task_results(tasks, outs)


def _row_tile(rows, cols, whole_up_to=256 * 1024):
    if rows * cols <= whole_up_to:
        return rows
    for t in (256, 176, 128, 64, 32, 16, 8):
        if rows % t == 0:
            return t
    return rows


def _pair_sum(name, partials, from_sibling):
    n = len(partials)

    def body(core_ref, *refs):
        for p_ref, s_ref, o_ref in zip(refs[:n], refs[n:2 * n], refs[2 * n:]):
            o_ref[0] = (p_ref[0, 0].astype(F32) + s_ref[0].astype(F32)).astype(o_ref.dtype)

    blk = [pl.BlockSpec((1, *s.shape[1:]), lambda q, core: (q, 0, 0)) for s in from_sibling]
    mine = [pl.BlockSpec((1, 1, *p.shape[1:]), lambda q, core: (q, core[0], 0, 0)) for p in partials]
    return pl.pallas_call(
        body, name=name,
        grid_spec=pltpu.PrefetchScalarGridSpec(num_scalar_prefetch=1, grid=(4,), in_specs=mine + blk, out_specs=blk),
        out_shape=[_sds(s.shape, s.dtype) for s in from_sibling],
        compiler_params=_params("parallel"),
    )(lax.axis_index("c").astype(jnp.int32).reshape(1), *[p.reshape(4, 2, *p.shape[1:]) for p in partials],
      *from_sibling)


def _adamw_math(w, g, m, v):
    nm = ADAM_B1 * m + (1.0 - ADAM_B1) * g
    nv = ADAM_B2 * v + (1.0 - ADAM_B2) * (g * g)
    m_hat = nm / (1.0 - ADAM_B1 ** ADAM_STEP)
    v_hat = nv / (1.0 - ADAM_B2 ** ADAM_STEP)
    return -ADAM_LR * (m_hat / (jnp.sqrt(v_hat) + ADAM_EPS) + ADAM_WD * w), nm, nv


def _update_small(params, loss_parts):
    n = len(params)

    def whole(shape):
        return pl.BlockSpec(shape, lambda i, rank=len(shape): (0,) * rank)

    def total(ref):
        acc = ref[0]
        for d in range(1, N_DEV):
            acc = acc + ref[d]
        return acc

    def body(*refs):
        ins, loss_ref, outs, loss_out = refs[:4 * n], refs[4 * n], refs[4 * n + 1:8 * n + 1], refs[8 * n + 1]
        for p in range(n):
            s_ref, w_ref, m_ref, v_ref = ins[4 * p:4 * p + 4]
            g_ref, d_ref, nm_ref, nv_ref = outs[4 * p:4 * p + 4]
            g = total(s_ref)
            g_ref[...] = g
            d_ref[...], nm_ref[...], nv_ref[...] = _adamw_math(w_ref[...], g, m_ref[...], v_ref[...])
        loss_out[...] = total(loss_ref)

    outs = pl.pallas_call(
        body, name="update_small", grid=(1,),
        in_specs=[whole(t.shape) for p in params for t in p] + [whole(loss_parts.shape)],
        out_specs=[whole(p[1].shape) for p in params for _ in range(4)] + [whole(loss_parts.shape[1:])],
        out_shape=[_sds(p[1].shape, F32) for p in params for _ in range(4)] + [_sds(loss_parts.shape[1:], F32)],
        compiler_params=_params("arbitrary"),
    )(*[t for p in params for t in p], loss_parts)
    return [outs[4 * p:4 * p + 4] for p in range(n)], outs[4 * n]


def _sum_adamw(name, params, tasks=()):
    n = len(params)
    parts = params[0][0].shape[0]
    shapes = [p[1].shape for p in params]
    tiles = [_row_tile(*shapes[0])] * n if len(set(shapes)) == 1 else [rows for rows, _ in shapes]
    steps = shapes[0][0] // tiles[0]

    def body(*refs):
        ins, outs = refs[:4 * n], refs[4 * n:]
        for p in range(n):
            s_ref, w_ref, m_ref, v_ref = ins[4 * p:4 * p + 4]
            g_ref, d_ref, nm_ref, nv_ref = outs[4 * p:4 * p + 4]
            g = s_ref[0].astype(F32)
            for d in range(1, parts):
                g = g + s_ref[d].astype(F32)
            g_ref[...] = g
            d_ref[...], nm_ref[...], nv_ref[...] = _adamw_math(w_ref[...], g, m_ref[...], v_ref[...])

    blk = [pl.BlockSpec((t, cols), lambda i: (i, 0)) for t, (_, cols) in zip(tiles, shapes)]
    stacked = [pl.BlockSpec((parts, t, cols), lambda i: (0, i, 0)) for t, (_, cols) in zip(tiles, shapes)]
    outs, results = _grid_call(
        body, name, steps,
        in_specs=[spec for p in range(n) for spec in (stacked[p], blk[p], blk[p], blk[p])],
        out_specs=[blk[p] for p in range(n) for _ in range(4)],
        out_shape=[_sds(shapes[p], F32) for p in range(n) for _ in range(4)],
        operands=[t for p in params for t in p], tasks=tasks)
    return [outs[4 * p:4 * p + 4] for p in range(n)], results


SMALL = ("b_gate", "w_pool", "pool_scale", "ln1_g", "ln1_b", "conv_b", "ln2_g", "ln2_b")
MIXER = ("w_branch_attn", "w_branch_pool", "w_out", "conv_w")
FFN = ("w_ffn_gate_t", "w_ffn_up_t", "w_ffn_down")


def _columns(t):
    return jnp.transpose(t, (1, 0, 2)).reshape(t.shape[1], N_DEV * t.shape[2])


def _row_blocks(t):
    return t.reshape(N_DEV * t.shape[1], t.shape[2])


def _by_owner(t):
    return t.reshape(N_DEV, t.shape[0] // N_DEV, t.shape[1])


def _reduce_halves(names, partials, from_sibling):
    return _pair_sum("pair_sum_" + names[0], partials, from_sibling)


def _local_step(x, target, shards, small):
    seq = x.shape[0]
    cos, sin = _rope_tables(seq)
    cast, ((w_in_all,),) = _to_bf16([shards[n] for n in MIXER[:3] + FFN], tasks=[_AllGather([shards["w_in"]])])
    shards = {**shards, **dict(zip(MIXER[:3] + FFN, cast))}
    (xt, q, k, v, u, g, kmean), (mixer,) = _proj_in(
        x, w_in_all, small["b_gate"], cos, sin, tasks=[_AllGather([shards[n] for n in MIXER], lag=2)])
    wba, wbp, wout, conv_w = _columns(mixer[0]), _columns(mixer[1]), _row_blocks(mixer[2]), _columns(mixer[3])
    (o, lse, bias), ((wgt, wut),) = _attn_fwd(
        q, k, v, kmean.reshape(seq // MOBA_BLOCK, D_ATTN),
        tasks=[_AllGather([shards["w_ffn_gate_t"], shards["w_ffn_up_t"]], lag=1)])
    (ya, yp, pooled, mixed, ypre, merged, xhat1, rstd1, h1, h1b), _ = _mix(
        o, u, g, x, wba, wbp, wout, small["w_pool"], small["pool_scale"], small["ln1_g"], small["ln1_b"])
    wgt, wut = _row_blocks(wgt), _row_blocks(wut)
    (a, uf, act), ((wd,),) = _ffn_up(
        h1b, wgt, wut, conv_w, small["conv_b"], tasks=[_AllGather([shards["w_ffn_down"]], lag=4)])
    wd = _row_blocks(wd)
    dr2, dr2b, loss, dg2, db2 = _ffn_down(act, wd, h1, target, small["ln2_g"], small["ln2_b"])

    da, du, dwd, dwg, dwu, dconv = _ffn_bwd(dr2b, h1b, a, uf, wd, conv_w, small["conv_b"])
    ffn_partials = [_by_owner(dwg), _by_owner(dwu), _by_owner(dwd)]
    (dr1, dr1b, dg1, db1), (ffn_sibling,) = _ln1_bwd(
        dr2, da, du, wgt, wut, xhat1, rstd1, small["ln1_g"], tasks=[_SiblingSend(ffn_partials)])
    ffn_chip = _reduce_halves(FFN, ffn_partials, ffn_sibling)
    (dzg, dya, dyp, do, dmixed, dpooled, dbg, dps), (gate_landed,) = _mix_bwd(
        dr1b, ya, yp, g, mixed, wout, wba, wbp, small["w_pool"], small["pool_scale"],
        tasks=[_ChipScatter(ffn_chip[0:1])])
    (dw_ba, dw_bp, dw_out, dw_pool), _ = _dw_mixers(o, ypre, merged, dya, dyp, dr1b, pooled, dmixed)
    mixer_partials = [dw_ba, dw_bp, dw_out]
    (dq, dk, dv), (up_down_landed, mixer_sibling) = _attn_bwd(
        q, k, v, bias, o, lse, do, cos, sin, tasks=[_ChipScatter(ffn_chip[1:3]), _SiblingSend(mixer_partials)])
    mixer_chip = _reduce_halves(MIXER[:3], mixer_partials, mixer_sibling)
    (grad_x, dz), _ = _in_bwd(dq, dk, dv, dpooled, dzg, dr1, w_in_all)
    little = [dbg, dw_pool, dps, dg1, db1, dconv[3:4], dg2, db2]
    conv_w_partials = dconv[0:3].reshape(3, N_DEV, FF_SHARD).transpose(1, 0, 2)
    dw_in, (mixer_landed,) = _matmul(
        "dw_in", xt, dz, (N_DEV, D_MODEL, D_ATTN), BF16, N_DEV,
        _full(xt.shape), pl.BlockSpec((seq, D_ATTN), lambda n: (0, n)),
        pl.BlockSpec((1, D_MODEL, D_ATTN), lambda n: (n, 0, 0)), tasks=[_ChipScatter(mixer_chip)])

    landed = dict(zip(FFN + MIXER[:3], gate_landed + up_down_landed + mixer_landed))
    return grad_x, landed, dw_in, little, conv_w_partials, loss


def kernel(x, w_in, b_gate, w_branch_attn, w_pool, pool_scale, w_branch_pool, w_out, ln1_g, ln1_b, w_ffn_gate, w_ffn_up, conv_w, conv_b, w_ffn_down, ln2_g, ln2_b, loss_target, m_w_in, m_b_gate, m_w_branch_attn, m_w_pool, m_pool_scale, m_w_branch_pool, m_w_out, m_ln1_g, m_ln1_b, m_w_ffn_gate, m_w_ffn_up, m_conv_w, m_conv_b, m_w_ffn_down, m_ln2_g, m_ln2_b, v_w_in, v_b_gate, v_w_branch_attn, v_w_pool, v_pool_scale, v_w_branch_pool, v_w_out, v_ln1_g, v_ln1_b, v_w_ffn_gate, v_w_ffn_up, v_conv_w, v_conv_b, v_w_ffn_down, v_ln2_g, v_ln2_b):
    weights = dict(w_in=w_in, b_gate=b_gate, w_branch_attn=w_branch_attn, w_pool=w_pool, pool_scale=pool_scale,
                   w_branch_pool=w_branch_pool, w_out=w_out, ln1_g=ln1_g, ln1_b=ln1_b, w_ffn_gate=w_ffn_gate,
                   w_ffn_up=w_ffn_up, conv_w=conv_w, conv_b=conv_b, w_ffn_down=w_ffn_down, ln2_g=ln2_g, ln2_b=ln2_b)
    m_in = dict(w_in=m_w_in, b_gate=m_b_gate, w_branch_attn=m_w_branch_attn, w_pool=m_w_pool,
                pool_scale=m_pool_scale, w_branch_pool=m_w_branch_pool, w_out=m_w_out, ln1_g=m_ln1_g, ln1_b=m_ln1_b,
                w_ffn_gate=m_w_ffn_gate, w_ffn_up=m_w_ffn_up, conv_w=m_conv_w, conv_b=m_conv_b,
                w_ffn_down=m_w_ffn_down, ln2_g=m_ln2_g, ln2_b=m_ln2_b)
    v_in = dict(w_in=v_w_in, b_gate=v_b_gate, w_branch_attn=v_w_branch_attn, w_pool=v_w_pool,
                pool_scale=v_pool_scale, w_branch_pool=v_w_branch_pool, w_out=v_w_out, ln1_g=v_ln1_g, ln1_b=v_ln1_b,
                w_ffn_gate=v_w_ffn_gate, w_ffn_up=v_w_ffn_up, conv_w=v_conv_w, conv_b=v_conv_b,
                w_ffn_down=v_w_ffn_down, ln2_g=v_ln2_g, ln2_b=v_ln2_b)
    weights = {n: a[0] for n, a in weights.items()}
    m_in = {n: a[0] for n, a in m_in.items()}
    v_in = {n: a[0] for n, a in v_in.items()}

    shards = {"w_in": weights["w_in"].astype(BF16), "w_branch_attn": weights["w_branch_attn"],
              "w_branch_pool": weights["w_branch_pool"], "w_out": weights["w_out"],
              "w_ffn_gate_t": weights["w_ffn_gate"].T, "w_ffn_up_t": weights["w_ffn_up"].T,
              "w_ffn_down": weights["w_ffn_down"], "conv_w": weights["conv_w"]}
    small = {"b_gate": weights["b_gate"][None], "w_pool": weights["w_pool"], "pool_scale": weights["pool_scale"][None],
             "ln1_g": weights["ln1_g"][None], "ln1_b": weights["ln1_b"][None], "conv_b": weights["conv_b"][None],
             "ln2_g": weights["ln2_g"][None], "ln2_b": weights["ln2_b"][None]}

    grad_x, landed, dw_in, little, conv_w_partials, loss_part = _local_step(x[0], loss_target[0], shards, small)

    grads, delta, new_m, new_v = {}, {}, {}, {}

    def param(n, transposed=False):
        if transposed:
            return landed[n + "_t"], weights[n].T, m_in[n].T, v_in[n].T
        return landed[n], weights[n], m_in[n], v_in[n]

    def keep(n, updated, transposed=False):
        grads[n], delta[n], new_m[n], new_v[n] = (t.T for t in updated) if transposed else updated

    ((w_in_sibling,),) = _exchange("sibling_grads", [_SiblingSend([dw_in])])
    w_in_chip = _reduce_halves(["w_in"], [dw_in], [w_in_sibling])
    (landed["w_in"],), (*small_all, loss_all), (conv_w_all,) = _exchange(
        "scatter_grads",
        [_ChipScatter(w_in_chip), _AllGather(little + [loss_part]), _DirectScatter([conv_w_partials])])
    (gate, up, down), _ = _sum_adamw(
        "update_w_ffn", [param("w_ffn_gate", True), param("w_ffn_up", True), param("w_ffn_down")])
    keep("w_ffn_gate", gate, True)
    keep("w_ffn_up", up, True)
    keep("w_ffn_down", down)
    mixers = ("w_branch_attn", "w_branch_pool", "w_out")
    for n, updated in zip(mixers, _sum_adamw("update_mixers", [param(n) for n in mixers])[0]):
        keep(n, updated)
    keep("w_in", _sum_adamw("update_w_in", [param("w_in")])[0][0])
    names = SMALL + ("conv_w",)
    rows = lambda t: t if t.ndim > 1 else t[None]
    small_updates, loss = _update_small(
        [(s, rows(weights[n]), rows(m_in[n]), rows(v_in[n])) for n, s in zip(names, small_all + [conv_w_all])],
        loss_all)
    for n, updated in zip(names, small_updates):
        keep(n, [t.reshape(weights[n].shape) for t in updated])
    loss = loss[0, 0]

    order = ("w_in", "b_gate", "w_branch_attn", "w_pool", "pool_scale", "w_branch_pool", "w_out", "ln1_g", "ln1_b",
             "w_ffn_gate", "w_ffn_up", "conv_w", "conv_b", "w_ffn_down", "ln2_g", "ln2_b")
    lead = lambda t: t[None]
    return (loss, lead(grad_x), *[lead(grads[n]) for n in order], *[lead(delta[n]) for n in order],
            *[lead(new_m[n]) for n in order], *[lead(new_v[n]) for n in order])
```
